```python
import math
import jax, jax.numpy as jnp
from jax import lax
import numpy as np

D_MODEL = 1024
BATCH = 8
SEQ = 4096
DEPTH = 1

RWKV_WIDTH = D_MODEL // 2
RWKV_HEAD = 64
RWKV_HEADS = RWKV_WIDTH // RWKV_HEAD
DECAY_RANK = 64
AAA_RANK = 64
GATE_RANK = 128
RWKV_SPLITS = (RWKV_WIDTH, 2 * RWKV_WIDTH, 3 * RWKV_WIDTH,
               3 * RWKV_WIDTH + DECAY_RANK, 3 * RWKV_WIDTH + DECAY_RANK + AAA_RANK)
N_RWKV_COLS = 3 * RWKV_WIDTH + DECAY_RANK + AAA_RANK + GATE_RANK
LNX_EPS = 64e-5
S5_WIDTH = D_MODEL // 2
S5_GROUP = 16
S5_GROUPS = S5_WIDTH // S5_GROUP
S5_STATE = 64
STEP_MIN = 1e-3
STEP_MAX = 1e-1
N_IN_COLS = N_RWKV_COLS + S5_WIDTH + 2 * D_MODEL
D_FF = 256 * ((8 * D_MODEL // 3 + 255) // 256)
CONV_WIDTH = 3
NORM_EPS = 1e-6

kernel_name = 'hybrid_rwkv7_s5_gated_merge_convffn'


def rms_norm(x, g):
    xf = x.astype(jnp.float32)
    y = xf * lax.rsqrt(jnp.mean(xf * xf, axis=-1, keepdims=True) + NORM_EPS)
    return y.astype(x.dtype) * g


def token_shift(p, mu):
    prev = jnp.pad(p, ((0, 0), (1, 0), (0, 0)))[:, :-1]
    return p + (prev - p) * mu


def wkv7_scan(r, decay, k, v, a_vec, b_vec):
    bsz, _, nh, n = r.shape

    def step(S, inp):
        r_t, w_t, k_t, v_t, a_t, b_t = inp
        sa = jnp.einsum('bhij,bhj->bhi', S, a_t)
        S = (S * w_t[:, :, None, :] + sa[..., None] * b_t[:, :, None, :]
             + v_t[..., None] * k_t[:, :, None, :])
        return S, jnp.einsum('bhij,bhj->bhi', S, r_t)

    xs = tuple(jnp.moveaxis(t, 1, 0) for t in (r, decay, k, v, a_vec, b_vec))
    S0 = jnp.zeros((bsz, nh, n, n), jnp.float32)
    _, ys = lax.scan(step, S0, xs)
    return jnp.moveaxis(ys, 0, 1)


def rwkv7_branch(p, mu, w0, w2, a0, a2, g2, k_k, k_a, r_k, lnx_w, lnx_b):
    dtype = p.dtype
    f32 = jnp.float32
    bsz, L, _ = p.shape
    p = token_shift(p.astype(f32), mu.astype(f32))
    r, k, v, wd, ad, gd = jnp.split(p, RWKV_SPLITS, axis=-1)
    w = -jax.nn.softplus(-(w0 + jnp.tanh(wd) @ w2)) - 0.5
    decay = jnp.exp(-jnp.exp(w))
    a = jax.nn.sigmoid(a0 + ad @ a2)
    g = jax.nn.sigmoid(gd) @ g2
    hs = (bsz, L, RWKV_HEADS, RWKV_HEAD)
    kk = (k * k_k).reshape(hs)
    kk = kk / jnp.maximum(jnp.linalg.norm(kk, axis=-1, keepdims=True), 1e-12)
    k = k * (1.0 + (a - 1.0) * k_a)
    rh, kh, vh = r.reshape(hs), k.reshape(hs), v.reshape(hs)
    ah = a.reshape(hs)
    y = wkv7_scan(rh, decay.reshape(hs), kh, vh, -kk, kk * ah)
    mean = jnp.mean(y, axis=-1, keepdims=True)
    var = jnp.mean(jnp.square(y - mean), axis=-1, keepdims=True)
    y = ((y - mean) * lax.rsqrt(var + LNX_EPS)).reshape(bsz, L, RWKV_WIDTH) * lnx_w + lnx_b
    bonus = jnp.sum(rh * kh * r_k, axis=-1, keepdims=True) * vh
    y = (y + bonus.reshape(bsz, L, RWKV_WIDTH)) * g
    return y.astype(dtype)


def s5_combine(left, right):
    a_i, b_i = left
    a_j, b_j = right
    return a_j * a_i, a_j * b_i + b_j


def s5_branch(u, a_re, a_im, b_re, b_im, c_re, c_im, d, log_step, w_glu, b_glu):
    f32 = jnp.float32
    bsz, L, _ = u.shape
    lam = lax.complex(a_re.astype(f32), a_im.astype(f32))
    dt = jnp.exp(log_step.astype(f32))[:, None]
    a_bar = jnp.exp(lam * dt)
    b_bar = ((a_bar - 1.0) / lam)[..., None] * lax.complex(b_re.astype(f32), b_im.astype(f32))
    c = lax.complex(c_re.astype(f32), c_im.astype(f32))
    ug = u.astype(f32).reshape(bsz, L, S5_GROUPS, S5_GROUP)
    bu = jnp.einsum('gpc,blgc->blgp', b_bar, ug.astype(jnp.complex64))
    a_elems = jnp.broadcast_to(a_bar, (1, L, S5_GROUPS, S5_STATE))
    _, states = lax.associative_scan(s5_combine, (a_elems, bu), axis=1)
    y = jnp.real(jnp.einsum('gcp,blgp->blgc', c, states)) + d.astype(f32).reshape(S5_GROUPS, S5_GROUP) * ug
    y = jax.nn.gelu(y.reshape(bsz, L, S5_WIDTH)).astype(u.dtype)
    return y * jax.nn.sigmoid(y @ w_glu + b_glu)


def conv_ffn(h, w_up, conv_w, conv_b, w_down):
    L = h.shape[1]
    z = h @ w_up
    zp = jnp.pad(z, ((0, 0), (CONV_WIDTH - 1, 0), (0, 0)))
    z = conv_b + sum(conv_w[j] * zp[:, j:j + L] for j in range(CONV_WIDTH))
    gate, val = jnp.split(z, 2, axis=-1)
    return (jax.nn.gelu(gate) * val) @ w_down


def _fwd_setup_inputs(seed: int = 0) -> dict:
    key = jax.random.key(seed)
    ks = jax.random.split(key, 40)
    f32 = jnp.float32
    Ld, W, G, P, C, F = DEPTH, RWKV_WIDTH, S5_GROUPS, S5_STATE, S5_GROUP, D_FF

    def nrm(k, shape, scale):
        return jax.random.normal(k, shape, f32) * scale

    def gain(k, shape):
        return 1.0 + 0.05 * jax.random.normal(k, shape, f32)

    n = jnp.arange(P, dtype=f32)
    return {
        'x': nrm(ks[0], (BATCH, SEQ, D_MODEL), 1.0),
        'norm_mix_pre': gain(ks[1], (Ld, D_MODEL)),
        'norm_mix_post': gain(ks[2], (Ld, D_MODEL)),
        'norm_ffn_pre': gain(ks[3], (Ld, D_MODEL)),
        'norm_ffn_post': gain(ks[4], (Ld, D_MODEL)),
        'w_in': nrm(ks[5], (Ld, D_MODEL, N_IN_COLS), D_MODEL ** -0.5),
        'b_gate': nrm(ks[6], (Ld, 2 * D_MODEL), 0.02),
        'rwkv_shift_mu': jax.random.uniform(ks[7], (Ld, N_RWKV_COLS), f32, 0.0, 1.0),
        'rwkv_w0': jax.random.uniform(ks[8], (Ld, W), f32, -6.0, -1.0),
        'rwkv_w2': nrm(ks[9], (Ld, DECAY_RANK, W), 0.1 * DECAY_RANK ** -0.5),
        'rwkv_a0': nrm(ks[10], (Ld, W), 0.1),
        'rwkv_a2': nrm(ks[11], (Ld, AAA_RANK, W), 0.1 * AAA_RANK ** -0.5),
        'rwkv_g2': nrm(ks[12], (Ld, GATE_RANK, W), GATE_RANK ** -0.5),
        'rwkv_k_k': 0.85 + 0.05 * jax.random.normal(ks[13], (Ld, W), f32),
        'rwkv_k_a': gain(ks[14], (Ld, W)),
        'rwkv_r_k': nrm(ks[15], (Ld, RWKV_HEADS, RWKV_HEAD), 0.3),
        'rwkv_lnx_w': gain(ks[16], (Ld, W)),
        'rwkv_lnx_b': nrm(ks[17], (Ld, W), 0.02),
        's5_a_re': -0.5 + 0.01 * jax.random.normal(ks[18], (Ld, G, P), f32),
        's5_a_im': math.pi * n + 0.01 * jax.random.normal(ks[19], (Ld, G, P), f32),
        's5_b_re': nrm(ks[20], (Ld, G, P, C), (2 * C) ** -0.5),
        's5_b_im': nrm(ks[21], (Ld, G, P, C), (2 * C) ** -0.5),
        's5_c_re': nrm(ks[22], (Ld, G, C, P), P ** -0.5),
        's5_c_im': nrm(ks[23], (Ld, G, C, P), P ** -0.5),
        's5_d': nrm(ks[24], (Ld, S5_WIDTH), 1.0),
        's5_log_step': jax.random.uniform(ks[25], (Ld, G), f32, math.log(STEP_MIN), math.log(STEP_MAX)),
        's5_w_glu': nrm(ks[26], (Ld, S5_WIDTH, S5_WIDTH), S5_WIDTH ** -0.5),
        's5_b_glu': nrm(ks[27], (Ld, S5_WIDTH), 0.02),
        'w_branch_rwkv': nrm(ks[28], (Ld, W, D_MODEL), W ** -0.5),
        'w_branch_s5': nrm(ks[29], (Ld, S5_WIDTH, D_MODEL), S5_WIDTH ** -0.5),
        'w_out': nrm(ks[30], (Ld, D_MODEL, D_MODEL), D_MODEL ** -0.5),
        'ffn_w_up': nrm(ks[31], (Ld, D_MODEL, 2 * F), D_MODEL ** -0.5),
        'ffn_conv_w': nrm(ks[32], (Ld, CONV_WIDTH, 2 * F), CONV_WIDTH ** -0.5),
        'ffn_conv_b': nrm(ks[33], (Ld, 2 * F), 0.02),
        'ffn_w_down': nrm(ks[34], (Ld, F, D_MODEL), F ** -0.5),
    }


def _fwd_reference(x, norm_mix_pre, norm_mix_post, norm_ffn_pre, norm_ffn_post, w_in, b_gate,
              rwkv_shift_mu, rwkv_w0, rwkv_w2, rwkv_a0, rwkv_a2, rwkv_g2, rwkv_k_k, rwkv_k_a,
              rwkv_r_k, rwkv_lnx_w, rwkv_lnx_b, s5_a_re, s5_a_im, s5_b_re, s5_b_im, s5_c_re,
              s5_c_im, s5_d, s5_log_step, s5_w_glu, s5_b_glu, w_branch_rwkv, w_branch_s5, w_out,
              ffn_w_up, ffn_conv_w, ffn_conv_b, ffn_w_down):
    for l in range(DEPTH):
        h = rms_norm(x, norm_mix_pre[l])
        proj = h @ w_in[l]
        p_rwkv = proj[..., :N_RWKV_COLS]
        u_s5 = proj[..., N_RWKV_COLS:N_RWKV_COLS + S5_WIDTH]
        gates = jax.nn.sigmoid(proj[..., N_RWKV_COLS + S5_WIDTH:] + b_gate[l])
        g_rwkv, g_s5 = jnp.split(gates, 2, axis=-1)
        o_rwkv = rwkv7_branch(p_rwkv, rwkv_shift_mu[l], rwkv_w0[l], rwkv_w2[l], rwkv_a0[l],
                              rwkv_a2[l], rwkv_g2[l], rwkv_k_k[l], rwkv_k_a[l], rwkv_r_k[l],
                              rwkv_lnx_w[l], rwkv_lnx_b[l]) @ w_branch_rwkv[l]
        o_s5 = s5_branch(u_s5, s5_a_re[l], s5_a_im[l], s5_b_re[l], s5_b_im[l], s5_c_re[l],
                         s5_c_im[l], s5_d[l], s5_log_step[l], s5_w_glu[l], s5_b_glu[l]) @ w_branch_s5[l]
        mixed = (g_rwkv * o_rwkv + g_s5 * o_s5) @ w_out[l]
        x = x + rms_norm(mixed, norm_mix_post[l])
        h = rms_norm(x, norm_ffn_pre[l])
        f = conv_ffn(h, ffn_w_up[l], ffn_conv_w[l], ffn_conv_b[l], ffn_w_down[l])
        x = x + rms_norm(f, norm_ffn_post[l])
    return x


import jax as _jax
import jax.numpy as _jnp

TWIN_FORMAT = 'train_step'
FWD_PARAMS = ['x', 'norm_mix_pre', 'norm_mix_post', 'norm_ffn_pre', 'norm_ffn_post', 'w_in', 'b_gate', 'rwkv_shift_mu', 'rwkv_w0', 'rwkv_w2', 'rwkv_a0', 'rwkv_a2', 'rwkv_g2', 'rwkv_k_k', 'rwkv_k_a', 'rwkv_r_k', 'rwkv_lnx_w', 'rwkv_lnx_b', 's5_a_re', 's5_a_im', 's5_b_re', 's5_b_im', 's5_c_re', 's5_c_im', 's5_d', 's5_log_step', 's5_w_glu', 's5_b_glu', 'w_branch_rwkv', 'w_branch_s5', 'w_out', 'ffn_w_up', 'ffn_conv_w', 'ffn_conv_b', 'ffn_w_down']
TWIN_WEIGHTS = ['norm_mix_pre', 'norm_mix_post', 'norm_ffn_pre', 'norm_ffn_post', 'w_in', 'b_gate', 'rwkv_shift_mu', 'rwkv_w0', 'rwkv_w2', 'rwkv_a0', 'rwkv_a2', 'rwkv_g2', 'rwkv_k_k', 'rwkv_k_a', 'rwkv_r_k', 'rwkv_lnx_w', 'rwkv_lnx_b', 's5_a_re', 's5_a_im', 's5_b_re', 's5_b_im', 's5_c_re', 's5_c_im', 's5_d', 's5_log_step', 's5_w_glu', 's5_b_glu', 'w_branch_rwkv', 'w_branch_s5', 'w_out', 'ffn_w_up', 'ffn_conv_w', 'ffn_conv_b', 'ffn_w_down']
TWIN_DIFF_INPUT = 'x'
TWIN_INPUTS = ['x', 'norm_mix_pre', 'norm_mix_post', 'norm_ffn_pre', 'norm_ffn_post', 'w_in', 'b_gate', 'rwkv_shift_mu', 'rwkv_w0', 'rwkv_w2', 'rwkv_a0', 'rwkv_a2', 'rwkv_g2', 'rwkv_k_k', 'rwkv_k_a', 'rwkv_r_k', 'rwkv_lnx_w', 'rwkv_lnx_b', 's5_a_re', 's5_a_im', 's5_b_re', 's5_b_im', 's5_c_re', 's5_c_im', 's5_d', 's5_log_step', 's5_w_glu', 's5_b_glu', 'w_branch_rwkv', 'w_branch_s5', 'w_out', 'ffn_w_up', 'ffn_conv_w', 'ffn_conv_b', 'ffn_w_down', 'loss_target', 'm_norm_mix_pre', 'm_norm_mix_post', 'm_norm_ffn_pre', 'm_norm_ffn_post', 'm_w_in', 'm_b_gate', 'm_rwkv_shift_mu', 'm_rwkv_w0', 'm_rwkv_w2', 'm_rwkv_a0', 'm_rwkv_a2', 'm_rwkv_g2', 'm_rwkv_k_k', 'm_rwkv_k_a', 'm_rwkv_r_k', 'm_rwkv_lnx_w', 'm_rwkv_lnx_b', 'm_s5_a_re', 'm_s5_a_im', 'm_s5_b_re', 'm_s5_b_im', 'm_s5_c_re', 'm_s5_c_im', 'm_s5_d', 'm_s5_log_step', 'm_s5_w_glu', 'm_s5_b_glu', 'm_w_branch_rwkv', 'm_w_branch_s5', 'm_w_out', 'm_ffn_w_up', 'm_ffn_conv_w', 'm_ffn_conv_b', 'm_ffn_w_down', 'v_norm_mix_pre', 'v_norm_mix_post', 'v_norm_ffn_pre', 'v_norm_ffn_post', 'v_w_in', 'v_b_gate', 'v_rwkv_shift_mu', 'v_rwkv_w0', 'v_rwkv_w2', 'v_rwkv_a0', 'v_rwkv_a2', 'v_rwkv_g2', 'v_rwkv_k_k', 'v_rwkv_k_a', 'v_rwkv_r_k', 'v_rwkv_lnx_w', 'v_rwkv_lnx_b', 'v_s5_a_re', 'v_s5_a_im', 'v_s5_b_re', 'v_s5_b_im', 'v_s5_c_re', 'v_s5_c_im', 'v_s5_d', 'v_s5_log_step', 'v_s5_w_glu', 'v_s5_b_glu', 'v_w_branch_rwkv', 'v_w_branch_s5', 'v_w_out', 'v_ffn_w_up', 'v_ffn_conv_w', 'v_ffn_conv_b', 'v_ffn_w_down']
TWIN_OUTPUTS = ['loss', 'grad_x', 'grad_norm_mix_pre', 'grad_norm_mix_post', 'grad_norm_ffn_pre', 'grad_norm_ffn_post', 'grad_w_in', 'grad_b_gate', 'grad_rwkv_shift_mu', 'grad_rwkv_w0', 'grad_rwkv_w2', 'grad_rwkv_a0', 'grad_rwkv_a2', 'grad_rwkv_g2', 'grad_rwkv_k_k', 'grad_rwkv_k_a', 'grad_rwkv_r_k', 'grad_rwkv_lnx_w', 'grad_rwkv_lnx_b', 'grad_s5_a_re', 'grad_s5_a_im', 'grad_s5_b_re', 'grad_s5_b_im', 'grad_s5_c_re', 'grad_s5_c_im', 'grad_s5_d', 'grad_s5_log_step', 'grad_s5_w_glu', 'grad_s5_b_glu', 'grad_w_branch_rwkv', 'grad_w_branch_s5', 'grad_w_out', 'grad_ffn_w_up', 'grad_ffn_conv_w', 'grad_ffn_conv_b', 'grad_ffn_w_down', 'delta_norm_mix_pre', 'delta_norm_mix_post', 'delta_norm_ffn_pre', 'delta_norm_ffn_post', 'delta_w_in', 'delta_b_gate', 'delta_rwkv_shift_mu', 'delta_rwkv_w0', 'delta_rwkv_w2', 'delta_rwkv_a0', 'delta_rwkv_a2', 'delta_rwkv_g2', 'delta_rwkv_k_k', 'delta_rwkv_k_a', 'delta_rwkv_r_k', 'delta_rwkv_lnx_w', 'delta_rwkv_lnx_b', 'delta_s5_a_re', 'delta_s5_a_im', 'delta_s5_b_re', 'delta_s5_b_im', 'delta_s5_c_re', 'delta_s5_c_im', 'delta_s5_d', 'delta_s5_log_step', 'delta_s5_w_glu', 'delta_s5_b_glu', 'delta_w_branch_rwkv', 'delta_w_branch_s5', 'delta_w_out', 'delta_ffn_w_up', 'delta_ffn_conv_w', 'delta_ffn_conv_b', 'delta_ffn_w_down', 'new_m_norm_mix_pre', 'new_m_norm_mix_post', 'new_m_norm_ffn_pre', 'new_m_norm_ffn_post', 'new_m_w_in', 'new_m_b_gate', 'new_m_rwkv_shift_mu', 'new_m_rwkv_w0', 'new_m_rwkv_w2', 'new_m_rwkv_a0', 'new_m_rwkv_a2', 'new_m_rwkv_g2', 'new_m_rwkv_k_k', 'new_m_rwkv_k_a', 'new_m_rwkv_r_k', 'new_m_rwkv_lnx_w', 'new_m_rwkv_lnx_b', 'new_m_s5_a_re', 'new_m_s5_a_im', 'new_m_s5_b_re', 'new_m_s5_b_im', 'new_m_s5_c_re', 'new_m_s5_c_im', 'new_m_s5_d', 'new_m_s5_log_step', 'new_m_s5_w_glu', 'new_m_s5_b_glu', 'new_m_w_branch_rwkv', 'new_m_w_branch_s5', 'new_m_w_out', 'new_m_ffn_w_up', 'new_m_ffn_conv_w', 'new_m_ffn_conv_b', 'new_m_ffn_w_down', 'new_v_norm_mix_pre', 'new_v_norm_mix_post', 'new_v_norm_ffn_pre', 'new_v_norm_ffn_post', 'new_v_w_in', 'new_v_b_gate', 'new_v_rwkv_shift_mu', 'new_v_rwkv_w0', 'new_v_rwkv_w2', 'new_v_rwkv_a0', 'new_v_rwkv_a2', 'new_v_rwkv_g2', 'new_v_rwkv_k_k', 'new_v_rwkv_k_a', 'new_v_rwkv_r_k', 'new_v_rwkv_lnx_w', 'new_v_rwkv_lnx_b', 'new_v_s5_a_re', 'new_v_s5_a_im', 'new_v_s5_b_re', 'new_v_s5_b_im', 'new_v_s5_c_re', 'new_v_s5_c_im', 'new_v_s5_d', 'new_v_s5_log_step', 'new_v_s5_w_glu', 'new_v_s5_b_glu', 'new_v_w_branch_rwkv', 'new_v_w_branch_s5', 'new_v_w_out', 'new_v_ffn_w_up', 'new_v_ffn_conv_w', 'new_v_ffn_conv_b', 'new_v_ffn_w_down']
TWIN_LEAF_KINDS = {'loss': 'loss', 'grad_x': 'grad_x', 'grad_norm_mix_pre': 'grad_w', 'grad_norm_mix_post': 'grad_w', 'grad_norm_ffn_pre': 'grad_w', 'grad_norm_ffn_post': 'grad_w', 'grad_w_in': 'grad_w', 'grad_b_gate': 'grad_w', 'grad_rwkv_shift_mu': 'grad_w', 'grad_rwkv_w0': 'grad_w', 'grad_rwkv_w2': 'grad_w', 'grad_rwkv_a0': 'grad_w', 'grad_rwkv_a2': 'grad_w', 'grad_rwkv_g2': 'grad_w', 'grad_rwkv_k_k': 'grad_w', 'grad_rwkv_k_a': 'grad_w', 'grad_rwkv_r_k': 'grad_w', 'grad_rwkv_lnx_w': 'grad_w', 'grad_rwkv_lnx_b': 'grad_w', 'grad_s5_a_re': 'grad_w', 'grad_s5_a_im': 'grad_w', 'grad_s5_b_re': 'grad_w', 'grad_s5_b_im': 'grad_w', 'grad_s5_c_re': 'grad_w', 'grad_s5_c_im': 'grad_w', 'grad_s5_d': 'grad_w', 'grad_s5_log_step': 'grad_w', 'grad_s5_w_glu': 'grad_w', 'grad_s5_b_glu': 'grad_w', 'grad_w_branch_rwkv': 'grad_w', 'grad_w_branch_s5': 'grad_w', 'grad_w_out': 'grad_w', 'grad_ffn_w_up': 'grad_w', 'grad_ffn_conv_w': 'grad_w', 'grad_ffn_conv_b': 'grad_w', 'grad_ffn_w_down': 'grad_w', 'delta_norm_mix_pre': 'delta_w', 'delta_norm_mix_post': 'delta_w', 'delta_norm_ffn_pre': 'delta_w', 'delta_norm_ffn_post': 'delta_w', 'delta_w_in': 'delta_w', 'delta_b_gate': 'delta_w', 'delta_rwkv_shift_mu': 'delta_w', 'delta_rwkv_w0': 'delta_w', 'delta_rwkv_w2': 'delta_w', 'delta_rwkv_a0': 'delta_w', 'delta_rwkv_a2': 'delta_w', 'delta_rwkv_g2': 'delta_w', 'delta_rwkv_k_k': 'delta_w', 'delta_rwkv_k_a': 'delta_w', 'delta_rwkv_r_k': 'delta_w', 'delta_rwkv_lnx_w': 'delta_w', 'delta_rwkv_lnx_b': 'delta_w', 'delta_s5_a_re': 'delta_w', 'delta_s5_a_im': 'delta_w', 'delta_s5_b_re': 'delta_w', 'delta_s5_b_im': 'delta_w', 'delta_s5_c_re': 'delta_w', 'delta_s5_c_im': 'delta_w', 'delta_s5_d': 'delta_w', 'delta_s5_log_step': 'delta_w', 'delta_s5_w_glu': 'delta_w', 'delta_s5_b_glu': 'delta_w', 'delta_w_branch_rwkv': 'delta_w', 'delta_w_branch_s5': 'delta_w', 'delta_w_out': 'delta_w', 'delta_ffn_w_up': 'delta_w', 'delta_ffn_conv_w': 'delta_w', 'delta_ffn_conv_b': 'delta_w', 'delta_ffn_w_down': 'delta_w', 'new_m_norm_mix_pre': 'new_m', 'new_m_norm_mix_post': 'new_m', 'new_m_norm_ffn_pre': 'new_m', 'new_m_norm_ffn_post': 'new_m', 'new_m_w_in': 'new_m', 'new_m_b_gate': 'new_m', 'new_m_rwkv_shift_mu': 'new_m', 'new_m_rwkv_w0': 'new_m', 'new_m_rwkv_w2': 'new_m', 'new_m_rwkv_a0': 'new_m', 'new_m_rwkv_a2': 'new_m', 'new_m_rwkv_g2': 'new_m', 'new_m_rwkv_k_k': 'new_m', 'new_m_rwkv_k_a': 'new_m', 'new_m_rwkv_r_k': 'new_m', 'new_m_rwkv_lnx_w': 'new_m', 'new_m_rwkv_lnx_b': 'new_m', 'new_m_s5_a_re': 'new_m', 'new_m_s5_a_im': 'new_m', 'new_m_s5_b_re': 'new_m', 'new_m_s5_b_im': 'new_m', 'new_m_s5_c_re': 'new_m', 'new_m_s5_c_im': 'new_m', 'new_m_s5_d': 'new_m', 'new_m_s5_log_step': 'new_m', 'new_m_s5_w_glu': 'new_m', 'new_m_s5_b_glu': 'new_m', 'new_m_w_branch_rwkv': 'new_m', 'new_m_w_branch_s5': 'new_m', 'new_m_w_out': 'new_m', 'new_m_ffn_w_up': 'new_m', 'new_m_ffn_conv_w': 'new_m', 'new_m_ffn_conv_b': 'new_m', 'new_m_ffn_w_down': 'new_m', 'new_v_norm_mix_pre': 'new_v', 'new_v_norm_mix_post': 'new_v', 'new_v_norm_ffn_pre': 'new_v', 'new_v_norm_ffn_post': 'new_v', 'new_v_w_in': 'new_v', 'new_v_b_gate': 'new_v', 'new_v_rwkv_shift_mu': 'new_v', 'new_v_rwkv_w0': 'new_v', 'new_v_rwkv_w2': 'new_v', 'new_v_rwkv_a0': 'new_v', 'new_v_rwkv_a2': 'new_v', 'new_v_rwkv_g2': 'new_v', 'new_v_rwkv_k_k': 'new_v', 'new_v_rwkv_k_a': 'new_v', 'new_v_rwkv_r_k': 'new_v', 'new_v_rwkv_lnx_w': 'new_v', 'new_v_rwkv_lnx_b': 'new_v', 'new_v_s5_a_re': 'new_v', 'new_v_s5_a_im': 'new_v', 'new_v_s5_b_re': 'new_v', 'new_v_s5_b_im': 'new_v', 'new_v_s5_c_re': 'new_v', 'new_v_s5_c_im': 'new_v', 'new_v_s5_d': 'new_v', 'new_v_s5_log_step': 'new_v', 'new_v_s5_w_glu': 'new_v', 'new_v_s5_b_glu': 'new_v', 'new_v_w_branch_rwkv': 'new_v', 'new_v_w_branch_s5': 'new_v', 'new_v_w_out': 'new_v', 'new_v_ffn_w_up': 'new_v', 'new_v_ffn_conv_w': 'new_v', 'new_v_ffn_conv_b': 'new_v', 'new_v_ffn_w_down': 'new_v'}


def _forward(args):
    return _fwd_reference(*[args[k] for k in FWD_PARAMS])


def _output_shape():
    out = _jax.eval_shape(lambda: _forward(_fwd_setup_inputs(0)))
    return out.shape, out.dtype

N_MICROBATCH = 1
ADAM_LR = 0.001
ADAM_B1 = 0.9
ADAM_B2 = 0.999
ADAM_EPS = 1e-08
ADAM_WD = 0.01
ADAM_STEP = 10
PER_EXAMPLE_BATCH_AXIS = {'x': 0, 'loss_target': 0}
SHARED_INPUTS = []
_WEIGHT_DTYPES = {'norm_mix_pre': _jnp.float32, 'norm_mix_post': _jnp.float32, 'norm_ffn_pre': _jnp.float32, 'norm_ffn_post': _jnp.float32, 'w_in': _jnp.float32, 'b_gate': _jnp.float32, 'rwkv_shift_mu': _jnp.float32, 'rwkv_w0': _jnp.float32, 'rwkv_w2': _jnp.float32, 'rwkv_a0': _jnp.float32, 'rwkv_a2': _jnp.float32, 'rwkv_g2': _jnp.float32, 'rwkv_k_k': _jnp.float32, 'rwkv_k_a': _jnp.float32, 'rwkv_r_k': _jnp.float32, 'rwkv_lnx_w': _jnp.float32, 'rwkv_lnx_b': _jnp.float32, 's5_a_re': _jnp.float32, 's5_a_im': _jnp.float32, 's5_b_re': _jnp.float32, 's5_b_im': _jnp.float32, 's5_c_re': _jnp.float32, 's5_c_im': _jnp.float32, 's5_d': _jnp.float32, 's5_log_step': _jnp.float32, 's5_w_glu': _jnp.float32, 's5_b_glu': _jnp.float32, 'w_branch_rwkv': _jnp.float32, 'w_branch_s5': _jnp.float32, 'w_out': _jnp.float32, 'ffn_w_up': _jnp.float32, 'ffn_conv_w': _jnp.float32, 'ffn_conv_b': _jnp.float32, 'ffn_w_down': _jnp.float32}
MOMENT_SCALE = {'norm_mix_pre': 9.092445e-01, 'norm_mix_post': 3.208196e+01, 'norm_ffn_pre': 6.724648e-01, 'norm_ffn_post': 3.189390e+01, 'w_in': 3.799523e-01, 'b_gate': 2.149067e-01, 'rwkv_shift_mu': 8.919104e-01, 'rwkv_w0': 2.288621e-01, 'rwkv_w2': 2.186712e-02, 'rwkv_a0': 2.655959e-01, 'rwkv_a2': 2.337208e-01, 'rwkv_g2': 5.882931e-01, 'rwkv_k_k': 7.769370e-01, 'rwkv_k_a': 6.053046e-01, 'rwkv_r_k': 1.003778e+00, 'rwkv_lnx_w': 7.327882e-01, 'rwkv_lnx_b': 2.334152e+00, 's5_a_re': 2.717278e-02, 's5_a_im': 2.351556e-02, 's5_b_re': 1.637155e-02, 's5_b_im': 1.606495e-02, 's5_c_re': 2.315897e-02, 's5_c_im': 2.294594e-02, 's5_d': 9.221251e-01, 's5_log_step': 3.084429e+01, 's5_w_glu': 1.258202e-01, 's5_b_glu': 3.248969e-01, 'w_branch_rwkv': 4.686269e-01, 'w_branch_s5': 5.545695e-01, 'w_out': 7.062190e-01, 'ffn_w_up': 2.646145e-01, 'ffn_conv_w': 2.952056e-01, 'ffn_conv_b': 1.266465e+00, 'ffn_w_down': 5.270672e-01}


def _to_microbatches(a, axis):
    t = _jnp.moveaxis(a, axis, 0)
    t = t.reshape((N_MICROBATCH, t.shape[0] // N_MICROBATCH) + t.shape[1:])
    return _jnp.moveaxis(t, 1, axis + 1)


def setup_inputs(seed: int = 0) -> dict:
    inp = _fwd_setup_inputs(seed)
    key = _jax.random.fold_in(_jax.random.key(seed), 7919)
    shape, _ = _output_shape()
    out = dict(inp)
    out["loss_target"] = _jax.random.normal(_jax.random.fold_in(key, 0), shape, _jnp.float32)
    for i, name in enumerate(TWIN_WEIGHTS):
        w = inp[name].astype(_jnp.float32)
        if MOMENT_SCALE is None:
            s = _jnp.sqrt(_jnp.mean(_jnp.square(w)) + 1e-30)
        else:
            s = MOMENT_SCALE[name]
        km, kv = _jax.random.split(_jax.random.fold_in(key, i + 1))
        out[name] = w
        out["m_" + name] = s * _jax.random.normal(km, w.shape, _jnp.float32)
        out["v_" + name] = (s * s) * _jax.random.uniform(kv, w.shape, _jnp.float32, 0.5, 1.5)
    if N_MICROBATCH > 1:
        for name, axis in PER_EXAMPLE_BATCH_AXIS.items():
            out[name] = _to_microbatches(out[name], axis)
    return {'x': out['x'], 'norm_mix_pre': out['norm_mix_pre'], 'norm_mix_post': out['norm_mix_post'], 'norm_ffn_pre': out['norm_ffn_pre'], 'norm_ffn_post': out['norm_ffn_post'], 'w_in': out['w_in'], 'b_gate': out['b_gate'], 'rwkv_shift_mu': out['rwkv_shift_mu'], 'rwkv_w0': out['rwkv_w0'], 'rwkv_w2': out['rwkv_w2'], 'rwkv_a0': out['rwkv_a0'], 'rwkv_a2': out['rwkv_a2'], 'rwkv_g2': out['rwkv_g2'], 'rwkv_k_k': out['rwkv_k_k'], 'rwkv_k_a': out['rwkv_k_a'], 'rwkv_r_k': out['rwkv_r_k'], 'rwkv_lnx_w': out['rwkv_lnx_w'], 'rwkv_lnx_b': out['rwkv_lnx_b'], 's5_a_re': out['s5_a_re'], 's5_a_im': out['s5_a_im'], 's5_b_re': out['s5_b_re'], 's5_b_im': out['s5_b_im'], 's5_c_re': out['s5_c_re'], 's5_c_im': out['s5_c_im'], 's5_d': out['s5_d'], 's5_log_step': out['s5_log_step'], 's5_w_glu': out['s5_w_glu'], 's5_b_glu': out['s5_b_glu'], 'w_branch_rwkv': out['w_branch_rwkv'], 'w_branch_s5': out['w_branch_s5'], 'w_out': out['w_out'], 'ffn_w_up': out['ffn_w_up'], 'ffn_conv_w': out['ffn_conv_w'], 'ffn_conv_b': out['ffn_conv_b'], 'ffn_w_down': out['ffn_w_down'], 'loss_target': out['loss_target'], 'm_norm_mix_pre': out['m_norm_mix_pre'], 'm_norm_mix_post': out['m_norm_mix_post'], 'm_norm_ffn_pre': out['m_norm_ffn_pre'], 'm_norm_ffn_post': out['m_norm_ffn_post'], 'm_w_in': out['m_w_in'], 'm_b_gate': out['m_b_gate'], 'm_rwkv_shift_mu': out['m_rwkv_shift_mu'], 'm_rwkv_w0': out['m_rwkv_w0'], 'm_rwkv_w2': out['m_rwkv_w2'], 'm_rwkv_a0': out['m_rwkv_a0'], 'm_rwkv_a2': out['m_rwkv_a2'], 'm_rwkv_g2': out['m_rwkv_g2'], 'm_rwkv_k_k': out['m_rwkv_k_k'], 'm_rwkv_k_a': out['m_rwkv_k_a'], 'm_rwkv_r_k': out['m_rwkv_r_k'], 'm_rwkv_lnx_w': out['m_rwkv_lnx_w'], 'm_rwkv_lnx_b': out['m_rwkv_lnx_b'], 'm_s5_a_re': out['m_s5_a_re'], 'm_s5_a_im': out['m_s5_a_im'], 'm_s5_b_re': out['m_s5_b_re'], 'm_s5_b_im': out['m_s5_b_im'], 'm_s5_c_re': out['m_s5_c_re'], 'm_s5_c_im': out['m_s5_c_im'], 'm_s5_d': out['m_s5_d'], 'm_s5_log_step': out['m_s5_log_step'], 'm_s5_w_glu': out['m_s5_w_glu'], 'm_s5_b_glu': out['m_s5_b_glu'], 'm_w_branch_rwkv': out['m_w_branch_rwkv'], 'm_w_branch_s5': out['m_w_branch_s5'], 'm_w_out': out['m_w_out'], 'm_ffn_w_up': out['m_ffn_w_up'], 'm_ffn_conv_w': out['m_ffn_conv_w'], 'm_ffn_conv_b': out['m_ffn_conv_b'], 'm_ffn_w_down': out['m_ffn_w_down'], 'v_norm_mix_pre': out['v_norm_mix_pre'], 'v_norm_mix_post': out['v_norm_mix_post'], 'v_norm_ffn_pre': out['v_norm_ffn_pre'], 'v_norm_ffn_post': out['v_norm_ffn_post'], 'v_w_in': out['v_w_in'], 'v_b_gate': out['v_b_gate'], 'v_rwkv_shift_mu': out['v_rwkv_shift_mu'], 'v_rwkv_w0': out['v_rwkv_w0'], 'v_rwkv_w2': out['v_rwkv_w2'], 'v_rwkv_a0': out['v_rwkv_a0'], 'v_rwkv_a2': out['v_rwkv_a2'], 'v_rwkv_g2': out['v_rwkv_g2'], 'v_rwkv_k_k': out['v_rwkv_k_k'], 'v_rwkv_k_a': out['v_rwkv_k_a'], 'v_rwkv_r_k': out['v_rwkv_r_k'], 'v_rwkv_lnx_w': out['v_rwkv_lnx_w'], 'v_rwkv_lnx_b': out['v_rwkv_lnx_b'], 'v_s5_a_re': out['v_s5_a_re'], 'v_s5_a_im': out['v_s5_a_im'], 'v_s5_b_re': out['v_s5_b_re'], 'v_s5_b_im': out['v_s5_b_im'], 'v_s5_c_re': out['v_s5_c_re'], 'v_s5_c_im': out['v_s5_c_im'], 'v_s5_d': out['v_s5_d'], 'v_s5_log_step': out['v_s5_log_step'], 'v_s5_w_glu': out['v_s5_w_glu'], 'v_s5_b_glu': out['v_s5_b_glu'], 'v_w_branch_rwkv': out['v_w_branch_rwkv'], 'v_w_branch_s5': out['v_w_branch_s5'], 'v_w_out': out['v_w_out'], 'v_ffn_w_up': out['v_ffn_w_up'], 'v_ffn_conv_w': out['v_ffn_conv_w'], 'v_ffn_conv_b': out['v_ffn_conv_b'], 'v_ffn_w_down': out['v_ffn_w_down']}


def _loss(weights, diff, rest, loss_target):
    with _jax.named_scope("forward"):
        args = {**rest, TWIN_DIFF_INPUT: diff, **{k: w.astype(_WEIGHT_DTYPES[k]) for k, w in weights.items()}}
        y = _forward(args)
    with _jax.named_scope("loss_head"):
        err = _jnp.square(y.astype(_jnp.float32) - loss_target)
        return 0.5 * _jnp.sum(_jnp.mean(err, axis=-1)) if err.ndim else 0.5 * err


def _adamw(w, g, m, v):
    m = ADAM_B1 * m + (1.0 - ADAM_B1) * g
    v = ADAM_B2 * v + (1.0 - ADAM_B2) * _jnp.square(g)
    m_hat = m / (1.0 - ADAM_B1 ** ADAM_STEP)
    v_hat = v / (1.0 - ADAM_B2 ** ADAM_STEP)
    delta = -ADAM_LR * (m_hat / (_jnp.sqrt(v_hat) + ADAM_EPS) + ADAM_WD * w)
    return delta, m, v


def reference(x, norm_mix_pre, norm_mix_post, norm_ffn_pre, norm_ffn_post, w_in, b_gate, rwkv_shift_mu, rwkv_w0, rwkv_w2, rwkv_a0, rwkv_a2, rwkv_g2, rwkv_k_k, rwkv_k_a, rwkv_r_k, rwkv_lnx_w, rwkv_lnx_b, s5_a_re, s5_a_im, s5_b_re, s5_b_im, s5_c_re, s5_c_im, s5_d, s5_log_step, s5_w_glu, s5_b_glu, w_branch_rwkv, w_branch_s5, w_out, ffn_w_up, ffn_conv_w, ffn_conv_b, ffn_w_down, loss_target, m_norm_mix_pre, m_norm_mix_post, m_norm_ffn_pre, m_norm_ffn_post, m_w_in, m_b_gate, m_rwkv_shift_mu, m_rwkv_w0, m_rwkv_w2, m_rwkv_a0, m_rwkv_a2, m_rwkv_g2, m_rwkv_k_k, m_rwkv_k_a, m_rwkv_r_k, m_rwkv_lnx_w, m_rwkv_lnx_b, m_s5_a_re, m_s5_a_im, m_s5_b_re, m_s5_b_im, m_s5_c_re, m_s5_c_im, m_s5_d, m_s5_log_step, m_s5_w_glu, m_s5_b_glu, m_w_branch_rwkv, m_w_branch_s5, m_w_out, m_ffn_w_up, m_ffn_conv_w, m_ffn_conv_b, m_ffn_w_down, v_norm_mix_pre, v_norm_mix_post, v_norm_ffn_pre, v_norm_ffn_post, v_w_in, v_b_gate, v_rwkv_shift_mu, v_rwkv_w0, v_rwkv_w2, v_rwkv_a0, v_rwkv_a2, v_rwkv_g2, v_rwkv_k_k, v_rwkv_k_a, v_rwkv_r_k, v_rwkv_lnx_w, v_rwkv_lnx_b, v_s5_a_re, v_s5_a_im, v_s5_b_re, v_s5_b_im, v_s5_c_re, v_s5_c_im, v_s5_d, v_s5_log_step, v_s5_w_glu, v_s5_b_glu, v_w_branch_rwkv, v_w_branch_s5, v_w_out, v_ffn_w_up, v_ffn_conv_w, v_ffn_conv_b, v_ffn_w_down):
    given = dict(x=x, norm_mix_pre=norm_mix_pre, norm_mix_post=norm_mix_post, norm_ffn_pre=norm_ffn_pre, norm_ffn_post=norm_ffn_post, w_in=w_in, b_gate=b_gate, rwkv_shift_mu=rwkv_shift_mu, rwkv_w0=rwkv_w0, rwkv_w2=rwkv_w2, rwkv_a0=rwkv_a0, rwkv_a2=rwkv_a2, rwkv_g2=rwkv_g2, rwkv_k_k=rwkv_k_k, rwkv_k_a=rwkv_k_a, rwkv_r_k=rwkv_r_k, rwkv_lnx_w=rwkv_lnx_w, rwkv_lnx_b=rwkv_lnx_b, s5_a_re=s5_a_re, s5_a_im=s5_a_im, s5_b_re=s5_b_re, s5_b_im=s5_b_im, s5_c_re=s5_c_re, s5_c_im=s5_c_im, s5_d=s5_d, s5_log_step=s5_log_step, s5_w_glu=s5_w_glu, s5_b_glu=s5_b_glu, w_branch_rwkv=w_branch_rwkv, w_branch_s5=w_branch_s5, w_out=w_out, ffn_w_up=ffn_w_up, ffn_conv_w=ffn_conv_w, ffn_conv_b=ffn_conv_b, ffn_w_down=ffn_w_down, loss_target=loss_target, m_norm_mix_pre=m_norm_mix_pre, m_norm_mix_post=m_norm_mix_post, m_norm_ffn_pre=m_norm_ffn_pre, m_norm_ffn_post=m_norm_ffn_post, m_w_in=m_w_in, m_b_gate=m_b_gate, m_rwkv_shift_mu=m_rwkv_shift_mu, m_rwkv_w0=m_rwkv_w0, m_rwkv_w2=m_rwkv_w2, m_rwkv_a0=m_rwkv_a0, m_rwkv_a2=m_rwkv_a2, m_rwkv_g2=m_rwkv_g2, m_rwkv_k_k=m_rwkv_k_k, m_rwkv_k_a=m_rwkv_k_a, m_rwkv_r_k=m_rwkv_r_k, m_rwkv_lnx_w=m_rwkv_lnx_w, m_rwkv_lnx_b=m_rwkv_lnx_b, m_s5_a_re=m_s5_a_re, m_s5_a_im=m_s5_a_im, m_s5_b_re=m_s5_b_re, m_s5_b_im=m_s5_b_im, m_s5_c_re=m_s5_c_re, m_s5_c_im=m_s5_c_im, m_s5_d=m_s5_d, m_s5_log_step=m_s5_log_step, m_s5_w_glu=m_s5_w_glu, m_s5_b_glu=m_s5_b_glu, m_w_branch_rwkv=m_w_branch_rwkv, m_w_branch_s5=m_w_branch_s5, m_w_out=m_w_out, m_ffn_w_up=m_ffn_w_up, m_ffn_conv_w=m_ffn_conv_w, m_ffn_conv_b=m_ffn_conv_b, m_ffn_w_down=m_ffn_w_down, v_norm_mix_pre=v_norm_mix_pre, v_norm_mix_post=v_norm_mix_post, v_norm_ffn_pre=v_norm_ffn_pre, v_norm_ffn_post=v_norm_ffn_post, v_w_in=v_w_in, v_b_gate=v_b_gate, v_rwkv_shift_mu=v_rwkv_shift_mu, v_rwkv_w0=v_rwkv_w0, v_rwkv_w2=v_rwkv_w2, v_rwkv_a0=v_rwkv_a0, v_rwkv_a2=v_rwkv_a2, v_rwkv_g2=v_rwkv_g2, v_rwkv_k_k=v_rwkv_k_k, v_rwkv_k_a=v_rwkv_k_a, v_rwkv_r_k=v_rwkv_r_k, v_rwkv_lnx_w=v_rwkv_lnx_w, v_rwkv_lnx_b=v_rwkv_lnx_b, v_s5_a_re=v_s5_a_re, v_s5_a_im=v_s5_a_im, v_s5_b_re=v_s5_b_re, v_s5_b_im=v_s5_b_im, v_s5_c_re=v_s5_c_re, v_s5_c_im=v_s5_c_im, v_s5_d=v_s5_d, v_s5_log_step=v_s5_log_step, v_s5_w_glu=v_s5_w_glu, v_s5_b_glu=v_s5_b_glu, v_w_branch_rwkv=v_w_branch_rwkv, v_w_branch_s5=v_w_branch_s5, v_w_out=v_w_out, v_ffn_w_up=v_ffn_w_up, v_ffn_conv_w=v_ffn_conv_w, v_ffn_conv_b=v_ffn_conv_b, v_ffn_w_down=v_ffn_w_down)
    weights = {n: given[n] for n in TWIN_WEIGHTS}
    shared = {n: given[n] for n in SHARED_INPUTS}
    per_example = {n: given[n] for n in ['x']}
    grad_fn = _jax.value_and_grad(_loss, argnums=(0, 1))

    def one_microbatch(ex, loss_target):
        ex = dict(ex)
        diff = ex.pop(TWIN_DIFF_INPUT)
        return grad_fn(weights, diff, {**shared, **ex}, loss_target)

    if N_MICROBATCH == 1:
        loss, (grad_w, grad_x) = one_microbatch(per_example, given["loss_target"])
    else:
        def body(carry, xs):
            loss_sum, grad_sum = carry
            l_k, (gw_k, gx_k) = one_microbatch(xs[0], xs[1])
            with _jax.named_scope("update"):
                return (loss_sum + l_k, _jax.tree.map(_jnp.add, grad_sum, gw_k)), gx_k

        init = (_jnp.zeros((), _jnp.float32), _jax.tree.map(_jnp.zeros_like, weights))
        (loss, grad_w), grad_x = _jax.lax.scan(body, init, (per_example, given["loss_target"]))
    with _jax.named_scope("update"):
        delta_w, new_m, new_v = {}, {}, {}
        for n in TWIN_WEIGHTS:
            delta_w[n], new_m[n], new_v[n] = _adamw(weights[n], grad_w[n], given["m_" + n], given["v_" + n])
    return (loss, grad_x, *[grad_w[n] for n in TWIN_WEIGHTS], *[delta_w[n] for n in TWIN_WEIGHTS],
            *[new_m[n] for n in TWIN_WEIGHTS], *[new_v[n] for n in TWIN_WEIGHTS])
```

```python
import functools
import math

import jax
import jax.numpy as jnp
from jax import lax
from jax.experimental import pallas as pl
from jax.experimental.pallas import tpu as pltpu

F32 = jnp.float32
BF16 = jnp.bfloat16
SDS = jax.ShapeDtypeStruct
HI = lax.Precision.HIGHEST
MESH = pl.DeviceIdType.MESH
AXES = ("x", "y", "c")
N_DEV = 8

D_MODEL = 1024
RW = 512
HN = 64
N_RWKV = 1792
S5W = 512
S5G = 32
S5C = 16
S5P = 64
S5N = S5G * S5P
D_FF = 2816
NORM_EPS = 1e-6
LNX_EPS = 64e-5

ADAM_LR = 0.001
ADAM_B1 = 0.9
ADAM_B2 = 0.999
ADAM_EPS = 1e-08
ADAM_WD = 0.01
ADAM_STEP = 10

LANES = 128
SUBLANES = 8
VMEM_LIMIT = 56 * 1024 * 1024
WKV_CHUNK = 32


def _params(sem=("arbitrary",)):
    return pltpu.CompilerParams(dimension_semantics=sem, vmem_limit_bytes=VMEM_LIMIT)


def _pick(n, cap):
    best = None
    for t in range(LANES, min(n, cap) + 1, LANES):
        if n % t == 0:
            best = t
    return best or n


def _mm(a, b, *, ta=False, tb=False, out_dtype=F32, name):
    m = a.shape[1] if ta else a.shape[0]
    k = a.shape[0] if ta else a.shape[1]
    n = b.shape[0] if tb else b.shape[1]
    assert (b.shape[1] if tb else b.shape[0]) == k
    tm, tn, tk = _pick(m, 512), _pick(n, 2304), _pick(k, 512)
    nk = k // tk
    dims = (((0 if ta else 1,), (1 if tb else 0,)), ((), ()))

    def body(a_ref, b_ref, o_ref, acc_ref):
        kk = pl.program_id(2)

        @pl.when(kk == 0)
        def _():
            acc_ref[...] = jnp.zeros_like(acc_ref)

        acc_ref[...] += lax.dot_general(a_ref[...].astype(BF16), b_ref[...].astype(BF16), dims,
                                        preferred_element_type=F32)

        @pl.when(kk == nk - 1)
        def _():
            o_ref[...] = acc_ref[...].astype(o_ref.dtype)

    a_spec = pl.BlockSpec((tk, tm), lambda i, j, kk: (kk, i)) if ta else pl.BlockSpec((tm, tk), lambda i, j, kk: (i, kk))
    b_spec = pl.BlockSpec((tn, tk), lambda i, j, kk: (j, kk)) if tb else pl.BlockSpec((tk, tn), lambda i, j, kk: (kk, j))
    return pl.pallas_call(
        body, grid=(m // tm, n // tn, nk), in_specs=[a_spec, b_spec],
        out_specs=pl.BlockSpec((tm, tn), lambda i, j, kk: (i, j)),
        out_shape=SDS((m, n), out_dtype), scratch_shapes=[pltpu.VMEM((tm, tn), F32)],
        compiler_params=_params(("parallel", "parallel", "arbitrary")), name=name)(a, b)


def _rows(fn, rows, params, out_rows, out_accs, *, name, tl, reverse=False, scratch=()):
    first = rows[0][0] if isinstance(rows[0], tuple) else rows[0]
    length = first.shape[0]
    tl = min(tl, length)
    nt = length // tl
    rmap = (lambda i: nt - 1 - i) if reverse else (lambda i: i)
    specs, arrs = [], []
    for r in rows:
        arr, wdt, cb = r if isinstance(r, tuple) else (r, r.shape[1], 0)
        specs.append(pl.BlockSpec((tl, wdt), lambda i, cb=cb: (rmap(i), cb)))
        arrs.append(arr)
    for p in params:
        specs.append(pl.BlockSpec(p.shape, lambda i, nd=p.ndim: (0,) * nd))
        arrs.append(p)
    out_shape = [SDS((length, c), dt) for c, dt in out_rows] + [SDS(s, F32) for s in out_accs]
    out_specs = [pl.BlockSpec((tl, c), lambda i: (rmap(i), 0)) for c, _ in out_rows]
    out_specs += [pl.BlockSpec(s, lambda i, nd=len(s): (0,) * nd) for s in out_accs]
    nr, npar, nor, noa = len(rows), len(params), len(out_rows), len(out_accs)

    def body(*refs):
        rin, pin = refs[:nr], refs[nr:nr + npar]
        rout = refs[nr + npar:nr + npar + nor]
        aout = refs[nr + npar + nor:nr + npar + nor + noa]
        scr = refs[nr + npar + nor + noa:]
        step = pl.program_id(0)
        outs_r, outs_a = fn(step, [r[...] for r in rin], [p[...] for p in pin], scr)
        for ref, val in zip(rout, outs_r, strict=True):
            ref[...] = val.astype(ref.dtype)

        @pl.when(step == 0)
        def _():
            for ref in aout:
                ref[...] = jnp.zeros_like(ref)

        for ref, val in zip(aout, outs_a, strict=True):
            ref[...] += val.astype(F32)

    res = pl.pallas_call(body, grid=(nt,), in_specs=specs, out_specs=out_specs, out_shape=out_shape,
                         scratch_shapes=list(scratch), compiler_params=_params(), name=name)(*arrs)
    return list(res)


def _rms(x, g):
    return x * lax.rsqrt(jnp.mean(x * x, axis=-1, keepdims=True) + NORM_EPS) * g


def _sig(x):
    return 0.5 * (jnp.tanh(0.5 * x) + 1.0)


def _softplus(x):
    return jnp.maximum(x, 0.0) + jnp.log(1.0 + jnp.exp(-jnp.abs(x)))


def _gelu(x):
    return x * (0.5 * (1.0 + jnp.tanh(math.sqrt(2.0 / math.pi) * (x + 0.044715 * (x * x * x)))))


def _bdot(a, b):
    return jnp.dot(a.astype(BF16), b.astype(BF16), preferred_element_type=F32)


def _hdot(a, b):
    return jnp.dot(a, b, precision=HI, preferred_element_type=F32)


def _block_ones(n, blk):
    i = lax.broadcasted_iota(jnp.int32, (n, n), 0) // blk
    j = lax.broadcasted_iota(jnp.int32, (n, n), 1) // blk
    return (i == j).astype(F32)


def _rms_fwd(x, g, name):
    return _rows(lambda s, r, p, _: ([_rms(r[0], p[0])], []), [x], [g], [(x.shape[1], BF16)], [], name=name, tl=512)[0]


def _rms_bwd(x, g, dh, dres, name):
    def fn(s, r, p, _):
        _, vjp = jax.vjp(_rms, r[0], p[0])
        dx, dg = vjp(r[1])
        return [dx + r[2]], [dg]
    return _rows(fn, [x, dh, dres], [g], [(x.shape[1], F32)], [g.shape], name=name, tl=256)


def _rwkv_pre_math(k_, lr, w0, a0, k_k, k_a, w2p, a2p, g2p, o512):
    pre_w = w0 + _bdot(jnp.tanh(lr), w2p)
    w = -_softplus(-pre_w) - 0.5
    decay = jnp.exp(-jnp.exp(w))
    a = _sig(a0 + _bdot(lr, a2p))
    g = _bdot(_sig(lr), g2p)
    kr = k_ * k_k
    kk = kr / jnp.maximum(jnp.sqrt(_hdot(kr * kr, o512)), 1e-12)
    k2 = k_ * (1.0 + (a - 1.0) * k_a)
    return decay, k2, -kk, kk * a, g


def _shift_down(p, prev_row):
    row = lax.broadcasted_iota(jnp.int32, p.shape, 0)
    return jnp.where(row == 0, jnp.broadcast_to(prev_row, p.shape), pltpu.roll(p, 1, 0))


def _shift_up(q, next_row):
    n = q.shape[0]
    row = lax.broadcasted_iota(jnp.int32, q.shape, 0)
    return jnp.where(row == n - 1, jnp.broadcast_to(next_row, q.shape), pltpu.roll(q, n - 1, 0))


def _rwkv_pre_fwd(p, mu, small, o512):
    def fn(step, r, prm, scr):
        car = scr[0]

        @pl.when(step == 0)
        def _():
            car[...] = jnp.zeros_like(car)

        x = r[0]
        prev = _shift_down(x, car[SUBLANES - 1:SUBLANES, :])
        car[...] = x[x.shape[0] - SUBLANES:, :]
        xs = x + (prev - x) * prm[0]
        decay, k2, aa, bb, g = _rwkv_pre_math(xs[:, RW:2 * RW], xs[:, 3 * RW:], *prm[1:])
        return [xs[:, :RW], decay, k2, xs[:, 2 * RW:3 * RW], aa, bb, g], []
    return _rows(fn, [p], [mu, *small, o512], [(RW, F32)] * 7, [], name="rwkv_pre_fwd", tl=256,
                 scratch=[pltpu.VMEM((SUBLANES, N_RWKV), F32)])


def _rwkv_pre_bwd(p, cots, mu, small, o512):
    length = p.shape[0]
    tl = min(256, length)
    nt = length // tl
    rows_per = tl // SUBLANES
    params = [mu, *small, o512]
    acc_shapes = [mu.shape] + [q.shape for q in small]
    nr, npar, nacc = 2 + len(cots), len(params), len(acc_shapes)

    def body(*refs):
        rin, pin = refs[:nr], refs[nr:nr + npar]
        dp_ref = refs[nr + npar]
        aout = refs[nr + npar + 1:nr + npar + 1 + nacc]
        car_q = refs[nr + npar + 1 + nacc]
        step = pl.program_id(0)

        @pl.when(step == 0)
        def _():
            car_q[...] = jnp.zeros_like(car_q)
            for ref in aout:
                ref[...] = jnp.zeros_like(ref)

        x = rin[0][...]
        prev_row = jnp.where(step == nt - 1, 0.0, rin[1][SUBLANES - 1:SUBLANES, :])
        dr, ddecay, dk2, dv, daa, dbb, dg = [r[...] for r in rin[2:]]
        prm = [q[...] for q in pin]
        mu_, o512_ = prm[0], prm[-1]
        prev = _shift_down(x, prev_row)
        xs = x + (prev - x) * mu_
        _, vjp = jax.vjp(lambda k_, lr, *w: _rwkv_pre_math(k_, lr, *w, o512_), xs[:, RW:2 * RW], xs[:, 3 * RW:], *prm[1:-1])
        dk_, dlr, *dsmall = vjp((ddecay, dk2, daa, dbb, dg))
        dxs = jnp.concatenate([dr, dk_, dv, dlr], axis=1)
        q = dxs * mu_
        dp_ref[...] = dxs - q + _shift_up(q, car_q[0:1, :])
        car_q[...] = q[:SUBLANES, :]
        aout[0][...] += jnp.sum((prev - x) * dxs, axis=0, keepdims=True)
        for ref, val in zip(aout[1:], dsmall, strict=True):
            ref[...] += val

    rmap = lambda i: nt - 1 - i
    specs = [pl.BlockSpec((tl, N_RWKV), lambda i: (rmap(i), 0)),
             pl.BlockSpec((SUBLANES, N_RWKV), lambda i: (jnp.maximum(rmap(i) * rows_per - 1, 0), 0))]
    specs += [pl.BlockSpec((tl, RW), lambda i: (rmap(i), 0)) for _ in cots]
    specs += [pl.BlockSpec(q.shape, lambda i, nd=q.ndim: (0,) * nd) for q in params]
    out_shape = [SDS((length, N_RWKV), F32)] + [SDS(sh, F32) for sh in acc_shapes]
    out_specs = [pl.BlockSpec((tl, N_RWKV), lambda i: (rmap(i), 0))]
    out_specs += [pl.BlockSpec(sh, lambda i, nd=len(sh): (0,) * nd) for sh in acc_shapes]
    res = pl.pallas_call(body, grid=(nt,), in_specs=specs, out_specs=out_specs, out_shape=out_shape,
                         scratch_shapes=[pltpu.VMEM((SUBLANES, N_RWKV), F32)],
                         compiler_params=_params(), name="rwkv_pre_bwd")(p, p, *cots, *params)
    return list(res)


def _rwkv_post_math(y, r, k2, v, g, lnx_w, lnx_b, r_k, o512):
    mean = _hdot(y, o512) * (1.0 / HN)
    yc = y - mean
    var = _hdot(yc * yc, o512) * (1.0 / HN)
    yn = yc * lax.rsqrt(var + LNX_EPS) * lnx_w + lnx_b
    bonus = _hdot(r * k2 * r_k, o512) * v
    return (yn + bonus) * g


def _rwkv_post_fwd(y, r, k2, v, g, prm, o512):
    return _rows(lambda s, rr, p, _: ([_rwkv_post_math(*rr, *p)], []), [y, r, k2, v, g], [*prm, o512],
                 [(RW, BF16)], [], name="rwkv_post_fwd", tl=256)[0]


def _rwkv_post_bwd(y, r, k2, v, g, dout, prm, o512):
    def fn(s, rr, p, _):
        o = p[-1]
        _, vjp = jax.vjp(lambda *a: _rwkv_post_math(*a, o), *rr[:5], *p[:-1])
        gr = vjp(rr[5])
        return list(gr[:5]), list(gr[5:])
    return _rows(fn, [y, r, k2, v, g, dout], [*prm, o512], [(RW, F32)] * 5, [q.shape for q in prm],
                 name="rwkv_post_bwd", tl=256)


def _glu_math(ys, t, b):
    return ys * _sig(t + b)


def _glu_fwd(ys, t, b):
    return _rows(lambda s, r, p, _: ([_glu_math(r[0], r[1], p[0])], []), [ys, t], [b], [(S5W, BF16)], [],
                 name="s5_glu_fwd", tl=512)[0]


def _glu_bwd(ys, t, dout, b):
    def fn(s, r, p, _):
        _, vjp = jax.vjp(_glu_math, r[0], r[1], p[0])
        dys, dt, db = vjp(r[2])
        return [dys, dt], [db]
    return _rows(fn, [ys, t, dout], [b], [(S5W, F32)] * 2, [b.shape], name="s5_glu_bwd", tl=512)


def _merge_math(gp_r, gp_s, o_r, o_s, b_r, b_s):
    return _sig(gp_r + b_r) * o_r + _sig(gp_s + b_s) * o_s


def _merge_fwd(gpre, o_r, o_s, b_r, b_s):
    return _rows(lambda s, r, p, _: ([_merge_math(*r, *p)], []),
                 [(gpre, D_MODEL, 0), (gpre, D_MODEL, 1), o_r, o_s], [b_r, b_s], [(D_MODEL, BF16)], [],
                 name="merge_fwd", tl=256)[0]


def _merge_bwd(gpre, o_r, o_s, dmi, b_r, b_s):
    def fn(s, r, p, _):
        _, vjp = jax.vjp(_merge_math, *r[:4], *p)
        dgr, dgs, dor, dos, dbr, dbs = vjp(r[4])
        return [dgr, dgs, dor, dos], [dbr, dbs]
    return _rows(fn, [(gpre, D_MODEL, 0), (gpre, D_MODEL, 1), o_r, o_s, dmi], [b_r, b_s], [(D_MODEL, F32)] * 4,
                 [b_r.shape, b_s.shape], name="merge_bwd", tl=256)


def _mid_fwd(x, mixed, g_post, g_pre):
    def fn(s, r, p, _):
        x2 = r[0] + _rms(r[1], p[0])
        return [x2, _rms(x2, p[1])], []
    return _rows(fn, [x, mixed], [g_post, g_pre], [(D_MODEL, F32), (D_MODEL, BF16)], [], name="mid_fwd", tl=256)


def _mid_bwd(x2, mixed, dh2, dx3, g_post, g_pre):
    def fn(s, r, p, _):
        _, vjp1 = jax.vjp(_rms, r[0], p[1])
        dx2, dg_pre = vjp1(r[2])
        dx2 = dx2 + r[3]
        _, vjp2 = jax.vjp(_rms, r[1], p[0])
        dmixed, dg_post = vjp2(dx2)
        return [dx2, dmixed], [dg_post, dg_pre]
    return _rows(fn, [x2, mixed, dh2, dx3], [g_post, g_pre], [(D_MODEL, F32)] * 2, [g_post.shape, g_pre.shape],
                 name="mid_bwd", tl=256)


def _final(x2, f, tgt, g_post):
    def fn(s, r, p, _):
        y, vjp = jax.vjp(_rms, r[1], p[0])
        diff = r[0] + y - r[2]
        dx3 = diff * (1.0 / D_MODEL)
        df, dg = vjp(dx3)
        part = 0.5 * jnp.sum(jnp.sum(diff * diff, axis=1, keepdims=True), axis=0, keepdims=True) * (1.0 / D_MODEL)
        return [dx3, df], [dg, jnp.broadcast_to(part, (1, LANES))]
    return _rows(fn, [x2, f, tgt], [g_post], [(D_MODEL, F32)] * 2, [g_post.shape, (1, LANES)], name="final", tl=256)


def _conv_taps(z, car):
    row = lax.broadcasted_iota(jnp.int32, z.shape, 0)
    z1 = jnp.where(row == 0, jnp.broadcast_to(car[7:8, :], z.shape), pltpu.roll(z, 1, 0))
    z2 = pltpu.roll(z, 2, 0)
    z2 = jnp.where(row == 0, jnp.broadcast_to(car[6:7, :], z.shape), z2)
    z2 = jnp.where(row == 1, jnp.broadcast_to(car[7:8, :], z.shape), z2)
    return z1, z2


def _conv(z, car, w, b):
    z1, z2 = _conv_taps(z, car)
    return b + w[0:1, :] * z2 + w[1:2, :] * z1 + w[2:3, :] * z, z1, z2


def _conv_fwd(z, conv_w, conv_b):
    length = z.shape[0]
    tl = min(256, length)
    nt = length // tl
    tc = _pick(D_FF, 1536)
    nb = D_FF // tc

    def body(zg_ref, zv_ref, wg_ref, wv_ref, bg_ref, bv_ref, o_ref, cg, cv):
        @pl.when(pl.program_id(1) == 0)
        def _():
            cg[...] = jnp.zeros_like(cg)
            cv[...] = jnp.zeros_like(cv)

        zg, zv = zg_ref[...], zv_ref[...]
        gate, _, _ = _conv(zg, cg[...], wg_ref[...], bg_ref[...])
        val, _, _ = _conv(zv, cv[...], wv_ref[...], bv_ref[...])
        cg[...] = zg[tl - SUBLANES:, :]
        cv[...] = zv[tl - SUBLANES:, :]
        o_ref[...] = (_gelu(gate) * val).astype(o_ref.dtype)

    zspec = lambda off: pl.BlockSpec((tl, tc), lambda j, i: (i, j + off))
    wspec = lambda off, r: pl.BlockSpec((r, tc), lambda j, i: (0, j + off))
    return pl.pallas_call(
        body, grid=(nb, nt),
        in_specs=[zspec(0), zspec(nb), wspec(0, 3), wspec(nb, 3), wspec(0, 1), wspec(nb, 1)],
        out_specs=pl.BlockSpec((tl, tc), lambda j, i: (i, j)), out_shape=SDS((length, D_FF), BF16),
        scratch_shapes=[pltpu.VMEM((SUBLANES, tc), F32)] * 2,
        compiler_params=_params(("arbitrary", "arbitrary")), name="conv_fwd")(z, z, conv_w, conv_w, conv_b, conv_b)


def _conv_bwd(z, dact, conv_w, conv_b):
    length = z.shape[0]
    tl = min(256, length)
    nt = length // tl
    tc = _pick(D_FF, 1536)
    nb = D_FF // tc
    rows_per = tl // SUBLANES

    def half_bwd(dzc, z, z1, z2, w, dcar):
        n = tl
        row = lax.broadcasted_iota(jnp.int32, dzc.shape, 0)
        u1 = jnp.where(row == n - 1, jnp.broadcast_to(dcar[0:1, :], dzc.shape), pltpu.roll(dzc, n - 1, 0))
        u2 = pltpu.roll(dzc, n - 2, 0)
        u2 = jnp.where(row == n - 2, jnp.broadcast_to(dcar[0:1, :], dzc.shape), u2)
        u2 = jnp.where(row == n - 1, jnp.broadcast_to(dcar[1:2, :], dzc.shape), u2)
        dz = w[2:3, :] * dzc + w[1:2, :] * u1 + w[0:1, :] * u2
        dw = jnp.concatenate([jnp.sum(dzc * z2, axis=0, keepdims=True), jnp.sum(dzc * z1, axis=0, keepdims=True),
                              jnp.sum(dzc * z, axis=0, keepdims=True)], axis=0)
        return dz, dw, jnp.sum(dzc, axis=0, keepdims=True)

    def body(zg_ref, zv_ref, pg_ref, pv_ref, da_ref, wg_ref, wv_ref, bg_ref, bv_ref,
             dzg_ref, dzv_ref, dwg_ref, dwv_ref, dbg_ref, dbv_ref, cg, cv):
        step = pl.program_id(1)

        @pl.when(step == 0)
        def _():
            cg[...] = jnp.zeros_like(cg)
            cv[...] = jnp.zeros_like(cv)
            for ref in (dwg_ref, dwv_ref, dbg_ref, dbv_ref):
                ref[...] = jnp.zeros_like(ref)

        is_first_tile = step == nt - 1
        zg, zv = zg_ref[...], zv_ref[...]
        pg = jnp.where(is_first_tile, 0.0, pg_ref[...])
        pv = jnp.where(is_first_tile, 0.0, pv_ref[...])
        wg, wv = wg_ref[...], wv_ref[...]
        gate, zg1, zg2 = _conv(zg, pg, wg, bg_ref[...])
        val, zv1, zv2 = _conv(zv, pv, wv, bv_ref[...])
        act_g, vjp = jax.vjp(_gelu, gate)
        da = da_ref[...]
        dgate = vjp(da * val)[0]
        dval = da * act_g
        dzg, dwg, dbg = half_bwd(dgate, zg, zg1, zg2, wg, cg[...])
        dzv, dwv, dbv = half_bwd(dval, zv, zv1, zv2, wv, cv[...])
        cg[...] = dgate[:SUBLANES, :]
        cv[...] = dval[:SUBLANES, :]
        dzg_ref[...] = dzg
        dzv_ref[...] = dzv
        dwg_ref[...] += dwg
        dwv_ref[...] += dwv
        dbg_ref[...] += dbg
        dbv_ref[...] += dbv

    rmap = lambda i: nt - 1 - i
    zspec = lambda off: pl.BlockSpec((tl, tc), lambda j, i: (rmap(i), j + off))
    pspec = lambda off: pl.BlockSpec((SUBLANES, tc), lambda j, i: (jnp.maximum(rmap(i) * rows_per - 1, 0), j + off))
    wspec = lambda off, r: pl.BlockSpec((r, tc), lambda j, i: (0, j + off))
    out_w = lambda r: pl.BlockSpec((r, tc), lambda j, i: (0, j))
    dzg, dzv, dwg, dwv, dbg, dbv = pl.pallas_call(
        body, grid=(nb, nt),
        in_specs=[zspec(0), zspec(nb), pspec(0), pspec(nb), pl.BlockSpec((tl, tc), lambda j, i: (rmap(i), j)),
                  wspec(0, 3), wspec(nb, 3), wspec(0, 1), wspec(nb, 1)],
        out_specs=[pl.BlockSpec((tl, tc), lambda j, i: (rmap(i), j))] * 2 + [out_w(3), out_w(3), out_w(1), out_w(1)],
        out_shape=[SDS((length, D_FF), F32)] * 2 + [SDS((3, D_FF), F32)] * 2 + [SDS((1, D_FF), F32)] * 2,
        scratch_shapes=[pltpu.VMEM((SUBLANES, tc), F32)] * 2,
        compiler_params=_params(("arbitrary", "arbitrary")), name="conv_bwd")(z, z, z, z, dact, conv_w, conv_w, conv_b, conv_b)
    return dzg, dzv, jnp.concatenate([dwg, dwv], axis=1), jnp.concatenate([dbg, dbv], axis=1)


def _s5_prep_math(a_re, a_im, ls, b_re, b_im):
    dt = jnp.exp(ls)
    er = jnp.exp(a_re * dt)
    ph = a_im * dt
    abr, abi = er * jnp.cos(ph), er * jnp.sin(ph)
    den = a_re * a_re + a_im * a_im
    nr = abr - 1.0
    cr = (nr * a_re + abi * a_im) / den
    ci = (abi * a_re - nr * a_im) / den
    return abr, abi, cr * b_re - ci * b_im, cr * b_im + ci * b_re


def _s5_prep_fwd(a_re, a_im, ls, b_re, b_im):
    def body(ar, ai, l, br, bi, o1, o2, o3, o4):
        for ref, val in zip((o1, o2, o3, o4), _s5_prep_math(ar[...], ai[...], l[...], br[...], bi[...]), strict=True):
            ref[...] = val
    return pl.pallas_call(body, out_shape=[SDS((1, S5N), F32)] * 2 + [SDS((S5C, S5N), F32)] * 2,
                          name="s5_prep_fwd")(a_re, a_im, ls, b_re, b_im)


def _s5_prep_bwd(a_re, a_im, ls, b_re, b_im, cots, gsel):
    def body(ar, ai, l, br, bi, c1, c2, c3, c4, g_ref, o1, o2, o3, o4, o5):
        _, vjp = jax.vjp(_s5_prep_math, ar[...], ai[...], l[...], br[...], bi[...])
        d_ar, d_ai, d_ls, d_br, d_bi = vjp((c1[...], c2[...], c3[...], c4[...]))
        o1[...] = d_ar
        o2[...] = d_ai
        o3[...] = _hdot(jnp.broadcast_to(d_ls, (SUBLANES, S5N)), g_ref[...])
        o4[...] = d_br
        o5[...] = d_bi
    return pl.pallas_call(body, out_shape=[SDS((1, S5N), F32)] * 2 + [SDS((SUBLANES, LANES), F32)] + [SDS((S5C, S5N), F32)] * 2,
                          name="s5_prep_bwd")(a_re, a_im, ls, b_re, b_im, *cots, gsel)


def _cmul(ar, ai, br, bi):
    return ar * br - ai * bi, ar * bi + ai * br


def _s5_powers(abr, abi):
    shp = (SUBLANES, S5N)
    a1 = (jnp.broadcast_to(abr, shp), jnp.broadcast_to(abi, shp))
    a2 = _cmul(*a1, *a1)
    a4 = _cmul(*a2, *a2)
    row = lax.broadcasted_iota(jnp.int32, shp, 0)
    pr, pi = a1
    cur = a1
    for i in range(1, SUBLANES):
        cur = _cmul(*cur, *a1)
        pr = jnp.where(row == i, cur[0], pr)
        pi = jnp.where(row == i, cur[1], pi)
    return a1, a2, a4, (pr, pi)


def _s5_scan(sre_ref, sim_ref, car_re, car_im, abr, abi, n_rows, reverse):
    a1, a2, a4, (pr, pi) = _s5_powers(abr, abi)
    sgn = -1.0 if reverse else 1.0
    row = lax.broadcasted_iota(jnp.int32, (SUBLANES, S5N), 0)
    if reverse:
        qr, qi = pr, pi
        for i in range(SUBLANES):
            src = SUBLANES - 1 - i
            qr = jnp.where(row == i, jnp.broadcast_to(pr[src:src + 1, :], pr.shape), qr)
            qi = jnp.where(row == i, jnp.broadcast_to(pi[src:src + 1, :], pi.shape), qi)
        pr, pi = qr, qi
    nblk = n_rows // SUBLANES

    def blk(i, carry):
        cr, ci = carry
        b = (nblk - 1 - i) if reverse else i
        sl = pl.ds(pl.multiple_of(b * SUBLANES, SUBLANES), SUBLANES)
        xr, xi = sre_ref[sl, :], sim_ref[sl, :]
        for kk, (er, ei) in ((1, a1), (2, a2), (4, a4)):
            if reverse:
                sr = jnp.where(row < SUBLANES - kk, pltpu.roll(xr, SUBLANES - kk, 0), 0.0)
                si = jnp.where(row < SUBLANES - kk, pltpu.roll(xi, SUBLANES - kk, 0), 0.0)
            else:
                sr = jnp.where(row >= kk, pltpu.roll(xr, kk, 0), 0.0)
                si = jnp.where(row >= kk, pltpu.roll(xi, kk, 0), 0.0)
            dr, di = _cmul(er, sgn * ei, sr, si)
            xr, xi = xr + dr, xi + di
        dr, di = _cmul(pr, sgn * pi, cr, ci)
        xr, xi = xr + dr, xi + di
        sre_ref[sl, :] = xr
        sim_ref[sl, :] = xi
        edge = 0 if reverse else SUBLANES - 1
        return (jnp.broadcast_to(xr[edge:edge + 1, :], xr.shape), jnp.broadcast_to(xi[edge:edge + 1, :], xi.shape))

    cr, ci = lax.fori_loop(0, nblk, blk, (car_re[...], car_im[...]))
    car_re[...] = cr
    car_im[...] = ci


S5_BLK = 4
S5_BN = S5N // S5_BLK


def _s5_project_in(u, bblk_ref, sre_ref, sim_ref):
    for j in range(S5_BLK):
        bu = _hdot(u[:, j * LANES:(j + 1) * LANES], bblk_ref[j])
        sre_ref[:, j * S5_BN:(j + 1) * S5_BN] = bu[:, :S5_BN]
        sim_ref[:, j * S5_BN:(j + 1) * S5_BN] = bu[:, S5_BN:]


def _s5_project_out(u, d, cblk_ref, sre_ref, sim_ref):
    ys = []
    for j in range(S5_BLK):
        sl = slice(j * S5_BN, (j + 1) * S5_BN)
        ys.append(_hdot(sre_ref[:, sl], cblk_ref[j, :S5_BN, :]) + _hdot(sim_ref[:, sl], cblk_ref[j, S5_BN:, :]))
    return jnp.concatenate(ys, axis=1) + d * u


def _s5_fwd(u, bblk, cblk, abr, abi, d):
    length = u.shape[0]
    tl = min(256, length)
    nt = length // tl

    def body(u_ref, b_ref, c_ref, ar_ref, ai_ref, d_ref, ys_ref, kr_ref, ki_ref, sre, sim, car_re, car_im):
        @pl.when(pl.program_id(0) == 0)
        def _():
            car_re[...] = jnp.zeros_like(car_re)
            car_im[...] = jnp.zeros_like(car_im)

        kr_ref[0] = car_re[...]
        ki_ref[0] = car_im[...]
        u_ = u_ref[...]
        _s5_project_in(u_, b_ref, sre, sim)
        _s5_scan(sre, sim, car_re, car_im, ar_ref[...], ai_ref[...], tl, False)
        ys_ref[...] = _gelu(_s5_project_out(u_, d_ref[...], c_ref, sre, sim))

    full = lambda a: pl.BlockSpec(a.shape, lambda i, nd=a.ndim: (0,) * nd)
    chk = pl.BlockSpec((1, SUBLANES, S5N), lambda i: (i, 0, 0))
    return pl.pallas_call(
        body, grid=(nt,), in_specs=[pl.BlockSpec((tl, S5W), lambda i: (i, 0)), full(bblk), full(cblk), full(abr), full(abi), full(d)],
        out_specs=[pl.BlockSpec((tl, S5W), lambda i: (i, 0)), chk, chk],
        out_shape=[SDS((length, S5W), F32), SDS((nt, SUBLANES, S5N), F32), SDS((nt, SUBLANES, S5N), F32)],
        scratch_shapes=[pltpu.VMEM((tl, S5N), F32)] * 2 + [pltpu.VMEM((SUBLANES, S5N), F32)] * 2,
        compiler_params=_params(), name="s5_fwd")(u, bblk, cblk, abr, abi, d)


def _s5_bwd(u, dys_a, dys_b, kre, kim, bblk, cblk, abr, abi, d):
    length = u.shape[0]
    tl = min(256, length)
    nt = length // tl
    tn_dims = (((0,), (0,)), ((), ()))
    nt_dims = (((1,), (1,)), ((), ()))

    def body(u_ref, da_ref, db_ref, kr_ref, ki_ref, b_ref, c_ref, ar_ref, ai_ref, d_ref,
             du_ref, dB_ref, dC_ref, dar_ref, dai_ref, dd_ref, sre, sim, gre, gim, car_re, car_im, dcar_re, dcar_im):
        @pl.when(pl.program_id(0) == 0)
        def _():
            dcar_re[...] = jnp.zeros_like(dcar_re)
            dcar_im[...] = jnp.zeros_like(dcar_im)
            for ref in (dB_ref, dC_ref, dar_ref, dai_ref, dd_ref):
                ref[...] = jnp.zeros_like(ref)

        u_ = u_ref[...]
        abr_, abi_, d_ = ar_ref[...], ai_ref[...], d_ref[...]
        car_re[...] = kr_ref[0]
        car_im[...] = ki_ref[0]
        _s5_project_in(u_, b_ref, sre, sim)
        _s5_scan(sre, sim, car_re, car_im, abr_, abi_, tl, False)
        y = _s5_project_out(u_, d_, c_ref, sre, sim)
        _, vjp = jax.vjp(_gelu, y)
        dy = vjp(da_ref[...] + db_ref[...])[0]
        dd_ref[...] += jnp.sum(dy * u_, axis=0, keepdims=True)
        for j in range(S5_BLK):
            sl = slice(j * S5_BN, (j + 1) * S5_BN)
            dyj = dy[:, j * LANES:(j + 1) * LANES]
            gre[:, sl] = lax.dot_general(dyj, c_ref[j, :S5_BN, :], nt_dims, precision=HI, preferred_element_type=F32)
            gim[:, sl] = lax.dot_general(dyj, c_ref[j, S5_BN:, :], nt_dims, precision=HI, preferred_element_type=F32)
            dC_ref[j, :S5_BN, :] += lax.dot_general(sre[:, sl], dyj, tn_dims, precision=HI, preferred_element_type=F32)
            dC_ref[j, S5_BN:, :] += lax.dot_general(sim[:, sl], dyj, tn_dims, precision=HI, preferred_element_type=F32)
        _s5_scan(gre, gim, dcar_re, dcar_im, abr_, abi_, tl, True)
        gr, gi = gre[...], gim[...]
        pr = _shift_down(sre[...], kr_ref[0, 0:1, :])
        pi = _shift_down(sim[...], ki_ref[0, 0:1, :])
        dar_ref[...] += jnp.sum(gr * pr + gi * pi, axis=0, keepdims=True)
        dai_ref[...] += jnp.sum(gi * pr - gr * pi, axis=0, keepdims=True)
        dus = []
        for j in range(S5_BLK):
            sl = slice(j * S5_BN, (j + 1) * S5_BN)
            uj = u_[:, j * LANES:(j + 1) * LANES]
            dus.append(lax.dot_general(gre[:, sl], b_ref[j, :, :S5_BN], nt_dims, precision=HI, preferred_element_type=F32)
                       + lax.dot_general(gim[:, sl], b_ref[j, :, S5_BN:], nt_dims, precision=HI, preferred_element_type=F32))
            dB_ref[j, :, :S5_BN] += lax.dot_general(uj, gre[:, sl], tn_dims, precision=HI, preferred_element_type=F32)
            dB_ref[j, :, S5_BN:] += lax.dot_general(uj, gim[:, sl], tn_dims, precision=HI, preferred_element_type=F32)
        du_ref[...] = jnp.concatenate(dus, axis=1) + d_ * dy

    rmap = lambda i: nt - 1 - i
    full = lambda a: pl.BlockSpec(a.shape, lambda i, nd=a.ndim: (0,) * nd)
    row = pl.BlockSpec((tl, S5W), lambda i: (rmap(i), 0))
    chk = pl.BlockSpec((1, SUBLANES, S5N), lambda i: (rmap(i), 0, 0))
    return pl.pallas_call(
        body, grid=(nt,), in_specs=[row, row, row, chk, chk, full(bblk), full(cblk), full(abr), full(abi), full(d)],
        out_specs=[row, full(bblk), full(cblk), full(abr), full(abi), full(d)],
        out_shape=[SDS((length, S5W), F32), SDS(bblk.shape, F32), SDS(cblk.shape, F32), SDS(abr.shape, F32),
                   SDS(abi.shape, F32), SDS(d.shape, F32)],
        scratch_shapes=[pltpu.VMEM((tl, S5N), F32)] * 4 + [pltpu.VMEM((SUBLANES, S5N), F32)] * 4,
        compiler_params=_params(), name="s5_bwd")(u, dys_a, dys_b, kre, kim, bblk, cblk, abr, abi, d)


def _s5_blockdiag_in(bb):
    t = bb.reshape(S5C, S5_BLK, S5_BN).transpose(1, 0, 2)
    t = jnp.tile(t, (1, SUBLANES, 1))
    mask = (lax.broadcasted_iota(jnp.int32, (LANES, S5_BN), 0) // S5C) == (lax.broadcasted_iota(jnp.int32, (LANES, S5_BN), 1) // S5P)
    return jnp.where(mask[None], t, 0.0)


def _s5_blockdiag_in_t(dblk):
    t = dblk.reshape(S5_BLK, SUBLANES, S5C, SUBLANES, S5P)
    t = jnp.diagonal(t, axis1=1, axis2=3)
    return t.transpose(1, 0, 3, 2).reshape(S5C, S5N)


def _s5_blockdiag_out(c):
    t = c.reshape(S5_BLK, SUBLANES, S5C, S5P).transpose(0, 3, 1, 2).reshape(S5_BLK, S5P, LANES)
    t = jnp.tile(t, (1, SUBLANES, 1))
    mask = (lax.broadcasted_iota(jnp.int32, (S5_BN, LANES), 0) // S5P) == (lax.broadcasted_iota(jnp.int32, (S5_BN, LANES), 1) // S5C)
    return jnp.where(mask[None], t, 0.0)


def _s5_blockdiag_out_t(dblk):
    t = dblk.reshape(S5_BLK, SUBLANES, S5P, SUBLANES, S5C)
    t = jnp.diagonal(t, axis1=1, axis2=3)
    return t.transpose(0, 3, 2, 1).reshape(S5G, S5C, S5P)


def _wkv_consts():
    lane = lax.broadcasted_iota(jnp.int32, (4 * HN, LANES), 1)
    row = lax.broadcasted_iota(jnp.int32, (4 * HN, LANES), 0)
    diag = ((lane % HN) == (row % HN)).astype(F32)
    ones = _block_ones(LANES, HN).astype(BF16)
    return lane % HN, diag, ones


def _wkv_rowseg(p, ones):
    hi = p.astype(BF16)
    lo = (p - hi.astype(F32)).astype(BF16)
    return jnp.dot(hi, ones, preferred_element_type=F32) + jnp.dot(lo, ones, preferred_element_type=F32)


def _wkv_full(qb_ref, t):
    q = qb_ref[t]
    return jnp.concatenate([jnp.tile(q[:, hp * LANES:(hp + 1) * LANES], (SUBLANES, 1)) for hp in range(4)], axis=0)


def _wkv_colsum8(x):
    return jnp.concatenate([x[hp * HN:(hp + 1) * HN].reshape(SUBLANES, SUBLANES, LANES).sum(axis=0) for hp in range(4)], axis=1)


def _wkv_prebroadcast(src_refs, dst_refs, n):
    for s, dref in zip(src_refs, dst_refs, strict=True):
        for tt in range(n):
            dref[tt] = jnp.broadcast_to(s[tt:tt + 1, :], (SUBLANES, RW))


def _wkv_fwd(r, w, k, v, a, b):
    length = r.shape[0]
    c = min(WKV_CHUNK, length)
    nc = length // c

    def body(r_ref, w_ref, k_ref, v_ref, a_ref, b_ref, yt_ref, chk_ref, s_ref, y_ref, rb, wb, kb, vb, ab, bb):
        @pl.when(pl.program_id(0) == 0)
        def _():
            s_ref[...] = jnp.zeros_like(s_ref)

        chk_ref[0] = s_ref[...]
        _wkv_prebroadcast((r_ref, w_ref, k_ref, v_ref, a_ref, b_ref), (rb, wb, kb, vb, ab, bb), c)
        y_ref[...] = jnp.zeros_like(y_ref)
        lane_t, diag, ones = _wkv_consts()

        def step(t, carry):
            s = s_ref[...]
            sa = _wkv_rowseg(s * _wkv_full(ab, t), ones)
            vv = _wkv_rowseg(_wkv_full(vb, t) * diag, ones)
            s = s * _wkv_full(wb, t) + sa * _wkv_full(bb, t) + vv * _wkv_full(kb, t)
            s_ref[...] = s
            yb = _wkv_rowseg(s * _wkv_full(rb, t), ones)
            y_ref[...] = jnp.where(lane_t == t, yb, y_ref[...])
            return carry

        lax.fori_loop(0, c, step, 0)
        yt_ref[0] = y_ref[...]

    row = pl.BlockSpec((c, RW), lambda i: (i, 0))
    st = pl.BlockSpec((1, 4 * HN, LANES), lambda i: (i, 0, 0))
    return pl.pallas_call(
        body, grid=(nc,), in_specs=[row] * 6, out_specs=[st, st],
        out_shape=[SDS((nc, 4 * HN, LANES), F32)] * 2,
        scratch_shapes=[pltpu.VMEM((4 * HN, LANES), F32)] * 2 + [pltpu.VMEM((c, SUBLANES, RW), F32)] * 6,
        compiler_params=_params(), name="wkv_fwd")(r, w, k, v, a, b)


def _wkv_bwd(r, w, k, v, a, b, dy, chk):
    length = r.shape[0]
    c = min(WKV_CHUNK, length)
    nc = length // c

    def body(r_ref, w_ref, k_ref, v_ref, a_ref, b_ref, dy_ref, chk_ref,
             dr_ref, dw_ref, dk_ref, da_ref, db_ref, dvt_ref,
             ds_ref, cur_ref, dv_ref, sst, sast, vst, rb, wb, kb, vb, ab, bb, dyb, p_r, p_w, p_k, p_a, p_b):
        @pl.when(pl.program_id(0) == 0)
        def _():
            ds_ref[...] = jnp.zeros_like(ds_ref)

        _wkv_prebroadcast((r_ref, w_ref, k_ref, v_ref, a_ref, b_ref, dy_ref), (rb, wb, kb, vb, ab, bb, dyb), c)
        lane_t, diag, ones = _wkv_consts()
        cur_ref[...] = chk_ref[0]
        dv_ref[...] = jnp.zeros_like(dv_ref)

        def fwd(t, carry):
            s = cur_ref[...]
            sst[t] = s
            sa = _wkv_rowseg(s * _wkv_full(ab, t), ones)
            vv = _wkv_rowseg(_wkv_full(vb, t) * diag, ones)
            sast[t] = sa
            vst[t] = vv
            cur_ref[...] = s * _wkv_full(wb, t) + sa * _wkv_full(bb, t) + vv * _wkv_full(kb, t)
            return carry

        lax.fori_loop(0, c, fwd, 0)

        def bwd(i, carry):
            t = c - 1 - i
            s_t, s_prev, sa, vv = cur_ref[...], sst[t], sast[t], vst[t]
            dyy = _wkv_rowseg(_wkv_full(dyb, t) * diag, ones)
            ds = ds_ref[...] + dyy * _wkv_full(rb, t)
            p_r[t] = _wkv_colsum8(s_t * dyy)
            p_k[t] = _wkv_colsum8(ds * vv)
            dvb = _wkv_rowseg(ds * _wkv_full(kb, t), ones)
            dv_ref[...] = jnp.where(lane_t == t, dvb, dv_ref[...])
            p_b[t] = _wkv_colsum8(ds * sa)
            dsa = _wkv_rowseg(ds * _wkv_full(bb, t), ones)
            p_w[t] = _wkv_colsum8(ds * s_prev)
            p_a[t] = _wkv_colsum8(s_prev * dsa)
            ds_ref[...] = ds * _wkv_full(wb, t) + dsa * _wkv_full(ab, t)
            cur_ref[...] = s_prev
            return carry

        lax.fori_loop(0, c, bwd, 0)
        sel = (lax.broadcasted_iota(jnp.int32, (c, c * SUBLANES), 1) // SUBLANES
               == lax.broadcasted_iota(jnp.int32, (c, c * SUBLANES), 0)).astype(F32)
        for out_ref, part in ((dr_ref, p_r), (dw_ref, p_w), (dk_ref, p_k), (da_ref, p_a), (db_ref, p_b)):
            out_ref[...] = _hdot(sel, part[...].reshape(c * SUBLANES, RW))
        dvt_ref[0] = dv_ref[...]

    rmap = lambda i: nc - 1 - i
    row = pl.BlockSpec((c, RW), lambda i: (rmap(i), 0))
    st = pl.BlockSpec((1, 4 * HN, LANES), lambda i: (rmap(i), 0, 0))
    state = pltpu.VMEM((4 * HN, LANES), F32)
    return pl.pallas_call(
        body, grid=(nc,), in_specs=[row] * 7 + [st], out_specs=[row] * 5 + [st],
        out_shape=[SDS((length, RW), F32)] * 5 + [SDS((nc, 4 * HN, LANES), F32)],
        scratch_shapes=[state] * 3 + [pltpu.VMEM((c, 4 * HN, LANES), F32)] * 3 + [pltpu.VMEM((c, SUBLANES, RW), F32)] * 12,
        compiler_params=_params(), name="wkv_bwd")(r, w, k, v, a, b, dy, chk)


def _wkv_untranspose(yt, c):
    nc = yt.shape[0]
    t = yt.reshape(nc, 4, HN, 2, HN)[..., :c]
    return t.transpose(0, 4, 1, 3, 2).reshape(nc * c, RW)


def _my_place():
    return lax.axis_index("x"), lax.axis_index("y"), lax.axis_index("c")


def _peer(x, y, c, k):
    return (x ^ ((k >> 2) & 1), y ^ ((k >> 1) & 1), c ^ (k & 1))


def _all_gather(shard):
    rows = shard.shape[0]

    def body(in_ref, out_ref, send_sems, recv_sems, local_sem):
        x, y, c = _my_place()
        me = 4 * x + 2 * y + c
        mine = pltpu.make_async_copy(in_ref, out_ref.at[me], local_sem)
        mine.start()
        sends = []
        for k in range(1, N_DEV):
            cp = pltpu.make_async_remote_copy(src_ref=in_ref, dst_ref=out_ref.at[me], send_sem=send_sems.at[k - 1],
                                              recv_sem=recv_sems.at[k - 1], device_id=_peer(x, y, c, k), device_id_type=MESH)
            cp.start()
            sends.append(cp)
        for k in range(1, N_DEV):
            px, py, pc = _peer(x, y, c, k)
            pltpu.make_async_remote_copy(src_ref=in_ref, dst_ref=out_ref.at[4 * px + 2 * py + pc], send_sem=send_sems.at[k - 1],
                                         recv_sem=recv_sems.at[k - 1], device_id=(px, py, pc), device_id_type=MESH).wait_recv()
        for cp in sends:
            cp.wait_send()
        mine.wait()

    return pl.pallas_call(
        body, out_shape=SDS((N_DEV, rows, LANES), shard.dtype),
        in_specs=[pl.BlockSpec(memory_space=pl.ANY)], out_specs=pl.BlockSpec(memory_space=pl.ANY),
        scratch_shapes=[pltpu.SemaphoreType.DMA((N_DEV - 1,)), pltpu.SemaphoreType.DMA((N_DEV - 1,)), pltpu.SemaphoreType.DMA],
        name="all_gather_weights")(shard)


def _all_to_all(chunks):
    rows = chunks.shape[1]

    def body(in_ref, out_ref, send_sems, recv_sems, local_sem):
        x, y, c = _my_place()
        me = 4 * x + 2 * y + c
        mine = pltpu.make_async_copy(in_ref.at[me], out_ref.at[me], local_sem)
        mine.start()
        sends = []
        for k in range(1, N_DEV):
            px, py, pc = _peer(x, y, c, k)
            cp = pltpu.make_async_remote_copy(src_ref=in_ref.at[4 * px + 2 * py + pc], dst_ref=out_ref.at[me],
                                              send_sem=send_sems.at[k - 1], recv_sem=recv_sems.at[k - 1],
                                              device_id=(px, py, pc), device_id_type=MESH)
            cp.start()
            sends.append(cp)
        for k in range(1, N_DEV):
            px, py, pc = _peer(x, y, c, k)
            pltpu.make_async_remote_copy(src_ref=in_ref.at[me], dst_ref=out_ref.at[4 * px + 2 * py + pc], send_sem=send_sems.at[k - 1],
                                         recv_sem=recv_sems.at[k - 1], device_id=(px, py, pc), device_id_type=MESH).wait_recv()
        for cp in sends:
            cp.wait_send()
        mine.wait()

    return pl.pallas_call(
        body, out_shape=SDS((N_DEV, rows, LANES), chunks.dtype),
        in_specs=[pl.BlockSpec(memory_space=pl.ANY)], out_specs=pl.BlockSpec(memory_space=pl.ANY),
        scratch_shapes=[pltpu.SemaphoreType.DMA((N_DEV - 1,)), pltpu.SemaphoreType.DMA((N_DEV - 1,)), pltpu.SemaphoreType.DMA],
        name="exchange_grads")(chunks)


def _sum_adamw(parts, wgt, m, v):
    rows = wgt.shape[0]
    tr = PACK_ROWS

    def body(p_ref, w_ref, m_ref, v_ref, g_out, d_out, m_out, v_out):
        g = p_ref[0]
        for s in range(1, N_DEV):
            g = g + p_ref[s]
        m_new = ADAM_B1 * m_ref[...] + (1.0 - ADAM_B1) * g
        v_new = ADAM_B2 * v_ref[...] + (1.0 - ADAM_B2) * (g * g)
        m_hat = m_new / (1.0 - ADAM_B1 ** ADAM_STEP)
        v_hat = v_new / (1.0 - ADAM_B2 ** ADAM_STEP)
        g_out[...] = g
        d_out[...] = -ADAM_LR * (m_hat / (jnp.sqrt(v_hat) + ADAM_EPS) + ADAM_WD * w_ref[...])
        m_out[...] = m_new
        v_out[...] = v_new

    blk = pl.BlockSpec((tr, LANES), lambda i: (i, 0))
    return pl.pallas_call(
        body, grid=(rows // tr,), in_specs=[pl.BlockSpec((N_DEV, tr, LANES), lambda i: (0, i, 0)), blk, blk, blk],
        out_specs=[blk] * 4, out_shape=[SDS((rows, LANES), F32)] * 4,
        compiler_params=_params(("parallel",)), name="sum_adamw")(parts, wgt, m, v)


PACK_ALIGN = 2 * SUBLANES * LANES
PACK_ROWS = 512

SHARDED = {"w_in": 1, "rwkv_w2": 1, "rwkv_a2": 1, "rwkv_g2": 1, "s5_w_glu": 0, "w_branch_rwkv": 1, "w_branch_s5": 1,
           "w_out": 0, "ffn_w_up": 1, "ffn_conv_w": 1, "ffn_w_down": 0}
F32_GATHER = ("ffn_conv_w",)
REPLICATED = ("norm_mix_pre", "norm_mix_post", "norm_ffn_pre", "norm_ffn_post", "b_gate", "rwkv_shift_mu", "rwkv_w0",
              "rwkv_a0", "rwkv_k_k", "rwkv_k_a", "rwkv_r_k", "rwkv_lnx_w", "rwkv_lnx_b", "s5_a_re", "s5_a_im", "s5_b_re",
              "s5_b_im", "s5_c_re", "s5_c_im", "s5_d", "s5_log_step", "s5_b_glu", "ffn_conv_b")
WEIGHTS = ("norm_mix_pre", "norm_mix_post", "norm_ffn_pre", "norm_ffn_post", "w_in", "b_gate", "rwkv_shift_mu", "rwkv_w0",
           "rwkv_w2", "rwkv_a0", "rwkv_a2", "rwkv_g2", "rwkv_k_k", "rwkv_k_a", "rwkv_r_k", "rwkv_lnx_w", "rwkv_lnx_b",
           "s5_a_re", "s5_a_im", "s5_b_re", "s5_b_im", "s5_c_re", "s5_c_im", "s5_d", "s5_log_step", "s5_w_glu", "s5_b_glu",
           "w_branch_rwkv", "w_branch_s5", "w_out", "ffn_w_up", "ffn_conv_w", "ffn_conv_b", "ffn_w_down")


def _pad_flat(a):
    flat = a.reshape(-1)
    pad = (-flat.shape[0]) % PACK_ALIGN
    return jnp.pad(flat, (0, pad)) if pad else flat


def _pack(pieces):
    flat = jnp.concatenate([_pad_flat(p) for p in pieces])
    pad = (-flat.shape[0]) % (PACK_ROWS * LANES)
    return (jnp.pad(flat, (0, pad)) if pad else flat).reshape(-1, LANES)


def _unpack(buf, shapes):
    lead = buf.shape[:-2]
    flat = buf.reshape(*lead, -1)
    out, off = [], 0
    for s in shapes:
        n = math.prod(s)
        out.append(flat[..., off:off + n].reshape(*lead, *s))
        off += n + (-n) % PACK_ALIGN
    return out


def _unshard(stack, axis):
    if axis == 0:
        return stack.reshape(-1, stack.shape[2])
    return stack.transpose(1, 0, 2).reshape(stack.shape[1], -1)


def _reshard(full, axis):
    if axis == 0:
        return full.reshape(N_DEV, -1, full.shape[1])
    return full.reshape(full.shape[0], N_DEV, -1).transpose(1, 0, 2)


def _local_step(x, tgt, wt):
    o512 = _block_ones(RW, HN)
    w_in = wt["w_in"]
    w_rw, w_u, w_g = w_in[:, :N_RWKV], w_in[:, N_RWKV:N_RWKV + S5W], w_in[:, N_RWKV + S5W:]
    b_gate = wt["b_gate"]
    b_r, b_s = b_gate[:, :D_MODEL], b_gate[:, D_MODEL:]
    zpad = lambda a, lo, n: jnp.pad(a.astype(F32), ((lo, n - lo - a.shape[0]), (0, 0)))
    w2p, a2p, g2p = zpad(wt["rwkv_w2"], 0, 256), zpad(wt["rwkv_a2"], 64, 256), zpad(wt["rwkv_g2"], 128, 256)
    pre_small = [wt["rwkv_w0"], wt["rwkv_a0"], wt["rwkv_k_k"], wt["rwkv_k_a"], w2p, a2p, g2p]
    post_prm = [wt["rwkv_lnx_w"], wt["rwkv_lnx_b"], wt["rwkv_r_k"].reshape(1, RW)]
    mu = wt["rwkv_shift_mu"]
    c = min(WKV_CHUNK, x.shape[0])

    a_re, a_im = wt["s5_a_re"].reshape(1, S5N), wt["s5_a_im"].reshape(1, S5N)
    ls = jnp.repeat(wt["s5_log_step"].reshape(S5G), S5P).reshape(1, S5N)
    b_re_t = wt["s5_b_re"].reshape(S5N, S5C).T
    b_im_t = wt["s5_b_im"].reshape(S5N, S5C).T
    c_re, c_im = wt["s5_c_re"].reshape(S5G, S5C, S5P), wt["s5_c_im"].reshape(S5G, S5C, S5P)
    abr, abi, bbr, bbi = _s5_prep_fwd(a_re, a_im, ls, b_re_t, b_im_t)
    bblk = jnp.concatenate([_s5_blockdiag_in(bbr), _s5_blockdiag_in(bbi)], axis=2)
    cblk = jnp.concatenate([_s5_blockdiag_out(c_re), -_s5_blockdiag_out(c_im)], axis=1)
    s5_d = wt["s5_d"]

    h1 = _rms_fwd(x, wt["norm_mix_pre"], "rms1_fwd")
    p_rw = _mm(h1, w_rw, name="mm_proj_rwkv")
    u = _mm(h1, w_u, name="mm_proj_s5")
    gpre = _mm(h1, w_g, name="mm_proj_gate")
    r, decay, k2, v, aa, bb, g = _rwkv_pre_fwd(p_rw, mu, pre_small, o512)
    yt, chk = _wkv_fwd(r, decay, k2, v, aa, bb)
    y = _wkv_untranspose(yt, c)
    y_r = _rwkv_post_fwd(y, r, k2, v, g, post_prm, o512)
    o_r = _mm(y_r, wt["w_branch_rwkv"], name="mm_branch_rwkv")
    ys, kre, kim = _s5_fwd(u, bblk, cblk, abr, abi, s5_d)
    t_glu = _mm(ys, wt["s5_w_glu"], name="mm_glu")
    out5 = _glu_fwd(ys, t_glu, wt["s5_b_glu"])
    o_s = _mm(out5, wt["w_branch_s5"], name="mm_branch_s5")
    mi = _merge_fwd(gpre, o_r, o_s, b_r, b_s)
    mixed = _mm(mi, wt["w_out"], name="mm_out")
    x2, h2 = _mid_fwd(x, mixed, wt["norm_mix_post"], wt["norm_ffn_pre"])
    z = _mm(h2, wt["ffn_w_up"], name="mm_up")
    act = _conv_fwd(z, wt["ffn_conv_w"], wt["ffn_conv_b"])
    f = _mm(act, wt["ffn_w_down"], name="mm_down")
    dx3, df, d_norm_ffn_post, loss_part = _final(x2, f, tgt, wt["norm_ffn_post"])

    gr = {"norm_ffn_post": d_norm_ffn_post}
    dact = _mm(df, wt["ffn_w_down"], tb=True, name="mm_down_dx")
    gr["ffn_w_down"] = _mm(act, df, ta=True, name="mm_down_dw")
    dzg, dzv, gr["ffn_conv_w"], gr["ffn_conv_b"] = _conv_bwd(z, dact, wt["ffn_conv_w"], wt["ffn_conv_b"])
    dz = jnp.concatenate([dzg, dzv], axis=1)
    dh2 = _mm(dz, wt["ffn_w_up"], tb=True, name="mm_up_dx")
    gr["ffn_w_up"] = _mm(h2, dz, ta=True, name="mm_up_dw")
    dx2, dmixed, gr["norm_mix_post"], gr["norm_ffn_pre"] = _mid_bwd(x2, mixed, dh2, dx3, wt["norm_mix_post"], wt["norm_ffn_pre"])
    dmi = _mm(dmixed, wt["w_out"], tb=True, name="mm_out_dx")
    gr["w_out"] = _mm(mi, dmixed, ta=True, name="mm_out_dw")
    dgp_r, dgp_s, do_r, do_s, db_r, db_s = _merge_bwd(gpre, o_r, o_s, dmi, b_r, b_s)
    gr["b_gate"] = jnp.concatenate([db_r, db_s], axis=1)
    dout5 = _mm(do_s, wt["w_branch_s5"], tb=True, name="mm_branch_s5_dx")
    gr["w_branch_s5"] = _mm(out5, do_s, ta=True, name="mm_branch_s5_dw")
    dys_a, dt_glu, gr["s5_b_glu"] = _glu_bwd(ys, t_glu, dout5, wt["s5_b_glu"])
    dys_b = _mm(dt_glu, wt["s5_w_glu"], tb=True, name="mm_glu_dx")
    gr["s5_w_glu"] = _mm(ys, dt_glu, ta=True, name="mm_glu_dw")
    du, dbblk, dcblk, dabr, dabi, gr["s5_d"] = _s5_bwd(u, dys_a, dys_b, kre, kim, bblk, cblk, abr, abi, s5_d)
    gr["s5_c_re"] = _s5_blockdiag_out_t(dcblk[:, :S5_BN, :]).reshape(wt["s5_c_re"].shape)
    gr["s5_c_im"] = (-_s5_blockdiag_out_t(dcblk[:, S5_BN:, :])).reshape(wt["s5_c_im"].shape)
    dbbr, dbbi = _s5_blockdiag_in_t(dbblk[:, :, :S5_BN]), _s5_blockdiag_in_t(dbblk[:, :, S5_BN:])
    gsel = (lax.broadcasted_iota(jnp.int32, (S5N, LANES), 0) // S5P == lax.broadcasted_iota(jnp.int32, (S5N, LANES), 1)).astype(F32)
    d_are, d_aim, d_ls, d_bre_t, d_bim_t = _s5_prep_bwd(a_re, a_im, ls, b_re_t, b_im_t, (dabr, dabi, dbbr, dbbi), gsel)
    gr["s5_a_re"] = d_are.reshape(wt["s5_a_re"].shape)
    gr["s5_a_im"] = d_aim.reshape(wt["s5_a_im"].shape)
    gr["s5_log_step"] = d_ls[0:1, :S5G]
    gr["s5_b_re"] = d_bre_t.T.reshape(wt["s5_b_re"].shape)
    gr["s5_b_im"] = d_bim_t.T.reshape(wt["s5_b_im"].shape)
    dy_r = _mm(do_r, wt["w_branch_rwkv"], tb=True, name="mm_branch_rwkv_dx")
    gr["w_branch_rwkv"] = _mm(y_r, do_r, ta=True, name="mm_branch_rwkv_dw")
    dy, dr1, dk1, dv1, dg, gr["rwkv_lnx_w"], gr["rwkv_lnx_b"], d_rk = _rwkv_post_bwd(y, r, k2, v, g, dy_r, post_prm, o512)
    gr["rwkv_r_k"] = d_rk.reshape(wt["rwkv_r_k"].shape)
    dr2, ddecay, dk2, daa, dbb, dvt = _wkv_bwd(r, decay, k2, v, aa, bb, dy, chk)
    dv2 = _wkv_untranspose(dvt, c)
    cots = [dr1 + dr2, ddecay, dk1 + dk2, dv1 + dv2, daa, dbb, dg]
    dp_rw, gr["rwkv_shift_mu"], gr["rwkv_w0"], gr["rwkv_a0"], gr["rwkv_k_k"], gr["rwkv_k_a"], dw2p, da2p, dg2p = \
        _rwkv_pre_bwd(p_rw, cots, mu, pre_small, o512)
    gr["rwkv_w2"], gr["rwkv_a2"], gr["rwkv_g2"] = dw2p[:64], da2p[64:128], dg2p[128:]
    dproj = jnp.concatenate([dp_rw, du, dgp_r, dgp_s], axis=1)
    dh1 = _mm(dproj, w_in, tb=True, name="mm_proj_dx")
    gr["w_in"] = _mm(h1, dproj, ta=True, name="mm_proj_dw")
    dx, gr["norm_mix_pre"] = _rms_bwd(x, wt["norm_mix_pre"], dh1, dx2, "rms1_bwd")
    return loss_part[0, 0], dx, gr


def kernel(x, norm_mix_pre, norm_mix_post, norm_ffn_pre, norm_ffn_post, w_in, b_gate, rwkv_shift_mu, rwkv_w0, rwkv_w2, rwkv_a0, rwkv_a2, rwkv_g2, rwkv_k_k, rwkv_k_a, rwkv_r_k, rwkv_lnx_w, rwkv_lnx_b, s5_a_re, s5_a_im, s5_b_re, s5_b_im, s5_c_re, s5_c_im, s5_d, s5_log_step, s5_w_glu, s5_b_glu, w_branch_rwkv, w_branch_s5, w_out, ffn_w_up, ffn_conv_w, ffn_conv_b, ffn_w_down, loss_target, m_norm_mix_pre, m_norm_mix_post, m_norm_ffn_pre, m_norm_ffn_post, m_w_in, m_b_gate, m_rwkv_shift_mu, m_rwkv_w0, m_rwkv_w2, m_rwkv_a0, m_rwkv_a2, m_rwkv_g2, m_rwkv_k_k, m_rwkv_k_a, m_rwkv_r_k, m_rwkv_lnx_w, m_rwkv_lnx_b, m_s5_a_re, m_s5_a_im, m_s5_b_re, m_s5_b_im, m_s5_c_re, m_s5_c_im, m_s5_d, m_s5_log_step, m_s5_w_glu, m_s5_b_glu, m_w_branch_rwkv, m_w_branch_s5, m_w_out, m_ffn_w_up, m_ffn_conv_w, m_ffn_conv_b, m_ffn_w_down, v_norm_mix_pre, v_norm_mix_post, v_norm_ffn_pre, v_norm_ffn_post, v_w_in, v_b_gate, v_rwkv_shift_mu, v_rwkv_w0, v_rwkv_w2, v_rwkv_a0, v_rwkv_a2, v_rwkv_g2, v_rwkv_k_k, v_rwkv_k_a, v_rwkv_r_k, v_rwkv_lnx_w, v_rwkv_lnx_b, v_s5_a_re, v_s5_a_im, v_s5_b_re, v_s5_b_im, v_s5_c_re, v_s5_c_im, v_s5_d, v_s5_log_step, v_s5_w_glu, v_s5_b_glu, v_w_branch_rwkv, v_w_branch_s5, v_w_out, v_ffn_w_up, v_ffn_conv_w, v_ffn_conv_b, v_ffn_w_down):
    args = dict(locals())
    wgt = {n: args[n] for n in WEIGHTS}
    mom = {n: args["m_" + n] for n in WEIGHTS}
    var = {n: args["v_" + n] for n in WEIGHTS}
    sharded = list(SHARDED)
    shard_shapes = {n: wgt[n].shape[1:] for n in sharded}

    bf_names = [n for n in sharded if n not in F32_GATHER]
    bf_words = [lax.bitcast_convert_type(_pad_flat(wgt[n][0].astype(BF16)).reshape(-1, 2), F32) for n in bf_names]
    payload = _pack(bf_words + [wgt[n][0] for n in F32_GATHER])
    gathered = _all_gather(payload)
    word_shapes = [(w.shape[0],) for w in bf_words] + [shard_shapes[n] for n in F32_GATHER]
    got = _unpack(gathered, word_shapes)
    full = {}
    for n, words in zip(bf_names, got[:len(bf_names)], strict=True):
        halves = lax.bitcast_convert_type(words, BF16).reshape(N_DEV, -1)
        r_, c_ = shard_shapes[n]
        full[n] = _unshard(halves[:, :r_ * c_].reshape(N_DEV, r_, c_), SHARDED[n])
    for n, blocks in zip(F32_GATHER, got[len(bf_names):], strict=True):
        full[n] = _unshard(blocks, SHARDED[n])
    wt = {n: (full[n] if n in SHARDED else wgt[n]) for n in WEIGHTS}

    loss_part, dx, gr = _local_step(x[0], loss_target[0], wt)

    rep_flat = [gr[n].reshape(-1) for n in REPLICATED]
    rep_shapes = [(g_.shape[0],) for g_ in rep_flat]
    shard_blocks = [_reshard(gr[n].astype(F32), SHARDED[n]) for n in sharded]
    chunks = jnp.stack([_pack([blk[d_] for blk in shard_blocks] + rep_flat) for d_ in range(N_DEV)])
    parts = _all_to_all(chunks)

    pack_local = lambda src: _pack([src[n][0] for n in sharded] + [src[n].reshape(-1) for n in REPLICATED])
    g_sum, delta, m_new, v_new = _sum_adamw(parts, pack_local(wgt), pack_local(mom), pack_local(var))
    shapes = [shard_shapes[n] for n in sharded] + rep_shapes
    names = sharded + list(REPLICATED)

    def unpack_all(buf):
        vals = dict(zip(names, _unpack(buf, shapes), strict=True))
        return [vals[n].reshape(wgt[n].shape) for n in WEIGHTS]

    loss = lax.psum(loss_part, AXES)
    return (loss, dx[None], *unpack_all(g_sum), *unpack_all(delta), *unpack_all(m_new), *unpack_all(v_new))
```

```python
import functools
import math

import jax
import jax.numpy as jnp
from jax import lax
from jax.experimental import pallas as pl
from jax.experimental.pallas import tpu as pltpu

F32 = jnp.float32
BF16 = jnp.bfloat16
SDS = jax.ShapeDtypeStruct
HI = lax.Precision.HIGHEST
MESH = pl.DeviceIdType.MESH
AXES = ("x", "y", "c")
N_DEV = 8

D_MODEL = 1024
RW = 512
HN = 64
N_RWKV = 1792
S5W = 512
S5G = 32
S5C = 16
S5P = 64
S5N = S5G * S5P
D_FF = 2816
NORM_EPS = 1e-6
LNX_EPS = 64e-5

ADAM_LR = 0.001
ADAM_B1 = 0.9
ADAM_B2 = 0.999
ADAM_EPS = 1e-08
ADAM_WD = 0.01
ADAM_STEP = 10

LANES = 128
SUBLANES = 8
VMEM_LIMIT = 56 * 1024 * 1024
WKV_CHUNK = 32


def _params(sem=("arbitrary",)):
    return pltpu.CompilerParams(dimension_semantics=sem, vmem_limit_bytes=VMEM_LIMIT)


def _pick(n, cap):
    best = None
    for t in range(LANES, min(n, cap) + 1, LANES):
        if n % t == 0:
            best = t
    return best or n


def _mm(a, b, *, ta=False, tb=False, out_dtype=F32, name):
    m = a.shape[1] if ta else a.shape[0]
    k = a.shape[0] if ta else a.shape[1]
    n = b.shape[0] if tb else b.shape[1]
    assert (b.shape[1] if tb else b.shape[0]) == k
    tm, tn, tk = _pick(m, 512), _pick(n, 2304), _pick(k, 512)
    nk = k // tk
    dims = (((0 if ta else 1,), (1 if tb else 0,)), ((), ()))

    def body(a_ref, b_ref, o_ref, acc_ref):
        kk = pl.program_id(2)

        @pl.when(kk == 0)
        def _():
            acc_ref[...] = jnp.zeros_like(acc_ref)

        acc_ref[...] += lax.dot_general(a_ref[...].astype(BF16), b_ref[...].astype(BF16), dims,
                                        preferred_element_type=F32)

        @pl.when(kk == nk - 1)
        def _():
            o_ref[...] = acc_ref[...].astype(o_ref.dtype)

    a_spec = pl.BlockSpec((tk, tm), lambda i, j, kk: (kk, i)) if ta else pl.BlockSpec((tm, tk), lambda i, j, kk: (i, kk))
    b_spec = pl.BlockSpec((tn, tk), lambda i, j, kk: (j, kk)) if tb else pl.BlockSpec((tk, tn), lambda i, j, kk: (kk, j))
    return pl.pallas_call(
        body, grid=(m // tm, n // tn, nk), in_specs=[a_spec, b_spec],
        out_specs=pl.BlockSpec((tm, tn), lambda i, j, kk: (i, j)),
        out_shape=SDS((m, n), out_dtype), scratch_shapes=[pltpu.VMEM((tm, tn), F32)],
        compiler_params=_params(("parallel", "parallel", "arbitrary")), name=name)(a, b)


def _rows(fn, rows, params, out_rows, out_accs, *, name, tl, reverse=False, scratch=()):
    first = rows[0][0] if isinstance(rows[0], tuple) else rows[0]
    length = first.shape[0]
    tl = min(tl, length)
    nt = length // tl
    rmap = (lambda i: nt - 1 - i) if reverse else (lambda i: i)
    specs, arrs = [], []
    for r in rows:
        arr, wdt, cb = r if isinstance(r, tuple) else (r, r.shape[1], 0)
        specs.append(pl.BlockSpec((tl, wdt), lambda i, cb=cb: (rmap(i), cb)))
        arrs.append(arr)
    for p in params:
        specs.append(pl.BlockSpec(p.shape, lambda i, nd=p.ndim: (0,) * nd))
        arrs.append(p)
    out_shape = [SDS((length, c), dt) for c, dt in out_rows] + [SDS(s, F32) for s in out_accs]
    out_specs = [pl.BlockSpec((tl, c), lambda i: (rmap(i), 0)) for c, _ in out_rows]
    out_specs += [pl.BlockSpec(s, lambda i, nd=len(s): (0,) * nd) for s in out_accs]
    nr, npar, nor, noa = len(rows), len(params), len(out_rows), len(out_accs)

    def body(*refs):
        rin, pin = refs[:nr], refs[nr:nr + npar]
        rout = refs[nr + npar:nr + npar + nor]
        aout = refs[nr + npar + nor:nr + npar + nor + noa]
        scr = refs[nr + npar + nor + noa:]
        step = pl.program_id(0)
        outs_r, outs_a = fn(step, [r[...] for r in rin], [p[...] for p in pin], scr)
        for ref, val in zip(rout, outs_r, strict=True):
            ref[...] = val.astype(ref.dtype)

        @pl.when(step == 0)
        def _():
            for ref in aout:
                ref[...] = jnp.zeros_like(ref)

        for ref, val in zip(aout, outs_a, strict=True):
            ref[...] += val.astype(F32)

    res = pl.pallas_call(body, grid=(nt,), in_specs=specs, out_specs=out_specs, out_shape=out_shape,
                         scratch_shapes=list(scratch), compiler_params=_params(), name=name)(*arrs)
    return list(res)


def _rms(x, g):
    return x * lax.rsqrt(jnp.mean(x * x, axis=-1, keepdims=True) + NORM_EPS) * g


def _sig(x):
    return 0.5 * (jnp.tanh(0.5 * x) + 1.0)


def _softplus(x):
    return jnp.maximum(x, 0.0) + jnp.log(1.0 + jnp.exp(-jnp.abs(x)))


def _gelu(x):
    return x * (0.5 * (1.0 + jnp.tanh(math.sqrt(2.0 / math.pi) * (x + 0.044715 * (x * x * x)))))


def _bdot(a, b):
    return jnp.dot(a.astype(BF16), b.astype(BF16), preferred_element_type=F32)


def _hdot(a, b):
    return jnp.dot(a, b, precision=HI, preferred_element_type=F32)


def _block_ones(n, blk):
    i = lax.broadcasted_iota(jnp.int32, (n, n), 0) // blk
    j = lax.broadcasted_iota(jnp.int32, (n, n), 1) // blk
    return (i == j).astype(F32)


def _rms_fwd(x, g, name):
    return _rows(lambda s, r, p, _: ([_rms(r[0], p[0])], []), [x], [g], [(x.shape[1], BF16)], [], name=name, tl=512)[0]


def _rms_bwd(x, g, dh, dres, name):
    def fn(s, r, p, _):
        _, vjp = jax.vjp(_rms, r[0], p[0])
        dx, dg = vjp(r[1])
        return [dx + r[2]], [dg]
    return _rows(fn, [x, dh, dres], [g], [(x.shape[1], F32)], [g.shape], name=name, tl=256)


def _rwkv_pre_math(k_, lr, w0, a0, k_k, k_a, w2p, a2p, g2p, o512):
    pre_w = w0 + _bdot(jnp.tanh(lr), w2p)
    w = -_softplus(-pre_w) - 0.5
    decay = jnp.exp(-jnp.exp(w))
    a = _sig(a0 + _bdot(lr, a2p))
    g = _bdot(_sig(lr), g2p)
    kr = k_ * k_k
    kk = kr / jnp.maximum(jnp.sqrt(_hdot(kr * kr, o512)), 1e-12)
    k2 = k_ * (1.0 + (a - 1.0) * k_a)
    return decay, k2, -kk, kk * a, g


def _shift_down(p, prev_row):
    row = lax.broadcasted_iota(jnp.int32, p.shape, 0)
    return jnp.where(row == 0, jnp.broadcast_to(prev_row, p.shape), pltpu.roll(p, 1, 0))


def _shift_up(q, next_row):
    n = q.shape[0]
    row = lax.broadcasted_iota(jnp.int32, q.shape, 0)
    return jnp.where(row == n - 1, jnp.broadcast_to(next_row, q.shape), pltpu.roll(q, n - 1, 0))


def _rwkv_pre_fwd(p, mu, small, o512):
    def fn(step, r, prm, scr):
        car = scr[0]

        @pl.when(step == 0)
        def _():
            car[...] = jnp.zeros_like(car)

        x = r[0]
        prev = _shift_down(x, car[SUBLANES - 1:SUBLANES, :])
        car[...] = x[x.shape[0] - SUBLANES:, :]
        xs = x + (prev - x) * prm[0]
        decay, k2, aa, bb, g = _rwkv_pre_math(xs[:, RW:2 * RW], xs[:, 3 * RW:], *prm[1:])
        return [xs[:, :RW], decay, k2, xs[:, 2 * RW:3 * RW], aa, bb, g], []
    return _rows(fn, [p], [mu, *small, o512], [(RW, F32)] * 7, [], name="rwkv_pre_fwd", tl=256,
                 scratch=[pltpu.VMEM((SUBLANES, N_RWKV), F32)])


def _rwkv_pre_bwd(p, cots, mu, small, o512):
    length = p.shape[0]
    tl = min(256, length)
    nt = length // tl
    rows_per = tl // SUBLANES
    params = [mu, *small, o512]
    acc_shapes = [mu.shape] + [q.shape for q in small]
    nr, npar, nacc = 2 + len(cots), len(params), len(acc_shapes)

    def body(*refs):
        rin, pin = refs[:nr], refs[nr:nr + npar]
        dp_ref = refs[nr + npar]
        aout = refs[nr + npar + 1:nr + npar + 1 + nacc]
        car_q = refs[nr + npar + 1 + nacc]
        step = pl.program_id(0)

        @pl.when(step == 0)
        def _():
            car_q[...] = jnp.zeros_like(car_q)
            for ref in aout:
                ref[...] = jnp.zeros_like(ref)

        x = rin[0][...]
        prev_row = jnp.where(step == nt - 1, 0.0, rin[1][SUBLANES - 1:SUBLANES, :])
        dr, ddecay, dk2, dv, daa, dbb, dg = [r[...] for r in rin[2:]]
        prm = [q[...] for q in pin]
        mu_, o512_ = prm[0], prm[-1]
        prev = _shift_down(x, prev_row)
        xs = x + (prev - x) * mu_
        _, vjp = jax.vjp(lambda k_, lr, *w: _rwkv_pre_math(k_, lr, *w, o512_), xs[:, RW:2 * RW], xs[:, 3 * RW:], *prm[1:-1])
        dk_, dlr, *dsmall = vjp((ddecay, dk2, daa, dbb, dg))
        dxs = jnp.concatenate([dr, dk_, dv, dlr], axis=1)
        q = dxs * mu_
        dp_ref[...] = dxs - q + _shift_up(q, car_q[0:1, :])
        car_q[...] = q[:SUBLANES, :]
        aout[0][...] += jnp.sum((prev - x) * dxs, axis=0, keepdims=True)
        for ref, val in zip(aout[1:], dsmall, strict=True):
            ref[...] += val

    rmap = lambda i: nt - 1 - i
    specs = [pl.BlockSpec((tl, N_RWKV), lambda i: (rmap(i), 0)),
             pl.BlockSpec((SUBLANES, N_RWKV), lambda i: (jnp.maximum(rmap(i) * rows_per - 1, 0), 0))]
    specs += [pl.BlockSpec((tl, RW), lambda i: (rmap(i), 0)) for _ in cots]
    specs += [pl.BlockSpec(q.shape, lambda i, nd=q.ndim: (0,) * nd) for q in params]
    out_shape = [SDS((length, N_RWKV), F32)] + [SDS(sh, F32) for sh in acc_shapes]
    out_specs = [pl.BlockSpec((tl, N_RWKV), lambda i: (rmap(i), 0))]
    out_specs += [pl.BlockSpec(sh, lambda i, nd=len(sh): (0,) * nd) for sh in acc_shapes]
    res = pl.pallas_call(body, grid=(nt,), in_specs=specs, out_specs=out_specs, out_shape=out_shape,
                         scratch_shapes=[pltpu.VMEM((SUBLANES, N_RWKV), F32)],
                         compiler_params=_params(), name="rwkv_pre_bwd")(p, p, *cots, *params)
    return list(res)


def _rwkv_post_math(y, r, k2, v, g, lnx_w, lnx_b, r_k, o512):
    mean = _hdot(y, o512) * (1.0 / HN)
    yc = y - mean
    var = _hdot(yc * yc, o512) * (1.0 / HN)
    yn = yc * lax.rsqrt(var + LNX_EPS) * lnx_w + lnx_b
    bonus = _hdot(r * k2 * r_k, o512) * v
    return (yn + bonus) * g


def _rwkv_post_fwd(y, r, k2, v, g, prm, o512):
    return _rows(lambda s, rr, p, _: ([_rwkv_post_math(*rr, *p)], []), [y, r, k2, v, g], [*prm, o512],
                 [(RW, BF16)], [], name="rwkv_post_fwd", tl=256)[0]


def _rwkv_post_bwd(y, r, k2, v, g, dout, prm, o512):
    def fn(s, rr, p, _):
        o = p[-1]
        _, vjp = jax.vjp(lambda *a: _rwkv_post_math(*a, o), *rr[:5], *p[:-1])
        gr = vjp(rr[5])
        return list(gr[:5]), list(gr[5:])
    return _rows(fn, [y, r, k2, v, g, dout], [*prm, o512], [(RW, F32)] * 5, [q.shape for q in prm],
                 name="rwkv_post_bwd", tl=256)


def _glu_math(ys, t, b):
    return ys * _sig(t + b)


def _glu_fwd(ys, t, b):
    return _rows(lambda s, r, p, _: ([_glu_math(r[0], r[1], p[0])], []), [ys, t], [b], [(S5W, BF16)], [],
                 name="s5_glu_fwd", tl=512)[0]


def _glu_bwd(ys, t, dout, b):
    def fn(s, r, p, _):
        _, vjp = jax.vjp(_glu_math, r[0], r[1], p[0])
        dys, dt, db = vjp(r[2])
        return [dys, dt], [db]
    return _rows(fn, [ys, t, dout], [b], [(S5W, F32)] * 2, [b.shape], name="s5_glu_bwd", tl=512)


def _merge_math(gp_r, gp_s, o_r, o_s, b_r, b_s):
    return _sig(gp_r + b_r) * o_r + _sig(gp_s + b_s) * o_s


def _merge_fwd(gpre, o_r, o_s, b_r, b_s):
    return _rows(lambda s, r, p, _: ([_merge_math(*r, *p)], []),
                 [(gpre, D_MODEL, 0), (gpre, D_MODEL, 1), o_r, o_s], [b_r, b_s], [(D_MODEL, BF16)], [],
                 name="merge_fwd", tl=256)[0]


def _merge_bwd(gpre, o_r, o_s, dmi, b_r, b_s):
    def fn(s, r, p, _):
        _, vjp = jax.vjp(_merge_math, *r[:4], *p)
        dgr, dgs, dor, dos, dbr, dbs = vjp(r[4])
        return [dgr, dgs, dor, dos], [dbr, dbs]
    return _rows(fn, [(gpre, D_MODEL, 0), (gpre, D_MODEL, 1), o_r, o_s, dmi], [b_r, b_s], [(D_MODEL, F32)] * 4,
                 [b_r.shape, b_s.shape], name="merge_bwd", tl=256)


def _mid_fwd(x, mixed, g_post, g_pre):
    def fn(s, r, p, _):
        x2 = r[0] + _rms(r[1], p[0])
        return [x2, _rms(x2, p[1])], []
    return _rows(fn, [x, mixed], [g_post, g_pre], [(D_MODEL, F32), (D_MODEL, BF16)], [], name="mid_fwd", tl=256)


def _mid_bwd(x2, mixed, dh2, dx3, g_post, g_pre):
    def fn(s, r, p, _):
        _, vjp1 = jax.vjp(_rms, r[0], p[1])
        dx2, dg_pre = vjp1(r[2])
        dx2 = dx2 + r[3]
        _, vjp2 = jax.vjp(_rms, r[1], p[0])
        dmixed, dg_post = vjp2(dx2)
        return [dx2, dmixed], [dg_post, dg_pre]
    return _rows(fn, [x2, mixed, dh2, dx3], [g_post, g_pre], [(D_MODEL, F32)] * 2, [g_post.shape, g_pre.shape],
                 name="mid_bwd", tl=256)


def _final(x2, f, tgt, g_post):
    def fn(s, r, p, _):
        y, vjp = jax.vjp(_rms, r[1], p[0])
        diff = r[0] + y - r[2]
        dx3 = diff * (1.0 / D_MODEL)
        df, dg = vjp(dx3)
        part = 0.5 * jnp.sum(jnp.sum(diff * diff, axis=1, keepdims=True), axis=0, keepdims=True) * (1.0 / D_MODEL)
        return [dx3, df], [dg, jnp.broadcast_to(part, (1, LANES))]
    return _rows(fn, [x2, f, tgt], [g_post], [(D_MODEL, F32)] * 2, [g_post.shape, (1, LANES)], name="final", tl=256)


def _conv_taps(z, car):
    row = lax.broadcasted_iota(jnp.int32, z.shape, 0)
    z1 = jnp.where(row == 0, jnp.broadcast_to(car[7:8, :], z.shape), pltpu.roll(z, 1, 0))
    z2 = pltpu.roll(z, 2, 0)
    z2 = jnp.where(row == 0, jnp.broadcast_to(car[6:7, :], z.shape), z2)
    z2 = jnp.where(row == 1, jnp.broadcast_to(car[7:8, :], z.shape), z2)
    return z1, z2


def _conv(z, car, w, b):
    z1, z2 = _conv_taps(z, car)
    return b + w[0:1, :] * z2 + w[1:2, :] * z1 + w[2:3, :] * z, z1, z2


def _conv_fwd(z, conv_w, conv_b):
    length = z.shape[0]
    tl = min(256, length)
    nt = length // tl
    tc = _pick(D_FF, 1536)
    nb = D_FF // tc

    def body(zg_ref, zv_ref, wg_ref, wv_ref, bg_ref, bv_ref, o_ref, cg, cv):
        @pl.when(pl.program_id(1) == 0)
        def _():
            cg[...] = jnp.zeros_like(cg)
            cv[...] = jnp.zeros_like(cv)

        zg, zv = zg_ref[...], zv_ref[...]
        gate, _, _ = _conv(zg, cg[...], wg_ref[...], bg_ref[...])
        val, _, _ = _conv(zv, cv[...], wv_ref[...], bv_ref[...])
        cg[...] = zg[tl - SUBLANES:, :]
        cv[...] = zv[tl - SUBLANES:, :]
        o_ref[...] = (_gelu(gate) * val).astype(o_ref.dtype)

    zspec = lambda off: pl.BlockSpec((tl, tc), lambda j, i: (i, j + off))
    wspec = lambda off, r: pl.BlockSpec((r, tc), lambda j, i: (0, j + off))
    return pl.pallas_call(
        body, grid=(nb, nt),
        in_specs=[zspec(0), zspec(nb), wspec(0, 3), wspec(nb, 3), wspec(0, 1), wspec(nb, 1)],
        out_specs=pl.BlockSpec((tl, tc), lambda j, i: (i, j)), out_shape=SDS((length, D_FF), BF16),
        scratch_shapes=[pltpu.VMEM((SUBLANES, tc), F32)] * 2,
        compiler_params=_params(("arbitrary", "arbitrary")), name="conv_fwd")(z, z, conv_w, conv_w, conv_b, conv_b)


def _conv_bwd(z, dact, conv_w, conv_b):
    length = z.shape[0]
    tl = min(256, length)
    nt = length // tl
    tc = _pick(D_FF, 1536)
    nb = D_FF // tc
    rows_per = tl // SUBLANES

    def half_bwd(dzc, z, z1, z2, w, dcar):
        n = tl
        row = lax.broadcasted_iota(jnp.int32, dzc.shape, 0)
        u1 = jnp.where(row == n - 1, jnp.broadcast_to(dcar[0:1, :], dzc.shape), pltpu.roll(dzc, n - 1, 0))
        u2 = pltpu.roll(dzc, n - 2, 0)
        u2 = jnp.where(row == n - 2, jnp.broadcast_to(dcar[0:1, :], dzc.shape), u2)
        u2 = jnp.where(row == n - 1, jnp.broadcast_to(dcar[1:2, :], dzc.shape), u2)
        dz = w[2:3, :] * dzc + w[1:2, :] * u1 + w[0:1, :] * u2
        dw = jnp.concatenate([jnp.sum(dzc * z2, axis=0, keepdims=True), jnp.sum(dzc * z1, axis=0, keepdims=True),
                              jnp.sum(dzc * z, axis=0, keepdims=True)], axis=0)
        return dz, dw, jnp.sum(dzc, axis=0, keepdims=True)

    def body(zg_ref, zv_ref, pg_ref, pv_ref, da_ref, wg_ref, wv_ref, bg_ref, bv_ref,
             dzg_ref, dzv_ref, dwg_ref, dwv_ref, dbg_ref, dbv_ref, cg, cv):
        step = pl.program_id(1)

        @pl.when(step == 0)
        def _():
            cg[...] = jnp.zeros_like(cg)
            cv[...] = jnp.zeros_like(cv)
            for ref in (dwg_ref, dwv_ref, dbg_ref, dbv_ref):
                ref[...] = jnp.zeros_like(ref)

        is_first_tile = step == nt - 1
        zg, zv = zg_ref[...], zv_ref[...]
        pg = jnp.where(is_first_tile, 0.0, pg_ref[...])
        pv = jnp.where(is_first_tile, 0.0, pv_ref[...])
        wg, wv = wg_ref[...], wv_ref[...]
        gate, zg1, zg2 = _conv(zg, pg, wg, bg_ref[...])
        val, zv1, zv2 = _conv(zv, pv, wv, bv_ref[...])
        act_g, vjp = jax.vjp(_gelu, gate)
        da = da_ref[...]
        dgate = vjp(da * val)[0]
        dval = da * act_g
        dzg, dwg, dbg = half_bwd(dgate, zg, zg1, zg2, wg, cg[...])
        dzv, dwv, dbv = half_bwd(dval, zv, zv1, zv2, wv, cv[...])
        cg[...] = dgate[:SUBLANES, :]
        cv[...] = dval[:SUBLANES, :]
        dzg_ref[...] = dzg
        dzv_ref[...] = dzv
        dwg_ref[...] += dwg
        dwv_ref[...] += dwv
        dbg_ref[...] += dbg
        dbv_ref[...] += dbv

    rmap = lambda i: nt - 1 - i
    zspec = lambda off: pl.BlockSpec((tl, tc), lambda j, i: (rmap(i), j + off))
    pspec = lambda off: pl.BlockSpec((SUBLANES, tc), lambda j, i: (jnp.maximum(rmap(i) * rows_per - 1, 0), j + off))
    wspec = lambda off, r: pl.BlockSpec((r, tc), lambda j, i: (0, j + off))
    out_w = lambda r: pl.BlockSpec((r, tc), lambda j, i: (0, j))
    dzg, dzv, dwg, dwv, dbg, dbv = pl.pallas_call(
        body, grid=(nb, nt),
        in_specs=[zspec(0), zspec(nb), pspec(0), pspec(nb), pl.BlockSpec((tl, tc), lambda j, i: (rmap(i), j)),
                  wspec(0, 3), wspec(nb, 3), wspec(0, 1), wspec(nb, 1)],
        out_specs=[pl.BlockSpec((tl, tc), lambda j, i: (rmap(i), j))] * 2 + [out_w(3), out_w(3), out_w(1), out_w(1)],
        out_shape=[SDS((length, D_FF), F32)] * 2 + [SDS((3, D_FF), F32)] * 2 + [SDS((1, D_FF), F32)] * 2,
        scratch_shapes=[pltpu.VMEM((SUBLANES, tc), F32)] * 2,
        compiler_params=_params(("arbitrary", "arbitrary")), name="conv_bwd")(z, z, z, z, dact, conv_w, conv_w, conv_b, conv_b)
    return dzg, dzv, jnp.concatenate([dwg, dwv], axis=1), jnp.concatenate([dbg, dbv], axis=1)


def _s5_prep_math(a_re, a_im, ls, b_re, b_im):
    dt = jnp.exp(ls)
    er = jnp.exp(a_re * dt)
    ph = a_im * dt
    abr, abi = er * jnp.cos(ph), er * jnp.sin(ph)
    den = a_re * a_re + a_im * a_im
    nr = abr - 1.0
    cr = (nr * a_re + abi * a_im) / den
    ci = (abi * a_re - nr * a_im) / den
    return abr, abi, cr * b_re - ci * b_im, cr * b_im + ci * b_re


def _s5_prep_fwd(a_re, a_im, ls, b_re, b_im):
    def body(ar, ai, l, br, bi, o1, o2, o3, o4):
        for ref, val in zip((o1, o2, o3, o4), _s5_prep_math(ar[...], ai[...], l[...], br[...], bi[...]), strict=True):
            ref[...] = val
    return pl.pallas_call(body, out_shape=[SDS((1, S5N), F32)] * 2 + [SDS((S5C, S5N), F32)] * 2,
                          name="s5_prep_fwd")(a_re, a_im, ls, b_re, b_im)


def _s5_prep_bwd(a_re, a_im, ls, b_re, b_im, cots, gsel):
    def body(ar, ai, l, br, bi, c1, c2, c3, c4, g_ref, o1, o2, o3, o4, o5):
        _, vjp = jax.vjp(_s5_prep_math, ar[...], ai[...], l[...], br[...], bi[...])
        d_ar, d_ai, d_ls, d_br, d_bi = vjp((c1[...], c2[...], c3[...], c4[...]))
        o1[...] = d_ar
        o2[...] = d_ai
        o3[...] = _hdot(jnp.broadcast_to(d_ls, (SUBLANES, S5N)), g_ref[...])
        o4[...] = d_br
        o5[...] = d_bi
    return pl.pallas_call(body, out_shape=[SDS((1, S5N), F32)] * 2 + [SDS((SUBLANES, LANES), F32)] + [SDS((S5C, S5N), F32)] * 2,
                          name="s5_prep_bwd")(a_re, a_im, ls, b_re, b_im, *cots, gsel)


def _cmul(ar, ai, br, bi):
    return ar * br - ai * bi, ar * bi + ai * br


def _s5_powers(abr, abi):
    shp = (SUBLANES, S5N)
    a1 = (jnp.broadcast_to(abr, shp), jnp.broadcast_to(abi, shp))
    a2 = _cmul(*a1, *a1)
    a4 = _cmul(*a2, *a2)
    row = lax.broadcasted_iota(jnp.int32, shp, 0)
    pr, pi = a1
    cur = a1
    for i in range(1, SUBLANES):
        cur = _cmul(*cur, *a1)
        pr = jnp.where(row == i, cur[0], pr)
        pi = jnp.where(row == i, cur[1], pi)
    return a1, a2, a4, (pr, pi)


def _s5_scan(sre_ref, sim_ref, car_re, car_im, abr, abi, n_rows, reverse):
    a1, a2, a4, (pr, pi) = _s5_powers(abr, abi)
    sgn = -1.0 if reverse else 1.0
    row = lax.broadcasted_iota(jnp.int32, (SUBLANES, S5N), 0)
    if reverse:
        qr, qi = pr, pi
        for i in range(SUBLANES):
            src = SUBLANES - 1 - i
            qr = jnp.where(row == i, jnp.broadcast_to(pr[src:src + 1, :], pr.shape), qr)
            qi = jnp.where(row == i, jnp.broadcast_to(pi[src:src + 1, :], pi.shape), qi)
        pr, pi = qr, qi
    nblk = n_rows // SUBLANES

    def blk(i, carry):
        cr, ci = carry
        b = (nblk - 1 - i) if reverse else i
        sl = pl.ds(pl.multiple_of(b * SUBLANES, SUBLANES), SUBLANES)
        xr, xi = sre_ref[sl, :], sim_ref[sl, :]
        for kk, (er, ei) in ((1, a1), (2, a2), (4, a4)):
            if reverse:
                sr = jnp.where(row < SUBLANES - kk, pltpu.roll(xr, SUBLANES - kk, 0), 0.0)
                si = jnp.where(row < SUBLANES - kk, pltpu.roll(xi, SUBLANES - kk, 0), 0.0)
            else:
                sr = jnp.where(row >= kk, pltpu.roll(xr, kk, 0), 0.0)
                si = jnp.where(row >= kk, pltpu.roll(xi, kk, 0), 0.0)
            dr, di = _cmul(er, sgn * ei, sr, si)
            xr, xi = xr + dr, xi + di
        dr, di = _cmul(pr, sgn * pi, cr, ci)
        xr, xi = xr + dr, xi + di
        sre_ref[sl, :] = xr
        sim_ref[sl, :] = xi
        edge = 0 if reverse else SUBLANES - 1
        return (jnp.broadcast_to(xr[edge:edge + 1, :], xr.shape), jnp.broadcast_to(xi[edge:edge + 1, :], xi.shape))

    cr, ci = lax.fori_loop(0, nblk, blk, (car_re[...], car_im[...]))
    car_re[...] = cr
    car_im[...] = ci


S5_BLK = 4
S5_BN = S5N // S5_BLK


def _s5_project_in(u, bblk_ref, sre_ref, sim_ref):
    for j in range(S5_BLK):
        bu = _hdot(u[:, j * LANES:(j + 1) * LANES], bblk_ref[j])
        sre_ref[:, j * S5_BN:(j + 1) * S5_BN] = bu[:, :S5_BN]
        sim_ref[:, j * S5_BN:(j + 1) * S5_BN] = bu[:, S5_BN:]


def _s5_project_out(u, d, cblk_ref, sre_ref, sim_ref):
    ys = []
    for j in range(S5_BLK):
        sl = slice(j * S5_BN, (j + 1) * S5_BN)
        ys.append(_hdot(sre_ref[:, sl], cblk_ref[j, :S5_BN, :]) + _hdot(sim_ref[:, sl], cblk_ref[j, S5_BN:, :]))
    return jnp.concatenate(ys, axis=1) + d * u


def _s5_fwd(u, bblk, cblk, abr, abi, d):
    length = u.shape[0]
    tl = min(256, length)
    nt = length // tl

    def body(u_ref, b_ref, c_ref, ar_ref, ai_ref, d_ref, ys_ref, kr_ref, ki_ref, sre, sim, car_re, car_im):
        @pl.when(pl.program_id(0) == 0)
        def _():
            car_re[...] = jnp.zeros_like(car_re)
            car_im[...] = jnp.zeros_like(car_im)

        kr_ref[0] = car_re[...]
        ki_ref[0] = car_im[...]
        u_ = u_ref[...]
        _s5_project_in(u_, b_ref, sre, sim)
        _s5_scan(sre, sim, car_re, car_im, ar_ref[...], ai_ref[...], tl, False)
        ys_ref[...] = _gelu(_s5_project_out(u_, d_ref[...], c_ref, sre, sim))

    full = lambda a: pl.BlockSpec(a.shape, lambda i, nd=a.ndim: (0,) * nd)
    chk = pl.BlockSpec((1, SUBLANES, S5N), lambda i: (i, 0, 0))
    return pl.pallas_call(
        body, grid=(nt,), in_specs=[pl.BlockSpec((tl, S5W), lambda i: (i, 0)), full(bblk), full(cblk), full(abr), full(abi), full(d)],
        out_specs=[pl.BlockSpec((tl, S5W), lambda i: (i, 0)), chk, chk],
        out_shape=[SDS((length, S5W), F32), SDS((nt, SUBLANES, S5N), F32), SDS((nt, SUBLANES, S5N), F32)],
        scratch_shapes=[pltpu.VMEM((tl, S5N), F32)] * 2 + [pltpu.VMEM((SUBLANES, S5N), F32)] * 2,
        compiler_params=_params(), name="s5_fwd")(u, bblk, cblk, abr, abi, d)


def _s5_bwd(u, dys_a, dys_b, kre, kim, bblk, cblk, abr, abi, d):
    length = u.shape[0]
    tl = min(256, length)
    nt = length // tl
    tn_dims = (((0,), (0,)), ((), ()))
    nt_dims = (((1,), (1,)), ((), ()))

    def body(u_ref, da_ref, db_ref, kr_ref, ki_ref, b_ref, c_ref, ar_ref, ai_ref, d_ref,
             du_ref, dB_ref, dC_ref, dar_ref, dai_ref, dd_ref, sre, sim, gre, gim, car_re, car_im, dcar_re, dcar_im):
        @pl.when(pl.program_id(0) == 0)
        def _():
            dcar_re[...] = jnp.zeros_like(dcar_re)
            dcar_im[...] = jnp.zeros_like(dcar_im)
            for ref in (dB_ref, dC_ref, dar_ref, dai_ref, dd_ref):
                ref[...] = jnp.zeros_like(ref)

        u_ = u_ref[...]
        abr_, abi_, d_ = ar_ref[...], ai_ref[...], d_ref[...]
        car_re[...] = kr_ref[0]
        car_im[...] = ki_ref[0]
        _s5_project_in(u_, b_ref, sre, sim)
        _s5_scan(sre, sim, car_re, car_im, abr_, abi_, tl, False)
        y = _s5_project_out(u_, d_, c_ref, sre, sim)
        _, vjp = jax.vjp(_gelu, y)
        dy = vjp(da_ref[...] + db_ref[...])[0]
        dd_ref[...] += jnp.sum(dy * u_, axis=0, keepdims=True)
        for j in range(S5_BLK):
            sl = slice(j * S5_BN, (j + 1) * S5_BN)
            dyj = dy[:, j * LANES:(j + 1) * LANES]
            gre[:, sl] = lax.dot_general(dyj, c_ref[j, :S5_BN, :], nt_dims, precision=HI, preferred_element_type=F32)
            gim[:, sl] = lax.dot_general(dyj, c_ref[j, S5_BN:, :], nt_dims, precision=HI, preferred_element_type=F32)
            dC_ref[j, :S5_BN, :] += lax.dot_general(sre[:, sl], dyj, tn_dims, precision=HI, preferred_element_type=F32)
            dC_ref[j, S5_BN:, :] += lax.dot_general(sim[:, sl], dyj, tn_dims, precision=HI, preferred_element_type=F32)
        _s5_scan(gre, gim, dcar_re, dcar_im, abr_, abi_, tl, True)
        gr, gi = gre[...], gim[...]
        pr = _shift_down(sre[...], kr_ref[0, 0:1, :])
        pi = _shift_down(sim[...], ki_ref[0, 0:1, :])
        dar_ref[...] += jnp.sum(gr * pr + gi * pi, axis=0, keepdims=True)
        dai_ref[...] += jnp.sum(gi * pr - gr * pi, axis=0, keepdims=True)
        dus = []
        for j in range(S5_BLK):
            sl = slice(j * S5_BN, (j + 1) * S5_BN)
            uj = u_[:, j * LANES:(j + 1) * LANES]
            dus.append(lax.dot_general(gre[:, sl], b_ref[j, :, :S5_BN], nt_dims, precision=HI, preferred_element_type=F32)
                       + lax.dot_general(gim[:, sl], b_ref[j, :, S5_BN:], nt_dims, precision=HI, preferred_element_type=F32))
            dB_ref[j, :, :S5_BN] += lax.dot_general(uj, gre[:, sl], tn_dims, precision=HI, preferred_element_type=F32)
            dB_ref[j, :, S5_BN:] += lax.dot_general(uj, gim[:, sl], tn_dims, precision=HI, preferred_element_type=F32)
        du_ref[...] = jnp.concatenate(dus, axis=1) + d_ * dy

    rmap = lambda i: nt - 1 - i
    full = lambda a: pl.BlockSpec(a.shape, lambda i, nd=a.ndim: (0,) * nd)
    row = pl.BlockSpec((tl, S5W), lambda i: (rmap(i), 0))
    chk = pl.BlockSpec((1, SUBLANES, S5N), lambda i: (rmap(i), 0, 0))
    return pl.pallas_call(
        body, grid=(nt,), in_specs=[row, row, row, chk, chk, full(bblk), full(cblk), full(abr), full(abi), full(d)],
        out_specs=[row, full(bblk), full(cblk), full(abr), full(abi), full(d)],
        out_shape=[SDS((length, S5W), F32), SDS(bblk.shape, F32), SDS(cblk.shape, F32), SDS(abr.shape, F32),
                   SDS(abi.shape, F32), SDS(d.shape, F32)],
        scratch_shapes=[pltpu.VMEM((tl, S5N), F32)] * 4 + [pltpu.VMEM((SUBLANES, S5N), F32)] * 4,
        compiler_params=_params(), name="s5_bwd")(u, dys_a, dys_b, kre, kim, bblk, cblk, abr, abi, d)


def _s5_blockdiag_in(bb):
    t = bb.reshape(S5C, S5_BLK, S5_BN).transpose(1, 0, 2)
    t = jnp.tile(t, (1, SUBLANES, 1))
    mask = (lax.broadcasted_iota(jnp.int32, (LANES, S5_BN), 0) // S5C) == (lax.broadcasted_iota(jnp.int32, (LANES, S5_BN), 1) // S5P)
    return jnp.where(mask[None], t, 0.0)


def _s5_blockdiag_in_t(dblk):
    t = dblk.reshape(S5_BLK, SUBLANES, S5C, SUBLANES, S5P)
    t = jnp.diagonal(t, axis1=1, axis2=3)
    return t.transpose(1, 0, 3, 2).reshape(S5C, S5N)


def _s5_blockdiag_out(c):
    t = c.reshape(S5_BLK, SUBLANES, S5C, S5P).transpose(0, 3, 1, 2).reshape(S5_BLK, S5P, LANES)
    t = jnp.tile(t, (1, SUBLANES, 1))
    mask = (lax.broadcasted_iota(jnp.int32, (S5_BN, LANES), 0) // S5P) == (lax.broadcasted_iota(jnp.int32, (S5_BN, LANES), 1) // S5C)
    return jnp.where(mask[None], t, 0.0)


def _s5_blockdiag_out_t(dblk):
    t = dblk.reshape(S5_BLK, SUBLANES, S5P, SUBLANES, S5C)
    t = jnp.diagonal(t, axis1=1, axis2=3)
    return t.transpose(0, 3, 2, 1).reshape(S5G, S5C, S5P)


def _wkv_consts():
    lane = lax.broadcasted_iota(jnp.int32, (4 * HN, LANES), 1)
    row = lax.broadcasted_iota(jnp.int32, (4 * HN, LANES), 0)
    diag = ((lane % HN) == (row % HN)).astype(F32)
    ones = _block_ones(LANES, HN).astype(BF16)
    return lane % HN, diag, ones


def _wkv_rowseg(p, ones):
    hi = p.astype(BF16)
    lo = (p - hi.astype(F32)).astype(BF16)
    return jnp.dot(hi, ones, preferred_element_type=F32) + jnp.dot(lo, ones, preferred_element_type=F32)


def _wkv_full(qb_ref, t):
    q = qb_ref[t]
    return jnp.concatenate([jnp.tile(q[:, hp * LANES:(hp + 1) * LANES], (SUBLANES, 1)) for hp in range(4)], axis=0)


def _wkv_colsum8(x):
    return jnp.concatenate([x[hp * HN:(hp + 1) * HN].reshape(SUBLANES, SUBLANES, LANES).sum(axis=0) for hp in range(4)], axis=1)


def _wkv_prebroadcast(src_refs, dst_refs, n):
    for s, dref in zip(src_refs, dst_refs, strict=True):
        for tt in range(n):
            dref[tt] = jnp.broadcast_to(s[tt:tt + 1, :], (SUBLANES, RW))


def _wkv_transpose_out(acc, c, diag):
    lane = lax.broadcasted_iota(jnp.int32, (c, LANES), 1)
    outs = []
    for hp in range(4):
        zp = lax.dot_general(acc[hp * HN:(hp + 1) * HN], diag[:HN], (((0,), (0,)), ((), ())), precision=HI,
                             preferred_element_type=F32)
        outs.append(jnp.where(lane < HN, zp[:c], zp[HN:HN + c]))
    return jnp.concatenate(outs, axis=1)


def _wkv_fwd(r, w, k, v, a, b):
    length = r.shape[0]
    c = min(WKV_CHUNK, length)
    nc = length // c
    ns = 4 * HN

    def body(r_ref, w_ref, k_ref, v_ref, a_ref, b_ref, y_ref, sst_ref, sast_ref, fin_ref,
             s_ref, sa_ref, acc_ref, vst, rb, wb, kb, vb, ab, bb):
        @pl.when(pl.program_id(0) == 0)
        def _():
            s_ref[...] = jnp.zeros_like(s_ref)

        _wkv_prebroadcast((r_ref, w_ref, k_ref, v_ref, a_ref, b_ref), (rb, wb, kb, vb, ab, bb), c)
        lane_t, diag, ones = _wkv_consts()

        def spread_v(t, carry):
            vst[t] = _wkv_rowseg(_wkv_full(vb, t) * diag, ones)
            return carry

        lax.fori_loop(0, c, spread_v, 0)
        acc_ref[...] = jnp.zeros_like(acc_ref)
        sa_ref[...] = _wkv_rowseg(s_ref[...] * _wkv_full(ab, 0), ones)

        def step(t, carry):
            s, sa = s_ref[...], sa_ref[...]
            sst_ref[t] = s
            sast_ref[t] = sa
            s = s * _wkv_full(wb, t) + sa * _wkv_full(bb, t) + vst[t] * _wkv_full(kb, t)
            s_ref[...] = s
            nxt = jnp.minimum(t + 1, c - 1)
            both = _wkv_rowseg(jnp.concatenate([s * _wkv_full(ab, nxt), s * _wkv_full(rb, t)], axis=0), ones)
            sa_ref[...] = both[:ns]
            acc_ref[...] = jnp.where(lane_t == t, both[ns:], acc_ref[...])
            return carry

        lax.fori_loop(0, c, step, 0)
        y_ref[...] = _wkv_transpose_out(acc_ref[...], c, diag)
        fin_ref[...] = s_ref[...]

    row = pl.BlockSpec((c, RW), lambda i: (i, 0))
    st = pl.BlockSpec((c, ns, LANES), lambda i: (i, 0, 0))
    state = pltpu.VMEM((ns, LANES), F32)
    return pl.pallas_call(
        body, grid=(nc,), in_specs=[row] * 6,
        out_specs=[row, st, st, pl.BlockSpec((ns, LANES), lambda i: (0, 0))],
        out_shape=[SDS((length, RW), F32), SDS((length, ns, LANES), F32), SDS((length, ns, LANES), F32), SDS((ns, LANES), F32)],
        scratch_shapes=[state] * 3 + [pltpu.VMEM((c, ns, LANES), F32)] + [pltpu.VMEM((c, SUBLANES, RW), F32)] * 6,
        compiler_params=_params(), name="wkv_fwd")(r, w, k, v, a, b)


def _wkv_bwd(r, w, k, v, a, b, dy, sst, sast, fin):
    length = r.shape[0]
    c = min(WKV_CHUNK, length)
    nc = length // c
    ns = 4 * HN

    def body(r_ref, w_ref, k_ref, v_ref, a_ref, b_ref, dy_ref, sst_ref, sast_ref, snext_ref, fin_ref,
             dr_ref, dw_ref, dk_ref, da_ref, db_ref, dv_ref,
             ds_ref, cur_ref, acc_ref, vst, dyst, rb, wb, kb, vb, ab, bb, dyb, p_r, p_w, p_k, p_a, p_b):
        @pl.when(pl.program_id(0) == 0)
        def _():
            ds_ref[...] = jnp.zeros_like(ds_ref)

        _wkv_prebroadcast((r_ref, w_ref, k_ref, v_ref, a_ref, b_ref, dy_ref), (rb, wb, kb, vb, ab, bb, dyb), c)
        lane_t, diag, ones = _wkv_consts()

        def spread(t, carry):
            both = _wkv_rowseg(jnp.concatenate([_wkv_full(vb, t) * diag, _wkv_full(dyb, t) * diag], axis=0), ones)
            vst[t] = both[:ns]
            dyst[t] = both[ns:]
            return carry

        lax.fori_loop(0, c, spread, 0)
        acc_ref[...] = jnp.zeros_like(acc_ref)
        cur_ref[...] = jnp.where(pl.program_id(0) == 0, fin_ref[...], snext_ref[0])

        def bwd(i, carry):
            t = c - 1 - i
            s_t, s_prev, sa, vv, dyy = cur_ref[...], sst_ref[t], sast_ref[t], vst[t], dyst[t]
            ds = ds_ref[...] + dyy * _wkv_full(rb, t)
            p_r[t] = _wkv_colsum8(s_t * dyy)
            p_k[t] = _wkv_colsum8(ds * vv)
            p_b[t] = _wkv_colsum8(ds * sa)
            p_w[t] = _wkv_colsum8(ds * s_prev)
            both = _wkv_rowseg(jnp.concatenate([ds * _wkv_full(bb, t), ds * _wkv_full(kb, t)], axis=0), ones)
            dsa = both[:ns]
            acc_ref[...] = jnp.where(lane_t == t, both[ns:], acc_ref[...])
            p_a[t] = _wkv_colsum8(s_prev * dsa)
            ds_ref[...] = ds * _wkv_full(wb, t) + dsa * _wkv_full(ab, t)
            cur_ref[...] = s_prev
            return carry

        lax.fori_loop(0, c, bwd, 0)
        sel = (lax.broadcasted_iota(jnp.int32, (c, c * SUBLANES), 1) // SUBLANES
               == lax.broadcasted_iota(jnp.int32, (c, c * SUBLANES), 0)).astype(F32)
        for out_ref, part in ((dr_ref, p_r), (dw_ref, p_w), (dk_ref, p_k), (da_ref, p_a), (db_ref, p_b)):
            out_ref[...] = _hdot(sel, part[...].reshape(c * SUBLANES, RW))
        dv_ref[...] = _wkv_transpose_out(acc_ref[...], c, diag)

    rmap = lambda i: nc - 1 - i
    row = pl.BlockSpec((c, RW), lambda i: (rmap(i), 0))
    st = pl.BlockSpec((c, ns, LANES), lambda i: (rmap(i), 0, 0))
    nxt = pl.BlockSpec((1, ns, LANES), lambda i: (jnp.minimum((rmap(i) + 1) * c, length - 1), 0, 0))
    state = pltpu.VMEM((ns, LANES), F32)
    return pl.pallas_call(
        body, grid=(nc,), in_specs=[row] * 7 + [st, st, nxt, pl.BlockSpec((ns, LANES), lambda i: (0, 0))],
        out_specs=[row] * 6, out_shape=[SDS((length, RW), F32)] * 6,
        scratch_shapes=[state] * 3 + [pltpu.VMEM((c, ns, LANES), F32)] * 2 + [pltpu.VMEM((c, SUBLANES, RW), F32)] * 12,
        compiler_params=_params(), name="wkv_bwd")(r, w, k, v, a, b, dy, sst, sast, sst, fin)


def _my_place():
    return lax.axis_index("x"), lax.axis_index("y"), lax.axis_index("c")


def _peer(x, y, c, k):
    return (x ^ ((k >> 2) & 1), y ^ ((k >> 1) & 1), c ^ (k & 1))


def _all_gather(shard, name):
    rows = shard.shape[0]

    def body(in_ref, out_ref, send_sems, recv_sems, local_sem):
        x, y, c = _my_place()
        me = 4 * x + 2 * y + c
        mine = pltpu.make_async_copy(in_ref, out_ref.at[me], local_sem)
        mine.start()
        sends = []
        for k in range(1, N_DEV):
            cp = pltpu.make_async_remote_copy(src_ref=in_ref, dst_ref=out_ref.at[me], send_sem=send_sems.at[k - 1],
                                              recv_sem=recv_sems.at[k - 1], device_id=_peer(x, y, c, k), device_id_type=MESH)
            cp.start()
            sends.append(cp)
        for k in range(1, N_DEV):
            px, py, pc = _peer(x, y, c, k)
            pltpu.make_async_remote_copy(src_ref=in_ref, dst_ref=out_ref.at[4 * px + 2 * py + pc], send_sem=send_sems.at[k - 1],
                                         recv_sem=recv_sems.at[k - 1], device_id=(px, py, pc), device_id_type=MESH).wait_recv()
        for cp in sends:
            cp.wait_send()
        mine.wait()

    return pl.pallas_call(
        body, out_shape=SDS((N_DEV, rows, LANES), shard.dtype),
        in_specs=[pl.BlockSpec(memory_space=pl.ANY)], out_specs=pl.BlockSpec(memory_space=pl.ANY),
        scratch_shapes=[pltpu.SemaphoreType.DMA((N_DEV - 1,)), pltpu.SemaphoreType.DMA((N_DEV - 1,)), pltpu.SemaphoreType.DMA],
        name=name)(shard)


def _all_to_all(chunks):
    rows = chunks.shape[1]

    def body(in_ref, out_ref, send_sems, recv_sems, local_sem):
        x, y, c = _my_place()
        me = 4 * x + 2 * y + c
        mine = pltpu.make_async_copy(in_ref.at[me], out_ref.at[me], local_sem)
        mine.start()
        sends = []
        for k in range(1, N_DEV):
            px, py, pc = _peer(x, y, c, k)
            cp = pltpu.make_async_remote_copy(src_ref=in_ref.at[4 * px + 2 * py + pc], dst_ref=out_ref.at[me],
                                              send_sem=send_sems.at[k - 1], recv_sem=recv_sems.at[k - 1],
                                              device_id=(px, py, pc), device_id_type=MESH)
            cp.start()
            sends.append(cp)
        for k in range(1, N_DEV):
            px, py, pc = _peer(x, y, c, k)
            pltpu.make_async_remote_copy(src_ref=in_ref.at[me], dst_ref=out_ref.at[4 * px + 2 * py + pc], send_sem=send_sems.at[k - 1],
                                         recv_sem=recv_sems.at[k - 1], device_id=(px, py, pc), device_id_type=MESH).wait_recv()
        for cp in sends:
            cp.wait_send()
        mine.wait()

    return pl.pallas_call(
        body, out_shape=SDS((N_DEV, rows, LANES), chunks.dtype),
        in_specs=[pl.BlockSpec(memory_space=pl.ANY)], out_specs=pl.BlockSpec(memory_space=pl.ANY),
        scratch_shapes=[pltpu.SemaphoreType.DMA((N_DEV - 1,)), pltpu.SemaphoreType.DMA((N_DEV - 1,)), pltpu.SemaphoreType.DMA],
        name="exchange_grads")(chunks)


def _sum_adamw(parts, wgt, m, v):
    rows = wgt.shape[0]
    tr = PACK_ROWS

    def body(p_ref, w_ref, m_ref, v_ref, g_out, d_out, m_out, v_out):
        g = p_ref[0]
        for s in range(1, N_DEV):
            g = g + p_ref[s]
        m_new = ADAM_B1 * m_ref[...] + (1.0 - ADAM_B1) * g
        v_new = ADAM_B2 * v_ref[...] + (1.0 - ADAM_B2) * (g * g)
        m_hat = m_new / (1.0 - ADAM_B1 ** ADAM_STEP)
        v_hat = v_new / (1.0 - ADAM_B2 ** ADAM_STEP)
        g_out[...] = g
        d_out[...] = -ADAM_LR * (m_hat / (jnp.sqrt(v_hat) + ADAM_EPS) + ADAM_WD * w_ref[...])
        m_out[...] = m_new
        v_out[...] = v_new

    blk = pl.BlockSpec((tr, LANES), lambda i: (i, 0))
    return pl.pallas_call(
        body, grid=(rows // tr,), in_specs=[pl.BlockSpec((N_DEV, tr, LANES), lambda i: (0, i, 0)), blk, blk, blk],
        out_specs=[blk] * 4, out_shape=[SDS((rows, LANES), F32)] * 4,
        compiler_params=_params(("parallel",)), name="sum_adamw")(parts, wgt, m, v)


PACK_ALIGN = 2 * SUBLANES * LANES
PACK_ROWS = 512

SHARDED = {"w_in": 1, "rwkv_w2": 1, "rwkv_a2": 1, "rwkv_g2": 1, "s5_w_glu": 0, "w_branch_rwkv": 1, "w_branch_s5": 1,
           "w_out": 0, "ffn_w_up": 1, "ffn_conv_w": 1, "ffn_w_down": 0}
F32_GATHER = ("ffn_conv_w",)
REPLICATED = ("norm_mix_pre", "norm_mix_post", "norm_ffn_pre", "norm_ffn_post", "b_gate", "rwkv_shift_mu", "rwkv_w0",
              "rwkv_a0", "rwkv_k_k", "rwkv_k_a", "rwkv_r_k", "rwkv_lnx_w", "rwkv_lnx_b", "s5_a_re", "s5_a_im", "s5_b_re",
              "s5_b_im", "s5_c_re", "s5_c_im", "s5_d", "s5_log_step", "s5_b_glu", "ffn_conv_b")
WEIGHTS = ("norm_mix_pre", "norm_mix_post", "norm_ffn_pre", "norm_ffn_post", "w_in", "b_gate", "rwkv_shift_mu", "rwkv_w0",
           "rwkv_w2", "rwkv_a0", "rwkv_a2", "rwkv_g2", "rwkv_k_k", "rwkv_k_a", "rwkv_r_k", "rwkv_lnx_w", "rwkv_lnx_b",
           "s5_a_re", "s5_a_im", "s5_b_re", "s5_b_im", "s5_c_re", "s5_c_im", "s5_d", "s5_log_step", "s5_w_glu", "s5_b_glu",
           "w_branch_rwkv", "w_branch_s5", "w_out", "ffn_w_up", "ffn_conv_w", "ffn_conv_b", "ffn_w_down")


def _pad_flat(a):
    flat = a.reshape(-1)
    pad = (-flat.shape[0]) % PACK_ALIGN
    return jnp.pad(flat, (0, pad)) if pad else flat


def _pad_cols(a, mult):
    pad = (-a.shape[1]) % mult
    return jnp.pad(a, ((0, 0), (0, pad))) if pad else a


def _pack(pieces):
    flat = jnp.concatenate([_pad_flat(p) for p in pieces])
    pad = (-flat.shape[0]) % (PACK_ROWS * LANES)
    return (jnp.pad(flat, (0, pad)) if pad else flat).reshape(-1, LANES)


def _unpack(buf, shapes):
    lead = buf.shape[:-2]
    flat = buf.reshape(*lead, -1)
    out, off = [], 0
    for s in shapes:
        n = math.prod(s)
        out.append(flat[..., off:off + n].reshape(*lead, *s))
        off += n + (-n) % PACK_ALIGN
    return out


def _unshard(stack, axis):
    if axis == 0:
        return stack.reshape(-1, stack.shape[2])
    return stack.transpose(1, 0, 2).reshape(stack.shape[1], -1)


def _reshard(full, axis):
    if axis == 0:
        return full.reshape(N_DEV, -1, full.shape[1])
    return full.reshape(full.shape[0], N_DEV, -1).transpose(1, 0, 2)


def _local_step(x, tgt, wt):
    o512 = _block_ones(RW, HN)
    w_in = wt["w_in"]
    w_rw, w_u, w_g = w_in[:, :N_RWKV], w_in[:, N_RWKV:N_RWKV + S5W], w_in[:, N_RWKV + S5W:]
    b_gate = wt["b_gate"]
    b_r, b_s = b_gate[:, :D_MODEL], b_gate[:, D_MODEL:]
    zpad = lambda a, lo, n: jnp.pad(a.astype(F32), ((lo, n - lo - a.shape[0]), (0, 0)))
    w2p, a2p, g2p = zpad(wt["rwkv_w2"], 0, 256), zpad(wt["rwkv_a2"], 64, 256), zpad(wt["rwkv_g2"], 128, 256)
    pre_small = [wt["rwkv_w0"], wt["rwkv_a0"], wt["rwkv_k_k"], wt["rwkv_k_a"], w2p, a2p, g2p]
    post_prm = [wt["rwkv_lnx_w"], wt["rwkv_lnx_b"], wt["rwkv_r_k"].reshape(1, RW)]
    mu = wt["rwkv_shift_mu"]

    a_re, a_im = wt["s5_a_re"].reshape(1, S5N), wt["s5_a_im"].reshape(1, S5N)
    ls = jnp.repeat(wt["s5_log_step"].reshape(S5G), S5P).reshape(1, S5N)
    b_re_t = wt["s5_b_re"].reshape(S5N, S5C).T
    b_im_t = wt["s5_b_im"].reshape(S5N, S5C).T
    c_re, c_im = wt["s5_c_re"].reshape(S5G, S5C, S5P), wt["s5_c_im"].reshape(S5G, S5C, S5P)
    abr, abi, bbr, bbi = _s5_prep_fwd(a_re, a_im, ls, b_re_t, b_im_t)
    bblk = jnp.concatenate([_s5_blockdiag_in(bbr), _s5_blockdiag_in(bbi)], axis=2)
    cblk = jnp.concatenate([_s5_blockdiag_out(c_re), -_s5_blockdiag_out(c_im)], axis=1)
    s5_d = wt["s5_d"]

    h1 = _rms_fwd(x, wt["norm_mix_pre"], "rms1_fwd")
    p_rw = _mm(h1, w_rw, name="mm_proj_rwkv")
    u = _mm(h1, w_u, name="mm_proj_s5")
    gpre = _mm(h1, w_g, name="mm_proj_gate")
    r, decay, k2, v, aa, bb, g = _rwkv_pre_fwd(p_rw, mu, pre_small, o512)
    y, sst, sast, s_fin = _wkv_fwd(r, decay, k2, v, aa, bb)
    y_r = _rwkv_post_fwd(y, r, k2, v, g, post_prm, o512)
    o_r = _mm(y_r, wt["w_branch_rwkv"], name="mm_branch_rwkv")
    ys, kre, kim = _s5_fwd(u, bblk, cblk, abr, abi, s5_d)
    t_glu = _mm(ys, wt["s5_w_glu"], name="mm_glu")
    out5 = _glu_fwd(ys, t_glu, wt["s5_b_glu"])
    o_s = _mm(out5, wt["w_branch_s5"], name="mm_branch_s5")
    mi = _merge_fwd(gpre, o_r, o_s, b_r, b_s)
    mixed = _mm(mi, wt["w_out"], name="mm_out")
    x2, h2 = _mid_fwd(x, mixed, wt["norm_mix_post"], wt["norm_ffn_pre"])
    z = _mm(h2, wt["ffn_w_up"], name="mm_up")
    act = _conv_fwd(z, wt["ffn_conv_w"], wt["ffn_conv_b"])
    f = _mm(act, wt["ffn_w_down"], name="mm_down")
    dx3, df, d_norm_ffn_post, loss_part = _final(x2, f, tgt, wt["norm_ffn_post"])

    gr = {"norm_ffn_post": d_norm_ffn_post}
    dact = _mm(df, wt["ffn_w_down"], tb=True, name="mm_down_dx")
    gr["ffn_w_down"] = _mm(act, df, ta=True, name="mm_down_dw")
    dzg, dzv, gr["ffn_conv_w"], gr["ffn_conv_b"] = _conv_bwd(z, dact, wt["ffn_conv_w"], wt["ffn_conv_b"])
    dz = jnp.concatenate([dzg, dzv], axis=1)
    dh2 = _mm(dz, wt["ffn_w_up"], tb=True, name="mm_up_dx")
    gr["ffn_w_up"] = _mm(h2, dz, ta=True, name="mm_up_dw")
    dx2, dmixed, gr["norm_mix_post"], gr["norm_ffn_pre"] = _mid_bwd(x2, mixed, dh2, dx3, wt["norm_mix_post"], wt["norm_ffn_pre"])
    dmi = _mm(dmixed, wt["w_out"], tb=True, name="mm_out_dx")
    gr["w_out"] = _mm(mi, dmixed, ta=True, name="mm_out_dw")
    dgp_r, dgp_s, do_r, do_s, db_r, db_s = _merge_bwd(gpre, o_r, o_s, dmi, b_r, b_s)
    gr["b_gate"] = jnp.concatenate([db_r, db_s], axis=1)
    dout5 = _mm(do_s, wt["w_branch_s5"], tb=True, name="mm_branch_s5_dx")
    gr["w_branch_s5"] = _mm(out5, do_s, ta=True, name="mm_branch_s5_dw")
    dys_a, dt_glu, gr["s5_b_glu"] = _glu_bwd(ys, t_glu, dout5, wt["s5_b_glu"])
    dys_b = _mm(dt_glu, wt["s5_w_glu"], tb=True, name="mm_glu_dx")
    gr["s5_w_glu"] = _mm(ys, dt_glu, ta=True, name="mm_glu_dw")
    du, dbblk, dcblk, dabr, dabi, gr["s5_d"] = _s5_bwd(u, dys_a, dys_b, kre, kim, bblk, cblk, abr, abi, s5_d)
    gr["s5_c_re"] = _s5_blockdiag_out_t(dcblk[:, :S5_BN, :]).reshape(wt["s5_c_re"].shape)
    gr["s5_c_im"] = (-_s5_blockdiag_out_t(dcblk[:, S5_BN:, :])).reshape(wt["s5_c_im"].shape)
    dbbr, dbbi = _s5_blockdiag_in_t(dbblk[:, :, :S5_BN]), _s5_blockdiag_in_t(dbblk[:, :, S5_BN:])
    gsel = (lax.broadcasted_iota(jnp.int32, (S5N, LANES), 0) // S5P == lax.broadcasted_iota(jnp.int32, (S5N, LANES), 1)).astype(F32)
    d_are, d_aim, d_ls, d_bre_t, d_bim_t = _s5_prep_bwd(a_re, a_im, ls, b_re_t, b_im_t, (dabr, dabi, dbbr, dbbi), gsel)
    gr["s5_a_re"] = d_are.reshape(wt["s5_a_re"].shape)
    gr["s5_a_im"] = d_aim.reshape(wt["s5_a_im"].shape)
    gr["s5_log_step"] = d_ls[0:1, :S5G]
    gr["s5_b_re"] = d_bre_t.T.reshape(wt["s5_b_re"].shape)
    gr["s5_b_im"] = d_bim_t.T.reshape(wt["s5_b_im"].shape)
    dy_r = _mm(do_r, wt["w_branch_rwkv"], tb=True, name="mm_branch_rwkv_dx")
    gr["w_branch_rwkv"] = _mm(y_r, do_r, ta=True, name="mm_branch_rwkv_dw")
    dy, dr1, dk1, dv1, dg, gr["rwkv_lnx_w"], gr["rwkv_lnx_b"], d_rk = _rwkv_post_bwd(y, r, k2, v, g, dy_r, post_prm, o512)
    gr["rwkv_r_k"] = d_rk.reshape(wt["rwkv_r_k"].shape)
    dr2, ddecay, dk2, daa, dbb, dv2 = _wkv_bwd(r, decay, k2, v, aa, bb, dy, sst, sast, s_fin)
    cots = [dr1 + dr2, ddecay, dk1 + dk2, dv1 + dv2, daa, dbb, dg]
    dp_rw, gr["rwkv_shift_mu"], gr["rwkv_w0"], gr["rwkv_a0"], gr["rwkv_k_k"], gr["rwkv_k_a"], dw2p, da2p, dg2p = \
        _rwkv_pre_bwd(p_rw, cots, mu, pre_small, o512)
    gr["rwkv_w2"], gr["rwkv_a2"], gr["rwkv_g2"] = dw2p[:64], da2p[64:128], dg2p[128:]
    dproj = jnp.concatenate([dp_rw, du, dgp_r, dgp_s], axis=1)
    dh1 = _mm(dproj, w_in, tb=True, name="mm_proj_dx")
    gr["w_in"] = _mm(h1, dproj, ta=True, name="mm_proj_dw")
    dx, gr["norm_mix_pre"] = _rms_bwd(x, wt["norm_mix_pre"], dh1, dx2, "rms1_bwd")
    return loss_part[0, 0], dx, gr


def kernel(x, norm_mix_pre, norm_mix_post, norm_ffn_pre, norm_ffn_post, w_in, b_gate, rwkv_shift_mu, rwkv_w0, rwkv_w2, rwkv_a0, rwkv_a2, rwkv_g2, rwkv_k_k, rwkv_k_a, rwkv_r_k, rwkv_lnx_w, rwkv_lnx_b, s5_a_re, s5_a_im, s5_b_re, s5_b_im, s5_c_re, s5_c_im, s5_d, s5_log_step, s5_w_glu, s5_b_glu, w_branch_rwkv, w_branch_s5, w_out, ffn_w_up, ffn_conv_w, ffn_conv_b, ffn_w_down, loss_target, m_norm_mix_pre, m_norm_mix_post, m_norm_ffn_pre, m_norm_ffn_post, m_w_in, m_b_gate, m_rwkv_shift_mu, m_rwkv_w0, m_rwkv_w2, m_rwkv_a0, m_rwkv_a2, m_rwkv_g2, m_rwkv_k_k, m_rwkv_k_a, m_rwkv_r_k, m_rwkv_lnx_w, m_rwkv_lnx_b, m_s5_a_re, m_s5_a_im, m_s5_b_re, m_s5_b_im, m_s5_c_re, m_s5_c_im, m_s5_d, m_s5_log_step, m_s5_w_glu, m_s5_b_glu, m_w_branch_rwkv, m_w_branch_s5, m_w_out, m_ffn_w_up, m_ffn_conv_w, m_ffn_conv_b, m_ffn_w_down, v_norm_mix_pre, v_norm_mix_post, v_norm_ffn_pre, v_norm_ffn_post, v_w_in, v_b_gate, v_rwkv_shift_mu, v_rwkv_w0, v_rwkv_w2, v_rwkv_a0, v_rwkv_a2, v_rwkv_g2, v_rwkv_k_k, v_rwkv_k_a, v_rwkv_r_k, v_rwkv_lnx_w, v_rwkv_lnx_b, v_s5_a_re, v_s5_a_im, v_s5_b_re, v_s5_b_im, v_s5_c_re, v_s5_c_im, v_s5_d, v_s5_log_step, v_s5_w_glu, v_s5_b_glu, v_w_branch_rwkv, v_w_branch_s5, v_w_out, v_ffn_w_up, v_ffn_conv_w, v_ffn_conv_b, v_ffn_w_down):
    args = dict(locals())
    wgt = {n: args[n] for n in WEIGHTS}
    mom = {n: args["m_" + n] for n in WEIGHTS}
    var = {n: args["v_" + n] for n in WEIGHTS}
    sharded = list(SHARDED)
    shard_shapes = {n: wgt[n].shape[1:] for n in sharded}

    bf_names = [n for n in sharded if n not in F32_GATHER]
    got_bf = _unpack(_all_gather(_pack([wgt[n][0].astype(BF16) for n in bf_names]), "all_gather_weights"),
                     [shard_shapes[n] for n in bf_names])
    got_f32 = _unpack(_all_gather(_pack([wgt[n][0] for n in F32_GATHER]), "all_gather_taps"),
                      [shard_shapes[n] for n in F32_GATHER])
    full = {n: _unshard(blocks, SHARDED[n]) for n, blocks in zip(bf_names + list(F32_GATHER), got_bf + got_f32, strict=True)}
    wt = {n: (full[n] if n in SHARDED else wgt[n]) for n in WEIGHTS}

    loss_part, dx, gr = _local_step(x[0], loss_target[0], wt)

    rep_flat = [gr[n].reshape(-1) for n in REPLICATED]
    rep_shapes = [(g_.shape[0],) for g_ in rep_flat]
    pieces = [_pad_cols(_reshard(gr[n], SHARDED[n]).reshape(N_DEV, -1), PACK_ALIGN) for n in sharded]
    pieces += [jnp.broadcast_to(_pad_flat(g_)[None], (N_DEV, g_.shape[0] + (-g_.shape[0]) % PACK_ALIGN)) for g_ in rep_flat]
    chunks = _pad_cols(jnp.concatenate(pieces, axis=1), PACK_ROWS * LANES).reshape(N_DEV, -1, LANES)
    parts = _all_to_all(chunks)

    pack_local = lambda src: _pack([src[n][0] for n in sharded] + [src[n].reshape(-1) for n in REPLICATED])
    g_sum, delta, m_new, v_new = _sum_adamw(parts, pack_local(wgt), pack_local(mom), pack_local(var))
    shapes = [shard_shapes[n] for n in sharded] + rep_shapes
    names = sharded + list(REPLICATED)

    def unpack_all(buf):
        vals = dict(zip(names, _unpack(buf, shapes), strict=True))
        return [vals[n].reshape(wgt[n].shape) for n in WEIGHTS]

    loss = lax.psum(loss_part, AXES)
    return (loss, dx[None], *unpack_all(g_sum), *unpack_all(delta), *unpack_all(m_new), *unpack_all(v_new))
```

```python
import functools
import math

import jax
import jax.numpy as jnp
from jax import lax
from jax.experimental import pallas as pl
from jax.experimental.pallas import tpu as pltpu

F32 = jnp.float32
BF16 = jnp.bfloat16
SDS = jax.ShapeDtypeStruct
HI = lax.Precision.HIGHEST
MESH = pl.DeviceIdType.MESH
AXES = ("x", "y", "c")
N_DEV = 8

D_MODEL = 1024
RW = 512
HN = 64
N_RWKV = 1792
S5W = 512
S5G = 32
S5C = 16
S5P = 64
S5N = S5G * S5P
D_FF = 2816
NORM_EPS = 1e-6
LNX_EPS = 64e-5

ADAM_LR = 0.001
ADAM_B1 = 0.9
ADAM_B2 = 0.999
ADAM_EPS = 1e-08
ADAM_WD = 0.01
ADAM_STEP = 10

LANES = 128
SUBLANES = 8
VMEM_LIMIT = 56 * 1024 * 1024
WKV_CHUNK = 32


def _params(sem=("arbitrary",)):
    return pltpu.CompilerParams(dimension_semantics=sem, vmem_limit_bytes=VMEM_LIMIT)


def _pick(n, cap):
    best = None
    for t in range(LANES, min(n, cap) + 1, LANES):
        if n % t == 0:
            best = t
    return best or n


def _mm(a, b, *, ta=False, tb=False, out_dtype=F32, name):
    m = a.shape[1] if ta else a.shape[0]
    k = a.shape[0] if ta else a.shape[1]
    n = b.shape[0] if tb else b.shape[1]
    assert (b.shape[1] if tb else b.shape[0]) == k
    tm, tn, tk = _pick(m, 512), _pick(n, 2304), _pick(k, 512)
    nk = k // tk
    dims = (((0 if ta else 1,), (1 if tb else 0,)), ((), ()))

    def body(a_ref, b_ref, o_ref, acc_ref):
        kk = pl.program_id(2)

        @pl.when(kk == 0)
        def _():
            acc_ref[...] = jnp.zeros_like(acc_ref)

        acc_ref[...] += lax.dot_general(a_ref[...].astype(BF16), b_ref[...].astype(BF16), dims,
                                        preferred_element_type=F32)

        @pl.when(kk == nk - 1)
        def _():
            o_ref[...] = acc_ref[...].astype(o_ref.dtype)

    a_spec = pl.BlockSpec((tk, tm), lambda i, j, kk: (kk, i)) if ta else pl.BlockSpec((tm, tk), lambda i, j, kk: (i, kk))
    b_spec = pl.BlockSpec((tn, tk), lambda i, j, kk: (j, kk)) if tb else pl.BlockSpec((tk, tn), lambda i, j, kk: (kk, j))
    return pl.pallas_call(
        body, grid=(m // tm, n // tn, nk), in_specs=[a_spec, b_spec],
        out_specs=pl.BlockSpec((tm, tn), lambda i, j, kk: (i, j)),
        out_shape=SDS((m, n), out_dtype), scratch_shapes=[pltpu.VMEM((tm, tn), F32)],
        compiler_params=_params(("parallel", "parallel", "arbitrary")), name=name)(a, b)


def _rows(fn, rows, params, out_rows, out_accs, *, name, tl, reverse=False, scratch=()):
    first = rows[0][0] if isinstance(rows[0], tuple) else rows[0]
    length = first.shape[0]
    tl = min(tl, length)
    nt = length // tl
    rmap = (lambda i: nt - 1 - i) if reverse else (lambda i: i)
    specs, arrs = [], []
    for r in rows:
        arr, wdt, cb = r if isinstance(r, tuple) else (r, r.shape[1], 0)
        specs.append(pl.BlockSpec((tl, wdt), lambda i, cb=cb: (rmap(i), cb)))
        arrs.append(arr)
    for p in params:
        specs.append(pl.BlockSpec(p.shape, lambda i, nd=p.ndim: (0,) * nd))
        arrs.append(p)
    out_shape = [SDS((length, c), dt) for c, dt in out_rows] + [SDS(s, F32) for s in out_accs]
    out_specs = [pl.BlockSpec((tl, c), lambda i: (rmap(i), 0)) for c, _ in out_rows]
    out_specs += [pl.BlockSpec(s, lambda i, nd=len(s): (0,) * nd) for s in out_accs]
    nr, npar, nor, noa = len(rows), len(params), len(out_rows), len(out_accs)

    def body(*refs):
        rin, pin = refs[:nr], refs[nr:nr + npar]
        rout = refs[nr + npar:nr + npar + nor]
        aout = refs[nr + npar + nor:nr + npar + nor + noa]
        scr = refs[nr + npar + nor + noa:]
        step = pl.program_id(0)
        outs_r, outs_a = fn(step, [r[...] for r in rin], [p[...] for p in pin], scr)
        for ref, val in zip(rout, outs_r, strict=True):
            ref[...] = val.astype(ref.dtype)

        @pl.when(step == 0)
        def _():
            for ref in aout:
                ref[...] = jnp.zeros_like(ref)

        for ref, val in zip(aout, outs_a, strict=True):
            ref[...] += val.astype(F32)

    res = pl.pallas_call(body, grid=(nt,), in_specs=specs, out_specs=out_specs, out_shape=out_shape,
                         scratch_shapes=list(scratch), compiler_params=_params(), name=name)(*arrs)
    return list(res)


def _rms(x, g):
    return x * lax.rsqrt(jnp.mean(x * x, axis=-1, keepdims=True) + NORM_EPS) * g


def _sig(x):
    return 0.5 * (jnp.tanh(0.5 * x) + 1.0)


def _softplus(x):
    return jnp.maximum(x, 0.0) + jnp.log(1.0 + jnp.exp(-jnp.abs(x)))


def _gelu(x):
    return x * (0.5 * (1.0 + jnp.tanh(math.sqrt(2.0 / math.pi) * (x + 0.044715 * (x * x * x)))))


def _bdot(a, b):
    return jnp.dot(a.astype(BF16), b.astype(BF16), preferred_element_type=F32)


def _hdot(a, b):
    return jnp.dot(a, b, precision=HI, preferred_element_type=F32)


def _block_ones(n, blk):
    i = lax.broadcasted_iota(jnp.int32, (n, n), 0) // blk
    j = lax.broadcasted_iota(jnp.int32, (n, n), 1) // blk
    return (i == j).astype(F32)


def _rms_fwd(x, g, name):
    return _rows(lambda s, r, p, _: ([_rms(r[0], p[0])], []), [x], [g], [(x.shape[1], BF16)], [], name=name, tl=512)[0]


def _rms_bwd(x, g, dh, dres, name):
    def fn(s, r, p, _):
        _, vjp = jax.vjp(_rms, r[0], p[0])
        dx, dg = vjp(r[1])
        return [dx + r[2]], [dg]
    return _rows(fn, [x, dh, dres], [g], [(x.shape[1], F32)], [g.shape], name=name, tl=256)


def _rwkv_pre_math(k_, lr, w0, a0, k_k, k_a, w2p, a2p, g2p, o512):
    pre_w = w0 + _bdot(jnp.tanh(lr), w2p)
    w = -_softplus(-pre_w) - 0.5
    decay = jnp.exp(-jnp.exp(w))
    a = _sig(a0 + _bdot(lr, a2p))
    g = _bdot(_sig(lr), g2p)
    kr = k_ * k_k
    kk = kr / jnp.maximum(jnp.sqrt(_hdot(kr * kr, o512)), 1e-12)
    k2 = k_ * (1.0 + (a - 1.0) * k_a)
    return decay, k2, -kk, kk * a, g


def _shift_down(p, prev_row):
    row = lax.broadcasted_iota(jnp.int32, p.shape, 0)
    return jnp.where(row == 0, jnp.broadcast_to(prev_row, p.shape), pltpu.roll(p, 1, 0))


def _shift_up(q, next_row):
    n = q.shape[0]
    row = lax.broadcasted_iota(jnp.int32, q.shape, 0)
    return jnp.where(row == n - 1, jnp.broadcast_to(next_row, q.shape), pltpu.roll(q, n - 1, 0))


def _rwkv_pre_fwd(p, mu, small, o512):
    def fn(step, r, prm, scr):
        car = scr[0]

        @pl.when(step == 0)
        def _():
            car[...] = jnp.zeros_like(car)

        x = r[0]
        prev = _shift_down(x, car[SUBLANES - 1:SUBLANES, :])
        car[...] = x[x.shape[0] - SUBLANES:, :]
        xs = x + (prev - x) * prm[0]
        decay, k2, aa, bb, g = _rwkv_pre_math(xs[:, RW:2 * RW], xs[:, 3 * RW:], *prm[1:])
        return [xs[:, :RW], decay, k2, xs[:, 2 * RW:3 * RW], aa, bb, g], []
    return _rows(fn, [p], [mu, *small, o512], [(RW, F32)] * 7, [], name="rwkv_pre_fwd", tl=256,
                 scratch=[pltpu.VMEM((SUBLANES, N_RWKV), F32)])


def _rwkv_pre_bwd(p, cots, mu, small, o512):
    length = p.shape[0]
    tl = min(256, length)
    nt = length // tl
    rows_per = tl // SUBLANES
    params = [mu, *small, o512]
    acc_shapes = [mu.shape] + [q.shape for q in small]
    nr, npar, nacc = 2 + len(cots), len(params), len(acc_shapes)

    def body(*refs):
        rin, pin = refs[:nr], refs[nr:nr + npar]
        dp_ref = refs[nr + npar]
        aout = refs[nr + npar + 1:nr + npar + 1 + nacc]
        car_q = refs[nr + npar + 1 + nacc]
        step = pl.program_id(0)

        @pl.when(step == 0)
        def _():
            car_q[...] = jnp.zeros_like(car_q)
            for ref in aout:
                ref[...] = jnp.zeros_like(ref)

        x = rin[0][...]
        prev_row = jnp.where(step == nt - 1, 0.0, rin[1][SUBLANES - 1:SUBLANES, :])
        dr, ddecay, dk2, dv, daa, dbb, dg = [r[...] for r in rin[2:]]
        prm = [q[...] for q in pin]
        mu_, o512_ = prm[0], prm[-1]
        prev = _shift_down(x, prev_row)
        xs = x + (prev - x) * mu_
        _, vjp = jax.vjp(lambda k_, lr, *w: _rwkv_pre_math(k_, lr, *w, o512_), xs[:, RW:2 * RW], xs[:, 3 * RW:], *prm[1:-1])
        dk_, dlr, *dsmall = vjp((ddecay, dk2, daa, dbb, dg))
        dxs = jnp.concatenate([dr, dk_, dv, dlr], axis=1)
        q = dxs * mu_
        dp_ref[...] = dxs - q + _shift_up(q, car_q[0:1, :])
        car_q[...] = q[:SUBLANES, :]
        aout[0][...] += jnp.sum((prev - x) * dxs, axis=0, keepdims=True)
        for ref, val in zip(aout[1:], dsmall, strict=True):
            ref[...] += val

    rmap = lambda i: nt - 1 - i
    specs = [pl.BlockSpec((tl, N_RWKV), lambda i: (rmap(i), 0)),
             pl.BlockSpec((SUBLANES, N_RWKV), lambda i: (jnp.maximum(rmap(i) * rows_per - 1, 0), 0))]
    specs += [pl.BlockSpec((tl, RW), lambda i: (rmap(i), 0)) for _ in cots]
    specs += [pl.BlockSpec(q.shape, lambda i, nd=q.ndim: (0,) * nd) for q in params]
    out_shape = [SDS((length, N_RWKV), F32)] + [SDS(sh, F32) for sh in acc_shapes]
    out_specs = [pl.BlockSpec((tl, N_RWKV), lambda i: (rmap(i), 0))]
    out_specs += [pl.BlockSpec(sh, lambda i, nd=len(sh): (0,) * nd) for sh in acc_shapes]
    res = pl.pallas_call(body, grid=(nt,), in_specs=specs, out_specs=out_specs, out_shape=out_shape,
                         scratch_shapes=[pltpu.VMEM((SUBLANES, N_RWKV), F32)],
                         compiler_params=_params(), name="rwkv_pre_bwd")(p, p, *cots, *params)
    return list(res)


def _rwkv_post_math(y, r, k2, v, g, lnx_w, lnx_b, r_k, o512):
    mean = _hdot(y, o512) * (1.0 / HN)
    yc = y - mean
    var = _hdot(yc * yc, o512) * (1.0 / HN)
    yn = yc * lax.rsqrt(var + LNX_EPS) * lnx_w + lnx_b
    bonus = _hdot(r * k2 * r_k, o512) * v
    return (yn + bonus) * g


def _rwkv_post_fwd(y, r, k2, v, g, prm, o512):
    return _rows(lambda s, rr, p, _: ([_rwkv_post_math(*rr, *p)], []), [y, r, k2, v, g], [*prm, o512],
                 [(RW, BF16)], [], name="rwkv_post_fwd", tl=256)[0]


def _rwkv_post_bwd(y, r, k2, v, g, dout, prm, o512):
    def fn(s, rr, p, _):
        o = p[-1]
        _, vjp = jax.vjp(lambda *a: _rwkv_post_math(*a, o), *rr[:5], *p[:-1])
        gr = vjp(rr[5])
        return list(gr[:5]), list(gr[5:])
    return _rows(fn, [y, r, k2, v, g, dout], [*prm, o512], [(RW, F32)] * 5, [q.shape for q in prm],
                 name="rwkv_post_bwd", tl=256)


def _glu_math(ys, t, b):
    return ys * _sig(t + b)


def _glu_fwd(ys, t, b):
    return _rows(lambda s, r, p, _: ([_glu_math(r[0], r[1], p[0])], []), [ys, t], [b], [(S5W, BF16)], [],
                 name="s5_glu_fwd", tl=512)[0]


def _glu_bwd(ys, t, dout, b):
    def fn(s, r, p, _):
        _, vjp = jax.vjp(_glu_math, r[0], r[1], p[0])
        dys, dt, db = vjp(r[2])
        return [dys, dt], [db]
    return _rows(fn, [ys, t, dout], [b], [(S5W, F32)] * 2, [b.shape], name="s5_glu_bwd", tl=512)


def _merge_math(gp_r, gp_s, o_r, o_s, b_r, b_s):
    return _sig(gp_r + b_r) * o_r + _sig(gp_s + b_s) * o_s


def _merge_fwd(gpre, o_r, o_s, b_r, b_s):
    return _rows(lambda s, r, p, _: ([_merge_math(*r, *p)], []),
                 [(gpre, D_MODEL, 0), (gpre, D_MODEL, 1), o_r, o_s], [b_r, b_s], [(D_MODEL, BF16)], [],
                 name="merge_fwd", tl=256)[0]


def _merge_bwd(gpre, o_r, o_s, dmi, b_r, b_s):
    def fn(s, r, p, _):
        _, vjp = jax.vjp(_merge_math, *r[:4], *p)
        dgr, dgs, dor, dos, dbr, dbs = vjp(r[4])
        return [dgr, dgs, dor, dos], [dbr, dbs]
    return _rows(fn, [(gpre, D_MODEL, 0), (gpre, D_MODEL, 1), o_r, o_s, dmi], [b_r, b_s], [(D_MODEL, F32)] * 4,
                 [b_r.shape, b_s.shape], name="merge_bwd", tl=256)


def _mid_fwd(x, mixed, g_post, g_pre):
    def fn(s, r, p, _):
        x2 = r[0] + _rms(r[1], p[0])
        return [x2, _rms(x2, p[1])], []
    return _rows(fn, [x, mixed], [g_post, g_pre], [(D_MODEL, F32), (D_MODEL, BF16)], [], name="mid_fwd", tl=256)


def _mid_bwd(x2, mixed, dh2, dx3, g_post, g_pre):
    def fn(s, r, p, _):
        _, vjp1 = jax.vjp(_rms, r[0], p[1])
        dx2, dg_pre = vjp1(r[2])
        dx2 = dx2 + r[3]
        _, vjp2 = jax.vjp(_rms, r[1], p[0])
        dmixed, dg_post = vjp2(dx2)
        return [dx2, dmixed], [dg_post, dg_pre]
    return _rows(fn, [x2, mixed, dh2, dx3], [g_post, g_pre], [(D_MODEL, F32)] * 2, [g_post.shape, g_pre.shape],
                 name="mid_bwd", tl=256)


def _final(x2, f, tgt, g_post):
    def fn(s, r, p, _):
        y, vjp = jax.vjp(_rms, r[1], p[0])
        diff = r[0] + y - r[2]
        dx3 = diff * (1.0 / D_MODEL)
        df, dg = vjp(dx3)
        part = 0.5 * jnp.sum(jnp.sum(diff * diff, axis=1, keepdims=True), axis=0, keepdims=True) * (1.0 / D_MODEL)
        return [dx3, df], [dg, jnp.broadcast_to(part, (1, LANES))]
    return _rows(fn, [x2, f, tgt], [g_post], [(D_MODEL, F32)] * 2, [g_post.shape, (1, LANES)], name="final", tl=256)


def _conv_taps(z, car):
    row = lax.broadcasted_iota(jnp.int32, z.shape, 0)
    z1 = jnp.where(row == 0, jnp.broadcast_to(car[7:8, :], z.shape), pltpu.roll(z, 1, 0))
    z2 = pltpu.roll(z, 2, 0)
    z2 = jnp.where(row == 0, jnp.broadcast_to(car[6:7, :], z.shape), z2)
    z2 = jnp.where(row == 1, jnp.broadcast_to(car[7:8, :], z.shape), z2)
    return z1, z2


def _conv(z, car, w, b):
    z1, z2 = _conv_taps(z, car)
    return b + w[0:1, :] * z2 + w[1:2, :] * z1 + w[2:3, :] * z, z1, z2


def _conv_fwd(z, conv_w, conv_b):
    length = z.shape[0]
    tl = min(256, length)
    nt = length // tl
    tc = _pick(D_FF, 1536)
    nb = D_FF // tc

    def body(zg_ref, zv_ref, wg_ref, wv_ref, bg_ref, bv_ref, o_ref, cg, cv):
        @pl.when(pl.program_id(1) == 0)
        def _():
            cg[...] = jnp.zeros_like(cg)
            cv[...] = jnp.zeros_like(cv)

        zg, zv = zg_ref[...], zv_ref[...]
        gate, _, _ = _conv(zg, cg[...], wg_ref[...], bg_ref[...])
        val, _, _ = _conv(zv, cv[...], wv_ref[...], bv_ref[...])
        cg[...] = zg[tl - SUBLANES:, :]
        cv[...] = zv[tl - SUBLANES:, :]
        o_ref[...] = (_gelu(gate) * val).astype(o_ref.dtype)

    zspec = lambda off: pl.BlockSpec((tl, tc), lambda j, i: (i, j + off))
    wspec = lambda off, r: pl.BlockSpec((r, tc), lambda j, i: (0, j + off))
    return pl.pallas_call(
        body, grid=(nb, nt),
        in_specs=[zspec(0), zspec(nb), wspec(0, 3), wspec(nb, 3), wspec(0, 1), wspec(nb, 1)],
        out_specs=pl.BlockSpec((tl, tc), lambda j, i: (i, j)), out_shape=SDS((length, D_FF), BF16),
        scratch_shapes=[pltpu.VMEM((SUBLANES, tc), F32)] * 2,
        compiler_params=_params(("arbitrary", "arbitrary")), name="conv_fwd")(z, z, conv_w, conv_w, conv_b, conv_b)


def _conv_bwd(z, dact, conv_w, conv_b):
    length = z.shape[0]
    tl = min(256, length)
    nt = length // tl
    tc = _pick(D_FF, 1536)
    nb = D_FF // tc
    rows_per = tl // SUBLANES

    def half_bwd(dzc, z, z1, z2, w, dcar):
        n = tl
        row = lax.broadcasted_iota(jnp.int32, dzc.shape, 0)
        u1 = jnp.where(row == n - 1, jnp.broadcast_to(dcar[0:1, :], dzc.shape), pltpu.roll(dzc, n - 1, 0))
        u2 = pltpu.roll(dzc, n - 2, 0)
        u2 = jnp.where(row == n - 2, jnp.broadcast_to(dcar[0:1, :], dzc.shape), u2)
        u2 = jnp.where(row == n - 1, jnp.broadcast_to(dcar[1:2, :], dzc.shape), u2)
        dz = w[2:3, :] * dzc + w[1:2, :] * u1 + w[0:1, :] * u2
        dw = jnp.concatenate([jnp.sum(dzc * z2, axis=0, keepdims=True), jnp.sum(dzc * z1, axis=0, keepdims=True),
                              jnp.sum(dzc * z, axis=0, keepdims=True)], axis=0)
        return dz, dw, jnp.sum(dzc, axis=0, keepdims=True)

    def body(zg_ref, zv_ref, pg_ref, pv_ref, da_ref, wg_ref, wv_ref, bg_ref, bv_ref,
             dzg_ref, dzv_ref, dwg_ref, dwv_ref, dbg_ref, dbv_ref, cg, cv):
        step = pl.program_id(1)

        @pl.when(step == 0)
        def _():
            cg[...] = jnp.zeros_like(cg)
            cv[...] = jnp.zeros_like(cv)
            for ref in (dwg_ref, dwv_ref, dbg_ref, dbv_ref):
                ref[...] = jnp.zeros_like(ref)

        is_first_tile = step == nt - 1
        zg, zv = zg_ref[...], zv_ref[...]
        pg = jnp.where(is_first_tile, 0.0, pg_ref[...])
        pv = jnp.where(is_first_tile, 0.0, pv_ref[...])
        wg, wv = wg_ref[...], wv_ref[...]
        gate, zg1, zg2 = _conv(zg, pg, wg, bg_ref[...])
        val, zv1, zv2 = _conv(zv, pv, wv, bv_ref[...])
        act_g, vjp = jax.vjp(_gelu, gate)
        da = da_ref[...]
        dgate = vjp(da * val)[0]
        dval = da * act_g
        dzg, dwg, dbg = half_bwd(dgate, zg, zg1, zg2, wg, cg[...])
        dzv, dwv, dbv = half_bwd(dval, zv, zv1, zv2, wv, cv[...])
        cg[...] = dgate[:SUBLANES, :]
        cv[...] = dval[:SUBLANES, :]
        dzg_ref[...] = dzg
        dzv_ref[...] = dzv
        dwg_ref[...] += dwg
        dwv_ref[...] += dwv
        dbg_ref[...] += dbg
        dbv_ref[...] += dbv

    rmap = lambda i: nt - 1 - i
    zspec = lambda off: pl.BlockSpec((tl, tc), lambda j, i: (rmap(i), j + off))
    pspec = lambda off: pl.BlockSpec((SUBLANES, tc), lambda j, i: (jnp.maximum(rmap(i) * rows_per - 1, 0), j + off))
    wspec = lambda off, r: pl.BlockSpec((r, tc), lambda j, i: (0, j + off))
    out_w = lambda r: pl.BlockSpec((r, tc), lambda j, i: (0, j))
    dzg, dzv, dwg, dwv, dbg, dbv = pl.pallas_call(
        body, grid=(nb, nt),
        in_specs=[zspec(0), zspec(nb), pspec(0), pspec(nb), pl.BlockSpec((tl, tc), lambda j, i: (rmap(i), j)),
                  wspec(0, 3), wspec(nb, 3), wspec(0, 1), wspec(nb, 1)],
        out_specs=[pl.BlockSpec((tl, tc), lambda j, i: (rmap(i), j))] * 2 + [out_w(3), out_w(3), out_w(1), out_w(1)],
        out_shape=[SDS((length, D_FF), F32)] * 2 + [SDS((3, D_FF), F32)] * 2 + [SDS((1, D_FF), F32)] * 2,
        scratch_shapes=[pltpu.VMEM((SUBLANES, tc), F32)] * 2,
        compiler_params=_params(("arbitrary", "arbitrary")), name="conv_bwd")(z, z, z, z, dact, conv_w, conv_w, conv_b, conv_b)
    return dzg, dzv, jnp.concatenate([dwg, dwv], axis=1), jnp.concatenate([dbg, dbv], axis=1)


def _s5_prep_math(a_re, a_im, ls, b_re, b_im):
    dt = jnp.exp(ls)
    er = jnp.exp(a_re * dt)
    ph = a_im * dt
    abr, abi = er * jnp.cos(ph), er * jnp.sin(ph)
    den = a_re * a_re + a_im * a_im
    nr = abr - 1.0
    cr = (nr * a_re + abi * a_im) / den
    ci = (abi * a_re - nr * a_im) / den
    return abr, abi, cr * b_re - ci * b_im, cr * b_im + ci * b_re


def _s5_prep_fwd(a_re, a_im, ls, b_re, b_im):
    def body(ar, ai, l, br, bi, o1, o2, o3, o4):
        for ref, val in zip((o1, o2, o3, o4), _s5_prep_math(ar[...], ai[...], l[...], br[...], bi[...]), strict=True):
            ref[...] = val
    return pl.pallas_call(body, out_shape=[SDS((1, S5N), F32)] * 2 + [SDS((S5C, S5N), F32)] * 2,
                          name="s5_prep_fwd")(a_re, a_im, ls, b_re, b_im)


def _s5_prep_bwd(a_re, a_im, ls, b_re, b_im, cots, gsel):
    def body(ar, ai, l, br, bi, c1, c2, c3, c4, g_ref, o1, o2, o3, o4, o5):
        _, vjp = jax.vjp(_s5_prep_math, ar[...], ai[...], l[...], br[...], bi[...])
        d_ar, d_ai, d_ls, d_br, d_bi = vjp((c1[...], c2[...], c3[...], c4[...]))
        o1[...] = d_ar
        o2[...] = d_ai
        o3[...] = _hdot(jnp.broadcast_to(d_ls, (SUBLANES, S5N)), g_ref[...])
        o4[...] = d_br
        o5[...] = d_bi
    return pl.pallas_call(body, out_shape=[SDS((1, S5N), F32)] * 2 + [SDS((SUBLANES, LANES), F32)] + [SDS((S5C, S5N), F32)] * 2,
                          name="s5_prep_bwd")(a_re, a_im, ls, b_re, b_im, *cots, gsel)


def _cmul(ar, ai, br, bi):
    return ar * br - ai * bi, ar * bi + ai * br


def _s5_powers(abr, abi):
    shp = (SUBLANES, S5N)
    a1 = (jnp.broadcast_to(abr, shp), jnp.broadcast_to(abi, shp))
    a2 = _cmul(*a1, *a1)
    a4 = _cmul(*a2, *a2)
    row = lax.broadcasted_iota(jnp.int32, shp, 0)
    pr, pi = a1
    cur = a1
    for i in range(1, SUBLANES):
        cur = _cmul(*cur, *a1)
        pr = jnp.where(row == i, cur[0], pr)
        pi = jnp.where(row == i, cur[1], pi)
    return a1, a2, a4, (pr, pi)


def _s5_scan(sre_ref, sim_ref, car_re, car_im, abr, abi, n_rows, reverse):
    a1, a2, a4, (pr, pi) = _s5_powers(abr, abi)
    sgn = -1.0 if reverse else 1.0
    row = lax.broadcasted_iota(jnp.int32, (SUBLANES, S5N), 0)
    if reverse:
        qr, qi = pr, pi
        for i in range(SUBLANES):
            src = SUBLANES - 1 - i
            qr = jnp.where(row == i, jnp.broadcast_to(pr[src:src + 1, :], pr.shape), qr)
            qi = jnp.where(row == i, jnp.broadcast_to(pi[src:src + 1, :], pi.shape), qi)
        pr, pi = qr, qi
    nblk = n_rows // SUBLANES

    def blk(i, carry):
        cr, ci = carry
        b = (nblk - 1 - i) if reverse else i
        sl = pl.ds(pl.multiple_of(b * SUBLANES, SUBLANES), SUBLANES)
        xr, xi = sre_ref[sl, :], sim_ref[sl, :]
        for kk, (er, ei) in ((1, a1), (2, a2), (4, a4)):
            if reverse:
                sr = jnp.where(row < SUBLANES - kk, pltpu.roll(xr, SUBLANES - kk, 0), 0.0)
                si = jnp.where(row < SUBLANES - kk, pltpu.roll(xi, SUBLANES - kk, 0), 0.0)
            else:
                sr = jnp.where(row >= kk, pltpu.roll(xr, kk, 0), 0.0)
                si = jnp.where(row >= kk, pltpu.roll(xi, kk, 0), 0.0)
            dr, di = _cmul(er, sgn * ei, sr, si)
            xr, xi = xr + dr, xi + di
        dr, di = _cmul(pr, sgn * pi, cr, ci)
        xr, xi = xr + dr, xi + di
        sre_ref[sl, :] = xr
        sim_ref[sl, :] = xi
        edge = 0 if reverse else SUBLANES - 1
        return (jnp.broadcast_to(xr[edge:edge + 1, :], xr.shape), jnp.broadcast_to(xi[edge:edge + 1, :], xi.shape))

    cr, ci = lax.fori_loop(0, nblk, blk, (car_re[...], car_im[...]))
    car_re[...] = cr
    car_im[...] = ci


S5_BLK = 4
S5_BN = S5N // S5_BLK


def _s5_project_in(u, bblk_ref, sre_ref, sim_ref):
    for j in range(S5_BLK):
        bu = _hdot(u[:, j * LANES:(j + 1) * LANES], bblk_ref[j])
        sre_ref[:, j * S5_BN:(j + 1) * S5_BN] = bu[:, :S5_BN]
        sim_ref[:, j * S5_BN:(j + 1) * S5_BN] = bu[:, S5_BN:]


def _s5_project_out(u, d, cblk_ref, sre_ref, sim_ref):
    ys = []
    for j in range(S5_BLK):
        sl = slice(j * S5_BN, (j + 1) * S5_BN)
        ys.append(_hdot(sre_ref[:, sl], cblk_ref[j, :S5_BN, :]) + _hdot(sim_ref[:, sl], cblk_ref[j, S5_BN:, :]))
    return jnp.concatenate(ys, axis=1) + d * u


def _s5_fwd(u, bblk, cblk, abr, abi, d):
    length = u.shape[0]
    tl = min(256, length)
    nt = length // tl

    def body(u_ref, b_ref, c_ref, ar_ref, ai_ref, d_ref, ys_ref, kr_ref, ki_ref, sre, sim, car_re, car_im):
        @pl.when(pl.program_id(0) == 0)
        def _():
            car_re[...] = jnp.zeros_like(car_re)
            car_im[...] = jnp.zeros_like(car_im)

        kr_ref[0] = car_re[...]
        ki_ref[0] = car_im[...]
        u_ = u_ref[...]
        _s5_project_in(u_, b_ref, sre, sim)
        _s5_scan(sre, sim, car_re, car_im, ar_ref[...], ai_ref[...], tl, False)
        ys_ref[...] = _gelu(_s5_project_out(u_, d_ref[...], c_ref, sre, sim))

    full = lambda a: pl.BlockSpec(a.shape, lambda i, nd=a.ndim: (0,) * nd)
    chk = pl.BlockSpec((1, SUBLANES, S5N), lambda i: (i, 0, 0))
    return pl.pallas_call(
        body, grid=(nt,), in_specs=[pl.BlockSpec((tl, S5W), lambda i: (i, 0)), full(bblk), full(cblk), full(abr), full(abi), full(d)],
        out_specs=[pl.BlockSpec((tl, S5W), lambda i: (i, 0)), chk, chk],
        out_shape=[SDS((length, S5W), F32), SDS((nt, SUBLANES, S5N), F32), SDS((nt, SUBLANES, S5N), F32)],
        scratch_shapes=[pltpu.VMEM((tl, S5N), F32)] * 2 + [pltpu.VMEM((SUBLANES, S5N), F32)] * 2,
        compiler_params=_params(), name="s5_fwd")(u, bblk, cblk, abr, abi, d)


def _s5_bwd(u, dys_a, dys_b, kre, kim, bblk, cblk, abr, abi, d):
    length = u.shape[0]
    tl = min(256, length)
    nt = length // tl
    tn_dims = (((0,), (0,)), ((), ()))
    nt_dims = (((1,), (1,)), ((), ()))

    def body(u_ref, da_ref, db_ref, kr_ref, ki_ref, b_ref, c_ref, ar_ref, ai_ref, d_ref,
             du_ref, dB_ref, dC_ref, dar_ref, dai_ref, dd_ref, sre, sim, gre, gim, car_re, car_im, dcar_re, dcar_im):
        @pl.when(pl.program_id(0) == 0)
        def _():
            dcar_re[...] = jnp.zeros_like(dcar_re)
            dcar_im[...] = jnp.zeros_like(dcar_im)
            for ref in (dB_ref, dC_ref, dar_ref, dai_ref, dd_ref):
                ref[...] = jnp.zeros_like(ref)

        u_ = u_ref[...]
        abr_, abi_, d_ = ar_ref[...], ai_ref[...], d_ref[...]
        car_re[...] = kr_ref[0]
        car_im[...] = ki_ref[0]
        _s5_project_in(u_, b_ref, sre, sim)
        _s5_scan(sre, sim, car_re, car_im, abr_, abi_, tl, False)
        y = _s5_project_out(u_, d_, c_ref, sre, sim)
        _, vjp = jax.vjp(_gelu, y)
        dy = vjp(da_ref[...] + db_ref[...])[0]
        dd_ref[...] += jnp.sum(dy * u_, axis=0, keepdims=True)
        for j in range(S5_BLK):
            sl = slice(j * S5_BN, (j + 1) * S5_BN)
            dyj = dy[:, j * LANES:(j + 1) * LANES]
            gre[:, sl] = lax.dot_general(dyj, c_ref[j, :S5_BN, :], nt_dims, precision=HI, preferred_element_type=F32)
            gim[:, sl] = lax.dot_general(dyj, c_ref[j, S5_BN:, :], nt_dims, precision=HI, preferred_element_type=F32)
            dC_ref[j, :S5_BN, :] += lax.dot_general(sre[:, sl], dyj, tn_dims, precision=HI, preferred_element_type=F32)
            dC_ref[j, S5_BN:, :] += lax.dot_general(sim[:, sl], dyj, tn_dims, precision=HI, preferred_element_type=F32)
        _s5_scan(gre, gim, dcar_re, dcar_im, abr_, abi_, tl, True)
        gr, gi = gre[...], gim[...]
        pr = _shift_down(sre[...], kr_ref[0, 0:1, :])
        pi = _shift_down(sim[...], ki_ref[0, 0:1, :])
        dar_ref[...] += jnp.sum(gr * pr + gi * pi, axis=0, keepdims=True)
        dai_ref[...] += jnp.sum(gi * pr - gr * pi, axis=0, keepdims=True)
        dus = []
        for j in range(S5_BLK):
            sl = slice(j * S5_BN, (j + 1) * S5_BN)
            uj = u_[:, j * LANES:(j + 1) * LANES]
            dus.append(lax.dot_general(gre[:, sl], b_ref[j, :, :S5_BN], nt_dims, precision=HI, preferred_element_type=F32)
                       + lax.dot_general(gim[:, sl], b_ref[j, :, S5_BN:], nt_dims, precision=HI, preferred_element_type=F32))
            dB_ref[j, :, :S5_BN] += lax.dot_general(uj, gre[:, sl], tn_dims, precision=HI, preferred_element_type=F32)
            dB_ref[j, :, S5_BN:] += lax.dot_general(uj, gim[:, sl], tn_dims, precision=HI, preferred_element_type=F32)
        du_ref[...] = jnp.concatenate(dus, axis=1) + d_ * dy

    rmap = lambda i: nt - 1 - i
    full = lambda a: pl.BlockSpec(a.shape, lambda i, nd=a.ndim: (0,) * nd)
    row = pl.BlockSpec((tl, S5W), lambda i: (rmap(i), 0))
    chk = pl.BlockSpec((1, SUBLANES, S5N), lambda i: (rmap(i), 0, 0))
    return pl.pallas_call(
        body, grid=(nt,), in_specs=[row, row, row, chk, chk, full(bblk), full(cblk), full(abr), full(abi), full(d)],
        out_specs=[row, full(bblk), full(cblk), full(abr), full(abi), full(d)],
        out_shape=[SDS((length, S5W), F32), SDS(bblk.shape, F32), SDS(cblk.shape, F32), SDS(abr.shape, F32),
                   SDS(abi.shape, F32), SDS(d.shape, F32)],
        scratch_shapes=[pltpu.VMEM((tl, S5N), F32)] * 4 + [pltpu.VMEM((SUBLANES, S5N), F32)] * 4,
        compiler_params=_params(), name="s5_bwd")(u, dys_a, dys_b, kre, kim, bblk, cblk, abr, abi, d)


def _s5_blockdiag_in(bb):
    t = bb.reshape(S5C, S5_BLK, S5_BN).transpose(1, 0, 2)
    t = jnp.tile(t, (1, SUBLANES, 1))
    mask = (lax.broadcasted_iota(jnp.int32, (LANES, S5_BN), 0) // S5C) == (lax.broadcasted_iota(jnp.int32, (LANES, S5_BN), 1) // S5P)
    return jnp.where(mask[None], t, 0.0)


def _s5_blockdiag_in_t(dblk):
    t = dblk.reshape(S5_BLK, SUBLANES, S5C, SUBLANES, S5P)
    t = jnp.diagonal(t, axis1=1, axis2=3)
    return t.transpose(1, 0, 3, 2).reshape(S5C, S5N)


def _s5_blockdiag_out(c):
    t = c.reshape(S5_BLK, SUBLANES, S5C, S5P).transpose(0, 3, 1, 2).reshape(S5_BLK, S5P, LANES)
    t = jnp.tile(t, (1, SUBLANES, 1))
    mask = (lax.broadcasted_iota(jnp.int32, (S5_BN, LANES), 0) // S5P) == (lax.broadcasted_iota(jnp.int32, (S5_BN, LANES), 1) // S5C)
    return jnp.where(mask[None], t, 0.0)


def _s5_blockdiag_out_t(dblk):
    t = dblk.reshape(S5_BLK, SUBLANES, S5P, SUBLANES, S5C)
    t = jnp.diagonal(t, axis1=1, axis2=3)
    return t.transpose(0, 3, 2, 1).reshape(S5G, S5C, S5P)


def _wkv_consts():
    lane = lax.broadcasted_iota(jnp.int32, (4 * HN, LANES), 1)
    row = lax.broadcasted_iota(jnp.int32, (4 * HN, LANES), 0)
    diag = ((lane % HN) == (row % HN)).astype(F32)
    ones = _block_ones(LANES, HN).astype(BF16)
    return lane % HN, diag, ones


def _wkv_rowseg(p, ones):
    hi = p.astype(BF16)
    lo = (p - hi.astype(F32)).astype(BF16)
    return jnp.dot(hi, ones, preferred_element_type=F32) + jnp.dot(lo, ones, preferred_element_type=F32)


def _wkv_full(qb_ref, t):
    q = qb_ref[t]
    return jnp.concatenate([jnp.tile(q[:, hp * LANES:(hp + 1) * LANES], (SUBLANES, 1)) for hp in range(4)], axis=0)


def _wkv_colsum8(x):
    return jnp.concatenate([x[hp * HN:(hp + 1) * HN].reshape(SUBLANES, SUBLANES, LANES).sum(axis=0) for hp in range(4)], axis=1)


def _wkv_prebroadcast(src_refs, dst_refs, n):
    for s, dref in zip(src_refs, dst_refs, strict=True):
        for tt in range(n):
            dref[tt] = jnp.broadcast_to(s[tt:tt + 1, :], (SUBLANES, RW))


def _wkv_transpose_out(acc, c, diag):
    lane = lax.broadcasted_iota(jnp.int32, (c, LANES), 1)
    outs = []
    for hp in range(4):
        zp = lax.dot_general(acc[hp * HN:(hp + 1) * HN], diag[:HN], (((0,), (0,)), ((), ())), precision=HI,
                             preferred_element_type=F32)
        outs.append(jnp.where(lane < HN, zp[:c], zp[HN:HN + c]))
    return jnp.concatenate(outs, axis=1)


def _wkv_fwd(r, w, k, v, a, b):
    length = r.shape[0]
    c = min(WKV_CHUNK, length)
    nc = length // c
    ns = 4 * HN

    def body(r_ref, w_ref, k_ref, v_ref, a_ref, b_ref, y_ref, sst_ref, sast_ref, fin_ref,
             s_ref, sa_ref, acc_ref, vst, rb, wb, kb, vb, ab, bb):
        @pl.when(pl.program_id(0) == 0)
        def _():
            s_ref[...] = jnp.zeros_like(s_ref)

        _wkv_prebroadcast((r_ref, w_ref, k_ref, v_ref, a_ref, b_ref), (rb, wb, kb, vb, ab, bb), c)
        lane_t, diag, ones = _wkv_consts()

        def spread_v(t, carry):
            vst[t] = _wkv_rowseg(_wkv_full(vb, t) * diag, ones)
            return carry

        lax.fori_loop(0, c, spread_v, 0)
        acc_ref[...] = jnp.zeros_like(acc_ref)
        sa_ref[...] = _wkv_rowseg(s_ref[...] * _wkv_full(ab, 0), ones)

        def step(t, carry):
            s, sa = s_ref[...], sa_ref[...]
            sst_ref[t] = s
            sast_ref[t] = sa
            s = s * _wkv_full(wb, t) + sa * _wkv_full(bb, t) + vst[t] * _wkv_full(kb, t)
            s_ref[...] = s
            nxt = jnp.minimum(t + 1, c - 1)
            both = _wkv_rowseg(jnp.concatenate([s * _wkv_full(ab, nxt), s * _wkv_full(rb, t)], axis=0), ones)
            sa_ref[...] = both[:ns]
            acc_ref[...] = jnp.where(lane_t == t, both[ns:], acc_ref[...])
            return carry

        lax.fori_loop(0, c, step, 0)
        y_ref[...] = _wkv_transpose_out(acc_ref[...], c, diag)
        fin_ref[...] = s_ref[...]

    row = pl.BlockSpec((c, RW), lambda i: (i, 0))
    st = pl.BlockSpec((c, ns, LANES), lambda i: (i, 0, 0))
    state = pltpu.VMEM((ns, LANES), F32)
    return pl.pallas_call(
        body, grid=(nc,), in_specs=[row] * 6,
        out_specs=[row, st, st, pl.BlockSpec((ns, LANES), lambda i: (0, 0))],
        out_shape=[SDS((length, RW), F32), SDS((length, ns, LANES), F32), SDS((length, ns, LANES), F32), SDS((ns, LANES), F32)],
        scratch_shapes=[state] * 3 + [pltpu.VMEM((c, ns, LANES), F32)] + [pltpu.VMEM((c, SUBLANES, RW), F32)] * 6,
        compiler_params=_params(), name="wkv_fwd")(r, w, k, v, a, b)


def _wkv_bwd(r, w, k, v, a, b, dy, sst, sast, fin):
    length = r.shape[0]
    c = min(WKV_CHUNK, length)
    nc = length // c
    ns = 4 * HN

    def body(r_ref, w_ref, k_ref, v_ref, a_ref, b_ref, dy_ref, sst_ref, sast_ref, snext_ref, fin_ref,
             dr_ref, dw_ref, dk_ref, da_ref, db_ref, dv_ref,
             ds_ref, cur_ref, acc_ref, vst, dyst, rb, wb, kb, vb, ab, bb, dyb, p_r, p_w, p_k, p_a, p_b):
        @pl.when(pl.program_id(0) == 0)
        def _():
            ds_ref[...] = jnp.zeros_like(ds_ref)

        _wkv_prebroadcast((r_ref, w_ref, k_ref, v_ref, a_ref, b_ref, dy_ref), (rb, wb, kb, vb, ab, bb, dyb), c)
        lane_t, diag, ones = _wkv_consts()

        def spread(t, carry):
            both = _wkv_rowseg(jnp.concatenate([_wkv_full(vb, t) * diag, _wkv_full(dyb, t) * diag], axis=0), ones)
            vst[t] = both[:ns]
            dyst[t] = both[ns:]
            return carry

        lax.fori_loop(0, c, spread, 0)
        acc_ref[...] = jnp.zeros_like(acc_ref)
        cur_ref[...] = jnp.where(pl.program_id(0) == 0, fin_ref[...], snext_ref[0])

        def bwd(i, carry):
            t = c - 1 - i
            s_t, s_prev, sa, vv, dyy = cur_ref[...], sst_ref[t], sast_ref[t], vst[t], dyst[t]
            ds = ds_ref[...] + dyy * _wkv_full(rb, t)
            p_r[t] = _wkv_colsum8(s_t * dyy)
            p_k[t] = _wkv_colsum8(ds * vv)
            p_b[t] = _wkv_colsum8(ds * sa)
            p_w[t] = _wkv_colsum8(ds * s_prev)
            both = _wkv_rowseg(jnp.concatenate([ds * _wkv_full(bb, t), ds * _wkv_full(kb, t)], axis=0), ones)
            dsa = both[:ns]
            acc_ref[...] = jnp.where(lane_t == t, both[ns:], acc_ref[...])
            p_a[t] = _wkv_colsum8(s_prev * dsa)
            ds_ref[...] = ds * _wkv_full(wb, t) + dsa * _wkv_full(ab, t)
            cur_ref[...] = s_prev
            return carry

        lax.fori_loop(0, c, bwd, 0)
        sel = (lax.broadcasted_iota(jnp.int32, (c, c * SUBLANES), 1) // SUBLANES
               == lax.broadcasted_iota(jnp.int32, (c, c * SUBLANES), 0)).astype(F32)
        for out_ref, part in ((dr_ref, p_r), (dw_ref, p_w), (dk_ref, p_k), (da_ref, p_a), (db_ref, p_b)):
            out_ref[...] = _hdot(sel, part[...].reshape(c * SUBLANES, RW))
        dv_ref[...] = _wkv_transpose_out(acc_ref[...], c, diag)

    rmap = lambda i: nc - 1 - i
    row = pl.BlockSpec((c, RW), lambda i: (rmap(i), 0))
    st = pl.BlockSpec((c, ns, LANES), lambda i: (rmap(i), 0, 0))
    nxt = pl.BlockSpec((1, ns, LANES), lambda i: (jnp.minimum((rmap(i) + 1) * c, length - 1), 0, 0))
    state = pltpu.VMEM((ns, LANES), F32)
    return pl.pallas_call(
        body, grid=(nc,), in_specs=[row] * 7 + [st, st, nxt, pl.BlockSpec((ns, LANES), lambda i: (0, 0))],
        out_specs=[row] * 6, out_shape=[SDS((length, RW), F32)] * 6,
        scratch_shapes=[state] * 3 + [pltpu.VMEM((c, ns, LANES), F32)] * 2 + [pltpu.VMEM((c, SUBLANES, RW), F32)] * 12,
        compiler_params=_params(), name="wkv_bwd")(r, w, k, v, a, b, dy, sst, sast, sst, fin)


def _my_place():
    return lax.axis_index("x"), lax.axis_index("y"), lax.axis_index("c")


def _peer(x, y, c, k):
    return (x ^ ((k >> 2) & 1), y ^ ((k >> 1) & 1), c ^ (k & 1))


def _all_gather(shard, name):
    rows = shard.shape[0]

    def body(in_ref, out_ref, send_sems, recv_sems, local_sem):
        x, y, c = _my_place()
        me = 4 * x + 2 * y + c
        mine = pltpu.make_async_copy(in_ref, out_ref.at[me], local_sem)
        mine.start()
        sends = []
        for k in range(1, N_DEV):
            cp = pltpu.make_async_remote_copy(src_ref=in_ref, dst_ref=out_ref.at[me], send_sem=send_sems.at[k - 1],
                                              recv_sem=recv_sems.at[k - 1], device_id=_peer(x, y, c, k), device_id_type=MESH)
            cp.start()
            sends.append(cp)
        for k in range(1, N_DEV):
            px, py, pc = _peer(x, y, c, k)
            pltpu.make_async_remote_copy(src_ref=in_ref, dst_ref=out_ref.at[4 * px + 2 * py + pc], send_sem=send_sems.at[k - 1],
                                         recv_sem=recv_sems.at[k - 1], device_id=(px, py, pc), device_id_type=MESH).wait_recv()
        for cp in sends:
            cp.wait_send()
        mine.wait()

    return pl.pallas_call(
        body, out_shape=SDS((N_DEV, rows, LANES), shard.dtype),
        in_specs=[pl.BlockSpec(memory_space=pl.ANY)], out_specs=pl.BlockSpec(memory_space=pl.ANY),
        scratch_shapes=[pltpu.SemaphoreType.DMA((N_DEV - 1,)), pltpu.SemaphoreType.DMA((N_DEV - 1,)), pltpu.SemaphoreType.DMA],
        name=name)(shard)


def _all_to_all(chunks):
    rows = chunks.shape[1]

    def body(in_ref, out_ref, send_sems, recv_sems, local_sem):
        x, y, c = _my_place()
        me = 4 * x + 2 * y + c
        mine = pltpu.make_async_copy(in_ref.at[me], out_ref.at[me], local_sem)
        mine.start()
        sends = []
        for k in range(1, N_DEV):
            px, py, pc = _peer(x, y, c, k)
            cp = pltpu.make_async_remote_copy(src_ref=in_ref.at[4 * px + 2 * py + pc], dst_ref=out_ref.at[me],
                                              send_sem=send_sems.at[k - 1], recv_sem=recv_sems.at[k - 1],
                                              device_id=(px, py, pc), device_id_type=MESH)
            cp.start()
            sends.append(cp)
        for k in range(1, N_DEV):
            px, py, pc = _peer(x, y, c, k)
            pltpu.make_async_remote_copy(src_ref=in_ref.at[me], dst_ref=out_ref.at[4 * px + 2 * py + pc], send_sem=send_sems.at[k - 1],
                                         recv_sem=recv_sems.at[k - 1], device_id=(px, py, pc), device_id_type=MESH).wait_recv()
        for cp in sends:
            cp.wait_send()
        mine.wait()

    return pl.pallas_call(
        body, out_shape=SDS((N_DEV, rows, LANES), chunks.dtype),
        in_specs=[pl.BlockSpec(memory_space=pl.ANY)], out_specs=pl.BlockSpec(memory_space=pl.ANY),
        scratch_shapes=[pltpu.SemaphoreType.DMA((N_DEV - 1,)), pltpu.SemaphoreType.DMA((N_DEV - 1,)), pltpu.SemaphoreType.DMA],
        name="exchange_grads")(chunks)


def _sum_parts(parts):
    rows = parts.shape[1]

    def body(p_ref, g_out):
        g = p_ref[0]
        for s in range(1, N_DEV):
            g = g + p_ref[s]
        g_out[...] = g

    return pl.pallas_call(
        body, grid=(rows // PACK_ROWS,), in_specs=[pl.BlockSpec((N_DEV, PACK_ROWS, LANES), lambda i: (0, i, 0))],
        out_specs=pl.BlockSpec((PACK_ROWS, LANES), lambda i: (i, 0)), out_shape=SDS((rows, LANES), F32),
        compiler_params=_params(("parallel",)), name="sum_grads")(parts)


def _adamw(g, wgt, m, v):
    rows = wgt.shape[0]

    def body(g_ref, w_ref, m_ref, v_ref, d_out, m_out, v_out):
        g_ = g_ref[...]
        m_new = ADAM_B1 * m_ref[...] + (1.0 - ADAM_B1) * g_
        v_new = ADAM_B2 * v_ref[...] + (1.0 - ADAM_B2) * (g_ * g_)
        m_hat = m_new / (1.0 - ADAM_B1 ** ADAM_STEP)
        v_hat = v_new / (1.0 - ADAM_B2 ** ADAM_STEP)
        d_out[...] = -ADAM_LR * (m_hat / (jnp.sqrt(v_hat) + ADAM_EPS) + ADAM_WD * w_ref[...])
        m_out[...] = m_new
        v_out[...] = v_new

    blk = pl.BlockSpec((PACK_ROWS, LANES), lambda i: (i, 0))
    return pl.pallas_call(
        body, grid=(rows // PACK_ROWS,), in_specs=[blk] * 4, out_specs=[blk] * 3, out_shape=[SDS((rows, LANES), F32)] * 3,
        compiler_params=_params(("parallel",)), name="adamw")(g, wgt, m, v)


PACK_ALIGN = 2 * SUBLANES * LANES
PACK_ROWS = 512

ROW_SHARDED = ("s5_w_glu", "w_out", "ffn_w_down")
TRANSPOSED = ("w_in", "ffn_w_up", "w_branch_rwkv", "w_branch_s5")
SMALL_SHARDED = ("rwkv_w2", "rwkv_a2", "rwkv_g2", "ffn_conv_w")
SHARDED = ROW_SHARDED + TRANSPOSED + SMALL_SHARDED
F32_GATHER = ("ffn_conv_w",)
REPLICATED =("norm_mix_pre", "norm_mix_post", "norm_ffn_pre", "norm_ffn_post", "b_gate", "rwkv_shift_mu", "rwkv_w0",
              "rwkv_a0", "rwkv_k_k", "rwkv_k_a", "rwkv_r_k", "rwkv_lnx_w", "rwkv_lnx_b", "s5_a_re", "s5_a_im", "s5_b_re",
              "s5_b_im", "s5_c_re", "s5_c_im", "s5_d", "s5_log_step", "s5_b_glu", "ffn_conv_b")
WEIGHTS = ("norm_mix_pre", "norm_mix_post", "norm_ffn_pre", "norm_ffn_post", "w_in", "b_gate", "rwkv_shift_mu", "rwkv_w0",
           "rwkv_w2", "rwkv_a0", "rwkv_a2", "rwkv_g2", "rwkv_k_k", "rwkv_k_a", "rwkv_r_k", "rwkv_lnx_w", "rwkv_lnx_b",
           "s5_a_re", "s5_a_im", "s5_b_re", "s5_b_im", "s5_c_re", "s5_c_im", "s5_d", "s5_log_step", "s5_w_glu", "s5_b_glu",
           "w_branch_rwkv", "w_branch_s5", "w_out", "ffn_w_up", "ffn_conv_w", "ffn_conv_b", "ffn_w_down")


def _pad_flat(a):
    flat = a.reshape(-1)
    pad = (-flat.shape[0]) % PACK_ALIGN
    return jnp.pad(flat, (0, pad)) if pad else flat


def _pad_cols(a, mult):
    pad = (-a.shape[1]) % mult
    return jnp.pad(a, ((0, 0), (0, pad))) if pad else a


def _pack(pieces):
    flat = jnp.concatenate([_pad_flat(p) for p in pieces])
    pad = (-flat.shape[0]) % (PACK_ROWS * LANES)
    return (jnp.pad(flat, (0, pad)) if pad else flat).reshape(-1, LANES)


def _unpack(buf, shapes):
    lead = buf.shape[:-2]
    flat = buf.reshape(*lead, -1)
    out, off = [], 0
    for s in shapes:
        n = math.prod(s)
        out.append(flat[..., off:off + n].reshape(*lead, *s))
        off += n + (-n) % PACK_ALIGN
    return out


def _rows_to_full(stack):
    return stack.reshape(-1, stack.shape[2])


def _cols_to_full(stack):
    return stack.transpose(1, 0, 2).reshape(stack.shape[1], -1)


def _local_step(x, tgt, wt):
    o512 = _block_ones(RW, HN)
    w_in = wt["w_in"]
    w_rw, w_u, w_g = w_in[:N_RWKV], w_in[N_RWKV:N_RWKV + S5W], w_in[N_RWKV + S5W:]
    b_gate = wt["b_gate"]
    b_r, b_s = b_gate[:, :D_MODEL], b_gate[:, D_MODEL:]
    zpad = lambda a, lo, n: jnp.pad(a.astype(F32), ((lo, n - lo - a.shape[0]), (0, 0)))
    w2p, a2p, g2p = zpad(wt["rwkv_w2"], 0, 256), zpad(wt["rwkv_a2"], 64, 256), zpad(wt["rwkv_g2"], 128, 256)
    pre_small = [wt["rwkv_w0"], wt["rwkv_a0"], wt["rwkv_k_k"], wt["rwkv_k_a"], w2p, a2p, g2p]
    post_prm = [wt["rwkv_lnx_w"], wt["rwkv_lnx_b"], wt["rwkv_r_k"].reshape(1, RW)]
    mu = wt["rwkv_shift_mu"]

    a_re, a_im = wt["s5_a_re"].reshape(1, S5N), wt["s5_a_im"].reshape(1, S5N)
    ls = jnp.repeat(wt["s5_log_step"].reshape(S5G), S5P).reshape(1, S5N)
    b_re_t = wt["s5_b_re"].reshape(S5N, S5C).T
    b_im_t = wt["s5_b_im"].reshape(S5N, S5C).T
    c_re, c_im = wt["s5_c_re"].reshape(S5G, S5C, S5P), wt["s5_c_im"].reshape(S5G, S5C, S5P)
    abr, abi, bbr, bbi = _s5_prep_fwd(a_re, a_im, ls, b_re_t, b_im_t)
    bblk = jnp.concatenate([_s5_blockdiag_in(bbr), _s5_blockdiag_in(bbi)], axis=2)
    cblk = jnp.concatenate([_s5_blockdiag_out(c_re), -_s5_blockdiag_out(c_im)], axis=1)
    s5_d = wt["s5_d"]

    h1 = _rms_fwd(x, wt["norm_mix_pre"], "rms1_fwd")
    p_rw = _mm(h1, w_rw, tb=True, name="mm_proj_rwkv")
    u = _mm(h1, w_u, tb=True, name="mm_proj_s5")
    gpre = _mm(h1, w_g, tb=True, name="mm_proj_gate")
    r, decay, k2, v, aa, bb, g = _rwkv_pre_fwd(p_rw, mu, pre_small, o512)
    y, sst, sast, s_fin = _wkv_fwd(r, decay, k2, v, aa, bb)
    y_r = _rwkv_post_fwd(y, r, k2, v, g, post_prm, o512)
    o_r = _mm(y_r, wt["w_branch_rwkv"], tb=True, name="mm_branch_rwkv")
    ys, kre, kim = _s5_fwd(u, bblk, cblk, abr, abi, s5_d)
    t_glu = _mm(ys, wt["s5_w_glu"], name="mm_glu")
    out5 = _glu_fwd(ys, t_glu, wt["s5_b_glu"])
    o_s = _mm(out5, wt["w_branch_s5"], tb=True, name="mm_branch_s5")
    mi = _merge_fwd(gpre, o_r, o_s, b_r, b_s)
    mixed = _mm(mi, wt["w_out"], name="mm_out")
    x2, h2 = _mid_fwd(x, mixed, wt["norm_mix_post"], wt["norm_ffn_pre"])
    z = _mm(h2, wt["ffn_w_up"], tb=True, name="mm_up")
    act = _conv_fwd(z, wt["ffn_conv_w"], wt["ffn_conv_b"])
    f = _mm(act, wt["ffn_w_down"], name="mm_down")
    dx3, df, d_norm_ffn_post, loss_part = _final(x2, f, tgt, wt["norm_ffn_post"])

    gr = {"norm_ffn_post": d_norm_ffn_post}
    dact = _mm(df, wt["ffn_w_down"], tb=True, name="mm_down_dx")
    gr["ffn_w_down"] = _mm(act, df, ta=True, name="mm_down_dw")
    dzg, dzv, gr["ffn_conv_w"], gr["ffn_conv_b"] = _conv_bwd(z, dact, wt["ffn_conv_w"], wt["ffn_conv_b"])
    dz = jnp.concatenate([dzg, dzv], axis=1)
    dh2 = _mm(dz, wt["ffn_w_up"], name="mm_up_dx")
    gr["ffn_w_up"] = _mm(dz, h2, ta=True, name="mm_up_dw")
    dx2, dmixed, gr["norm_mix_post"], gr["norm_ffn_pre"] = _mid_bwd(x2, mixed, dh2, dx3, wt["norm_mix_post"], wt["norm_ffn_pre"])
    dmi = _mm(dmixed, wt["w_out"], tb=True, name="mm_out_dx")
    gr["w_out"] = _mm(mi, dmixed, ta=True, name="mm_out_dw")
    dgp_r, dgp_s, do_r, do_s, db_r, db_s = _merge_bwd(gpre, o_r, o_s, dmi, b_r, b_s)
    gr["b_gate"] = jnp.concatenate([db_r, db_s], axis=1)
    dout5 = _mm(do_s, wt["w_branch_s5"], name="mm_branch_s5_dx")
    gr["w_branch_s5"] = _mm(do_s, out5, ta=True, name="mm_branch_s5_dw")
    dys_a, dt_glu, gr["s5_b_glu"] = _glu_bwd(ys, t_glu, dout5, wt["s5_b_glu"])
    dys_b = _mm(dt_glu, wt["s5_w_glu"], tb=True, name="mm_glu_dx")
    gr["s5_w_glu"] = _mm(ys, dt_glu, ta=True, name="mm_glu_dw")
    du, dbblk, dcblk, dabr, dabi, gr["s5_d"] = _s5_bwd(u, dys_a, dys_b, kre, kim, bblk, cblk, abr, abi, s5_d)
    gr["s5_c_re"] = _s5_blockdiag_out_t(dcblk[:, :S5_BN, :]).reshape(wt["s5_c_re"].shape)
    gr["s5_c_im"] = (-_s5_blockdiag_out_t(dcblk[:, S5_BN:, :])).reshape(wt["s5_c_im"].shape)
    dbbr, dbbi = _s5_blockdiag_in_t(dbblk[:, :, :S5_BN]), _s5_blockdiag_in_t(dbblk[:, :, S5_BN:])
    gsel = (lax.broadcasted_iota(jnp.int32, (S5N, LANES), 0) // S5P == lax.broadcasted_iota(jnp.int32, (S5N, LANES), 1)).astype(F32)
    d_are, d_aim, d_ls, d_bre_t, d_bim_t = _s5_prep_bwd(a_re, a_im, ls, b_re_t, b_im_t, (dabr, dabi, dbbr, dbbi), gsel)
    gr["s5_a_re"] = d_are.reshape(wt["s5_a_re"].shape)
    gr["s5_a_im"] = d_aim.reshape(wt["s5_a_im"].shape)
    gr["s5_log_step"] = d_ls[0:1, :S5G]
    gr["s5_b_re"] = d_bre_t.T.reshape(wt["s5_b_re"].shape)
    gr["s5_b_im"] = d_bim_t.T.reshape(wt["s5_b_im"].shape)
    dy_r = _mm(do_r, wt["w_branch_rwkv"], name="mm_branch_rwkv_dx")
    gr["w_branch_rwkv"] = _mm(do_r, y_r, ta=True, name="mm_branch_rwkv_dw")
    dy, dr1, dk1, dv1, dg, gr["rwkv_lnx_w"], gr["rwkv_lnx_b"], d_rk = _rwkv_post_bwd(y, r, k2, v, g, dy_r, post_prm, o512)
    gr["rwkv_r_k"] = d_rk.reshape(wt["rwkv_r_k"].shape)
    dr2, ddecay, dk2, daa, dbb, dv2 = _wkv_bwd(r, decay, k2, v, aa, bb, dy, sst, sast, s_fin)
    cots = [dr1 + dr2, ddecay, dk1 + dk2, dv1 + dv2, daa, dbb, dg]
    dp_rw, gr["rwkv_shift_mu"], gr["rwkv_w0"], gr["rwkv_a0"], gr["rwkv_k_k"], gr["rwkv_k_a"], dw2p, da2p, dg2p = \
        _rwkv_pre_bwd(p_rw, cots, mu, pre_small, o512)
    gr["rwkv_w2"], gr["rwkv_a2"], gr["rwkv_g2"] = dw2p[:64], da2p[64:128], dg2p[128:]
    dproj = jnp.concatenate([dp_rw, du, dgp_r, dgp_s], axis=1)
    dh1 = _mm(dproj, w_in, name="mm_proj_dx")
    gr["w_in"] = _mm(dproj, h1, ta=True, name="mm_proj_dw")
    dx, gr["norm_mix_pre"] = _rms_bwd(x, wt["norm_mix_pre"], dh1, dx2, "rms1_bwd")
    return loss_part[0, 0], dx, gr


def kernel(x, norm_mix_pre, norm_mix_post, norm_ffn_pre, norm_ffn_post, w_in, b_gate, rwkv_shift_mu, rwkv_w0, rwkv_w2, rwkv_a0, rwkv_a2, rwkv_g2, rwkv_k_k, rwkv_k_a, rwkv_r_k, rwkv_lnx_w, rwkv_lnx_b, s5_a_re, s5_a_im, s5_b_re, s5_b_im, s5_c_re, s5_c_im, s5_d, s5_log_step, s5_w_glu, s5_b_glu, w_branch_rwkv, w_branch_s5, w_out, ffn_w_up, ffn_conv_w, ffn_conv_b, ffn_w_down, loss_target, m_norm_mix_pre, m_norm_mix_post, m_norm_ffn_pre, m_norm_ffn_post, m_w_in, m_b_gate, m_rwkv_shift_mu, m_rwkv_w0, m_rwkv_w2, m_rwkv_a0, m_rwkv_a2, m_rwkv_g2, m_rwkv_k_k, m_rwkv_k_a, m_rwkv_r_k, m_rwkv_lnx_w, m_rwkv_lnx_b, m_s5_a_re, m_s5_a_im, m_s5_b_re, m_s5_b_im, m_s5_c_re, m_s5_c_im, m_s5_d, m_s5_log_step, m_s5_w_glu, m_s5_b_glu, m_w_branch_rwkv, m_w_branch_s5, m_w_out, m_ffn_w_up, m_ffn_conv_w, m_ffn_conv_b, m_ffn_w_down, v_norm_mix_pre, v_norm_mix_post, v_norm_ffn_pre, v_norm_ffn_post, v_w_in, v_b_gate, v_rwkv_shift_mu, v_rwkv_w0, v_rwkv_w2, v_rwkv_a0, v_rwkv_a2, v_rwkv_g2, v_rwkv_k_k, v_rwkv_k_a, v_rwkv_r_k, v_rwkv_lnx_w, v_rwkv_lnx_b, v_s5_a_re, v_s5_a_im, v_s5_b_re, v_s5_b_im, v_s5_c_re, v_s5_c_im, v_s5_d, v_s5_log_step, v_s5_w_glu, v_s5_b_glu, v_w_branch_rwkv, v_w_branch_s5, v_w_out, v_ffn_w_up, v_ffn_conv_w, v_ffn_conv_b, v_ffn_w_down):
    args = dict(locals())
    wgt = {n: args[n] for n in WEIGHTS}
    mom = {n: args["m_" + n] for n in WEIGHTS}
    var = {n: args["v_" + n] for n in WEIGHTS}
    me = 4 * lax.axis_index("x") + 2 * lax.axis_index("y") + lax.axis_index("c")
    shard_shapes = {n: wgt[n].shape[1:] for n in SHARDED}
    sent_shapes = {n: (shard_shapes[n][::-1] if n in TRANSPOSED else shard_shapes[n]) for n in SHARDED}
    sent = lambda n, a: a.T if n in TRANSPOSED else a

    bf_names = [n for n in SHARDED if n not in F32_GATHER]
    got_bf = _unpack(_all_gather(_pack([sent(n, wgt[n][0].astype(BF16)) for n in bf_names]), "all_gather_weights"),
                     [sent_shapes[n] for n in bf_names])
    got_f32 = _unpack(_all_gather(_pack([wgt[n][0] for n in F32_GATHER]), "all_gather_taps"),
                      [sent_shapes[n] for n in F32_GATHER])
    wt = {n: wgt[n] for n in REPLICATED}
    for n, blocks in zip(bf_names + list(F32_GATHER), got_bf + got_f32, strict=True):
        wt[n] = _cols_to_full(blocks) if n in SMALL_SHARDED else _rows_to_full(blocks)

    loss_part, dx, gr = _local_step(x[0], loss_target[0], wt)

    big = ROW_SHARDED + TRANSPOSED
    whole = SMALL_SHARDED + REPLICATED
    whole_flat = [gr[n].reshape(-1) for n in whole]
    pieces = [_pad_cols(gr[n].reshape(N_DEV, -1), PACK_ALIGN) for n in big]
    pieces += [jnp.broadcast_to(_pad_flat(g_)[None], (N_DEV, g_.shape[0] + (-g_.shape[0]) % PACK_ALIGN)) for g_ in whole_flat]
    chunks = _pad_cols(jnp.concatenate(pieces, axis=1), PACK_ROWS * LANES).reshape(N_DEV, -1, LANES)
    g_sum = _sum_parts(_all_to_all(chunks))
    got = dict(zip(big + whole, _unpack(g_sum, [sent_shapes[n] for n in big] + [gr[n].shape for n in whole]), strict=True))
    grads = {}
    for n in WEIGHTS:
        if n in TRANSPOSED:
            grads[n] = got[n].T
        elif n in SMALL_SHARDED:
            cols = shard_shapes[n][1]
            grads[n] = lax.dynamic_slice_in_dim(got[n], me * cols, cols, axis=1)
        else:
            grads[n] = got[n]
        grads[n] = grads[n].reshape(wgt[n].shape)

    pack_local = lambda src: _pack([src[n].reshape(-1) for n in WEIGHTS])
    outs = _adamw(pack_local(grads), pack_local(wgt), pack_local(mom), pack_local(var))
    shapes = [wgt[n].shape for n in WEIGHTS]
    loss = lax.psum(loss_part, AXES)
    return (loss, dx[None], *[grads[n] for n in WEIGHTS], *_unpack(outs[0], shapes), *_unpack(outs[1], shapes),
            *_unpack(outs[2], shapes))
```

```python
import functools
import math

import jax
import jax.numpy as jnp
from jax import lax
from jax.experimental import pallas as pl
from jax.experimental.pallas import tpu as pltpu

F32 = jnp.float32
BF16 = jnp.bfloat16
SDS = jax.ShapeDtypeStruct
HI = lax.Precision.HIGHEST
MESH = pl.DeviceIdType.MESH
AXES = ("x", "y", "c")
N_DEV = 8

D_MODEL = 1024
RW = 512
HN = 64
N_RWKV = 1792
S5W = 512
S5G = 32
S5C = 16
S5P = 64
S5N = S5G * S5P
D_FF = 2816
NORM_EPS = 1e-6
LNX_EPS = 64e-5

ADAM_LR = 0.001
ADAM_B1 = 0.9
ADAM_B2 = 0.999
ADAM_EPS = 1e-08
ADAM_WD = 0.01
ADAM_STEP = 10

LANES = 128
SUBLANES = 8
VMEM_LIMIT = 56 * 1024 * 1024
WKV_CHUNK = 32


def _params(sem=("arbitrary",)):
    return pltpu.CompilerParams(dimension_semantics=sem, vmem_limit_bytes=VMEM_LIMIT)


def _pick(n, cap):
    best = None
    for t in range(LANES, min(n, cap) + 1, LANES):
        if n % t == 0:
            best = t
    return best or n


def _mm(a, b, *, ta=False, tb=False, out_dtype=F32, name):
    m = a.shape[1] if ta else a.shape[0]
    k = a.shape[0] if ta else a.shape[1]
    n = b.shape[0] if tb else b.shape[1]
    assert (b.shape[1] if tb else b.shape[0]) == k
    tm, tn, tk = _pick(m, 512), _pick(n, 2304), _pick(k, 512)
    nk = k // tk
    dims = (((0 if ta else 1,), (1 if tb else 0,)), ((), ()))

    def body(a_ref, b_ref, o_ref, acc_ref):
        kk = pl.program_id(2)

        @pl.when(kk == 0)
        def _():
            acc_ref[...] = jnp.zeros_like(acc_ref)

        acc_ref[...] += lax.dot_general(a_ref[...].astype(BF16), b_ref[...].astype(BF16), dims,
                                        preferred_element_type=F32)

        @pl.when(kk == nk - 1)
        def _():
            o_ref[...] = acc_ref[...].astype(o_ref.dtype)

    a_spec = pl.BlockSpec((tk, tm), lambda i, j, kk: (kk, i)) if ta else pl.BlockSpec((tm, tk), lambda i, j, kk: (i, kk))
    b_spec = pl.BlockSpec((tn, tk), lambda i, j, kk: (j, kk)) if tb else pl.BlockSpec((tk, tn), lambda i, j, kk: (kk, j))
    return pl.pallas_call(
        body, grid=(m // tm, n // tn, nk), in_specs=[a_spec, b_spec],
        out_specs=pl.BlockSpec((tm, tn), lambda i, j, kk: (i, j)),
        out_shape=SDS((m, n), out_dtype), scratch_shapes=[pltpu.VMEM((tm, tn), F32)],
        compiler_params=_params(("parallel", "parallel", "arbitrary")), name=name)(a, b)


def _rows(fn, rows, params, out_rows, out_accs, *, name, tl, reverse=False, scratch=()):
    first = rows[0][0] if isinstance(rows[0], tuple) else rows[0]
    length = first.shape[0]
    tl = min(tl, length)
    nt = length // tl
    rmap = (lambda i: nt - 1 - i) if reverse else (lambda i: i)
    specs, arrs = [], []
    for r in rows:
        arr, wdt, cb = r if isinstance(r, tuple) else (r, r.shape[1], 0)
        specs.append(pl.BlockSpec((tl, wdt), lambda i, cb=cb: (rmap(i), cb)))
        arrs.append(arr)
    for p in params:
        specs.append(pl.BlockSpec(p.shape, lambda i, nd=p.ndim: (0,) * nd))
        arrs.append(p)
    out_shape = [SDS((length, c), dt) for c, dt in out_rows] + [SDS(s, F32) for s in out_accs]
    out_specs = [pl.BlockSpec((tl, c), lambda i: (rmap(i), 0)) for c, _ in out_rows]
    out_specs += [pl.BlockSpec(s, lambda i, nd=len(s): (0,) * nd) for s in out_accs]
    nr, npar, nor, noa = len(rows), len(params), len(out_rows), len(out_accs)

    def body(*refs):
        rin, pin = refs[:nr], refs[nr:nr + npar]
        rout = refs[nr + npar:nr + npar + nor]
        aout = refs[nr + npar + nor:nr + npar + nor + noa]
        scr = refs[nr + npar + nor + noa:]
        step = pl.program_id(0)
        outs_r, outs_a = fn(step, [r[...] for r in rin], [p[...] for p in pin], scr)
        for ref, val in zip(rout, outs_r, strict=True):
            ref[...] = val.astype(ref.dtype)

        @pl.when(step == 0)
        def _():
            for ref in aout:
                ref[...] = jnp.zeros_like(ref)

        for ref, val in zip(aout, outs_a, strict=True):
            ref[...] += val.astype(F32)

    res = pl.pallas_call(body, grid=(nt,), in_specs=specs, out_specs=out_specs, out_shape=out_shape,
                         scratch_shapes=list(scratch), compiler_params=_params(), name=name)(*arrs)
    return list(res)


def _rms(x, g):
    return x * lax.rsqrt(jnp.mean(x * x, axis=-1, keepdims=True) + NORM_EPS) * g


def _sig(x):
    return 0.5 * (jnp.tanh(0.5 * x) + 1.0)


def _softplus(x):
    return jnp.maximum(x, 0.0) + jnp.log(1.0 + jnp.exp(-jnp.abs(x)))


def _gelu(x):
    return x * (0.5 * (1.0 + jnp.tanh(math.sqrt(2.0 / math.pi) * (x + 0.044715 * (x * x * x)))))


def _bdot(a, b):
    return jnp.dot(a.astype(BF16), b.astype(BF16), preferred_element_type=F32)


def _hdot(a, b):
    return jnp.dot(a, b, precision=HI, preferred_element_type=F32)


def _block_ones(n, blk):
    i = lax.broadcasted_iota(jnp.int32, (n, n), 0) // blk
    j = lax.broadcasted_iota(jnp.int32, (n, n), 1) // blk
    return (i == j).astype(F32)


def _rms_fwd(x, g, name):
    return _rows(lambda s, r, p, _: ([_rms(r[0], p[0])], []), [x], [g], [(x.shape[1], BF16)], [], name=name, tl=512)[0]


def _rms_bwd(x, g, dh, dres, name):
    def fn(s, r, p, _):
        _, vjp = jax.vjp(_rms, r[0], p[0])
        dx, dg = vjp(r[1])
        return [dx + r[2]], [dg]
    return _rows(fn, [x, dh, dres], [g], [(x.shape[1], F32)], [g.shape], name=name, tl=256)


def _rwkv_pre_math(k_, lr, w0, a0, k_k, k_a, w2p, a2p, g2p, o512):
    pre_w = w0 + _bdot(jnp.tanh(lr), w2p)
    w = -_softplus(-pre_w) - 0.5
    decay = jnp.exp(-jnp.exp(w))
    a = _sig(a0 + _bdot(lr, a2p))
    g = _bdot(_sig(lr), g2p)
    kr = k_ * k_k
    kk = kr / jnp.maximum(jnp.sqrt(_hdot(kr * kr, o512)), 1e-12)
    k2 = k_ * (1.0 + (a - 1.0) * k_a)
    return decay, k2, -kk, kk * a, g


def _shift_down(p, prev_row):
    row = lax.broadcasted_iota(jnp.int32, p.shape, 0)
    return jnp.where(row == 0, jnp.broadcast_to(prev_row, p.shape), pltpu.roll(p, 1, 0))


def _shift_up(q, next_row):
    n = q.shape[0]
    row = lax.broadcasted_iota(jnp.int32, q.shape, 0)
    return jnp.where(row == n - 1, jnp.broadcast_to(next_row, q.shape), pltpu.roll(q, n - 1, 0))


def _rwkv_pre_fwd(p, mu, small, o512):
    def fn(step, r, prm, scr):
        car = scr[0]

        @pl.when(step == 0)
        def _():
            car[...] = jnp.zeros_like(car)

        x = r[0]
        prev = _shift_down(x, car[SUBLANES - 1:SUBLANES, :])
        car[...] = x[x.shape[0] - SUBLANES:, :]
        xs = x + (prev - x) * prm[0]
        decay, k2, aa, bb, g = _rwkv_pre_math(xs[:, RW:2 * RW], xs[:, 3 * RW:], *prm[1:])
        return [xs[:, :RW], decay, k2, xs[:, 2 * RW:3 * RW], aa, bb, g], []
    return _rows(fn, [p], [mu, *small, o512], [(RW, F32)] * 7, [], name="rwkv_pre_fwd", tl=256,
                 scratch=[pltpu.VMEM((SUBLANES, N_RWKV), F32)])


def _rwkv_pre_bwd(p, cots, mu, small, o512):
    length = p.shape[0]
    tl = min(256, length)
    nt = length // tl
    rows_per = tl // SUBLANES
    params = [mu, *small, o512]
    acc_shapes = [mu.shape] + [q.shape for q in small]
    nr, npar, nacc = 2 + len(cots), len(params), len(acc_shapes)

    def body(*refs):
        rin, pin = refs[:nr], refs[nr:nr + npar]
        dp_ref = refs[nr + npar]
        aout = refs[nr + npar + 1:nr + npar + 1 + nacc]
        car_q = refs[nr + npar + 1 + nacc]
        step = pl.program_id(0)

        @pl.when(step == 0)
        def _():
            car_q[...] = jnp.zeros_like(car_q)
            for ref in aout:
                ref[...] = jnp.zeros_like(ref)

        x = rin[0][...]
        prev_row = jnp.where(step == nt - 1, 0.0, rin[1][SUBLANES - 1:SUBLANES, :])
        dr, ddecay, dk2, dv, daa, dbb, dg = [r[...] for r in rin[2:]]
        prm = [q[...] for q in pin]
        mu_, o512_ = prm[0], prm[-1]
        prev = _shift_down(x, prev_row)
        xs = x + (prev - x) * mu_
        _, vjp = jax.vjp(lambda k_, lr, *w: _rwkv_pre_math(k_, lr, *w, o512_), xs[:, RW:2 * RW], xs[:, 3 * RW:], *prm[1:-1])
        dk_, dlr, *dsmall = vjp((ddecay, dk2, daa, dbb, dg))
        dxs = jnp.concatenate([dr, dk_, dv, dlr], axis=1)
        q = dxs * mu_
        dp_ref[...] = dxs - q + _shift_up(q, car_q[0:1, :])
        car_q[...] = q[:SUBLANES, :]
        aout[0][...] += jnp.sum((prev - x) * dxs, axis=0, keepdims=True)
        for ref, val in zip(aout[1:], dsmall, strict=True):
            ref[...] += val

    rmap = lambda i: nt - 1 - i
    specs = [pl.BlockSpec((tl, N_RWKV), lambda i: (rmap(i), 0)),
             pl.BlockSpec((SUBLANES, N_RWKV), lambda i: (jnp.maximum(rmap(i) * rows_per - 1, 0), 0))]
    specs += [pl.BlockSpec((tl, RW), lambda i: (rmap(i), 0)) for _ in cots]
    specs += [pl.BlockSpec(q.shape, lambda i, nd=q.ndim: (0,) * nd) for q in params]
    out_shape = [SDS((length, N_RWKV), F32)] + [SDS(sh, F32) for sh in acc_shapes]
    out_specs = [pl.BlockSpec((tl, N_RWKV), lambda i: (rmap(i), 0))]
    out_specs += [pl.BlockSpec(sh, lambda i, nd=len(sh): (0,) * nd) for sh in acc_shapes]
    res = pl.pallas_call(body, grid=(nt,), in_specs=specs, out_specs=out_specs, out_shape=out_shape,
                         scratch_shapes=[pltpu.VMEM((SUBLANES, N_RWKV), F32)],
                         compiler_params=_params(), name="rwkv_pre_bwd")(p, p, *cots, *params)
    return list(res)


def _rwkv_post_math(y, r, k2, v, g, lnx_w, lnx_b, r_k, o512):
    mean = _hdot(y, o512) * (1.0 / HN)
    yc = y - mean
    var = _hdot(yc * yc, o512) * (1.0 / HN)
    yn = yc * lax.rsqrt(var + LNX_EPS) * lnx_w + lnx_b
    bonus = _hdot(r * k2 * r_k, o512) * v
    return (yn + bonus) * g


def _rwkv_post_fwd(y, r, k2, v, g, prm, o512):
    return _rows(lambda s, rr, p, _: ([_rwkv_post_math(*rr, *p)], []), [y, r, k2, v, g], [*prm, o512],
                 [(RW, BF16)], [], name="rwkv_post_fwd", tl=256)[0]


def _rwkv_post_bwd(y, r, k2, v, g, dout, prm, o512):
    def fn(s, rr, p, _):
        o = p[-1]
        _, vjp = jax.vjp(lambda *a: _rwkv_post_math(*a, o), *rr[:5], *p[:-1])
        gr = vjp(rr[5])
        return list(gr[:5]), list(gr[5:])
    return _rows(fn, [y, r, k2, v, g, dout], [*prm, o512], [(RW, F32)] * 5, [q.shape for q in prm],
                 name="rwkv_post_bwd", tl=256)


def _glu_math(ys, t, b):
    return ys * _sig(t + b)


def _glu_fwd(ys, t, b):
    return _rows(lambda s, r, p, _: ([_glu_math(r[0], r[1], p[0])], []), [ys, t], [b], [(S5W, BF16)], [],
                 name="s5_glu_fwd", tl=512)[0]


def _glu_bwd(ys, t, dout, b):
    def fn(s, r, p, _):
        _, vjp = jax.vjp(_glu_math, r[0], r[1], p[0])
        dys, dt, db = vjp(r[2])
        return [dys, dt], [db]
    return _rows(fn, [ys, t, dout], [b], [(S5W, F32)] * 2, [b.shape], name="s5_glu_bwd", tl=512)


def _merge_math(gp_r, gp_s, o_r, o_s, b_r, b_s):
    return _sig(gp_r + b_r) * o_r + _sig(gp_s + b_s) * o_s


def _merge_fwd(gpre, o_r, o_s, b_r, b_s):
    return _rows(lambda s, r, p, _: ([_merge_math(*r, *p)], []),
                 [(gpre, D_MODEL, 0), (gpre, D_MODEL, 1), o_r, o_s], [b_r, b_s], [(D_MODEL, BF16)], [],
                 name="merge_fwd", tl=256)[0]


def _merge_bwd(gpre, o_r, o_s, dmi, b_r, b_s):
    def fn(s, r, p, _):
        _, vjp = jax.vjp(_merge_math, *r[:4], *p)
        dgr, dgs, dor, dos, dbr, dbs = vjp(r[4])
        return [dgr, dgs, dor, dos], [dbr, dbs]
    return _rows(fn, [(gpre, D_MODEL, 0), (gpre, D_MODEL, 1), o_r, o_s, dmi], [b_r, b_s], [(D_MODEL, F32)] * 4,
                 [b_r.shape, b_s.shape], name="merge_bwd", tl=256)


def _mid_fwd(x, mixed, g_post, g_pre):
    def fn(s, r, p, _):
        x2 = r[0] + _rms(r[1], p[0])
        return [x2, _rms(x2, p[1])], []
    return _rows(fn, [x, mixed], [g_post, g_pre], [(D_MODEL, F32), (D_MODEL, BF16)], [], name="mid_fwd", tl=256)


def _mid_bwd(x2, mixed, dh2, dx3, g_post, g_pre):
    def fn(s, r, p, _):
        _, vjp1 = jax.vjp(_rms, r[0], p[1])
        dx2, dg_pre = vjp1(r[2])
        dx2 = dx2 + r[3]
        _, vjp2 = jax.vjp(_rms, r[1], p[0])
        dmixed, dg_post = vjp2(dx2)
        return [dx2, dmixed], [dg_post, dg_pre]
    return _rows(fn, [x2, mixed, dh2, dx3], [g_post, g_pre], [(D_MODEL, F32)] * 2, [g_post.shape, g_pre.shape],
                 name="mid_bwd", tl=256)


def _final(x2, f, tgt, g_post):
    def fn(s, r, p, _):
        y, vjp = jax.vjp(_rms, r[1], p[0])
        diff = r[0] + y - r[2]
        dx3 = diff * (1.0 / D_MODEL)
        df, dg = vjp(dx3)
        part = 0.5 * jnp.sum(jnp.sum(diff * diff, axis=1, keepdims=True), axis=0, keepdims=True) * (1.0 / D_MODEL)
        return [dx3, df], [dg, jnp.broadcast_to(part, (1, LANES))]
    return _rows(fn, [x2, f, tgt], [g_post], [(D_MODEL, F32)] * 2, [g_post.shape, (1, LANES)], name="final", tl=256)


def _conv_taps(z, car):
    row = lax.broadcasted_iota(jnp.int32, z.shape, 0)
    z1 = jnp.where(row == 0, jnp.broadcast_to(car[7:8, :], z.shape), pltpu.roll(z, 1, 0))
    z2 = pltpu.roll(z, 2, 0)
    z2 = jnp.where(row == 0, jnp.broadcast_to(car[6:7, :], z.shape), z2)
    z2 = jnp.where(row == 1, jnp.broadcast_to(car[7:8, :], z.shape), z2)
    return z1, z2


def _conv(z, car, w, b):
    z1, z2 = _conv_taps(z, car)
    return b + w[0:1, :] * z2 + w[1:2, :] * z1 + w[2:3, :] * z, z1, z2


def _conv_fwd(z, conv_w, conv_b):
    length = z.shape[0]
    tl = min(256, length)
    nt = length // tl
    tc = _pick(D_FF, 1536)
    nb = D_FF // tc

    def body(zg_ref, zv_ref, wg_ref, wv_ref, bg_ref, bv_ref, o_ref, cg, cv):
        @pl.when(pl.program_id(1) == 0)
        def _():
            cg[...] = jnp.zeros_like(cg)
            cv[...] = jnp.zeros_like(cv)

        zg, zv = zg_ref[...], zv_ref[...]
        gate, _, _ = _conv(zg, cg[...], wg_ref[...], bg_ref[...])
        val, _, _ = _conv(zv, cv[...], wv_ref[...], bv_ref[...])
        cg[...] = zg[tl - SUBLANES:, :]
        cv[...] = zv[tl - SUBLANES:, :]
        o_ref[...] = (_gelu(gate) * val).astype(o_ref.dtype)

    zspec = lambda off: pl.BlockSpec((tl, tc), lambda j, i: (i, j + off))
    wspec = lambda off, r: pl.BlockSpec((r, tc), lambda j, i: (0, j + off))
    return pl.pallas_call(
        body, grid=(nb, nt),
        in_specs=[zspec(0), zspec(nb), wspec(0, 3), wspec(nb, 3), wspec(0, 1), wspec(nb, 1)],
        out_specs=pl.BlockSpec((tl, tc), lambda j, i: (i, j)), out_shape=SDS((length, D_FF), BF16),
        scratch_shapes=[pltpu.VMEM((SUBLANES, tc), F32)] * 2,
        compiler_params=_params(("arbitrary", "arbitrary")), name="conv_fwd")(z, z, conv_w, conv_w, conv_b, conv_b)


def _conv_bwd(z, dact, conv_w, conv_b):
    length = z.shape[0]
    tl = min(256, length)
    nt = length // tl
    tc = _pick(D_FF, 1536)
    nb = D_FF // tc
    rows_per = tl // SUBLANES

    def half_bwd(dzc, z, z1, z2, w, dcar):
        n = tl
        row = lax.broadcasted_iota(jnp.int32, dzc.shape, 0)
        u1 = jnp.where(row == n - 1, jnp.broadcast_to(dcar[0:1, :], dzc.shape), pltpu.roll(dzc, n - 1, 0))
        u2 = pltpu.roll(dzc, n - 2, 0)
        u2 = jnp.where(row == n - 2, jnp.broadcast_to(dcar[0:1, :], dzc.shape), u2)
        u2 = jnp.where(row == n - 1, jnp.broadcast_to(dcar[1:2, :], dzc.shape), u2)
        dz = w[2:3, :] * dzc + w[1:2, :] * u1 + w[0:1, :] * u2
        dw = jnp.concatenate([jnp.sum(dzc * z2, axis=0, keepdims=True), jnp.sum(dzc * z1, axis=0, keepdims=True),
                              jnp.sum(dzc * z, axis=0, keepdims=True)], axis=0)
        return dz, dw, jnp.sum(dzc, axis=0, keepdims=True)

    def body(zg_ref, zv_ref, pg_ref, pv_ref, da_ref, wg_ref, wv_ref, bg_ref, bv_ref,
             dzg_ref, dzv_ref, dwg_ref, dwv_ref, dbg_ref, dbv_ref, cg, cv):
        step = pl.program_id(1)

        @pl.when(step == 0)
        def _():
            cg[...] = jnp.zeros_like(cg)
            cv[...] = jnp.zeros_like(cv)
            for ref in (dwg_ref, dwv_ref, dbg_ref, dbv_ref):
                ref[...] = jnp.zeros_like(ref)

        is_first_tile = step == nt - 1
        zg, zv = zg_ref[...], zv_ref[...]
        pg = jnp.where(is_first_tile, 0.0, pg_ref[...])
        pv = jnp.where(is_first_tile, 0.0, pv_ref[...])
        wg, wv = wg_ref[...], wv_ref[...]
        gate, zg1, zg2 = _conv(zg, pg, wg, bg_ref[...])
        val, zv1, zv2 = _conv(zv, pv, wv, bv_ref[...])
        act_g, vjp = jax.vjp(_gelu, gate)
        da = da_ref[...]
        dgate = vjp(da * val)[0]
        dval = da * act_g
        dzg, dwg, dbg = half_bwd(dgate, zg, zg1, zg2, wg, cg[...])
        dzv, dwv, dbv = half_bwd(dval, zv, zv1, zv2, wv, cv[...])
        cg[...] = dgate[:SUBLANES, :]
        cv[...] = dval[:SUBLANES, :]
        dzg_ref[...] = dzg
        dzv_ref[...] = dzv
        dwg_ref[...] += dwg
        dwv_ref[...] += dwv
        dbg_ref[...] += dbg
        dbv_ref[...] += dbv

    rmap = lambda i: nt - 1 - i
    zspec = lambda off: pl.BlockSpec((tl, tc), lambda j, i: (rmap(i), j + off))
    pspec = lambda off: pl.BlockSpec((SUBLANES, tc), lambda j, i: (jnp.maximum(rmap(i) * rows_per - 1, 0), j + off))
    wspec = lambda off, r: pl.BlockSpec((r, tc), lambda j, i: (0, j + off))
    out_w = lambda r: pl.BlockSpec((r, tc), lambda j, i: (0, j))
    dzg, dzv, dwg, dwv, dbg, dbv = pl.pallas_call(
        body, grid=(nb, nt),
        in_specs=[zspec(0), zspec(nb), pspec(0), pspec(nb), pl.BlockSpec((tl, tc), lambda j, i: (rmap(i), j)),
                  wspec(0, 3), wspec(nb, 3), wspec(0, 1), wspec(nb, 1)],
        out_specs=[pl.BlockSpec((tl, tc), lambda j, i: (rmap(i), j))] * 2 + [out_w(3), out_w(3), out_w(1), out_w(1)],
        out_shape=[SDS((length, D_FF), F32)] * 2 + [SDS((3, D_FF), F32)] * 2 + [SDS((1, D_FF), F32)] * 2,
        scratch_shapes=[pltpu.VMEM((SUBLANES, tc), F32)] * 2,
        compiler_params=_params(("arbitrary", "arbitrary")), name="conv_bwd")(z, z, z, z, dact, conv_w, conv_w, conv_b, conv_b)
    return dzg, dzv, jnp.concatenate([dwg, dwv], axis=1), jnp.concatenate([dbg, dbv], axis=1)


def _s5_prep_math(a_re, a_im, ls, b_re, b_im):
    dt = jnp.exp(ls)
    er = jnp.exp(a_re * dt)
    ph = a_im * dt
    abr, abi = er * jnp.cos(ph), er * jnp.sin(ph)
    den = a_re * a_re + a_im * a_im
    nr = abr - 1.0
    cr = (nr * a_re + abi * a_im) / den
    ci = (abi * a_re - nr * a_im) / den
    return abr, abi, cr * b_re - ci * b_im, cr * b_im + ci * b_re


def _s5_prep_fwd(a_re, a_im, ls, b_re, b_im):
    def body(ar, ai, l, br, bi, o1, o2, o3, o4):
        for ref, val in zip((o1, o2, o3, o4), _s5_prep_math(ar[...], ai[...], l[...], br[...], bi[...]), strict=True):
            ref[...] = val
    return pl.pallas_call(body, out_shape=[SDS((1, S5N), F32)] * 2 + [SDS((S5C, S5N), F32)] * 2,
                          name="s5_prep_fwd")(a_re, a_im, ls, b_re, b_im)


def _s5_prep_bwd(a_re, a_im, ls, b_re, b_im, cots, gsel):
    def body(ar, ai, l, br, bi, c1, c2, c3, c4, g_ref, o1, o2, o3, o4, o5):
        _, vjp = jax.vjp(_s5_prep_math, ar[...], ai[...], l[...], br[...], bi[...])
        d_ar, d_ai, d_ls, d_br, d_bi = vjp((c1[...], c2[...], c3[...], c4[...]))
        o1[...] = d_ar
        o2[...] = d_ai
        o3[...] = _hdot(jnp.broadcast_to(d_ls, (SUBLANES, S5N)), g_ref[...])
        o4[...] = d_br
        o5[...] = d_bi
    return pl.pallas_call(body, out_shape=[SDS((1, S5N), F32)] * 2 + [SDS((SUBLANES, LANES), F32)] + [SDS((S5C, S5N), F32)] * 2,
                          name="s5_prep_bwd")(a_re, a_im, ls, b_re, b_im, *cots, gsel)


def _cmul(ar, ai, br, bi):
    return ar * br - ai * bi, ar * bi + ai * br


def _s5_powers(abr, abi):
    shp = (SUBLANES, S5N)
    a1 = (jnp.broadcast_to(abr, shp), jnp.broadcast_to(abi, shp))
    a2 = _cmul(*a1, *a1)
    a4 = _cmul(*a2, *a2)
    row = lax.broadcasted_iota(jnp.int32, shp, 0)
    pr, pi = a1
    cur = a1
    for i in range(1, SUBLANES):
        cur = _cmul(*cur, *a1)
        pr = jnp.where(row == i, cur[0], pr)
        pi = jnp.where(row == i, cur[1], pi)
    return a1, a2, a4, (pr, pi)


def _s5_scan(sre_ref, sim_ref, car_re, car_im, abr, abi, n_rows, reverse):
    a1, a2, a4, (pr, pi) = _s5_powers(abr, abi)
    sgn = -1.0 if reverse else 1.0
    row = lax.broadcasted_iota(jnp.int32, (SUBLANES, S5N), 0)
    if reverse:
        qr, qi = pr, pi
        for i in range(SUBLANES):
            src = SUBLANES - 1 - i
            qr = jnp.where(row == i, jnp.broadcast_to(pr[src:src + 1, :], pr.shape), qr)
            qi = jnp.where(row == i, jnp.broadcast_to(pi[src:src + 1, :], pi.shape), qi)
        pr, pi = qr, qi
    nblk = n_rows // SUBLANES

    def blk(i, carry):
        cr, ci = carry
        b = (nblk - 1 - i) if reverse else i
        sl = pl.ds(pl.multiple_of(b * SUBLANES, SUBLANES), SUBLANES)
        xr, xi = sre_ref[sl, :], sim_ref[sl, :]
        for kk, (er, ei) in ((1, a1), (2, a2), (4, a4)):
            if reverse:
                sr = jnp.where(row < SUBLANES - kk, pltpu.roll(xr, SUBLANES - kk, 0), 0.0)
                si = jnp.where(row < SUBLANES - kk, pltpu.roll(xi, SUBLANES - kk, 0), 0.0)
            else:
                sr = jnp.where(row >= kk, pltpu.roll(xr, kk, 0), 0.0)
                si = jnp.where(row >= kk, pltpu.roll(xi, kk, 0), 0.0)
            dr, di = _cmul(er, sgn * ei, sr, si)
            xr, xi = xr + dr, xi + di
        dr, di = _cmul(pr, sgn * pi, cr, ci)
        xr, xi = xr + dr, xi + di
        sre_ref[sl, :] = xr
        sim_ref[sl, :] = xi
        edge = 0 if reverse else SUBLANES - 1
        return (jnp.broadcast_to(xr[edge:edge + 1, :], xr.shape), jnp.broadcast_to(xi[edge:edge + 1, :], xi.shape))

    cr, ci = lax.fori_loop(0, nblk, blk, (car_re[...], car_im[...]))
    car_re[...] = cr
    car_im[...] = ci


S5_BLK = 4
S5_BN = S5N // S5_BLK


def _split2(x):
    hi = x.astype(BF16)
    return hi, (x - hi.astype(F32)).astype(BF16)


def _dot3(a, b, ca, cb):
    ah, al = _split2(a)
    bh, bl = _split2(b)
    return lax.dot_general(jnp.concatenate([ah, ah, al], axis=ca), jnp.concatenate([bh, bl, bh], axis=cb),
                           (((ca,), (cb,)), ((), ())), preferred_element_type=F32)


def _s5_project_in(u, bblk_ref, sre_ref, sim_ref):
    for j in range(S5_BLK):
        bu = _dot3(u[:, j * LANES:(j + 1) * LANES], bblk_ref[j], 1, 0)
        sre_ref[:, j * S5_BN:(j + 1) * S5_BN] = bu[:, :S5_BN]
        sim_ref[:, j * S5_BN:(j + 1) * S5_BN] = bu[:, S5_BN:]


def _s5_project_out(u, d, cblk_ref, sre_ref, sim_ref):
    ys = []
    for j in range(S5_BLK):
        sl = slice(j * S5_BN, (j + 1) * S5_BN)
        ys.append(_dot3(sre_ref[:, sl], cblk_ref[j, :S5_BN, :], 1, 0) + _dot3(sim_ref[:, sl], cblk_ref[j, S5_BN:, :], 1, 0))
    return jnp.concatenate(ys, axis=1) + d * u


def _s5_fwd(u, bblk, cblk, abr, abi, d):
    length = u.shape[0]
    tl = min(256, length)
    nt = length // tl

    def body(u_ref, b_ref, c_ref, ar_ref, ai_ref, d_ref, ys_ref, kr_ref, ki_ref, sre, sim, car_re, car_im):
        @pl.when(pl.program_id(0) == 0)
        def _():
            car_re[...] = jnp.zeros_like(car_re)
            car_im[...] = jnp.zeros_like(car_im)

        kr_ref[0] = car_re[...]
        ki_ref[0] = car_im[...]
        u_ = u_ref[...]
        _s5_project_in(u_, b_ref, sre, sim)
        _s5_scan(sre, sim, car_re, car_im, ar_ref[...], ai_ref[...], tl, False)
        ys_ref[...] = _gelu(_s5_project_out(u_, d_ref[...], c_ref, sre, sim))

    full = lambda a: pl.BlockSpec(a.shape, lambda i, nd=a.ndim: (0,) * nd)
    chk = pl.BlockSpec((1, SUBLANES, S5N), lambda i: (i, 0, 0))
    return pl.pallas_call(
        body, grid=(nt,), in_specs=[pl.BlockSpec((tl, S5W), lambda i: (i, 0)), full(bblk), full(cblk), full(abr), full(abi), full(d)],
        out_specs=[pl.BlockSpec((tl, S5W), lambda i: (i, 0)), chk, chk],
        out_shape=[SDS((length, S5W), F32), SDS((nt, SUBLANES, S5N), F32), SDS((nt, SUBLANES, S5N), F32)],
        scratch_shapes=[pltpu.VMEM((tl, S5N), F32)] * 2 + [pltpu.VMEM((SUBLANES, S5N), F32)] * 2,
        compiler_params=_params(), name="s5_fwd")(u, bblk, cblk, abr, abi, d)


def _s5_bwd(u, dys_a, dys_b, kre, kim, bblk, cblk, abr, abi, d):
    length = u.shape[0]
    tl = min(256, length)
    nt = length // tl

    def body(u_ref, da_ref, db_ref, kr_ref, ki_ref, b_ref, c_ref, ar_ref, ai_ref, d_ref,
             du_ref, dB_ref, dC_ref, dar_ref, dai_ref, dd_ref, sre, sim, gre, gim, car_re, car_im, dcar_re, dcar_im):
        @pl.when(pl.program_id(0) == 0)
        def _():
            dcar_re[...] = jnp.zeros_like(dcar_re)
            dcar_im[...] = jnp.zeros_like(dcar_im)
            for ref in (dB_ref, dC_ref, dar_ref, dai_ref, dd_ref):
                ref[...] = jnp.zeros_like(ref)

        u_ = u_ref[...]
        abr_, abi_, d_ = ar_ref[...], ai_ref[...], d_ref[...]
        car_re[...] = kr_ref[0]
        car_im[...] = ki_ref[0]
        _s5_project_in(u_, b_ref, sre, sim)
        _s5_scan(sre, sim, car_re, car_im, abr_, abi_, tl, False)
        y = _s5_project_out(u_, d_, c_ref, sre, sim)
        _, vjp = jax.vjp(_gelu, y)
        dy = vjp(da_ref[...] + db_ref[...])[0]
        dd_ref[...] += jnp.sum(dy * u_, axis=0, keepdims=True)
        for j in range(S5_BLK):
            sl = slice(j * S5_BN, (j + 1) * S5_BN)
            dyj = dy[:, j * LANES:(j + 1) * LANES]
            gre[:, sl] = _dot3(dyj, c_ref[j, :S5_BN, :], 1, 1)
            gim[:, sl] = _dot3(dyj, c_ref[j, S5_BN:, :], 1, 1)
            dC_ref[j, :S5_BN, :] += _dot3(sre[:, sl], dyj, 0, 0)
            dC_ref[j, S5_BN:, :] += _dot3(sim[:, sl], dyj, 0, 0)
        _s5_scan(gre, gim, dcar_re, dcar_im, abr_, abi_, tl, True)
        gr, gi = gre[...], gim[...]
        pr = _shift_down(sre[...], kr_ref[0, 0:1, :])
        pi = _shift_down(sim[...], ki_ref[0, 0:1, :])
        dar_ref[...] += jnp.sum(gr * pr + gi * pi, axis=0, keepdims=True)
        dai_ref[...] += jnp.sum(gi * pr - gr * pi, axis=0, keepdims=True)
        dus = []
        for j in range(S5_BLK):
            sl = slice(j * S5_BN, (j + 1) * S5_BN)
            uj = u_[:, j * LANES:(j + 1) * LANES]
            dus.append(_dot3(gre[:, sl], b_ref[j, :, :S5_BN], 1, 1) + _dot3(gim[:, sl], b_ref[j, :, S5_BN:], 1, 1))
            dB_ref[j, :, :S5_BN] += _dot3(uj, gre[:, sl], 0, 0)
            dB_ref[j, :, S5_BN:] += _dot3(uj, gim[:, sl], 0, 0)
        du_ref[...] = jnp.concatenate(dus, axis=1) + d_ * dy

    rmap = lambda i: nt - 1 - i
    full = lambda a: pl.BlockSpec(a.shape, lambda i, nd=a.ndim: (0,) * nd)
    row = pl.BlockSpec((tl, S5W), lambda i: (rmap(i), 0))
    chk = pl.BlockSpec((1, SUBLANES, S5N), lambda i: (rmap(i), 0, 0))
    return pl.pallas_call(
        body, grid=(nt,), in_specs=[row, row, row, chk, chk, full(bblk), full(cblk), full(abr), full(abi), full(d)],
        out_specs=[row, full(bblk), full(cblk), full(abr), full(abi), full(d)],
        out_shape=[SDS((length, S5W), F32), SDS(bblk.shape, F32), SDS(cblk.shape, F32), SDS(abr.shape, F32),
                   SDS(abi.shape, F32), SDS(d.shape, F32)],
        scratch_shapes=[pltpu.VMEM((tl, S5N), F32)] * 4 + [pltpu.VMEM((SUBLANES, S5N), F32)] * 4,
        compiler_params=_params(), name="s5_bwd")(u, dys_a, dys_b, kre, kim, bblk, cblk, abr, abi, d)


def _s5_blockdiag_in(bb):
    t = bb.reshape(S5C, S5_BLK, S5_BN).transpose(1, 0, 2)
    t = jnp.tile(t, (1, SUBLANES, 1))
    mask = (lax.broadcasted_iota(jnp.int32, (LANES, S5_BN), 0) // S5C) == (lax.broadcasted_iota(jnp.int32, (LANES, S5_BN), 1) // S5P)
    return jnp.where(mask[None], t, 0.0)


def _s5_blockdiag_in_t(dblk):
    t = dblk.reshape(S5_BLK, SUBLANES, S5C, SUBLANES, S5P)
    t = jnp.diagonal(t, axis1=1, axis2=3)
    return t.transpose(1, 0, 3, 2).reshape(S5C, S5N)


def _s5_blockdiag_out(c):
    t = c.reshape(S5_BLK, SUBLANES, S5C, S5P).transpose(0, 3, 1, 2).reshape(S5_BLK, S5P, LANES)
    t = jnp.tile(t, (1, SUBLANES, 1))
    mask = (lax.broadcasted_iota(jnp.int32, (S5_BN, LANES), 0) // S5P) == (lax.broadcasted_iota(jnp.int32, (S5_BN, LANES), 1) // S5C)
    return jnp.where(mask[None], t, 0.0)


def _s5_blockdiag_out_t(dblk):
    t = dblk.reshape(S5_BLK, SUBLANES, S5P, SUBLANES, S5C)
    t = jnp.diagonal(t, axis1=1, axis2=3)
    return t.transpose(0, 3, 2, 1).reshape(S5G, S5C, S5P)


def _wkv_consts():
    lane = lax.broadcasted_iota(jnp.int32, (4 * HN, LANES), 1)
    row = lax.broadcasted_iota(jnp.int32, (4 * HN, LANES), 0)
    diag = ((lane % HN) == (row % HN)).astype(F32)
    ones = _block_ones(LANES, HN).astype(BF16)
    return lane % HN, diag, jnp.concatenate([ones, ones], axis=0)


def _wkv_split(p):
    hi = p.astype(BF16)
    lo = (p - hi.astype(F32)).astype(BF16)
    return jnp.concatenate([hi, lo], axis=1)


def _wkv_rowseg(p, ones):
    return jnp.dot(_wkv_split(p), ones, preferred_element_type=F32)


def _wkv_rowseg_all(lhs_ref, ones, out_ref):
    c, ns = lhs_ref.shape[0], lhs_ref.shape[1]
    out_ref[...] = jnp.dot(lhs_ref[...].reshape(c * ns, 2 * LANES), ones, preferred_element_type=F32).reshape(c, ns, LANES)


def _wkv_full(qb_ref, t):
    q = qb_ref[t]
    return jnp.concatenate([jnp.tile(q[:, hp * LANES:(hp + 1) * LANES], (SUBLANES, 1)) for hp in range(4)], axis=0)


def _wkv_colsum8(x):
    return jnp.concatenate([x[hp * HN:(hp + 1) * HN].reshape(SUBLANES, SUBLANES, LANES).sum(axis=0) for hp in range(4)], axis=1)


def _wkv_prebroadcast(src_refs, dst_refs, n):
    for s, dref in zip(src_refs, dst_refs, strict=True):
        for tt in range(n):
            dref[tt] = jnp.broadcast_to(s[tt:tt + 1, :], (SUBLANES, RW))


def _wkv_transpose_out(acc, c, diag):
    lane = lax.broadcasted_iota(jnp.int32, (c, LANES), 1)
    outs = []
    for hp in range(4):
        zp = lax.dot_general(acc[hp * HN:(hp + 1) * HN], diag[:HN], (((0,), (0,)), ((), ())), precision=HI,
                             preferred_element_type=F32)
        outs.append(jnp.where(lane < HN, zp[:c], zp[HN:HN + c]))
    return jnp.concatenate(outs, axis=1)


def _wkv_fwd(r, w, k, v, a, b):
    length = r.shape[0]
    c = min(WKV_CHUNK, length)
    nc = length // c
    ns = 4 * HN

    def body(r_ref, w_ref, k_ref, v_ref, a_ref, b_ref, y_ref, sst_ref, sast_ref, fin_ref,
             s_ref, sa_ref, acc_ref, vst, lhs_ref, rb, wb, kb, vb, ab, bb):
        @pl.when(pl.program_id(0) == 0)
        def _():
            s_ref[...] = jnp.zeros_like(s_ref)

        _wkv_prebroadcast((r_ref, w_ref, k_ref, v_ref, a_ref, b_ref), (rb, wb, kb, vb, ab, bb), c)
        lane_t, diag, ones = _wkv_consts()

        def fill_v(t, carry):
            lhs_ref[t] = _wkv_split(_wkv_full(vb, t) * diag)
            return carry

        lax.fori_loop(0, c, fill_v, 0)
        _wkv_rowseg_all(lhs_ref, ones, vst)
        sa_ref[...] = _wkv_rowseg(s_ref[...] * _wkv_full(ab, 0), ones)

        def step(t, carry):
            s, sa = s_ref[...], sa_ref[...]
            sst_ref[t] = s
            sast_ref[t] = sa
            s = s * _wkv_full(wb, t) + sa * _wkv_full(bb, t) + vst[t] * _wkv_full(kb, t)
            s_ref[...] = s
            sa_ref[...] = _wkv_rowseg(s * _wkv_full(ab, jnp.minimum(t + 1, c - 1)), ones)
            lhs_ref[t] = _wkv_split(s * _wkv_full(rb, t))
            return carry

        lax.fori_loop(0, c, step, 0)
        _wkv_rowseg_all(lhs_ref, ones, vst)
        acc_ref[...] = jnp.zeros_like(acc_ref)

        def gather_y(t, carry):
            acc_ref[...] = jnp.where(lane_t == t, vst[t], acc_ref[...])
            return carry

        lax.fori_loop(0, c, gather_y, 0)
        y_ref[...] = _wkv_transpose_out(acc_ref[...], c, diag)
        fin_ref[...] = s_ref[...]

    row = pl.BlockSpec((c, RW), lambda i: (i, 0))
    st = pl.BlockSpec((c, ns, LANES), lambda i: (i, 0, 0))
    state = pltpu.VMEM((ns, LANES), F32)
    return pl.pallas_call(
        body, grid=(nc,), in_specs=[row] * 6,
        out_specs=[row, st, st, pl.BlockSpec((ns, LANES), lambda i: (0, 0))],
        out_shape=[SDS((length, RW), F32), SDS((length, ns, LANES), F32), SDS((length, ns, LANES), F32), SDS((ns, LANES), F32)],
        scratch_shapes=[state] * 3 + [pltpu.VMEM((c, ns, LANES), F32), pltpu.VMEM((c, ns, 2 * LANES), BF16)]
        + [pltpu.VMEM((c, SUBLANES, RW), F32)] * 6,
        compiler_params=_params(), name="wkv_fwd")(r, w, k, v, a, b)


def _wkv_bwd(r, w, k, v, a, b, dy, sst, sast, fin):
    length = r.shape[0]
    c = min(WKV_CHUNK, length)
    nc = length // c
    ns = 4 * HN

    def body(r_ref, w_ref, k_ref, v_ref, a_ref, b_ref, dy_ref, sst_ref, sast_ref, snext_ref, fin_ref,
             dr_ref, dw_ref, dk_ref, da_ref, db_ref, dv_ref,
             ds_ref, cur_ref, acc_ref, vst, dyst, lhs_ref, rb, wb, kb, vb, ab, bb, dyb, p_r, p_w, p_k, p_a, p_b):
        @pl.when(pl.program_id(0) == 0)
        def _():
            ds_ref[...] = jnp.zeros_like(ds_ref)

        _wkv_prebroadcast((r_ref, w_ref, k_ref, v_ref, a_ref, b_ref, dy_ref), (rb, wb, kb, vb, ab, bb, dyb), c)
        lane_t, diag, ones = _wkv_consts()

        for src, dst in ((vb, vst), (dyb, dyst)):
            def fill(t, carry, src=src):
                lhs_ref[t] = _wkv_split(_wkv_full(src, t) * diag)
                return carry

            lax.fori_loop(0, c, fill, 0)
            _wkv_rowseg_all(lhs_ref, ones, dst)
        cur_ref[...] = jnp.where(pl.program_id(0) == 0, fin_ref[...], snext_ref[0])

        def bwd(i, carry):
            t = c - 1 - i
            s_t, s_prev, sa, vv, dyy = cur_ref[...], sst_ref[t], sast_ref[t], vst[t], dyst[t]
            ds = ds_ref[...] + dyy * _wkv_full(rb, t)
            dsa = _wkv_rowseg(ds * _wkv_full(bb, t), ones)
            ds_ref[...] = ds * _wkv_full(wb, t) + dsa * _wkv_full(ab, t)
            cur_ref[...] = s_prev
            lhs_ref[t] = _wkv_split(ds * _wkv_full(kb, t))
            p_r[t] = _wkv_colsum8(s_t * dyy)
            p_k[t] = _wkv_colsum8(ds * vv)
            p_b[t] = _wkv_colsum8(ds * sa)
            p_w[t] = _wkv_colsum8(ds * s_prev)
            p_a[t] = _wkv_colsum8(s_prev * dsa)
            return carry

        lax.fori_loop(0, c, bwd, 0)
        _wkv_rowseg_all(lhs_ref, ones, vst)
        acc_ref[...] = jnp.zeros_like(acc_ref)

        def gather_dv(t, carry):
            acc_ref[...] = jnp.where(lane_t == t, vst[t], acc_ref[...])
            return carry

        lax.fori_loop(0, c, gather_dv, 0)
        sel =(lax.broadcasted_iota(jnp.int32, (c, c * SUBLANES), 1) // SUBLANES
               == lax.broadcasted_iota(jnp.int32, (c, c * SUBLANES), 0)).astype(F32)
        for out_ref, part in ((dr_ref, p_r), (dw_ref, p_w), (dk_ref, p_k), (da_ref, p_a), (db_ref, p_b)):
            out_ref[...] = _hdot(sel, part[...].reshape(c * SUBLANES, RW))
        dv_ref[...] = _wkv_transpose_out(acc_ref[...], c, diag)

    rmap = lambda i: nc - 1 - i
    row = pl.BlockSpec((c, RW), lambda i: (rmap(i), 0))
    st = pl.BlockSpec((c, ns, LANES), lambda i: (rmap(i), 0, 0))
    nxt = pl.BlockSpec((1, ns, LANES), lambda i: (jnp.minimum((rmap(i) + 1) * c, length - 1), 0, 0))
    state = pltpu.VMEM((ns, LANES), F32)
    return pl.pallas_call(
        body, grid=(nc,), in_specs=[row] * 7 + [st, st, nxt, pl.BlockSpec((ns, LANES), lambda i: (0, 0))],
        out_specs=[row] * 6, out_shape=[SDS((length, RW), F32)] * 6,
        scratch_shapes=[state] * 3 + [pltpu.VMEM((c, ns, LANES), F32)] * 2 + [pltpu.VMEM((c, ns, 2 * LANES), BF16)]
        + [pltpu.VMEM((c, SUBLANES, RW), F32)] * 12,
        compiler_params=_params(), name="wkv_bwd")(r, w, k, v, a, b, dy, sst, sast, sst, fin)


def _my_place():
    return lax.axis_index("x"), lax.axis_index("y"), lax.axis_index("c")


def _peer(x, y, c, k):
    return (x ^ ((k >> 2) & 1), y ^ ((k >> 1) & 1), c ^ (k & 1))


def _all_gather(shard, name):
    rows = shard.shape[0]

    def body(in_ref, out_ref, send_sems, recv_sems, local_sem):
        x, y, c = _my_place()
        me = 4 * x + 2 * y + c
        mine = pltpu.make_async_copy(in_ref, out_ref.at[me], local_sem)
        mine.start()
        sends = []
        for k in range(1, N_DEV):
            cp = pltpu.make_async_remote_copy(src_ref=in_ref, dst_ref=out_ref.at[me], send_sem=send_sems.at[k - 1],
                                              recv_sem=recv_sems.at[k - 1], device_id=_peer(x, y, c, k), device_id_type=MESH)
            cp.start()
            sends.append(cp)
        for k in range(1, N_DEV):
            px, py, pc = _peer(x, y, c, k)
            pltpu.make_async_remote_copy(src_ref=in_ref, dst_ref=out_ref.at[4 * px + 2 * py + pc], send_sem=send_sems.at[k - 1],
                                         recv_sem=recv_sems.at[k - 1], device_id=(px, py, pc), device_id_type=MESH).wait_recv()
        for cp in sends:
            cp.wait_send()
        mine.wait()

    return pl.pallas_call(
        body, out_shape=SDS((N_DEV, rows, LANES), shard.dtype),
        in_specs=[pl.BlockSpec(memory_space=pl.ANY)], out_specs=pl.BlockSpec(memory_space=pl.ANY),
        scratch_shapes=[pltpu.SemaphoreType.DMA((N_DEV - 1,)), pltpu.SemaphoreType.DMA((N_DEV - 1,)), pltpu.SemaphoreType.DMA],
        name=name)(shard)


def _all_to_all(chunks):
    rows = chunks.shape[1]

    def body(in_ref, out_ref, send_sems, recv_sems, local_sem):
        x, y, c = _my_place()
        me = 4 * x + 2 * y + c
        mine = pltpu.make_async_copy(in_ref.at[me], out_ref.at[me], local_sem)
        mine.start()
        sends = []
        for k in range(1, N_DEV):
            px, py, pc = _peer(x, y, c, k)
            cp = pltpu.make_async_remote_copy(src_ref=in_ref.at[4 * px + 2 * py + pc], dst_ref=out_ref.at[me],
                                              send_sem=send_sems.at[k - 1], recv_sem=recv_sems.at[k - 1],
                                              device_id=(px, py, pc), device_id_type=MESH)
            cp.start()
            sends.append(cp)
        for k in range(1, N_DEV):
            px, py, pc = _peer(x, y, c, k)
            pltpu.make_async_remote_copy(src_ref=in_ref.at[me], dst_ref=out_ref.at[4 * px + 2 * py + pc], send_sem=send_sems.at[k - 1],
                                         recv_sem=recv_sems.at[k - 1], device_id=(px, py, pc), device_id_type=MESH).wait_recv()
        for cp in sends:
            cp.wait_send()
        mine.wait()

    return pl.pallas_call(
        body, out_shape=SDS((N_DEV, rows, LANES), chunks.dtype),
        in_specs=[pl.BlockSpec(memory_space=pl.ANY)], out_specs=pl.BlockSpec(memory_space=pl.ANY),
        scratch_shapes=[pltpu.SemaphoreType.DMA((N_DEV - 1,)), pltpu.SemaphoreType.DMA((N_DEV - 1,)), pltpu.SemaphoreType.DMA],
        name="exchange_grads")(chunks)


def _sum_parts(parts):
    rows = parts.shape[1]

    def body(p_ref, g_out):
        g = p_ref[0]
        for s in range(1, N_DEV):
            g = g + p_ref[s]
        g_out[...] = g

    return pl.pallas_call(
        body, grid=(rows // PACK_ROWS,), in_specs=[pl.BlockSpec((N_DEV, PACK_ROWS, LANES), lambda i: (0, i, 0))],
        out_specs=pl.BlockSpec((PACK_ROWS, LANES), lambda i: (i, 0)), out_shape=SDS((rows, LANES), F32),
        compiler_params=_params(("parallel",)), name="sum_grads")(parts)


def _adamw(g, wgt, m, v):
    rows = wgt.shape[0]

    def body(g_ref, w_ref, m_ref, v_ref, d_out, m_out, v_out):
        g_ = g_ref[...]
        m_new = ADAM_B1 * m_ref[...] + (1.0 - ADAM_B1) * g_
        v_new = ADAM_B2 * v_ref[...] + (1.0 - ADAM_B2) * (g_ * g_)
        m_hat = m_new / (1.0 - ADAM_B1 ** ADAM_STEP)
        v_hat = v_new / (1.0 - ADAM_B2 ** ADAM_STEP)
        d_out[...] = -ADAM_LR * (m_hat / (jnp.sqrt(v_hat) + ADAM_EPS) + ADAM_WD * w_ref[...])
        m_out[...] = m_new
        v_out[...] = v_new

    blk = pl.BlockSpec((PACK_ROWS, LANES), lambda i: (i, 0))
    return pl.pallas_call(
        body, grid=(rows // PACK_ROWS,), in_specs=[blk] * 4, out_specs=[blk] * 3, out_shape=[SDS((rows, LANES), F32)] * 3,
        compiler_params=_params(("parallel",)), name="adamw")(g, wgt, m, v)


PACK_ALIGN = 2 * SUBLANES * LANES
PACK_ROWS = 512

ROW_SHARDED = ("s5_w_glu", "w_out", "ffn_w_down")
TRANSPOSED = ("w_in", "ffn_w_up", "w_branch_rwkv", "w_branch_s5")
SMALL_SHARDED = ("rwkv_w2", "rwkv_a2", "rwkv_g2", "ffn_conv_w")
SHARDED = ROW_SHARDED + TRANSPOSED + SMALL_SHARDED
F32_GATHER = ("ffn_conv_w",)
REPLICATED =("norm_mix_pre", "norm_mix_post", "norm_ffn_pre", "norm_ffn_post", "b_gate", "rwkv_shift_mu", "rwkv_w0",
              "rwkv_a0", "rwkv_k_k", "rwkv_k_a", "rwkv_r_k", "rwkv_lnx_w", "rwkv_lnx_b", "s5_a_re", "s5_a_im", "s5_b_re",
              "s5_b_im", "s5_c_re", "s5_c_im", "s5_d", "s5_log_step", "s5_b_glu", "ffn_conv_b")
WEIGHTS = ("norm_mix_pre", "norm_mix_post", "norm_ffn_pre", "norm_ffn_post", "w_in", "b_gate", "rwkv_shift_mu", "rwkv_w0",
           "rwkv_w2", "rwkv_a0", "rwkv_a2", "rwkv_g2", "rwkv_k_k", "rwkv_k_a", "rwkv_r_k", "rwkv_lnx_w", "rwkv_lnx_b",
           "s5_a_re", "s5_a_im", "s5_b_re", "s5_b_im", "s5_c_re", "s5_c_im", "s5_d", "s5_log_step", "s5_w_glu", "s5_b_glu",
           "w_branch_rwkv", "w_branch_s5", "w_out", "ffn_w_up", "ffn_conv_w", "ffn_conv_b", "ffn_w_down")


def _pad_flat(a):
    flat = a.reshape(-1)
    pad = (-flat.shape[0]) % PACK_ALIGN
    return jnp.pad(flat, (0, pad)) if pad else flat


def _pad_cols(a, mult):
    pad = (-a.shape[1]) % mult
    return jnp.pad(a, ((0, 0), (0, pad))) if pad else a


def _pack(pieces):
    flat = jnp.concatenate([_pad_flat(p) for p in pieces])
    pad = (-flat.shape[0]) % (PACK_ROWS * LANES)
    return (jnp.pad(flat, (0, pad)) if pad else flat).reshape(-1, LANES)


def _unpack(buf, shapes):
    lead = buf.shape[:-2]
    flat = buf.reshape(*lead, -1)
    out, off = [], 0
    for s in shapes:
        n = math.prod(s)
        out.append(flat[..., off:off + n].reshape(*lead, *s))
        off += n + (-n) % PACK_ALIGN
    return out


def _rows_to_full(stack):
    return stack.reshape(-1, stack.shape[2])


def _cols_to_full(stack):
    return stack.transpose(1, 0, 2).reshape(stack.shape[1], -1)


def _local_step(x, tgt, wt):
    o512 = _block_ones(RW, HN)
    w_in = wt["w_in"]
    w_rw, w_u, w_g = w_in[:N_RWKV], w_in[N_RWKV:N_RWKV + S5W], w_in[N_RWKV + S5W:]
    b_gate = wt["b_gate"]
    b_r, b_s = b_gate[:, :D_MODEL], b_gate[:, D_MODEL:]
    zpad = lambda a, lo, n: jnp.pad(a.astype(F32), ((lo, n - lo - a.shape[0]), (0, 0)))
    w2p, a2p, g2p = zpad(wt["rwkv_w2"], 0, 256), zpad(wt["rwkv_a2"], 64, 256), zpad(wt["rwkv_g2"], 128, 256)
    pre_small = [wt["rwkv_w0"], wt["rwkv_a0"], wt["rwkv_k_k"], wt["rwkv_k_a"], w2p, a2p, g2p]
    post_prm = [wt["rwkv_lnx_w"], wt["rwkv_lnx_b"], wt["rwkv_r_k"].reshape(1, RW)]
    mu = wt["rwkv_shift_mu"]

    a_re, a_im = wt["s5_a_re"].reshape(1, S5N), wt["s5_a_im"].reshape(1, S5N)
    ls = jnp.repeat(wt["s5_log_step"].reshape(S5G), S5P).reshape(1, S5N)
    b_re_t = wt["s5_b_re"].reshape(S5N, S5C).T
    b_im_t = wt["s5_b_im"].reshape(S5N, S5C).T
    c_re, c_im = wt["s5_c_re"].reshape(S5G, S5C, S5P), wt["s5_c_im"].reshape(S5G, S5C, S5P)
    abr, abi, bbr, bbi = _s5_prep_fwd(a_re, a_im, ls, b_re_t, b_im_t)
    bblk = jnp.concatenate([_s5_blockdiag_in(bbr), _s5_blockdiag_in(bbi)], axis=2)
    cblk = jnp.concatenate([_s5_blockdiag_out(c_re), -_s5_blockdiag_out(c_im)], axis=1)
    s5_d = wt["s5_d"]

    h1 = _rms_fwd(x, wt["norm_mix_pre"], "rms1_fwd")
    p_rw = _mm(h1, w_rw, tb=True, name="mm_proj_rwkv")
    u = _mm(h1, w_u, tb=True, name="mm_proj_s5")
    gpre = _mm(h1, w_g, tb=True, name="mm_proj_gate")
    r, decay, k2, v, aa, bb, g = _rwkv_pre_fwd(p_rw, mu, pre_small, o512)
    y, sst, sast, s_fin = _wkv_fwd(r, decay, k2, v, aa, bb)
    y_r = _rwkv_post_fwd(y, r, k2, v, g, post_prm, o512)
    o_r = _mm(y_r, wt["w_branch_rwkv"], tb=True, name="mm_branch_rwkv")
    ys, kre, kim = _s5_fwd(u, bblk, cblk, abr, abi, s5_d)
    t_glu = _mm(ys, wt["s5_w_glu"], name="mm_glu")
    out5 = _glu_fwd(ys, t_glu, wt["s5_b_glu"])
    o_s = _mm(out5, wt["w_branch_s5"], tb=True, name="mm_branch_s5")
    mi = _merge_fwd(gpre, o_r, o_s, b_r, b_s)
    mixed = _mm(mi, wt["w_out"], name="mm_out")
    x2, h2 = _mid_fwd(x, mixed, wt["norm_mix_post"], wt["norm_ffn_pre"])
    z = _mm(h2, wt["ffn_w_up"], tb=True, name="mm_up")
    act = _conv_fwd(z, wt["ffn_conv_w"], wt["ffn_conv_b"])
    f = _mm(act, wt["ffn_w_down"], name="mm_down")
    dx3, df, d_norm_ffn_post, loss_part = _final(x2, f, tgt, wt["norm_ffn_post"])

    gr = {"norm_ffn_post": d_norm_ffn_post}
    dact = _mm(df, wt["ffn_w_down"], tb=True, name="mm_down_dx")
    gr["ffn_w_down"] = _mm(act, df, ta=True, name="mm_down_dw")
    dzg, dzv, gr["ffn_conv_w"], gr["ffn_conv_b"] = _conv_bwd(z, dact, wt["ffn_conv_w"], wt["ffn_conv_b"])
    dz = jnp.concatenate([dzg, dzv], axis=1)
    dh2 = _mm(dz, wt["ffn_w_up"], name="mm_up_dx")
    gr["ffn_w_up"] = _mm(dz, h2, ta=True, name="mm_up_dw")
    dx2, dmixed, gr["norm_mix_post"], gr["norm_ffn_pre"] = _mid_bwd(x2, mixed, dh2, dx3, wt["norm_mix_post"], wt["norm_ffn_pre"])
    dmi = _mm(dmixed, wt["w_out"], tb=True, name="mm_out_dx")
    gr["w_out"] = _mm(mi, dmixed, ta=True, name="mm_out_dw")
    dgp_r, dgp_s, do_r, do_s, db_r, db_s = _merge_bwd(gpre, o_r, o_s, dmi, b_r, b_s)
    gr["b_gate"] = jnp.concatenate([db_r, db_s], axis=1)
    dout5 = _mm(do_s, wt["w_branch_s5"], name="mm_branch_s5_dx")
    gr["w_branch_s5"] = _mm(do_s, out5, ta=True, name="mm_branch_s5_dw")
    dys_a, dt_glu, gr["s5_b_glu"] = _glu_bwd(ys, t_glu, dout5, wt["s5_b_glu"])
    dys_b = _mm(dt_glu, wt["s5_w_glu"], tb=True, name="mm_glu_dx")
    gr["s5_w_glu"] = _mm(ys, dt_glu, ta=True, name="mm_glu_dw")
    du, dbblk, dcblk, dabr, dabi, gr["s5_d"] = _s5_bwd(u, dys_a, dys_b, kre, kim, bblk, cblk, abr, abi, s5_d)
    gr["s5_c_re"] = _s5_blockdiag_out_t(dcblk[:, :S5_BN, :]).reshape(wt["s5_c_re"].shape)
    gr["s5_c_im"] = (-_s5_blockdiag_out_t(dcblk[:, S5_BN:, :])).reshape(wt["s5_c_im"].shape)
    dbbr, dbbi = _s5_blockdiag_in_t(dbblk[:, :, :S5_BN]), _s5_blockdiag_in_t(dbblk[:, :, S5_BN:])
    gsel = (lax.broadcasted_iota(jnp.int32, (S5N, LANES), 0) // S5P == lax.broadcasted_iota(jnp.int32, (S5N, LANES), 1)).astype(F32)
    d_are, d_aim, d_ls, d_bre_t, d_bim_t = _s5_prep_bwd(a_re, a_im, ls, b_re_t, b_im_t, (dabr, dabi, dbbr, dbbi), gsel)
    gr["s5_a_re"] = d_are.reshape(wt["s5_a_re"].shape)
    gr["s5_a_im"] = d_aim.reshape(wt["s5_a_im"].shape)
    gr["s5_log_step"] = d_ls[0:1, :S5G]
    gr["s5_b_re"] = d_bre_t.T.reshape(wt["s5_b_re"].shape)
    gr["s5_b_im"] = d_bim_t.T.reshape(wt["s5_b_im"].shape)
    dy_r = _mm(do_r, wt["w_branch_rwkv"], name="mm_branch_rwkv_dx")
    gr["w_branch_rwkv"] = _mm(do_r, y_r, ta=True, name="mm_branch_rwkv_dw")
    dy, dr1, dk1, dv1, dg, gr["rwkv_lnx_w"], gr["rwkv_lnx_b"], d_rk = _rwkv_post_bwd(y, r, k2, v, g, dy_r, post_prm, o512)
    gr["rwkv_r_k"] = d_rk.reshape(wt["rwkv_r_k"].shape)
    dr2, ddecay, dk2, daa, dbb, dv2 = _wkv_bwd(r, decay, k2, v, aa, bb, dy, sst, sast, s_fin)
    cots = [dr1 + dr2, ddecay, dk1 + dk2, dv1 + dv2, daa, dbb, dg]
    dp_rw, gr["rwkv_shift_mu"], gr["rwkv_w0"], gr["rwkv_a0"], gr["rwkv_k_k"], gr["rwkv_k_a"], dw2p, da2p, dg2p = \
        _rwkv_pre_bwd(p_rw, cots, mu, pre_small, o512)
    gr["rwkv_w2"], gr["rwkv_a2"], gr["rwkv_g2"] = dw2p[:64], da2p[64:128], dg2p[128:]
    dproj = jnp.concatenate([dp_rw, du, dgp_r, dgp_s], axis=1)
    dh1 = _mm(dproj, w_in, name="mm_proj_dx")
    gr["w_in"] = _mm(dproj, h1, ta=True, name="mm_proj_dw")
    dx, gr["norm_mix_pre"] = _rms_bwd(x, wt["norm_mix_pre"], dh1, dx2, "rms1_bwd")
    return loss_part[0, 0], dx, gr


def kernel(x, norm_mix_pre, norm_mix_post, norm_ffn_pre, norm_ffn_post, w_in, b_gate, rwkv_shift_mu, rwkv_w0, rwkv_w2, rwkv_a0, rwkv_a2, rwkv_g2, rwkv_k_k, rwkv_k_a, rwkv_r_k, rwkv_lnx_w, rwkv_lnx_b, s5_a_re, s5_a_im, s5_b_re, s5_b_im, s5_c_re, s5_c_im, s5_d, s5_log_step, s5_w_glu, s5_b_glu, w_branch_rwkv, w_branch_s5, w_out, ffn_w_up, ffn_conv_w, ffn_conv_b, ffn_w_down, loss_target, m_norm_mix_pre, m_norm_mix_post, m_norm_ffn_pre, m_norm_ffn_post, m_w_in, m_b_gate, m_rwkv_shift_mu, m_rwkv_w0, m_rwkv_w2, m_rwkv_a0, m_rwkv_a2, m_rwkv_g2, m_rwkv_k_k, m_rwkv_k_a, m_rwkv_r_k, m_rwkv_lnx_w, m_rwkv_lnx_b, m_s5_a_re, m_s5_a_im, m_s5_b_re, m_s5_b_im, m_s5_c_re, m_s5_c_im, m_s5_d, m_s5_log_step, m_s5_w_glu, m_s5_b_glu, m_w_branch_rwkv, m_w_branch_s5, m_w_out, m_ffn_w_up, m_ffn_conv_w, m_ffn_conv_b, m_ffn_w_down, v_norm_mix_pre, v_norm_mix_post, v_norm_ffn_pre, v_norm_ffn_post, v_w_in, v_b_gate, v_rwkv_shift_mu, v_rwkv_w0, v_rwkv_w2, v_rwkv_a0, v_rwkv_a2, v_rwkv_g2, v_rwkv_k_k, v_rwkv_k_a, v_rwkv_r_k, v_rwkv_lnx_w, v_rwkv_lnx_b, v_s5_a_re, v_s5_a_im, v_s5_b_re, v_s5_b_im, v_s5_c_re, v_s5_c_im, v_s5_d, v_s5_log_step, v_s5_w_glu, v_s5_b_glu, v_w_branch_rwkv, v_w_branch_s5, v_w_out, v_ffn_w_up, v_ffn_conv_w, v_ffn_conv_b, v_ffn_w_down):
    args = dict(locals())
    wgt = {n: args[n] for n in WEIGHTS}
    mom = {n: args["m_" + n] for n in WEIGHTS}
    var = {n: args["v_" + n] for n in WEIGHTS}
    me = 4 * lax.axis_index("x") + 2 * lax.axis_index("y") + lax.axis_index("c")
    shard_shapes = {n: wgt[n].shape[1:] for n in SHARDED}
    sent_shapes = {n: (shard_shapes[n][::-1] if n in TRANSPOSED else shard_shapes[n]) for n in SHARDED}
    sent = lambda n, a: a.T if n in TRANSPOSED else a

    bf_names = [n for n in SHARDED if n not in F32_GATHER]
    got_bf = _unpack(_all_gather(_pack([sent(n, wgt[n][0].astype(BF16)) for n in bf_names]), "all_gather_weights"),
                     [sent_shapes[n] for n in bf_names])
    got_f32 = _unpack(_all_gather(_pack([wgt[n][0] for n in F32_GATHER]), "all_gather_taps"),
                      [sent_shapes[n] for n in F32_GATHER])
    wt = {n: wgt[n] for n in REPLICATED}
    for n, blocks in zip(bf_names + list(F32_GATHER), got_bf + got_f32, strict=True):
        wt[n] = _cols_to_full(blocks) if n in SMALL_SHARDED else _rows_to_full(blocks)

    loss_part, dx, gr = _local_step(x[0], loss_target[0], wt)

    big = ROW_SHARDED + TRANSPOSED
    whole = SMALL_SHARDED + REPLICATED
    whole_flat = [gr[n].reshape(-1) for n in whole]
    pieces = [_pad_cols(gr[n].reshape(N_DEV, -1), PACK_ALIGN) for n in big]
    pieces += [jnp.broadcast_to(_pad_flat(g_)[None], (N_DEV, g_.shape[0] + (-g_.shape[0]) % PACK_ALIGN)) for g_ in whole_flat]
    chunks = _pad_cols(jnp.concatenate(pieces, axis=1), PACK_ROWS * LANES).reshape(N_DEV, -1, LANES)
    g_sum = _sum_parts(_all_to_all(chunks))
    got = dict(zip(big + whole, _unpack(g_sum, [sent_shapes[n] for n in big] + [gr[n].shape for n in whole]), strict=True))
    grads = {}
    for n in WEIGHTS:
        if n in TRANSPOSED:
            grads[n] = got[n].T
        elif n in SMALL_SHARDED:
            cols = shard_shapes[n][1]
            grads[n] = lax.dynamic_slice_in_dim(got[n], me * cols, cols, axis=1)
        else:
            grads[n] = got[n]
        grads[n] = grads[n].reshape(wgt[n].shape)

    pack_local = lambda src: _pack([src[n].reshape(-1) for n in WEIGHTS])
    outs = _adamw(pack_local(grads), pack_local(wgt), pack_local(mom), pack_local(var))
    shapes = [wgt[n].shape for n in WEIGHTS]
    loss = lax.psum(loss_part, AXES)
    return (loss, dx[None], *[grads[n] for n in WEIGHTS], *_unpack(outs[0], shapes), *_unpack(outs[1], shapes),
            *_unpack(outs[2], shapes))
```

```python
import functools
import math

import jax
import jax.numpy as jnp
from jax import lax
from jax.experimental import pallas as pl
from jax.experimental.pallas import tpu as pltpu

F32 = jnp.float32
BF16 = jnp.bfloat16
SDS = jax.ShapeDtypeStruct
HI = lax.Precision.HIGHEST
MESH = pl.DeviceIdType.MESH
AXES = ("x", "y", "c")
N_DEV = 8

D_MODEL = 1024
RW = 512
HN = 64
N_RWKV = 1792
S5W = 512
S5G = 32
S5C = 16
S5P = 64
S5N = S5G * S5P
D_FF = 2816
NORM_EPS = 1e-6
LNX_EPS = 64e-5

ADAM_LR = 0.001
ADAM_B1 = 0.9
ADAM_B2 = 0.999
ADAM_EPS = 1e-08
ADAM_WD = 0.01
ADAM_STEP = 10

LANES = 128
SUBLANES = 8
VMEM_LIMIT = 56 * 1024 * 1024
WKV_CHUNK = 32


def _params(sem=("arbitrary",)):
    return pltpu.CompilerParams(dimension_semantics=sem, vmem_limit_bytes=VMEM_LIMIT)


def _pick(n, cap):
    best = None
    for t in range(LANES, min(n, cap) + 1, LANES):
        if n % t == 0:
            best = t
    return best or n


def _mm(a, b, *, ta=False, tb=False, out_dtype=F32, name):
    m = a.shape[1] if ta else a.shape[0]
    k = a.shape[0] if ta else a.shape[1]
    n = b.shape[0] if tb else b.shape[1]
    assert (b.shape[1] if tb else b.shape[0]) == k
    tm, tn, tk = _pick(m, 512), _pick(n, 2304), _pick(k, 512)
    nk = k // tk
    dims = (((0 if ta else 1,), (1 if tb else 0,)), ((), ()))

    def body(a_ref, b_ref, o_ref, acc_ref):
        kk = pl.program_id(2)

        @pl.when(kk == 0)
        def _():
            acc_ref[...] = jnp.zeros_like(acc_ref)

        acc_ref[...] += lax.dot_general(a_ref[...].astype(BF16), b_ref[...].astype(BF16), dims,
                                        preferred_element_type=F32)

        @pl.when(kk == nk - 1)
        def _():
            o_ref[...] = acc_ref[...].astype(o_ref.dtype)

    a_spec = pl.BlockSpec((tk, tm), lambda i, j, kk: (kk, i)) if ta else pl.BlockSpec((tm, tk), lambda i, j, kk: (i, kk))
    b_spec = pl.BlockSpec((tn, tk), lambda i, j, kk: (j, kk)) if tb else pl.BlockSpec((tk, tn), lambda i, j, kk: (kk, j))
    return pl.pallas_call(
        body, grid=(m // tm, n // tn, nk), in_specs=[a_spec, b_spec],
        out_specs=pl.BlockSpec((tm, tn), lambda i, j, kk: (i, j)),
        out_shape=SDS((m, n), out_dtype), scratch_shapes=[pltpu.VMEM((tm, tn), F32)],
        compiler_params=_params(("parallel", "parallel", "arbitrary")), name=name)(a, b)


def _rows(fn, rows, params, out_rows, out_accs, *, name, tl, reverse=False, scratch=()):
    first = rows[0][0] if isinstance(rows[0], tuple) else rows[0]
    length = first.shape[0]
    tl = min(tl, length)
    nt = length // tl
    rmap = (lambda i: nt - 1 - i) if reverse else (lambda i: i)
    specs, arrs = [], []
    for r in rows:
        arr, wdt, cb = r if isinstance(r, tuple) else (r, r.shape[1], 0)
        specs.append(pl.BlockSpec((tl, wdt), lambda i, cb=cb: (rmap(i), cb)))
        arrs.append(arr)
    for p in params:
        specs.append(pl.BlockSpec(p.shape, lambda i, nd=p.ndim: (0,) * nd))
        arrs.append(p)
    out_shape = [SDS((length, c), dt) for c, dt in out_rows] + [SDS(s, F32) for s in out_accs]
    out_specs = [pl.BlockSpec((tl, c), lambda i: (rmap(i), 0)) for c, _ in out_rows]
    out_specs += [pl.BlockSpec(s, lambda i, nd=len(s): (0,) * nd) for s in out_accs]
    nr, npar, nor, noa = len(rows), len(params), len(out_rows), len(out_accs)

    def body(*refs):
        rin, pin = refs[:nr], refs[nr:nr + npar]
        rout = refs[nr + npar:nr + npar + nor]
        aout = refs[nr + npar + nor:nr + npar + nor + noa]
        scr = refs[nr + npar + nor + noa:]
        step = pl.program_id(0)
        outs_r, outs_a = fn(step, [r[...] for r in rin], [p[...] for p in pin], scr)
        for ref, val in zip(rout, outs_r, strict=True):
            ref[...] = val.astype(ref.dtype)

        @pl.when(step == 0)
        def _():
            for ref in aout:
                ref[...] = jnp.zeros_like(ref)

        for ref, val in zip(aout, outs_a, strict=True):
            ref[...] += val.astype(F32)

    res = pl.pallas_call(body, grid=(nt,), in_specs=specs, out_specs=out_specs, out_shape=out_shape,
                         scratch_shapes=list(scratch), compiler_params=_params(), name=name)(*arrs)
    return list(res)


def _rms(x, g):
    return x * lax.rsqrt(jnp.mean(x * x, axis=-1, keepdims=True) + NORM_EPS) * g


def _sig(x):
    return 0.5 * (jnp.tanh(0.5 * x) + 1.0)


def _softplus(x):
    return jnp.maximum(x, 0.0) + jnp.log(1.0 + jnp.exp(-jnp.abs(x)))


def _gelu(x):
    return x * (0.5 * (1.0 + jnp.tanh(math.sqrt(2.0 / math.pi) * (x + 0.044715 * (x * x * x)))))


def _bdot(a, b):
    return jnp.dot(a.astype(BF16), b.astype(BF16), preferred_element_type=F32)


def _hdot(a, b):
    return jnp.dot(a, b, precision=HI, preferred_element_type=F32)


def _block_ones(n, blk):
    i = lax.broadcasted_iota(jnp.int32, (n, n), 0) // blk
    j = lax.broadcasted_iota(jnp.int32, (n, n), 1) // blk
    return (i == j).astype(F32)


def _rms_fwd(x, g, name):
    return _rows(lambda s, r, p, _: ([_rms(r[0], p[0])], []), [x], [g], [(x.shape[1], BF16)], [], name=name, tl=512)[0]


def _rms_bwd(x, g, dh, dres, name):
    def fn(s, r, p, _):
        _, vjp = jax.vjp(_rms, r[0], p[0])
        dx, dg = vjp(r[1])
        return [dx + r[2]], [dg]
    return _rows(fn, [x, dh, dres], [g], [(x.shape[1], F32)], [g.shape], name=name, tl=256)


def _rwkv_pre_math(k_, lr, w0, a0, k_k, k_a, w2p, a2p, g2p, o512):
    pre_w = w0 + _bdot(jnp.tanh(lr), w2p)
    w = -_softplus(-pre_w) - 0.5
    decay = jnp.exp(-jnp.exp(w))
    a = _sig(a0 + _bdot(lr, a2p))
    g = _bdot(_sig(lr), g2p)
    kr = k_ * k_k
    kk = kr / jnp.maximum(jnp.sqrt(_hdot(kr * kr, o512)), 1e-12)
    k2 = k_ * (1.0 + (a - 1.0) * k_a)
    return decay, k2, -kk, kk * a, g


def _shift_down(p, prev_row):
    row = lax.broadcasted_iota(jnp.int32, p.shape, 0)
    return jnp.where(row == 0, jnp.broadcast_to(prev_row, p.shape), pltpu.roll(p, 1, 0))


def _shift_up(q, next_row):
    n = q.shape[0]
    row = lax.broadcasted_iota(jnp.int32, q.shape, 0)
    return jnp.where(row == n - 1, jnp.broadcast_to(next_row, q.shape), pltpu.roll(q, n - 1, 0))


def _rwkv_pre_fwd(p, mu, small, o512):
    def fn(step, r, prm, scr):
        car = scr[0]

        @pl.when(step == 0)
        def _():
            car[...] = jnp.zeros_like(car)

        x = r[0]
        prev = _shift_down(x, car[SUBLANES - 1:SUBLANES, :])
        car[...] = x[x.shape[0] - SUBLANES:, :]
        xs = x + (prev - x) * prm[0]
        decay, k2, aa, bb, g = _rwkv_pre_math(xs[:, RW:2 * RW], xs[:, 3 * RW:], *prm[1:])
        return [xs[:, :RW], decay, k2, xs[:, 2 * RW:3 * RW], aa, bb, g], []
    return _rows(fn, [p], [mu, *small, o512], [(RW, F32)] * 7, [], name="rwkv_pre_fwd", tl=256,
                 scratch=[pltpu.VMEM((SUBLANES, N_RWKV), F32)])


def _rwkv_pre_bwd(p, cots, mu, small, o512):
    length = p.shape[0]
    tl = min(256, length)
    nt = length // tl
    rows_per = tl // SUBLANES
    params = [mu, *small, o512]
    acc_shapes = [mu.shape] + [q.shape for q in small]
    nr, npar, nacc = 2 + len(cots), len(params), len(acc_shapes)

    def body(*refs):
        rin, pin = refs[:nr], refs[nr:nr + npar]
        dp_ref = refs[nr + npar]
        aout = refs[nr + npar + 1:nr + npar + 1 + nacc]
        car_q = refs[nr + npar + 1 + nacc]
        step = pl.program_id(0)

        @pl.when(step == 0)
        def _():
            car_q[...] = jnp.zeros_like(car_q)
            for ref in aout:
                ref[...] = jnp.zeros_like(ref)

        x = rin[0][...]
        prev_row = jnp.where(step == nt - 1, 0.0, rin[1][SUBLANES - 1:SUBLANES, :])
        dr, ddecay, dk2, dv, daa, dbb, dg = [r[...] for r in rin[2:]]
        prm = [q[...] for q in pin]
        mu_, o512_ = prm[0], prm[-1]
        prev = _shift_down(x, prev_row)
        xs = x + (prev - x) * mu_
        _, vjp = jax.vjp(lambda k_, lr, *w: _rwkv_pre_math(k_, lr, *w, o512_), xs[:, RW:2 * RW], xs[:, 3 * RW:], *prm[1:-1])
        dk_, dlr, *dsmall = vjp((ddecay, dk2, daa, dbb, dg))
        dxs = jnp.concatenate([dr, dk_, dv, dlr], axis=1)
        q = dxs * mu_
        dp_ref[...] = dxs - q + _shift_up(q, car_q[0:1, :])
        car_q[...] = q[:SUBLANES, :]
        aout[0][...] += jnp.sum((prev - x) * dxs, axis=0, keepdims=True)
        for ref, val in zip(aout[1:], dsmall, strict=True):
            ref[...] += val

    rmap = lambda i: nt - 1 - i
    specs = [pl.BlockSpec((tl, N_RWKV), lambda i: (rmap(i), 0)),
             pl.BlockSpec((SUBLANES, N_RWKV), lambda i: (jnp.maximum(rmap(i) * rows_per - 1, 0), 0))]
    specs += [pl.BlockSpec((tl, RW), lambda i: (rmap(i), 0)) for _ in cots]
    specs += [pl.BlockSpec(q.shape, lambda i, nd=q.ndim: (0,) * nd) for q in params]
    out_shape = [SDS((length, N_RWKV), F32)] + [SDS(sh, F32) for sh in acc_shapes]
    out_specs = [pl.BlockSpec((tl, N_RWKV), lambda i: (rmap(i), 0))]
    out_specs += [pl.BlockSpec(sh, lambda i, nd=len(sh): (0,) * nd) for sh in acc_shapes]
    res = pl.pallas_call(body, grid=(nt,), in_specs=specs, out_specs=out_specs, out_shape=out_shape,
                         scratch_shapes=[pltpu.VMEM((SUBLANES, N_RWKV), F32)],
                         compiler_params=_params(), name="rwkv_pre_bwd")(p, p, *cots, *params)
    return list(res)


def _rwkv_post_math(y, r, k2, v, g, lnx_w, lnx_b, r_k, o512):
    mean = _hdot(y, o512) * (1.0 / HN)
    yc = y - mean
    var = _hdot(yc * yc, o512) * (1.0 / HN)
    yn = yc * lax.rsqrt(var + LNX_EPS) * lnx_w + lnx_b
    bonus = _hdot(r * k2 * r_k, o512) * v
    return (yn + bonus) * g


def _rwkv_post_fwd(y, r, k2, v, g, prm, o512):
    return _rows(lambda s, rr, p, _: ([_rwkv_post_math(*rr, *p)], []), [y, r, k2, v, g], [*prm, o512],
                 [(RW, BF16)], [], name="rwkv_post_fwd", tl=256)[0]


def _rwkv_post_bwd(y, r, k2, v, g, dout, prm, o512):
    def fn(s, rr, p, _):
        o = p[-1]
        _, vjp = jax.vjp(lambda *a: _rwkv_post_math(*a, o), *rr[:5], *p[:-1])
        gr = vjp(rr[5])
        return list(gr[:5]), list(gr[5:])
    return _rows(fn, [y, r, k2, v, g, dout], [*prm, o512], [(RW, F32)] * 5, [q.shape for q in prm],
                 name="rwkv_post_bwd", tl=256)


def _glu_math(ys, t, b):
    return ys * _sig(t + b)


def _glu_fwd(ys, t, b):
    return _rows(lambda s, r, p, _: ([_glu_math(r[0], r[1], p[0])], []), [ys, t], [b], [(S5W, BF16)], [],
                 name="s5_glu_fwd", tl=512)[0]


def _glu_bwd(ys, t, dout, b):
    def fn(s, r, p, _):
        _, vjp = jax.vjp(_glu_math, r[0], r[1], p[0])
        dys, dt, db = vjp(r[2])
        return [dys, dt], [db]
    return _rows(fn, [ys, t, dout], [b], [(S5W, F32)] * 2, [b.shape], name="s5_glu_bwd", tl=512)


def _merge_math(gp_r, gp_s, o_r, o_s, b_r, b_s):
    return _sig(gp_r + b_r) * o_r + _sig(gp_s + b_s) * o_s


def _merge_fwd(gpre, o_r, o_s, b_r, b_s):
    return _rows(lambda s, r, p, _: ([_merge_math(*r, *p)], []),
                 [(gpre, D_MODEL, 0), (gpre, D_MODEL, 1), o_r, o_s], [b_r, b_s], [(D_MODEL, BF16)], [],
                 name="merge_fwd", tl=256)[0]


def _merge_bwd(gpre, o_r, o_s, dmi, b_r, b_s):
    def fn(s, r, p, _):
        _, vjp = jax.vjp(_merge_math, *r[:4], *p)
        dgr, dgs, dor, dos, dbr, dbs = vjp(r[4])
        return [dgr, dgs, dor, dos], [dbr, dbs]
    return _rows(fn, [(gpre, D_MODEL, 0), (gpre, D_MODEL, 1), o_r, o_s, dmi], [b_r, b_s], [(D_MODEL, F32)] * 4,
                 [b_r.shape, b_s.shape], name="merge_bwd", tl=256)


def _mid_fwd(x, mixed, g_post, g_pre):
    def fn(s, r, p, _):
        x2 = r[0] + _rms(r[1], p[0])
        return [x2, _rms(x2, p[1])], []
    return _rows(fn, [x, mixed], [g_post, g_pre], [(D_MODEL, F32), (D_MODEL, BF16)], [], name="mid_fwd", tl=256)


def _mid_bwd(x2, mixed, dh2, dx3, g_post, g_pre):
    def fn(s, r, p, _):
        _, vjp1 = jax.vjp(_rms, r[0], p[1])
        dx2, dg_pre = vjp1(r[2])
        dx2 = dx2 + r[3]
        _, vjp2 = jax.vjp(_rms, r[1], p[0])
        dmixed, dg_post = vjp2(dx2)
        return [dx2, dmixed], [dg_post, dg_pre]
    return _rows(fn, [x2, mixed, dh2, dx3], [g_post, g_pre], [(D_MODEL, F32)] * 2, [g_post.shape, g_pre.shape],
                 name="mid_bwd", tl=256)


def _final(x2, f, tgt, g_post):
    def fn(s, r, p, _):
        y, vjp = jax.vjp(_rms, r[1], p[0])
        diff = r[0] + y - r[2]
        dx3 = diff * (1.0 / D_MODEL)
        df, dg = vjp(dx3)
        part = 0.5 * jnp.sum(jnp.sum(diff * diff, axis=1, keepdims=True), axis=0, keepdims=True) * (1.0 / D_MODEL)
        return [dx3, df], [dg, jnp.broadcast_to(part, (1, LANES))]
    return _rows(fn, [x2, f, tgt], [g_post], [(D_MODEL, F32)] * 2, [g_post.shape, (1, LANES)], name="final", tl=256)


def _conv_taps(z, car):
    row = lax.broadcasted_iota(jnp.int32, z.shape, 0)
    z1 = jnp.where(row == 0, jnp.broadcast_to(car[7:8, :], z.shape), pltpu.roll(z, 1, 0))
    z2 = pltpu.roll(z, 2, 0)
    z2 = jnp.where(row == 0, jnp.broadcast_to(car[6:7, :], z.shape), z2)
    z2 = jnp.where(row == 1, jnp.broadcast_to(car[7:8, :], z.shape), z2)
    return z1, z2


def _conv(z, car, w, b):
    z1, z2 = _conv_taps(z, car)
    return b + w[0:1, :] * z2 + w[1:2, :] * z1 + w[2:3, :] * z, z1, z2


def _conv_fwd(z, conv_w, conv_b):
    length = z.shape[0]
    tl = min(256, length)
    nt = length // tl
    tc = _pick(D_FF, 1536)
    nb = D_FF // tc

    def body(zg_ref, zv_ref, wg_ref, wv_ref, bg_ref, bv_ref, o_ref, cg, cv):
        @pl.when(pl.program_id(1) == 0)
        def _():
            cg[...] = jnp.zeros_like(cg)
            cv[...] = jnp.zeros_like(cv)

        zg, zv = zg_ref[...], zv_ref[...]
        gate, _, _ = _conv(zg, cg[...], wg_ref[...], bg_ref[...])
        val, _, _ = _conv(zv, cv[...], wv_ref[...], bv_ref[...])
        cg[...] = zg[tl - SUBLANES:, :]
        cv[...] = zv[tl - SUBLANES:, :]
        o_ref[...] = (_gelu(gate) * val).astype(o_ref.dtype)

    zspec = lambda off: pl.BlockSpec((tl, tc), lambda j, i: (i, j + off))
    wspec = lambda off, r: pl.BlockSpec((r, tc), lambda j, i: (0, j + off))
    return pl.pallas_call(
        body, grid=(nb, nt),
        in_specs=[zspec(0), zspec(nb), wspec(0, 3), wspec(nb, 3), wspec(0, 1), wspec(nb, 1)],
        out_specs=pl.BlockSpec((tl, tc), lambda j, i: (i, j)), out_shape=SDS((length, D_FF), BF16),
        scratch_shapes=[pltpu.VMEM((SUBLANES, tc), F32)] * 2,
        compiler_params=_params(("arbitrary", "arbitrary")), name="conv_fwd")(z, z, conv_w, conv_w, conv_b, conv_b)


def _conv_bwd(z, dact, conv_w, conv_b):
    length = z.shape[0]
    tl = min(256, length)
    nt = length // tl
    tc = _pick(D_FF, 1536)
    nb = D_FF // tc
    rows_per = tl // SUBLANES

    def half_bwd(dzc, z, z1, z2, w, dcar):
        n = tl
        row = lax.broadcasted_iota(jnp.int32, dzc.shape, 0)
        u1 = jnp.where(row == n - 1, jnp.broadcast_to(dcar[0:1, :], dzc.shape), pltpu.roll(dzc, n - 1, 0))
        u2 = pltpu.roll(dzc, n - 2, 0)
        u2 = jnp.where(row == n - 2, jnp.broadcast_to(dcar[0:1, :], dzc.shape), u2)
        u2 = jnp.where(row == n - 1, jnp.broadcast_to(dcar[1:2, :], dzc.shape), u2)
        dz = w[2:3, :] * dzc + w[1:2, :] * u1 + w[0:1, :] * u2
        dw = jnp.concatenate([jnp.sum(dzc * z2, axis=0, keepdims=True), jnp.sum(dzc * z1, axis=0, keepdims=True),
                              jnp.sum(dzc * z, axis=0, keepdims=True)], axis=0)
        return dz, dw, jnp.sum(dzc, axis=0, keepdims=True)

    def body(zg_ref, zv_ref, pg_ref, pv_ref, da_ref, wg_ref, wv_ref, bg_ref, bv_ref,
             dzg_ref, dzv_ref, dwg_ref, dwv_ref, dbg_ref, dbv_ref, cg, cv):
        step = pl.program_id(1)

        @pl.when(step == 0)
        def _():
            cg[...] = jnp.zeros_like(cg)
            cv[...] = jnp.zeros_like(cv)
            for ref in (dwg_ref, dwv_ref, dbg_ref, dbv_ref):
                ref[...] = jnp.zeros_like(ref)

        is_first_tile = step == nt - 1
        zg, zv = zg_ref[...], zv_ref[...]
        pg = jnp.where(is_first_tile, 0.0, pg_ref[...])
        pv = jnp.where(is_first_tile, 0.0, pv_ref[...])
        wg, wv = wg_ref[...], wv_ref[...]
        gate, zg1, zg2 = _conv(zg, pg, wg, bg_ref[...])
        val, zv1, zv2 = _conv(zv, pv, wv, bv_ref[...])
        act_g, vjp = jax.vjp(_gelu, gate)
        da = da_ref[...]
        dgate = vjp(da * val)[0]
        dval = da * act_g
        dzg, dwg, dbg = half_bwd(dgate, zg, zg1, zg2, wg, cg[...])
        dzv, dwv, dbv = half_bwd(dval, zv, zv1, zv2, wv, cv[...])
        cg[...] = dgate[:SUBLANES, :]
        cv[...] = dval[:SUBLANES, :]
        dzg_ref[...] = dzg
        dzv_ref[...] = dzv
        dwg_ref[...] += dwg
        dwv_ref[...] += dwv
        dbg_ref[...] += dbg
        dbv_ref[...] += dbv

    rmap = lambda i: nt - 1 - i
    zspec = lambda off: pl.BlockSpec((tl, tc), lambda j, i: (rmap(i), j + off))
    pspec = lambda off: pl.BlockSpec((SUBLANES, tc), lambda j, i: (jnp.maximum(rmap(i) * rows_per - 1, 0), j + off))
    wspec = lambda off, r: pl.BlockSpec((r, tc), lambda j, i: (0, j + off))
    out_w = lambda r: pl.BlockSpec((r, tc), lambda j, i: (0, j))
    dzg, dzv, dwg, dwv, dbg, dbv = pl.pallas_call(
        body, grid=(nb, nt),
        in_specs=[zspec(0), zspec(nb), pspec(0), pspec(nb), pl.BlockSpec((tl, tc), lambda j, i: (rmap(i), j)),
                  wspec(0, 3), wspec(nb, 3), wspec(0, 1), wspec(nb, 1)],
        out_specs=[pl.BlockSpec((tl, tc), lambda j, i: (rmap(i), j))] * 2 + [out_w(3), out_w(3), out_w(1), out_w(1)],
        out_shape=[SDS((length, D_FF), F32)] * 2 + [SDS((3, D_FF), F32)] * 2 + [SDS((1, D_FF), F32)] * 2,
        scratch_shapes=[pltpu.VMEM((SUBLANES, tc), F32)] * 2,
        compiler_params=_params(("arbitrary", "arbitrary")), name="conv_bwd")(z, z, z, z, dact, conv_w, conv_w, conv_b, conv_b)
    return dzg, dzv, jnp.concatenate([dwg, dwv], axis=1), jnp.concatenate([dbg, dbv], axis=1)


def _s5_prep_math(a_re, a_im, ls, b_re, b_im):
    dt = jnp.exp(ls)
    er = jnp.exp(a_re * dt)
    ph = a_im * dt
    abr, abi = er * jnp.cos(ph), er * jnp.sin(ph)
    den = a_re * a_re + a_im * a_im
    nr = abr - 1.0
    cr = (nr * a_re + abi * a_im) / den
    ci = (abi * a_re - nr * a_im) / den
    return abr, abi, cr * b_re - ci * b_im, cr * b_im + ci * b_re


def _s5_prep_fwd(a_re, a_im, ls, b_re, b_im):
    def body(ar, ai, l, br, bi, o1, o2, o3, o4):
        for ref, val in zip((o1, o2, o3, o4), _s5_prep_math(ar[...], ai[...], l[...], br[...], bi[...]), strict=True):
            ref[...] = val
    return pl.pallas_call(body, out_shape=[SDS((1, S5N), F32)] * 2 + [SDS((S5C, S5N), F32)] * 2,
                          name="s5_prep_fwd")(a_re, a_im, ls, b_re, b_im)


def _s5_prep_bwd(a_re, a_im, ls, b_re, b_im, cots, gsel):
    def body(ar, ai, l, br, bi, c1, c2, c3, c4, g_ref, o1, o2, o3, o4, o5):
        _, vjp = jax.vjp(_s5_prep_math, ar[...], ai[...], l[...], br[...], bi[...])
        d_ar, d_ai, d_ls, d_br, d_bi = vjp((c1[...], c2[...], c3[...], c4[...]))
        o1[...] = d_ar
        o2[...] = d_ai
        o3[...] = _hdot(jnp.broadcast_to(d_ls, (SUBLANES, S5N)), g_ref[...])
        o4[...] = d_br
        o5[...] = d_bi
    return pl.pallas_call(body, out_shape=[SDS((1, S5N), F32)] * 2 + [SDS((SUBLANES, LANES), F32)] + [SDS((S5C, S5N), F32)] * 2,
                          name="s5_prep_bwd")(a_re, a_im, ls, b_re, b_im, *cots, gsel)


def _cmul(ar, ai, br, bi):
    return ar * br - ai * bi, ar * bi + ai * br


def _s5_powers(abr, abi):
    shp = (SUBLANES, S5N)
    a1 = (jnp.broadcast_to(abr, shp), jnp.broadcast_to(abi, shp))
    a2 = _cmul(*a1, *a1)
    a4 = _cmul(*a2, *a2)
    row = lax.broadcasted_iota(jnp.int32, shp, 0)
    pr, pi = a1
    cur = a1
    for i in range(1, SUBLANES):
        cur = _cmul(*cur, *a1)
        pr = jnp.where(row == i, cur[0], pr)
        pi = jnp.where(row == i, cur[1], pi)
    return a1, a2, a4, (pr, pi)


def _s5_scan(sre_ref, sim_ref, car_re, car_im, abr, abi, n_rows, reverse):
    a1, a2, a4, (pr, pi) = _s5_powers(abr, abi)
    sgn = -1.0 if reverse else 1.0
    row = lax.broadcasted_iota(jnp.int32, (SUBLANES, S5N), 0)
    if reverse:
        qr, qi = pr, pi
        for i in range(SUBLANES):
            src = SUBLANES - 1 - i
            qr = jnp.where(row == i, jnp.broadcast_to(pr[src:src + 1, :], pr.shape), qr)
            qi = jnp.where(row == i, jnp.broadcast_to(pi[src:src + 1, :], pi.shape), qi)
        pr, pi = qr, qi
    nblk = n_rows // SUBLANES

    def blk(i, carry):
        cr, ci = carry
        b = (nblk - 1 - i) if reverse else i
        sl = pl.ds(pl.multiple_of(b * SUBLANES, SUBLANES), SUBLANES)
        xr, xi = sre_ref[sl, :], sim_ref[sl, :]
        for kk, (er, ei) in ((1, a1), (2, a2), (4, a4)):
            if reverse:
                sr = jnp.where(row < SUBLANES - kk, pltpu.roll(xr, SUBLANES - kk, 0), 0.0)
                si = jnp.where(row < SUBLANES - kk, pltpu.roll(xi, SUBLANES - kk, 0), 0.0)
            else:
                sr = jnp.where(row >= kk, pltpu.roll(xr, kk, 0), 0.0)
                si = jnp.where(row >= kk, pltpu.roll(xi, kk, 0), 0.0)
            dr, di = _cmul(er, sgn * ei, sr, si)
            xr, xi = xr + dr, xi + di
        dr, di = _cmul(pr, sgn * pi, cr, ci)
        xr, xi = xr + dr, xi + di
        sre_ref[sl, :] = xr
        sim_ref[sl, :] = xi
        edge = 0 if reverse else SUBLANES - 1
        return (jnp.broadcast_to(xr[edge:edge + 1, :], xr.shape), jnp.broadcast_to(xi[edge:edge + 1, :], xi.shape))

    cr, ci = lax.fori_loop(0, nblk, blk, (car_re[...], car_im[...]))
    car_re[...] = cr
    car_im[...] = ci


S5_BLK = 4
S5_BN = S5N // S5_BLK


def _split2(x):
    hi = x.astype(BF16)
    return hi, (x - hi.astype(F32)).astype(BF16)


def _dot3(a, b, ca, cb):
    ah, al = _split2(a)
    bh, bl = _split2(b)
    return lax.dot_general(jnp.concatenate([ah, ah, al], axis=ca), jnp.concatenate([bh, bl, bh], axis=cb),
                           (((ca,), (cb,)), ((), ())), preferred_element_type=F32)


def _s5_project_in(u, bblk_ref, sre_ref, sim_ref):
    for j in range(S5_BLK):
        bu = _dot3(u[:, j * LANES:(j + 1) * LANES], bblk_ref[j], 1, 0)
        sre_ref[:, j * S5_BN:(j + 1) * S5_BN] = bu[:, :S5_BN]
        sim_ref[:, j * S5_BN:(j + 1) * S5_BN] = bu[:, S5_BN:]


def _s5_project_out(u, d, cblk_ref, sre_ref, sim_ref):
    ys = []
    for j in range(S5_BLK):
        sl = slice(j * S5_BN, (j + 1) * S5_BN)
        ys.append(_dot3(sre_ref[:, sl], cblk_ref[j, :S5_BN, :], 1, 0) + _dot3(sim_ref[:, sl], cblk_ref[j, S5_BN:, :], 1, 0))
    return jnp.concatenate(ys, axis=1) + d * u


def _s5_fwd(u, bblk, cblk, abr, abi, d):
    length = u.shape[0]
    tl = min(256, length)
    nt = length // tl

    def body(u_ref, b_ref, c_ref, ar_ref, ai_ref, d_ref, ys_ref, kr_ref, ki_ref, sre, sim, car_re, car_im):
        @pl.when(pl.program_id(0) == 0)
        def _():
            car_re[...] = jnp.zeros_like(car_re)
            car_im[...] = jnp.zeros_like(car_im)

        kr_ref[0] = car_re[...]
        ki_ref[0] = car_im[...]
        u_ = u_ref[...]
        _s5_project_in(u_, b_ref, sre, sim)
        _s5_scan(sre, sim, car_re, car_im, ar_ref[...], ai_ref[...], tl, False)
        ys_ref[...] = _gelu(_s5_project_out(u_, d_ref[...], c_ref, sre, sim))

    full = lambda a: pl.BlockSpec(a.shape, lambda i, nd=a.ndim: (0,) * nd)
    chk = pl.BlockSpec((1, SUBLANES, S5N), lambda i: (i, 0, 0))
    return pl.pallas_call(
        body, grid=(nt,), in_specs=[pl.BlockSpec((tl, S5W), lambda i: (i, 0)), full(bblk), full(cblk), full(abr), full(abi), full(d)],
        out_specs=[pl.BlockSpec((tl, S5W), lambda i: (i, 0)), chk, chk],
        out_shape=[SDS((length, S5W), F32), SDS((nt, SUBLANES, S5N), F32), SDS((nt, SUBLANES, S5N), F32)],
        scratch_shapes=[pltpu.VMEM((tl, S5N), F32)] * 2 + [pltpu.VMEM((SUBLANES, S5N), F32)] * 2,
        compiler_params=_params(), name="s5_fwd")(u, bblk, cblk, abr, abi, d)


def _s5_bwd(u, dys_a, dys_b, kre, kim, bblk, cblk, abr, abi, d):
    length = u.shape[0]
    tl = min(256, length)
    nt = length // tl

    def body(u_ref, da_ref, db_ref, kr_ref, ki_ref, b_ref, c_ref, ar_ref, ai_ref, d_ref,
             du_ref, dB_ref, dC_ref, dar_ref, dai_ref, dd_ref, sre, sim, gre, gim, car_re, car_im, dcar_re, dcar_im):
        @pl.when(pl.program_id(0) == 0)
        def _():
            dcar_re[...] = jnp.zeros_like(dcar_re)
            dcar_im[...] = jnp.zeros_like(dcar_im)
            for ref in (dB_ref, dC_ref, dar_ref, dai_ref, dd_ref):
                ref[...] = jnp.zeros_like(ref)

        u_ = u_ref[...]
        abr_, abi_, d_ = ar_ref[...], ai_ref[...], d_ref[...]
        car_re[...] = kr_ref[0]
        car_im[...] = ki_ref[0]
        _s5_project_in(u_, b_ref, sre, sim)
        _s5_scan(sre, sim, car_re, car_im, abr_, abi_, tl, False)
        y = _s5_project_out(u_, d_, c_ref, sre, sim)
        _, vjp = jax.vjp(_gelu, y)
        dy = vjp(da_ref[...] + db_ref[...])[0]
        dd_ref[...] += jnp.sum(dy * u_, axis=0, keepdims=True)
        for j in range(S5_BLK):
            sl = slice(j * S5_BN, (j + 1) * S5_BN)
            dyj = dy[:, j * LANES:(j + 1) * LANES]
            gre[:, sl] = _dot3(dyj, c_ref[j, :S5_BN, :], 1, 1)
            gim[:, sl] = _dot3(dyj, c_ref[j, S5_BN:, :], 1, 1)
            dC_ref[j, :S5_BN, :] += _dot3(sre[:, sl], dyj, 0, 0)
            dC_ref[j, S5_BN:, :] += _dot3(sim[:, sl], dyj, 0, 0)
        _s5_scan(gre, gim, dcar_re, dcar_im, abr_, abi_, tl, True)
        gr, gi = gre[...], gim[...]
        pr = _shift_down(sre[...], kr_ref[0, 0:1, :])
        pi = _shift_down(sim[...], ki_ref[0, 0:1, :])
        dar_ref[...] += jnp.sum(gr * pr + gi * pi, axis=0, keepdims=True)
        dai_ref[...] += jnp.sum(gi * pr - gr * pi, axis=0, keepdims=True)
        dus = []
        for j in range(S5_BLK):
            sl = slice(j * S5_BN, (j + 1) * S5_BN)
            uj = u_[:, j * LANES:(j + 1) * LANES]
            dus.append(_dot3(gre[:, sl], b_ref[j, :, :S5_BN], 1, 1) + _dot3(gim[:, sl], b_ref[j, :, S5_BN:], 1, 1))
            dB_ref[j, :, :S5_BN] += _dot3(uj, gre[:, sl], 0, 0)
            dB_ref[j, :, S5_BN:] += _dot3(uj, gim[:, sl], 0, 0)
        du_ref[...] = jnp.concatenate(dus, axis=1) + d_ * dy

    rmap = lambda i: nt - 1 - i
    full = lambda a: pl.BlockSpec(a.shape, lambda i, nd=a.ndim: (0,) * nd)
    row = pl.BlockSpec((tl, S5W), lambda i: (rmap(i), 0))
    chk = pl.BlockSpec((1, SUBLANES, S5N), lambda i: (rmap(i), 0, 0))
    return pl.pallas_call(
        body, grid=(nt,), in_specs=[row, row, row, chk, chk, full(bblk), full(cblk), full(abr), full(abi), full(d)],
        out_specs=[row, full(bblk), full(cblk), full(abr), full(abi), full(d)],
        out_shape=[SDS((length, S5W), F32), SDS(bblk.shape, F32), SDS(cblk.shape, F32), SDS(abr.shape, F32),
                   SDS(abi.shape, F32), SDS(d.shape, F32)],
        scratch_shapes=[pltpu.VMEM((tl, S5N), F32)] * 4 + [pltpu.VMEM((SUBLANES, S5N), F32)] * 4,
        compiler_params=_params(), name="s5_bwd")(u, dys_a, dys_b, kre, kim, bblk, cblk, abr, abi, d)


def _s5_blockdiag_in(bb):
    t = bb.reshape(S5C, S5_BLK, S5_BN).transpose(1, 0, 2)
    t = jnp.tile(t, (1, SUBLANES, 1))
    mask = (lax.broadcasted_iota(jnp.int32, (LANES, S5_BN), 0) // S5C) == (lax.broadcasted_iota(jnp.int32, (LANES, S5_BN), 1) // S5P)
    return jnp.where(mask[None], t, 0.0)


def _s5_blockdiag_in_t(dblk):
    t = dblk.reshape(S5_BLK, SUBLANES, S5C, SUBLANES, S5P)
    t = jnp.diagonal(t, axis1=1, axis2=3)
    return t.transpose(1, 0, 3, 2).reshape(S5C, S5N)


def _s5_blockdiag_out(c):
    t = c.reshape(S5_BLK, SUBLANES, S5C, S5P).transpose(0, 3, 1, 2).reshape(S5_BLK, S5P, LANES)
    t = jnp.tile(t, (1, SUBLANES, 1))
    mask = (lax.broadcasted_iota(jnp.int32, (S5_BN, LANES), 0) // S5P) == (lax.broadcasted_iota(jnp.int32, (S5_BN, LANES), 1) // S5C)
    return jnp.where(mask[None], t, 0.0)


def _s5_blockdiag_out_t(dblk):
    t = dblk.reshape(S5_BLK, SUBLANES, S5P, SUBLANES, S5C)
    t = jnp.diagonal(t, axis1=1, axis2=3)
    return t.transpose(0, 3, 2, 1).reshape(S5G, S5C, S5P)


def _wkv_consts():
    lane = lax.broadcasted_iota(jnp.int32, (4 * HN, LANES), 1)
    row = lax.broadcasted_iota(jnp.int32, (4 * HN, LANES), 0)
    diag = ((lane % HN) == (row % HN)).astype(F32)
    ones = _block_ones(LANES, HN).astype(BF16)
    return lane % HN, diag, jnp.concatenate([ones, ones], axis=0)


def _wkv_split(p):
    hi = p.astype(BF16)
    lo = (p - hi.astype(F32)).astype(BF16)
    return jnp.concatenate([hi, lo], axis=1)


def _wkv_rowseg(p, ones):
    return jnp.dot(_wkv_split(p), ones, preferred_element_type=F32)


def _wkv_rowseg_all(lhs_ref, ones, out_ref):
    c, ns = lhs_ref.shape[0], lhs_ref.shape[1]
    out_ref[...] = jnp.dot(lhs_ref[...].reshape(c * ns, 2 * LANES), ones, preferred_element_type=F32).reshape(c, ns, LANES)


def _wkv_full(qb_ref, t):
    q = qb_ref[t]
    return jnp.concatenate([jnp.tile(q[:, hp * LANES:(hp + 1) * LANES], (SUBLANES, 1)) for hp in range(4)], axis=0)


def _wkv_colsum8(x):
    return jnp.concatenate([x[hp * HN:(hp + 1) * HN].reshape(SUBLANES, SUBLANES, LANES).sum(axis=0) for hp in range(4)], axis=1)


def _wkv_prebroadcast(src_refs, dst_refs, n):
    rep = (lax.broadcasted_iota(jnp.int32, (n * SUBLANES, n), 0) // SUBLANES
           == lax.broadcasted_iota(jnp.int32, (n * SUBLANES, n), 1)).astype(BF16)
    rep3 = jnp.concatenate([rep, rep, rep], axis=1)
    for s, dref in zip(src_refs, dst_refs, strict=True):
        x = s[...]
        hi = x.astype(BF16)
        rest = x - hi.astype(F32)
        mid = rest.astype(BF16)
        lo = (rest - mid.astype(F32)).astype(BF16)
        dref[...] = jnp.dot(rep3, jnp.concatenate([hi, mid, lo], axis=0), preferred_element_type=F32).reshape(n, SUBLANES, RW)


def _wkv_pair_sums(lhs_ref, out_ref):
    n, ns = out_ref.shape[0], out_ref.shape[1]
    ones_pair = _block_ones(2 * LANES, HN).astype(BF16)
    out_ref[...] = jnp.dot(lhs_ref[0:n].reshape(n * ns, 2 * LANES), ones_pair,
                           preferred_element_type=F32).reshape(n, ns, 2 * LANES)


def _wkv_transpose_out(acc, c, diag):
    lane = lax.broadcasted_iota(jnp.int32, (c, LANES), 1)
    outs = []
    for hp in range(4):
        zp = lax.dot_general(acc[hp * HN:(hp + 1) * HN], diag[:HN], (((0,), (0,)), ((), ())), precision=HI,
                             preferred_element_type=F32)
        outs.append(jnp.where(lane < HN, zp[:c], zp[HN:HN + c]))
    return jnp.concatenate(outs, axis=1)


def _wkv_fwd(r, w, k, v, a, b):
    length = r.shape[0]
    c = min(WKV_CHUNK, length)
    nc = length // c
    ns = 4 * HN

    def body(r_ref, w_ref, k_ref, v_ref, a_ref, b_ref, y_ref, sst_ref, sast_ref, vst_ref, fin_ref,
             s_ref, sa_ref, acc_ref, lhs_ref, ysum_ref, rb, wb, kb, vb, ab, bb):
        @pl.when(pl.program_id(0) == 0)
        def _():
            s_ref[...] = jnp.zeros_like(s_ref)

        _wkv_prebroadcast((r_ref, w_ref, k_ref, v_ref, a_ref, b_ref), (rb, wb, kb, vb, ab, bb), c)
        lane_t, diag, ones = _wkv_consts()

        def fill_v(t, carry):
            lhs_ref[t] = _wkv_split(_wkv_full(vb, t) * diag)
            return carry

        lax.fori_loop(0, c, fill_v, 0)
        _wkv_rowseg_all(lhs_ref, ones, vst_ref)
        sa_ref[...] = _wkv_rowseg(s_ref[...] * _wkv_full(ab, 0), ones)

        def pair(p, carry):
            for half in range(2):
                t = 2 * p + half
                s, sa = s_ref[...], sa_ref[...]
                sst_ref[t] = s
                sast_ref[t] = sa
                s = s * _wkv_full(wb, t) + sa * _wkv_full(bb, t) + vst_ref[t] * _wkv_full(kb, t)
                s_ref[...] = s
                sa_ref[...] = _wkv_rowseg(s * _wkv_full(ab, jnp.minimum(t + 1, c - 1)), ones)
                lhs_ref[p, :, half * LANES:(half + 1) * LANES] = (s * _wkv_full(rb, t)).astype(BF16)
            return carry

        lax.fori_loop(0, c // 2, pair, 0)
        _wkv_pair_sums(lhs_ref, ysum_ref)
        acc_ref[...] = jnp.zeros_like(acc_ref)

        def gather_y(p, carry):
            both = ysum_ref[p]
            acc = jnp.where(lane_t == 2 * p, both[:, :LANES], acc_ref[...])
            acc_ref[...] = jnp.where(lane_t == 2 * p + 1, both[:, LANES:], acc)
            return carry

        lax.fori_loop(0, c // 2, gather_y, 0)
        y_ref[...] = _wkv_transpose_out(acc_ref[...], c, diag)
        fin_ref[...] = s_ref[...]

    row = pl.BlockSpec((c, RW), lambda i: (i, 0))
    st = pl.BlockSpec((c, ns, LANES), lambda i: (i, 0, 0))
    state = pltpu.VMEM((ns, LANES), F32)
    return pl.pallas_call(
        body, grid=(nc,), in_specs=[row] * 6,
        out_specs=[row, st, st, st, pl.BlockSpec((ns, LANES), lambda i: (0, 0))],
        out_shape=[SDS((length, RW), F32)] + [SDS((length, ns, LANES), F32)] * 3 + [SDS((ns, LANES), F32)],
        scratch_shapes=[state] * 3 + [pltpu.VMEM((c, ns, 2 * LANES), BF16), pltpu.VMEM((c // 2, ns, 2 * LANES), F32)]
        + [pltpu.VMEM((c, SUBLANES, RW), F32)] * 6,
        compiler_params=_params(), name="wkv_fwd")(r, w, k, v, a, b)


def _wkv_bwd(r, w, k, a, b, dy, sst, sast, vst, fin):
    length = r.shape[0]
    c = min(WKV_CHUNK, length)
    nc = length // c
    ns = 4 * HN

    def body(r_ref, w_ref, k_ref, a_ref, b_ref, dy_ref, sst_ref, sast_ref, vst_ref, snext_ref, fin_ref,
             dr_ref, dw_ref, dk_ref, da_ref, db_ref, dv_ref,
             ds_ref, cur_ref, acc_ref, dyst, lhs_ref, dvsum_ref, rb, wb, kb, ab, bb, dyb, p_r, p_w, p_k, p_a, p_b):
        @pl.when(pl.program_id(0) == 0)
        def _():
            ds_ref[...] = jnp.zeros_like(ds_ref)

        _wkv_prebroadcast((r_ref, w_ref, k_ref, a_ref, b_ref, dy_ref), (rb, wb, kb, ab, bb, dyb), c)
        lane_t, diag, ones = _wkv_consts()

        def fill(t, carry):
            lhs_ref[t] = _wkv_split(_wkv_full(dyb, t) * diag)
            return carry

        lax.fori_loop(0, c, fill, 0)
        _wkv_rowseg_all(lhs_ref, ones, dyst)
        cur_ref[...] = jnp.where(pl.program_id(0) == 0, fin_ref[...], snext_ref[0])

        def pair(p, carry):
            for half in range(2):
                t = c - 1 - (2 * p + half)
                s_t, s_prev, sa, vv, dyy = cur_ref[...], sst_ref[t], sast_ref[t], vst_ref[t], dyst[t]
                ds = ds_ref[...] + dyy * _wkv_full(rb, t)
                dsa = _wkv_rowseg(ds * _wkv_full(bb, t), ones)
                ds_ref[...] = ds * _wkv_full(wb, t) + dsa * _wkv_full(ab, t)
                cur_ref[...] = s_prev
                lhs_ref[p, :, half * LANES:(half + 1) * LANES] = (ds * _wkv_full(kb, t)).astype(BF16)
                p_r[t] = _wkv_colsum8(s_t * dyy)
                p_k[t] = _wkv_colsum8(ds * vv)
                p_b[t] = _wkv_colsum8(ds * sa)
                p_w[t] = _wkv_colsum8(ds * s_prev)
                p_a[t] = _wkv_colsum8(s_prev * dsa)
            return carry

        lax.fori_loop(0, c // 2, pair, 0)
        _wkv_pair_sums(lhs_ref, dvsum_ref)
        acc_ref[...] = jnp.zeros_like(acc_ref)

        def gather_dv(p, carry):
            both = dvsum_ref[p]
            acc = jnp.where(lane_t == c - 1 - 2 * p, both[:, :LANES], acc_ref[...])
            acc_ref[...] = jnp.where(lane_t == c - 2 - 2 * p, both[:, LANES:], acc)
            return carry

        lax.fori_loop(0, c // 2, gather_dv, 0)
        sel = (lax.broadcasted_iota(jnp.int32, (c, c * SUBLANES), 1) // SUBLANES
               == lax.broadcasted_iota(jnp.int32, (c, c * SUBLANES), 0)).astype(F32)
        for out_ref, part in ((dr_ref, p_r), (dw_ref, p_w), (dk_ref, p_k), (da_ref, p_a), (db_ref, p_b)):
            out_ref[...] = _dot3(sel, part[...].reshape(c * SUBLANES, RW), 1, 0)
        dv_ref[...] = _wkv_transpose_out(acc_ref[...], c, diag)

    rmap = lambda i: nc - 1 - i
    row = pl.BlockSpec((c, RW), lambda i: (rmap(i), 0))
    st = pl.BlockSpec((c, ns, LANES), lambda i: (rmap(i), 0, 0))
    nxt = pl.BlockSpec((1, ns, LANES), lambda i: (jnp.minimum((rmap(i) + 1) * c, length - 1), 0, 0))
    state = pltpu.VMEM((ns, LANES), F32)
    return pl.pallas_call(
        body, grid=(nc,), in_specs=[row] * 6 + [st, st, st, nxt, pl.BlockSpec((ns, LANES), lambda i: (0, 0))],
        out_specs=[row] * 6, out_shape=[SDS((length, RW), F32)] * 6,
        scratch_shapes=[state] * 3 + [pltpu.VMEM((c, ns, LANES), F32), pltpu.VMEM((c, ns, 2 * LANES), BF16),
                                       pltpu.VMEM((c // 2, ns, 2 * LANES), F32)]
        + [pltpu.VMEM((c, SUBLANES, RW), F32)] * 11,
        compiler_params=_params(), name="wkv_bwd")(r, w, k, a, b, dy, sst, sast, vst, sst, fin)


def _my_place():
    return lax.axis_index("x"), lax.axis_index("y"), lax.axis_index("c")


def _peer(x, y, c, k):
    return (x ^ ((k >> 2) & 1), y ^ ((k >> 1) & 1), c ^ (k & 1))


def _all_gather(shard, name):
    rows = shard.shape[0]

    def body(in_ref, out_ref, send_sems, recv_sems, local_sem):
        x, y, c = _my_place()
        me = 4 * x + 2 * y + c
        mine = pltpu.make_async_copy(in_ref, out_ref.at[me], local_sem)
        mine.start()
        sends = []
        for k in range(1, N_DEV):
            cp = pltpu.make_async_remote_copy(src_ref=in_ref, dst_ref=out_ref.at[me], send_sem=send_sems.at[k - 1],
                                              recv_sem=recv_sems.at[k - 1], device_id=_peer(x, y, c, k), device_id_type=MESH)
            cp.start()
            sends.append(cp)
        for k in range(1, N_DEV):
            px, py, pc = _peer(x, y, c, k)
            pltpu.make_async_remote_copy(src_ref=in_ref, dst_ref=out_ref.at[4 * px + 2 * py + pc], send_sem=send_sems.at[k - 1],
                                         recv_sem=recv_sems.at[k - 1], device_id=(px, py, pc), device_id_type=MESH).wait_recv()
        for cp in sends:
            cp.wait_send()
        mine.wait()

    return pl.pallas_call(
        body, out_shape=SDS((N_DEV, rows, LANES), shard.dtype),
        in_specs=[pl.BlockSpec(memory_space=pl.ANY)], out_specs=pl.BlockSpec(memory_space=pl.ANY),
        scratch_shapes=[pltpu.SemaphoreType.DMA((N_DEV - 1,)), pltpu.SemaphoreType.DMA((N_DEV - 1,)), pltpu.SemaphoreType.DMA],
        name=name)(shard)


def _all_to_all(chunks, name):
    rows = chunks.shape[1]

    def body(in_ref, out_ref, send_sems, recv_sems, local_sem):
        x, y, c = _my_place()
        me = 4 * x + 2 * y + c
        mine = pltpu.make_async_copy(in_ref.at[me], out_ref.at[me], local_sem)
        mine.start()
        sends = []
        for k in range(1, N_DEV):
            px, py, pc = _peer(x, y, c, k)
            cp = pltpu.make_async_remote_copy(src_ref=in_ref.at[4 * px + 2 * py + pc], dst_ref=out_ref.at[me],
                                              send_sem=send_sems.at[k - 1], recv_sem=recv_sems.at[k - 1],
                                              device_id=(px, py, pc), device_id_type=MESH)
            cp.start()
            sends.append(cp)
        for k in range(1, N_DEV):
            px, py, pc = _peer(x, y, c, k)
            pltpu.make_async_remote_copy(src_ref=in_ref.at[me], dst_ref=out_ref.at[4 * px + 2 * py + pc], send_sem=send_sems.at[k - 1],
                                         recv_sem=recv_sems.at[k - 1], device_id=(px, py, pc), device_id_type=MESH).wait_recv()
        for cp in sends:
            cp.wait_send()
        mine.wait()

    return pl.pallas_call(
        body, out_shape=SDS((N_DEV, rows, LANES), chunks.dtype),
        in_specs=[pl.BlockSpec(memory_space=pl.ANY)], out_specs=pl.BlockSpec(memory_space=pl.ANY),
        scratch_shapes=[pltpu.SemaphoreType.DMA((N_DEV - 1,)), pltpu.SemaphoreType.DMA((N_DEV - 1,)), pltpu.SemaphoreType.DMA],
        name=name)(chunks)


def _sum_parts(parts, name):
    rows = parts.shape[1]

    def body(p_ref, g_out):
        g = p_ref[0].astype(F32)
        for s in range(1, N_DEV):
            g = g + p_ref[s].astype(F32)
        g_out[...] = g

    return pl.pallas_call(
        body, grid=(rows // PACK_ROWS,), in_specs=[pl.BlockSpec((N_DEV, PACK_ROWS, LANES), lambda i: (0, i, 0))],
        out_specs=pl.BlockSpec((PACK_ROWS, LANES), lambda i: (i, 0)), out_shape=SDS((rows, LANES), F32),
        compiler_params=_params(("parallel",)), name=name)(parts)


def _adamw(g, wgt, m, v):
    rows = wgt.shape[0]

    def body(g_ref, w_ref, m_ref, v_ref, d_out, m_out, v_out):
        g_ = g_ref[...]
        m_new = ADAM_B1 * m_ref[...] + (1.0 - ADAM_B1) * g_
        v_new = ADAM_B2 * v_ref[...] + (1.0 - ADAM_B2) * (g_ * g_)
        m_hat = m_new / (1.0 - ADAM_B1 ** ADAM_STEP)
        v_hat = v_new / (1.0 - ADAM_B2 ** ADAM_STEP)
        d_out[...] = -ADAM_LR * (m_hat / (jnp.sqrt(v_hat) + ADAM_EPS) + ADAM_WD * w_ref[...])
        m_out[...] = m_new
        v_out[...] = v_new

    blk = pl.BlockSpec((PACK_ROWS, LANES), lambda i: (i, 0))
    return pl.pallas_call(
        body, grid=(rows // PACK_ROWS,), in_specs=[blk] * 4, out_specs=[blk] * 3, out_shape=[SDS((rows, LANES), F32)] * 3,
        compiler_params=_params(("parallel",)), name="adamw")(g, wgt, m, v)


PACK_ALIGN = 2 * SUBLANES * LANES
PACK_ROWS = 512

ROW_SHARDED = ("s5_w_glu", "w_out", "ffn_w_down")
TRANSPOSED = ("w_in", "ffn_w_up", "w_branch_rwkv", "w_branch_s5")
SMALL_SHARDED = ("rwkv_w2", "rwkv_a2", "rwkv_g2", "ffn_conv_w")
SHARDED = ROW_SHARDED + TRANSPOSED + SMALL_SHARDED
F32_GATHER = ("ffn_conv_w",)
REPLICATED =("norm_mix_pre", "norm_mix_post", "norm_ffn_pre", "norm_ffn_post", "b_gate", "rwkv_shift_mu", "rwkv_w0",
              "rwkv_a0", "rwkv_k_k", "rwkv_k_a", "rwkv_r_k", "rwkv_lnx_w", "rwkv_lnx_b", "s5_a_re", "s5_a_im", "s5_b_re",
              "s5_b_im", "s5_c_re", "s5_c_im", "s5_d", "s5_log_step", "s5_b_glu", "ffn_conv_b")
WEIGHTS = ("norm_mix_pre", "norm_mix_post", "norm_ffn_pre", "norm_ffn_post", "w_in", "b_gate", "rwkv_shift_mu", "rwkv_w0",
           "rwkv_w2", "rwkv_a0", "rwkv_a2", "rwkv_g2", "rwkv_k_k", "rwkv_k_a", "rwkv_r_k", "rwkv_lnx_w", "rwkv_lnx_b",
           "s5_a_re", "s5_a_im", "s5_b_re", "s5_b_im", "s5_c_re", "s5_c_im", "s5_d", "s5_log_step", "s5_w_glu", "s5_b_glu",
           "w_branch_rwkv", "w_branch_s5", "w_out", "ffn_w_up", "ffn_conv_w", "ffn_conv_b", "ffn_w_down")


def _pad_flat(a):
    flat = a.reshape(-1)
    pad = (-flat.shape[0]) % PACK_ALIGN
    return jnp.pad(flat, (0, pad)) if pad else flat


def _pad_cols(a, mult):
    pad = (-a.shape[1]) % mult
    return jnp.pad(a, ((0, 0), (0, pad))) if pad else a


def _pack(pieces):
    flat = jnp.concatenate([_pad_flat(p) for p in pieces])
    pad = (-flat.shape[0]) % (PACK_ROWS * LANES)
    return (jnp.pad(flat, (0, pad)) if pad else flat).reshape(-1, LANES)


def _unpack(buf, shapes):
    lead = buf.shape[:-2]
    flat = buf.reshape(*lead, -1)
    out, off = [], 0
    for s in shapes:
        n = math.prod(s)
        out.append(flat[..., off:off + n].reshape(*lead, *s))
        off += n + (-n) % PACK_ALIGN
    return out


def _rows_to_full(stack):
    return stack.reshape(-1, stack.shape[2])


def _cols_to_full(stack):
    return stack.transpose(1, 0, 2).reshape(stack.shape[1], -1)


def _local_step(x, tgt, wt):
    o512 = _block_ones(RW, HN)
    w_in = wt["w_in"]
    w_rw, w_u, w_g = w_in[:N_RWKV], w_in[N_RWKV:N_RWKV + S5W], w_in[N_RWKV + S5W:]
    b_gate = wt["b_gate"]
    b_r, b_s = b_gate[:, :D_MODEL], b_gate[:, D_MODEL:]
    zpad = lambda a, lo, n: jnp.pad(a.astype(F32), ((lo, n - lo - a.shape[0]), (0, 0)))
    w2p, a2p, g2p = zpad(wt["rwkv_w2"], 0, 256), zpad(wt["rwkv_a2"], 64, 256), zpad(wt["rwkv_g2"], 128, 256)
    pre_small = [wt["rwkv_w0"], wt["rwkv_a0"], wt["rwkv_k_k"], wt["rwkv_k_a"], w2p, a2p, g2p]
    post_prm = [wt["rwkv_lnx_w"], wt["rwkv_lnx_b"], wt["rwkv_r_k"].reshape(1, RW)]
    mu = wt["rwkv_shift_mu"]

    a_re, a_im = wt["s5_a_re"].reshape(1, S5N), wt["s5_a_im"].reshape(1, S5N)
    ls = jnp.repeat(wt["s5_log_step"].reshape(S5G), S5P).reshape(1, S5N)
    b_re_t = wt["s5_b_re"].reshape(S5N, S5C).T
    b_im_t = wt["s5_b_im"].reshape(S5N, S5C).T
    c_re, c_im = wt["s5_c_re"].reshape(S5G, S5C, S5P), wt["s5_c_im"].reshape(S5G, S5C, S5P)
    abr, abi, bbr, bbi = _s5_prep_fwd(a_re, a_im, ls, b_re_t, b_im_t)
    bblk = jnp.concatenate([_s5_blockdiag_in(bbr), _s5_blockdiag_in(bbi)], axis=2)
    cblk = jnp.concatenate([_s5_blockdiag_out(c_re), -_s5_blockdiag_out(c_im)], axis=1)
    s5_d = wt["s5_d"]

    h1 = _rms_fwd(x, wt["norm_mix_pre"], "rms1_fwd")
    p_rw = _mm(h1, w_rw, tb=True, name="mm_proj_rwkv")
    u = _mm(h1, w_u, tb=True, name="mm_proj_s5")
    gpre = _mm(h1, w_g, tb=True, name="mm_proj_gate")
    r, decay, k2, v, aa, bb, g = _rwkv_pre_fwd(p_rw, mu, pre_small, o512)
    y, sst, sast, vst, s_fin = _wkv_fwd(r, decay, k2, v, aa, bb)
    y_r = _rwkv_post_fwd(y, r, k2, v, g, post_prm, o512)
    o_r = _mm(y_r, wt["w_branch_rwkv"], tb=True, name="mm_branch_rwkv")
    ys, kre, kim = _s5_fwd(u, bblk, cblk, abr, abi, s5_d)
    t_glu = _mm(ys, wt["s5_w_glu"], name="mm_glu")
    out5 = _glu_fwd(ys, t_glu, wt["s5_b_glu"])
    o_s = _mm(out5, wt["w_branch_s5"], tb=True, name="mm_branch_s5")
    mi = _merge_fwd(gpre, o_r, o_s, b_r, b_s)
    mixed = _mm(mi, wt["w_out"], name="mm_out")
    x2, h2 = _mid_fwd(x, mixed, wt["norm_mix_post"], wt["norm_ffn_pre"])
    z = _mm(h2, wt["ffn_w_up"], tb=True, name="mm_up")
    act = _conv_fwd(z, wt["ffn_conv_w"], wt["ffn_conv_b"])
    f = _mm(act, wt["ffn_w_down"], name="mm_down")
    dx3, df, d_norm_ffn_post, loss_part = _final(x2, f, tgt, wt["norm_ffn_post"])

    gr = {"norm_ffn_post": d_norm_ffn_post}
    dact = _mm(df, wt["ffn_w_down"], tb=True, name="mm_down_dx")
    gr["ffn_w_down"] = _mm(act, df, ta=True, name="mm_down_dw")
    dzg, dzv, gr["ffn_conv_w"], gr["ffn_conv_b"] = _conv_bwd(z, dact, wt["ffn_conv_w"], wt["ffn_conv_b"])
    dz = jnp.concatenate([dzg, dzv], axis=1)
    dh2 = _mm(dz, wt["ffn_w_up"], name="mm_up_dx")
    gr["ffn_w_up"] = _mm(dz, h2, ta=True, name="mm_up_dw")
    dx2, dmixed, gr["norm_mix_post"], gr["norm_ffn_pre"] = _mid_bwd(x2, mixed, dh2, dx3, wt["norm_mix_post"], wt["norm_ffn_pre"])
    dmi = _mm(dmixed, wt["w_out"], tb=True, name="mm_out_dx")
    gr["w_out"] = _mm(mi, dmixed, ta=True, name="mm_out_dw")
    dgp_r, dgp_s, do_r, do_s, db_r, db_s = _merge_bwd(gpre, o_r, o_s, dmi, b_r, b_s)
    gr["b_gate"] = jnp.concatenate([db_r, db_s], axis=1)
    dout5 = _mm(do_s, wt["w_branch_s5"], name="mm_branch_s5_dx")
    gr["w_branch_s5"] = _mm(do_s, out5, ta=True, name="mm_branch_s5_dw")
    dys_a, dt_glu, gr["s5_b_glu"] = _glu_bwd(ys, t_glu, dout5, wt["s5_b_glu"])
    dys_b = _mm(dt_glu, wt["s5_w_glu"], tb=True, name="mm_glu_dx")
    gr["s5_w_glu"] = _mm(ys, dt_glu, ta=True, name="mm_glu_dw")
    du, dbblk, dcblk, dabr, dabi, gr["s5_d"] = _s5_bwd(u, dys_a, dys_b, kre, kim, bblk, cblk, abr, abi, s5_d)
    gr["s5_c_re"] = _s5_blockdiag_out_t(dcblk[:, :S5_BN, :]).reshape(wt["s5_c_re"].shape)
    gr["s5_c_im"] = (-_s5_blockdiag_out_t(dcblk[:, S5_BN:, :])).reshape(wt["s5_c_im"].shape)
    dbbr, dbbi = _s5_blockdiag_in_t(dbblk[:, :, :S5_BN]), _s5_blockdiag_in_t(dbblk[:, :, S5_BN:])
    gsel = (lax.broadcasted_iota(jnp.int32, (S5N, LANES), 0) // S5P == lax.broadcasted_iota(jnp.int32, (S5N, LANES), 1)).astype(F32)
    d_are, d_aim, d_ls, d_bre_t, d_bim_t = _s5_prep_bwd(a_re, a_im, ls, b_re_t, b_im_t, (dabr, dabi, dbbr, dbbi), gsel)
    gr["s5_a_re"] = d_are.reshape(wt["s5_a_re"].shape)
    gr["s5_a_im"] = d_aim.reshape(wt["s5_a_im"].shape)
    gr["s5_log_step"] = d_ls[0:1, :S5G]
    gr["s5_b_re"] = d_bre_t.T.reshape(wt["s5_b_re"].shape)
    gr["s5_b_im"] = d_bim_t.T.reshape(wt["s5_b_im"].shape)
    dy_r = _mm(do_r, wt["w_branch_rwkv"], name="mm_branch_rwkv_dx")
    gr["w_branch_rwkv"] = _mm(do_r, y_r, ta=True, name="mm_branch_rwkv_dw")
    dy, dr1, dk1, dv1, dg, gr["rwkv_lnx_w"], gr["rwkv_lnx_b"], d_rk = _rwkv_post_bwd(y, r, k2, v, g, dy_r, post_prm, o512)
    gr["rwkv_r_k"] = d_rk.reshape(wt["rwkv_r_k"].shape)
    dr2, ddecay, dk2, daa, dbb, dv2 = _wkv_bwd(r, decay, k2, aa, bb, dy, sst, sast, vst, s_fin)
    cots = [dr1 + dr2, ddecay, dk1 + dk2, dv1 + dv2, daa, dbb, dg]
    dp_rw, gr["rwkv_shift_mu"], gr["rwkv_w0"], gr["rwkv_a0"], gr["rwkv_k_k"], gr["rwkv_k_a"], dw2p, da2p, dg2p = \
        _rwkv_pre_bwd(p_rw, cots, mu, pre_small, o512)
    gr["rwkv_w2"], gr["rwkv_a2"], gr["rwkv_g2"] = dw2p[:64], da2p[64:128], dg2p[128:]
    dproj = jnp.concatenate([dp_rw, du, dgp_r, dgp_s], axis=1)
    dh1 = _mm(dproj, w_in, name="mm_proj_dx")
    gr["w_in"] = _mm(dproj, h1, ta=True, name="mm_proj_dw")
    dx, gr["norm_mix_pre"] = _rms_bwd(x, wt["norm_mix_pre"], dh1, dx2, "rms1_bwd")
    return loss_part[0, 0], dx, gr


def kernel(x, norm_mix_pre, norm_mix_post, norm_ffn_pre, norm_ffn_post, w_in, b_gate, rwkv_shift_mu, rwkv_w0, rwkv_w2, rwkv_a0, rwkv_a2, rwkv_g2, rwkv_k_k, rwkv_k_a, rwkv_r_k, rwkv_lnx_w, rwkv_lnx_b, s5_a_re, s5_a_im, s5_b_re, s5_b_im, s5_c_re, s5_c_im, s5_d, s5_log_step, s5_w_glu, s5_b_glu, w_branch_rwkv, w_branch_s5, w_out, ffn_w_up, ffn_conv_w, ffn_conv_b, ffn_w_down, loss_target, m_norm_mix_pre, m_norm_mix_post, m_norm_ffn_pre, m_norm_ffn_post, m_w_in, m_b_gate, m_rwkv_shift_mu, m_rwkv_w0, m_rwkv_w2, m_rwkv_a0, m_rwkv_a2, m_rwkv_g2, m_rwkv_k_k, m_rwkv_k_a, m_rwkv_r_k, m_rwkv_lnx_w, m_rwkv_lnx_b, m_s5_a_re, m_s5_a_im, m_s5_b_re, m_s5_b_im, m_s5_c_re, m_s5_c_im, m_s5_d, m_s5_log_step, m_s5_w_glu, m_s5_b_glu, m_w_branch_rwkv, m_w_branch_s5, m_w_out, m_ffn_w_up, m_ffn_conv_w, m_ffn_conv_b, m_ffn_w_down, v_norm_mix_pre, v_norm_mix_post, v_norm_ffn_pre, v_norm_ffn_post, v_w_in, v_b_gate, v_rwkv_shift_mu, v_rwkv_w0, v_rwkv_w2, v_rwkv_a0, v_rwkv_a2, v_rwkv_g2, v_rwkv_k_k, v_rwkv_k_a, v_rwkv_r_k, v_rwkv_lnx_w, v_rwkv_lnx_b, v_s5_a_re, v_s5_a_im, v_s5_b_re, v_s5_b_im, v_s5_c_re, v_s5_c_im, v_s5_d, v_s5_log_step, v_s5_w_glu, v_s5_b_glu, v_w_branch_rwkv, v_w_branch_s5, v_w_out, v_ffn_w_up, v_ffn_conv_w, v_ffn_conv_b, v_ffn_w_down):
    args = dict(locals())
    wgt = {n: args[n] for n in WEIGHTS}
    mom = {n: args["m_" + n] for n in WEIGHTS}
    var = {n: args["v_" + n] for n in WEIGHTS}
    me = 4 * lax.axis_index("x") + 2 * lax.axis_index("y") + lax.axis_index("c")
    shard_shapes = {n: wgt[n].shape[1:] for n in SHARDED}
    sent_shapes = {n: (shard_shapes[n][::-1] if n in TRANSPOSED else shard_shapes[n]) for n in SHARDED}
    sent = lambda n, a: a.T if n in TRANSPOSED else a

    bf_names = [n for n in SHARDED if n not in F32_GATHER]
    got_bf = _unpack(_all_gather(_pack([sent(n, wgt[n][0].astype(BF16)) for n in bf_names]), "all_gather_weights"),
                     [sent_shapes[n] for n in bf_names])
    got_f32 = _unpack(_all_gather(_pack([wgt[n][0] for n in F32_GATHER]), "all_gather_taps"),
                      [sent_shapes[n] for n in F32_GATHER])
    wt = {n: wgt[n] for n in REPLICATED}
    for n, blocks in zip(bf_names + list(F32_GATHER), got_bf + got_f32, strict=True):
        wt[n] = _cols_to_full(blocks) if n in SMALL_SHARDED else _rows_to_full(blocks)

    loss_part, dx, gr = _local_step(x[0], loss_target[0], wt)

    big = ROW_SHARDED + TRANSPOSED
    whole = SMALL_SHARDED + REPLICATED
    pieces = [_pad_cols(gr[n].reshape(N_DEV, -1).astype(BF16), PACK_ALIGN) for n in big]
    chunks = _pad_cols(jnp.concatenate(pieces, axis=1), PACK_ROWS * LANES).reshape(N_DEV, -1, LANES)
    big_sum = _sum_parts(_all_to_all(chunks, "exchange_grads"), "sum_grads")
    small_sum = _sum_parts(_all_gather(_pack([gr[n] for n in whole]), "gather_small_grads"), "sum_small_grads")
    got = dict(zip(big + whole, _unpack(big_sum, [sent_shapes[n] for n in big]) + _unpack(small_sum, [gr[n].shape for n in whole]),
                   strict=True))
    grads = {}
    for n in WEIGHTS:
        if n in TRANSPOSED:
            grads[n] = got[n].T
        elif n in SMALL_SHARDED:
            cols = shard_shapes[n][1]
            grads[n] = lax.dynamic_slice_in_dim(got[n], me * cols, cols, axis=1)
        else:
            grads[n] = got[n]
        grads[n] = grads[n].reshape(wgt[n].shape)

    pack_local = lambda src: _pack([src[n].reshape(-1) for n in WEIGHTS])
    outs = _adamw(pack_local(grads), pack_local(wgt), pack_local(mom), pack_local(var))
    shapes = [wgt[n].shape for n in WEIGHTS]
    loss = lax.psum(loss_part, AXES)
    return (loss, dx[None], *[grads[n] for n in WEIGHTS], *_unpack(outs[0], shapes), *_unpack(outs[1], shapes),
            *_unpack(outs[2], shapes))
```

```python
import functools
import math

import jax
import jax.numpy as jnp
from jax import lax
from jax.experimental import pallas as pl
from jax.experimental.pallas import tpu as pltpu

F32 = jnp.float32
BF16 = jnp.bfloat16
SDS = jax.ShapeDtypeStruct
HI = lax.Precision.HIGHEST
MESH = pl.DeviceIdType.MESH
AXES = ("x", "y", "c")
N_DEV = 8

D_MODEL = 1024
RW = 512
HN = 64
N_RWKV = 1792
S5W = 512
S5G = 32
S5C = 16
S5P = 64
S5N = S5G * S5P
D_FF = 2816
NORM_EPS = 1e-6
LNX_EPS = 64e-5

ADAM_LR = 0.001
ADAM_B1 = 0.9
ADAM_B2 = 0.999
ADAM_EPS = 1e-08
ADAM_WD = 0.01
ADAM_STEP = 10

LANES = 128
SUBLANES = 8
VMEM_LIMIT = 56 * 1024 * 1024
WKV_CHUNK = 32


def _params(sem=("arbitrary",)):
    return pltpu.CompilerParams(dimension_semantics=sem, vmem_limit_bytes=VMEM_LIMIT)


def _pick(n, cap):
    best = None
    for t in range(LANES, min(n, cap) + 1, LANES):
        if n % t == 0:
            best = t
    return best or n


def _mm(a, b, *, ta=False, tb=False, out_dtype=F32, name):
    m = a.shape[1] if ta else a.shape[0]
    k = a.shape[0] if ta else a.shape[1]
    n = b.shape[0] if tb else b.shape[1]
    assert (b.shape[1] if tb else b.shape[0]) == k
    tm, tn, tk = _pick(m, 1024), _pick(n, 2304), _pick(k, 512)
    nk = k // tk
    dims = (((0 if ta else 1,), (1 if tb else 0,)), ((), ()))

    def body(a_ref, b_ref, o_ref, acc_ref):
        kk = pl.program_id(2)

        @pl.when(kk == 0)
        def _():
            acc_ref[...] = jnp.zeros_like(acc_ref)

        acc_ref[...] += lax.dot_general(a_ref[...].astype(BF16), b_ref[...].astype(BF16), dims,
                                        preferred_element_type=F32)

        @pl.when(kk == nk - 1)
        def _():
            o_ref[...] = acc_ref[...].astype(o_ref.dtype)

    a_spec = pl.BlockSpec((tk, tm), lambda i, j, kk: (kk, i)) if ta else pl.BlockSpec((tm, tk), lambda i, j, kk: (i, kk))
    b_spec = pl.BlockSpec((tn, tk), lambda i, j, kk: (j, kk)) if tb else pl.BlockSpec((tk, tn), lambda i, j, kk: (kk, j))
    return pl.pallas_call(
        body, grid=(m // tm, n // tn, nk), in_specs=[a_spec, b_spec],
        out_specs=pl.BlockSpec((tm, tn), lambda i, j, kk: (i, j)),
        out_shape=SDS((m, n), out_dtype), scratch_shapes=[pltpu.VMEM((tm, tn), F32)],
        compiler_params=_params(("parallel", "parallel", "arbitrary")), name=name)(a, b)


def _rows(fn, rows, params, out_rows, out_accs, *, name, tl, reverse=False, scratch=()):
    first = rows[0][0] if isinstance(rows[0], tuple) else rows[0]
    length = first.shape[0]
    tl = min(tl, length)
    nt = length // tl
    rmap = (lambda i: nt - 1 - i) if reverse else (lambda i: i)
    specs, arrs = [], []
    for r in rows:
        arr, wdt, cb = r if isinstance(r, tuple) else (r, r.shape[1], 0)
        specs.append(pl.BlockSpec((tl, wdt), lambda i, cb=cb: (rmap(i), cb)))
        arrs.append(arr)
    for p in params:
        specs.append(pl.BlockSpec(p.shape, lambda i, nd=p.ndim: (0,) * nd))
        arrs.append(p)
    out_shape = [SDS((length, c), dt) for c, dt in out_rows] + [SDS(s, F32) for s in out_accs]
    out_specs = [pl.BlockSpec((tl, c), lambda i: (rmap(i), 0)) for c, _ in out_rows]
    out_specs += [pl.BlockSpec(s, lambda i, nd=len(s): (0,) * nd) for s in out_accs]
    nr, npar, nor, noa = len(rows), len(params), len(out_rows), len(out_accs)

    def body(*refs):
        rin, pin = refs[:nr], refs[nr:nr + npar]
        rout = refs[nr + npar:nr + npar + nor]
        aout = refs[nr + npar + nor:nr + npar + nor + noa]
        scr = refs[nr + npar + nor + noa:]
        step = pl.program_id(0)
        outs_r, outs_a = fn(step, [r[...] for r in rin], [p[...] for p in pin], scr)
        for ref, val in zip(rout, outs_r, strict=True):
            ref[...] = val.astype(ref.dtype)

        @pl.when(step == 0)
        def _():
            for ref in aout:
                ref[...] = jnp.zeros_like(ref)

        for ref, val in zip(aout, outs_a, strict=True):
            ref[...] += val.astype(F32)

    res = pl.pallas_call(body, grid=(nt,), in_specs=specs, out_specs=out_specs, out_shape=out_shape,
                         scratch_shapes=list(scratch), compiler_params=_params(), name=name)(*arrs)
    return list(res)


def _rms(x, g):
    return x * lax.rsqrt(jnp.mean(x * x, axis=-1, keepdims=True) + NORM_EPS) * g


def _sig(x):
    return 0.5 * (jnp.tanh(0.5 * x) + 1.0)


def _softplus(x):
    return jnp.maximum(x, 0.0) + jnp.log(1.0 + jnp.exp(-jnp.abs(x)))


def _gelu(x):
    return x * (0.5 * (1.0 + jnp.tanh(math.sqrt(2.0 / math.pi) * (x + 0.044715 * (x * x * x)))))


def _bdot(a, b):
    return jnp.dot(a.astype(BF16), b.astype(BF16), preferred_element_type=F32)


def _hdot(a, b):
    return jnp.dot(a, b, precision=HI, preferred_element_type=F32)


def _segsum_impl(x, ones):
    hi = x.astype(BF16)
    lo = (x - hi.astype(F32)).astype(BF16)
    ones = ones.astype(BF16)
    return jnp.dot(jnp.concatenate([hi, lo], axis=1), jnp.concatenate([ones, ones], axis=0), preferred_element_type=F32)


@jax.custom_vjp
def _segsum(x, ones):
    return _segsum_impl(x, ones)


_segsum.defvjp(lambda x, ones: (_segsum_impl(x, ones), ones),
               lambda ones, g: (_segsum_impl(g, ones), jnp.zeros_like(ones)))


def _block_ones(n, blk):
    i = lax.broadcasted_iota(jnp.int32, (n, n), 0) // blk
    j = lax.broadcasted_iota(jnp.int32, (n, n), 1) // blk
    return (i == j).astype(F32)


def _rms_fwd(x, g, name):
    return _rows(lambda s, r, p, _: ([_rms(r[0], p[0])], []), [x], [g], [(x.shape[1], BF16)], [], name=name, tl=512)[0]


def _rms_bwd(x, g, dh, dres, name):
    def fn(s, r, p, _):
        _, vjp = jax.vjp(_rms, r[0], p[0])
        dx, dg = vjp(r[1])
        return [dx + r[2]], [dg]
    return _rows(fn, [x, dh, dres], [g], [(x.shape[1], F32)], [g.shape], name=name, tl=256)


def _rwkv_pre_math(k_, lr, w0, a0, k_k, k_a, w2p, a2p, g2p, o512):
    pre_w = w0 + _bdot(jnp.tanh(lr), w2p)
    w = -_softplus(-pre_w) - 0.5
    decay = jnp.exp(-jnp.exp(w))
    a = _sig(a0 + _bdot(lr, a2p))
    g = _bdot(_sig(lr), g2p)
    kr = k_ * k_k
    kk = kr / jnp.maximum(jnp.sqrt(_segsum(kr * kr, o512)), 1e-12)
    k2 = k_ * (1.0 + (a - 1.0) * k_a)
    return decay, k2, -kk, kk * a, g


def _shift_down(p, prev_row):
    row = lax.broadcasted_iota(jnp.int32, p.shape, 0)
    return jnp.where(row == 0, jnp.broadcast_to(prev_row, p.shape), pltpu.roll(p, 1, 0))


def _shift_up(q, next_row):
    n = q.shape[0]
    row = lax.broadcasted_iota(jnp.int32, q.shape, 0)
    return jnp.where(row == n - 1, jnp.broadcast_to(next_row, q.shape), pltpu.roll(q, n - 1, 0))


def _rwkv_pre_fwd(p, mu, small, o512):
    def fn(step, r, prm, scr):
        car = scr[0]

        @pl.when(step == 0)
        def _():
            car[...] = jnp.zeros_like(car)

        x = r[0]
        prev = _shift_down(x, car[SUBLANES - 1:SUBLANES, :])
        car[...] = x[x.shape[0] - SUBLANES:, :]
        xs = x + (prev - x) * prm[0]
        decay, k2, aa, bb, g = _rwkv_pre_math(xs[:, RW:2 * RW], xs[:, 3 * RW:], *prm[1:])
        return [xs[:, :RW], decay, k2, xs[:, 2 * RW:3 * RW], aa, bb, g], []
    return _rows(fn, [p], [mu, *small, o512], [(RW, F32)] * 7, [], name="rwkv_pre_fwd", tl=256,
                 scratch=[pltpu.VMEM((SUBLANES, N_RWKV), F32)])


def _rwkv_pre_bwd(p, cots, mu, small, o512):
    length = p.shape[0]
    tl = min(256, length)
    nt = length // tl
    rows_per = tl // SUBLANES
    params = [mu, *small, o512]
    acc_shapes = [mu.shape] + [q.shape for q in small]
    nr, npar, nacc = 2 + len(cots), len(params), len(acc_shapes)

    def body(*refs):
        rin, pin = refs[:nr], refs[nr:nr + npar]
        dp_ref = refs[nr + npar]
        aout = refs[nr + npar + 1:nr + npar + 1 + nacc]
        car_q = refs[nr + npar + 1 + nacc]
        step = pl.program_id(0)

        @pl.when(step == 0)
        def _():
            car_q[...] = jnp.zeros_like(car_q)
            for ref in aout:
                ref[...] = jnp.zeros_like(ref)

        x = rin[0][...]
        prev_row = jnp.where(step == nt - 1, 0.0, rin[1][SUBLANES - 1:SUBLANES, :])
        dr, ddecay, dk2, dv, daa, dbb, dg = [r[...] for r in rin[2:]]
        prm = [q[...] for q in pin]
        mu_, o512_ = prm[0], prm[-1]
        prev = _shift_down(x, prev_row)
        xs = x + (prev - x) * mu_
        _, vjp = jax.vjp(lambda k_, lr, *w: _rwkv_pre_math(k_, lr, *w, o512_), xs[:, RW:2 * RW], xs[:, 3 * RW:], *prm[1:-1])
        dk_, dlr, *dsmall = vjp((ddecay, dk2, daa, dbb, dg))
        dxs = jnp.concatenate([dr, dk_, dv, dlr], axis=1)
        q = dxs * mu_
        dp_ref[...] = (dxs - q + _shift_up(q, car_q[0:1, :])).astype(dp_ref.dtype)
        car_q[...] = q[:SUBLANES, :]
        aout[0][...] += jnp.sum((prev - x) * dxs, axis=0, keepdims=True)
        for ref, val in zip(aout[1:], dsmall, strict=True):
            ref[...] += val

    rmap = lambda i: nt - 1 - i
    specs = [pl.BlockSpec((tl, N_RWKV), lambda i: (rmap(i), 0)),
             pl.BlockSpec((SUBLANES, N_RWKV), lambda i: (jnp.maximum(rmap(i) * rows_per - 1, 0), 0))]
    specs += [pl.BlockSpec((tl, RW), lambda i: (rmap(i), 0)) for _ in cots]
    specs += [pl.BlockSpec(q.shape, lambda i, nd=q.ndim: (0,) * nd) for q in params]
    out_shape = [SDS((length, N_RWKV), BF16)] + [SDS(sh, F32) for sh in acc_shapes]
    out_specs = [pl.BlockSpec((tl, N_RWKV), lambda i: (rmap(i), 0))]
    out_specs += [pl.BlockSpec(sh, lambda i, nd=len(sh): (0,) * nd) for sh in acc_shapes]
    res = pl.pallas_call(body, grid=(nt,), in_specs=specs, out_specs=out_specs, out_shape=out_shape,
                         scratch_shapes=[pltpu.VMEM((SUBLANES, N_RWKV), F32)],
                         compiler_params=_params(), name="rwkv_pre_bwd")(p, p, *cots, *params)
    return list(res)


def _rwkv_post_math(y, r, k2, v, g, lnx_w, lnx_b, r_k, o512):
    mean = _segsum(y, o512) * (1.0 / HN)
    yc = y - mean
    var = _segsum(yc * yc, o512) * (1.0 / HN)
    yn = yc * lax.rsqrt(var + LNX_EPS) * lnx_w + lnx_b
    bonus = _segsum(r * k2 * r_k, o512) * v
    return (yn + bonus) * g


def _rwkv_post_fwd(y, r, k2, v, g, prm, o512):
    return _rows(lambda s, rr, p, _: ([_rwkv_post_math(*rr, *p)], []), [y, r, k2, v, g], [*prm, o512],
                 [(RW, BF16)], [], name="rwkv_post_fwd", tl=256)[0]


def _rwkv_post_bwd(y, r, k2, v, g, dout, prm, o512):
    def fn(s, rr, p, _):
        o = p[-1]
        _, vjp = jax.vjp(lambda *a: _rwkv_post_math(*a, o), *rr[:5], *p[:-1])
        gr = vjp(rr[5])
        return list(gr[:5]), list(gr[5:])
    return _rows(fn, [y, r, k2, v, g, dout], [*prm, o512], [(RW, F32)] * 5, [q.shape for q in prm],
                 name="rwkv_post_bwd", tl=256)


def _glu_math(ys, t, b):
    return ys * _sig(t + b)


def _glu_fwd(ys, t, b):
    return _rows(lambda s, r, p, _: ([_glu_math(r[0], r[1], p[0])], []), [ys, t], [b], [(S5W, BF16)], [],
                 name="s5_glu_fwd", tl=512)[0]


def _glu_bwd(ys, t, dout, b):
    def fn(s, r, p, _):
        _, vjp = jax.vjp(_glu_math, r[0], r[1], p[0])
        dys, dt, db = vjp(r[2])
        return [dys, dt], [db]
    return _rows(fn, [ys, t, dout], [b], [(S5W, F32), (S5W, BF16)], [b.shape], name="s5_glu_bwd", tl=512)


def _merge_math(gp_r, gp_s, o_r, o_s, b_r, b_s):
    return _sig(gp_r + b_r) * o_r + _sig(gp_s + b_s) * o_s


def _merge_fwd(gpre, o_r, o_s, b_r, b_s):
    return _rows(lambda s, r, p, _: ([_merge_math(*r, *p)], []),
                 [(gpre, D_MODEL, 0), (gpre, D_MODEL, 1), o_r, o_s], [b_r, b_s], [(D_MODEL, BF16)], [],
                 name="merge_fwd", tl=256)[0]


def _merge_bwd(gpre, o_r, o_s, dmi, b_r, b_s):
    def fn(s, r, p, _):
        _, vjp = jax.vjp(_merge_math, *r[:4], *p)
        dgr, dgs, dor, dos, dbr, dbs = vjp(r[4])
        return [dgr, dgs, dor, dos], [dbr, dbs]
    return _rows(fn, [(gpre, D_MODEL, 0), (gpre, D_MODEL, 1), o_r, o_s, dmi], [b_r, b_s], [(D_MODEL, BF16)] * 4,
                 [b_r.shape, b_s.shape], name="merge_bwd", tl=256)


def _mid_fwd(x, mixed, g_post, g_pre):
    def fn(s, r, p, _):
        x2 = r[0] + _rms(r[1], p[0])
        return [x2, _rms(x2, p[1])], []
    return _rows(fn, [x, mixed], [g_post, g_pre], [(D_MODEL, F32), (D_MODEL, BF16)], [], name="mid_fwd", tl=256)


def _mid_bwd(x2, mixed, dh2, dx3, g_post, g_pre):
    def fn(s, r, p, _):
        _, vjp1 = jax.vjp(_rms, r[0], p[1])
        dx2, dg_pre = vjp1(r[2])
        dx2 = dx2 + r[3]
        _, vjp2 = jax.vjp(_rms, r[1], p[0])
        dmixed, dg_post = vjp2(dx2)
        return [dx2, dmixed], [dg_post, dg_pre]
    return _rows(fn, [x2, mixed, dh2, dx3], [g_post, g_pre], [(D_MODEL, F32), (D_MODEL, BF16)], [g_post.shape, g_pre.shape],
                 name="mid_bwd", tl=256)


def _final(x2, f, tgt, g_post):
    def fn(s, r, p, _):
        y, vjp = jax.vjp(_rms, r[1], p[0])
        diff = r[0] + y - r[2]
        dx3 = diff * (1.0 / D_MODEL)
        df, dg = vjp(dx3)
        part = 0.5 * jnp.sum(jnp.sum(diff * diff, axis=1, keepdims=True), axis=0, keepdims=True) * (1.0 / D_MODEL)
        return [dx3, df], [dg, jnp.broadcast_to(part, (1, LANES))]
    return _rows(fn, [x2, f, tgt], [g_post], [(D_MODEL, F32), (D_MODEL, BF16)], [g_post.shape, (1, LANES)], name="final", tl=256)


def _conv_taps(z, car):
    row = lax.broadcasted_iota(jnp.int32, z.shape, 0)
    z1 = jnp.where(row == 0, jnp.broadcast_to(car[7:8, :], z.shape), pltpu.roll(z, 1, 0))
    z2 = pltpu.roll(z, 2, 0)
    z2 = jnp.where(row == 0, jnp.broadcast_to(car[6:7, :], z.shape), z2)
    z2 = jnp.where(row == 1, jnp.broadcast_to(car[7:8, :], z.shape), z2)
    return z1, z2


def _conv(z, car, w, b):
    z1, z2 = _conv_taps(z, car)
    return b + w[0:1, :] * z2 + w[1:2, :] * z1 + w[2:3, :] * z, z1, z2


def _conv_fwd(z, conv_w, conv_b):
    length = z.shape[0]
    tl = min(256, length)
    nt = length // tl
    tc = _pick(D_FF, 1536)
    nb = D_FF // tc

    def body(zg_ref, zv_ref, wg_ref, wv_ref, bg_ref, bv_ref, o_ref, cg, cv):
        @pl.when(pl.program_id(1) == 0)
        def _():
            cg[...] = jnp.zeros_like(cg)
            cv[...] = jnp.zeros_like(cv)

        zg, zv = zg_ref[...], zv_ref[...]
        gate, _, _ = _conv(zg, cg[...], wg_ref[...], bg_ref[...])
        val, _, _ = _conv(zv, cv[...], wv_ref[...], bv_ref[...])
        cg[...] = zg[tl - SUBLANES:, :]
        cv[...] = zv[tl - SUBLANES:, :]
        o_ref[...] = (_gelu(gate) * val).astype(o_ref.dtype)

    zspec = lambda off: pl.BlockSpec((tl, tc), lambda j, i: (i, j + off))
    wspec = lambda off, r: pl.BlockSpec((r, tc), lambda j, i: (0, j + off))
    return pl.pallas_call(
        body, grid=(nb, nt),
        in_specs=[zspec(0), zspec(nb), wspec(0, 3), wspec(nb, 3), wspec(0, 1), wspec(nb, 1)],
        out_specs=pl.BlockSpec((tl, tc), lambda j, i: (i, j)), out_shape=SDS((length, D_FF), BF16),
        scratch_shapes=[pltpu.VMEM((SUBLANES, tc), F32)] * 2,
        compiler_params=_params(("arbitrary", "arbitrary")), name="conv_fwd")(z, z, conv_w, conv_w, conv_b, conv_b)


def _conv_bwd(z, dact, conv_w, conv_b):
    length = z.shape[0]
    tl = min(256, length)
    nt = length // tl
    tc = _pick(D_FF, 1536)
    nb = D_FF // tc
    rows_per = tl // SUBLANES

    def half_bwd(dzc, z, z1, z2, w, dcar):
        n = tl
        row = lax.broadcasted_iota(jnp.int32, dzc.shape, 0)
        u1 = jnp.where(row == n - 1, jnp.broadcast_to(dcar[0:1, :], dzc.shape), pltpu.roll(dzc, n - 1, 0))
        u2 = pltpu.roll(dzc, n - 2, 0)
        u2 = jnp.where(row == n - 2, jnp.broadcast_to(dcar[0:1, :], dzc.shape), u2)
        u2 = jnp.where(row == n - 1, jnp.broadcast_to(dcar[1:2, :], dzc.shape), u2)
        dz = w[2:3, :] * dzc + w[1:2, :] * u1 + w[0:1, :] * u2
        dw = jnp.concatenate([jnp.sum(dzc * z2, axis=0, keepdims=True), jnp.sum(dzc * z1, axis=0, keepdims=True),
                              jnp.sum(dzc * z, axis=0, keepdims=True)], axis=0)
        return dz, dw, jnp.sum(dzc, axis=0, keepdims=True)

    def body(zg_ref, zv_ref, pg_ref, pv_ref, da_ref, wg_ref, wv_ref, bg_ref, bv_ref,
             dzg_ref, dzv_ref, dwg_ref, dwv_ref, dbg_ref, dbv_ref, cg, cv):
        step = pl.program_id(1)

        @pl.when(step == 0)
        def _():
            cg[...] = jnp.zeros_like(cg)
            cv[...] = jnp.zeros_like(cv)
            for ref in (dwg_ref, dwv_ref, dbg_ref, dbv_ref):
                ref[...] = jnp.zeros_like(ref)

        is_first_tile = step == nt - 1
        zg, zv = zg_ref[...], zv_ref[...]
        pg = jnp.where(is_first_tile, 0.0, pg_ref[...])
        pv = jnp.where(is_first_tile, 0.0, pv_ref[...])
        wg, wv = wg_ref[...], wv_ref[...]
        gate, zg1, zg2 = _conv(zg, pg, wg, bg_ref[...])
        val, zv1, zv2 = _conv(zv, pv, wv, bv_ref[...])
        act_g, vjp = jax.vjp(_gelu, gate)
        da = da_ref[...]
        dgate = vjp(da * val)[0]
        dval = da * act_g
        dzg, dwg, dbg = half_bwd(dgate, zg, zg1, zg2, wg, cg[...])
        dzv, dwv, dbv = half_bwd(dval, zv, zv1, zv2, wv, cv[...])
        cg[...] = dgate[:SUBLANES, :]
        cv[...] = dval[:SUBLANES, :]
        dzg_ref[...] = dzg.astype(dzg_ref.dtype)
        dzv_ref[...] = dzv.astype(dzv_ref.dtype)
        dwg_ref[...] += dwg
        dwv_ref[...] += dwv
        dbg_ref[...] += dbg
        dbv_ref[...] += dbv

    rmap = lambda i: nt - 1 - i
    zspec = lambda off: pl.BlockSpec((tl, tc), lambda j, i: (rmap(i), j + off))
    pspec = lambda off: pl.BlockSpec((SUBLANES, tc), lambda j, i: (jnp.maximum(rmap(i) * rows_per - 1, 0), j + off))
    wspec = lambda off, r: pl.BlockSpec((r, tc), lambda j, i: (0, j + off))
    out_w = lambda r: pl.BlockSpec((r, tc), lambda j, i: (0, j))
    dzg, dzv, dwg, dwv, dbg, dbv = pl.pallas_call(
        body, grid=(nb, nt),
        in_specs=[zspec(0), zspec(nb), pspec(0), pspec(nb), pl.BlockSpec((tl, tc), lambda j, i: (rmap(i), j)),
                  wspec(0, 3), wspec(nb, 3), wspec(0, 1), wspec(nb, 1)],
        out_specs=[pl.BlockSpec((tl, tc), lambda j, i: (rmap(i), j))] * 2 + [out_w(3), out_w(3), out_w(1), out_w(1)],
        out_shape=[SDS((length, D_FF), BF16)] * 2 + [SDS((3, D_FF), F32)] * 2 + [SDS((1, D_FF), F32)] * 2,
        scratch_shapes=[pltpu.VMEM((SUBLANES, tc), F32)] * 2,
        compiler_params=_params(("arbitrary", "arbitrary")), name="conv_bwd")(z, z, z, z, dact, conv_w, conv_w, conv_b, conv_b)
    return dzg, dzv, jnp.concatenate([dwg, dwv], axis=1), jnp.concatenate([dbg, dbv], axis=1)


def _s5_prep_math(a_re, a_im, ls, b_re, b_im):
    dt = jnp.exp(ls)
    er = jnp.exp(a_re * dt)
    ph = a_im * dt
    abr, abi = er * jnp.cos(ph), er * jnp.sin(ph)
    den = a_re * a_re + a_im * a_im
    nr = abr - 1.0
    cr = (nr * a_re + abi * a_im) / den
    ci = (abi * a_re - nr * a_im) / den
    return abr, abi, cr * b_re - ci * b_im, cr * b_im + ci * b_re


def _s5_prep_fwd(a_re, a_im, ls, b_re, b_im):
    def body(ar, ai, l, br, bi, o1, o2, o3, o4):
        for ref, val in zip((o1, o2, o3, o4), _s5_prep_math(ar[...], ai[...], l[...], br[...], bi[...]), strict=True):
            ref[...] = val
    return pl.pallas_call(body, out_shape=[SDS((1, S5N), F32)] * 2 + [SDS((S5C, S5N), F32)] * 2,
                          name="s5_prep_fwd")(a_re, a_im, ls, b_re, b_im)


def _s5_prep_bwd(a_re, a_im, ls, b_re, b_im, cots, gsel):
    def body(ar, ai, l, br, bi, c1, c2, c3, c4, g_ref, o1, o2, o3, o4, o5):
        _, vjp = jax.vjp(_s5_prep_math, ar[...], ai[...], l[...], br[...], bi[...])
        d_ar, d_ai, d_ls, d_br, d_bi = vjp((c1[...], c2[...], c3[...], c4[...]))
        o1[...] = d_ar
        o2[...] = d_ai
        o3[...] = _hdot(jnp.broadcast_to(d_ls, (SUBLANES, S5N)), g_ref[...])
        o4[...] = d_br
        o5[...] = d_bi
    return pl.pallas_call(body, out_shape=[SDS((1, S5N), F32)] * 2 + [SDS((SUBLANES, LANES), F32)] + [SDS((S5C, S5N), F32)] * 2,
                          name="s5_prep_bwd")(a_re, a_im, ls, b_re, b_im, *cots, gsel)


def _cmul(ar, ai, br, bi):
    return ar * br - ai * bi, ar * bi + ai * br


def _s5_powers(abr, abi):
    shp = (SUBLANES, S5N)
    a1 = (jnp.broadcast_to(abr, shp), jnp.broadcast_to(abi, shp))
    a2 = _cmul(*a1, *a1)
    a4 = _cmul(*a2, *a2)
    row = lax.broadcasted_iota(jnp.int32, shp, 0)
    pr, pi = a1
    cur = a1
    for i in range(1, SUBLANES):
        cur = _cmul(*cur, *a1)
        pr = jnp.where(row == i, cur[0], pr)
        pi = jnp.where(row == i, cur[1], pi)
    return a1, a2, a4, (pr, pi)


def _s5_scan(sre_ref, sim_ref, car_re, car_im, abr, abi, n_rows, reverse):
    a1, a2, a4, (pr, pi) = _s5_powers(abr, abi)
    sgn = -1.0 if reverse else 1.0
    row = lax.broadcasted_iota(jnp.int32, (SUBLANES, S5N), 0)
    if reverse:
        qr, qi = pr, pi
        for i in range(SUBLANES):
            src = SUBLANES - 1 - i
            qr = jnp.where(row == i, jnp.broadcast_to(pr[src:src + 1, :], pr.shape), qr)
            qi = jnp.where(row == i, jnp.broadcast_to(pi[src:src + 1, :], pi.shape), qi)
        pr, pi = qr, qi
    nblk = n_rows // SUBLANES

    def blk(i, carry):
        cr, ci = carry
        b = (nblk - 1 - i) if reverse else i
        sl = pl.ds(pl.multiple_of(b * SUBLANES, SUBLANES), SUBLANES)
        xr, xi = sre_ref[sl, :], sim_ref[sl, :]
        for kk, (er, ei) in ((1, a1), (2, a2), (4, a4)):
            if reverse:
                sr = jnp.where(row < SUBLANES - kk, pltpu.roll(xr, SUBLANES - kk, 0), 0.0)
                si = jnp.where(row < SUBLANES - kk, pltpu.roll(xi, SUBLANES - kk, 0), 0.0)
            else:
                sr = jnp.where(row >= kk, pltpu.roll(xr, kk, 0), 0.0)
                si = jnp.where(row >= kk, pltpu.roll(xi, kk, 0), 0.0)
            dr, di = _cmul(er, sgn * ei, sr, si)
            xr, xi = xr + dr, xi + di
        dr, di = _cmul(pr, sgn * pi, cr, ci)
        xr, xi = xr + dr, xi + di
        sre_ref[sl, :] = xr
        sim_ref[sl, :] = xi
        edge = 0 if reverse else SUBLANES - 1
        return (jnp.broadcast_to(xr[edge:edge + 1, :], xr.shape), jnp.broadcast_to(xi[edge:edge + 1, :], xi.shape))

    cr, ci = lax.fori_loop(0, nblk, blk, (car_re[...], car_im[...]))
    car_re[...] = cr
    car_im[...] = ci


S5_BLK = 4
S5_BN = S5N // S5_BLK


def _split2(x):
    hi = x.astype(BF16)
    return hi, (x - hi.astype(F32)).astype(BF16)


def _dot3(a, b, ca, cb):
    ah, al = _split2(a)
    bh, bl = _split2(b)
    return lax.dot_general(jnp.concatenate([ah, ah, al], axis=ca), jnp.concatenate([bh, bl, bh], axis=cb),
                           (((ca,), (cb,)), ((), ())), preferred_element_type=F32)


def _s5_project_in(u, bblk_ref, sre_ref, sim_ref):
    for j in range(S5_BLK):
        bu = _dot3(u[:, j * LANES:(j + 1) * LANES], bblk_ref[j], 1, 0)
        sre_ref[:, j * S5_BN:(j + 1) * S5_BN] = bu[:, :S5_BN]
        sim_ref[:, j * S5_BN:(j + 1) * S5_BN] = bu[:, S5_BN:]


def _s5_project_out(u, d, cblk_ref, sre_ref, sim_ref):
    ys = []
    for j in range(S5_BLK):
        sl = slice(j * S5_BN, (j + 1) * S5_BN)
        ys.append(_dot3(sre_ref[:, sl], cblk_ref[j, :S5_BN, :], 1, 0) + _dot3(sim_ref[:, sl], cblk_ref[j, S5_BN:, :], 1, 0))
    return jnp.concatenate(ys, axis=1) + d * u


def _s5_fwd(u, bblk, cblk, abr, abi, d):
    length = u.shape[0]
    tl = min(256, length)
    nt = length // tl

    def body(u_ref, b_ref, c_ref, ar_ref, ai_ref, d_ref, ys_ref, kr_ref, ki_ref, sre, sim, car_re, car_im):
        @pl.when(pl.program_id(0) == 0)
        def _():
            car_re[...] = jnp.zeros_like(car_re)
            car_im[...] = jnp.zeros_like(car_im)

        kr_ref[0] = car_re[...]
        ki_ref[0] = car_im[...]
        u_ = u_ref[...]
        _s5_project_in(u_, b_ref, sre, sim)
        _s5_scan(sre, sim, car_re, car_im, ar_ref[...], ai_ref[...], tl, False)
        ys_ref[...] = _gelu(_s5_project_out(u_, d_ref[...], c_ref, sre, sim))

    full = lambda a: pl.BlockSpec(a.shape, lambda i, nd=a.ndim: (0,) * nd)
    chk = pl.BlockSpec((1, SUBLANES, S5N), lambda i: (i, 0, 0))
    return pl.pallas_call(
        body, grid=(nt,), in_specs=[pl.BlockSpec((tl, S5W), lambda i: (i, 0)), full(bblk), full(cblk), full(abr), full(abi), full(d)],
        out_specs=[pl.BlockSpec((tl, S5W), lambda i: (i, 0)), chk, chk],
        out_shape=[SDS((length, S5W), F32), SDS((nt, SUBLANES, S5N), F32), SDS((nt, SUBLANES, S5N), F32)],
        scratch_shapes=[pltpu.VMEM((tl, S5N), F32)] * 2 + [pltpu.VMEM((SUBLANES, S5N), F32)] * 2,
        compiler_params=_params(), name="s5_fwd")(u, bblk, cblk, abr, abi, d)


def _s5_bwd(u, dys_a, dys_b, kre, kim, bblk, cblk, abr, abi, d):
    length = u.shape[0]
    tl = min(256, length)
    nt = length // tl

    def body(u_ref, da_ref, db_ref, kr_ref, ki_ref, b_ref, c_ref, ar_ref, ai_ref, d_ref,
             du_ref, dB_ref, dC_ref, dar_ref, dai_ref, dd_ref, sre, sim, gre, gim, car_re, car_im, dcar_re, dcar_im):
        @pl.when(pl.program_id(0) == 0)
        def _():
            dcar_re[...] = jnp.zeros_like(dcar_re)
            dcar_im[...] = jnp.zeros_like(dcar_im)
            for ref in (dB_ref, dC_ref, dar_ref, dai_ref, dd_ref):
                ref[...] = jnp.zeros_like(ref)

        u_ = u_ref[...]
        abr_, abi_, d_ = ar_ref[...], ai_ref[...], d_ref[...]
        car_re[...] = kr_ref[0]
        car_im[...] = ki_ref[0]
        _s5_project_in(u_, b_ref, sre, sim)
        _s5_scan(sre, sim, car_re, car_im, abr_, abi_, tl, False)
        y = _s5_project_out(u_, d_, c_ref, sre, sim)
        _, vjp = jax.vjp(_gelu, y)
        dy = vjp(da_ref[...] + db_ref[...])[0]
        dd_ref[...] += jnp.sum(dy * u_, axis=0, keepdims=True)
        for j in range(S5_BLK):
            sl = slice(j * S5_BN, (j + 1) * S5_BN)
            dyj = dy[:, j * LANES:(j + 1) * LANES]
            gre[:, sl] = _dot3(dyj, c_ref[j, :S5_BN, :], 1, 1)
            gim[:, sl] = _dot3(dyj, c_ref[j, S5_BN:, :], 1, 1)
            dC_ref[j, :S5_BN, :] += _dot3(sre[:, sl], dyj, 0, 0)
            dC_ref[j, S5_BN:, :] += _dot3(sim[:, sl], dyj, 0, 0)
        _s5_scan(gre, gim, dcar_re, dcar_im, abr_, abi_, tl, True)
        gr, gi = gre[...], gim[...]
        pr = _shift_down(sre[...], kr_ref[0, 0:1, :])
        pi = _shift_down(sim[...], ki_ref[0, 0:1, :])
        dar_ref[...] += jnp.sum(gr * pr + gi * pi, axis=0, keepdims=True)
        dai_ref[...] += jnp.sum(gi * pr - gr * pi, axis=0, keepdims=True)
        dus = []
        for j in range(S5_BLK):
            sl = slice(j * S5_BN, (j + 1) * S5_BN)
            uj = u_[:, j * LANES:(j + 1) * LANES]
            dus.append(_dot3(gre[:, sl], b_ref[j, :, :S5_BN], 1, 1) + _dot3(gim[:, sl], b_ref[j, :, S5_BN:], 1, 1))
            dB_ref[j, :, :S5_BN] += _dot3(uj, gre[:, sl], 0, 0)
            dB_ref[j, :, S5_BN:] += _dot3(uj, gim[:, sl], 0, 0)
        du_ref[...] = (jnp.concatenate(dus, axis=1) + d_ * dy).astype(du_ref.dtype)

    rmap = lambda i: nt - 1 - i
    full = lambda a: pl.BlockSpec(a.shape, lambda i, nd=a.ndim: (0,) * nd)
    row = pl.BlockSpec((tl, S5W), lambda i: (rmap(i), 0))
    chk = pl.BlockSpec((1, SUBLANES, S5N), lambda i: (rmap(i), 0, 0))
    return pl.pallas_call(
        body, grid=(nt,), in_specs=[row, row, row, chk, chk, full(bblk), full(cblk), full(abr), full(abi), full(d)],
        out_specs=[row, full(bblk), full(cblk), full(abr), full(abi), full(d)],
        out_shape=[SDS((length, S5W), BF16), SDS(bblk.shape, F32), SDS(cblk.shape, F32), SDS(abr.shape, F32),
                   SDS(abi.shape, F32), SDS(d.shape, F32)],
        scratch_shapes=[pltpu.VMEM((tl, S5N), F32)] * 4 + [pltpu.VMEM((SUBLANES, S5N), F32)] * 4,
        compiler_params=_params(), name="s5_bwd")(u, dys_a, dys_b, kre, kim, bblk, cblk, abr, abi, d)


def _s5_blockdiag_in(bb):
    t = bb.reshape(S5C, S5_BLK, S5_BN).transpose(1, 0, 2)
    t = jnp.tile(t, (1, SUBLANES, 1))
    mask = (lax.broadcasted_iota(jnp.int32, (LANES, S5_BN), 0) // S5C) == (lax.broadcasted_iota(jnp.int32, (LANES, S5_BN), 1) // S5P)
    return jnp.where(mask[None], t, 0.0)


def _s5_blockdiag_in_t(dblk):
    t = dblk.reshape(S5_BLK, SUBLANES, S5C, SUBLANES, S5P)
    t = jnp.diagonal(t, axis1=1, axis2=3)
    return t.transpose(1, 0, 3, 2).reshape(S5C, S5N)


def _s5_blockdiag_out(c):
    t = c.reshape(S5_BLK, SUBLANES, S5C, S5P).transpose(0, 3, 1, 2).reshape(S5_BLK, S5P, LANES)
    t = jnp.tile(t, (1, SUBLANES, 1))
    mask = (lax.broadcasted_iota(jnp.int32, (S5_BN, LANES), 0) // S5P) == (lax.broadcasted_iota(jnp.int32, (S5_BN, LANES), 1) // S5C)
    return jnp.where(mask[None], t, 0.0)


def _s5_blockdiag_out_t(dblk):
    t = dblk.reshape(S5_BLK, SUBLANES, S5P, SUBLANES, S5C)
    t = jnp.diagonal(t, axis1=1, axis2=3)
    return t.transpose(0, 3, 2, 1).reshape(S5G, S5C, S5P)


def _wkv_consts():
    lane = lax.broadcasted_iota(jnp.int32, (4 * HN, LANES), 1)
    row = lax.broadcasted_iota(jnp.int32, (4 * HN, LANES), 0)
    diag = ((lane % HN) == (row % HN)).astype(F32)
    ones = _block_ones(LANES, HN).astype(BF16)
    return lane % HN, diag, jnp.concatenate([ones, ones], axis=0)


def _wkv_split(p):
    hi = p.astype(BF16)
    lo = (p - hi.astype(F32)).astype(BF16)
    return jnp.concatenate([hi, lo], axis=1)


def _wkv_rowseg(p, ones):
    return jnp.dot(_wkv_split(p), ones, preferred_element_type=F32)


def _wkv_rowseg_all(lhs_ref, ones, out_ref):
    c, ns = lhs_ref.shape[0], lhs_ref.shape[1]
    out_ref[...] = jnp.dot(lhs_ref[...].reshape(c * ns, 2 * LANES), ones, preferred_element_type=F32).reshape(c, ns, LANES)


def _wkv_full(qb_ref, t):
    q = qb_ref[t]
    return jnp.concatenate([jnp.tile(q[:, hp * LANES:(hp + 1) * LANES], (SUBLANES, 1)) for hp in range(4)], axis=0)


def _wkv_colsum8(x):
    return jnp.concatenate([x[hp * HN:(hp + 1) * HN].reshape(SUBLANES, SUBLANES, LANES).sum(axis=0) for hp in range(4)], axis=1)


def _wkv_prebroadcast(src_refs, dst_refs, n):
    rep = (lax.broadcasted_iota(jnp.int32, (n * SUBLANES, n), 0) // SUBLANES
           == lax.broadcasted_iota(jnp.int32, (n * SUBLANES, n), 1)).astype(BF16)
    rep3 = jnp.concatenate([rep, rep, rep], axis=1)
    for s, dref in zip(src_refs, dst_refs, strict=True):
        x = s[...]
        hi = x.astype(BF16)
        rest = x - hi.astype(F32)
        mid = rest.astype(BF16)
        lo = (rest - mid.astype(F32)).astype(BF16)
        dref[...] = jnp.dot(rep3, jnp.concatenate([hi, mid, lo], axis=0), preferred_element_type=F32).reshape(n, SUBLANES, RW)


def _wkv_pair_sums(lhs_ref, out_ref):
    n, ns = out_ref.shape[0], out_ref.shape[1]
    ones_pair = _block_ones(2 * LANES, HN).astype(BF16)
    out_ref[...] = jnp.dot(lhs_ref[0:n].reshape(n * ns, 2 * LANES), ones_pair,
                           preferred_element_type=F32).reshape(n, ns, 2 * LANES)


def _wkv_transpose_out(acc, c, diag):
    lane = lax.broadcasted_iota(jnp.int32, (c, LANES), 1)
    outs = []
    for hp in range(4):
        zp = lax.dot_general(acc[hp * HN:(hp + 1) * HN], diag[:HN], (((0,), (0,)), ((), ())), precision=HI,
                             preferred_element_type=F32)
        outs.append(jnp.where(lane < HN, zp[:c], zp[HN:HN + c]))
    return jnp.concatenate(outs, axis=1)


def _wkv_fwd(r, w, k, v, a, b):
    length = r.shape[0]
    c = min(WKV_CHUNK, length)
    nc = length // c
    ns = 4 * HN

    def body(r_ref, w_ref, k_ref, v_ref, a_ref, b_ref, y_ref, sst_ref, sast_ref, vst_ref, fin_ref,
             s_ref, sa_ref, acc_ref, lhs_ref, ysum_ref, rb, wb, kb, vb, ab, bb):
        @pl.when(pl.program_id(0) == 0)
        def _():
            s_ref[...] = jnp.zeros_like(s_ref)

        _wkv_prebroadcast((r_ref, w_ref, k_ref, v_ref, a_ref, b_ref), (rb, wb, kb, vb, ab, bb), c)
        lane_t, diag, ones = _wkv_consts()

        def fill_v(t, carry):
            lhs_ref[t] = _wkv_split(_wkv_full(vb, t) * diag)
            return carry

        lax.fori_loop(0, c, fill_v, 0)
        _wkv_rowseg_all(lhs_ref, ones, vst_ref)
        sa_ref[...] = _wkv_rowseg(s_ref[...] * _wkv_full(ab, 0), ones)

        def pair(p, carry):
            for half in range(2):
                t = 2 * p + half
                s, sa = s_ref[...], sa_ref[...]
                sst_ref[t] = s
                sast_ref[t] = sa
                s = s * _wkv_full(wb, t) + sa * _wkv_full(bb, t) + vst_ref[t] * _wkv_full(kb, t)
                s_ref[...] = s
                sa_ref[...] = _wkv_rowseg(s * _wkv_full(ab, jnp.minimum(t + 1, c - 1)), ones)
                lhs_ref[p, :, half * LANES:(half + 1) * LANES] = (s * _wkv_full(rb, t)).astype(BF16)
            return carry

        lax.fori_loop(0, c // 2, pair, 0)
        _wkv_pair_sums(lhs_ref, ysum_ref)
        acc_ref[...] = jnp.zeros_like(acc_ref)

        def gather_y(p, carry):
            both = ysum_ref[p]
            acc = jnp.where(lane_t == 2 * p, both[:, :LANES], acc_ref[...])
            acc_ref[...] = jnp.where(lane_t == 2 * p + 1, both[:, LANES:], acc)
            return carry

        lax.fori_loop(0, c // 2, gather_y, 0)
        y_ref[...] = _wkv_transpose_out(acc_ref[...], c, diag)
        fin_ref[...] = s_ref[...]

    row = pl.BlockSpec((c, RW), lambda i: (i, 0))
    st = pl.BlockSpec((c, ns, LANES), lambda i: (i, 0, 0))
    state = pltpu.VMEM((ns, LANES), F32)
    return pl.pallas_call(
        body, grid=(nc,), in_specs=[row] * 6,
        out_specs=[row, st, st, st, pl.BlockSpec((ns, LANES), lambda i: (0, 0))],
        out_shape=[SDS((length, RW), F32)] + [SDS((length, ns, LANES), F32)] * 3 + [SDS((ns, LANES), F32)],
        scratch_shapes=[state] * 3 + [pltpu.VMEM((c, ns, 2 * LANES), BF16), pltpu.VMEM((c // 2, ns, 2 * LANES), F32)]
        + [pltpu.VMEM((c, SUBLANES, RW), F32)] * 6,
        compiler_params=_params(), name="wkv_fwd")(r, w, k, v, a, b)


def _wkv_bwd(r, w, k, a, b, dy, sst, sast, vst, fin):
    length = r.shape[0]
    c = min(WKV_CHUNK, length)
    nc = length // c
    ns = 4 * HN

    def body(r_ref, w_ref, k_ref, a_ref, b_ref, dy_ref, sst_ref, sast_ref, vst_ref, snext_ref, fin_ref,
             dr_ref, dw_ref, dk_ref, da_ref, db_ref, dv_ref,
             ds_ref, cur_ref, acc_ref, dyst, lhs_ref, dvsum_ref, rb, wb, kb, ab, bb, dyb, p_r, p_w, p_k, p_a, p_b):
        @pl.when(pl.program_id(0) == 0)
        def _():
            ds_ref[...] = jnp.zeros_like(ds_ref)

        _wkv_prebroadcast((r_ref, w_ref, k_ref, a_ref, b_ref, dy_ref), (rb, wb, kb, ab, bb, dyb), c)
        lane_t, diag, ones = _wkv_consts()

        def fill(t, carry):
            lhs_ref[t] = _wkv_split(_wkv_full(dyb, t) * diag)
            return carry

        lax.fori_loop(0, c, fill, 0)
        _wkv_rowseg_all(lhs_ref, ones, dyst)
        cur_ref[...] = jnp.where(pl.program_id(0) == 0, fin_ref[...], snext_ref[0])

        def pair(p, carry):
            for half in range(2):
                t = c - 1 - (2 * p + half)
                s_t, s_prev, sa, vv, dyy = cur_ref[...], sst_ref[t], sast_ref[t], vst_ref[t], dyst[t]
                ds = ds_ref[...] + dyy * _wkv_full(rb, t)
                dsa = _wkv_rowseg(ds * _wkv_full(bb, t), ones)
                ds_ref[...] = ds * _wkv_full(wb, t) + dsa * _wkv_full(ab, t)
                cur_ref[...] = s_prev
                lhs_ref[p, :, half * LANES:(half + 1) * LANES] = (ds * _wkv_full(kb, t)).astype(BF16)
                p_r[t] = _wkv_colsum8(s_t * dyy)
                p_k[t] = _wkv_colsum8(ds * vv)
                p_b[t] = _wkv_colsum8(ds * sa)
                p_w[t] = _wkv_colsum8(ds * s_prev)
                p_a[t] = _wkv_colsum8(s_prev * dsa)
            return carry

        lax.fori_loop(0, c // 2, pair, 0)
        _wkv_pair_sums(lhs_ref, dvsum_ref)
        acc_ref[...] = jnp.zeros_like(acc_ref)

        def gather_dv(p, carry):
            both = dvsum_ref[p]
            acc = jnp.where(lane_t == c - 1 - 2 * p, both[:, :LANES], acc_ref[...])
            acc_ref[...] = jnp.where(lane_t == c - 2 - 2 * p, both[:, LANES:], acc)
            return carry

        lax.fori_loop(0, c // 2, gather_dv, 0)
        sel = (lax.broadcasted_iota(jnp.int32, (c, c * SUBLANES), 1) // SUBLANES
               == lax.broadcasted_iota(jnp.int32, (c, c * SUBLANES), 0)).astype(F32)
        for out_ref, part in ((dr_ref, p_r), (dw_ref, p_w), (dk_ref, p_k), (da_ref, p_a), (db_ref, p_b)):
            out_ref[...] = _dot3(sel, part[...].reshape(c * SUBLANES, RW), 1, 0)
        dv_ref[...] = _wkv_transpose_out(acc_ref[...], c, diag)

    rmap = lambda i: nc - 1 - i
    row = pl.BlockSpec((c, RW), lambda i: (rmap(i), 0))
    st = pl.BlockSpec((c, ns, LANES), lambda i: (rmap(i), 0, 0))
    nxt = pl.BlockSpec((1, ns, LANES), lambda i: (jnp.minimum((rmap(i) + 1) * c, length - 1), 0, 0))
    state = pltpu.VMEM((ns, LANES), F32)
    return pl.pallas_call(
        body, grid=(nc,), in_specs=[row] * 6 + [st, st, st, nxt, pl.BlockSpec((ns, LANES), lambda i: (0, 0))],
        out_specs=[row] * 6, out_shape=[SDS((length, RW), F32)] * 6,
        scratch_shapes=[state] * 3 + [pltpu.VMEM((c, ns, LANES), F32), pltpu.VMEM((c, ns, 2 * LANES), BF16),
                                       pltpu.VMEM((c // 2, ns, 2 * LANES), F32)]
        + [pltpu.VMEM((c, SUBLANES, RW), F32)] * 11,
        compiler_params=_params(), name="wkv_bwd")(r, w, k, a, b, dy, sst, sast, vst, sst, fin)


def _my_place():
    return lax.axis_index("x"), lax.axis_index("y"), lax.axis_index("c")


def _peer(x, y, c, k):
    return (x ^ ((k >> 2) & 1), y ^ ((k >> 1) & 1), c ^ (k & 1))


def _all_gather(shard, name):
    rows = shard.shape[0]

    def body(in_ref, out_ref, send_sems, recv_sems, local_sem):
        x, y, c = _my_place()
        me = 4 * x + 2 * y + c
        mine = pltpu.make_async_copy(in_ref, out_ref.at[me], local_sem)
        mine.start()
        sends = []
        for k in range(1, N_DEV):
            cp = pltpu.make_async_remote_copy(src_ref=in_ref, dst_ref=out_ref.at[me], send_sem=send_sems.at[k - 1],
                                              recv_sem=recv_sems.at[k - 1], device_id=_peer(x, y, c, k), device_id_type=MESH)
            cp.start()
            sends.append(cp)
        for k in range(1, N_DEV):
            px, py, pc = _peer(x, y, c, k)
            pltpu.make_async_remote_copy(src_ref=in_ref, dst_ref=out_ref.at[4 * px + 2 * py + pc], send_sem=send_sems.at[k - 1],
                                         recv_sem=recv_sems.at[k - 1], device_id=(px, py, pc), device_id_type=MESH).wait_recv()
        for cp in sends:
            cp.wait_send()
        mine.wait()

    return pl.pallas_call(
        body, out_shape=SDS((N_DEV, rows, LANES), shard.dtype),
        in_specs=[pl.BlockSpec(memory_space=pl.ANY)], out_specs=pl.BlockSpec(memory_space=pl.ANY),
        scratch_shapes=[pltpu.SemaphoreType.DMA((N_DEV - 1,)), pltpu.SemaphoreType.DMA((N_DEV - 1,)), pltpu.SemaphoreType.DMA],
        name=name)(shard)


def _all_to_all(chunks, name):
    rows = chunks.shape[1]

    def body(in_ref, out_ref, send_sems, recv_sems, local_sem):
        x, y, c = _my_place()
        me = 4 * x + 2 * y + c
        mine = pltpu.make_async_copy(in_ref.at[me], out_ref.at[me], local_sem)
        mine.start()
        sends = []
        for k in range(1, N_DEV):
            px, py, pc = _peer(x, y, c, k)
            cp = pltpu.make_async_remote_copy(src_ref=in_ref.at[4 * px + 2 * py + pc], dst_ref=out_ref.at[me],
                                              send_sem=send_sems.at[k - 1], recv_sem=recv_sems.at[k - 1],
                                              device_id=(px, py, pc), device_id_type=MESH)
            cp.start()
            sends.append(cp)
        for k in range(1, N_DEV):
            px, py, pc = _peer(x, y, c, k)
            pltpu.make_async_remote_copy(src_ref=in_ref.at[me], dst_ref=out_ref.at[4 * px + 2 * py + pc], send_sem=send_sems.at[k - 1],
                                         recv_sem=recv_sems.at[k - 1], device_id=(px, py, pc), device_id_type=MESH).wait_recv()
        for cp in sends:
            cp.wait_send()
        mine.wait()

    return pl.pallas_call(
        body, out_shape=SDS((N_DEV, rows, LANES), chunks.dtype),
        in_specs=[pl.BlockSpec(memory_space=pl.ANY)], out_specs=pl.BlockSpec(memory_space=pl.ANY),
        scratch_shapes=[pltpu.SemaphoreType.DMA((N_DEV - 1,)), pltpu.SemaphoreType.DMA((N_DEV - 1,)), pltpu.SemaphoreType.DMA],
        name=name)(chunks)


def _sum_parts(parts, name):
    rows = parts.shape[1]

    def body(p_ref, g_out):
        g = p_ref[0].astype(F32)
        for s in range(1, N_DEV):
            g = g + p_ref[s].astype(F32)
        g_out[...] = g

    return pl.pallas_call(
        body, grid=(rows // PACK_ROWS,), in_specs=[pl.BlockSpec((N_DEV, PACK_ROWS, LANES), lambda i: (0, i, 0))],
        out_specs=pl.BlockSpec((PACK_ROWS, LANES), lambda i: (i, 0)), out_shape=SDS((rows, LANES), F32),
        compiler_params=_params(("parallel",)), name=name)(parts)


def _adamw(g, wgt, m, v):
    rows = wgt.shape[0]

    def body(g_ref, w_ref, m_ref, v_ref, d_out, m_out, v_out):
        g_ = g_ref[...]
        m_new = ADAM_B1 * m_ref[...] + (1.0 - ADAM_B1) * g_
        v_new = ADAM_B2 * v_ref[...] + (1.0 - ADAM_B2) * (g_ * g_)
        m_hat = m_new / (1.0 - ADAM_B1 ** ADAM_STEP)
        v_hat = v_new / (1.0 - ADAM_B2 ** ADAM_STEP)
        d_out[...] = -ADAM_LR * (m_hat / (jnp.sqrt(v_hat) + ADAM_EPS) + ADAM_WD * w_ref[...])
        m_out[...] = m_new
        v_out[...] = v_new

    blk = pl.BlockSpec((PACK_ROWS, LANES), lambda i: (i, 0))
    return pl.pallas_call(
        body, grid=(rows // PACK_ROWS,), in_specs=[blk] * 4, out_specs=[blk] * 3, out_shape=[SDS((rows, LANES), F32)] * 3,
        compiler_params=_params(("parallel",)), name="adamw")(g, wgt, m, v)


PACK_ALIGN = 2 * SUBLANES * LANES
PACK_ROWS = 512

ROW_SHARDED = ("s5_w_glu", "w_out", "ffn_w_down")
TRANSPOSED = ("w_in", "ffn_w_up", "w_branch_rwkv", "w_branch_s5")
SMALL_SHARDED = ("rwkv_w2", "rwkv_a2", "rwkv_g2", "ffn_conv_w")
SHARDED = ROW_SHARDED + TRANSPOSED + SMALL_SHARDED
F32_GATHER = ("ffn_conv_w",)
REPLICATED =("norm_mix_pre", "norm_mix_post", "norm_ffn_pre", "norm_ffn_post", "b_gate", "rwkv_shift_mu", "rwkv_w0",
              "rwkv_a0", "rwkv_k_k", "rwkv_k_a", "rwkv_r_k", "rwkv_lnx_w", "rwkv_lnx_b", "s5_a_re", "s5_a_im", "s5_b_re",
              "s5_b_im", "s5_c_re", "s5_c_im", "s5_d", "s5_log_step", "s5_b_glu", "ffn_conv_b")
WEIGHTS = ("norm_mix_pre", "norm_mix_post", "norm_ffn_pre", "norm_ffn_post", "w_in", "b_gate", "rwkv_shift_mu", "rwkv_w0",
           "rwkv_w2", "rwkv_a0", "rwkv_a2", "rwkv_g2", "rwkv_k_k", "rwkv_k_a", "rwkv_r_k", "rwkv_lnx_w", "rwkv_lnx_b",
           "s5_a_re", "s5_a_im", "s5_b_re", "s5_b_im", "s5_c_re", "s5_c_im", "s5_d", "s5_log_step", "s5_w_glu", "s5_b_glu",
           "w_branch_rwkv", "w_branch_s5", "w_out", "ffn_w_up", "ffn_conv_w", "ffn_conv_b", "ffn_w_down")


def _pad_flat(a):
    flat = a.reshape(-1)
    pad = (-flat.shape[0]) % PACK_ALIGN
    return jnp.pad(flat, (0, pad)) if pad else flat


def _pad_cols(a, mult):
    pad = (-a.shape[1]) % mult
    return jnp.pad(a, ((0, 0), (0, pad))) if pad else a


def _pack(pieces):
    flat = jnp.concatenate([_pad_flat(p) for p in pieces])
    pad = (-flat.shape[0]) % (PACK_ROWS * LANES)
    return (jnp.pad(flat, (0, pad)) if pad else flat).reshape(-1, LANES)


def _unpack(buf, shapes):
    lead = buf.shape[:-2]
    flat = buf.reshape(*lead, -1)
    out, off = [], 0
    for s in shapes:
        n = math.prod(s)
        out.append(flat[..., off:off + n].reshape(*lead, *s))
        off += n + (-n) % PACK_ALIGN
    return out


def _rows_to_full(stack):
    return stack.reshape(-1, stack.shape[2])


def _cols_to_full(stack):
    return stack.transpose(1, 0, 2).reshape(stack.shape[1], -1)


def _local_step(x, tgt, wt):
    o512 = _block_ones(RW, HN)
    w_in = wt["w_in"]
    w_rw, w_u, w_g = w_in[:N_RWKV], w_in[N_RWKV:N_RWKV + S5W], w_in[N_RWKV + S5W:]
    b_gate = wt["b_gate"]
    b_r, b_s = b_gate[:, :D_MODEL], b_gate[:, D_MODEL:]
    zpad = lambda a, lo, n: jnp.pad(a.astype(F32), ((lo, n - lo - a.shape[0]), (0, 0)))
    w2p, a2p, g2p = zpad(wt["rwkv_w2"], 0, 256), zpad(wt["rwkv_a2"], 64, 256), zpad(wt["rwkv_g2"], 128, 256)
    pre_small = [wt["rwkv_w0"], wt["rwkv_a0"], wt["rwkv_k_k"], wt["rwkv_k_a"], w2p, a2p, g2p]
    post_prm = [wt["rwkv_lnx_w"], wt["rwkv_lnx_b"], wt["rwkv_r_k"].reshape(1, RW)]
    mu = wt["rwkv_shift_mu"]

    a_re, a_im = wt["s5_a_re"].reshape(1, S5N), wt["s5_a_im"].reshape(1, S5N)
    ls = jnp.repeat(wt["s5_log_step"].reshape(S5G), S5P).reshape(1, S5N)
    b_re_t = wt["s5_b_re"].reshape(S5N, S5C).T
    b_im_t = wt["s5_b_im"].reshape(S5N, S5C).T
    c_re, c_im = wt["s5_c_re"].reshape(S5G, S5C, S5P), wt["s5_c_im"].reshape(S5G, S5C, S5P)
    abr, abi, bbr, bbi = _s5_prep_fwd(a_re, a_im, ls, b_re_t, b_im_t)
    bblk = jnp.concatenate([_s5_blockdiag_in(bbr), _s5_blockdiag_in(bbi)], axis=2)
    cblk = jnp.concatenate([_s5_blockdiag_out(c_re), -_s5_blockdiag_out(c_im)], axis=1)
    s5_d = wt["s5_d"]

    h1 = _rms_fwd(x, wt["norm_mix_pre"], "rms1_fwd")
    p_rw = _mm(h1, w_rw, tb=True, name="mm_proj_rwkv")
    u = _mm(h1, w_u, tb=True, name="mm_proj_s5")
    gpre = _mm(h1, w_g, tb=True, name="mm_proj_gate")
    r, decay, k2, v, aa, bb, g = _rwkv_pre_fwd(p_rw, mu, pre_small, o512)
    y, sst, sast, vst, s_fin = _wkv_fwd(r, decay, k2, v, aa, bb)
    y_r = _rwkv_post_fwd(y, r, k2, v, g, post_prm, o512)
    o_r = _mm(y_r, wt["w_branch_rwkv"], tb=True, name="mm_branch_rwkv")
    ys, kre, kim = _s5_fwd(u, bblk, cblk, abr, abi, s5_d)
    t_glu = _mm(ys, wt["s5_w_glu"], name="mm_glu")
    out5 = _glu_fwd(ys, t_glu, wt["s5_b_glu"])
    o_s = _mm(out5, wt["w_branch_s5"], tb=True, name="mm_branch_s5")
    mi = _merge_fwd(gpre, o_r, o_s, b_r, b_s)
    mixed = _mm(mi, wt["w_out"], name="mm_out")
    x2, h2 = _mid_fwd(x, mixed, wt["norm_mix_post"], wt["norm_ffn_pre"])
    z = _mm(h2, wt["ffn_w_up"], tb=True, name="mm_up")
    act = _conv_fwd(z, wt["ffn_conv_w"], wt["ffn_conv_b"])
    f = _mm(act, wt["ffn_w_down"], name="mm_down")
    dx3, df, d_norm_ffn_post, loss_part = _final(x2, f, tgt, wt["norm_ffn_post"])

    gr = {"norm_ffn_post": d_norm_ffn_post}
    dact = _mm(df, wt["ffn_w_down"], tb=True, name="mm_down_dx")
    gr["ffn_w_down"] = _mm(act, df, ta=True, name="mm_down_dw")
    dzg, dzv, gr["ffn_conv_w"], gr["ffn_conv_b"] = _conv_bwd(z, dact, wt["ffn_conv_w"], wt["ffn_conv_b"])
    dz = jnp.concatenate([dzg, dzv], axis=1)
    dh2 = _mm(dz, wt["ffn_w_up"], name="mm_up_dx")
    gr["ffn_w_up"] = _mm(dz, h2, ta=True, name="mm_up_dw")
    dx2, dmixed, gr["norm_mix_post"], gr["norm_ffn_pre"] = _mid_bwd(x2, mixed, dh2, dx3, wt["norm_mix_post"], wt["norm_ffn_pre"])
    dmi = _mm(dmixed, wt["w_out"], tb=True, name="mm_out_dx")
    gr["w_out"] = _mm(mi, dmixed, ta=True, name="mm_out_dw")
    dgp_r, dgp_s, do_r, do_s, db_r, db_s = _merge_bwd(gpre, o_r, o_s, dmi, b_r, b_s)
    gr["b_gate"] = jnp.concatenate([db_r, db_s], axis=1)
    dout5 = _mm(do_s, wt["w_branch_s5"], name="mm_branch_s5_dx")
    gr["w_branch_s5"] = _mm(do_s, out5, ta=True, name="mm_branch_s5_dw")
    dys_a, dt_glu, gr["s5_b_glu"] = _glu_bwd(ys, t_glu, dout5, wt["s5_b_glu"])
    dys_b = _mm(dt_glu, wt["s5_w_glu"], tb=True, name="mm_glu_dx")
    gr["s5_w_glu"] = _mm(ys, dt_glu, ta=True, name="mm_glu_dw")
    du, dbblk, dcblk, dabr, dabi, gr["s5_d"] = _s5_bwd(u, dys_a, dys_b, kre, kim, bblk, cblk, abr, abi, s5_d)
    gr["s5_c_re"] = _s5_blockdiag_out_t(dcblk[:, :S5_BN, :]).reshape(wt["s5_c_re"].shape)
    gr["s5_c_im"] = (-_s5_blockdiag_out_t(dcblk[:, S5_BN:, :])).reshape(wt["s5_c_im"].shape)
    dbbr, dbbi = _s5_blockdiag_in_t(dbblk[:, :, :S5_BN]), _s5_blockdiag_in_t(dbblk[:, :, S5_BN:])
    gsel = (lax.broadcasted_iota(jnp.int32, (S5N, LANES), 0) // S5P == lax.broadcasted_iota(jnp.int32, (S5N, LANES), 1)).astype(F32)
    d_are, d_aim, d_ls, d_bre_t, d_bim_t = _s5_prep_bwd(a_re, a_im, ls, b_re_t, b_im_t, (dabr, dabi, dbbr, dbbi), gsel)
    gr["s5_a_re"] = d_are.reshape(wt["s5_a_re"].shape)
    gr["s5_a_im"] = d_aim.reshape(wt["s5_a_im"].shape)
    gr["s5_log_step"] = d_ls[0:1, :S5G]
    gr["s5_b_re"] = d_bre_t.T.reshape(wt["s5_b_re"].shape)
    gr["s5_b_im"] = d_bim_t.T.reshape(wt["s5_b_im"].shape)
    dy_r = _mm(do_r, wt["w_branch_rwkv"], name="mm_branch_rwkv_dx")
    gr["w_branch_rwkv"] = _mm(do_r, y_r, ta=True, name="mm_branch_rwkv_dw")
    dy, dr1, dk1, dv1, dg, gr["rwkv_lnx_w"], gr["rwkv_lnx_b"], d_rk = _rwkv_post_bwd(y, r, k2, v, g, dy_r, post_prm, o512)
    gr["rwkv_r_k"] = d_rk.reshape(wt["rwkv_r_k"].shape)
    dr2, ddecay, dk2, daa, dbb, dv2 = _wkv_bwd(r, decay, k2, aa, bb, dy, sst, sast, vst, s_fin)
    cots = [dr1 + dr2, ddecay, dk1 + dk2, dv1 + dv2, daa, dbb, dg]
    dp_rw, gr["rwkv_shift_mu"], gr["rwkv_w0"], gr["rwkv_a0"], gr["rwkv_k_k"], gr["rwkv_k_a"], dw2p, da2p, dg2p = \
        _rwkv_pre_bwd(p_rw, cots, mu, pre_small, o512)
    gr["rwkv_w2"], gr["rwkv_a2"], gr["rwkv_g2"] = dw2p[:64], da2p[64:128], dg2p[128:]
    dproj = jnp.concatenate([dp_rw, du, dgp_r, dgp_s], axis=1)
    dh1 = _mm(dproj, w_in, name="mm_proj_dx")
    gr["w_in"] = _mm(dproj, h1, ta=True, name="mm_proj_dw")
    dx, gr["norm_mix_pre"] = _rms_bwd(x, wt["norm_mix_pre"], dh1, dx2, "rms1_bwd")
    return loss_part[0, 0], dx, gr


def kernel(x, norm_mix_pre, norm_mix_post, norm_ffn_pre, norm_ffn_post, w_in, b_gate, rwkv_shift_mu, rwkv_w0, rwkv_w2, rwkv_a0, rwkv_a2, rwkv_g2, rwkv_k_k, rwkv_k_a, rwkv_r_k, rwkv_lnx_w, rwkv_lnx_b, s5_a_re, s5_a_im, s5_b_re, s5_b_im, s5_c_re, s5_c_im, s5_d, s5_log_step, s5_w_glu, s5_b_glu, w_branch_rwkv, w_branch_s5, w_out, ffn_w_up, ffn_conv_w, ffn_conv_b, ffn_w_down, loss_target, m_norm_mix_pre, m_norm_mix_post, m_norm_ffn_pre, m_norm_ffn_post, m_w_in, m_b_gate, m_rwkv_shift_mu, m_rwkv_w0, m_rwkv_w2, m_rwkv_a0, m_rwkv_a2, m_rwkv_g2, m_rwkv_k_k, m_rwkv_k_a, m_rwkv_r_k, m_rwkv_lnx_w, m_rwkv_lnx_b, m_s5_a_re, m_s5_a_im, m_s5_b_re, m_s5_b_im, m_s5_c_re, m_s5_c_im, m_s5_d, m_s5_log_step, m_s5_w_glu, m_s5_b_glu, m_w_branch_rwkv, m_w_branch_s5, m_w_out, m_ffn_w_up, m_ffn_conv_w, m_ffn_conv_b, m_ffn_w_down, v_norm_mix_pre, v_norm_mix_post, v_norm_ffn_pre, v_norm_ffn_post, v_w_in, v_b_gate, v_rwkv_shift_mu, v_rwkv_w0, v_rwkv_w2, v_rwkv_a0, v_rwkv_a2, v_rwkv_g2, v_rwkv_k_k, v_rwkv_k_a, v_rwkv_r_k, v_rwkv_lnx_w, v_rwkv_lnx_b, v_s5_a_re, v_s5_a_im, v_s5_b_re, v_s5_b_im, v_s5_c_re, v_s5_c_im, v_s5_d, v_s5_log_step, v_s5_w_glu, v_s5_b_glu, v_w_branch_rwkv, v_w_branch_s5, v_w_out, v_ffn_w_up, v_ffn_conv_w, v_ffn_conv_b, v_ffn_w_down):
    args = dict(locals())
    wgt = {n: args[n] for n in WEIGHTS}
    mom = {n: args["m_" + n] for n in WEIGHTS}
    var = {n: args["v_" + n] for n in WEIGHTS}
    me = 4 * lax.axis_index("x") + 2 * lax.axis_index("y") + lax.axis_index("c")
    shard_shapes = {n: wgt[n].shape[1:] for n in SHARDED}
    sent_shapes = {n: (shard_shapes[n][::-1] if n in TRANSPOSED else shard_shapes[n]) for n in SHARDED}
    sent = lambda n, a: a.T if n in TRANSPOSED else a

    bf_names = [n for n in SHARDED if n not in F32_GATHER]
    got_bf = _unpack(_all_gather(_pack([sent(n, wgt[n][0].astype(BF16)) for n in bf_names]), "all_gather_weights"),
                     [sent_shapes[n] for n in bf_names])
    got_f32 = _unpack(_all_gather(_pack([wgt[n][0] for n in F32_GATHER]), "all_gather_taps"),
                      [sent_shapes[n] for n in F32_GATHER])
    wt = {n: wgt[n] for n in REPLICATED}
    for n, blocks in zip(bf_names + list(F32_GATHER), got_bf + got_f32, strict=True):
        wt[n] = _cols_to_full(blocks) if n in SMALL_SHARDED else _rows_to_full(blocks)

    loss_part, dx, gr = _local_step(x[0], loss_target[0], wt)

    big = ROW_SHARDED + TRANSPOSED
    whole = SMALL_SHARDED + REPLICATED
    pieces = [_pad_cols(gr[n].reshape(N_DEV, -1).astype(BF16), PACK_ALIGN) for n in big]
    chunks = _pad_cols(jnp.concatenate(pieces, axis=1), PACK_ROWS * LANES).reshape(N_DEV, -1, LANES)
    big_sum = _sum_parts(_all_to_all(chunks, "exchange_grads"), "sum_grads")
    small_sum = _sum_parts(_all_gather(_pack([gr[n] for n in whole]), "gather_small_grads"), "sum_small_grads")
    got = dict(zip(big + whole, _unpack(big_sum, [sent_shapes[n] for n in big]) + _unpack(small_sum, [gr[n].shape for n in whole]),
                   strict=True))
    grads = {}
    for n in WEIGHTS:
        if n in TRANSPOSED:
            grads[n] = got[n].T
        elif n in SMALL_SHARDED:
            cols = shard_shapes[n][1]
            grads[n] = lax.dynamic_slice_in_dim(got[n], me * cols, cols, axis=1)
        else:
            grads[n] = got[n]
        grads[n] = grads[n].reshape(wgt[n].shape)

    pack_local = lambda src: _pack([src[n].reshape(-1) for n in WEIGHTS])
    outs = _adamw(pack_local(grads), pack_local(wgt), pack_local(mom), pack_local(var))
    shapes = [wgt[n].shape for n in WEIGHTS]
    loss = lax.psum(loss_part, AXES)
    return (loss, dx[None], *[grads[n] for n in WEIGHTS], *_unpack(outs[0], shapes), *_unpack(outs[1], shapes),
            *_unpack(outs[2], shapes))
```

```python
import functools
import math

import jax
import jax.numpy as jnp
from jax import lax
from jax.experimental import pallas as pl
from jax.experimental.pallas import tpu as pltpu

F32 = jnp.float32
BF16 = jnp.bfloat16
SDS = jax.ShapeDtypeStruct
HI = lax.Precision.HIGHEST
MESH = pl.DeviceIdType.MESH
AXES = ("x", "y", "c")
N_DEV = 8

D_MODEL = 1024
RW = 512
HN = 64
N_RWKV = 1792
S5W = 512
S5G = 32
S5C = 16
S5P = 64
S5N = S5G * S5P
D_FF = 2816
NORM_EPS = 1e-6
LNX_EPS = 64e-5

ADAM_LR = 0.001
ADAM_B1 = 0.9
ADAM_B2 = 0.999
ADAM_EPS = 1e-08
ADAM_WD = 0.01
ADAM_STEP = 10

LANES = 128
SUBLANES = 8
VMEM_LIMIT = 56 * 1024 * 1024
WKV_CHUNK = 32


def _params(sem=("arbitrary",)):
    return pltpu.CompilerParams(dimension_semantics=sem, vmem_limit_bytes=VMEM_LIMIT)


def _pick(n, cap):
    best = None
    for t in range(LANES, min(n, cap) + 1, LANES):
        if n % t == 0:
            best = t
    return best or n


def _mm(a, b, *, ta=False, tb=False, out_dtype=F32, name):
    m = a.shape[1] if ta else a.shape[0]
    k = a.shape[0] if ta else a.shape[1]
    n = b.shape[0] if tb else b.shape[1]
    assert (b.shape[1] if tb else b.shape[0]) == k
    tm, tn, tk = _pick(m, 1024), _pick(n, 2304), _pick(k, 512)
    nk = k // tk
    dims = (((0 if ta else 1,), (1 if tb else 0,)), ((), ()))

    def body(a_ref, b_ref, o_ref, acc_ref):
        kk = pl.program_id(2)

        @pl.when(kk == 0)
        def _():
            acc_ref[...] = jnp.zeros_like(acc_ref)

        acc_ref[...] += lax.dot_general(a_ref[...].astype(BF16), b_ref[...].astype(BF16), dims,
                                        preferred_element_type=F32)

        @pl.when(kk == nk - 1)
        def _():
            o_ref[...] = acc_ref[...].astype(o_ref.dtype)

    a_spec = pl.BlockSpec((tk, tm), lambda i, j, kk: (kk, i)) if ta else pl.BlockSpec((tm, tk), lambda i, j, kk: (i, kk))
    b_spec = pl.BlockSpec((tn, tk), lambda i, j, kk: (j, kk)) if tb else pl.BlockSpec((tk, tn), lambda i, j, kk: (kk, j))
    return pl.pallas_call(
        body, grid=(m // tm, n // tn, nk), in_specs=[a_spec, b_spec],
        out_specs=pl.BlockSpec((tm, tn), lambda i, j, kk: (i, j)),
        out_shape=SDS((m, n), out_dtype), scratch_shapes=[pltpu.VMEM((tm, tn), F32)],
        compiler_params=_params(("parallel", "parallel", "arbitrary")), name=name)(a, b)


def _rows(fn, rows, params, out_rows, out_accs, *, name, tl, reverse=False, scratch=()):
    first = rows[0][0] if isinstance(rows[0], tuple) else rows[0]
    length = first.shape[0]
    tl = min(tl, length)
    nt = length // tl
    rmap = (lambda i: nt - 1 - i) if reverse else (lambda i: i)
    specs, arrs = [], []
    for r in rows:
        arr, wdt, cb = r if isinstance(r, tuple) else (r, r.shape[1], 0)
        specs.append(pl.BlockSpec((tl, wdt), lambda i, cb=cb: (rmap(i), cb)))
        arrs.append(arr)
    for p in params:
        specs.append(pl.BlockSpec(p.shape, lambda i, nd=p.ndim: (0,) * nd))
        arrs.append(p)
    out_shape = [SDS((length, c), dt) for c, dt in out_rows] + [SDS(s, F32) for s in out_accs]
    out_specs = [pl.BlockSpec((tl, c), lambda i: (rmap(i), 0)) for c, _ in out_rows]
    out_specs += [pl.BlockSpec(s, lambda i, nd=len(s): (0,) * nd) for s in out_accs]
    nr, npar, nor, noa = len(rows), len(params), len(out_rows), len(out_accs)

    def body(*refs):
        rin, pin = refs[:nr], refs[nr:nr + npar]
        rout = refs[nr + npar:nr + npar + nor]
        aout = refs[nr + npar + nor:nr + npar + nor + noa]
        scr = refs[nr + npar + nor + noa:]
        step = pl.program_id(0)
        outs_r, outs_a = fn(step, [r[...] for r in rin], [p[...] for p in pin], scr)
        for ref, val in zip(rout, outs_r, strict=True):
            ref[...] = val.astype(ref.dtype)

        @pl.when(step == 0)
        def _():
            for ref in aout:
                ref[...] = jnp.zeros_like(ref)

        for ref, val in zip(aout, outs_a, strict=True):
            ref[...] += val.astype(F32)

    res = pl.pallas_call(body, grid=(nt,), in_specs=specs, out_specs=out_specs, out_shape=out_shape,
                         scratch_shapes=list(scratch), compiler_params=_params(), name=name)(*arrs)
    return list(res)


def _rms(x, g):
    return x * lax.rsqrt(jnp.mean(x * x, axis=-1, keepdims=True) + NORM_EPS) * g


def _sig(x):
    return 0.5 * (jnp.tanh(0.5 * x) + 1.0)


def _softplus(x):
    return jnp.maximum(x, 0.0) + jnp.log(1.0 + jnp.exp(-jnp.abs(x)))


def _gelu(x):
    return x * (0.5 * (1.0 + jnp.tanh(math.sqrt(2.0 / math.pi) * (x + 0.044715 * (x * x * x)))))


def _bdot(a, b):
    return jnp.dot(a.astype(BF16), b.astype(BF16), preferred_element_type=F32)


def _hdot(a, b):
    return jnp.dot(a, b, precision=HI, preferred_element_type=F32)


def _segsum_impl(x, ones):
    hi = x.astype(BF16)
    lo = (x - hi.astype(F32)).astype(BF16)
    ones = ones.astype(BF16)
    return jnp.dot(jnp.concatenate([hi, lo], axis=1), jnp.concatenate([ones, ones], axis=0), preferred_element_type=F32)


@jax.custom_vjp
def _segsum(x, ones):
    return _segsum_impl(x, ones)


_segsum.defvjp(lambda x, ones: (_segsum_impl(x, ones), ones),
               lambda ones, g: (_segsum_impl(g, ones), jnp.zeros_like(ones)))


def _block_ones(n, blk):
    i = lax.broadcasted_iota(jnp.int32, (n, n), 0) // blk
    j = lax.broadcasted_iota(jnp.int32, (n, n), 1) // blk
    return (i == j).astype(F32)


def _rms_fwd(x, g, name):
    return _rows(lambda s, r, p, _: ([_rms(r[0], p[0])], []), [x], [g], [(x.shape[1], BF16)], [], name=name, tl=512)[0]


def _rms_bwd(x, g, dh, dres, name):
    def fn(s, r, p, _):
        _, vjp = jax.vjp(_rms, r[0], p[0])
        dx, dg = vjp(r[1])
        return [dx + r[2]], [dg]
    return _rows(fn, [x, dh, dres], [g], [(x.shape[1], F32)], [g.shape], name=name, tl=256)


def _rwkv_pre_math(k_, lr, w0, a0, k_k, k_a, w2p, a2p, g2p, o512):
    pre_w = w0 + _bdot(jnp.tanh(lr), w2p)
    w = -_softplus(-pre_w) - 0.5
    decay = jnp.exp(-jnp.exp(w))
    a = _sig(a0 + _bdot(lr, a2p))
    g = _bdot(_sig(lr), g2p)
    kr = k_ * k_k
    kk = kr / jnp.maximum(jnp.sqrt(_segsum(kr * kr, o512)), 1e-12)
    k2 = k_ * (1.0 + (a - 1.0) * k_a)
    return decay, k2, -kk, kk * a, g


def _shift_down(p, prev_row):
    row = lax.broadcasted_iota(jnp.int32, p.shape, 0)
    return jnp.where(row == 0, jnp.broadcast_to(prev_row, p.shape), pltpu.roll(p, 1, 0))


def _shift_up(q, next_row):
    n = q.shape[0]
    row = lax.broadcasted_iota(jnp.int32, q.shape, 0)
    return jnp.where(row == n - 1, jnp.broadcast_to(next_row, q.shape), pltpu.roll(q, n - 1, 0))


def _rwkv_pre_fwd(p, mu, small, o512):
    def fn(step, r, prm, scr):
        car = scr[0]

        @pl.when(step == 0)
        def _():
            car[...] = jnp.zeros_like(car)

        x = r[0]
        prev = _shift_down(x, car[SUBLANES - 1:SUBLANES, :])
        car[...] = x[x.shape[0] - SUBLANES:, :]
        xs = x + (prev - x) * prm[0]
        decay, k2, aa, bb, g = _rwkv_pre_math(xs[:, RW:2 * RW], xs[:, 3 * RW:], *prm[1:])
        return [xs[:, :RW], decay, k2, xs[:, 2 * RW:3 * RW], aa, bb, g], []
    return _rows(fn, [p], [mu, *small, o512], [(RW, F32)] * 7, [], name="rwkv_pre_fwd", tl=256,
                 scratch=[pltpu.VMEM((SUBLANES, N_RWKV), F32)])


def _rwkv_pre_bwd(p, cots, mu, small, o512):
    length = p.shape[0]
    tl = min(256, length)
    nt = length // tl
    rows_per = tl // SUBLANES
    params = [mu, *small, o512]
    acc_shapes = [mu.shape] + [q.shape for q in small]
    nr, npar, nacc = 2 + len(cots), len(params), len(acc_shapes)

    def body(*refs):
        rin, pin = refs[:nr], refs[nr:nr + npar]
        dp_ref = refs[nr + npar]
        aout = refs[nr + npar + 1:nr + npar + 1 + nacc]
        car_q = refs[nr + npar + 1 + nacc]
        step = pl.program_id(0)

        @pl.when(step == 0)
        def _():
            car_q[...] = jnp.zeros_like(car_q)
            for ref in aout:
                ref[...] = jnp.zeros_like(ref)

        x = rin[0][...]
        prev_row = jnp.where(step == nt - 1, 0.0, rin[1][SUBLANES - 1:SUBLANES, :])
        dr, ddecay, dk2, dv, daa, dbb, dg = [r[...] for r in rin[2:]]
        prm = [q[...] for q in pin]
        mu_, o512_ = prm[0], prm[-1]
        prev = _shift_down(x, prev_row)
        xs = x + (prev - x) * mu_
        _, vjp = jax.vjp(lambda k_, lr, *w: _rwkv_pre_math(k_, lr, *w, o512_), xs[:, RW:2 * RW], xs[:, 3 * RW:], *prm[1:-1])
        dk_, dlr, *dsmall = vjp((ddecay, dk2, daa, dbb, dg))
        dxs = jnp.concatenate([dr, dk_, dv, dlr], axis=1)
        q = dxs * mu_
        dp_ref[...] = (dxs - q + _shift_up(q, car_q[0:1, :])).astype(dp_ref.dtype)
        car_q[...] = q[:SUBLANES, :]
        aout[0][...] += jnp.sum((prev - x) * dxs, axis=0, keepdims=True)
        for ref, val in zip(aout[1:], dsmall, strict=True):
            ref[...] += val

    rmap = lambda i: nt - 1 - i
    specs = [pl.BlockSpec((tl, N_RWKV), lambda i: (rmap(i), 0)),
             pl.BlockSpec((SUBLANES, N_RWKV), lambda i: (jnp.maximum(rmap(i) * rows_per - 1, 0), 0))]
    specs += [pl.BlockSpec((tl, RW), lambda i: (rmap(i), 0)) for _ in cots]
    specs += [pl.BlockSpec(q.shape, lambda i, nd=q.ndim: (0,) * nd) for q in params]
    out_shape = [SDS((length, N_RWKV), BF16)] + [SDS(sh, F32) for sh in acc_shapes]
    out_specs = [pl.BlockSpec((tl, N_RWKV), lambda i: (rmap(i), 0))]
    out_specs += [pl.BlockSpec(sh, lambda i, nd=len(sh): (0,) * nd) for sh in acc_shapes]
    res = pl.pallas_call(body, grid=(nt,), in_specs=specs, out_specs=out_specs, out_shape=out_shape,
                         scratch_shapes=[pltpu.VMEM((SUBLANES, N_RWKV), F32)],
                         compiler_params=_params(), name="rwkv_pre_bwd")(p, p, *cots, *params)
    return list(res)


def _rwkv_post_math(y, r, k2, v, g, lnx_w, lnx_b, r_k, o512):
    mean = _segsum(y, o512) * (1.0 / HN)
    yc = y - mean
    var = _segsum(yc * yc, o512) * (1.0 / HN)
    yn = yc * lax.rsqrt(var + LNX_EPS) * lnx_w + lnx_b
    bonus = _segsum(r * k2 * r_k, o512) * v
    return (yn + bonus) * g


def _rwkv_post_fwd(y, r, k2, v, g, prm, o512):
    return _rows(lambda s, rr, p, _: ([_rwkv_post_math(*rr, *p)], []), [y, r, k2, v, g], [*prm, o512],
                 [(RW, BF16)], [], name="rwkv_post_fwd", tl=256)[0]


def _rwkv_post_bwd(y, r, k2, v, g, dout, prm, o512):
    def fn(s, rr, p, _):
        o = p[-1]
        _, vjp = jax.vjp(lambda *a: _rwkv_post_math(*a, o), *rr[:5], *p[:-1])
        gr = vjp(rr[5])
        return list(gr[:5]), list(gr[5:])
    return _rows(fn, [y, r, k2, v, g, dout], [*prm, o512], [(RW, F32)] * 5, [q.shape for q in prm],
                 name="rwkv_post_bwd", tl=256)


def _glu_math(ys, t, b):
    return ys * _sig(t + b)


def _glu_fwd(ys, t, b):
    return _rows(lambda s, r, p, _: ([_glu_math(r[0], r[1], p[0])], []), [ys, t], [b], [(S5W, BF16)], [],
                 name="s5_glu_fwd", tl=512)[0]


def _glu_bwd(ys, t, dout, b):
    def fn(s, r, p, _):
        _, vjp = jax.vjp(_glu_math, r[0], r[1], p[0])
        dys, dt, db = vjp(r[2])
        return [dys, dt], [db]
    return _rows(fn, [ys, t, dout], [b], [(S5W, F32), (S5W, BF16)], [b.shape], name="s5_glu_bwd", tl=512)


def _merge_math(gp_r, gp_s, o_r, o_s, b_r, b_s):
    return _sig(gp_r + b_r) * o_r + _sig(gp_s + b_s) * o_s


def _merge_fwd(gpre, o_r, o_s, b_r, b_s):
    return _rows(lambda s, r, p, _: ([_merge_math(*r, *p)], []),
                 [(gpre, D_MODEL, 0), (gpre, D_MODEL, 1), o_r, o_s], [b_r, b_s], [(D_MODEL, BF16)], [],
                 name="merge_fwd", tl=256)[0]


def _merge_bwd(gpre, o_r, o_s, dmi, b_r, b_s):
    def fn(s, r, p, _):
        _, vjp = jax.vjp(_merge_math, *r[:4], *p)
        dgr, dgs, dor, dos, dbr, dbs = vjp(r[4])
        return [dgr, dgs, dor, dos], [dbr, dbs]
    return _rows(fn, [(gpre, D_MODEL, 0), (gpre, D_MODEL, 1), o_r, o_s, dmi], [b_r, b_s], [(D_MODEL, BF16)] * 4,
                 [b_r.shape, b_s.shape], name="merge_bwd", tl=256)


def _mid_fwd(x, mixed, g_post, g_pre):
    def fn(s, r, p, _):
        x2 = r[0] + _rms(r[1], p[0])
        return [x2, _rms(x2, p[1])], []
    return _rows(fn, [x, mixed], [g_post, g_pre], [(D_MODEL, F32), (D_MODEL, BF16)], [], name="mid_fwd", tl=256)


def _mid_bwd(x2, mixed, dh2, dx3, g_post, g_pre):
    def fn(s, r, p, _):
        _, vjp1 = jax.vjp(_rms, r[0], p[1])
        dx2, dg_pre = vjp1(r[2])
        dx2 = dx2 + r[3]
        _, vjp2 = jax.vjp(_rms, r[1], p[0])
        dmixed, dg_post = vjp2(dx2)
        return [dx2, dmixed], [dg_post, dg_pre]
    return _rows(fn, [x2, mixed, dh2, dx3], [g_post, g_pre], [(D_MODEL, F32), (D_MODEL, BF16)], [g_post.shape, g_pre.shape],
                 name="mid_bwd", tl=256)


def _final(x2, f, tgt, g_post):
    def fn(s, r, p, _):
        y, vjp = jax.vjp(_rms, r[1], p[0])
        diff = r[0] + y - r[2]
        dx3 = diff * (1.0 / D_MODEL)
        df, dg = vjp(dx3)
        part = 0.5 * jnp.sum(jnp.sum(diff * diff, axis=1, keepdims=True), axis=0, keepdims=True) * (1.0 / D_MODEL)
        return [dx3, df], [dg, jnp.broadcast_to(part, (1, LANES))]
    return _rows(fn, [x2, f, tgt], [g_post], [(D_MODEL, F32), (D_MODEL, BF16)], [g_post.shape, (1, LANES)], name="final", tl=256)


def _conv_taps(z, car):
    row = lax.broadcasted_iota(jnp.int32, z.shape, 0)
    z1 = jnp.where(row == 0, jnp.broadcast_to(car[7:8, :], z.shape), pltpu.roll(z, 1, 0))
    z2 = pltpu.roll(z, 2, 0)
    z2 = jnp.where(row == 0, jnp.broadcast_to(car[6:7, :], z.shape), z2)
    z2 = jnp.where(row == 1, jnp.broadcast_to(car[7:8, :], z.shape), z2)
    return z1, z2


def _conv(z, car, w, b):
    z1, z2 = _conv_taps(z, car)
    return b + w[0:1, :] * z2 + w[1:2, :] * z1 + w[2:3, :] * z, z1, z2


def _conv_fwd(z, conv_w, conv_b):
    length = z.shape[0]
    tl = min(256, length)
    nt = length // tl
    tc = _pick(D_FF, 1536)
    nb = D_FF // tc

    def body(zg_ref, zv_ref, wg_ref, wv_ref, bg_ref, bv_ref, o_ref, cg, cv):
        @pl.when(pl.program_id(1) == 0)
        def _():
            cg[...] = jnp.zeros_like(cg)
            cv[...] = jnp.zeros_like(cv)

        zg, zv = zg_ref[...], zv_ref[...]
        gate, _, _ = _conv(zg, cg[...], wg_ref[...], bg_ref[...])
        val, _, _ = _conv(zv, cv[...], wv_ref[...], bv_ref[...])
        cg[...] = zg[tl - SUBLANES:, :]
        cv[...] = zv[tl - SUBLANES:, :]
        o_ref[...] = (_gelu(gate) * val).astype(o_ref.dtype)

    zspec = lambda off: pl.BlockSpec((tl, tc), lambda j, i: (i, j + off))
    wspec = lambda off, r: pl.BlockSpec((r, tc), lambda j, i: (0, j + off))
    return pl.pallas_call(
        body, grid=(nb, nt),
        in_specs=[zspec(0), zspec(nb), wspec(0, 3), wspec(nb, 3), wspec(0, 1), wspec(nb, 1)],
        out_specs=pl.BlockSpec((tl, tc), lambda j, i: (i, j)), out_shape=SDS((length, D_FF), BF16),
        scratch_shapes=[pltpu.VMEM((SUBLANES, tc), F32)] * 2,
        compiler_params=_params(("arbitrary", "arbitrary")), name="conv_fwd")(z, z, conv_w, conv_w, conv_b, conv_b)


def _conv_bwd(z, dact, conv_w, conv_b):
    length = z.shape[0]
    tl = min(256, length)
    nt = length // tl
    tc = _pick(D_FF, 1536)
    nb = D_FF // tc
    rows_per = tl // SUBLANES

    def half_bwd(dzc, z, z1, z2, w, dcar):
        n = tl
        row = lax.broadcasted_iota(jnp.int32, dzc.shape, 0)
        u1 = jnp.where(row == n - 1, jnp.broadcast_to(dcar[0:1, :], dzc.shape), pltpu.roll(dzc, n - 1, 0))
        u2 = pltpu.roll(dzc, n - 2, 0)
        u2 = jnp.where(row == n - 2, jnp.broadcast_to(dcar[0:1, :], dzc.shape), u2)
        u2 = jnp.where(row == n - 1, jnp.broadcast_to(dcar[1:2, :], dzc.shape), u2)
        dz = w[2:3, :] * dzc + w[1:2, :] * u1 + w[0:1, :] * u2
        dw = jnp.concatenate([jnp.sum(dzc * z2, axis=0, keepdims=True), jnp.sum(dzc * z1, axis=0, keepdims=True),
                              jnp.sum(dzc * z, axis=0, keepdims=True)], axis=0)
        return dz, dw, jnp.sum(dzc, axis=0, keepdims=True)

    def body(zg_ref, zv_ref, pg_ref, pv_ref, da_ref, wg_ref, wv_ref, bg_ref, bv_ref,
             dzg_ref, dzv_ref, dwg_ref, dwv_ref, dbg_ref, dbv_ref, cg, cv):
        step = pl.program_id(1)

        @pl.when(step == 0)
        def _():
            cg[...] = jnp.zeros_like(cg)
            cv[...] = jnp.zeros_like(cv)
            for ref in (dwg_ref, dwv_ref, dbg_ref, dbv_ref):
                ref[...] = jnp.zeros_like(ref)

        is_first_tile = step == nt - 1
        zg, zv = zg_ref[...], zv_ref[...]
        pg = jnp.where(is_first_tile, 0.0, pg_ref[...])
        pv = jnp.where(is_first_tile, 0.0, pv_ref[...])
        wg, wv = wg_ref[...], wv_ref[...]
        gate, zg1, zg2 = _conv(zg, pg, wg, bg_ref[...])
        val, zv1, zv2 = _conv(zv, pv, wv, bv_ref[...])
        act_g, vjp = jax.vjp(_gelu, gate)
        da = da_ref[...]
        dgate = vjp(da * val)[0]
        dval = da * act_g
        dzg, dwg, dbg = half_bwd(dgate, zg, zg1, zg2, wg, cg[...])
        dzv, dwv, dbv = half_bwd(dval, zv, zv1, zv2, wv, cv[...])
        cg[...] = dgate[:SUBLANES, :]
        cv[...] = dval[:SUBLANES, :]
        dzg_ref[...] = dzg.astype(dzg_ref.dtype)
        dzv_ref[...] = dzv.astype(dzv_ref.dtype)
        dwg_ref[...] += dwg
        dwv_ref[...] += dwv
        dbg_ref[...] += dbg
        dbv_ref[...] += dbv

    rmap = lambda i: nt - 1 - i
    zspec = lambda off: pl.BlockSpec((tl, tc), lambda j, i: (rmap(i), j + off))
    pspec = lambda off: pl.BlockSpec((SUBLANES, tc), lambda j, i: (jnp.maximum(rmap(i) * rows_per - 1, 0), j + off))
    wspec = lambda off, r: pl.BlockSpec((r, tc), lambda j, i: (0, j + off))
    out_w = lambda r: pl.BlockSpec((r, tc), lambda j, i: (0, j))
    dzg, dzv, dwg, dwv, dbg, dbv = pl.pallas_call(
        body, grid=(nb, nt),
        in_specs=[zspec(0), zspec(nb), pspec(0), pspec(nb), pl.BlockSpec((tl, tc), lambda j, i: (rmap(i), j)),
                  wspec(0, 3), wspec(nb, 3), wspec(0, 1), wspec(nb, 1)],
        out_specs=[pl.BlockSpec((tl, tc), lambda j, i: (rmap(i), j))] * 2 + [out_w(3), out_w(3), out_w(1), out_w(1)],
        out_shape=[SDS((length, D_FF), BF16)] * 2 + [SDS((3, D_FF), F32)] * 2 + [SDS((1, D_FF), F32)] * 2,
        scratch_shapes=[pltpu.VMEM((SUBLANES, tc), F32)] * 2,
        compiler_params=_params(("arbitrary", "arbitrary")), name="conv_bwd")(z, z, z, z, dact, conv_w, conv_w, conv_b, conv_b)
    return dzg, dzv, jnp.concatenate([dwg, dwv], axis=1), jnp.concatenate([dbg, dbv], axis=1)


def _s5_prep_math(a_re, a_im, ls, b_re, b_im):
    dt = jnp.exp(ls)
    er = jnp.exp(a_re * dt)
    ph = a_im * dt
    abr, abi = er * jnp.cos(ph), er * jnp.sin(ph)
    den = a_re * a_re + a_im * a_im
    nr = abr - 1.0
    cr = (nr * a_re + abi * a_im) / den
    ci = (abi * a_re - nr * a_im) / den
    return abr, abi, cr * b_re - ci * b_im, cr * b_im + ci * b_re


def _s5_prep_fwd(a_re, a_im, ls, b_re, b_im):
    def body(ar, ai, l, br, bi, o1, o2, o3, o4):
        for ref, val in zip((o1, o2, o3, o4), _s5_prep_math(ar[...], ai[...], l[...], br[...], bi[...]), strict=True):
            ref[...] = val
    return pl.pallas_call(body, out_shape=[SDS((1, S5N), F32)] * 2 + [SDS((S5C, S5N), F32)] * 2,
                          name="s5_prep_fwd")(a_re, a_im, ls, b_re, b_im)


def _s5_prep_bwd(a_re, a_im, ls, b_re, b_im, cots, gsel):
    def body(ar, ai, l, br, bi, c1, c2, c3, c4, g_ref, o1, o2, o3, o4, o5):
        _, vjp = jax.vjp(_s5_prep_math, ar[...], ai[...], l[...], br[...], bi[...])
        d_ar, d_ai, d_ls, d_br, d_bi = vjp((c1[...], c2[...], c3[...], c4[...]))
        o1[...] = d_ar
        o2[...] = d_ai
        o3[...] = _hdot(jnp.broadcast_to(d_ls, (SUBLANES, S5N)), g_ref[...])
        o4[...] = d_br
        o5[...] = d_bi
    return pl.pallas_call(body, out_shape=[SDS((1, S5N), F32)] * 2 + [SDS((SUBLANES, LANES), F32)] + [SDS((S5C, S5N), F32)] * 2,
                          name="s5_prep_bwd")(a_re, a_im, ls, b_re, b_im, *cots, gsel)


def _cmul(ar, ai, br, bi):
    return ar * br - ai * bi, ar * bi + ai * br


def _s5_powers(abr, abi):
    shp = (SUBLANES, S5N)
    a1 = (jnp.broadcast_to(abr, shp), jnp.broadcast_to(abi, shp))
    a2 = _cmul(*a1, *a1)
    a4 = _cmul(*a2, *a2)
    row = lax.broadcasted_iota(jnp.int32, shp, 0)
    pr, pi = a1
    cur = a1
    for i in range(1, SUBLANES):
        cur = _cmul(*cur, *a1)
        pr = jnp.where(row == i, cur[0], pr)
        pi = jnp.where(row == i, cur[1], pi)
    return a1, a2, a4, (pr, pi)


def _s5_scan(sre_ref, sim_ref, car_re, car_im, abr, abi, n_rows, reverse):
    a1, a2, a4, (pr, pi) = _s5_powers(abr, abi)
    sgn = -1.0 if reverse else 1.0
    row = lax.broadcasted_iota(jnp.int32, (SUBLANES, S5N), 0)
    if reverse:
        qr, qi = pr, pi
        for i in range(SUBLANES):
            src = SUBLANES - 1 - i
            qr = jnp.where(row == i, jnp.broadcast_to(pr[src:src + 1, :], pr.shape), qr)
            qi = jnp.where(row == i, jnp.broadcast_to(pi[src:src + 1, :], pi.shape), qi)
        pr, pi = qr, qi
    nblk = n_rows // SUBLANES

    def blk(i, carry):
        cr, ci = carry
        b = (nblk - 1 - i) if reverse else i
        sl = pl.ds(pl.multiple_of(b * SUBLANES, SUBLANES), SUBLANES)
        xr, xi = sre_ref[sl, :], sim_ref[sl, :]
        for kk, (er, ei) in ((1, a1), (2, a2), (4, a4)):
            if reverse:
                sr = jnp.where(row < SUBLANES - kk, pltpu.roll(xr, SUBLANES - kk, 0), 0.0)
                si = jnp.where(row < SUBLANES - kk, pltpu.roll(xi, SUBLANES - kk, 0), 0.0)
            else:
                sr = jnp.where(row >= kk, pltpu.roll(xr, kk, 0), 0.0)
                si = jnp.where(row >= kk, pltpu.roll(xi, kk, 0), 0.0)
            dr, di = _cmul(er, sgn * ei, sr, si)
            xr, xi = xr + dr, xi + di
        dr, di = _cmul(pr, sgn * pi, cr, ci)
        xr, xi = xr + dr, xi + di
        sre_ref[sl, :] = xr
        sim_ref[sl, :] = xi
        edge = 0 if reverse else SUBLANES - 1
        return (jnp.broadcast_to(xr[edge:edge + 1, :], xr.shape), jnp.broadcast_to(xi[edge:edge + 1, :], xi.shape))

    cr, ci = lax.fori_loop(0, nblk, blk, (car_re[...], car_im[...]))
    car_re[...] = cr
    car_im[...] = ci


S5_BLK = 4
S5_BN = S5N // S5_BLK


def _split2(x):
    hi = x.astype(BF16)
    return hi, (x - hi.astype(F32)).astype(BF16)


def _dot3(a, b, ca, cb):
    ah, al = _split2(a)
    bh, bl = _split2(b)
    return lax.dot_general(jnp.concatenate([ah, ah, al], axis=ca), jnp.concatenate([bh, bl, bh], axis=cb),
                           (((ca,), (cb,)), ((), ())), preferred_element_type=F32)


def _s5_project_in(u, bblk_ref, sre_ref, sim_ref):
    for j in range(S5_BLK):
        bu = _dot3(u[:, j * LANES:(j + 1) * LANES], bblk_ref[j], 1, 0)
        sre_ref[:, j * S5_BN:(j + 1) * S5_BN] = bu[:, :S5_BN]
        sim_ref[:, j * S5_BN:(j + 1) * S5_BN] = bu[:, S5_BN:]


def _s5_project_out(u, d, cblk_ref, sre_ref, sim_ref):
    ys = []
    for j in range(S5_BLK):
        sl = slice(j * S5_BN, (j + 1) * S5_BN)
        ys.append(_dot3(sre_ref[:, sl], cblk_ref[j, :S5_BN, :], 1, 0) + _dot3(sim_ref[:, sl], cblk_ref[j, S5_BN:, :], 1, 0))
    return jnp.concatenate(ys, axis=1) + d * u


def _s5_fwd(u, bblk, cblk, abr, abi, d):
    length = u.shape[0]
    tl = min(256, length)
    nt = length // tl

    def body(u_ref, b_ref, c_ref, ar_ref, ai_ref, d_ref, ys_ref, kr_ref, ki_ref, sre, sim, car_re, car_im):
        @pl.when(pl.program_id(0) == 0)
        def _():
            car_re[...] = jnp.zeros_like(car_re)
            car_im[...] = jnp.zeros_like(car_im)

        kr_ref[0] = car_re[...]
        ki_ref[0] = car_im[...]
        u_ = u_ref[...]
        _s5_project_in(u_, b_ref, sre, sim)
        _s5_scan(sre, sim, car_re, car_im, ar_ref[...], ai_ref[...], tl, False)
        ys_ref[...] = _gelu(_s5_project_out(u_, d_ref[...], c_ref, sre, sim))

    full = lambda a: pl.BlockSpec(a.shape, lambda i, nd=a.ndim: (0,) * nd)
    chk = pl.BlockSpec((1, SUBLANES, S5N), lambda i: (i, 0, 0))
    return pl.pallas_call(
        body, grid=(nt,), in_specs=[pl.BlockSpec((tl, S5W), lambda i: (i, 0)), full(bblk), full(cblk), full(abr), full(abi), full(d)],
        out_specs=[pl.BlockSpec((tl, S5W), lambda i: (i, 0)), chk, chk],
        out_shape=[SDS((length, S5W), F32), SDS((nt, SUBLANES, S5N), F32), SDS((nt, SUBLANES, S5N), F32)],
        scratch_shapes=[pltpu.VMEM((tl, S5N), F32)] * 2 + [pltpu.VMEM((SUBLANES, S5N), F32)] * 2,
        compiler_params=_params(), name="s5_fwd")(u, bblk, cblk, abr, abi, d)


def _s5_bwd(u, dys_a, dys_b, kre, kim, bblk, cblk, abr, abi, d):
    length = u.shape[0]
    tl = min(256, length)
    nt = length // tl

    def body(u_ref, da_ref, db_ref, kr_ref, ki_ref, b_ref, c_ref, ar_ref, ai_ref, d_ref,
             du_ref, dB_ref, dC_ref, dar_ref, dai_ref, dd_ref, sre, sim, gre, gim, car_re, car_im, dcar_re, dcar_im):
        @pl.when(pl.program_id(0) == 0)
        def _():
            dcar_re[...] = jnp.zeros_like(dcar_re)
            dcar_im[...] = jnp.zeros_like(dcar_im)
            for ref in (dB_ref, dC_ref, dar_ref, dai_ref, dd_ref):
                ref[...] = jnp.zeros_like(ref)

        u_ = u_ref[...]
        abr_, abi_, d_ = ar_ref[...], ai_ref[...], d_ref[...]
        car_re[...] = kr_ref[0]
        car_im[...] = ki_ref[0]
        _s5_project_in(u_, b_ref, sre, sim)
        _s5_scan(sre, sim, car_re, car_im, abr_, abi_, tl, False)
        y = _s5_project_out(u_, d_, c_ref, sre, sim)
        _, vjp = jax.vjp(_gelu, y)
        dy = vjp(da_ref[...] + db_ref[...])[0]
        dd_ref[...] += jnp.sum(dy * u_, axis=0, keepdims=True)
        for j in range(S5_BLK):
            sl = slice(j * S5_BN, (j + 1) * S5_BN)
            dyj = dy[:, j * LANES:(j + 1) * LANES]
            gre[:, sl] = _dot3(dyj, c_ref[j, :S5_BN, :], 1, 1)
            gim[:, sl] = _dot3(dyj, c_ref[j, S5_BN:, :], 1, 1)
            dC_ref[j, :S5_BN, :] += _dot3(sre[:, sl], dyj, 0, 0)
            dC_ref[j, S5_BN:, :] += _dot3(sim[:, sl], dyj, 0, 0)
        _s5_scan(gre, gim, dcar_re, dcar_im, abr_, abi_, tl, True)
        gr, gi = gre[...], gim[...]
        pr = _shift_down(sre[...], kr_ref[0, 0:1, :])
        pi = _shift_down(sim[...], ki_ref[0, 0:1, :])
        dar_ref[...] += jnp.sum(gr * pr + gi * pi, axis=0, keepdims=True)
        dai_ref[...] += jnp.sum(gi * pr - gr * pi, axis=0, keepdims=True)
        dus = []
        for j in range(S5_BLK):
            sl = slice(j * S5_BN, (j + 1) * S5_BN)
            uj = u_[:, j * LANES:(j + 1) * LANES]
            dus.append(_dot3(gre[:, sl], b_ref[j, :, :S5_BN], 1, 1) + _dot3(gim[:, sl], b_ref[j, :, S5_BN:], 1, 1))
            dB_ref[j, :, :S5_BN] += _dot3(uj, gre[:, sl], 0, 0)
            dB_ref[j, :, S5_BN:] += _dot3(uj, gim[:, sl], 0, 0)
        du_ref[...] = (jnp.concatenate(dus, axis=1) + d_ * dy).astype(du_ref.dtype)

    rmap = lambda i: nt - 1 - i
    full = lambda a: pl.BlockSpec(a.shape, lambda i, nd=a.ndim: (0,) * nd)
    row = pl.BlockSpec((tl, S5W), lambda i: (rmap(i), 0))
    chk = pl.BlockSpec((1, SUBLANES, S5N), lambda i: (rmap(i), 0, 0))
    return pl.pallas_call(
        body, grid=(nt,), in_specs=[row, row, row, chk, chk, full(bblk), full(cblk), full(abr), full(abi), full(d)],
        out_specs=[row, full(bblk), full(cblk), full(abr), full(abi), full(d)],
        out_shape=[SDS((length, S5W), BF16), SDS(bblk.shape, F32), SDS(cblk.shape, F32), SDS(abr.shape, F32),
                   SDS(abi.shape, F32), SDS(d.shape, F32)],
        scratch_shapes=[pltpu.VMEM((tl, S5N), F32)] * 4 + [pltpu.VMEM((SUBLANES, S5N), F32)] * 4,
        compiler_params=_params(), name="s5_bwd")(u, dys_a, dys_b, kre, kim, bblk, cblk, abr, abi, d)


def _s5_blockdiag_in(bb):
    t = bb.reshape(S5C, S5_BLK, S5_BN).transpose(1, 0, 2)
    t = jnp.tile(t, (1, SUBLANES, 1))
    mask = (lax.broadcasted_iota(jnp.int32, (LANES, S5_BN), 0) // S5C) == (lax.broadcasted_iota(jnp.int32, (LANES, S5_BN), 1) // S5P)
    return jnp.where(mask[None], t, 0.0)


def _s5_blockdiag_in_t(dblk):
    t = dblk.reshape(S5_BLK, SUBLANES, S5C, SUBLANES, S5P)
    t = jnp.diagonal(t, axis1=1, axis2=3)
    return t.transpose(1, 0, 3, 2).reshape(S5C, S5N)


def _s5_blockdiag_out(c):
    t = c.reshape(S5_BLK, SUBLANES, S5C, S5P).transpose(0, 3, 1, 2).reshape(S5_BLK, S5P, LANES)
    t = jnp.tile(t, (1, SUBLANES, 1))
    mask = (lax.broadcasted_iota(jnp.int32, (S5_BN, LANES), 0) // S5P) == (lax.broadcasted_iota(jnp.int32, (S5_BN, LANES), 1) // S5C)
    return jnp.where(mask[None], t, 0.0)


def _s5_blockdiag_out_t(dblk):
    t = dblk.reshape(S5_BLK, SUBLANES, S5P, SUBLANES, S5C)
    t = jnp.diagonal(t, axis1=1, axis2=3)
    return t.transpose(0, 3, 2, 1).reshape(S5G, S5C, S5P)


def _wkv_consts():
    lane = lax.broadcasted_iota(jnp.int32, (4 * HN, LANES), 1)
    row = lax.broadcasted_iota(jnp.int32, (4 * HN, LANES), 0)
    diag = ((lane % HN) == (row % HN)).astype(F32)
    ones = _block_ones(LANES, HN).astype(BF16)
    return lane % HN, diag, jnp.concatenate([ones, ones], axis=0)


def _wkv_split(p):
    hi = p.astype(BF16)
    lo = (p - hi.astype(F32)).astype(BF16)
    return jnp.concatenate([hi, lo], axis=1)


def _wkv_rowseg(p, ones):
    return jnp.dot(_wkv_split(p), ones, preferred_element_type=F32)


def _wkv_rowseg_all(lhs_ref, ones, out_ref):
    c, ns = lhs_ref.shape[0], lhs_ref.shape[1]
    out_ref[...] = jnp.dot(lhs_ref[...].reshape(c * ns, 2 * LANES), ones, preferred_element_type=F32).reshape(c, ns, LANES)


def _wkv_full(qb_ref, t):
    q = qb_ref[t]
    return jnp.concatenate([jnp.tile(q[:, hp * LANES:(hp + 1) * LANES], (SUBLANES, 1)) for hp in range(4)], axis=0)


def _wkv_colsum8(x):
    return jnp.concatenate([x[hp * HN:(hp + 1) * HN].reshape(SUBLANES, SUBLANES, LANES).sum(axis=0) for hp in range(4)], axis=1)


def _wkv_prebroadcast(src_refs, dst_refs, n):
    rep = (lax.broadcasted_iota(jnp.int32, (n * SUBLANES, n), 0) // SUBLANES
           == lax.broadcasted_iota(jnp.int32, (n * SUBLANES, n), 1)).astype(BF16)
    rep3 = jnp.concatenate([rep, rep, rep], axis=1)
    for s, dref in zip(src_refs, dst_refs, strict=True):
        x = s[...]
        hi = x.astype(BF16)
        rest = x - hi.astype(F32)
        mid = rest.astype(BF16)
        lo = (rest - mid.astype(F32)).astype(BF16)
        dref[...] = jnp.dot(rep3, jnp.concatenate([hi, mid, lo], axis=0), preferred_element_type=F32).reshape(n, SUBLANES, RW)


def _wkv_pair_sums(lhs_ref, out_ref):
    n, ns = out_ref.shape[0], out_ref.shape[1]
    ones_pair = _block_ones(2 * LANES, HN).astype(BF16)
    out_ref[...] = jnp.dot(lhs_ref[0:n].reshape(n * ns, 2 * LANES), ones_pair,
                           preferred_element_type=F32).reshape(n, ns, 2 * LANES)


def _wkv_transpose_out(acc, c, diag):
    lane = lax.broadcasted_iota(jnp.int32, (c, LANES), 1)
    outs = []
    for hp in range(4):
        zp = lax.dot_general(acc[hp * HN:(hp + 1) * HN], diag[:HN], (((0,), (0,)), ((), ())), precision=HI,
                             preferred_element_type=F32)
        outs.append(jnp.where(lane < HN, zp[:c], zp[HN:HN + c]))
    return jnp.concatenate(outs, axis=1)


def _wkv_fwd(r, w, k, v, a, b):
    length = r.shape[0]
    c = min(WKV_CHUNK, length)
    nc = length // c
    ns = 4 * HN

    def body(r_ref, w_ref, k_ref, v_ref, a_ref, b_ref, y_ref, sst_ref, sast_ref, vst_ref, fin_ref,
             s_ref, sa_ref, acc_ref, lhs_ref, ysum_ref, rb, wb, kb, vb, ab, bb):
        @pl.when(pl.program_id(0) == 0)
        def _():
            s_ref[...] = jnp.zeros_like(s_ref)

        _wkv_prebroadcast((r_ref, w_ref, k_ref, v_ref, a_ref, b_ref), (rb, wb, kb, vb, ab, bb), c)
        lane_t, diag, ones = _wkv_consts()

        def fill_v(t, carry):
            lhs_ref[t] = _wkv_split(_wkv_full(vb, t) * diag)
            return carry

        lax.fori_loop(0, c, fill_v, 0)
        _wkv_rowseg_all(lhs_ref, ones, vst_ref)
        sa_ref[...] = _wkv_rowseg(s_ref[...] * _wkv_full(ab, 0), ones)

        def pair(p, carry):
            for half in range(2):
                t = 2 * p + half
                s, sa = s_ref[...], sa_ref[...]
                sst_ref[t] = s
                sast_ref[t] = sa
                s = s * _wkv_full(wb, t) + sa * _wkv_full(bb, t) + vst_ref[t] * _wkv_full(kb, t)
                s_ref[...] = s
                sa_ref[...] = _wkv_rowseg(s * _wkv_full(ab, jnp.minimum(t + 1, c - 1)), ones)
                lhs_ref[p, :, half * LANES:(half + 1) * LANES] = (s * _wkv_full(rb, t)).astype(BF16)
            return carry

        lax.fori_loop(0, c // 2, pair, 0)
        _wkv_pair_sums(lhs_ref, ysum_ref)
        acc_ref[...] = jnp.zeros_like(acc_ref)

        def gather_y(p, carry):
            both = ysum_ref[p]
            acc = jnp.where(lane_t == 2 * p, both[:, :LANES], acc_ref[...])
            acc_ref[...] = jnp.where(lane_t == 2 * p + 1, both[:, LANES:], acc)
            return carry

        lax.fori_loop(0, c // 2, gather_y, 0)
        y_ref[...] = _wkv_transpose_out(acc_ref[...], c, diag)
        fin_ref[...] = s_ref[...]

    row = pl.BlockSpec((c, RW), lambda i: (i, 0))
    st = pl.BlockSpec((c, ns, LANES), lambda i: (i, 0, 0))
    state = pltpu.VMEM((ns, LANES), F32)
    return pl.pallas_call(
        body, grid=(nc,), in_specs=[row] * 6,
        out_specs=[row, st, st, st, pl.BlockSpec((ns, LANES), lambda i: (0, 0))],
        out_shape=[SDS((length, RW), F32)] + [SDS((length, ns, LANES), F32)] * 3 + [SDS((ns, LANES), F32)],
        scratch_shapes=[state] * 3 + [pltpu.VMEM((c, ns, 2 * LANES), BF16), pltpu.VMEM((c // 2, ns, 2 * LANES), F32)]
        + [pltpu.VMEM((c, SUBLANES, RW), F32)] * 6,
        compiler_params=_params(), name="wkv_fwd")(r, w, k, v, a, b)


def _wkv_bwd(r, w, k, a, b, dy, sst, sast, vst, fin):
    length = r.shape[0]
    c = min(WKV_CHUNK, length)
    nc = length // c
    ns = 4 * HN

    def body(r_ref, w_ref, k_ref, a_ref, b_ref, dy_ref, sst_ref, sast_ref, vst_ref, snext_ref, fin_ref,
             dr_ref, dw_ref, dk_ref, da_ref, db_ref, dv_ref,
             ds_ref, cur_ref, acc_ref, dyst, lhs_ref, dvsum_ref, rb, wb, kb, ab, bb, dyb, p_r, p_w, p_k, p_a, p_b):
        @pl.when(pl.program_id(0) == 0)
        def _():
            ds_ref[...] = jnp.zeros_like(ds_ref)

        _wkv_prebroadcast((r_ref, w_ref, k_ref, a_ref, b_ref, dy_ref), (rb, wb, kb, ab, bb, dyb), c)
        lane_t, diag, ones = _wkv_consts()

        def fill(t, carry):
            lhs_ref[t] = _wkv_split(_wkv_full(dyb, t) * diag)
            return carry

        lax.fori_loop(0, c, fill, 0)
        _wkv_rowseg_all(lhs_ref, ones, dyst)
        cur_ref[...] = jnp.where(pl.program_id(0) == 0, fin_ref[...], snext_ref[0])

        def pair(p, carry):
            for half in range(2):
                t = c - 1 - (2 * p + half)
                s_t, s_prev, sa, vv, dyy = cur_ref[...], sst_ref[t], sast_ref[t], vst_ref[t], dyst[t]
                ds = ds_ref[...] + dyy * _wkv_full(rb, t)
                dsa = _wkv_rowseg(ds * _wkv_full(bb, t), ones)
                ds_ref[...] = ds * _wkv_full(wb, t) + dsa * _wkv_full(ab, t)
                cur_ref[...] = s_prev
                lhs_ref[p, :, half * LANES:(half + 1) * LANES] = (ds * _wkv_full(kb, t)).astype(BF16)
                p_r[t] = _wkv_colsum8(s_t * dyy)
                p_k[t] = _wkv_colsum8(ds * vv)
                p_b[t] = _wkv_colsum8(ds * sa)
                p_w[t] = _wkv_colsum8(ds * s_prev)
                p_a[t] = _wkv_colsum8(s_prev * dsa)
            return carry

        lax.fori_loop(0, c // 2, pair, 0)
        _wkv_pair_sums(lhs_ref, dvsum_ref)
        acc_ref[...] = jnp.zeros_like(acc_ref)

        def gather_dv(p, carry):
            both = dvsum_ref[p]
            acc = jnp.where(lane_t == c - 1 - 2 * p, both[:, :LANES], acc_ref[...])
            acc_ref[...] = jnp.where(lane_t == c - 2 - 2 * p, both[:, LANES:], acc)
            return carry

        lax.fori_loop(0, c // 2, gather_dv, 0)
        sel = (lax.broadcasted_iota(jnp.int32, (c, c * SUBLANES), 1) // SUBLANES
               == lax.broadcasted_iota(jnp.int32, (c, c * SUBLANES), 0)).astype(F32)
        for out_ref, part in ((dr_ref, p_r), (dw_ref, p_w), (dk_ref, p_k), (da_ref, p_a), (db_ref, p_b)):
            out_ref[...] = _dot3(sel, part[...].reshape(c * SUBLANES, RW), 1, 0)
        dv_ref[...] = _wkv_transpose_out(acc_ref[...], c, diag)

    rmap = lambda i: nc - 1 - i
    row = pl.BlockSpec((c, RW), lambda i: (rmap(i), 0))
    st = pl.BlockSpec((c, ns, LANES), lambda i: (rmap(i), 0, 0))
    nxt = pl.BlockSpec((1, ns, LANES), lambda i: (jnp.minimum((rmap(i) + 1) * c, length - 1), 0, 0))
    state = pltpu.VMEM((ns, LANES), F32)
    return pl.pallas_call(
        body, grid=(nc,), in_specs=[row] * 6 + [st, st, st, nxt, pl.BlockSpec((ns, LANES), lambda i: (0, 0))],
        out_specs=[row] * 6, out_shape=[SDS((length, RW), F32)] * 6,
        scratch_shapes=[state] * 3 + [pltpu.VMEM((c, ns, LANES), F32), pltpu.VMEM((c, ns, 2 * LANES), BF16),
                                       pltpu.VMEM((c // 2, ns, 2 * LANES), F32)]
        + [pltpu.VMEM((c, SUBLANES, RW), F32)] * 11,
        compiler_params=_params(), name="wkv_bwd")(r, w, k, a, b, dy, sst, sast, vst, sst, fin)


def _my_place():
    return lax.axis_index("x"), lax.axis_index("y"), lax.axis_index("c")


def _peer(x, y, c, k):
    return (x ^ ((k >> 2) & 1), y ^ ((k >> 1) & 1), c ^ (k & 1))


def _all_gather(shard, name):
    rows = shard.shape[0]

    def body(in_ref, out_ref, send_sems, recv_sems, local_sem):
        x, y, c = _my_place()
        me = 4 * x + 2 * y + c
        mine = pltpu.make_async_copy(in_ref, out_ref.at[me], local_sem)
        mine.start()
        sends = []
        for k in range(1, N_DEV):
            cp = pltpu.make_async_remote_copy(src_ref=in_ref, dst_ref=out_ref.at[me], send_sem=send_sems.at[k - 1],
                                              recv_sem=recv_sems.at[k - 1], device_id=_peer(x, y, c, k), device_id_type=MESH)
            cp.start()
            sends.append(cp)
        for k in range(1, N_DEV):
            px, py, pc = _peer(x, y, c, k)
            pltpu.make_async_remote_copy(src_ref=in_ref, dst_ref=out_ref.at[4 * px + 2 * py + pc], send_sem=send_sems.at[k - 1],
                                         recv_sem=recv_sems.at[k - 1], device_id=(px, py, pc), device_id_type=MESH).wait_recv()
        for cp in sends:
            cp.wait_send()
        mine.wait()

    return pl.pallas_call(
        body, out_shape=SDS((N_DEV, rows, LANES), shard.dtype),
        in_specs=[pl.BlockSpec(memory_space=pl.ANY)], out_specs=pl.BlockSpec(memory_space=pl.ANY),
        scratch_shapes=[pltpu.SemaphoreType.DMA((N_DEV - 1,)), pltpu.SemaphoreType.DMA((N_DEV - 1,)), pltpu.SemaphoreType.DMA],
        name=name)(shard)


def _all_to_all(chunks, name):
    rows = chunks.shape[1]

    def body(in_ref, out_ref, send_sems, recv_sems, local_sem):
        x, y, c = _my_place()
        me = 4 * x + 2 * y + c
        mine = pltpu.make_async_copy(in_ref.at[me], out_ref.at[me], local_sem)
        mine.start()
        sends = []
        for k in range(1, N_DEV):
            px, py, pc = _peer(x, y, c, k)
            cp = pltpu.make_async_remote_copy(src_ref=in_ref.at[4 * px + 2 * py + pc], dst_ref=out_ref.at[me],
                                              send_sem=send_sems.at[k - 1], recv_sem=recv_sems.at[k - 1],
                                              device_id=(px, py, pc), device_id_type=MESH)
            cp.start()
            sends.append(cp)
        for k in range(1, N_DEV):
            px, py, pc = _peer(x, y, c, k)
            pltpu.make_async_remote_copy(src_ref=in_ref.at[me], dst_ref=out_ref.at[4 * px + 2 * py + pc], send_sem=send_sems.at[k - 1],
                                         recv_sem=recv_sems.at[k - 1], device_id=(px, py, pc), device_id_type=MESH).wait_recv()
        for cp in sends:
            cp.wait_send()
        mine.wait()

    return pl.pallas_call(
        body, out_shape=SDS((N_DEV, rows, LANES), chunks.dtype),
        in_specs=[pl.BlockSpec(memory_space=pl.ANY)], out_specs=pl.BlockSpec(memory_space=pl.ANY),
        scratch_shapes=[pltpu.SemaphoreType.DMA((N_DEV - 1,)), pltpu.SemaphoreType.DMA((N_DEV - 1,)), pltpu.SemaphoreType.DMA],
        name=name)(chunks)


_HBM_SPEC = pl.BlockSpec(memory_space=pltpu.HBM)
_SEM_SPEC = pl.BlockSpec(memory_space=pltpu.SEMAPHORE)
_DATAFLOW = pltpu.SideEffectType.DATAFLOW_SIDE_EFFECTING


def _exchange_start(src, per_peer, name):
    rows = src.shape[-2]
    land = jnp.zeros((N_DEV, rows, LANES), src.dtype)

    def body(src_ref, land_ref, send_sems, recv_sems, src_thru, land_thru, token):
        x, y, c = _my_place()
        me = 4 * x + 2 * y + c
        for k in range(1, N_DEV):
            px, py, pc = _peer(x, y, c, k)
            mine = src_ref.at[4 * px + 2 * py + pc] if per_peer else src_ref
            pltpu.make_async_remote_copy(src_ref=mine, dst_ref=land_ref.at[me], send_sem=send_sems.at[k - 1],
                                         recv_sem=recv_sems.at[k - 1], device_id=(px, py, pc), device_id_type=MESH).start()
        token[...] = jnp.zeros_like(token)

    return pl.pallas_call(
        body, name=name,
        out_shape=(pltpu.SemaphoreType.DMA((N_DEV - 1,)), pltpu.SemaphoreType.DMA((N_DEV - 1,)), pltpu.HBM(src.shape, src.dtype),
                   pltpu.HBM(land.shape, land.dtype), SDS((SUBLANES, LANES), F32)),
        in_specs=(_HBM_SPEC, _HBM_SPEC),
        out_specs=(_SEM_SPEC, _SEM_SPEC, _HBM_SPEC, _HBM_SPEC, pl.BlockSpec(memory_space=pltpu.VMEM)),
        input_output_aliases={0: 2, 1: 3}, compiler_params=pltpu.CompilerParams(has_side_effects=_DATAFLOW),
    )(pltpu.with_memory_space_constraint(src, pltpu.HBM), pltpu.with_memory_space_constraint(land, pltpu.HBM))


def _exchange_wait(started, after, per_peer, name):
    send_sems, recv_sems, src_thru, land_thru, _ = started

    def body(src_ref, land_ref, send_sems, recv_sems, after_ref, src_dead, got_ref):
        x, y, c = _my_place()
        me = 4 * x + 2 * y + c
        for k in range(1, N_DEV):
            px, py, pc = _peer(x, y, c, k)
            mine = src_ref.at[me] if per_peer else src_ref
            copy = pltpu.make_async_remote_copy(src_ref=mine, dst_ref=land_ref.at[4 * px + 2 * py + pc], send_sem=send_sems.at[k - 1],
                                                recv_sem=recv_sems.at[k - 1], device_id=(px, py, pc), device_id_type=MESH)
            copy.wait_send()
            copy.wait_recv()

    return pl.pallas_call(
        body, name=name, out_shape=(pltpu.HBM(src_thru.shape, src_thru.dtype), pltpu.HBM(land_thru.shape, land_thru.dtype)),
        in_specs=(_HBM_SPEC, _HBM_SPEC, _SEM_SPEC, _SEM_SPEC, pl.BlockSpec(memory_space=pl.ANY)),
        out_specs=(_HBM_SPEC, _HBM_SPEC), input_output_aliases={0: 0, 1: 1},
        compiler_params=pltpu.CompilerParams(has_side_effects=_DATAFLOW),
    )(src_thru, land_thru, send_sems, recv_sems, after)[1]


def _sum_parts(parts, name, own=None):
    rows = parts.shape[1]

    def body(*refs):
        p_ref, g_out = refs[0], refs[-1]
        g = p_ref[0].astype(F32)
        for s in range(1, N_DEV):
            g = g + p_ref[s].astype(F32)
        if own is not None:
            g = g + refs[1][...].astype(F32)
        g_out[...] = g

    blk = pl.BlockSpec((PACK_ROWS, LANES), lambda i: (i, 0))
    extra = [] if own is None else [own]
    return pl.pallas_call(
        body, grid=(rows // PACK_ROWS,),
        in_specs=[pl.BlockSpec((N_DEV, PACK_ROWS, LANES), lambda i: (0, i, 0))] + [blk] * len(extra),
        out_specs=blk, out_shape=SDS((rows, LANES), F32),
        compiler_params=_params(("parallel",)), name=name)(parts, *extra)


def _adamw(g, wgt, m, v):
    rows = wgt.shape[0]

    def body(g_ref, w_ref, m_ref, v_ref, d_out, m_out, v_out):
        g_ = g_ref[...]
        m_new = ADAM_B1 * m_ref[...] + (1.0 - ADAM_B1) * g_
        v_new = ADAM_B2 * v_ref[...] + (1.0 - ADAM_B2) * (g_ * g_)
        m_hat = m_new / (1.0 - ADAM_B1 ** ADAM_STEP)
        v_hat = v_new / (1.0 - ADAM_B2 ** ADAM_STEP)
        d_out[...] = -ADAM_LR * (m_hat / (jnp.sqrt(v_hat) + ADAM_EPS) + ADAM_WD * w_ref[...])
        m_out[...] = m_new
        v_out[...] = v_new

    blk = pl.BlockSpec((PACK_ROWS, LANES), lambda i: (i, 0))
    return pl.pallas_call(
        body, grid=(rows // PACK_ROWS,), in_specs=[blk] * 4, out_specs=[blk] * 3, out_shape=[SDS((rows, LANES), F32)] * 3,
        compiler_params=_params(("parallel",)), name="adamw")(g, wgt, m, v)


PACK_ALIGN = 2 * SUBLANES * LANES
PACK_ROWS = 512

ROW_SHARDED = ("s5_w_glu", "w_out", "ffn_w_down")
TRANSPOSED = ("w_in", "ffn_w_up", "w_branch_rwkv", "w_branch_s5")
SMALL_SHARDED = ("rwkv_w2", "rwkv_a2", "rwkv_g2", "ffn_conv_w")
SHARDED = ROW_SHARDED + TRANSPOSED + SMALL_SHARDED
F32_GATHER = ("ffn_conv_w",)
REPLICATED =("norm_mix_pre", "norm_mix_post", "norm_ffn_pre", "norm_ffn_post", "b_gate", "rwkv_shift_mu", "rwkv_w0",
              "rwkv_a0", "rwkv_k_k", "rwkv_k_a", "rwkv_r_k", "rwkv_lnx_w", "rwkv_lnx_b", "s5_a_re", "s5_a_im", "s5_b_re",
              "s5_b_im", "s5_c_re", "s5_c_im", "s5_d", "s5_log_step", "s5_b_glu", "ffn_conv_b")
WEIGHTS = ("norm_mix_pre", "norm_mix_post", "norm_ffn_pre", "norm_ffn_post", "w_in", "b_gate", "rwkv_shift_mu", "rwkv_w0",
           "rwkv_w2", "rwkv_a0", "rwkv_a2", "rwkv_g2", "rwkv_k_k", "rwkv_k_a", "rwkv_r_k", "rwkv_lnx_w", "rwkv_lnx_b",
           "s5_a_re", "s5_a_im", "s5_b_re", "s5_b_im", "s5_c_re", "s5_c_im", "s5_d", "s5_log_step", "s5_w_glu", "s5_b_glu",
           "w_branch_rwkv", "w_branch_s5", "w_out", "ffn_w_up", "ffn_conv_w", "ffn_conv_b", "ffn_w_down")


def _pad_flat(a):
    flat = a.reshape(-1)
    pad = (-flat.shape[0]) % PACK_ALIGN
    return jnp.pad(flat, (0, pad)) if pad else flat


def _pad_cols(a, mult):
    pad = (-a.shape[1]) % mult
    return jnp.pad(a, ((0, 0), (0, pad))) if pad else a


def _pack(pieces):
    flat = jnp.concatenate([_pad_flat(p) for p in pieces])
    pad = (-flat.shape[0]) % (PACK_ROWS * LANES)
    return (jnp.pad(flat, (0, pad)) if pad else flat).reshape(-1, LANES)


def _unpack(buf, shapes):
    lead = buf.shape[:-2]
    flat = buf.reshape(*lead, -1)
    out, off = [], 0
    for s in shapes:
        n = math.prod(s)
        out.append(flat[..., off:off + n].reshape(*lead, *s))
        off += n + (-n) % PACK_ALIGN
    return out


def _rows_to_full(stack):
    return stack.reshape(-1, stack.shape[2])


def _cols_to_full(stack):
    return stack.transpose(1, 0, 2).reshape(stack.shape[1], -1)


EARLY_GRADS = ("ffn_w_down", "ffn_w_up", "w_out")


def _local_step(x, tgt, wt, get_late, send_early):
    wt = dict(wt)
    o512 = _block_ones(RW, HN)
    w_in = wt["w_in"]
    w_rw, w_u, w_g = w_in[:N_RWKV], w_in[N_RWKV:N_RWKV + S5W], w_in[N_RWKV + S5W:]
    b_gate = wt["b_gate"]
    b_r, b_s = b_gate[:, :D_MODEL], b_gate[:, D_MODEL:]
    zpad = lambda a, lo, n: jnp.pad(a.astype(F32), ((lo, n - lo - a.shape[0]), (0, 0)))
    w2p, a2p, g2p = zpad(wt["rwkv_w2"], 0, 256), zpad(wt["rwkv_a2"], 64, 256), zpad(wt["rwkv_g2"], 128, 256)
    pre_small = [wt["rwkv_w0"], wt["rwkv_a0"], wt["rwkv_k_k"], wt["rwkv_k_a"], w2p, a2p, g2p]
    post_prm = [wt["rwkv_lnx_w"], wt["rwkv_lnx_b"], wt["rwkv_r_k"].reshape(1, RW)]
    mu = wt["rwkv_shift_mu"]

    a_re, a_im = wt["s5_a_re"].reshape(1, S5N), wt["s5_a_im"].reshape(1, S5N)
    ls = jnp.repeat(wt["s5_log_step"].reshape(S5G), S5P).reshape(1, S5N)
    b_re_t = wt["s5_b_re"].reshape(S5N, S5C).T
    b_im_t = wt["s5_b_im"].reshape(S5N, S5C).T
    c_re, c_im = wt["s5_c_re"].reshape(S5G, S5C, S5P), wt["s5_c_im"].reshape(S5G, S5C, S5P)
    abr, abi, bbr, bbi = _s5_prep_fwd(a_re, a_im, ls, b_re_t, b_im_t)
    bblk = jnp.concatenate([_s5_blockdiag_in(bbr), _s5_blockdiag_in(bbi)], axis=2)
    cblk = jnp.concatenate([_s5_blockdiag_out(c_re), -_s5_blockdiag_out(c_im)], axis=1)
    s5_d = wt["s5_d"]

    h1 = _rms_fwd(x, wt["norm_mix_pre"], "rms1_fwd")
    p_rw = _mm(h1, w_rw, tb=True, name="mm_proj_rwkv")
    u = _mm(h1, w_u, tb=True, name="mm_proj_s5")
    gpre = _mm(h1, w_g, tb=True, name="mm_proj_gate")
    r, decay, k2, v, aa, bb, g = _rwkv_pre_fwd(p_rw, mu, pre_small, o512)
    y, sst, sast, vst, s_fin = _wkv_fwd(r, decay, k2, v, aa, bb)
    wt.update(get_late(y))
    y_r = _rwkv_post_fwd(y, r, k2, v, g, post_prm, o512)
    o_r = _mm(y_r, wt["w_branch_rwkv"], tb=True, name="mm_branch_rwkv")
    ys, kre, kim = _s5_fwd(u, bblk, cblk, abr, abi, s5_d)
    t_glu = _mm(ys, wt["s5_w_glu"], name="mm_glu")
    out5 = _glu_fwd(ys, t_glu, wt["s5_b_glu"])
    o_s = _mm(out5, wt["w_branch_s5"], tb=True, name="mm_branch_s5")
    mi = _merge_fwd(gpre, o_r, o_s, b_r, b_s)
    mixed = _mm(mi, wt["w_out"], name="mm_out")
    x2, h2 = _mid_fwd(x, mixed, wt["norm_mix_post"], wt["norm_ffn_pre"])
    z = _mm(h2, wt["ffn_w_up"], tb=True, name="mm_up")
    act = _conv_fwd(z, wt["ffn_conv_w"], wt["ffn_conv_b"])
    f = _mm(act, wt["ffn_w_down"], name="mm_down")
    dx3, df, d_norm_ffn_post, loss_part = _final(x2, f, tgt, wt["norm_ffn_post"])

    gr = {"norm_ffn_post": d_norm_ffn_post}
    dact = _mm(df, wt["ffn_w_down"], tb=True, name="mm_down_dx")
    gr["ffn_w_down"] = _mm(act, df, ta=True, name="mm_down_dw")
    dzg, dzv, gr["ffn_conv_w"], gr["ffn_conv_b"] = _conv_bwd(z, dact, wt["ffn_conv_w"], wt["ffn_conv_b"])
    dz = jnp.concatenate([dzg, dzv], axis=1)
    dh2 = _mm(dz, wt["ffn_w_up"], name="mm_up_dx")
    gr["ffn_w_up"] = _mm(dz, h2, ta=True, name="mm_up_dw")
    dx2, dmixed, gr["norm_mix_post"], gr["norm_ffn_pre"] = _mid_bwd(x2, mixed, dh2, dx3, wt["norm_mix_post"], wt["norm_ffn_pre"])
    dmi = _mm(dmixed, wt["w_out"], tb=True, name="mm_out_dx")
    gr["w_out"] = _mm(mi, dmixed, ta=True, name="mm_out_dw")
    token = send_early({n: gr[n] for n in EARLY_GRADS})
    b_r = b_r + jnp.tile(token[0:1, :], (1, D_MODEL // LANES))
    dgp_r, dgp_s, do_r, do_s, db_r, db_s = _merge_bwd(gpre, o_r, o_s, dmi, b_r, b_s)
    gr["b_gate"] = jnp.concatenate([db_r, db_s], axis=1)
    dout5 = _mm(do_s, wt["w_branch_s5"], name="mm_branch_s5_dx")
    gr["w_branch_s5"] = _mm(do_s, out5, ta=True, name="mm_branch_s5_dw")
    dys_a, dt_glu, gr["s5_b_glu"] = _glu_bwd(ys, t_glu, dout5, wt["s5_b_glu"])
    dys_b = _mm(dt_glu, wt["s5_w_glu"], tb=True, name="mm_glu_dx")
    gr["s5_w_glu"] = _mm(ys, dt_glu, ta=True, name="mm_glu_dw")
    du, dbblk, dcblk, dabr, dabi, gr["s5_d"] = _s5_bwd(u, dys_a, dys_b, kre, kim, bblk, cblk, abr, abi, s5_d)
    gr["s5_c_re"] = _s5_blockdiag_out_t(dcblk[:, :S5_BN, :]).reshape(wt["s5_c_re"].shape)
    gr["s5_c_im"] = (-_s5_blockdiag_out_t(dcblk[:, S5_BN:, :])).reshape(wt["s5_c_im"].shape)
    dbbr, dbbi = _s5_blockdiag_in_t(dbblk[:, :, :S5_BN]), _s5_blockdiag_in_t(dbblk[:, :, S5_BN:])
    gsel = (lax.broadcasted_iota(jnp.int32, (S5N, LANES), 0) // S5P == lax.broadcasted_iota(jnp.int32, (S5N, LANES), 1)).astype(F32)
    d_are, d_aim, d_ls, d_bre_t, d_bim_t = _s5_prep_bwd(a_re, a_im, ls, b_re_t, b_im_t, (dabr, dabi, dbbr, dbbi), gsel)
    gr["s5_a_re"] = d_are.reshape(wt["s5_a_re"].shape)
    gr["s5_a_im"] = d_aim.reshape(wt["s5_a_im"].shape)
    gr["s5_log_step"] = d_ls[0:1, :S5G]
    gr["s5_b_re"] = d_bre_t.T.reshape(wt["s5_b_re"].shape)
    gr["s5_b_im"] = d_bim_t.T.reshape(wt["s5_b_im"].shape)
    dy_r = _mm(do_r, wt["w_branch_rwkv"], name="mm_branch_rwkv_dx")
    gr["w_branch_rwkv"] = _mm(do_r, y_r, ta=True, name="mm_branch_rwkv_dw")
    dy, dr1, dk1, dv1, dg, gr["rwkv_lnx_w"], gr["rwkv_lnx_b"], d_rk = _rwkv_post_bwd(y, r, k2, v, g, dy_r, post_prm, o512)
    gr["rwkv_r_k"] = d_rk.reshape(wt["rwkv_r_k"].shape)
    dr2, ddecay, dk2, daa, dbb, dv2 = _wkv_bwd(r, decay, k2, aa, bb, dy, sst, sast, vst, s_fin)
    cots = [dr1 + dr2, ddecay, dk1 + dk2, dv1 + dv2, daa, dbb, dg]
    dp_rw, gr["rwkv_shift_mu"], gr["rwkv_w0"], gr["rwkv_a0"], gr["rwkv_k_k"], gr["rwkv_k_a"], dw2p, da2p, dg2p = \
        _rwkv_pre_bwd(p_rw, cots, mu, pre_small, o512)
    gr["rwkv_w2"], gr["rwkv_a2"], gr["rwkv_g2"] = dw2p[:64], da2p[64:128], dg2p[128:]
    dproj = jnp.concatenate([dp_rw, du, dgp_r, dgp_s], axis=1)
    dh1 = _mm(dproj, w_in, name="mm_proj_dx")
    gr["w_in"] = _mm(dproj, h1, ta=True, name="mm_proj_dw")
    dx, gr["norm_mix_pre"] = _rms_bwd(x, wt["norm_mix_pre"], dh1, dx2, "rms1_bwd")
    return loss_part[0, 0], dx, gr


def kernel(x, norm_mix_pre, norm_mix_post, norm_ffn_pre, norm_ffn_post, w_in, b_gate, rwkv_shift_mu, rwkv_w0, rwkv_w2, rwkv_a0, rwkv_a2, rwkv_g2, rwkv_k_k, rwkv_k_a, rwkv_r_k, rwkv_lnx_w, rwkv_lnx_b, s5_a_re, s5_a_im, s5_b_re, s5_b_im, s5_c_re, s5_c_im, s5_d, s5_log_step, s5_w_glu, s5_b_glu, w_branch_rwkv, w_branch_s5, w_out, ffn_w_up, ffn_conv_w, ffn_conv_b, ffn_w_down, loss_target, m_norm_mix_pre, m_norm_mix_post, m_norm_ffn_pre, m_norm_ffn_post, m_w_in, m_b_gate, m_rwkv_shift_mu, m_rwkv_w0, m_rwkv_w2, m_rwkv_a0, m_rwkv_a2, m_rwkv_g2, m_rwkv_k_k, m_rwkv_k_a, m_rwkv_r_k, m_rwkv_lnx_w, m_rwkv_lnx_b, m_s5_a_re, m_s5_a_im, m_s5_b_re, m_s5_b_im, m_s5_c_re, m_s5_c_im, m_s5_d, m_s5_log_step, m_s5_w_glu, m_s5_b_glu, m_w_branch_rwkv, m_w_branch_s5, m_w_out, m_ffn_w_up, m_ffn_conv_w, m_ffn_conv_b, m_ffn_w_down, v_norm_mix_pre, v_norm_mix_post, v_norm_ffn_pre, v_norm_ffn_post, v_w_in, v_b_gate, v_rwkv_shift_mu, v_rwkv_w0, v_rwkv_w2, v_rwkv_a0, v_rwkv_a2, v_rwkv_g2, v_rwkv_k_k, v_rwkv_k_a, v_rwkv_r_k, v_rwkv_lnx_w, v_rwkv_lnx_b, v_s5_a_re, v_s5_a_im, v_s5_b_re, v_s5_b_im, v_s5_c_re, v_s5_c_im, v_s5_d, v_s5_log_step, v_s5_w_glu, v_s5_b_glu, v_w_branch_rwkv, v_w_branch_s5, v_w_out, v_ffn_w_up, v_ffn_conv_w, v_ffn_conv_b, v_ffn_w_down):
    args = dict(locals())
    wgt = {n: args[n] for n in WEIGHTS}
    mom = {n: args["m_" + n] for n in WEIGHTS}
    var = {n: args["v_" + n] for n in WEIGHTS}
    me = 4 * lax.axis_index("x") + 2 * lax.axis_index("y") + lax.axis_index("c")
    shard_shapes = {n: wgt[n].shape[1:] for n in SHARDED}
    sent_shapes = {n: (shard_shapes[n][::-1] if n in TRANSPOSED else shard_shapes[n]) for n in SHARDED}
    sent = lambda n, a: a.T if n in TRANSPOSED else a

    unshard = lambda n, blocks: _cols_to_full(blocks) if n in SMALL_SHARDED else _rows_to_full(blocks)

    first = ("w_in", "rwkv_w2", "rwkv_a2", "rwkv_g2")
    late = tuple(n for n in SHARDED if n not in first and n not in F32_GATHER)
    late_payload = _pack([sent(n, wgt[n][0].astype(BF16)) for n in late])
    late_started = _exchange_start(late_payload, False, "gather_late_start")
    got_first = _unpack(_all_gather(_pack([sent(n, wgt[n][0].astype(BF16)) for n in first]), "all_gather_weights"),
                        [sent_shapes[n] for n in first])
    got_f32 = _unpack(_all_gather(_pack([wgt[n][0] for n in F32_GATHER]), "all_gather_taps"),
                      [sent_shapes[n] for n in F32_GATHER])
    wt = {n: wgt[n] for n in REPLICATED}
    for n, blocks in zip(first + F32_GATHER, got_first + got_f32, strict=True):
        wt[n] = unshard(n, blocks)

    def get_late(after):
        land = _exchange_wait(late_started, after, False, "gather_late_wait")
        blocks = _unpack(lax.dynamic_update_index_in_dim(land, late_payload, me, 0), [sent_shapes[n] for n in late])
        return {n: unshard(n, b_) for n, b_ in zip(late, blocks, strict=True)}

    def pack_chunks(names, grads_):
        pieces = [_pad_cols(grads_[n].reshape(N_DEV, -1).astype(BF16), PACK_ALIGN) for n in names]
        return _pad_cols(jnp.concatenate(pieces, axis=1), PACK_ROWS * LANES).reshape(N_DEV, -1, LANES)

    early = {}

    def send_early(grads_):
        early["chunks"] = pack_chunks(EARLY_GRADS, grads_)
        early["started"] = _exchange_start(early["chunks"], True, "exchange_early_start")
        return early["started"][4]

    loss_part, dx, gr = _local_step(x[0], loss_target[0], wt, get_late, send_early)

    rest = tuple(n for n in ROW_SHARDED + TRANSPOSED if n not in EARLY_GRADS)
    whole = SMALL_SHARDED + REPLICATED
    early_land = _exchange_wait(early["started"], dx, True, "exchange_early_wait")
    early_sum = _sum_parts(early_land, "sum_early_grads", own=lax.dynamic_index_in_dim(early["chunks"], me, 0, keepdims=False))
    rest_sum = _sum_parts(_all_to_all(pack_chunks(rest, gr), "exchange_grads"), "sum_grads")
    small_sum = _sum_parts(_all_gather(_pack([gr[n] for n in whole]), "gather_small_grads"), "sum_small_grads")
    got = dict(zip(EARLY_GRADS + rest + whole,
                   _unpack(early_sum, [sent_shapes[n] for n in EARLY_GRADS]) + _unpack(rest_sum, [sent_shapes[n] for n in rest])
                   + _unpack(small_sum, [gr[n].shape for n in whole]), strict=True))
    grads = {}
    for n in WEIGHTS:
        if n in TRANSPOSED:
            grads[n] = got[n].T
        elif n in SMALL_SHARDED:
            cols = shard_shapes[n][1]
            grads[n] = lax.dynamic_slice_in_dim(got[n], me * cols, cols, axis=1)
        else:
            grads[n] = got[n]
        grads[n] = grads[n].reshape(wgt[n].shape)

    pack_local = lambda src: _pack([src[n].reshape(-1) for n in WEIGHTS])
    outs = _adamw(pack_local(grads), pack_local(wgt), pack_local(mom), pack_local(var))
    shapes = [wgt[n].shape for n in WEIGHTS]
    loss = lax.psum(loss_part, AXES)
    return (loss, dx[None], *[grads[n] for n in WEIGHTS], *_unpack(outs[0], shapes), *_unpack(outs[1], shapes),
            *_unpack(outs[2], shapes))
```

```python
import functools
import math

import jax
import jax.numpy as jnp
from jax import lax
from jax.experimental import pallas as pl
from jax.experimental.pallas import tpu as pltpu

F32 = jnp.float32
BF16 = jnp.bfloat16
SDS = jax.ShapeDtypeStruct
HI = lax.Precision.HIGHEST
MESH = pl.DeviceIdType.MESH
AXES = ("x", "y", "c")
N_DEV = 8

D_MODEL = 1024
RW = 512
HN = 64
N_RWKV = 1792
S5W = 512
S5G = 32
S5C = 16
S5P = 64
S5N = S5G * S5P
D_FF = 2816
NORM_EPS = 1e-6
LNX_EPS = 64e-5

ADAM_LR = 0.001
ADAM_B1 = 0.9
ADAM_B2 = 0.999
ADAM_EPS = 1e-08
ADAM_WD = 0.01
ADAM_STEP = 10

LANES = 128
SUBLANES = 8
VMEM_LIMIT = 56 * 1024 * 1024
WKV_CHUNK = 32


def _params(sem=("arbitrary",)):
    return pltpu.CompilerParams(dimension_semantics=sem, vmem_limit_bytes=VMEM_LIMIT)


def _pick(n, cap):
    best = None
    for t in range(LANES, min(n, cap) + 1, LANES):
        if n % t == 0:
            best = t
    return best or n


def _mm(a, b, *, ta=False, tb=False, out_dtype=F32, name):
    m = a.shape[1] if ta else a.shape[0]
    k = a.shape[0] if ta else a.shape[1]
    n = b.shape[0] if tb else b.shape[1]
    assert (b.shape[1] if tb else b.shape[0]) == k
    tm, tn, tk = _pick(m, 1024), _pick(n, 2304), _pick(k, 512)
    nk = k // tk
    dims = (((0 if ta else 1,), (1 if tb else 0,)), ((), ()))

    def body(a_ref, b_ref, o_ref, acc_ref):
        kk = pl.program_id(2)

        @pl.when(kk == 0)
        def _():
            acc_ref[...] = jnp.zeros_like(acc_ref)

        acc_ref[...] += lax.dot_general(a_ref[...].astype(BF16), b_ref[...].astype(BF16), dims,
                                        preferred_element_type=F32)

        @pl.when(kk == nk - 1)
        def _():
            o_ref[...] = acc_ref[...].astype(o_ref.dtype)

    a_spec = pl.BlockSpec((tk, tm), lambda i, j, kk: (kk, i)) if ta else pl.BlockSpec((tm, tk), lambda i, j, kk: (i, kk))
    b_spec = pl.BlockSpec((tn, tk), lambda i, j, kk: (j, kk)) if tb else pl.BlockSpec((tk, tn), lambda i, j, kk: (kk, j))
    return pl.pallas_call(
        body, grid=(m // tm, n // tn, nk), in_specs=[a_spec, b_spec],
        out_specs=pl.BlockSpec((tm, tn), lambda i, j, kk: (i, j)),
        out_shape=SDS((m, n), out_dtype), scratch_shapes=[pltpu.VMEM((tm, tn), F32)],
        compiler_params=_params(("parallel", "parallel", "arbitrary")), name=name)(a, b)


def _rows(fn, rows, params, out_rows, out_accs, *, name, tl, reverse=False, scratch=()):
    first = rows[0][0] if isinstance(rows[0], tuple) else rows[0]
    length = first.shape[0]
    tl = min(tl, length)
    nt = length // tl
    rmap = (lambda i: nt - 1 - i) if reverse else (lambda i: i)
    specs, arrs = [], []
    for r in rows:
        arr, wdt, cb = r if isinstance(r, tuple) else (r, r.shape[1], 0)
        specs.append(pl.BlockSpec((tl, wdt), lambda i, cb=cb: (rmap(i), cb)))
        arrs.append(arr)
    for p in params:
        specs.append(pl.BlockSpec(p.shape, lambda i, nd=p.ndim: (0,) * nd))
        arrs.append(p)
    out_shape = [SDS((length, c), dt) for c, dt in out_rows] + [SDS(s, F32) for s in out_accs]
    out_specs = [pl.BlockSpec((tl, c), lambda i: (rmap(i), 0)) for c, _ in out_rows]
    out_specs += [pl.BlockSpec(s, lambda i, nd=len(s): (0,) * nd) for s in out_accs]
    nr, npar, nor, noa = len(rows), len(params), len(out_rows), len(out_accs)

    def body(*refs):
        rin, pin = refs[:nr], refs[nr:nr + npar]
        rout = refs[nr + npar:nr + npar + nor]
        aout = refs[nr + npar + nor:nr + npar + nor + noa]
        scr = refs[nr + npar + nor + noa:]
        step = pl.program_id(0)
        outs_r, outs_a = fn(step, [r[...] for r in rin], [p[...] for p in pin], scr)
        for ref, val in zip(rout, outs_r, strict=True):
            ref[...] = val.astype(ref.dtype)

        @pl.when(step == 0)
        def _():
            for ref in aout:
                ref[...] = jnp.zeros_like(ref)

        for ref, val in zip(aout, outs_a, strict=True):
            ref[...] += val.astype(F32)

    res = pl.pallas_call(body, grid=(nt,), in_specs=specs, out_specs=out_specs, out_shape=out_shape,
                         scratch_shapes=list(scratch), compiler_params=_params(), name=name)(*arrs)
    return list(res)


def _rms(x, g):
    return x * lax.rsqrt(jnp.mean(x * x, axis=-1, keepdims=True) + NORM_EPS) * g


def _sig(x):
    return 0.5 * (jnp.tanh(0.5 * x) + 1.0)


def _softplus(x):
    return jnp.maximum(x, 0.0) + jnp.log(1.0 + jnp.exp(-jnp.abs(x)))


def _gelu(x):
    return x * (0.5 * (1.0 + jnp.tanh(math.sqrt(2.0 / math.pi) * (x + 0.044715 * (x * x * x)))))


def _bdot(a, b):
    return jnp.dot(a.astype(BF16), b.astype(BF16), preferred_element_type=F32)


def _hdot(a, b):
    return jnp.dot(a, b, precision=HI, preferred_element_type=F32)


def _segsum_impl(x, ones):
    hi = x.astype(BF16)
    lo = (x - hi.astype(F32)).astype(BF16)
    ones = ones.astype(BF16)
    return jnp.dot(jnp.concatenate([hi, lo], axis=1), jnp.concatenate([ones, ones], axis=0), preferred_element_type=F32)


@jax.custom_vjp
def _segsum(x, ones):
    return _segsum_impl(x, ones)


_segsum.defvjp(lambda x, ones: (_segsum_impl(x, ones), ones),
               lambda ones, g: (_segsum_impl(g, ones), jnp.zeros_like(ones)))


def _block_ones(n, blk):
    i = lax.broadcasted_iota(jnp.int32, (n, n), 0) // blk
    j = lax.broadcasted_iota(jnp.int32, (n, n), 1) // blk
    return (i == j).astype(F32)


def _rms_fwd(x, g, name):
    return _rows(lambda s, r, p, _: ([_rms(r[0], p[0])], []), [x], [g], [(x.shape[1], BF16)], [], name=name, tl=512)[0]


def _rms_bwd(x, g, dh, dres, name):
    def fn(s, r, p, _):
        _, vjp = jax.vjp(_rms, r[0], p[0])
        dx, dg = vjp(r[1])
        return [dx + r[2]], [dg]
    return _rows(fn, [x, dh, dres], [g], [(x.shape[1], F32)], [g.shape], name=name, tl=256)


def _rwkv_pre_math(k_, lr, w0, a0, k_k, k_a, w2p, a2p, g2p, o512):
    pre_w = w0 + _bdot(jnp.tanh(lr), w2p)
    w = -_softplus(-pre_w) - 0.5
    decay = jnp.exp(-jnp.exp(w))
    a = _sig(a0 + _bdot(lr, a2p))
    g = _bdot(_sig(lr), g2p)
    kr = k_ * k_k
    kk = kr / jnp.maximum(jnp.sqrt(_segsum(kr * kr, o512)), 1e-12)
    k2 = k_ * (1.0 + (a - 1.0) * k_a)
    return decay, k2, -kk, kk * a, g


def _shift_down(p, prev_row):
    row = lax.broadcasted_iota(jnp.int32, p.shape, 0)
    return jnp.where(row == 0, jnp.broadcast_to(prev_row, p.shape), pltpu.roll(p, 1, 0))


def _shift_up(q, next_row):
    n = q.shape[0]
    row = lax.broadcasted_iota(jnp.int32, q.shape, 0)
    return jnp.where(row == n - 1, jnp.broadcast_to(next_row, q.shape), pltpu.roll(q, n - 1, 0))


def _rwkv_pre_fwd(p, mu, small, o512):
    def fn(step, r, prm, scr):
        car = scr[0]

        @pl.when(step == 0)
        def _():
            car[...] = jnp.zeros_like(car)

        x = r[0]
        prev = _shift_down(x, car[SUBLANES - 1:SUBLANES, :])
        car[...] = x[x.shape[0] - SUBLANES:, :]
        xs = x + (prev - x) * prm[0]
        decay, k2, aa, bb, g = _rwkv_pre_math(xs[:, RW:2 * RW], xs[:, 3 * RW:], *prm[1:])
        return [xs[:, :RW], decay, k2, xs[:, 2 * RW:3 * RW], aa, bb, g], []
    return _rows(fn, [p], [mu, *small, o512], [(RW, F32)] * 7, [], name="rwkv_pre_fwd", tl=256,
                 scratch=[pltpu.VMEM((SUBLANES, N_RWKV), F32)])


def _rwkv_pre_bwd(p, cots, mu, small, o512):
    length = p.shape[0]
    tl = min(256, length)
    nt = length // tl
    rows_per = tl // SUBLANES
    params = [mu, *small, o512]
    acc_shapes = [mu.shape] + [q.shape for q in small]
    nr, npar, nacc = 2 + len(cots), len(params), len(acc_shapes)

    def body(*refs):
        rin, pin = refs[:nr], refs[nr:nr + npar]
        dp_ref = refs[nr + npar]
        aout = refs[nr + npar + 1:nr + npar + 1 + nacc]
        car_q = refs[nr + npar + 1 + nacc]
        step = pl.program_id(0)

        @pl.when(step == 0)
        def _():
            car_q[...] = jnp.zeros_like(car_q)
            for ref in aout:
                ref[...] = jnp.zeros_like(ref)

        x = rin[0][...]
        prev_row = jnp.where(step == nt - 1, 0.0, rin[1][SUBLANES - 1:SUBLANES, :])
        dr, ddecay, dk2, dv, daa, dbb, dg = [r[...] for r in rin[2:]]
        prm = [q[...] for q in pin]
        mu_, o512_ = prm[0], prm[-1]
        prev = _shift_down(x, prev_row)
        xs = x + (prev - x) * mu_
        _, vjp = jax.vjp(lambda k_, lr, *w: _rwkv_pre_math(k_, lr, *w, o512_), xs[:, RW:2 * RW], xs[:, 3 * RW:], *prm[1:-1])
        dk_, dlr, *dsmall = vjp((ddecay, dk2, daa, dbb, dg))
        dxs = jnp.concatenate([dr, dk_, dv, dlr], axis=1)
        q = dxs * mu_
        dp_ref[...] = (dxs - q + _shift_up(q, car_q[0:1, :])).astype(dp_ref.dtype)
        car_q[...] = q[:SUBLANES, :]
        aout[0][...] += jnp.sum((prev - x) * dxs, axis=0, keepdims=True)
        for ref, val in zip(aout[1:], dsmall, strict=True):
            ref[...] += val

    rmap = lambda i: nt - 1 - i
    specs = [pl.BlockSpec((tl, N_RWKV), lambda i: (rmap(i), 0)),
             pl.BlockSpec((SUBLANES, N_RWKV), lambda i: (jnp.maximum(rmap(i) * rows_per - 1, 0), 0))]
    specs += [pl.BlockSpec((tl, RW), lambda i: (rmap(i), 0)) for _ in cots]
    specs += [pl.BlockSpec(q.shape, lambda i, nd=q.ndim: (0,) * nd) for q in params]
    out_shape = [SDS((length, N_RWKV), BF16)] + [SDS(sh, F32) for sh in acc_shapes]
    out_specs = [pl.BlockSpec((tl, N_RWKV), lambda i: (rmap(i), 0))]
    out_specs += [pl.BlockSpec(sh, lambda i, nd=len(sh): (0,) * nd) for sh in acc_shapes]
    res = pl.pallas_call(body, grid=(nt,), in_specs=specs, out_specs=out_specs, out_shape=out_shape,
                         scratch_shapes=[pltpu.VMEM((SUBLANES, N_RWKV), F32)],
                         compiler_params=_params(), name="rwkv_pre_bwd")(p, p, *cots, *params)
    return list(res)


def _rwkv_post_math(y, r, k2, v, g, lnx_w, lnx_b, r_k, o512):
    mean = _segsum(y, o512) * (1.0 / HN)
    yc = y - mean
    var = _segsum(yc * yc, o512) * (1.0 / HN)
    yn = yc * lax.rsqrt(var + LNX_EPS) * lnx_w + lnx_b
    bonus = _segsum(r * k2 * r_k, o512) * v
    return (yn + bonus) * g


def _rwkv_post_fwd(y, r, k2, v, g, prm, o512):
    return _rows(lambda s, rr, p, _: ([_rwkv_post_math(*rr, *p)], []), [y, r, k2, v, g], [*prm, o512],
                 [(RW, BF16)], [], name="rwkv_post_fwd", tl=256)[0]


def _rwkv_post_bwd(y, r, k2, v, g, dout, prm, o512):
    def fn(s, rr, p, _):
        o = p[-1]
        _, vjp = jax.vjp(lambda *a: _rwkv_post_math(*a, o), *rr[:5], *p[:-1])
        gr = vjp(rr[5])
        return list(gr[:5]), list(gr[5:])
    return _rows(fn, [y, r, k2, v, g, dout], [*prm, o512], [(RW, F32)] * 5, [q.shape for q in prm],
                 name="rwkv_post_bwd", tl=256)


def _glu_math(ys, t, b):
    return ys * _sig(t + b)


def _glu_fwd(ys, t, b):
    return _rows(lambda s, r, p, _: ([_glu_math(r[0], r[1], p[0])], []), [ys, t], [b], [(S5W, BF16)], [],
                 name="s5_glu_fwd", tl=512)[0]


def _glu_bwd(ys, t, dout, b):
    def fn(s, r, p, _):
        _, vjp = jax.vjp(_glu_math, r[0], r[1], p[0])
        dys, dt, db = vjp(r[2])
        return [dys, dt], [db]
    return _rows(fn, [ys, t, dout], [b], [(S5W, F32), (S5W, BF16)], [b.shape], name="s5_glu_bwd", tl=512)


def _merge_math(gp_r, gp_s, o_r, o_s, b_r, b_s):
    return _sig(gp_r + b_r) * o_r + _sig(gp_s + b_s) * o_s


def _merge_fwd(gpre, o_r, o_s, b_r, b_s):
    return _rows(lambda s, r, p, _: ([_merge_math(*r, *p)], []),
                 [(gpre, D_MODEL, 0), (gpre, D_MODEL, 1), o_r, o_s], [b_r, b_s], [(D_MODEL, BF16)], [],
                 name="merge_fwd", tl=256)[0]


def _merge_bwd(gpre, o_r, o_s, dmi, b_r, b_s):
    def fn(s, r, p, _):
        _, vjp = jax.vjp(_merge_math, *r[:4], *p)
        dgr, dgs, dor, dos, dbr, dbs = vjp(r[4])
        return [dgr, dgs, dor, dos], [dbr, dbs]
    return _rows(fn, [(gpre, D_MODEL, 0), (gpre, D_MODEL, 1), o_r, o_s, dmi], [b_r, b_s], [(D_MODEL, BF16)] * 4,
                 [b_r.shape, b_s.shape], name="merge_bwd", tl=256)


def _mid_fwd(x, mixed, g_post, g_pre):
    def fn(s, r, p, _):
        x2 = r[0] + _rms(r[1], p[0])
        return [x2, _rms(x2, p[1])], []
    return _rows(fn, [x, mixed], [g_post, g_pre], [(D_MODEL, F32), (D_MODEL, BF16)], [], name="mid_fwd", tl=256)


def _mid_bwd(x2, mixed, dh2, dx3, g_post, g_pre):
    def fn(s, r, p, _):
        _, vjp1 = jax.vjp(_rms, r[0], p[1])
        dx2, dg_pre = vjp1(r[2])
        dx2 = dx2 + r[3]
        _, vjp2 = jax.vjp(_rms, r[1], p[0])
        dmixed, dg_post = vjp2(dx2)
        return [dx2, dmixed], [dg_post, dg_pre]
    return _rows(fn, [x2, mixed, dh2, dx3], [g_post, g_pre], [(D_MODEL, F32), (D_MODEL, BF16)], [g_post.shape, g_pre.shape],
                 name="mid_bwd", tl=256)


def _final(x2, f, tgt, g_post):
    def fn(s, r, p, _):
        y, vjp = jax.vjp(_rms, r[1], p[0])
        diff = r[0] + y - r[2]
        dx3 = diff * (1.0 / D_MODEL)
        df, dg = vjp(dx3)
        part = 0.5 * jnp.sum(jnp.sum(diff * diff, axis=1, keepdims=True), axis=0, keepdims=True) * (1.0 / D_MODEL)
        return [dx3, df], [dg, jnp.broadcast_to(part, (1, LANES))]
    return _rows(fn, [x2, f, tgt], [g_post], [(D_MODEL, F32), (D_MODEL, BF16)], [g_post.shape, (1, LANES)], name="final", tl=256)


def _conv_taps(z, car):
    row = lax.broadcasted_iota(jnp.int32, z.shape, 0)
    z1 = jnp.where(row == 0, jnp.broadcast_to(car[7:8, :], z.shape), pltpu.roll(z, 1, 0))
    z2 = pltpu.roll(z, 2, 0)
    z2 = jnp.where(row == 0, jnp.broadcast_to(car[6:7, :], z.shape), z2)
    z2 = jnp.where(row == 1, jnp.broadcast_to(car[7:8, :], z.shape), z2)
    return z1, z2


def _conv(z, car, w, b):
    z1, z2 = _conv_taps(z, car)
    return b + w[0:1, :] * z2 + w[1:2, :] * z1 + w[2:3, :] * z, z1, z2


def _conv_fwd(z, conv_w, conv_b):
    length = z.shape[0]
    tl = min(256, length)
    nt = length // tl
    tc = _pick(D_FF, 1536)
    nb = D_FF // tc

    def body(zg_ref, zv_ref, wg_ref, wv_ref, bg_ref, bv_ref, o_ref, cg, cv):
        @pl.when(pl.program_id(1) == 0)
        def _():
            cg[...] = jnp.zeros_like(cg)
            cv[...] = jnp.zeros_like(cv)

        zg, zv = zg_ref[...], zv_ref[...]
        gate, _, _ = _conv(zg, cg[...], wg_ref[...], bg_ref[...])
        val, _, _ = _conv(zv, cv[...], wv_ref[...], bv_ref[...])
        cg[...] = zg[tl - SUBLANES:, :]
        cv[...] = zv[tl - SUBLANES:, :]
        o_ref[...] = (_gelu(gate) * val).astype(o_ref.dtype)

    zspec = lambda off: pl.BlockSpec((tl, tc), lambda j, i: (i, j + off))
    wspec = lambda off, r: pl.BlockSpec((r, tc), lambda j, i: (0, j + off))
    return pl.pallas_call(
        body, grid=(nb, nt),
        in_specs=[zspec(0), zspec(nb), wspec(0, 3), wspec(nb, 3), wspec(0, 1), wspec(nb, 1)],
        out_specs=pl.BlockSpec((tl, tc), lambda j, i: (i, j)), out_shape=SDS((length, D_FF), BF16),
        scratch_shapes=[pltpu.VMEM((SUBLANES, tc), F32)] * 2,
        compiler_params=_params(("arbitrary", "arbitrary")), name="conv_fwd")(z, z, conv_w, conv_w, conv_b, conv_b)


def _conv_bwd(z, dact, conv_w, conv_b):
    length = z.shape[0]
    tl = min(256, length)
    nt = length // tl
    tc = _pick(D_FF, 1536)
    nb = D_FF // tc
    rows_per = tl // SUBLANES

    def half_bwd(dzc, z, z1, z2, w, dcar):
        n = tl
        row = lax.broadcasted_iota(jnp.int32, dzc.shape, 0)
        u1 = jnp.where(row == n - 1, jnp.broadcast_to(dcar[0:1, :], dzc.shape), pltpu.roll(dzc, n - 1, 0))
        u2 = pltpu.roll(dzc, n - 2, 0)
        u2 = jnp.where(row == n - 2, jnp.broadcast_to(dcar[0:1, :], dzc.shape), u2)
        u2 = jnp.where(row == n - 1, jnp.broadcast_to(dcar[1:2, :], dzc.shape), u2)
        dz = w[2:3, :] * dzc + w[1:2, :] * u1 + w[0:1, :] * u2
        dw = jnp.concatenate([jnp.sum(dzc * z2, axis=0, keepdims=True), jnp.sum(dzc * z1, axis=0, keepdims=True),
                              jnp.sum(dzc * z, axis=0, keepdims=True)], axis=0)
        return dz, dw, jnp.sum(dzc, axis=0, keepdims=True)

    def body(zg_ref, zv_ref, pg_ref, pv_ref, da_ref, wg_ref, wv_ref, bg_ref, bv_ref,
             dzg_ref, dzv_ref, dwg_ref, dwv_ref, dbg_ref, dbv_ref, cg, cv):
        step = pl.program_id(1)

        @pl.when(step == 0)
        def _():
            cg[...] = jnp.zeros_like(cg)
            cv[...] = jnp.zeros_like(cv)
            for ref in (dwg_ref, dwv_ref, dbg_ref, dbv_ref):
                ref[...] = jnp.zeros_like(ref)

        is_first_tile = step == nt - 1
        zg, zv = zg_ref[...], zv_ref[...]
        pg = jnp.where(is_first_tile, 0.0, pg_ref[...])
        pv = jnp.where(is_first_tile, 0.0, pv_ref[...])
        wg, wv = wg_ref[...], wv_ref[...]
        gate, zg1, zg2 = _conv(zg, pg, wg, bg_ref[...])
        val, zv1, zv2 = _conv(zv, pv, wv, bv_ref[...])
        act_g, vjp = jax.vjp(_gelu, gate)
        da = da_ref[...]
        dgate = vjp(da * val)[0]
        dval = da * act_g
        dzg, dwg, dbg = half_bwd(dgate, zg, zg1, zg2, wg, cg[...])
        dzv, dwv, dbv = half_bwd(dval, zv, zv1, zv2, wv, cv[...])
        cg[...] = dgate[:SUBLANES, :]
        cv[...] = dval[:SUBLANES, :]
        dzg_ref[...] = dzg.astype(dzg_ref.dtype)
        dzv_ref[...] = dzv.astype(dzv_ref.dtype)
        dwg_ref[...] += dwg
        dwv_ref[...] += dwv
        dbg_ref[...] += dbg
        dbv_ref[...] += dbv

    rmap = lambda i: nt - 1 - i
    zspec = lambda off: pl.BlockSpec((tl, tc), lambda j, i: (rmap(i), j + off))
    pspec = lambda off: pl.BlockSpec((SUBLANES, tc), lambda j, i: (jnp.maximum(rmap(i) * rows_per - 1, 0), j + off))
    wspec = lambda off, r: pl.BlockSpec((r, tc), lambda j, i: (0, j + off))
    out_w = lambda r: pl.BlockSpec((r, tc), lambda j, i: (0, j))
    dzg, dzv, dwg, dwv, dbg, dbv = pl.pallas_call(
        body, grid=(nb, nt),
        in_specs=[zspec(0), zspec(nb), pspec(0), pspec(nb), pl.BlockSpec((tl, tc), lambda j, i: (rmap(i), j)),
                  wspec(0, 3), wspec(nb, 3), wspec(0, 1), wspec(nb, 1)],
        out_specs=[pl.BlockSpec((tl, tc), lambda j, i: (rmap(i), j))] * 2 + [out_w(3), out_w(3), out_w(1), out_w(1)],
        out_shape=[SDS((length, D_FF), BF16)] * 2 + [SDS((3, D_FF), F32)] * 2 + [SDS((1, D_FF), F32)] * 2,
        scratch_shapes=[pltpu.VMEM((SUBLANES, tc), F32)] * 2,
        compiler_params=_params(("arbitrary", "arbitrary")), name="conv_bwd")(z, z, z, z, dact, conv_w, conv_w, conv_b, conv_b)
    return dzg, dzv, jnp.concatenate([dwg, dwv], axis=1), jnp.concatenate([dbg, dbv], axis=1)


def _s5_prep_math(a_re, a_im, ls, b_re, b_im):
    dt = jnp.exp(ls)
    er = jnp.exp(a_re * dt)
    ph = a_im * dt
    abr, abi = er * jnp.cos(ph), er * jnp.sin(ph)
    den = a_re * a_re + a_im * a_im
    nr = abr - 1.0
    cr = (nr * a_re + abi * a_im) / den
    ci = (abi * a_re - nr * a_im) / den
    return abr, abi, cr * b_re - ci * b_im, cr * b_im + ci * b_re


def _s5_prep_fwd(a_re, a_im, ls, b_re, b_im):
    def body(ar, ai, l, br, bi, o1, o2, o3, o4):
        for ref, val in zip((o1, o2, o3, o4), _s5_prep_math(ar[...], ai[...], l[...], br[...], bi[...]), strict=True):
            ref[...] = val
    return pl.pallas_call(body, out_shape=[SDS((1, S5N), F32)] * 2 + [SDS((S5C, S5N), F32)] * 2,
                          name="s5_prep_fwd")(a_re, a_im, ls, b_re, b_im)


def _s5_prep_bwd(a_re, a_im, ls, b_re, b_im, cots, gsel):
    def body(ar, ai, l, br, bi, c1, c2, c3, c4, g_ref, o1, o2, o3, o4, o5):
        _, vjp = jax.vjp(_s5_prep_math, ar[...], ai[...], l[...], br[...], bi[...])
        d_ar, d_ai, d_ls, d_br, d_bi = vjp((c1[...], c2[...], c3[...], c4[...]))
        o1[...] = d_ar
        o2[...] = d_ai
        o3[...] = _hdot(jnp.broadcast_to(d_ls, (SUBLANES, S5N)), g_ref[...])
        o4[...] = d_br
        o5[...] = d_bi
    return pl.pallas_call(body, out_shape=[SDS((1, S5N), F32)] * 2 + [SDS((SUBLANES, LANES), F32)] + [SDS((S5C, S5N), F32)] * 2,
                          name="s5_prep_bwd")(a_re, a_im, ls, b_re, b_im, *cots, gsel)


def _cmul(ar, ai, br, bi):
    return ar * br - ai * bi, ar * bi + ai * br


def _s5_powers(abr, abi):
    shp = (SUBLANES, S5N)
    a1 = (jnp.broadcast_to(abr, shp), jnp.broadcast_to(abi, shp))
    a2 = _cmul(*a1, *a1)
    a4 = _cmul(*a2, *a2)
    row = lax.broadcasted_iota(jnp.int32, shp, 0)
    pr, pi = a1
    cur = a1
    for i in range(1, SUBLANES):
        cur = _cmul(*cur, *a1)
        pr = jnp.where(row == i, cur[0], pr)
        pi = jnp.where(row == i, cur[1], pi)
    return a1, a2, a4, (pr, pi)


def _s5_scan(sre_ref, sim_ref, car_re, car_im, abr, abi, n_rows, reverse):
    a1, a2, a4, (pr, pi) = _s5_powers(abr, abi)
    sgn = -1.0 if reverse else 1.0
    row = lax.broadcasted_iota(jnp.int32, (SUBLANES, S5N), 0)
    if reverse:
        qr, qi = pr, pi
        for i in range(SUBLANES):
            src = SUBLANES - 1 - i
            qr = jnp.where(row == i, jnp.broadcast_to(pr[src:src + 1, :], pr.shape), qr)
            qi = jnp.where(row == i, jnp.broadcast_to(pi[src:src + 1, :], pi.shape), qi)
        pr, pi = qr, qi
    nblk = n_rows // SUBLANES

    def blk(i, carry):
        cr, ci = carry
        b = (nblk - 1 - i) if reverse else i
        sl = pl.ds(pl.multiple_of(b * SUBLANES, SUBLANES), SUBLANES)
        xr, xi = sre_ref[sl, :], sim_ref[sl, :]
        for kk, (er, ei) in ((1, a1), (2, a2), (4, a4)):
            if reverse:
                sr = jnp.where(row < SUBLANES - kk, pltpu.roll(xr, SUBLANES - kk, 0), 0.0)
                si = jnp.where(row < SUBLANES - kk, pltpu.roll(xi, SUBLANES - kk, 0), 0.0)
            else:
                sr = jnp.where(row >= kk, pltpu.roll(xr, kk, 0), 0.0)
                si = jnp.where(row >= kk, pltpu.roll(xi, kk, 0), 0.0)
            dr, di = _cmul(er, sgn * ei, sr, si)
            xr, xi = xr + dr, xi + di
        dr, di = _cmul(pr, sgn * pi, cr, ci)
        xr, xi = xr + dr, xi + di
        sre_ref[sl, :] = xr
        sim_ref[sl, :] = xi
        edge = 0 if reverse else SUBLANES - 1
        return (jnp.broadcast_to(xr[edge:edge + 1, :], xr.shape), jnp.broadcast_to(xi[edge:edge + 1, :], xi.shape))

    cr, ci = lax.fori_loop(0, nblk, blk, (car_re[...], car_im[...]))
    car_re[...] = cr
    car_im[...] = ci


S5_BLK = 4
S5_BN = S5N // S5_BLK


def _split2(x):
    hi = x.astype(BF16)
    return hi, (x - hi.astype(F32)).astype(BF16)


def _dot3(a, b, ca, cb):
    ah, al = _split2(a)
    bh, bl = _split2(b)
    return lax.dot_general(jnp.concatenate([ah, ah, al], axis=ca), jnp.concatenate([bh, bl, bh], axis=cb),
                           (((ca,), (cb,)), ((), ())), preferred_element_type=F32)


def _dot1(a, b, ca, cb):
    return lax.dot_general(a.astype(BF16), b.astype(BF16), (((ca,), (cb,)), ((), ())), preferred_element_type=F32)


def _s5_project_in(u, bblk_ref, sre_ref, sim_ref):
    for j in range(S5_BLK):
        bu = _dot1(u[:, j * LANES:(j + 1) * LANES], bblk_ref[j], 1, 0)
        sre_ref[:, j * S5_BN:(j + 1) * S5_BN] = bu[:, :S5_BN]
        sim_ref[:, j * S5_BN:(j + 1) * S5_BN] = bu[:, S5_BN:]


def _s5_project_out(u, d, cblk_ref, sre_ref, sim_ref):
    ys = []
    for j in range(S5_BLK):
        sl = slice(j * S5_BN, (j + 1) * S5_BN)
        ys.append(_dot1(sre_ref[:, sl], cblk_ref[j, :S5_BN, :], 1, 0) + _dot1(sim_ref[:, sl], cblk_ref[j, S5_BN:, :], 1, 0))
    return jnp.concatenate(ys, axis=1) + d * u


def _s5_fwd(u, bblk, cblk, abr, abi, d):
    length = u.shape[0]
    tl = min(256, length)
    nt = length // tl

    def body(u_ref, b_ref, c_ref, ar_ref, ai_ref, d_ref, ys_ref, kr_ref, ki_ref, sre, sim, car_re, car_im):
        @pl.when(pl.program_id(0) == 0)
        def _():
            car_re[...] = jnp.zeros_like(car_re)
            car_im[...] = jnp.zeros_like(car_im)

        kr_ref[0] = car_re[...]
        ki_ref[0] = car_im[...]
        u_ = u_ref[...]
        _s5_project_in(u_, b_ref, sre, sim)
        _s5_scan(sre, sim, car_re, car_im, ar_ref[...], ai_ref[...], tl, False)
        ys_ref[...] = _gelu(_s5_project_out(u_, d_ref[...], c_ref, sre, sim))

    full = lambda a: pl.BlockSpec(a.shape, lambda i, nd=a.ndim: (0,) * nd)
    chk = pl.BlockSpec((1, SUBLANES, S5N), lambda i: (i, 0, 0))
    return pl.pallas_call(
        body, grid=(nt,), in_specs=[pl.BlockSpec((tl, S5W), lambda i: (i, 0)), full(bblk), full(cblk), full(abr), full(abi), full(d)],
        out_specs=[pl.BlockSpec((tl, S5W), lambda i: (i, 0)), chk, chk],
        out_shape=[SDS((length, S5W), F32), SDS((nt, SUBLANES, S5N), F32), SDS((nt, SUBLANES, S5N), F32)],
        scratch_shapes=[pltpu.VMEM((tl, S5N), F32)] * 2 + [pltpu.VMEM((SUBLANES, S5N), F32)] * 2,
        compiler_params=_params(), name="s5_fwd")(u, bblk, cblk, abr, abi, d)


def _s5_bwd(u, dys_a, dys_b, kre, kim, bblk, cblk, abr, abi, d):
    length = u.shape[0]
    tl = min(256, length)
    nt = length // tl

    def body(u_ref, da_ref, db_ref, kr_ref, ki_ref, b_ref, c_ref, ar_ref, ai_ref, d_ref,
             du_ref, dB_ref, dC_ref, dar_ref, dai_ref, dd_ref, sre, sim, gre, gim, car_re, car_im, dcar_re, dcar_im):
        @pl.when(pl.program_id(0) == 0)
        def _():
            dcar_re[...] = jnp.zeros_like(dcar_re)
            dcar_im[...] = jnp.zeros_like(dcar_im)
            for ref in (dB_ref, dC_ref, dar_ref, dai_ref, dd_ref):
                ref[...] = jnp.zeros_like(ref)

        u_ = u_ref[...]
        abr_, abi_, d_ = ar_ref[...], ai_ref[...], d_ref[...]
        car_re[...] = kr_ref[0]
        car_im[...] = ki_ref[0]
        _s5_project_in(u_, b_ref, sre, sim)
        _s5_scan(sre, sim, car_re, car_im, abr_, abi_, tl, False)
        y = _s5_project_out(u_, d_, c_ref, sre, sim)
        _, vjp = jax.vjp(_gelu, y)
        dy = vjp(da_ref[...] + db_ref[...])[0]
        dd_ref[...] += jnp.sum(dy * u_, axis=0, keepdims=True)
        for j in range(S5_BLK):
            sl = slice(j * S5_BN, (j + 1) * S5_BN)
            dyj = dy[:, j * LANES:(j + 1) * LANES]
            gre[:, sl] = _dot1(dyj, c_ref[j, :S5_BN, :], 1, 1)
            gim[:, sl] = _dot1(dyj, c_ref[j, S5_BN:, :], 1, 1)
            dC_ref[j, :S5_BN, :] += _dot1(sre[:, sl], dyj, 0, 0)
            dC_ref[j, S5_BN:, :] += _dot1(sim[:, sl], dyj, 0, 0)
        _s5_scan(gre, gim, dcar_re, dcar_im, abr_, abi_, tl, True)
        gr, gi = gre[...], gim[...]
        pr = _shift_down(sre[...], kr_ref[0, 0:1, :])
        pi = _shift_down(sim[...], ki_ref[0, 0:1, :])
        dar_ref[...] += jnp.sum(gr * pr + gi * pi, axis=0, keepdims=True)
        dai_ref[...] += jnp.sum(gi * pr - gr * pi, axis=0, keepdims=True)
        dus = []
        for j in range(S5_BLK):
            sl = slice(j * S5_BN, (j + 1) * S5_BN)
            uj = u_[:, j * LANES:(j + 1) * LANES]
            dus.append(_dot1(gre[:, sl], b_ref[j, :, :S5_BN], 1, 1) + _dot1(gim[:, sl], b_ref[j, :, S5_BN:], 1, 1))
            dB_ref[j, :, :S5_BN] += _dot1(uj, gre[:, sl], 0, 0)
            dB_ref[j, :, S5_BN:] += _dot1(uj, gim[:, sl], 0, 0)
        du_ref[...] = (jnp.concatenate(dus, axis=1) + d_ * dy).astype(du_ref.dtype)

    rmap = lambda i: nt - 1 - i
    full = lambda a: pl.BlockSpec(a.shape, lambda i, nd=a.ndim: (0,) * nd)
    row = pl.BlockSpec((tl, S5W), lambda i: (rmap(i), 0))
    chk = pl.BlockSpec((1, SUBLANES, S5N), lambda i: (rmap(i), 0, 0))
    return pl.pallas_call(
        body, grid=(nt,), in_specs=[row, row, row, chk, chk, full(bblk), full(cblk), full(abr), full(abi), full(d)],
        out_specs=[row, full(bblk), full(cblk), full(abr), full(abi), full(d)],
        out_shape=[SDS((length, S5W), BF16), SDS(bblk.shape, F32), SDS(cblk.shape, F32), SDS(abr.shape, F32),
                   SDS(abi.shape, F32), SDS(d.shape, F32)],
        scratch_shapes=[pltpu.VMEM((tl, S5N), F32)] * 4 + [pltpu.VMEM((SUBLANES, S5N), F32)] * 4,
        compiler_params=_params(), name="s5_bwd")(u, dys_a, dys_b, kre, kim, bblk, cblk, abr, abi, d)


def _s5_blockdiag_in(bb):
    t = bb.reshape(S5C, S5_BLK, S5_BN).transpose(1, 0, 2)
    t = jnp.tile(t, (1, SUBLANES, 1))
    mask = (lax.broadcasted_iota(jnp.int32, (LANES, S5_BN), 0) // S5C) == (lax.broadcasted_iota(jnp.int32, (LANES, S5_BN), 1) // S5P)
    return jnp.where(mask[None], t, 0.0)


def _s5_blockdiag_in_t(dblk):
    t = dblk.reshape(S5_BLK, SUBLANES, S5C, SUBLANES, S5P)
    t = jnp.diagonal(t, axis1=1, axis2=3)
    return t.transpose(1, 0, 3, 2).reshape(S5C, S5N)


def _s5_blockdiag_out(c):
    t = c.reshape(S5_BLK, SUBLANES, S5C, S5P).transpose(0, 3, 1, 2).reshape(S5_BLK, S5P, LANES)
    t = jnp.tile(t, (1, SUBLANES, 1))
    mask = (lax.broadcasted_iota(jnp.int32, (S5_BN, LANES), 0) // S5P) == (lax.broadcasted_iota(jnp.int32, (S5_BN, LANES), 1) // S5C)
    return jnp.where(mask[None], t, 0.0)


def _s5_blockdiag_out_t(dblk):
    t = dblk.reshape(S5_BLK, SUBLANES, S5P, SUBLANES, S5C)
    t = jnp.diagonal(t, axis1=1, axis2=3)
    return t.transpose(0, 3, 2, 1).reshape(S5G, S5C, S5P)


def _wkv_consts():
    lane = lax.broadcasted_iota(jnp.int32, (4 * HN, LANES), 1)
    row = lax.broadcasted_iota(jnp.int32, (4 * HN, LANES), 0)
    diag = ((lane % HN) == (row % HN)).astype(F32)
    ones = _block_ones(LANES, HN).astype(BF16)
    return lane % HN, diag, jnp.concatenate([ones, ones], axis=0)


def _wkv_split(p):
    hi = p.astype(BF16)
    lo = (p - hi.astype(F32)).astype(BF16)
    return jnp.concatenate([hi, lo], axis=1)


def _wkv_rowseg(p, ones):
    return jnp.dot(_wkv_split(p), ones, preferred_element_type=F32)


def _wkv_rowseg_all(lhs_ref, ones, out_ref):
    c, ns = lhs_ref.shape[0], lhs_ref.shape[1]
    out_ref[...] = jnp.dot(lhs_ref[...].reshape(c * ns, 2 * LANES), ones, preferred_element_type=F32).reshape(c, ns, LANES)


def _wkv_full(qb_ref, t):
    q = qb_ref[t]
    return jnp.concatenate([jnp.tile(q[:, hp * LANES:(hp + 1) * LANES], (SUBLANES, 1)) for hp in range(4)], axis=0)


def _wkv_colsum8(x):
    return jnp.concatenate([x[hp * HN:(hp + 1) * HN].reshape(SUBLANES, SUBLANES, LANES).sum(axis=0) for hp in range(4)], axis=1)


def _wkv_prebroadcast(src_refs, dst_refs, n):
    rep = (lax.broadcasted_iota(jnp.int32, (n * SUBLANES, n), 0) // SUBLANES
           == lax.broadcasted_iota(jnp.int32, (n * SUBLANES, n), 1)).astype(BF16)
    rep3 = jnp.concatenate([rep, rep, rep], axis=1)
    for s, dref in zip(src_refs, dst_refs, strict=True):
        x = s[...]
        hi = x.astype(BF16)
        rest = x - hi.astype(F32)
        mid = rest.astype(BF16)
        lo = (rest - mid.astype(F32)).astype(BF16)
        dref[...] = jnp.dot(rep3, jnp.concatenate([hi, mid, lo], axis=0), preferred_element_type=F32).reshape(n, SUBLANES, RW)


def _wkv_pair_sums(lhs_ref, out_ref):
    n, ns = out_ref.shape[0], out_ref.shape[1]
    ones_pair = _block_ones(2 * LANES, HN).astype(BF16)
    out_ref[...] = jnp.dot(lhs_ref[0:n].reshape(n * ns, 2 * LANES), ones_pair,
                           preferred_element_type=F32).reshape(n, ns, 2 * LANES)


def _wkv_transpose_out(acc, c, diag):
    lane = lax.broadcasted_iota(jnp.int32, (c, LANES), 1)
    outs = []
    for hp in range(4):
        zp = lax.dot_general(acc[hp * HN:(hp + 1) * HN], diag[:HN], (((0,), (0,)), ((), ())), precision=HI,
                             preferred_element_type=F32)
        outs.append(jnp.where(lane < HN, zp[:c], zp[HN:HN + c]))
    return jnp.concatenate(outs, axis=1)


def _wkv_fwd(r, w, k, v, a, b):
    length = r.shape[0]
    c = min(WKV_CHUNK, length)
    nc = length // c
    ns = 4 * HN

    def body(r_ref, w_ref, k_ref, v_ref, a_ref, b_ref, y_ref, sst_ref, sast_ref, vst_ref, fin_ref,
             s_ref, sa_ref, acc_ref, lhs_ref, ysum_ref, rb, wb, kb, vb, ab, bb):
        @pl.when(pl.program_id(0) == 0)
        def _():
            s_ref[...] = jnp.zeros_like(s_ref)

        _wkv_prebroadcast((r_ref, w_ref, k_ref, v_ref, a_ref, b_ref), (rb, wb, kb, vb, ab, bb), c)
        lane_t, diag, ones = _wkv_consts()

        def fill_v(t, carry):
            lhs_ref[t] = _wkv_split(_wkv_full(vb, t) * diag)
            return carry

        lax.fori_loop(0, c, fill_v, 0)
        _wkv_rowseg_all(lhs_ref, ones, vst_ref)
        sa_ref[...] = _wkv_rowseg(s_ref[...] * _wkv_full(ab, 0), ones)

        def pair(p, carry):
            for half in range(2):
                t = 2 * p + half
                s, sa = s_ref[...], sa_ref[...]
                sst_ref[t] = s
                sast_ref[t] = sa
                s = s * _wkv_full(wb, t) + sa * _wkv_full(bb, t) + vst_ref[t] * _wkv_full(kb, t)
                s_ref[...] = s
                sa_ref[...] = _wkv_rowseg(s * _wkv_full(ab, jnp.minimum(t + 1, c - 1)), ones)
                lhs_ref[p, :, half * LANES:(half + 1) * LANES] = (s * _wkv_full(rb, t)).astype(BF16)
            return carry

        lax.fori_loop(0, c // 2, pair, 0)
        _wkv_pair_sums(lhs_ref, ysum_ref)
        acc_ref[...] = jnp.zeros_like(acc_ref)

        def gather_y(p, carry):
            both = ysum_ref[p]
            acc = jnp.where(lane_t == 2 * p, both[:, :LANES], acc_ref[...])
            acc_ref[...] = jnp.where(lane_t == 2 * p + 1, both[:, LANES:], acc)
            return carry

        lax.fori_loop(0, c // 2, gather_y, 0)
        y_ref[...] = _wkv_transpose_out(acc_ref[...], c, diag)
        fin_ref[...] = s_ref[...]

    row = pl.BlockSpec((c, RW), lambda i: (i, 0))
    st = pl.BlockSpec((c, ns, LANES), lambda i: (i, 0, 0))
    state = pltpu.VMEM((ns, LANES), F32)
    return pl.pallas_call(
        body, grid=(nc,), in_specs=[row] * 6,
        out_specs=[row, st, st, st, pl.BlockSpec((ns, LANES), lambda i: (0, 0))],
        out_shape=[SDS((length, RW), F32)] + [SDS((length, ns, LANES), F32)] * 3 + [SDS((ns, LANES), F32)],
        scratch_shapes=[state] * 3 + [pltpu.VMEM((c, ns, 2 * LANES), BF16), pltpu.VMEM((c // 2, ns, 2 * LANES), F32)]
        + [pltpu.VMEM((c, SUBLANES, RW), F32)] * 6,
        compiler_params=_params(), name="wkv_fwd")(r, w, k, v, a, b)


def _wkv_bwd(r, w, k, a, b, dy, sst, sast, vst, fin):
    length = r.shape[0]
    c = min(WKV_CHUNK, length)
    nc = length // c
    ns = 4 * HN

    def body(r_ref, w_ref, k_ref, a_ref, b_ref, dy_ref, sst_ref, sast_ref, vst_ref, snext_ref, fin_ref,
             dr_ref, dw_ref, dk_ref, da_ref, db_ref, dv_ref,
             ds_ref, cur_ref, acc_ref, dyst, lhs_ref, dvsum_ref, rb, wb, kb, ab, bb, dyb, p_r, p_w, p_k, p_a, p_b):
        @pl.when(pl.program_id(0) == 0)
        def _():
            ds_ref[...] = jnp.zeros_like(ds_ref)

        _wkv_prebroadcast((r_ref, w_ref, k_ref, a_ref, b_ref, dy_ref), (rb, wb, kb, ab, bb, dyb), c)
        lane_t, diag, ones = _wkv_consts()

        def fill(t, carry):
            lhs_ref[t] = _wkv_split(_wkv_full(dyb, t) * diag)
            return carry

        lax.fori_loop(0, c, fill, 0)
        _wkv_rowseg_all(lhs_ref, ones, dyst)
        cur_ref[...] = jnp.where(pl.program_id(0) == 0, fin_ref[...], snext_ref[0])

        def pair(p, carry):
            for half in range(2):
                t = c - 1 - (2 * p + half)
                s_t, s_prev, sa, vv, dyy = cur_ref[...], sst_ref[t], sast_ref[t], vst_ref[t], dyst[t]
                ds = ds_ref[...] + dyy * _wkv_full(rb, t)
                dsa = _wkv_rowseg(ds * _wkv_full(bb, t), ones)
                ds_ref[...] = ds * _wkv_full(wb, t) + dsa * _wkv_full(ab, t)
                cur_ref[...] = s_prev
                lhs_ref[p, :, half * LANES:(half + 1) * LANES] = (ds * _wkv_full(kb, t)).astype(BF16)
                p_r[t] = _wkv_colsum8(s_t * dyy)
                p_k[t] = _wkv_colsum8(ds * vv)
                p_b[t] = _wkv_colsum8(ds * sa)
                p_w[t] = _wkv_colsum8(ds * s_prev)
                p_a[t] = _wkv_colsum8(s_prev * dsa)
            return carry

        lax.fori_loop(0, c // 2, pair, 0)
        _wkv_pair_sums(lhs_ref, dvsum_ref)
        acc_ref[...] = jnp.zeros_like(acc_ref)

        def gather_dv(p, carry):
            both = dvsum_ref[p]
            acc = jnp.where(lane_t == c - 1 - 2 * p, both[:, :LANES], acc_ref[...])
            acc_ref[...] = jnp.where(lane_t == c - 2 - 2 * p, both[:, LANES:], acc)
            return carry

        lax.fori_loop(0, c // 2, gather_dv, 0)
        sel = (lax.broadcasted_iota(jnp.int32, (c, c * SUBLANES), 1) // SUBLANES
               == lax.broadcasted_iota(jnp.int32, (c, c * SUBLANES), 0)).astype(F32)
        for out_ref, part in ((dr_ref, p_r), (dw_ref, p_w), (dk_ref, p_k), (da_ref, p_a), (db_ref, p_b)):
            out_ref[...] = _dot3(sel, part[...].reshape(c * SUBLANES, RW), 1, 0)
        dv_ref[...] = _wkv_transpose_out(acc_ref[...], c, diag)

    rmap = lambda i: nc - 1 - i
    row = pl.BlockSpec((c, RW), lambda i: (rmap(i), 0))
    st = pl.BlockSpec((c, ns, LANES), lambda i: (rmap(i), 0, 0))
    nxt = pl.BlockSpec((1, ns, LANES), lambda i: (jnp.minimum((rmap(i) + 1) * c, length - 1), 0, 0))
    state = pltpu.VMEM((ns, LANES), F32)
    return pl.pallas_call(
        body, grid=(nc,), in_specs=[row] * 6 + [st, st, st, nxt, pl.BlockSpec((ns, LANES), lambda i: (0, 0))],
        out_specs=[row] * 6, out_shape=[SDS((length, RW), F32)] * 6,
        scratch_shapes=[state] * 3 + [pltpu.VMEM((c, ns, LANES), F32), pltpu.VMEM((c, ns, 2 * LANES), BF16),
                                       pltpu.VMEM((c // 2, ns, 2 * LANES), F32)]
        + [pltpu.VMEM((c, SUBLANES, RW), F32)] * 11,
        compiler_params=_params(), name="wkv_bwd")(r, w, k, a, b, dy, sst, sast, vst, sst, fin)


def _my_place():
    return lax.axis_index("x"), lax.axis_index("y"), lax.axis_index("c")


def _peer(x, y, c, k):
    return (x ^ ((k >> 2) & 1), y ^ ((k >> 1) & 1), c ^ (k & 1))


def _all_gather(shard, name):
    rows = shard.shape[0]

    def body(in_ref, out_ref, send_sems, recv_sems, local_sem):
        x, y, c = _my_place()
        me = 4 * x + 2 * y + c
        mine = pltpu.make_async_copy(in_ref, out_ref.at[me], local_sem)
        mine.start()
        sends = []
        for k in range(1, N_DEV):
            cp = pltpu.make_async_remote_copy(src_ref=in_ref, dst_ref=out_ref.at[me], send_sem=send_sems.at[k - 1],
                                              recv_sem=recv_sems.at[k - 1], device_id=_peer(x, y, c, k), device_id_type=MESH)
            cp.start()
            sends.append(cp)
        for k in range(1, N_DEV):
            px, py, pc = _peer(x, y, c, k)
            pltpu.make_async_remote_copy(src_ref=in_ref, dst_ref=out_ref.at[4 * px + 2 * py + pc], send_sem=send_sems.at[k - 1],
                                         recv_sem=recv_sems.at[k - 1], device_id=(px, py, pc), device_id_type=MESH).wait_recv()
        for cp in sends:
            cp.wait_send()
        mine.wait()

    return pl.pallas_call(
        body, out_shape=SDS((N_DEV, rows, LANES), shard.dtype),
        in_specs=[pl.BlockSpec(memory_space=pl.ANY)], out_specs=pl.BlockSpec(memory_space=pl.ANY),
        scratch_shapes=[pltpu.SemaphoreType.DMA((N_DEV - 1,)), pltpu.SemaphoreType.DMA((N_DEV - 1,)), pltpu.SemaphoreType.DMA],
        name=name)(shard)


def _all_to_all(chunks, name):
    rows = chunks.shape[1]

    def body(in_ref, out_ref, send_sems, recv_sems, local_sem):
        x, y, c = _my_place()
        me = 4 * x + 2 * y + c
        mine = pltpu.make_async_copy(in_ref.at[me], out_ref.at[me], local_sem)
        mine.start()
        sends = []
        for k in range(1, N_DEV):
            px, py, pc = _peer(x, y, c, k)
            cp = pltpu.make_async_remote_copy(src_ref=in_ref.at[4 * px + 2 * py + pc], dst_ref=out_ref.at[me],
                                              send_sem=send_sems.at[k - 1], recv_sem=recv_sems.at[k - 1],
                                              device_id=(px, py, pc), device_id_type=MESH)
            cp.start()
            sends.append(cp)
        for k in range(1, N_DEV):
            px, py, pc = _peer(x, y, c, k)
            pltpu.make_async_remote_copy(src_ref=in_ref.at[me], dst_ref=out_ref.at[4 * px + 2 * py + pc], send_sem=send_sems.at[k - 1],
                                         recv_sem=recv_sems.at[k - 1], device_id=(px, py, pc), device_id_type=MESH).wait_recv()
        for cp in sends:
            cp.wait_send()
        mine.wait()

    return pl.pallas_call(
        body, out_shape=SDS((N_DEV, rows, LANES), chunks.dtype),
        in_specs=[pl.BlockSpec(memory_space=pl.ANY)], out_specs=pl.BlockSpec(memory_space=pl.ANY),
        scratch_shapes=[pltpu.SemaphoreType.DMA((N_DEV - 1,)), pltpu.SemaphoreType.DMA((N_DEV - 1,)), pltpu.SemaphoreType.DMA],
        name=name)(chunks)


_HBM_SPEC = pl.BlockSpec(memory_space=pltpu.HBM)
_SEM_SPEC = pl.BlockSpec(memory_space=pltpu.SEMAPHORE)
_DATAFLOW = pltpu.SideEffectType.DATAFLOW_SIDE_EFFECTING


def _exchange_start(src, per_peer, name):
    rows = src.shape[-2]
    land = jnp.zeros((N_DEV, rows, LANES), src.dtype)

    def body(src_ref, land_ref, send_sems, recv_sems, src_thru, land_thru, token):
        x, y, c = _my_place()
        me = 4 * x + 2 * y + c
        for k in range(1, N_DEV):
            px, py, pc = _peer(x, y, c, k)
            mine = src_ref.at[4 * px + 2 * py + pc] if per_peer else src_ref
            pltpu.make_async_remote_copy(src_ref=mine, dst_ref=land_ref.at[me], send_sem=send_sems.at[k - 1],
                                         recv_sem=recv_sems.at[k - 1], device_id=(px, py, pc), device_id_type=MESH).start()
        token[...] = jnp.zeros_like(token)

    return pl.pallas_call(
        body, name=name,
        out_shape=(pltpu.SemaphoreType.DMA((N_DEV - 1,)), pltpu.SemaphoreType.DMA((N_DEV - 1,)), pltpu.HBM(src.shape, src.dtype),
                   pltpu.HBM(land.shape, land.dtype), SDS((SUBLANES, LANES), F32)),
        in_specs=(_HBM_SPEC, _HBM_SPEC),
        out_specs=(_SEM_SPEC, _SEM_SPEC, _HBM_SPEC, _HBM_SPEC, pl.BlockSpec(memory_space=pltpu.VMEM)),
        input_output_aliases={0: 2, 1: 3}, compiler_params=pltpu.CompilerParams(has_side_effects=_DATAFLOW),
    )(pltpu.with_memory_space_constraint(src, pltpu.HBM), pltpu.with_memory_space_constraint(land, pltpu.HBM))


def _exchange_wait(started, after, per_peer, name):
    send_sems, recv_sems, src_thru, land_thru, _ = started

    def body(src_ref, land_ref, send_sems, recv_sems, after_ref, src_dead, got_ref):
        x, y, c = _my_place()
        me = 4 * x + 2 * y + c
        for k in range(1, N_DEV):
            px, py, pc = _peer(x, y, c, k)
            mine = src_ref.at[me] if per_peer else src_ref
            copy = pltpu.make_async_remote_copy(src_ref=mine, dst_ref=land_ref.at[4 * px + 2 * py + pc], send_sem=send_sems.at[k - 1],
                                                recv_sem=recv_sems.at[k - 1], device_id=(px, py, pc), device_id_type=MESH)
            copy.wait_send()
            copy.wait_recv()

    return pl.pallas_call(
        body, name=name, out_shape=(pltpu.HBM(src_thru.shape, src_thru.dtype), pltpu.HBM(land_thru.shape, land_thru.dtype)),
        in_specs=(_HBM_SPEC, _HBM_SPEC, _SEM_SPEC, _SEM_SPEC, pl.BlockSpec(memory_space=pl.ANY)),
        out_specs=(_HBM_SPEC, _HBM_SPEC), input_output_aliases={0: 0, 1: 1},
        compiler_params=pltpu.CompilerParams(has_side_effects=_DATAFLOW),
    )(src_thru, land_thru, send_sems, recv_sems, after)[1]


def _sum_parts(parts, name, own=None):
    rows = parts.shape[1]

    def body(*refs):
        p_ref, g_out = refs[0], refs[-1]
        g = p_ref[0].astype(F32)
        for s in range(1, N_DEV):
            g = g + p_ref[s].astype(F32)
        if own is not None:
            g = g + refs[1][...].astype(F32)
        g_out[...] = g

    blk = pl.BlockSpec((PACK_ROWS, LANES), lambda i: (i, 0))
    extra = [] if own is None else [own]
    return pl.pallas_call(
        body, grid=(rows // PACK_ROWS,),
        in_specs=[pl.BlockSpec((N_DEV, PACK_ROWS, LANES), lambda i: (0, i, 0))] + [blk] * len(extra),
        out_specs=blk, out_shape=SDS((rows, LANES), F32),
        compiler_params=_params(("parallel",)), name=name)(parts, *extra)


def _adamw(g, wgt, m, v):
    rows = wgt.shape[0]

    def body(g_ref, w_ref, m_ref, v_ref, d_out, m_out, v_out):
        g_ = g_ref[...]
        m_new = ADAM_B1 * m_ref[...] + (1.0 - ADAM_B1) * g_
        v_new = ADAM_B2 * v_ref[...] + (1.0 - ADAM_B2) * (g_ * g_)
        m_hat = m_new / (1.0 - ADAM_B1 ** ADAM_STEP)
        v_hat = v_new / (1.0 - ADAM_B2 ** ADAM_STEP)
        d_out[...] = -ADAM_LR * (m_hat / (jnp.sqrt(v_hat) + ADAM_EPS) + ADAM_WD * w_ref[...])
        m_out[...] = m_new
        v_out[...] = v_new

    blk = pl.BlockSpec((PACK_ROWS, LANES), lambda i: (i, 0))
    return pl.pallas_call(
        body, grid=(rows // PACK_ROWS,), in_specs=[blk] * 4, out_specs=[blk] * 3, out_shape=[SDS((rows, LANES), F32)] * 3,
        compiler_params=_params(("parallel",)), name="adamw")(g, wgt, m, v)


PACK_ALIGN = 2 * SUBLANES * LANES
PACK_ROWS = 512

ROW_SHARDED = ("s5_w_glu", "w_out", "ffn_w_down")
TRANSPOSED = ("w_in", "ffn_w_up", "w_branch_rwkv", "w_branch_s5")
SMALL_SHARDED = ("rwkv_w2", "rwkv_a2", "rwkv_g2", "ffn_conv_w")
SHARDED = ROW_SHARDED + TRANSPOSED + SMALL_SHARDED
F32_GATHER = ("ffn_conv_w",)
REPLICATED =("norm_mix_pre", "norm_mix_post", "norm_ffn_pre", "norm_ffn_post", "b_gate", "rwkv_shift_mu", "rwkv_w0",
              "rwkv_a0", "rwkv_k_k", "rwkv_k_a", "rwkv_r_k", "rwkv_lnx_w", "rwkv_lnx_b", "s5_a_re", "s5_a_im", "s5_b_re",
              "s5_b_im", "s5_c_re", "s5_c_im", "s5_d", "s5_log_step", "s5_b_glu", "ffn_conv_b")
WEIGHTS = ("norm_mix_pre", "norm_mix_post", "norm_ffn_pre", "norm_ffn_post", "w_in", "b_gate", "rwkv_shift_mu", "rwkv_w0",
           "rwkv_w2", "rwkv_a0", "rwkv_a2", "rwkv_g2", "rwkv_k_k", "rwkv_k_a", "rwkv_r_k", "rwkv_lnx_w", "rwkv_lnx_b",
           "s5_a_re", "s5_a_im", "s5_b_re", "s5_b_im", "s5_c_re", "s5_c_im", "s5_d", "s5_log_step", "s5_w_glu", "s5_b_glu",
           "w_branch_rwkv", "w_branch_s5", "w_out", "ffn_w_up", "ffn_conv_w", "ffn_conv_b", "ffn_w_down")


def _pad_flat(a):
    flat = a.reshape(-1)
    pad = (-flat.shape[0]) % PACK_ALIGN
    return jnp.pad(flat, (0, pad)) if pad else flat


def _pad_cols(a, mult):
    pad = (-a.shape[1]) % mult
    return jnp.pad(a, ((0, 0), (0, pad))) if pad else a


def _pack(pieces):
    flat = jnp.concatenate([_pad_flat(p) for p in pieces])
    pad = (-flat.shape[0]) % (PACK_ROWS * LANES)
    return (jnp.pad(flat, (0, pad)) if pad else flat).reshape(-1, LANES)


def _unpack(buf, shapes):
    lead = buf.shape[:-2]
    flat = buf.reshape(*lead, -1)
    out, off = [], 0
    for s in shapes:
        n = math.prod(s)
        out.append(flat[..., off:off + n].reshape(*lead, *s))
        off += n + (-n) % PACK_ALIGN
    return out


def _rows_to_full(stack):
    return stack.reshape(-1, stack.shape[2])


def _cols_to_full(stack):
    return stack.transpose(1, 0, 2).reshape(stack.shape[1], -1)


EARLY_GRADS = ("ffn_w_down", "ffn_w_up", "w_out")
REST_GRADS = ("s5_w_glu", "w_in", "w_branch_rwkv", "w_branch_s5")


def _local_step(x, tgt, wt, get_late, send_early, send_rest):
    wt = dict(wt)
    o512 = _block_ones(RW, HN)
    w_in = wt["w_in"]
    w_rw, w_u, w_g = w_in[:N_RWKV], w_in[N_RWKV:N_RWKV + S5W], w_in[N_RWKV + S5W:]
    b_gate = wt["b_gate"]
    b_r, b_s = b_gate[:, :D_MODEL], b_gate[:, D_MODEL:]
    zpad = lambda a, lo, n: jnp.pad(a.astype(F32), ((lo, n - lo - a.shape[0]), (0, 0)))
    w2p, a2p, g2p = zpad(wt["rwkv_w2"], 0, 256), zpad(wt["rwkv_a2"], 64, 256), zpad(wt["rwkv_g2"], 128, 256)
    pre_small = [wt["rwkv_w0"], wt["rwkv_a0"], wt["rwkv_k_k"], wt["rwkv_k_a"], w2p, a2p, g2p]
    post_prm = [wt["rwkv_lnx_w"], wt["rwkv_lnx_b"], wt["rwkv_r_k"].reshape(1, RW)]
    mu = wt["rwkv_shift_mu"]

    a_re, a_im = wt["s5_a_re"].reshape(1, S5N), wt["s5_a_im"].reshape(1, S5N)
    ls = jnp.repeat(wt["s5_log_step"].reshape(S5G), S5P).reshape(1, S5N)
    b_re_t = wt["s5_b_re"].reshape(S5N, S5C).T
    b_im_t = wt["s5_b_im"].reshape(S5N, S5C).T
    c_re, c_im = wt["s5_c_re"].reshape(S5G, S5C, S5P), wt["s5_c_im"].reshape(S5G, S5C, S5P)
    abr, abi, bbr, bbi = _s5_prep_fwd(a_re, a_im, ls, b_re_t, b_im_t)
    bblk = jnp.concatenate([_s5_blockdiag_in(bbr), _s5_blockdiag_in(bbi)], axis=2)
    cblk = jnp.concatenate([_s5_blockdiag_out(c_re), -_s5_blockdiag_out(c_im)], axis=1)
    s5_d = wt["s5_d"]

    h1 = _rms_fwd(x, wt["norm_mix_pre"], "rms1_fwd")
    p_rw = _mm(h1, w_rw, tb=True, name="mm_proj_rwkv")
    u = _mm(h1, w_u, tb=True, name="mm_proj_s5")
    gpre = _mm(h1, w_g, tb=True, name="mm_proj_gate")
    r, decay, k2, v, aa, bb, g = _rwkv_pre_fwd(p_rw, mu, pre_small, o512)
    y, sst, sast, vst, s_fin = _wkv_fwd(r, decay, k2, v, aa, bb)
    wt.update(get_late(y))
    y_r = _rwkv_post_fwd(y, r, k2, v, g, post_prm, o512)
    o_r = _mm(y_r, wt["w_branch_rwkv"], tb=True, name="mm_branch_rwkv")
    ys, kre, kim = _s5_fwd(u, bblk, cblk, abr, abi, s5_d)
    t_glu = _mm(ys, wt["s5_w_glu"], name="mm_glu")
    out5 = _glu_fwd(ys, t_glu, wt["s5_b_glu"])
    o_s = _mm(out5, wt["w_branch_s5"], tb=True, name="mm_branch_s5")
    mi = _merge_fwd(gpre, o_r, o_s, b_r, b_s)
    mixed = _mm(mi, wt["w_out"], name="mm_out")
    x2, h2 = _mid_fwd(x, mixed, wt["norm_mix_post"], wt["norm_ffn_pre"])
    z = _mm(h2, wt["ffn_w_up"], tb=True, name="mm_up")
    act = _conv_fwd(z, wt["ffn_conv_w"], wt["ffn_conv_b"])
    f = _mm(act, wt["ffn_w_down"], name="mm_down")
    dx3, df, d_norm_ffn_post, loss_part = _final(x2, f, tgt, wt["norm_ffn_post"])

    gr = {"norm_ffn_post": d_norm_ffn_post}
    dact = _mm(df, wt["ffn_w_down"], tb=True, name="mm_down_dx")
    gr["ffn_w_down"] = _mm(act, df, ta=True, name="mm_down_dw")
    dzg, dzv, gr["ffn_conv_w"], gr["ffn_conv_b"] = _conv_bwd(z, dact, wt["ffn_conv_w"], wt["ffn_conv_b"])
    dz = jnp.concatenate([dzg, dzv], axis=1)
    dh2 = _mm(dz, wt["ffn_w_up"], name="mm_up_dx")
    gr["ffn_w_up"] = _mm(dz, h2, ta=True, name="mm_up_dw")
    dx2, dmixed, gr["norm_mix_post"], gr["norm_ffn_pre"] = _mid_bwd(x2, mixed, dh2, dx3, wt["norm_mix_post"], wt["norm_ffn_pre"])
    dmi = _mm(dmixed, wt["w_out"], tb=True, name="mm_out_dx")
    gr["w_out"] = _mm(mi, dmixed, ta=True, name="mm_out_dw")
    token = send_early({n: gr[n] for n in EARLY_GRADS})
    b_r = b_r + jnp.tile(token[0:1, :], (1, D_MODEL // LANES))
    dgp_r, dgp_s, do_r, do_s, db_r, db_s = _merge_bwd(gpre, o_r, o_s, dmi, b_r, b_s)
    gr["b_gate"] = jnp.concatenate([db_r, db_s], axis=1)
    dout5 = _mm(do_s, wt["w_branch_s5"], name="mm_branch_s5_dx")
    gr["w_branch_s5"] = _mm(do_s, out5, ta=True, name="mm_branch_s5_dw")
    dys_a, dt_glu, gr["s5_b_glu"] = _glu_bwd(ys, t_glu, dout5, wt["s5_b_glu"])
    dys_b = _mm(dt_glu, wt["s5_w_glu"], tb=True, name="mm_glu_dx")
    gr["s5_w_glu"] = _mm(ys, dt_glu, ta=True, name="mm_glu_dw")
    du, dbblk, dcblk, dabr, dabi, gr["s5_d"] = _s5_bwd(u, dys_a, dys_b, kre, kim, bblk, cblk, abr, abi, s5_d)
    gr["s5_c_re"] = _s5_blockdiag_out_t(dcblk[:, :S5_BN, :]).reshape(wt["s5_c_re"].shape)
    gr["s5_c_im"] = (-_s5_blockdiag_out_t(dcblk[:, S5_BN:, :])).reshape(wt["s5_c_im"].shape)
    dbbr, dbbi = _s5_blockdiag_in_t(dbblk[:, :, :S5_BN]), _s5_blockdiag_in_t(dbblk[:, :, S5_BN:])
    gsel = (lax.broadcasted_iota(jnp.int32, (S5N, LANES), 0) // S5P == lax.broadcasted_iota(jnp.int32, (S5N, LANES), 1)).astype(F32)
    d_are, d_aim, d_ls, d_bre_t, d_bim_t = _s5_prep_bwd(a_re, a_im, ls, b_re_t, b_im_t, (dabr, dabi, dbbr, dbbi), gsel)
    gr["s5_a_re"] = d_are.reshape(wt["s5_a_re"].shape)
    gr["s5_a_im"] = d_aim.reshape(wt["s5_a_im"].shape)
    gr["s5_log_step"] = d_ls[0:1, :S5G]
    gr["s5_b_re"] = d_bre_t.T.reshape(wt["s5_b_re"].shape)
    gr["s5_b_im"] = d_bim_t.T.reshape(wt["s5_b_im"].shape)
    dy_r = _mm(do_r, wt["w_branch_rwkv"], name="mm_branch_rwkv_dx")
    gr["w_branch_rwkv"] = _mm(do_r, y_r, ta=True, name="mm_branch_rwkv_dw")
    dy, dr1, dk1, dv1, dg, gr["rwkv_lnx_w"], gr["rwkv_lnx_b"], d_rk = _rwkv_post_bwd(y, r, k2, v, g, dy_r, post_prm, o512)
    gr["rwkv_r_k"] = d_rk.reshape(wt["rwkv_r_k"].shape)
    dr2, ddecay, dk2, daa, dbb, dv2 = _wkv_bwd(r, decay, k2, aa, bb, dy, sst, sast, vst, s_fin)
    cots = [dr1 + dr2, ddecay, dk1 + dk2, dv1 + dv2, daa, dbb, dg]
    dp_rw, gr["rwkv_shift_mu"], gr["rwkv_w0"], gr["rwkv_a0"], gr["rwkv_k_k"], gr["rwkv_k_a"], dw2p, da2p, dg2p = \
        _rwkv_pre_bwd(p_rw, cots, mu, pre_small, o512)
    gr["rwkv_w2"], gr["rwkv_a2"], gr["rwkv_g2"] = dw2p[:64], da2p[64:128], dg2p[128:]
    dproj = jnp.concatenate([dp_rw, du, dgp_r, dgp_s], axis=1)
    gr["w_in"] = _mm(dproj, h1, ta=True, name="mm_proj_dw")
    token = send_rest({n: gr[n] for n in REST_GRADS})
    dh1 = _mm(dproj, w_in + token[0, 0].astype(w_in.dtype), name="mm_proj_dx")
    dx, gr["norm_mix_pre"] = _rms_bwd(x, wt["norm_mix_pre"], dh1, dx2, "rms1_bwd")
    return loss_part[0, 0], dx, gr


def kernel(x, norm_mix_pre, norm_mix_post, norm_ffn_pre, norm_ffn_post, w_in, b_gate, rwkv_shift_mu, rwkv_w0, rwkv_w2, rwkv_a0, rwkv_a2, rwkv_g2, rwkv_k_k, rwkv_k_a, rwkv_r_k, rwkv_lnx_w, rwkv_lnx_b, s5_a_re, s5_a_im, s5_b_re, s5_b_im, s5_c_re, s5_c_im, s5_d, s5_log_step, s5_w_glu, s5_b_glu, w_branch_rwkv, w_branch_s5, w_out, ffn_w_up, ffn_conv_w, ffn_conv_b, ffn_w_down, loss_target, m_norm_mix_pre, m_norm_mix_post, m_norm_ffn_pre, m_norm_ffn_post, m_w_in, m_b_gate, m_rwkv_shift_mu, m_rwkv_w0, m_rwkv_w2, m_rwkv_a0, m_rwkv_a2, m_rwkv_g2, m_rwkv_k_k, m_rwkv_k_a, m_rwkv_r_k, m_rwkv_lnx_w, m_rwkv_lnx_b, m_s5_a_re, m_s5_a_im, m_s5_b_re, m_s5_b_im, m_s5_c_re, m_s5_c_im, m_s5_d, m_s5_log_step, m_s5_w_glu, m_s5_b_glu, m_w_branch_rwkv, m_w_branch_s5, m_w_out, m_ffn_w_up, m_ffn_conv_w, m_ffn_conv_b, m_ffn_w_down, v_norm_mix_pre, v_norm_mix_post, v_norm_ffn_pre, v_norm_ffn_post, v_w_in, v_b_gate, v_rwkv_shift_mu, v_rwkv_w0, v_rwkv_w2, v_rwkv_a0, v_rwkv_a2, v_rwkv_g2, v_rwkv_k_k, v_rwkv_k_a, v_rwkv_r_k, v_rwkv_lnx_w, v_rwkv_lnx_b, v_s5_a_re, v_s5_a_im, v_s5_b_re, v_s5_b_im, v_s5_c_re, v_s5_c_im, v_s5_d, v_s5_log_step, v_s5_w_glu, v_s5_b_glu, v_w_branch_rwkv, v_w_branch_s5, v_w_out, v_ffn_w_up, v_ffn_conv_w, v_ffn_conv_b, v_ffn_w_down):
    args = dict(locals())
    wgt = {n: args[n] for n in WEIGHTS}
    mom = {n: args["m_" + n] for n in WEIGHTS}
    var = {n: args["v_" + n] for n in WEIGHTS}
    me = 4 * lax.axis_index("x") + 2 * lax.axis_index("y") + lax.axis_index("c")
    shard_shapes = {n: wgt[n].shape[1:] for n in SHARDED}
    sent_shapes = {n: (shard_shapes[n][::-1] if n in TRANSPOSED else shard_shapes[n]) for n in SHARDED}
    sent = lambda n, a: a.T if n in TRANSPOSED else a

    unshard = lambda n, blocks: _cols_to_full(blocks) if n in SMALL_SHARDED else _rows_to_full(blocks)

    first = ("w_in", "rwkv_w2", "rwkv_a2", "rwkv_g2")
    late = tuple(n for n in SHARDED if n not in first and n not in F32_GATHER)
    late_payload = _pack([sent(n, wgt[n][0].astype(BF16)) for n in late])
    late_started = _exchange_start(late_payload, False, "gather_late_start")
    got_first = _unpack(_all_gather(_pack([sent(n, wgt[n][0].astype(BF16)) for n in first]), "all_gather_weights"),
                        [sent_shapes[n] for n in first])
    got_f32 = _unpack(_all_gather(_pack([wgt[n][0] for n in F32_GATHER]), "all_gather_taps"),
                      [sent_shapes[n] for n in F32_GATHER])
    wt = {n: wgt[n] for n in REPLICATED}
    for n, blocks in zip(first + F32_GATHER, got_first + got_f32, strict=True):
        wt[n] = unshard(n, blocks)

    def get_late(after):
        land = _exchange_wait(late_started, after, False, "gather_late_wait")
        blocks = _unpack(lax.dynamic_update_index_in_dim(land, late_payload, me, 0), [sent_shapes[n] for n in late])
        return {n: unshard(n, b_) for n, b_ in zip(late, blocks, strict=True)}

    def pack_chunks(names, grads_):
        pieces = [_pad_cols(grads_[n].reshape(N_DEV, -1).astype(BF16), PACK_ALIGN) for n in names]
        return _pad_cols(jnp.concatenate(pieces, axis=1), PACK_ROWS * LANES).reshape(N_DEV, -1, LANES)

    sent_grads = {}

    def sender(key, names):
        def send(grads_):
            chunks = pack_chunks(names, grads_)
            sent_grads[key] = (_exchange_start(chunks, True, "exchange_" + key + "_start"), chunks)
            return sent_grads[key][0][4]
        return send

    loss_part, dx, gr = _local_step(x[0], loss_target[0], wt, get_late, sender("early", EARLY_GRADS), sender("rest", REST_GRADS))

    whole = SMALL_SHARDED + REPLICATED

    def ended(key, after):
        started, chunks = sent_grads[key]
        land = _exchange_wait(started, after, True, "exchange_" + key + "_wait")
        return _sum_parts(land, "sum_" + key + "_grads", own=lax.dynamic_index_in_dim(chunks, me, 0, keepdims=False))

    early_sum, rest_sum = ended("early", dx), ended("rest", dx)
    small_sum = _sum_parts(_all_gather(_pack([gr[n] for n in whole]), "gather_small_grads"), "sum_small_grads")
    got = dict(zip(EARLY_GRADS + REST_GRADS + whole,
                   _unpack(early_sum, [sent_shapes[n] for n in EARLY_GRADS]) + _unpack(rest_sum, [sent_shapes[n] for n in REST_GRADS])
                   + _unpack(small_sum, [gr[n].shape for n in whole]), strict=True))
    grads = {}
    for n in WEIGHTS:
        if n in TRANSPOSED:
            grads[n] = got[n].T
        elif n in SMALL_SHARDED:
            cols = shard_shapes[n][1]
            grads[n] = lax.dynamic_slice_in_dim(got[n], me * cols, cols, axis=1)
        else:
            grads[n] = got[n]
        grads[n] = grads[n].reshape(wgt[n].shape)

    pack_local = lambda src: _pack([src[n].reshape(-1) for n in WEIGHTS])
    outs = _adamw(pack_local(grads), pack_local(wgt), pack_local(mom), pack_local(var))
    shapes = [wgt[n].shape for n in WEIGHTS]
    loss = lax.psum(loss_part, AXES)
    return (loss, dx[None], *[grads[n] for n in WEIGHTS], *_unpack(outs[0], shapes), *_unpack(outs[1], shapes),
            *_unpack(outs[2], shapes))
```

```python
import functools
import math

import jax
import jax.numpy as jnp
from jax import lax
from jax.experimental import pallas as pl
from jax.experimental.pallas import tpu as pltpu

F32 = jnp.float32
BF16 = jnp.bfloat16
SDS = jax.ShapeDtypeStruct
HI = lax.Precision.HIGHEST
MESH = pl.DeviceIdType.MESH
AXES = ("x", "y", "c")
N_DEV = 8

D_MODEL = 1024
RW = 512
HN = 64
N_RWKV = 1792
S5W = 512
S5G = 32
S5C = 16
S5P = 64
S5N = S5G * S5P
D_FF = 2816
NORM_EPS = 1e-6
LNX_EPS = 64e-5

ADAM_LR = 0.001
ADAM_B1 = 0.9
ADAM_B2 = 0.999
ADAM_EPS = 1e-08
ADAM_WD = 0.01
ADAM_STEP = 10

LANES = 128
SUBLANES = 8
VMEM_LIMIT = 56 * 1024 * 1024
WKV_CHUNK = 32


def _params(sem=("arbitrary",)):
    return pltpu.CompilerParams(dimension_semantics=sem, vmem_limit_bytes=VMEM_LIMIT)


def _pick(n, cap):
    best = None
    for t in range(LANES, min(n, cap) + 1, LANES):
        if n % t == 0:
            best = t
    return best or n


def _mm(a, b, *, ta=False, tb=False, out_dtype=F32, name):
    m = a.shape[1] if ta else a.shape[0]
    k = a.shape[0] if ta else a.shape[1]
    n = b.shape[0] if tb else b.shape[1]
    assert (b.shape[1] if tb else b.shape[0]) == k
    tm, tn, tk = _pick(m, 1024), _pick(n, 2304), _pick(k, 512)
    nk = k // tk
    dims = (((0 if ta else 1,), (1 if tb else 0,)), ((), ()))

    def body(a_ref, b_ref, o_ref, acc_ref):
        kk = pl.program_id(2)

        @pl.when(kk == 0)
        def _():
            acc_ref[...] = jnp.zeros_like(acc_ref)

        acc_ref[...] += lax.dot_general(a_ref[...].astype(BF16), b_ref[...].astype(BF16), dims,
                                        preferred_element_type=F32)

        @pl.when(kk == nk - 1)
        def _():
            o_ref[...] = acc_ref[...].astype(o_ref.dtype)

    a_spec = pl.BlockSpec((tk, tm), lambda i, j, kk: (kk, i)) if ta else pl.BlockSpec((tm, tk), lambda i, j, kk: (i, kk))
    b_spec = pl.BlockSpec((tn, tk), lambda i, j, kk: (j, kk)) if tb else pl.BlockSpec((tk, tn), lambda i, j, kk: (kk, j))
    return pl.pallas_call(
        body, grid=(m // tm, n // tn, nk), in_specs=[a_spec, b_spec],
        out_specs=pl.BlockSpec((tm, tn), lambda i, j, kk: (i, j)),
        out_shape=SDS((m, n), out_dtype), scratch_shapes=[pltpu.VMEM((tm, tn), F32)],
        compiler_params=_params(("parallel", "parallel", "arbitrary")), name=name)(a, b)


def _rows(fn, rows, params, out_rows, out_accs, *, name, tl, reverse=False, scratch=()):
    first = rows[0][0] if isinstance(rows[0], tuple) else rows[0]
    length = first.shape[0]
    tl = min(tl, length)
    nt = length // tl
    rmap = (lambda i: nt - 1 - i) if reverse else (lambda i: i)
    specs, arrs = [], []
    for r in rows:
        arr, wdt, cb = r if isinstance(r, tuple) else (r, r.shape[1], 0)
        specs.append(pl.BlockSpec((tl, wdt), lambda i, cb=cb: (rmap(i), cb)))
        arrs.append(arr)
    for p in params:
        specs.append(pl.BlockSpec(p.shape, lambda i, nd=p.ndim: (0,) * nd))
        arrs.append(p)
    out_shape = [SDS((length, c), dt) for c, dt in out_rows] + [SDS(s, F32) for s in out_accs]
    out_specs = [pl.BlockSpec((tl, c), lambda i: (rmap(i), 0)) for c, _ in out_rows]
    out_specs += [pl.BlockSpec(s, lambda i, nd=len(s): (0,) * nd) for s in out_accs]
    nr, npar, nor, noa = len(rows), len(params), len(out_rows), len(out_accs)

    def body(*refs):
        rin, pin = refs[:nr], refs[nr:nr + npar]
        rout = refs[nr + npar:nr + npar + nor]
        aout = refs[nr + npar + nor:nr + npar + nor + noa]
        scr = refs[nr + npar + nor + noa:]
        step = pl.program_id(0)
        outs_r, outs_a = fn(step, [r[...] for r in rin], [p[...] for p in pin], scr)
        for ref, val in zip(rout, outs_r, strict=True):
            ref[...] = val.astype(ref.dtype)

        @pl.when(step == 0)
        def _():
            for ref in aout:
                ref[...] = jnp.zeros_like(ref)

        for ref, val in zip(aout, outs_a, strict=True):
            ref[...] += val.astype(F32)

    res = pl.pallas_call(body, grid=(nt,), in_specs=specs, out_specs=out_specs, out_shape=out_shape,
                         scratch_shapes=list(scratch), compiler_params=_params(), name=name)(*arrs)
    return list(res)


def _rms(x, g):
    return x * lax.rsqrt(jnp.mean(x * x, axis=-1, keepdims=True) + NORM_EPS) * g


def _sig(x):
    return 0.5 * (jnp.tanh(0.5 * x) + 1.0)


def _softplus(x):
    return jnp.maximum(x, 0.0) + jnp.log(1.0 + jnp.exp(-jnp.abs(x)))


def _gelu(x):
    return x * (0.5 * (1.0 + jnp.tanh(math.sqrt(2.0 / math.pi) * (x + 0.044715 * (x * x * x)))))


def _bdot(a, b):
    return jnp.dot(a.astype(BF16), b.astype(BF16), preferred_element_type=F32)


def _hdot(a, b):
    return jnp.dot(a, b, precision=HI, preferred_element_type=F32)


def _segsum_impl(x, ones):
    hi = x.astype(BF16)
    lo = (x - hi.astype(F32)).astype(BF16)
    ones = ones.astype(BF16)
    return jnp.dot(jnp.concatenate([hi, lo], axis=1), jnp.concatenate([ones, ones], axis=0), preferred_element_type=F32)


@jax.custom_vjp
def _segsum(x, ones):
    return _segsum_impl(x, ones)


_segsum.defvjp(lambda x, ones: (_segsum_impl(x, ones), ones),
               lambda ones, g: (_segsum_impl(g, ones), jnp.zeros_like(ones)))


def _block_ones(n, blk):
    i = lax.broadcasted_iota(jnp.int32, (n, n), 0) // blk
    j = lax.broadcasted_iota(jnp.int32, (n, n), 1) // blk
    return (i == j).astype(F32)


def _rms_fwd(x, g, name):
    return _rows(lambda s, r, p, _: ([_rms(r[0], p[0])], []), [x], [g], [(x.shape[1], BF16)], [], name=name, tl=512)[0]


def _rms_bwd(x, g, dh, dres, name):
    def fn(s, r, p, _):
        _, vjp = jax.vjp(_rms, r[0], p[0])
        dx, dg = vjp(r[1])
        return [dx + r[2]], [dg]
    return _rows(fn, [x, dh, dres], [g], [(x.shape[1], F32)], [g.shape], name=name, tl=256)


def _rwkv_pre_math(k_, lr, w0, a0, k_k, k_a, w2p, a2p, g2p, o512):
    pre_w = w0 + _bdot(jnp.tanh(lr), w2p)
    w = -_softplus(-pre_w) - 0.5
    decay = jnp.exp(-jnp.exp(w))
    a = _sig(a0 + _bdot(lr, a2p))
    g = _bdot(_sig(lr), g2p)
    kr = k_ * k_k
    kk = kr / jnp.maximum(jnp.sqrt(_segsum(kr * kr, o512)), 1e-12)
    k2 = k_ * (1.0 + (a - 1.0) * k_a)
    return decay, k2, -kk, kk * a, g


def _shift_down(p, prev_row):
    row = lax.broadcasted_iota(jnp.int32, p.shape, 0)
    return jnp.where(row == 0, jnp.broadcast_to(prev_row, p.shape), pltpu.roll(p, 1, 0))


def _shift_up(q, next_row):
    n = q.shape[0]
    row = lax.broadcasted_iota(jnp.int32, q.shape, 0)
    return jnp.where(row == n - 1, jnp.broadcast_to(next_row, q.shape), pltpu.roll(q, n - 1, 0))


def _rwkv_pre_fwd(p, mu, small, o512):
    def fn(step, r, prm, scr):
        car = scr[0]

        @pl.when(step == 0)
        def _():
            car[...] = jnp.zeros_like(car)

        x = r[0]
        prev = _shift_down(x, car[SUBLANES - 1:SUBLANES, :])
        car[...] = x[x.shape[0] - SUBLANES:, :]
        xs = x + (prev - x) * prm[0]
        decay, k2, aa, bb, g = _rwkv_pre_math(xs[:, RW:2 * RW], xs[:, 3 * RW:], *prm[1:])
        return [xs[:, :RW], decay, k2, xs[:, 2 * RW:3 * RW], aa, bb, g], []
    return _rows(fn, [p], [mu, *small, o512], [(RW, F32)] * 7, [], name="rwkv_pre_fwd", tl=256,
                 scratch=[pltpu.VMEM((SUBLANES, N_RWKV), F32)])


def _rwkv_pre_bwd(p, cots, mu, small, o512):
    length = p.shape[0]
    tl = min(256, length)
    nt = length // tl
    rows_per = tl // SUBLANES
    params = [mu, *small, o512]
    acc_shapes = [mu.shape] + [q.shape for q in small]
    nr, npar, nacc = 2 + len(cots), len(params), len(acc_shapes)

    def body(*refs):
        rin, pin = refs[:nr], refs[nr:nr + npar]
        dp_ref = refs[nr + npar]
        aout = refs[nr + npar + 1:nr + npar + 1 + nacc]
        car_q = refs[nr + npar + 1 + nacc]
        step = pl.program_id(0)

        @pl.when(step == 0)
        def _():
            car_q[...] = jnp.zeros_like(car_q)
            for ref in aout:
                ref[...] = jnp.zeros_like(ref)

        x = rin[0][...]
        prev_row = jnp.where(step == nt - 1, 0.0, rin[1][SUBLANES - 1:SUBLANES, :])
        dr, ddecay, dk2, dv, daa, dbb, dg = [r[...] for r in rin[2:]]
        prm = [q[...] for q in pin]
        mu_, o512_ = prm[0], prm[-1]
        prev = _shift_down(x, prev_row)
        xs = x + (prev - x) * mu_
        _, vjp = jax.vjp(lambda k_, lr, *w: _rwkv_pre_math(k_, lr, *w, o512_), xs[:, RW:2 * RW], xs[:, 3 * RW:], *prm[1:-1])
        dk_, dlr, *dsmall = vjp((ddecay, dk2, daa, dbb, dg))
        dxs = jnp.concatenate([dr, dk_, dv, dlr], axis=1)
        q = dxs * mu_
        dp_ref[...] = (dxs - q + _shift_up(q, car_q[0:1, :])).astype(dp_ref.dtype)
        car_q[...] = q[:SUBLANES, :]
        aout[0][...] += jnp.sum((prev - x) * dxs, axis=0, keepdims=True)
        for ref, val in zip(aout[1:], dsmall, strict=True):
            ref[...] += val

    rmap = lambda i: nt - 1 - i
    specs = [pl.BlockSpec((tl, N_RWKV), lambda i: (rmap(i), 0)),
             pl.BlockSpec((SUBLANES, N_RWKV), lambda i: (jnp.maximum(rmap(i) * rows_per - 1, 0), 0))]
    specs += [pl.BlockSpec((tl, RW), lambda i: (rmap(i), 0)) for _ in cots]
    specs += [pl.BlockSpec(q.shape, lambda i, nd=q.ndim: (0,) * nd) for q in params]
    out_shape = [SDS((length, N_RWKV), BF16)] + [SDS(sh, F32) for sh in acc_shapes]
    out_specs = [pl.BlockSpec((tl, N_RWKV), lambda i: (rmap(i), 0))]
    out_specs += [pl.BlockSpec(sh, lambda i, nd=len(sh): (0,) * nd) for sh in acc_shapes]
    res = pl.pallas_call(body, grid=(nt,), in_specs=specs, out_specs=out_specs, out_shape=out_shape,
                         scratch_shapes=[pltpu.VMEM((SUBLANES, N_RWKV), F32)],
                         compiler_params=_params(), name="rwkv_pre_bwd")(p, p, *cots, *params)
    return list(res)


def _rwkv_post_math(y, r, k2, v, g, lnx_w, lnx_b, r_k, o512):
    mean = _segsum(y, o512) * (1.0 / HN)
    yc = y - mean
    var = _segsum(yc * yc, o512) * (1.0 / HN)
    yn = yc * lax.rsqrt(var + LNX_EPS) * lnx_w + lnx_b
    bonus = _segsum(r * k2 * r_k, o512) * v
    return (yn + bonus) * g


def _rwkv_post_fwd(y, r, k2, v, g, prm, o512):
    return _rows(lambda s, rr, p, _: ([_rwkv_post_math(*rr, *p)], []), [y, r, k2, v, g], [*prm, o512],
                 [(RW, BF16)], [], name="rwkv_post_fwd", tl=256)[0]


def _rwkv_post_bwd(y, r, k2, v, g, dout, prm, o512):
    def fn(s, rr, p, _):
        o = p[-1]
        _, vjp = jax.vjp(lambda *a: _rwkv_post_math(*a, o), *rr[:5], *p[:-1])
        gr = vjp(rr[5])
        return list(gr[:5]), list(gr[5:])
    return _rows(fn, [y, r, k2, v, g, dout], [*prm, o512], [(RW, F32)] * 5, [q.shape for q in prm],
                 name="rwkv_post_bwd", tl=256)


def _glu_math(ys, t, b):
    return ys * _sig(t + b)


def _glu_fwd(ys, t, b):
    return _rows(lambda s, r, p, _: ([_glu_math(r[0], r[1], p[0])], []), [ys, t], [b], [(S5W, BF16)], [],
                 name="s5_glu_fwd", tl=512)[0]


def _glu_bwd(ys, t, dout, b):
    def fn(s, r, p, _):
        _, vjp = jax.vjp(_glu_math, r[0], r[1], p[0])
        dys, dt, db = vjp(r[2])
        return [dys, dt], [db]
    return _rows(fn, [ys, t, dout], [b], [(S5W, F32), (S5W, BF16)], [b.shape], name="s5_glu_bwd", tl=512)


def _merge_math(gp_r, gp_s, o_r, o_s, b_r, b_s):
    return _sig(gp_r + b_r) * o_r + _sig(gp_s + b_s) * o_s


def _merge_fwd(gpre, o_r, o_s, b_r, b_s):
    return _rows(lambda s, r, p, _: ([_merge_math(*r, *p)], []),
                 [(gpre, D_MODEL, 0), (gpre, D_MODEL, 1), o_r, o_s], [b_r, b_s], [(D_MODEL, BF16)], [],
                 name="merge_fwd", tl=256)[0]


def _merge_bwd(gpre, o_r, o_s, dmi, b_r, b_s):
    def fn(s, r, p, _):
        _, vjp = jax.vjp(_merge_math, *r[:4], *p)
        dgr, dgs, dor, dos, dbr, dbs = vjp(r[4])
        return [dgr, dgs, dor, dos], [dbr, dbs]
    return _rows(fn, [(gpre, D_MODEL, 0), (gpre, D_MODEL, 1), o_r, o_s, dmi], [b_r, b_s], [(D_MODEL, BF16)] * 4,
                 [b_r.shape, b_s.shape], name="merge_bwd", tl=256)


def _mid_fwd(x, mixed, g_post, g_pre):
    def fn(s, r, p, _):
        x2 = r[0] + _rms(r[1], p[0])
        return [x2, _rms(x2, p[1])], []
    return _rows(fn, [x, mixed], [g_post, g_pre], [(D_MODEL, F32), (D_MODEL, BF16)], [], name="mid_fwd", tl=256)


def _mid_bwd(x2, mixed, dh2, dx3, g_post, g_pre):
    def fn(s, r, p, _):
        _, vjp1 = jax.vjp(_rms, r[0], p[1])
        dx2, dg_pre = vjp1(r[2])
        dx2 = dx2 + r[3]
        _, vjp2 = jax.vjp(_rms, r[1], p[0])
        dmixed, dg_post = vjp2(dx2)
        return [dx2, dmixed], [dg_post, dg_pre]
    return _rows(fn, [x2, mixed, dh2, dx3], [g_post, g_pre], [(D_MODEL, F32), (D_MODEL, BF16)], [g_post.shape, g_pre.shape],
                 name="mid_bwd", tl=256)


def _final(x2, f, tgt, g_post):
    def fn(s, r, p, _):
        y, vjp = jax.vjp(_rms, r[1], p[0])
        diff = r[0] + y - r[2]
        dx3 = diff * (1.0 / D_MODEL)
        df, dg = vjp(dx3)
        part = 0.5 * jnp.sum(jnp.sum(diff * diff, axis=1, keepdims=True), axis=0, keepdims=True) * (1.0 / D_MODEL)
        return [dx3, df], [dg, jnp.broadcast_to(part, (1, LANES))]
    return _rows(fn, [x2, f, tgt], [g_post], [(D_MODEL, F32), (D_MODEL, BF16)], [g_post.shape, (1, LANES)], name="final", tl=256)


def _conv_taps(z, car):
    row = lax.broadcasted_iota(jnp.int32, z.shape, 0)
    z1 = jnp.where(row == 0, jnp.broadcast_to(car[7:8, :], z.shape), pltpu.roll(z, 1, 0))
    z2 = pltpu.roll(z, 2, 0)
    z2 = jnp.where(row == 0, jnp.broadcast_to(car[6:7, :], z.shape), z2)
    z2 = jnp.where(row == 1, jnp.broadcast_to(car[7:8, :], z.shape), z2)
    return z1, z2


def _conv(z, car, w, b):
    z1, z2 = _conv_taps(z, car)
    return b + w[0:1, :] * z2 + w[1:2, :] * z1 + w[2:3, :] * z, z1, z2


def _conv_fwd(z, conv_w, conv_b):
    length = z.shape[0]
    tl = min(256, length)
    nt = length // tl
    tc = _pick(D_FF, 1536)
    nb = D_FF // tc

    def body(zg_ref, zv_ref, wg_ref, wv_ref, bg_ref, bv_ref, o_ref, cg, cv):
        @pl.when(pl.program_id(1) == 0)
        def _():
            cg[...] = jnp.zeros_like(cg)
            cv[...] = jnp.zeros_like(cv)

        zg, zv = zg_ref[...], zv_ref[...]
        gate, _, _ = _conv(zg, cg[...], wg_ref[...], bg_ref[...])
        val, _, _ = _conv(zv, cv[...], wv_ref[...], bv_ref[...])
        cg[...] = zg[tl - SUBLANES:, :]
        cv[...] = zv[tl - SUBLANES:, :]
        o_ref[...] = (_gelu(gate) * val).astype(o_ref.dtype)

    zspec = lambda off: pl.BlockSpec((tl, tc), lambda j, i: (i, j + off))
    wspec = lambda off, r: pl.BlockSpec((r, tc), lambda j, i: (0, j + off))
    return pl.pallas_call(
        body, grid=(nb, nt),
        in_specs=[zspec(0), zspec(nb), wspec(0, 3), wspec(nb, 3), wspec(0, 1), wspec(nb, 1)],
        out_specs=pl.BlockSpec((tl, tc), lambda j, i: (i, j)), out_shape=SDS((length, D_FF), BF16),
        scratch_shapes=[pltpu.VMEM((SUBLANES, tc), F32)] * 2,
        compiler_params=_params(("arbitrary", "arbitrary")), name="conv_fwd")(z, z, conv_w, conv_w, conv_b, conv_b)


def _conv_bwd(z, dact, conv_w, conv_b):
    length = z.shape[0]
    tl = min(256, length)
    nt = length // tl
    tc = _pick(D_FF, 1536)
    nb = D_FF // tc
    rows_per = tl // SUBLANES

    def half_bwd(dzc, z, z1, z2, w, dcar):
        n = tl
        row = lax.broadcasted_iota(jnp.int32, dzc.shape, 0)
        u1 = jnp.where(row == n - 1, jnp.broadcast_to(dcar[0:1, :], dzc.shape), pltpu.roll(dzc, n - 1, 0))
        u2 = pltpu.roll(dzc, n - 2, 0)
        u2 = jnp.where(row == n - 2, jnp.broadcast_to(dcar[0:1, :], dzc.shape), u2)
        u2 = jnp.where(row == n - 1, jnp.broadcast_to(dcar[1:2, :], dzc.shape), u2)
        dz = w[2:3, :] * dzc + w[1:2, :] * u1 + w[0:1, :] * u2
        dw = jnp.concatenate([jnp.sum(dzc * z2, axis=0, keepdims=True), jnp.sum(dzc * z1, axis=0, keepdims=True),
                              jnp.sum(dzc * z, axis=0, keepdims=True)], axis=0)
        return dz, dw, jnp.sum(dzc, axis=0, keepdims=True)

    def body(zg_ref, zv_ref, pg_ref, pv_ref, da_ref, wg_ref, wv_ref, bg_ref, bv_ref,
             dzg_ref, dzv_ref, dwg_ref, dwv_ref, dbg_ref, dbv_ref, cg, cv):
        step = pl.program_id(1)

        @pl.when(step == 0)
        def _():
            cg[...] = jnp.zeros_like(cg)
            cv[...] = jnp.zeros_like(cv)
            for ref in (dwg_ref, dwv_ref, dbg_ref, dbv_ref):
                ref[...] = jnp.zeros_like(ref)

        is_first_tile = step == nt - 1
        zg, zv = zg_ref[...], zv_ref[...]
        pg = jnp.where(is_first_tile, 0.0, pg_ref[...])
        pv = jnp.where(is_first_tile, 0.0, pv_ref[...])
        wg, wv = wg_ref[...], wv_ref[...]
        gate, zg1, zg2 = _conv(zg, pg, wg, bg_ref[...])
        val, zv1, zv2 = _conv(zv, pv, wv, bv_ref[...])
        act_g, vjp = jax.vjp(_gelu, gate)
        da = da_ref[...]
        dgate = vjp(da * val)[0]
        dval = da * act_g
        dzg, dwg, dbg = half_bwd(dgate, zg, zg1, zg2, wg, cg[...])
        dzv, dwv, dbv = half_bwd(dval, zv, zv1, zv2, wv, cv[...])
        cg[...] = dgate[:SUBLANES, :]
        cv[...] = dval[:SUBLANES, :]
        dzg_ref[...] = dzg.astype(dzg_ref.dtype)
        dzv_ref[...] = dzv.astype(dzv_ref.dtype)
        dwg_ref[...] += dwg
        dwv_ref[...] += dwv
        dbg_ref[...] += dbg
        dbv_ref[...] += dbv

    rmap = lambda i: nt - 1 - i
    zspec = lambda off: pl.BlockSpec((tl, tc), lambda j, i: (rmap(i), j + off))
    pspec = lambda off: pl.BlockSpec((SUBLANES, tc), lambda j, i: (jnp.maximum(rmap(i) * rows_per - 1, 0), j + off))
    wspec = lambda off, r: pl.BlockSpec((r, tc), lambda j, i: (0, j + off))
    out_w = lambda r: pl.BlockSpec((r, tc), lambda j, i: (0, j))
    dzg, dzv, dwg, dwv, dbg, dbv = pl.pallas_call(
        body, grid=(nb, nt),
        in_specs=[zspec(0), zspec(nb), pspec(0), pspec(nb), pl.BlockSpec((tl, tc), lambda j, i: (rmap(i), j)),
                  wspec(0, 3), wspec(nb, 3), wspec(0, 1), wspec(nb, 1)],
        out_specs=[pl.BlockSpec((tl, tc), lambda j, i: (rmap(i), j))] * 2 + [out_w(3), out_w(3), out_w(1), out_w(1)],
        out_shape=[SDS((length, D_FF), BF16)] * 2 + [SDS((3, D_FF), F32)] * 2 + [SDS((1, D_FF), F32)] * 2,
        scratch_shapes=[pltpu.VMEM((SUBLANES, tc), F32)] * 2,
        compiler_params=_params(("arbitrary", "arbitrary")), name="conv_bwd")(z, z, z, z, dact, conv_w, conv_w, conv_b, conv_b)
    return dzg, dzv, jnp.concatenate([dwg, dwv], axis=1), jnp.concatenate([dbg, dbv], axis=1)


def _s5_prep_math(a_re, a_im, ls, b_re, b_im):
    dt = jnp.exp(ls)
    er = jnp.exp(a_re * dt)
    ph = a_im * dt
    abr, abi = er * jnp.cos(ph), er * jnp.sin(ph)
    den = a_re * a_re + a_im * a_im
    nr = abr - 1.0
    cr = (nr * a_re + abi * a_im) / den
    ci = (abi * a_re - nr * a_im) / den
    return abr, abi, cr * b_re - ci * b_im, cr * b_im + ci * b_re


def _s5_prep_fwd(a_re, a_im, ls, b_re, b_im):
    def body(ar, ai, l, br, bi, o1, o2, o3, o4):
        for ref, val in zip((o1, o2, o3, o4), _s5_prep_math(ar[...], ai[...], l[...], br[...], bi[...]), strict=True):
            ref[...] = val
    return pl.pallas_call(body, out_shape=[SDS((1, S5N), F32)] * 2 + [SDS((S5C, S5N), F32)] * 2,
                          name="s5_prep_fwd")(a_re, a_im, ls, b_re, b_im)


def _s5_prep_bwd(a_re, a_im, ls, b_re, b_im, cots, gsel):
    def body(ar, ai, l, br, bi, c1, c2, c3, c4, g_ref, o1, o2, o3, o4, o5):
        _, vjp = jax.vjp(_s5_prep_math, ar[...], ai[...], l[...], br[...], bi[...])
        d_ar, d_ai, d_ls, d_br, d_bi = vjp((c1[...], c2[...], c3[...], c4[...]))
        o1[...] = d_ar
        o2[...] = d_ai
        o3[...] = _hdot(jnp.broadcast_to(d_ls, (SUBLANES, S5N)), g_ref[...])
        o4[...] = d_br
        o5[...] = d_bi
    return pl.pallas_call(body, out_shape=[SDS((1, S5N), F32)] * 2 + [SDS((SUBLANES, LANES), F32)] + [SDS((S5C, S5N), F32)] * 2,
                          name="s5_prep_bwd")(a_re, a_im, ls, b_re, b_im, *cots, gsel)


def _cmul(ar, ai, br, bi):
    return ar * br - ai * bi, ar * bi + ai * br


def _s5_powers(abr, abi):
    shp = (SUBLANES, S5N)
    a1 = (jnp.broadcast_to(abr, shp), jnp.broadcast_to(abi, shp))
    a2 = _cmul(*a1, *a1)
    a4 = _cmul(*a2, *a2)
    row = lax.broadcasted_iota(jnp.int32, shp, 0)
    pr, pi = a1
    cur = a1
    for i in range(1, SUBLANES):
        cur = _cmul(*cur, *a1)
        pr = jnp.where(row == i, cur[0], pr)
        pi = jnp.where(row == i, cur[1], pi)
    return a1, a2, a4, (pr, pi)


def _s5_scan(sre_ref, sim_ref, car_re, car_im, abr, abi, n_rows, reverse):
    a1, a2, a4, (pr, pi) = _s5_powers(abr, abi)
    sgn = -1.0 if reverse else 1.0
    row = lax.broadcasted_iota(jnp.int32, (SUBLANES, S5N), 0)
    if reverse:
        qr, qi = pr, pi
        for i in range(SUBLANES):
            src = SUBLANES - 1 - i
            qr = jnp.where(row == i, jnp.broadcast_to(pr[src:src + 1, :], pr.shape), qr)
            qi = jnp.where(row == i, jnp.broadcast_to(pi[src:src + 1, :], pi.shape), qi)
        pr, pi = qr, qi
    nblk = n_rows // SUBLANES

    def blk(i, carry):
        cr, ci = carry
        b = (nblk - 1 - i) if reverse else i
        sl = pl.ds(pl.multiple_of(b * SUBLANES, SUBLANES), SUBLANES)
        xr, xi = sre_ref[sl, :], sim_ref[sl, :]
        for kk, (er, ei) in ((1, a1), (2, a2), (4, a4)):
            if reverse:
                sr = jnp.where(row < SUBLANES - kk, pltpu.roll(xr, SUBLANES - kk, 0), 0.0)
                si = jnp.where(row < SUBLANES - kk, pltpu.roll(xi, SUBLANES - kk, 0), 0.0)
            else:
                sr = jnp.where(row >= kk, pltpu.roll(xr, kk, 0), 0.0)
                si = jnp.where(row >= kk, pltpu.roll(xi, kk, 0), 0.0)
            dr, di = _cmul(er, sgn * ei, sr, si)
            xr, xi = xr + dr, xi + di
        dr, di = _cmul(pr, sgn * pi, cr, ci)
        xr, xi = xr + dr, xi + di
        sre_ref[sl, :] = xr
        sim_ref[sl, :] = xi
        edge = 0 if reverse else SUBLANES - 1
        return (jnp.broadcast_to(xr[edge:edge + 1, :], xr.shape), jnp.broadcast_to(xi[edge:edge + 1, :], xi.shape))

    cr, ci = lax.fori_loop(0, nblk, blk, (car_re[...], car_im[...]))
    car_re[...] = cr
    car_im[...] = ci


S5_BLK = 4
S5_BN = S5N // S5_BLK


def _split2(x):
    hi = x.astype(BF16)
    return hi, (x - hi.astype(F32)).astype(BF16)


def _dot3(a, b, ca, cb):
    ah, al = _split2(a)
    bh, bl = _split2(b)
    return lax.dot_general(jnp.concatenate([ah, ah, al], axis=ca), jnp.concatenate([bh, bl, bh], axis=cb),
                           (((ca,), (cb,)), ((), ())), preferred_element_type=F32)


def _dot1(a, b, ca, cb):
    return lax.dot_general(a.astype(BF16), b.astype(BF16), (((ca,), (cb,)), ((), ())), preferred_element_type=F32)


def _s5_project_in(u, bblk_ref, sre_ref, sim_ref):
    for j in range(S5_BLK):
        bu = _dot1(u[:, j * LANES:(j + 1) * LANES], bblk_ref[j], 1, 0)
        sre_ref[:, j * S5_BN:(j + 1) * S5_BN] = bu[:, :S5_BN]
        sim_ref[:, j * S5_BN:(j + 1) * S5_BN] = bu[:, S5_BN:]


def _s5_project_out(u, d, cblk_ref, sre_ref, sim_ref):
    ys = []
    for j in range(S5_BLK):
        sl = slice(j * S5_BN, (j + 1) * S5_BN)
        ys.append(_dot1(sre_ref[:, sl], cblk_ref[j, :S5_BN, :], 1, 0) + _dot1(sim_ref[:, sl], cblk_ref[j, S5_BN:, :], 1, 0))
    return jnp.concatenate(ys, axis=1) + d * u


def _s5_fwd(u, bblk, cblk, abr, abi, d):
    length = u.shape[0]
    tl = min(256, length)
    nt = length // tl

    def body(u_ref, b_ref, c_ref, ar_ref, ai_ref, d_ref, ys_ref, kr_ref, ki_ref, sre, sim, car_re, car_im):
        @pl.when(pl.program_id(0) == 0)
        def _():
            car_re[...] = jnp.zeros_like(car_re)
            car_im[...] = jnp.zeros_like(car_im)

        kr_ref[0] = car_re[...]
        ki_ref[0] = car_im[...]
        u_ = u_ref[...]
        _s5_project_in(u_, b_ref, sre, sim)
        _s5_scan(sre, sim, car_re, car_im, ar_ref[...], ai_ref[...], tl, False)
        ys_ref[...] = _gelu(_s5_project_out(u_, d_ref[...], c_ref, sre, sim))

    full = lambda a: pl.BlockSpec(a.shape, lambda i, nd=a.ndim: (0,) * nd)
    chk = pl.BlockSpec((1, SUBLANES, S5N), lambda i: (i, 0, 0))
    return pl.pallas_call(
        body, grid=(nt,), in_specs=[pl.BlockSpec((tl, S5W), lambda i: (i, 0)), full(bblk), full(cblk), full(abr), full(abi), full(d)],
        out_specs=[pl.BlockSpec((tl, S5W), lambda i: (i, 0)), chk, chk],
        out_shape=[SDS((length, S5W), F32), SDS((nt, SUBLANES, S5N), F32), SDS((nt, SUBLANES, S5N), F32)],
        scratch_shapes=[pltpu.VMEM((tl, S5N), F32)] * 2 + [pltpu.VMEM((SUBLANES, S5N), F32)] * 2,
        compiler_params=_params(), name="s5_fwd")(u, bblk, cblk, abr, abi, d)


def _s5_bwd(u, dys_a, dys_b, kre, kim, bblk, cblk, abr, abi, d):
    length = u.shape[0]
    tl = min(256, length)
    nt = length // tl

    def body(u_ref, da_ref, db_ref, kr_ref, ki_ref, b_ref, c_ref, ar_ref, ai_ref, d_ref,
             du_ref, dB_ref, dC_ref, dar_ref, dai_ref, dd_ref, sre, sim, gre, gim, car_re, car_im, dcar_re, dcar_im):
        @pl.when(pl.program_id(0) == 0)
        def _():
            dcar_re[...] = jnp.zeros_like(dcar_re)
            dcar_im[...] = jnp.zeros_like(dcar_im)
            for ref in (dB_ref, dC_ref, dar_ref, dai_ref, dd_ref):
                ref[...] = jnp.zeros_like(ref)

        u_ = u_ref[...]
        abr_, abi_, d_ = ar_ref[...], ai_ref[...], d_ref[...]
        car_re[...] = kr_ref[0]
        car_im[...] = ki_ref[0]
        _s5_project_in(u_, b_ref, sre, sim)
        _s5_scan(sre, sim, car_re, car_im, abr_, abi_, tl, False)
        y = _s5_project_out(u_, d_, c_ref, sre, sim)
        _, vjp = jax.vjp(_gelu, y)
        dy = vjp(da_ref[...] + db_ref[...])[0]
        dd_ref[...] += jnp.sum(dy * u_, axis=0, keepdims=True)
        for j in range(S5_BLK):
            sl = slice(j * S5_BN, (j + 1) * S5_BN)
            dyj = dy[:, j * LANES:(j + 1) * LANES]
            gre[:, sl] = _dot1(dyj, c_ref[j, :S5_BN, :], 1, 1)
            gim[:, sl] = _dot1(dyj, c_ref[j, S5_BN:, :], 1, 1)
            dC_ref[j, :S5_BN, :] += _dot1(sre[:, sl], dyj, 0, 0)
            dC_ref[j, S5_BN:, :] += _dot1(sim[:, sl], dyj, 0, 0)
        _s5_scan(gre, gim, dcar_re, dcar_im, abr_, abi_, tl, True)
        gr, gi = gre[...], gim[...]
        pr = _shift_down(sre[...], kr_ref[0, 0:1, :])
        pi = _shift_down(sim[...], ki_ref[0, 0:1, :])
        dar_ref[...] += jnp.sum(gr * pr + gi * pi, axis=0, keepdims=True)
        dai_ref[...] += jnp.sum(gi * pr - gr * pi, axis=0, keepdims=True)
        dus = []
        for j in range(S5_BLK):
            sl = slice(j * S5_BN, (j + 1) * S5_BN)
            uj = u_[:, j * LANES:(j + 1) * LANES]
            dus.append(_dot1(gre[:, sl], b_ref[j, :, :S5_BN], 1, 1) + _dot1(gim[:, sl], b_ref[j, :, S5_BN:], 1, 1))
            dB_ref[j, :, :S5_BN] += _dot1(uj, gre[:, sl], 0, 0)
            dB_ref[j, :, S5_BN:] += _dot1(uj, gim[:, sl], 0, 0)
        du_ref[...] = (jnp.concatenate(dus, axis=1) + d_ * dy).astype(du_ref.dtype)

    rmap = lambda i: nt - 1 - i
    full = lambda a: pl.BlockSpec(a.shape, lambda i, nd=a.ndim: (0,) * nd)
    row = pl.BlockSpec((tl, S5W), lambda i: (rmap(i), 0))
    chk = pl.BlockSpec((1, SUBLANES, S5N), lambda i: (rmap(i), 0, 0))
    return pl.pallas_call(
        body, grid=(nt,), in_specs=[row, row, row, chk, chk, full(bblk), full(cblk), full(abr), full(abi), full(d)],
        out_specs=[row, full(bblk), full(cblk), full(abr), full(abi), full(d)],
        out_shape=[SDS((length, S5W), BF16), SDS(bblk.shape, F32), SDS(cblk.shape, F32), SDS(abr.shape, F32),
                   SDS(abi.shape, F32), SDS(d.shape, F32)],
        scratch_shapes=[pltpu.VMEM((tl, S5N), F32)] * 4 + [pltpu.VMEM((SUBLANES, S5N), F32)] * 4,
        compiler_params=_params(), name="s5_bwd")(u, dys_a, dys_b, kre, kim, bblk, cblk, abr, abi, d)


def _s5_blockdiag_in(bb):
    t = bb.reshape(S5C, S5_BLK, S5_BN).transpose(1, 0, 2)
    t = jnp.tile(t, (1, SUBLANES, 1))
    mask = (lax.broadcasted_iota(jnp.int32, (LANES, S5_BN), 0) // S5C) == (lax.broadcasted_iota(jnp.int32, (LANES, S5_BN), 1) // S5P)
    return jnp.where(mask[None], t, 0.0)


def _s5_blockdiag_in_t(dblk):
    t = dblk.reshape(S5_BLK, SUBLANES, S5C, SUBLANES, S5P)
    t = jnp.diagonal(t, axis1=1, axis2=3)
    return t.transpose(1, 0, 3, 2).reshape(S5C, S5N)


def _s5_blockdiag_out(c):
    t = c.reshape(S5_BLK, SUBLANES, S5C, S5P).transpose(0, 3, 1, 2).reshape(S5_BLK, S5P, LANES)
    t = jnp.tile(t, (1, SUBLANES, 1))
    mask = (lax.broadcasted_iota(jnp.int32, (S5_BN, LANES), 0) // S5P) == (lax.broadcasted_iota(jnp.int32, (S5_BN, LANES), 1) // S5C)
    return jnp.where(mask[None], t, 0.0)


def _s5_blockdiag_out_t(dblk):
    t = dblk.reshape(S5_BLK, SUBLANES, S5P, SUBLANES, S5C)
    t = jnp.diagonal(t, axis1=1, axis2=3)
    return t.transpose(0, 3, 2, 1).reshape(S5G, S5C, S5P)


def _wkv_consts():
    lane = lax.broadcasted_iota(jnp.int32, (4 * HN, LANES), 1)
    row = lax.broadcasted_iota(jnp.int32, (4 * HN, LANES), 0)
    diag = ((lane % HN) == (row % HN)).astype(F32)
    ones = _block_ones(LANES, HN).astype(BF16)
    return lane % HN, diag, jnp.concatenate([ones, ones], axis=0)


def _wkv_split(p):
    hi = p.astype(BF16)
    lo = (p - hi.astype(F32)).astype(BF16)
    return jnp.concatenate([hi, lo], axis=1)


def _wkv_rowseg(p, ones):
    return jnp.dot(_wkv_split(p), ones, preferred_element_type=F32)


def _wkv_rowseg_all(lhs_ref, ones, out_ref):
    c, ns = lhs_ref.shape[0], lhs_ref.shape[1]
    out_ref[...] = jnp.dot(lhs_ref[...].reshape(c * ns, 2 * LANES), ones, preferred_element_type=F32).reshape(c, ns, LANES)


def _wkv_full(qb_ref, t):
    q = qb_ref[t]
    return jnp.concatenate([jnp.tile(q[:, hp * LANES:(hp + 1) * LANES], (SUBLANES, 1)) for hp in range(4)], axis=0)


def _wkv_colsum8(x):
    return jnp.concatenate([x[hp * HN:(hp + 1) * HN].reshape(SUBLANES, SUBLANES, LANES).sum(axis=0) for hp in range(4)], axis=1)


def _wkv_prebroadcast(src_refs, dst_refs, n):
    rep = (lax.broadcasted_iota(jnp.int32, (n * SUBLANES, n), 0) // SUBLANES
           == lax.broadcasted_iota(jnp.int32, (n * SUBLANES, n), 1)).astype(BF16)
    rep3 = jnp.concatenate([rep, rep, rep], axis=1)
    for s, dref in zip(src_refs, dst_refs, strict=True):
        x = s[...]
        hi = x.astype(BF16)
        rest = x - hi.astype(F32)
        mid = rest.astype(BF16)
        lo = (rest - mid.astype(F32)).astype(BF16)
        dref[...] = jnp.dot(rep3, jnp.concatenate([hi, mid, lo], axis=0), preferred_element_type=F32).reshape(n, SUBLANES, RW)


def _wkv_pair_sums(lhs_ref, out_ref):
    n, ns = out_ref.shape[0], out_ref.shape[1]
    ones_pair = _block_ones(2 * LANES, HN).astype(BF16)
    out_ref[...] = jnp.dot(lhs_ref[0:n].reshape(n * ns, 2 * LANES), ones_pair,
                           preferred_element_type=F32).reshape(n, ns, 2 * LANES)


def _wkv_transpose_out(acc, c, diag):
    lane = lax.broadcasted_iota(jnp.int32, (c, LANES), 1)
    outs = []
    for hp in range(4):
        zp = lax.dot_general(acc[hp * HN:(hp + 1) * HN], diag[:HN], (((0,), (0,)), ((), ())), precision=HI,
                             preferred_element_type=F32)
        outs.append(jnp.where(lane < HN, zp[:c], zp[HN:HN + c]))
    return jnp.concatenate(outs, axis=1)


def _wkv_fwd(r, w, k, v, a, b):
    length = r.shape[0]
    c = min(WKV_CHUNK, length)
    nc = length // c
    ns = 4 * HN

    def body(r_ref, w_ref, k_ref, v_ref, a_ref, b_ref, y_ref, sst_ref, sast_ref, vst_ref, fin_ref,
             s_ref, acc_ref, lhs_ref, ysum_ref, rb, wb, kb, vb, ab, bb, wab, beb, kab):
        @pl.when(pl.program_id(0) == 0)
        def _():
            s_ref[...] = jnp.zeros_like(s_ref)

        a_next = pltpu.roll(a_ref[...], c - 1, 0)
        o512 = _block_ones(RW, HN)
        _wkv_prebroadcast((r_ref, w_ref, k_ref, v_ref, a_ref, b_ref, w_ref[...] * a_next,
                           _segsum_impl(b_ref[...] * a_next, o512), _segsum_impl(k_ref[...] * a_next, o512)),
                          (rb, wb, kb, vb, ab, bb, wab, beb, kab), c)
        lane_t, diag, ones = _wkv_consts()

        def fill_v(t, carry):
            lhs_ref[t] = _wkv_split(_wkv_full(vb, t) * diag)
            return carry

        lax.fori_loop(0, c, fill_v, 0)
        _wkv_rowseg_all(lhs_ref, ones, vst_ref)

        def pair(p, carry):
            t0, t1 = 2 * p, 2 * p + 1
            s0 = s_ref[...]
            both = _wkv_rowseg(jnp.concatenate([s0 * _wkv_full(ab, t0), s0 * _wkv_full(wab, t0)], axis=0), ones)
            sa0, v0 = both[:ns], vst_ref[t0]
            s1 = s0 * _wkv_full(wb, t0) + sa0 * _wkv_full(bb, t0) + v0 * _wkv_full(kb, t0)
            sa1 = both[ns:] + sa0 * _wkv_full(beb, t0) + v0 * _wkv_full(kab, t0)
            s2 = s1 * _wkv_full(wb, t1) + sa1 * _wkv_full(bb, t1) + vst_ref[t1] * _wkv_full(kb, t1)
            sst_ref[t0] = s0
            sast_ref[t0] = sa0
            sst_ref[t1] = s1
            sast_ref[t1] = sa1
            s_ref[...] = s2
            lhs_ref[p, :, 0:LANES] = (s1 * _wkv_full(rb, t0)).astype(BF16)
            lhs_ref[p, :, LANES:2 * LANES] = (s2 * _wkv_full(rb, t1)).astype(BF16)
            return carry

        lax.fori_loop(0, c // 2, pair, 0)
        _wkv_pair_sums(lhs_ref, ysum_ref)
        acc_ref[...] = jnp.zeros_like(acc_ref)

        def gather_y(p, carry):
            both = ysum_ref[p]
            acc = jnp.where(lane_t == 2 * p, both[:, :LANES], acc_ref[...])
            acc_ref[...] = jnp.where(lane_t == 2 * p + 1, both[:, LANES:], acc)
            return carry

        lax.fori_loop(0, c // 2, gather_y, 0)
        y_ref[...] = _wkv_transpose_out(acc_ref[...], c, diag)
        fin_ref[...] = s_ref[...]

    row = pl.BlockSpec((c, RW), lambda i: (i, 0))
    st = pl.BlockSpec((c, ns, LANES), lambda i: (i, 0, 0))
    state = pltpu.VMEM((ns, LANES), F32)
    return pl.pallas_call(
        body, grid=(nc,), in_specs=[row] * 6,
        out_specs=[row, st, st, st, pl.BlockSpec((ns, LANES), lambda i: (0, 0))],
        out_shape=[SDS((length, RW), F32)] + [SDS((length, ns, LANES), F32)] * 3 + [SDS((ns, LANES), F32)],
        scratch_shapes=[state] * 2 + [pltpu.VMEM((c, ns, 2 * LANES), BF16), pltpu.VMEM((c // 2, ns, 2 * LANES), F32)]
        + [pltpu.VMEM((c, SUBLANES, RW), F32)] * 9,
        compiler_params=_params(), name="wkv_fwd")(r, w, k, v, a, b)


def _wkv_bwd(r, w, k, a, b, dy, sst, sast, vst, fin):
    length = r.shape[0]
    c = min(WKV_CHUNK, length)
    nc = length // c
    ns = 4 * HN

    def body(r_ref, w_ref, k_ref, a_ref, b_ref, dy_ref, sst_ref, sast_ref, vst_ref, snext_ref, fin_ref,
             dr_ref, dw_ref, dk_ref, da_ref, db_ref, dv_ref,
             ds_ref, cur_ref, acc_ref, dyst, lhs_ref, dvsum_ref, rb, wb, kb, ab, bb, dyb, wbb, alb, rhb,
             p_r, p_w, p_k, p_a, p_b):
        @pl.when(pl.program_id(0) == 0)
        def _():
            ds_ref[...] = jnp.zeros_like(ds_ref)

        b_prev = pltpu.roll(b_ref[...], 1, 0)
        o512 = _block_ones(RW, HN)
        _wkv_prebroadcast((r_ref, w_ref, k_ref, a_ref, b_ref, dy_ref, w_ref[...] * b_prev,
                           _segsum_impl(a_ref[...] * b_prev, o512), _segsum_impl(r_ref[...] * b_ref[...], o512)),
                          (rb, wb, kb, ab, bb, dyb, wbb, alb, rhb), c)
        lane_t, diag, ones = _wkv_consts()

        def fill(q, carry):
            for half in range(2):
                lhs_ref[q, :, half * LANES:(half + 1) * LANES] = (_wkv_full(dyb, 2 * q + half) * diag).astype(BF16)
            return carry

        lax.fori_loop(0, c // 2, fill, 0)
        _wkv_pair_sums(lhs_ref, dyst)
        cur_ref[...] = jnp.where(pl.program_id(0) == 0, fin_ref[...], snext_ref[0])

        def pair(p, carry):
            ta = c - 1 - 2 * p
            tb = ta - 1
            s_a, s_b, s_c = cur_ref[...], sst_ref[ta], sst_ref[tb]
            dy_pair = dyst[c // 2 - 1 - p]
            dy_a, dy_b = dy_pair[:, LANES:], dy_pair[:, :LANES]
            ds_a = ds_ref[...] + dy_a * _wkv_full(rb, ta)
            both = _wkv_rowseg(jnp.concatenate([ds_a * _wkv_full(bb, ta), ds_a * _wkv_full(wbb, ta)], axis=0), ones)
            dsa_a = both[:ns]
            ds_b = ds_a * _wkv_full(wb, ta) + dsa_a * _wkv_full(ab, ta) + dy_b * _wkv_full(rb, tb)
            dsa_b = both[ns:] + dsa_a * _wkv_full(alb, ta) + dy_b * _wkv_full(rhb, tb)
            ds_ref[...] = ds_b * _wkv_full(wb, tb) + dsa_b * _wkv_full(ab, tb)
            cur_ref[...] = s_c
            lhs_ref[p, :, 0:LANES] = (ds_a * _wkv_full(kb, ta)).astype(BF16)
            lhs_ref[p, :, LANES:2 * LANES] = (ds_b * _wkv_full(kb, tb)).astype(BF16)
            for t, s_t, s_prev, ds, dsa, dyy in ((ta, s_a, s_b, ds_a, dsa_a, dy_a), (tb, s_b, s_c, ds_b, dsa_b, dy_b)):
                p_r[t] = _wkv_colsum8(s_t * dyy)
                p_k[t] = _wkv_colsum8(ds * vst_ref[t])
                p_b[t] = _wkv_colsum8(ds * sast_ref[t])
                p_w[t] = _wkv_colsum8(ds * s_prev)
                p_a[t] = _wkv_colsum8(s_prev * dsa)
            return carry

        lax.fori_loop(0, c // 2, pair, 0)
        _wkv_pair_sums(lhs_ref, dvsum_ref)
        acc_ref[...] = jnp.zeros_like(acc_ref)

        def gather_dv(p, carry):
            both = dvsum_ref[p]
            acc = jnp.where(lane_t == c - 1 - 2 * p, both[:, :LANES], acc_ref[...])
            acc_ref[...] = jnp.where(lane_t == c - 2 - 2 * p, both[:, LANES:], acc)
            return carry

        lax.fori_loop(0, c // 2, gather_dv, 0)
        sel = (lax.broadcasted_iota(jnp.int32, (c, c * SUBLANES), 1) // SUBLANES
               == lax.broadcasted_iota(jnp.int32, (c, c * SUBLANES), 0)).astype(F32)
        for out_ref, part in ((dr_ref, p_r), (dw_ref, p_w), (dk_ref, p_k), (da_ref, p_a), (db_ref, p_b)):
            out_ref[...] = _dot3(sel, part[...].reshape(c * SUBLANES, RW), 1, 0)
        dv_ref[...] = _wkv_transpose_out(acc_ref[...], c, diag)

    rmap = lambda i: nc - 1 - i
    row = pl.BlockSpec((c, RW), lambda i: (rmap(i), 0))
    st = pl.BlockSpec((c, ns, LANES), lambda i: (rmap(i), 0, 0))
    nxt = pl.BlockSpec((1, ns, LANES), lambda i: (jnp.minimum((rmap(i) + 1) * c, length - 1), 0, 0))
    state = pltpu.VMEM((ns, LANES), F32)
    return pl.pallas_call(
        body, grid=(nc,), in_specs=[row] * 6 + [st, st, st, nxt, pl.BlockSpec((ns, LANES), lambda i: (0, 0))],
        out_specs=[row] * 6, out_shape=[SDS((length, RW), F32)] * 6,
        scratch_shapes=[state] * 3 + [pltpu.VMEM((c // 2, ns, 2 * LANES), F32), pltpu.VMEM((c, ns, 2 * LANES), BF16),
                                       pltpu.VMEM((c // 2, ns, 2 * LANES), F32)]
        + [pltpu.VMEM((c, SUBLANES, RW), F32)] * 14,
        compiler_params=_params(), name="wkv_bwd")(r, w, k, a, b, dy, sst, sast, vst, sst, fin)


def _my_place():
    return lax.axis_index("x"), lax.axis_index("y"), lax.axis_index("c")


def _peer(x, y, c, k):
    return (x ^ ((k >> 2) & 1), y ^ ((k >> 1) & 1), c ^ (k & 1))


def _all_gather(shard, name):
    rows = shard.shape[0]

    def body(in_ref, out_ref, send_sems, recv_sems, local_sem):
        x, y, c = _my_place()
        me = 4 * x + 2 * y + c
        mine = pltpu.make_async_copy(in_ref, out_ref.at[me], local_sem)
        mine.start()
        sends = []
        for k in range(1, N_DEV):
            cp = pltpu.make_async_remote_copy(src_ref=in_ref, dst_ref=out_ref.at[me], send_sem=send_sems.at[k - 1],
                                              recv_sem=recv_sems.at[k - 1], device_id=_peer(x, y, c, k), device_id_type=MESH)
            cp.start()
            sends.append(cp)
        for k in range(1, N_DEV):
            px, py, pc = _peer(x, y, c, k)
            pltpu.make_async_remote_copy(src_ref=in_ref, dst_ref=out_ref.at[4 * px + 2 * py + pc], send_sem=send_sems.at[k - 1],
                                         recv_sem=recv_sems.at[k - 1], device_id=(px, py, pc), device_id_type=MESH).wait_recv()
        for cp in sends:
            cp.wait_send()
        mine.wait()

    return pl.pallas_call(
        body, out_shape=SDS((N_DEV, rows, LANES), shard.dtype),
        in_specs=[pl.BlockSpec(memory_space=pl.ANY)], out_specs=pl.BlockSpec(memory_space=pl.ANY),
        scratch_shapes=[pltpu.SemaphoreType.DMA((N_DEV - 1,)), pltpu.SemaphoreType.DMA((N_DEV - 1,)), pltpu.SemaphoreType.DMA],
        name=name)(shard)


def _all_to_all(chunks, name):
    rows = chunks.shape[1]

    def body(in_ref, out_ref, send_sems, recv_sems, local_sem):
        x, y, c = _my_place()
        me = 4 * x + 2 * y + c
        mine = pltpu.make_async_copy(in_ref.at[me], out_ref.at[me], local_sem)
        mine.start()
        sends = []
        for k in range(1, N_DEV):
            px, py, pc = _peer(x, y, c, k)
            cp = pltpu.make_async_remote_copy(src_ref=in_ref.at[4 * px + 2 * py + pc], dst_ref=out_ref.at[me],
                                              send_sem=send_sems.at[k - 1], recv_sem=recv_sems.at[k - 1],
                                              device_id=(px, py, pc), device_id_type=MESH)
            cp.start()
            sends.append(cp)
        for k in range(1, N_DEV):
            px, py, pc = _peer(x, y, c, k)
            pltpu.make_async_remote_copy(src_ref=in_ref.at[me], dst_ref=out_ref.at[4 * px + 2 * py + pc], send_sem=send_sems.at[k - 1],
                                         recv_sem=recv_sems.at[k - 1], device_id=(px, py, pc), device_id_type=MESH).wait_recv()
        for cp in sends:
            cp.wait_send()
        mine.wait()

    return pl.pallas_call(
        body, out_shape=SDS((N_DEV, rows, LANES), chunks.dtype),
        in_specs=[pl.BlockSpec(memory_space=pl.ANY)], out_specs=pl.BlockSpec(memory_space=pl.ANY),
        scratch_shapes=[pltpu.SemaphoreType.DMA((N_DEV - 1,)), pltpu.SemaphoreType.DMA((N_DEV - 1,)), pltpu.SemaphoreType.DMA],
        name=name)(chunks)


_HBM_SPEC = pl.BlockSpec(memory_space=pltpu.HBM)
_SEM_SPEC = pl.BlockSpec(memory_space=pltpu.SEMAPHORE)
_DATAFLOW = pltpu.SideEffectType.DATAFLOW_SIDE_EFFECTING


def _exchange_start(src, per_peer, name):
    rows = src.shape[-2]
    land = jnp.zeros((N_DEV, rows, LANES), src.dtype)

    def body(src_ref, land_ref, send_sems, recv_sems, src_thru, land_thru, token):
        x, y, c = _my_place()
        me = 4 * x + 2 * y + c
        for k in range(1, N_DEV):
            px, py, pc = _peer(x, y, c, k)
            mine = src_ref.at[4 * px + 2 * py + pc] if per_peer else src_ref
            pltpu.make_async_remote_copy(src_ref=mine, dst_ref=land_ref.at[me], send_sem=send_sems.at[k - 1],
                                         recv_sem=recv_sems.at[k - 1], device_id=(px, py, pc), device_id_type=MESH).start()
        token[...] = jnp.zeros_like(token)

    return pl.pallas_call(
        body, name=name,
        out_shape=(pltpu.SemaphoreType.DMA((N_DEV - 1,)), pltpu.SemaphoreType.DMA((N_DEV - 1,)), pltpu.HBM(src.shape, src.dtype),
                   pltpu.HBM(land.shape, land.dtype), SDS((SUBLANES, LANES), F32)),
        in_specs=(_HBM_SPEC, _HBM_SPEC),
        out_specs=(_SEM_SPEC, _SEM_SPEC, _HBM_SPEC, _HBM_SPEC, pl.BlockSpec(memory_space=pltpu.VMEM)),
        input_output_aliases={0: 2, 1: 3}, compiler_params=pltpu.CompilerParams(has_side_effects=_DATAFLOW),
    )(pltpu.with_memory_space_constraint(src, pltpu.HBM), pltpu.with_memory_space_constraint(land, pltpu.HBM))


def _exchange_wait(started, after, per_peer, name):
    send_sems, recv_sems, src_thru, land_thru, _ = started

    def body(src_ref, land_ref, send_sems, recv_sems, after_ref, src_dead, got_ref):
        x, y, c = _my_place()
        me = 4 * x + 2 * y + c
        for k in range(1, N_DEV):
            px, py, pc = _peer(x, y, c, k)
            mine = src_ref.at[me] if per_peer else src_ref
            copy = pltpu.make_async_remote_copy(src_ref=mine, dst_ref=land_ref.at[4 * px + 2 * py + pc], send_sem=send_sems.at[k - 1],
                                                recv_sem=recv_sems.at[k - 1], device_id=(px, py, pc), device_id_type=MESH)
            copy.wait_send()
            copy.wait_recv()

    return pl.pallas_call(
        body, name=name, out_shape=(pltpu.HBM(src_thru.shape, src_thru.dtype), pltpu.HBM(land_thru.shape, land_thru.dtype)),
        in_specs=(_HBM_SPEC, _HBM_SPEC, _SEM_SPEC, _SEM_SPEC, pl.BlockSpec(memory_space=pl.ANY)),
        out_specs=(_HBM_SPEC, _HBM_SPEC), input_output_aliases={0: 0, 1: 1},
        compiler_params=pltpu.CompilerParams(has_side_effects=_DATAFLOW),
    )(src_thru, land_thru, send_sems, recv_sems, after)[1]


def _sum_parts(parts, name, own=None):
    rows = parts.shape[1]

    def body(*refs):
        p_ref, g_out = refs[0], refs[-1]
        g = p_ref[0].astype(F32)
        for s in range(1, N_DEV):
            g = g + p_ref[s].astype(F32)
        if own is not None:
            g = g + refs[1][...].astype(F32)
        g_out[...] = g

    blk = pl.BlockSpec((PACK_ROWS, LANES), lambda i: (i, 0))
    extra = [] if own is None else [own]
    return pl.pallas_call(
        body, grid=(rows // PACK_ROWS,),
        in_specs=[pl.BlockSpec((N_DEV, PACK_ROWS, LANES), lambda i: (0, i, 0))] + [blk] * len(extra),
        out_specs=blk, out_shape=SDS((rows, LANES), F32),
        compiler_params=_params(("parallel",)), name=name)(parts, *extra)


def _adamw(g, wgt, m, v):
    rows = wgt.shape[0]

    def body(g_ref, w_ref, m_ref, v_ref, d_out, m_out, v_out):
        g_ = g_ref[...]
        m_new = ADAM_B1 * m_ref[...] + (1.0 - ADAM_B1) * g_
        v_new = ADAM_B2 * v_ref[...] + (1.0 - ADAM_B2) * (g_ * g_)
        m_hat = m_new / (1.0 - ADAM_B1 ** ADAM_STEP)
        v_hat = v_new / (1.0 - ADAM_B2 ** ADAM_STEP)
        d_out[...] = -ADAM_LR * (m_hat / (jnp.sqrt(v_hat) + ADAM_EPS) + ADAM_WD * w_ref[...])
        m_out[...] = m_new
        v_out[...] = v_new

    blk = pl.BlockSpec((PACK_ROWS, LANES), lambda i: (i, 0))
    return pl.pallas_call(
        body, grid=(rows // PACK_ROWS,), in_specs=[blk] * 4, out_specs=[blk] * 3, out_shape=[SDS((rows, LANES), F32)] * 3,
        compiler_params=_params(("parallel",)), name="adamw")(g, wgt, m, v)


PACK_ALIGN = 2 * SUBLANES * LANES
PACK_ROWS = 512

ROW_SHARDED = ("s5_w_glu", "w_out", "ffn_w_down")
TRANSPOSED = ("w_in", "ffn_w_up", "w_branch_rwkv", "w_branch_s5")
SMALL_SHARDED = ("rwkv_w2", "rwkv_a2", "rwkv_g2", "ffn_conv_w")
SHARDED = ROW_SHARDED + TRANSPOSED + SMALL_SHARDED
F32_GATHER = ("ffn_conv_w",)
REPLICATED =("norm_mix_pre", "norm_mix_post", "norm_ffn_pre", "norm_ffn_post", "b_gate", "rwkv_shift_mu", "rwkv_w0",
              "rwkv_a0", "rwkv_k_k", "rwkv_k_a", "rwkv_r_k", "rwkv_lnx_w", "rwkv_lnx_b", "s5_a_re", "s5_a_im", "s5_b_re",
              "s5_b_im", "s5_c_re", "s5_c_im", "s5_d", "s5_log_step", "s5_b_glu", "ffn_conv_b")
WEIGHTS = ("norm_mix_pre", "norm_mix_post", "norm_ffn_pre", "norm_ffn_post", "w_in", "b_gate", "rwkv_shift_mu", "rwkv_w0",
           "rwkv_w2", "rwkv_a0", "rwkv_a2", "rwkv_g2", "rwkv_k_k", "rwkv_k_a", "rwkv_r_k", "rwkv_lnx_w", "rwkv_lnx_b",
           "s5_a_re", "s5_a_im", "s5_b_re", "s5_b_im", "s5_c_re", "s5_c_im", "s5_d", "s5_log_step", "s5_w_glu", "s5_b_glu",
           "w_branch_rwkv", "w_branch_s5", "w_out", "ffn_w_up", "ffn_conv_w", "ffn_conv_b", "ffn_w_down")


def _pad_flat(a):
    flat = a.reshape(-1)
    pad = (-flat.shape[0]) % PACK_ALIGN
    return jnp.pad(flat, (0, pad)) if pad else flat


def _pad_cols(a, mult):
    pad = (-a.shape[1]) % mult
    return jnp.pad(a, ((0, 0), (0, pad))) if pad else a


def _pack(pieces):
    flat = jnp.concatenate([_pad_flat(p) for p in pieces])
    pad = (-flat.shape[0]) % (PACK_ROWS * LANES)
    return (jnp.pad(flat, (0, pad)) if pad else flat).reshape(-1, LANES)


def _unpack(buf, shapes):
    lead = buf.shape[:-2]
    flat = buf.reshape(*lead, -1)
    out, off = [], 0
    for s in shapes:
        n = math.prod(s)
        out.append(flat[..., off:off + n].reshape(*lead, *s))
        off += n + (-n) % PACK_ALIGN
    return out


def _rows_to_full(stack):
    return stack.reshape(-1, stack.shape[2])


def _cols_to_full(stack):
    return stack.transpose(1, 0, 2).reshape(stack.shape[1], -1)


EARLY_GRADS = ("ffn_w_down", "ffn_w_up", "w_out")
REST_GRADS = ("s5_w_glu", "w_in", "w_branch_rwkv", "w_branch_s5")


def _local_step(x, tgt, wt, get_late, send_early, send_rest):
    wt = dict(wt)
    o512 = _block_ones(RW, HN)
    w_in = wt["w_in"]
    w_rw, w_u, w_g = w_in[:N_RWKV], w_in[N_RWKV:N_RWKV + S5W], w_in[N_RWKV + S5W:]
    b_gate = wt["b_gate"]
    b_r, b_s = b_gate[:, :D_MODEL], b_gate[:, D_MODEL:]
    zpad = lambda a, lo, n: jnp.pad(a.astype(F32), ((lo, n - lo - a.shape[0]), (0, 0)))
    w2p, a2p, g2p = zpad(wt["rwkv_w2"], 0, 256), zpad(wt["rwkv_a2"], 64, 256), zpad(wt["rwkv_g2"], 128, 256)
    pre_small = [wt["rwkv_w0"], wt["rwkv_a0"], wt["rwkv_k_k"], wt["rwkv_k_a"], w2p, a2p, g2p]
    post_prm = [wt["rwkv_lnx_w"], wt["rwkv_lnx_b"], wt["rwkv_r_k"].reshape(1, RW)]
    mu = wt["rwkv_shift_mu"]

    a_re, a_im = wt["s5_a_re"].reshape(1, S5N), wt["s5_a_im"].reshape(1, S5N)
    ls = jnp.repeat(wt["s5_log_step"].reshape(S5G), S5P).reshape(1, S5N)
    b_re_t = wt["s5_b_re"].reshape(S5N, S5C).T
    b_im_t = wt["s5_b_im"].reshape(S5N, S5C).T
    c_re, c_im = wt["s5_c_re"].reshape(S5G, S5C, S5P), wt["s5_c_im"].reshape(S5G, S5C, S5P)
    abr, abi, bbr, bbi = _s5_prep_fwd(a_re, a_im, ls, b_re_t, b_im_t)
    bblk = jnp.concatenate([_s5_blockdiag_in(bbr), _s5_blockdiag_in(bbi)], axis=2)
    cblk = jnp.concatenate([_s5_blockdiag_out(c_re), -_s5_blockdiag_out(c_im)], axis=1)
    s5_d = wt["s5_d"]

    h1 = _rms_fwd(x, wt["norm_mix_pre"], "rms1_fwd")
    p_rw = _mm(h1, w_rw, tb=True, name="mm_proj_rwkv")
    u = _mm(h1, w_u, tb=True, name="mm_proj_s5")
    gpre = _mm(h1, w_g, tb=True, name="mm_proj_gate")
    r, decay, k2, v, aa, bb, g = _rwkv_pre_fwd(p_rw, mu, pre_small, o512)
    y, sst, sast, vst, s_fin = _wkv_fwd(r, decay, k2, v, aa, bb)
    wt.update(get_late(y))
    y_r = _rwkv_post_fwd(y, r, k2, v, g, post_prm, o512)
    o_r = _mm(y_r, wt["w_branch_rwkv"], tb=True, name="mm_branch_rwkv")
    ys, kre, kim = _s5_fwd(u, bblk, cblk, abr, abi, s5_d)
    t_glu = _mm(ys, wt["s5_w_glu"], name="mm_glu")
    out5 = _glu_fwd(ys, t_glu, wt["s5_b_glu"])
    o_s = _mm(out5, wt["w_branch_s5"], tb=True, name="mm_branch_s5")
    mi = _merge_fwd(gpre, o_r, o_s, b_r, b_s)
    mixed = _mm(mi, wt["w_out"], name="mm_out")
    x2, h2 = _mid_fwd(x, mixed, wt["norm_mix_post"], wt["norm_ffn_pre"])
    z = _mm(h2, wt["ffn_w_up"], tb=True, name="mm_up")
    act = _conv_fwd(z, wt["ffn_conv_w"], wt["ffn_conv_b"])
    f = _mm(act, wt["ffn_w_down"], name="mm_down")
    dx3, df, d_norm_ffn_post, loss_part = _final(x2, f, tgt, wt["norm_ffn_post"])

    gr = {"norm_ffn_post": d_norm_ffn_post}
    dact = _mm(df, wt["ffn_w_down"], tb=True, name="mm_down_dx")
    gr["ffn_w_down"] = _mm(act, df, ta=True, name="mm_down_dw")
    dzg, dzv, gr["ffn_conv_w"], gr["ffn_conv_b"] = _conv_bwd(z, dact, wt["ffn_conv_w"], wt["ffn_conv_b"])
    dz = jnp.concatenate([dzg, dzv], axis=1)
    dh2 = _mm(dz, wt["ffn_w_up"], name="mm_up_dx")
    gr["ffn_w_up"] = _mm(dz, h2, ta=True, name="mm_up_dw")
    dx2, dmixed, gr["norm_mix_post"], gr["norm_ffn_pre"] = _mid_bwd(x2, mixed, dh2, dx3, wt["norm_mix_post"], wt["norm_ffn_pre"])
    dmi = _mm(dmixed, wt["w_out"], tb=True, name="mm_out_dx")
    gr["w_out"] = _mm(mi, dmixed, ta=True, name="mm_out_dw")
    token = send_early({n: gr[n] for n in EARLY_GRADS})
    b_r = b_r + jnp.tile(token[0:1, :], (1, D_MODEL // LANES))
    dgp_r, dgp_s, do_r, do_s, db_r, db_s = _merge_bwd(gpre, o_r, o_s, dmi, b_r, b_s)
    gr["b_gate"] = jnp.concatenate([db_r, db_s], axis=1)
    dout5 = _mm(do_s, wt["w_branch_s5"], name="mm_branch_s5_dx")
    gr["w_branch_s5"] = _mm(do_s, out5, ta=True, name="mm_branch_s5_dw")
    dys_a, dt_glu, gr["s5_b_glu"] = _glu_bwd(ys, t_glu, dout5, wt["s5_b_glu"])
    dys_b = _mm(dt_glu, wt["s5_w_glu"], tb=True, name="mm_glu_dx")
    gr["s5_w_glu"] = _mm(ys, dt_glu, ta=True, name="mm_glu_dw")
    du, dbblk, dcblk, dabr, dabi, gr["s5_d"] = _s5_bwd(u, dys_a, dys_b, kre, kim, bblk, cblk, abr, abi, s5_d)
    gr["s5_c_re"] = _s5_blockdiag_out_t(dcblk[:, :S5_BN, :]).reshape(wt["s5_c_re"].shape)
    gr["s5_c_im"] = (-_s5_blockdiag_out_t(dcblk[:, S5_BN:, :])).reshape(wt["s5_c_im"].shape)
    dbbr, dbbi = _s5_blockdiag_in_t(dbblk[:, :, :S5_BN]), _s5_blockdiag_in_t(dbblk[:, :, S5_BN:])
    gsel = (lax.broadcasted_iota(jnp.int32, (S5N, LANES), 0) // S5P == lax.broadcasted_iota(jnp.int32, (S5N, LANES), 1)).astype(F32)
    d_are, d_aim, d_ls, d_bre_t, d_bim_t = _s5_prep_bwd(a_re, a_im, ls, b_re_t, b_im_t, (dabr, dabi, dbbr, dbbi), gsel)
    gr["s5_a_re"] = d_are.reshape(wt["s5_a_re"].shape)
    gr["s5_a_im"] = d_aim.reshape(wt["s5_a_im"].shape)
    gr["s5_log_step"] = d_ls[0:1, :S5G]
    gr["s5_b_re"] = d_bre_t.T.reshape(wt["s5_b_re"].shape)
    gr["s5_b_im"] = d_bim_t.T.reshape(wt["s5_b_im"].shape)
    dy_r = _mm(do_r, wt["w_branch_rwkv"], name="mm_branch_rwkv_dx")
    gr["w_branch_rwkv"] = _mm(do_r, y_r, ta=True, name="mm_branch_rwkv_dw")
    dy, dr1, dk1, dv1, dg, gr["rwkv_lnx_w"], gr["rwkv_lnx_b"], d_rk = _rwkv_post_bwd(y, r, k2, v, g, dy_r, post_prm, o512)
    gr["rwkv_r_k"] = d_rk.reshape(wt["rwkv_r_k"].shape)
    dr2, ddecay, dk2, daa, dbb, dv2 = _wkv_bwd(r, decay, k2, aa, bb, dy, sst, sast, vst, s_fin)
    cots = [dr1 + dr2, ddecay, dk1 + dk2, dv1 + dv2, daa, dbb, dg]
    dp_rw, gr["rwkv_shift_mu"], gr["rwkv_w0"], gr["rwkv_a0"], gr["rwkv_k_k"], gr["rwkv_k_a"], dw2p, da2p, dg2p = \
        _rwkv_pre_bwd(p_rw, cots, mu, pre_small, o512)
    gr["rwkv_w2"], gr["rwkv_a2"], gr["rwkv_g2"] = dw2p[:64], da2p[64:128], dg2p[128:]
    dproj = jnp.concatenate([dp_rw, du, dgp_r, dgp_s], axis=1)
    gr["w_in"] = _mm(dproj, h1, ta=True, name="mm_proj_dw")
    token = send_rest({n: gr[n] for n in REST_GRADS})
    dh1 = _mm(dproj, w_in + token[0, 0].astype(w_in.dtype), name="mm_proj_dx")
    dx, gr["norm_mix_pre"] = _rms_bwd(x, wt["norm_mix_pre"], dh1, dx2, "rms1_bwd")
    return loss_part[0, 0], dx, gr


def kernel(x, norm_mix_pre, norm_mix_post, norm_ffn_pre, norm_ffn_post, w_in, b_gate, rwkv_shift_mu, rwkv_w0, rwkv_w2, rwkv_a0, rwkv_a2, rwkv_g2, rwkv_k_k, rwkv_k_a, rwkv_r_k, rwkv_lnx_w, rwkv_lnx_b, s5_a_re, s5_a_im, s5_b_re, s5_b_im, s5_c_re, s5_c_im, s5_d, s5_log_step, s5_w_glu, s5_b_glu, w_branch_rwkv, w_branch_s5, w_out, ffn_w_up, ffn_conv_w, ffn_conv_b, ffn_w_down, loss_target, m_norm_mix_pre, m_norm_mix_post, m_norm_ffn_pre, m_norm_ffn_post, m_w_in, m_b_gate, m_rwkv_shift_mu, m_rwkv_w0, m_rwkv_w2, m_rwkv_a0, m_rwkv_a2, m_rwkv_g2, m_rwkv_k_k, m_rwkv_k_a, m_rwkv_r_k, m_rwkv_lnx_w, m_rwkv_lnx_b, m_s5_a_re, m_s5_a_im, m_s5_b_re, m_s5_b_im, m_s5_c_re, m_s5_c_im, m_s5_d, m_s5_log_step, m_s5_w_glu, m_s5_b_glu, m_w_branch_rwkv, m_w_branch_s5, m_w_out, m_ffn_w_up, m_ffn_conv_w, m_ffn_conv_b, m_ffn_w_down, v_norm_mix_pre, v_norm_mix_post, v_norm_ffn_pre, v_norm_ffn_post, v_w_in, v_b_gate, v_rwkv_shift_mu, v_rwkv_w0, v_rwkv_w2, v_rwkv_a0, v_rwkv_a2, v_rwkv_g2, v_rwkv_k_k, v_rwkv_k_a, v_rwkv_r_k, v_rwkv_lnx_w, v_rwkv_lnx_b, v_s5_a_re, v_s5_a_im, v_s5_b_re, v_s5_b_im, v_s5_c_re, v_s5_c_im, v_s5_d, v_s5_log_step, v_s5_w_glu, v_s5_b_glu, v_w_branch_rwkv, v_w_branch_s5, v_w_out, v_ffn_w_up, v_ffn_conv_w, v_ffn_conv_b, v_ffn_w_down):
    args = dict(locals())
    wgt = {n: args[n] for n in WEIGHTS}
    mom = {n: args["m_" + n] for n in WEIGHTS}
    var = {n: args["v_" + n] for n in WEIGHTS}
    me = 4 * lax.axis_index("x") + 2 * lax.axis_index("y") + lax.axis_index("c")
    shard_shapes = {n: wgt[n].shape[1:] for n in SHARDED}
    sent_shapes = {n: (shard_shapes[n][::-1] if n in TRANSPOSED else shard_shapes[n]) for n in SHARDED}
    sent = lambda n, a: a.T if n in TRANSPOSED else a

    unshard = lambda n, blocks: _cols_to_full(blocks) if n in SMALL_SHARDED else _rows_to_full(blocks)

    first = ("w_in", "rwkv_w2", "rwkv_a2", "rwkv_g2")
    late = tuple(n for n in SHARDED if n not in first and n not in F32_GATHER)
    late_payload = _pack([sent(n, wgt[n][0].astype(BF16)) for n in late])
    late_started = _exchange_start(late_payload, False, "gather_late_start")
    got_first = _unpack(_all_gather(_pack([sent(n, wgt[n][0].astype(BF16)) for n in first]), "all_gather_weights"),
                        [sent_shapes[n] for n in first])
    got_f32 = _unpack(_all_gather(_pack([wgt[n][0] for n in F32_GATHER]), "all_gather_taps"),
                      [sent_shapes[n] for n in F32_GATHER])
    wt = {n: wgt[n] for n in REPLICATED}
    for n, blocks in zip(first + F32_GATHER, got_first + got_f32, strict=True):
        wt[n] = unshard(n, blocks)

    def get_late(after):
        land = _exchange_wait(late_started, after, False, "gather_late_wait")
        blocks = _unpack(lax.dynamic_update_index_in_dim(land, late_payload, me, 0), [sent_shapes[n] for n in late])
        return {n: unshard(n, b_) for n, b_ in zip(late, blocks, strict=True)}

    def pack_chunks(names, grads_):
        pieces = [_pad_cols(grads_[n].reshape(N_DEV, -1).astype(BF16), PACK_ALIGN) for n in names]
        return _pad_cols(jnp.concatenate(pieces, axis=1), PACK_ROWS * LANES).reshape(N_DEV, -1, LANES)

    sent_grads = {}

    def sender(key, names):
        def send(grads_):
            chunks = pack_chunks(names, grads_)
            sent_grads[key] = (_exchange_start(chunks, True, "exchange_" + key + "_start"), chunks)
            return sent_grads[key][0][4]
        return send

    loss_part, dx, gr = _local_step(x[0], loss_target[0], wt, get_late, sender("early", EARLY_GRADS), sender("rest", REST_GRADS))

    whole = SMALL_SHARDED + REPLICATED

    def ended(key, after):
        started, chunks = sent_grads[key]
        land = _exchange_wait(started, after, True, "exchange_" + key + "_wait")
        return _sum_parts(land, "sum_" + key + "_grads", own=lax.dynamic_index_in_dim(chunks, me, 0, keepdims=False))

    early_sum, rest_sum = ended("early", dx), ended("rest", dx)
    small_sum = _sum_parts(_all_gather(_pack([gr[n] for n in whole]), "gather_small_grads"), "sum_small_grads")
    got = dict(zip(EARLY_GRADS + REST_GRADS + whole,
                   _unpack(early_sum, [sent_shapes[n] for n in EARLY_GRADS]) + _unpack(rest_sum, [sent_shapes[n] for n in REST_GRADS])
                   + _unpack(small_sum, [gr[n].shape for n in whole]), strict=True))
    grads = {}
    for n in WEIGHTS:
        if n in TRANSPOSED:
            grads[n] = got[n].T
        elif n in SMALL_SHARDED:
            cols = shard_shapes[n][1]
            grads[n] = lax.dynamic_slice_in_dim(got[n], me * cols, cols, axis=1)
        else:
            grads[n] = got[n]
        grads[n] = grads[n].reshape(wgt[n].shape)

    pack_local = lambda src: _pack([src[n].reshape(-1) for n in WEIGHTS])
    outs = _adamw(pack_local(grads), pack_local(wgt), pack_local(mom), pack_local(var))
    shapes = [wgt[n].shape for n in WEIGHTS]
    loss = lax.psum(loss_part, AXES)
    return (loss, dx[None], *[grads[n] for n in WEIGHTS], *_unpack(outs[0], shapes), *_unpack(outs[1], shapes),
            *_unpack(outs[2], shapes))
```

```python
import functools
import math

import jax
import jax.numpy as jnp
from jax import lax
from jax.experimental import pallas as pl
from jax.experimental.pallas import tpu as pltpu

F32 = jnp.float32
BF16 = jnp.bfloat16
SDS = jax.ShapeDtypeStruct
HI = lax.Precision.HIGHEST
MESH = pl.DeviceIdType.MESH
AXES = ("x", "y", "c")
N_DEV = 8

D_MODEL = 1024
RW = 512
HN = 64
N_RWKV = 1792
S5W = 512
S5G = 32
S5C = 16
S5P = 64
S5N = S5G * S5P
D_FF = 2816
NORM_EPS = 1e-6
LNX_EPS = 64e-5

ADAM_LR = 0.001
ADAM_B1 = 0.9
ADAM_B2 = 0.999
ADAM_EPS = 1e-08
ADAM_WD = 0.01
ADAM_STEP = 10

LANES = 128
SUBLANES = 8
VMEM_LIMIT = 56 * 1024 * 1024
WKV_CHUNK = 32


def _params(sem=("arbitrary",)):
    return pltpu.CompilerParams(dimension_semantics=sem, vmem_limit_bytes=VMEM_LIMIT)


def _pick(n, cap):
    best = None
    for t in range(LANES, min(n, cap) + 1, LANES):
        if n % t == 0:
            best = t
    return best or n


def _mm(a, b, *, ta=False, tb=False, out_dtype=F32, name):
    m = a.shape[1] if ta else a.shape[0]
    k = a.shape[0] if ta else a.shape[1]
    n = b.shape[0] if tb else b.shape[1]
    assert (b.shape[1] if tb else b.shape[0]) == k
    tm, tn, tk = _pick(m, 1024), _pick(n, 2304), _pick(k, 512)
    nk = k // tk
    dims = (((0 if ta else 1,), (1 if tb else 0,)), ((), ()))

    def body(a_ref, b_ref, o_ref, acc_ref):
        kk = pl.program_id(2)

        @pl.when(kk == 0)
        def _():
            acc_ref[...] = jnp.zeros_like(acc_ref)

        acc_ref[...] += lax.dot_general(a_ref[...].astype(BF16), b_ref[...].astype(BF16), dims,
                                        preferred_element_type=F32)

        @pl.when(kk == nk - 1)
        def _():
            o_ref[...] = acc_ref[...].astype(o_ref.dtype)

    a_spec = pl.BlockSpec((tk, tm), lambda i, j, kk: (kk, i)) if ta else pl.BlockSpec((tm, tk), lambda i, j, kk: (i, kk))
    b_spec = pl.BlockSpec((tn, tk), lambda i, j, kk: (j, kk)) if tb else pl.BlockSpec((tk, tn), lambda i, j, kk: (kk, j))
    return pl.pallas_call(
        body, grid=(m // tm, n // tn, nk), in_specs=[a_spec, b_spec],
        out_specs=pl.BlockSpec((tm, tn), lambda i, j, kk: (i, j)),
        out_shape=SDS((m, n), out_dtype), scratch_shapes=[pltpu.VMEM((tm, tn), F32)],
        compiler_params=_params(("parallel", "parallel", "arbitrary")), name=name)(a, b)


def _rows(fn, rows, params, out_rows, out_accs, *, name, tl, reverse=False, scratch=()):
    first = rows[0][0] if isinstance(rows[0], tuple) else rows[0]
    length = first.shape[0]
    tl = min(tl, length)
    nt = length // tl
    rmap = (lambda i: nt - 1 - i) if reverse else (lambda i: i)
    specs, arrs = [], []
    for r in rows:
        arr, wdt, cb = r if isinstance(r, tuple) else (r, r.shape[1], 0)
        specs.append(pl.BlockSpec((tl, wdt), lambda i, cb=cb: (rmap(i), cb)))
        arrs.append(arr)
    for p in params:
        specs.append(pl.BlockSpec(p.shape, lambda i, nd=p.ndim: (0,) * nd))
        arrs.append(p)
    out_shape = [SDS((length, c), dt) for c, dt in out_rows] + [SDS(s, F32) for s in out_accs]
    out_specs = [pl.BlockSpec((tl, c), lambda i: (rmap(i), 0)) for c, _ in out_rows]
    out_specs += [pl.BlockSpec(s, lambda i, nd=len(s): (0,) * nd) for s in out_accs]
    nr, npar, nor, noa = len(rows), len(params), len(out_rows), len(out_accs)

    def body(*refs):
        rin, pin = refs[:nr], refs[nr:nr + npar]
        rout = refs[nr + npar:nr + npar + nor]
        aout = refs[nr + npar + nor:nr + npar + nor + noa]
        scr = refs[nr + npar + nor + noa:]
        step = pl.program_id(0)
        outs_r, outs_a = fn(step, [r[...] for r in rin], [p[...] for p in pin], scr)
        for ref, val in zip(rout, outs_r, strict=True):
            ref[...] = val.astype(ref.dtype)

        @pl.when(step == 0)
        def _():
            for ref in aout:
                ref[...] = jnp.zeros_like(ref)

        for ref, val in zip(aout, outs_a, strict=True):
            ref[...] += val.astype(F32)

    res = pl.pallas_call(body, grid=(nt,), in_specs=specs, out_specs=out_specs, out_shape=out_shape,
                         scratch_shapes=list(scratch), compiler_params=_params(), name=name)(*arrs)
    return list(res)


def _rms(x, g):
    return x * lax.rsqrt(jnp.mean(x * x, axis=-1, keepdims=True) + NORM_EPS) * g


def _sig(x):
    return 0.5 * (jnp.tanh(0.5 * x) + 1.0)


def _softplus(x):
    return jnp.maximum(x, 0.0) + jnp.log(1.0 + jnp.exp(-jnp.abs(x)))


def _gelu(x):
    return x * (0.5 * (1.0 + jnp.tanh(math.sqrt(2.0 / math.pi) * (x + 0.044715 * (x * x * x)))))


def _bdot(a, b):
    return jnp.dot(a.astype(BF16), b.astype(BF16), preferred_element_type=F32)


def _hdot(a, b):
    return jnp.dot(a, b, precision=HI, preferred_element_type=F32)


def _segsum_impl(x, ones):
    hi = x.astype(BF16)
    lo = (x - hi.astype(F32)).astype(BF16)
    ones = ones.astype(BF16)
    return jnp.dot(jnp.concatenate([hi, lo], axis=1), jnp.concatenate([ones, ones], axis=0), preferred_element_type=F32)


@jax.custom_vjp
def _segsum(x, ones):
    return _segsum_impl(x, ones)


_segsum.defvjp(lambda x, ones: (_segsum_impl(x, ones), ones),
               lambda ones, g: (_segsum_impl(g, ones), jnp.zeros_like(ones)))


def _block_ones(n, blk):
    i = lax.broadcasted_iota(jnp.int32, (n, n), 0) // blk
    j = lax.broadcasted_iota(jnp.int32, (n, n), 1) // blk
    return (i == j).astype(F32)


def _rms_fwd(x, g, name):
    return _rows(lambda s, r, p, _: ([_rms(r[0], p[0])], []), [x], [g], [(x.shape[1], BF16)], [], name=name, tl=512)[0]


def _rms_bwd(x, g, dh, dres, name):
    def fn(s, r, p, _):
        _, vjp = jax.vjp(_rms, r[0], p[0])
        dx, dg = vjp(r[1])
        return [dx + r[2]], [dg]
    return _rows(fn, [x, dh, dres], [g], [(x.shape[1], F32)], [g.shape], name=name, tl=256)


def _rwkv_pre_math(k_, lr, w0, a0, k_k, k_a, w2p, a2p, g2p, o512):
    pre_w = w0 + _bdot(jnp.tanh(lr), w2p)
    w = -_softplus(-pre_w) - 0.5
    decay = jnp.exp(-jnp.exp(w))
    a = _sig(a0 + _bdot(lr, a2p))
    g = _bdot(_sig(lr), g2p)
    kr = k_ * k_k
    kk = kr / jnp.maximum(jnp.sqrt(_segsum(kr * kr, o512)), 1e-12)
    k2 = k_ * (1.0 + (a - 1.0) * k_a)
    return decay, k2, -kk, kk * a, g


def _shift_down(p, prev_row):
    row = lax.broadcasted_iota(jnp.int32, p.shape, 0)
    return jnp.where(row == 0, jnp.broadcast_to(prev_row, p.shape), pltpu.roll(p, 1, 0))


def _shift_up(q, next_row):
    n = q.shape[0]
    row = lax.broadcasted_iota(jnp.int32, q.shape, 0)
    return jnp.where(row == n - 1, jnp.broadcast_to(next_row, q.shape), pltpu.roll(q, n - 1, 0))


def _rwkv_pre_fwd(p, mu, small, o512):
    def fn(step, r, prm, scr):
        car = scr[0]

        @pl.when(step == 0)
        def _():
            car[...] = jnp.zeros_like(car)

        x = r[0]
        prev = _shift_down(x, car[SUBLANES - 1:SUBLANES, :])
        car[...] = x[x.shape[0] - SUBLANES:, :]
        xs = x + (prev - x) * prm[0]
        decay, k2, aa, bb, g = _rwkv_pre_math(xs[:, RW:2 * RW], xs[:, 3 * RW:], *prm[1:])
        return [xs[:, :RW], decay, k2, xs[:, 2 * RW:3 * RW], aa, bb, g], []
    return _rows(fn, [p], [mu, *small, o512], [(RW, F32)] * 7, [], name="rwkv_pre_fwd", tl=256,
                 scratch=[pltpu.VMEM((SUBLANES, N_RWKV), F32)])


def _rwkv_pre_bwd(p, cots, mu, small, o512):
    length = p.shape[0]
    tl = min(256, length)
    nt = length // tl
    rows_per = tl // SUBLANES
    params = [mu, *small, o512]
    acc_shapes = [mu.shape] + [q.shape for q in small]
    nr, npar, nacc = 2 + len(cots), len(params), len(acc_shapes)

    def body(*refs):
        rin, pin = refs[:nr], refs[nr:nr + npar]
        dp_ref = refs[nr + npar]
        aout = refs[nr + npar + 1:nr + npar + 1 + nacc]
        car_q = refs[nr + npar + 1 + nacc]
        step = pl.program_id(0)

        @pl.when(step == 0)
        def _():
            car_q[...] = jnp.zeros_like(car_q)
            for ref in aout:
                ref[...] = jnp.zeros_like(ref)

        x = rin[0][...]
        prev_row = jnp.where(step == nt - 1, 0.0, rin[1][SUBLANES - 1:SUBLANES, :])
        dr, ddecay, dk2, dv, daa, dbb, dg = [r[...] for r in rin[2:]]
        prm = [q[...] for q in pin]
        mu_, o512_ = prm[0], prm[-1]
        prev = _shift_down(x, prev_row)
        xs = x + (prev - x) * mu_
        _, vjp = jax.vjp(lambda k_, lr, *w: _rwkv_pre_math(k_, lr, *w, o512_), xs[:, RW:2 * RW], xs[:, 3 * RW:], *prm[1:-1])
        dk_, dlr, *dsmall = vjp((ddecay, dk2, daa, dbb, dg))
        dxs = jnp.concatenate([dr, dk_, dv, dlr], axis=1)
        q = dxs * mu_
        dp_ref[...] = (dxs - q + _shift_up(q, car_q[0:1, :])).astype(dp_ref.dtype)
        car_q[...] = q[:SUBLANES, :]
        aout[0][...] += jnp.sum((prev - x) * dxs, axis=0, keepdims=True)
        for ref, val in zip(aout[1:], dsmall, strict=True):
            ref[...] += val

    rmap = lambda i: nt - 1 - i
    specs = [pl.BlockSpec((tl, N_RWKV), lambda i: (rmap(i), 0)),
             pl.BlockSpec((SUBLANES, N_RWKV), lambda i: (jnp.maximum(rmap(i) * rows_per - 1, 0), 0))]
    specs += [pl.BlockSpec((tl, RW), lambda i: (rmap(i), 0)) for _ in cots]
    specs += [pl.BlockSpec(q.shape, lambda i, nd=q.ndim: (0,) * nd) for q in params]
    out_shape = [SDS((length, N_RWKV), BF16)] + [SDS(sh, F32) for sh in acc_shapes]
    out_specs = [pl.BlockSpec((tl, N_RWKV), lambda i: (rmap(i), 0))]
    out_specs += [pl.BlockSpec(sh, lambda i, nd=len(sh): (0,) * nd) for sh in acc_shapes]
    res = pl.pallas_call(body, grid=(nt,), in_specs=specs, out_specs=out_specs, out_shape=out_shape,
                         scratch_shapes=[pltpu.VMEM((SUBLANES, N_RWKV), F32)],
                         compiler_params=_params(), name="rwkv_pre_bwd")(p, p, *cots, *params)
    return list(res)


def _rwkv_post_math(y, r, k2, v, g, lnx_w, lnx_b, r_k, o512):
    mean = _segsum(y, o512) * (1.0 / HN)
    yc = y - mean
    var = _segsum(yc * yc, o512) * (1.0 / HN)
    yn = yc * lax.rsqrt(var + LNX_EPS) * lnx_w + lnx_b
    bonus = _segsum(r * k2 * r_k, o512) * v
    return (yn + bonus) * g


def _rwkv_post_fwd(y, r, k2, v, g, prm, o512):
    return _rows(lambda s, rr, p, _: ([_rwkv_post_math(*rr, *p)], []), [y, r, k2, v, g], [*prm, o512],
                 [(RW, BF16)], [], name="rwkv_post_fwd", tl=256)[0]


def _rwkv_post_bwd(y, r, k2, v, g, dout, prm, o512):
    def fn(s, rr, p, _):
        o = p[-1]
        _, vjp = jax.vjp(lambda *a: _rwkv_post_math(*a, o), *rr[:5], *p[:-1])
        gr = vjp(rr[5])
        return list(gr[:5]), list(gr[5:])
    return _rows(fn, [y, r, k2, v, g, dout], [*prm, o512], [(RW, F32)] * 5, [q.shape for q in prm],
                 name="rwkv_post_bwd", tl=256)


def _glu_math(ys, t, b):
    return ys * _sig(t + b)


def _glu_fwd(ys, t, b):
    return _rows(lambda s, r, p, _: ([_glu_math(r[0], r[1], p[0])], []), [ys, t], [b], [(S5W, BF16)], [],
                 name="s5_glu_fwd", tl=512)[0]


def _glu_bwd(ys, t, dout, b):
    def fn(s, r, p, _):
        _, vjp = jax.vjp(_glu_math, r[0], r[1], p[0])
        dys, dt, db = vjp(r[2])
        return [dys, dt], [db]
    return _rows(fn, [ys, t, dout], [b], [(S5W, F32), (S5W, BF16)], [b.shape], name="s5_glu_bwd", tl=512)


def _merge_math(gp_r, gp_s, o_r, o_s, b_r, b_s):
    return _sig(gp_r + b_r) * o_r + _sig(gp_s + b_s) * o_s


def _merge_fwd(gpre, o_r, o_s, b_r, b_s):
    return _rows(lambda s, r, p, _: ([_merge_math(*r, *p)], []),
                 [(gpre, D_MODEL, 0), (gpre, D_MODEL, 1), o_r, o_s], [b_r, b_s], [(D_MODEL, BF16)], [],
                 name="merge_fwd", tl=256)[0]


def _merge_bwd(gpre, o_r, o_s, dmi, b_r, b_s):
    def fn(s, r, p, _):
        _, vjp = jax.vjp(_merge_math, *r[:4], *p)
        dgr, dgs, dor, dos, dbr, dbs = vjp(r[4])
        return [dgr, dgs, dor, dos], [dbr, dbs]
    return _rows(fn, [(gpre, D_MODEL, 0), (gpre, D_MODEL, 1), o_r, o_s, dmi], [b_r, b_s], [(D_MODEL, BF16)] * 4,
                 [b_r.shape, b_s.shape], name="merge_bwd", tl=256)


def _mid_fwd(x, mixed, g_post, g_pre):
    def fn(s, r, p, _):
        x2 = r[0] + _rms(r[1], p[0])
        return [x2, _rms(x2, p[1])], []
    return _rows(fn, [x, mixed], [g_post, g_pre], [(D_MODEL, F32), (D_MODEL, BF16)], [], name="mid_fwd", tl=256)


def _mid_bwd(x2, mixed, dh2, dx3, g_post, g_pre):
    def fn(s, r, p, _):
        _, vjp1 = jax.vjp(_rms, r[0], p[1])
        dx2, dg_pre = vjp1(r[2])
        dx2 = dx2 + r[3]
        _, vjp2 = jax.vjp(_rms, r[1], p[0])
        dmixed, dg_post = vjp2(dx2)
        return [dx2, dmixed], [dg_post, dg_pre]
    return _rows(fn, [x2, mixed, dh2, dx3], [g_post, g_pre], [(D_MODEL, F32), (D_MODEL, BF16)], [g_post.shape, g_pre.shape],
                 name="mid_bwd", tl=256)


def _final(x2, f, tgt, g_post):
    def fn(s, r, p, _):
        y, vjp = jax.vjp(_rms, r[1], p[0])
        diff = r[0] + y - r[2]
        dx3 = diff * (1.0 / D_MODEL)
        df, dg = vjp(dx3)
        part = 0.5 * jnp.sum(jnp.sum(diff * diff, axis=1, keepdims=True), axis=0, keepdims=True) * (1.0 / D_MODEL)
        return [dx3, df], [dg, jnp.broadcast_to(part, (1, LANES))]
    return _rows(fn, [x2, f, tgt], [g_post], [(D_MODEL, F32), (D_MODEL, BF16)], [g_post.shape, (1, LANES)], name="final", tl=256)


def _conv_taps(z, car):
    row = lax.broadcasted_iota(jnp.int32, z.shape, 0)
    z1 = jnp.where(row == 0, jnp.broadcast_to(car[7:8, :], z.shape), pltpu.roll(z, 1, 0))
    z2 = pltpu.roll(z, 2, 0)
    z2 = jnp.where(row == 0, jnp.broadcast_to(car[6:7, :], z.shape), z2)
    z2 = jnp.where(row == 1, jnp.broadcast_to(car[7:8, :], z.shape), z2)
    return z1, z2


def _conv(z, car, w, b):
    z1, z2 = _conv_taps(z, car)
    return b + w[0:1, :] * z2 + w[1:2, :] * z1 + w[2:3, :] * z, z1, z2


def _conv_fwd(z, conv_w, conv_b):
    length = z.shape[0]
    tl = min(256, length)
    nt = length // tl
    tc = _pick(D_FF, 1536)
    nb = D_FF // tc

    def body(zg_ref, zv_ref, wg_ref, wv_ref, bg_ref, bv_ref, o_ref, cg, cv):
        @pl.when(pl.program_id(1) == 0)
        def _():
            cg[...] = jnp.zeros_like(cg)
            cv[...] = jnp.zeros_like(cv)

        zg, zv = zg_ref[...], zv_ref[...]
        gate, _, _ = _conv(zg, cg[...], wg_ref[...], bg_ref[...])
        val, _, _ = _conv(zv, cv[...], wv_ref[...], bv_ref[...])
        cg[...] = zg[tl - SUBLANES:, :]
        cv[...] = zv[tl - SUBLANES:, :]
        o_ref[...] = (_gelu(gate) * val).astype(o_ref.dtype)

    zspec = lambda off: pl.BlockSpec((tl, tc), lambda j, i: (i, j + off))
    wspec = lambda off, r: pl.BlockSpec((r, tc), lambda j, i: (0, j + off))
    return pl.pallas_call(
        body, grid=(nb, nt),
        in_specs=[zspec(0), zspec(nb), wspec(0, 3), wspec(nb, 3), wspec(0, 1), wspec(nb, 1)],
        out_specs=pl.BlockSpec((tl, tc), lambda j, i: (i, j)), out_shape=SDS((length, D_FF), BF16),
        scratch_shapes=[pltpu.VMEM((SUBLANES, tc), F32)] * 2,
        compiler_params=_params(("arbitrary", "arbitrary")), name="conv_fwd")(z, z, conv_w, conv_w, conv_b, conv_b)


def _conv_bwd(z, dact, conv_w, conv_b):
    length = z.shape[0]
    tl = min(256, length)
    nt = length // tl
    tc = _pick(D_FF, 1536)
    nb = D_FF // tc
    rows_per = tl // SUBLANES

    def half_bwd(dzc, z, z1, z2, w, dcar):
        n = tl
        row = lax.broadcasted_iota(jnp.int32, dzc.shape, 0)
        u1 = jnp.where(row == n - 1, jnp.broadcast_to(dcar[0:1, :], dzc.shape), pltpu.roll(dzc, n - 1, 0))
        u2 = pltpu.roll(dzc, n - 2, 0)
        u2 = jnp.where(row == n - 2, jnp.broadcast_to(dcar[0:1, :], dzc.shape), u2)
        u2 = jnp.where(row == n - 1, jnp.broadcast_to(dcar[1:2, :], dzc.shape), u2)
        dz = w[2:3, :] * dzc + w[1:2, :] * u1 + w[0:1, :] * u2
        dw = jnp.concatenate([jnp.sum(dzc * z2, axis=0, keepdims=True), jnp.sum(dzc * z1, axis=0, keepdims=True),
                              jnp.sum(dzc * z, axis=0, keepdims=True)], axis=0)
        return dz, dw, jnp.sum(dzc, axis=0, keepdims=True)

    def body(zg_ref, zv_ref, pg_ref, pv_ref, da_ref, wg_ref, wv_ref, bg_ref, bv_ref,
             dzg_ref, dzv_ref, dwg_ref, dwv_ref, dbg_ref, dbv_ref, cg, cv):
        step = pl.program_id(1)

        @pl.when(step == 0)
        def _():
            cg[...] = jnp.zeros_like(cg)
            cv[...] = jnp.zeros_like(cv)
            for ref in (dwg_ref, dwv_ref, dbg_ref, dbv_ref):
                ref[...] = jnp.zeros_like(ref)

        is_first_tile = step == nt - 1
        zg, zv = zg_ref[...], zv_ref[...]
        pg = jnp.where(is_first_tile, 0.0, pg_ref[...])
        pv = jnp.where(is_first_tile, 0.0, pv_ref[...])
        wg, wv = wg_ref[...], wv_ref[...]
        gate, zg1, zg2 = _conv(zg, pg, wg, bg_ref[...])
        val, zv1, zv2 = _conv(zv, pv, wv, bv_ref[...])
        act_g, vjp = jax.vjp(_gelu, gate)
        da = da_ref[...]
        dgate = vjp(da * val)[0]
        dval = da * act_g
        dzg, dwg, dbg = half_bwd(dgate, zg, zg1, zg2, wg, cg[...])
        dzv, dwv, dbv = half_bwd(dval, zv, zv1, zv2, wv, cv[...])
        cg[...] = dgate[:SUBLANES, :]
        cv[...] = dval[:SUBLANES, :]
        dzg_ref[...] = dzg.astype(dzg_ref.dtype)
        dzv_ref[...] = dzv.astype(dzv_ref.dtype)
        dwg_ref[...] += dwg
        dwv_ref[...] += dwv
        dbg_ref[...] += dbg
        dbv_ref[...] += dbv

    rmap = lambda i: nt - 1 - i
    zspec = lambda off: pl.BlockSpec((tl, tc), lambda j, i: (rmap(i), j + off))
    pspec = lambda off: pl.BlockSpec((SUBLANES, tc), lambda j, i: (jnp.maximum(rmap(i) * rows_per - 1, 0), j + off))
    wspec = lambda off, r: pl.BlockSpec((r, tc), lambda j, i: (0, j + off))
    out_w = lambda r: pl.BlockSpec((r, tc), lambda j, i: (0, j))
    dzg, dzv, dwg, dwv, dbg, dbv = pl.pallas_call(
        body, grid=(nb, nt),
        in_specs=[zspec(0), zspec(nb), pspec(0), pspec(nb), pl.BlockSpec((tl, tc), lambda j, i: (rmap(i), j)),
                  wspec(0, 3), wspec(nb, 3), wspec(0, 1), wspec(nb, 1)],
        out_specs=[pl.BlockSpec((tl, tc), lambda j, i: (rmap(i), j))] * 2 + [out_w(3), out_w(3), out_w(1), out_w(1)],
        out_shape=[SDS((length, D_FF), BF16)] * 2 + [SDS((3, D_FF), F32)] * 2 + [SDS((1, D_FF), F32)] * 2,
        scratch_shapes=[pltpu.VMEM((SUBLANES, tc), F32)] * 2,
        compiler_params=_params(("arbitrary", "arbitrary")), name="conv_bwd")(z, z, z, z, dact, conv_w, conv_w, conv_b, conv_b)
    return dzg, dzv, jnp.concatenate([dwg, dwv], axis=1), jnp.concatenate([dbg, dbv], axis=1)


def _s5_prep_math(a_re, a_im, ls, b_re, b_im):
    dt = jnp.exp(ls)
    er = jnp.exp(a_re * dt)
    ph = a_im * dt
    abr, abi = er * jnp.cos(ph), er * jnp.sin(ph)
    den = a_re * a_re + a_im * a_im
    nr = abr - 1.0
    cr = (nr * a_re + abi * a_im) / den
    ci = (abi * a_re - nr * a_im) / den
    return abr, abi, cr * b_re - ci * b_im, cr * b_im + ci * b_re


def _s5_prep_fwd(a_re, a_im, ls, b_re, b_im):
    def body(ar, ai, l, br, bi, o1, o2, o3, o4):
        for ref, val in zip((o1, o2, o3, o4), _s5_prep_math(ar[...], ai[...], l[...], br[...], bi[...]), strict=True):
            ref[...] = val
    return pl.pallas_call(body, out_shape=[SDS((1, S5N), F32)] * 2 + [SDS((S5C, S5N), F32)] * 2,
                          name="s5_prep_fwd")(a_re, a_im, ls, b_re, b_im)


def _s5_prep_bwd(a_re, a_im, ls, b_re, b_im, cots, gsel):
    def body(ar, ai, l, br, bi, c1, c2, c3, c4, g_ref, o1, o2, o3, o4, o5):
        _, vjp = jax.vjp(_s5_prep_math, ar[...], ai[...], l[...], br[...], bi[...])
        d_ar, d_ai, d_ls, d_br, d_bi = vjp((c1[...], c2[...], c3[...], c4[...]))
        o1[...] = d_ar
        o2[...] = d_ai
        o3[...] = _hdot(jnp.broadcast_to(d_ls, (SUBLANES, S5N)), g_ref[...])
        o4[...] = d_br
        o5[...] = d_bi
    return pl.pallas_call(body, out_shape=[SDS((1, S5N), F32)] * 2 + [SDS((SUBLANES, LANES), F32)] + [SDS((S5C, S5N), F32)] * 2,
                          name="s5_prep_bwd")(a_re, a_im, ls, b_re, b_im, *cots, gsel)


def _cmul(ar, ai, br, bi):
    return ar * br - ai * bi, ar * bi + ai * br


def _s5_powers(abr, abi):
    shp = (SUBLANES, S5N)
    a1 = (jnp.broadcast_to(abr, shp), jnp.broadcast_to(abi, shp))
    a2 = _cmul(*a1, *a1)
    a4 = _cmul(*a2, *a2)
    row = lax.broadcasted_iota(jnp.int32, shp, 0)
    pr, pi = a1
    cur = a1
    for i in range(1, SUBLANES):
        cur = _cmul(*cur, *a1)
        pr = jnp.where(row == i, cur[0], pr)
        pi = jnp.where(row == i, cur[1], pi)
    return a1, a2, a4, (pr, pi)


def _s5_scan(sre_ref, sim_ref, car_re, car_im, abr, abi, n_rows, reverse):
    a1, a2, a4, (pr, pi) = _s5_powers(abr, abi)
    sgn = -1.0 if reverse else 1.0
    row = lax.broadcasted_iota(jnp.int32, (SUBLANES, S5N), 0)
    if reverse:
        qr, qi = pr, pi
        for i in range(SUBLANES):
            src = SUBLANES - 1 - i
            qr = jnp.where(row == i, jnp.broadcast_to(pr[src:src + 1, :], pr.shape), qr)
            qi = jnp.where(row == i, jnp.broadcast_to(pi[src:src + 1, :], pi.shape), qi)
        pr, pi = qr, qi
    nblk = n_rows // SUBLANES

    def blk(i, carry):
        cr, ci = carry
        b = (nblk - 1 - i) if reverse else i
        sl = pl.ds(pl.multiple_of(b * SUBLANES, SUBLANES), SUBLANES)
        xr, xi = sre_ref[sl, :], sim_ref[sl, :]
        for kk, (er, ei) in ((1, a1), (2, a2), (4, a4)):
            if reverse:
                sr = jnp.where(row < SUBLANES - kk, pltpu.roll(xr, SUBLANES - kk, 0), 0.0)
                si = jnp.where(row < SUBLANES - kk, pltpu.roll(xi, SUBLANES - kk, 0), 0.0)
            else:
                sr = jnp.where(row >= kk, pltpu.roll(xr, kk, 0), 0.0)
                si = jnp.where(row >= kk, pltpu.roll(xi, kk, 0), 0.0)
            dr, di = _cmul(er, sgn * ei, sr, si)
            xr, xi = xr + dr, xi + di
        dr, di = _cmul(pr, sgn * pi, cr, ci)
        xr, xi = xr + dr, xi + di
        sre_ref[sl, :] = xr
        sim_ref[sl, :] = xi
        edge = 0 if reverse else SUBLANES - 1
        return (jnp.broadcast_to(xr[edge:edge + 1, :], xr.shape), jnp.broadcast_to(xi[edge:edge + 1, :], xi.shape))

    cr, ci = lax.fori_loop(0, nblk, blk, (car_re[...], car_im[...]))
    car_re[...] = cr
    car_im[...] = ci


S5_BLK = 4
S5_BN = S5N // S5_BLK


def _split2(x):
    hi = x.astype(BF16)
    return hi, (x - hi.astype(F32)).astype(BF16)


def _dot3(a, b, ca, cb):
    ah, al = _split2(a)
    bh, bl = _split2(b)
    return lax.dot_general(jnp.concatenate([ah, ah, al], axis=ca), jnp.concatenate([bh, bl, bh], axis=cb),
                           (((ca,), (cb,)), ((), ())), preferred_element_type=F32)


def _dot1(a, b, ca, cb):
    return lax.dot_general(a.astype(BF16), b.astype(BF16), (((ca,), (cb,)), ((), ())), preferred_element_type=F32)


def _s5_project_in(u, bblk_ref, sre_ref, sim_ref):
    for j in range(S5_BLK):
        bu = _dot1(u[:, j * LANES:(j + 1) * LANES], bblk_ref[j], 1, 0)
        sre_ref[:, j * S5_BN:(j + 1) * S5_BN] = bu[:, :S5_BN]
        sim_ref[:, j * S5_BN:(j + 1) * S5_BN] = bu[:, S5_BN:]


def _s5_project_out(u, d, cblk_ref, sre_ref, sim_ref):
    ys = []
    for j in range(S5_BLK):
        sl = slice(j * S5_BN, (j + 1) * S5_BN)
        ys.append(_dot1(sre_ref[:, sl], cblk_ref[j, :S5_BN, :], 1, 0) + _dot1(sim_ref[:, sl], cblk_ref[j, S5_BN:, :], 1, 0))
    return jnp.concatenate(ys, axis=1) + d * u


def _s5_fwd(u, bblk, cblk, abr, abi, d):
    length = u.shape[0]
    tl = min(256, length)
    nt = length // tl

    def body(u_ref, b_ref, c_ref, ar_ref, ai_ref, d_ref, ys_ref, kr_ref, ki_ref, sre, sim, car_re, car_im):
        @pl.when(pl.program_id(0) == 0)
        def _():
            car_re[...] = jnp.zeros_like(car_re)
            car_im[...] = jnp.zeros_like(car_im)

        kr_ref[0] = car_re[...]
        ki_ref[0] = car_im[...]
        u_ = u_ref[...]
        _s5_project_in(u_, b_ref, sre, sim)
        _s5_scan(sre, sim, car_re, car_im, ar_ref[...], ai_ref[...], tl, False)
        ys_ref[...] = _gelu(_s5_project_out(u_, d_ref[...], c_ref, sre, sim))

    full = lambda a: pl.BlockSpec(a.shape, lambda i, nd=a.ndim: (0,) * nd)
    chk = pl.BlockSpec((1, SUBLANES, S5N), lambda i: (i, 0, 0))
    return pl.pallas_call(
        body, grid=(nt,), in_specs=[pl.BlockSpec((tl, S5W), lambda i: (i, 0)), full(bblk), full(cblk), full(abr), full(abi), full(d)],
        out_specs=[pl.BlockSpec((tl, S5W), lambda i: (i, 0)), chk, chk],
        out_shape=[SDS((length, S5W), F32), SDS((nt, SUBLANES, S5N), F32), SDS((nt, SUBLANES, S5N), F32)],
        scratch_shapes=[pltpu.VMEM((tl, S5N), F32)] * 2 + [pltpu.VMEM((SUBLANES, S5N), F32)] * 2,
        compiler_params=_params(), name="s5_fwd")(u, bblk, cblk, abr, abi, d)


def _s5_bwd(u, dys_a, dys_b, kre, kim, bblk, cblk, abr, abi, d):
    length = u.shape[0]
    tl = min(256, length)
    nt = length // tl

    def body(u_ref, da_ref, db_ref, kr_ref, ki_ref, b_ref, c_ref, ar_ref, ai_ref, d_ref,
             du_ref, dB_ref, dC_ref, dar_ref, dai_ref, dd_ref, sre, sim, gre, gim, car_re, car_im, dcar_re, dcar_im):
        @pl.when(pl.program_id(0) == 0)
        def _():
            dcar_re[...] = jnp.zeros_like(dcar_re)
            dcar_im[...] = jnp.zeros_like(dcar_im)
            for ref in (dB_ref, dC_ref, dar_ref, dai_ref, dd_ref):
                ref[...] = jnp.zeros_like(ref)

        u_ = u_ref[...]
        abr_, abi_, d_ = ar_ref[...], ai_ref[...], d_ref[...]
        car_re[...] = kr_ref[0]
        car_im[...] = ki_ref[0]
        _s5_project_in(u_, b_ref, sre, sim)
        _s5_scan(sre, sim, car_re, car_im, abr_, abi_, tl, False)
        y = _s5_project_out(u_, d_, c_ref, sre, sim)
        _, vjp = jax.vjp(_gelu, y)
        dy = vjp(da_ref[...] + db_ref[...])[0]
        dd_ref[...] += jnp.sum(dy * u_, axis=0, keepdims=True)
        for j in range(S5_BLK):
            sl = slice(j * S5_BN, (j + 1) * S5_BN)
            dyj = dy[:, j * LANES:(j + 1) * LANES]
            gre[:, sl] = _dot1(dyj, c_ref[j, :S5_BN, :], 1, 1)
            gim[:, sl] = _dot1(dyj, c_ref[j, S5_BN:, :], 1, 1)
            dC_ref[j, :S5_BN, :] += _dot1(sre[:, sl], dyj, 0, 0)
            dC_ref[j, S5_BN:, :] += _dot1(sim[:, sl], dyj, 0, 0)
        _s5_scan(gre, gim, dcar_re, dcar_im, abr_, abi_, tl, True)
        gr, gi = gre[...], gim[...]
        pr = _shift_down(sre[...], kr_ref[0, 0:1, :])
        pi = _shift_down(sim[...], ki_ref[0, 0:1, :])
        dar_ref[...] += jnp.sum(gr * pr + gi * pi, axis=0, keepdims=True)
        dai_ref[...] += jnp.sum(gi * pr - gr * pi, axis=0, keepdims=True)
        dus = []
        for j in range(S5_BLK):
            sl = slice(j * S5_BN, (j + 1) * S5_BN)
            uj = u_[:, j * LANES:(j + 1) * LANES]
            dus.append(_dot1(gre[:, sl], b_ref[j, :, :S5_BN], 1, 1) + _dot1(gim[:, sl], b_ref[j, :, S5_BN:], 1, 1))
            dB_ref[j, :, :S5_BN] += _dot1(uj, gre[:, sl], 0, 0)
            dB_ref[j, :, S5_BN:] += _dot1(uj, gim[:, sl], 0, 0)
        du_ref[...] = (jnp.concatenate(dus, axis=1) + d_ * dy).astype(du_ref.dtype)

    rmap = lambda i: nt - 1 - i
    full = lambda a: pl.BlockSpec(a.shape, lambda i, nd=a.ndim: (0,) * nd)
    row = pl.BlockSpec((tl, S5W), lambda i: (rmap(i), 0))
    chk = pl.BlockSpec((1, SUBLANES, S5N), lambda i: (rmap(i), 0, 0))
    return pl.pallas_call(
        body, grid=(nt,), in_specs=[row, row, row, chk, chk, full(bblk), full(cblk), full(abr), full(abi), full(d)],
        out_specs=[row, full(bblk), full(cblk), full(abr), full(abi), full(d)],
        out_shape=[SDS((length, S5W), BF16), SDS(bblk.shape, F32), SDS(cblk.shape, F32), SDS(abr.shape, F32),
                   SDS(abi.shape, F32), SDS(d.shape, F32)],
        scratch_shapes=[pltpu.VMEM((tl, S5N), F32)] * 4 + [pltpu.VMEM((SUBLANES, S5N), F32)] * 4,
        compiler_params=_params(), name="s5_bwd")(u, dys_a, dys_b, kre, kim, bblk, cblk, abr, abi, d)


def _s5_blockdiag_in(bb):
    t = bb.reshape(S5C, S5_BLK, S5_BN).transpose(1, 0, 2)
    t = jnp.tile(t, (1, SUBLANES, 1))
    mask = (lax.broadcasted_iota(jnp.int32, (LANES, S5_BN), 0) // S5C) == (lax.broadcasted_iota(jnp.int32, (LANES, S5_BN), 1) // S5P)
    return jnp.where(mask[None], t, 0.0)


def _s5_blockdiag_in_t(dblk):
    t = dblk.reshape(S5_BLK, SUBLANES, S5C, SUBLANES, S5P)
    t = jnp.diagonal(t, axis1=1, axis2=3)
    return t.transpose(1, 0, 3, 2).reshape(S5C, S5N)


def _s5_blockdiag_out(c):
    t = c.reshape(S5_BLK, SUBLANES, S5C, S5P).transpose(0, 3, 1, 2).reshape(S5_BLK, S5P, LANES)
    t = jnp.tile(t, (1, SUBLANES, 1))
    mask = (lax.broadcasted_iota(jnp.int32, (S5_BN, LANES), 0) // S5P) == (lax.broadcasted_iota(jnp.int32, (S5_BN, LANES), 1) // S5C)
    return jnp.where(mask[None], t, 0.0)


def _s5_blockdiag_out_t(dblk):
    t = dblk.reshape(S5_BLK, SUBLANES, S5P, SUBLANES, S5C)
    t = jnp.diagonal(t, axis1=1, axis2=3)
    return t.transpose(0, 3, 2, 1).reshape(S5G, S5C, S5P)


def _wkv_consts():
    lane = lax.broadcasted_iota(jnp.int32, (4 * HN, LANES), 1)
    row = lax.broadcasted_iota(jnp.int32, (4 * HN, LANES), 0)
    diag = ((lane % HN) == (row % HN)).astype(F32)
    ones = _block_ones(LANES, HN).astype(BF16)
    return lane % HN, diag, jnp.concatenate([ones, ones], axis=0)


def _wkv_split(p):
    hi = p.astype(BF16)
    lo = (p - hi.astype(F32)).astype(BF16)
    return jnp.concatenate([hi, lo], axis=1)


def _wkv_rowseg(p, ones):
    return jnp.dot(_wkv_split(p), ones, preferred_element_type=F32)


def _wkv_rowseg_all(lhs_ref, ones, out_ref):
    c, ns = lhs_ref.shape[0], lhs_ref.shape[1]
    out_ref[...] = jnp.dot(lhs_ref[...].reshape(c * ns, 2 * LANES), ones, preferred_element_type=F32).reshape(c, ns, LANES)


def _wkv_full(qb_ref, t):
    q = qb_ref[t]
    return jnp.concatenate([jnp.tile(q[:, hp * LANES:(hp + 1) * LANES], (SUBLANES, 1)) for hp in range(4)], axis=0)


def _wkv_colsum8(x):
    return jnp.concatenate([x[hp * HN:(hp + 1) * HN].reshape(SUBLANES, SUBLANES, LANES).sum(axis=0) for hp in range(4)], axis=1)


def _wkv_prebroadcast(src_refs, dst_refs, n):
    rep = (lax.broadcasted_iota(jnp.int32, (n * SUBLANES, n), 0) // SUBLANES
           == lax.broadcasted_iota(jnp.int32, (n * SUBLANES, n), 1)).astype(BF16)
    rep3 = jnp.concatenate([rep, rep, rep], axis=1)
    for s, dref in zip(src_refs, dst_refs, strict=True):
        x = s[...]
        hi = x.astype(BF16)
        rest = x - hi.astype(F32)
        mid = rest.astype(BF16)
        lo = (rest - mid.astype(F32)).astype(BF16)
        dref[...] = jnp.dot(rep3, jnp.concatenate([hi, mid, lo], axis=0), preferred_element_type=F32).reshape(n, SUBLANES, RW)


def _wkv_pair_sums(lhs_ref, out_ref):
    n, ns = out_ref.shape[0], out_ref.shape[1]
    ones_pair = _block_ones(2 * LANES, HN).astype(BF16)
    out_ref[...] = jnp.dot(lhs_ref[0:n].reshape(n * ns, 2 * LANES), ones_pair,
                           preferred_element_type=F32).reshape(n, ns, 2 * LANES)


def _wkv_transpose_out(acc, c, diag):
    lane = lax.broadcasted_iota(jnp.int32, (c, LANES), 1)
    outs = []
    for hp in range(4):
        zp = lax.dot_general(acc[hp * HN:(hp + 1) * HN], diag[:HN], (((0,), (0,)), ((), ())), precision=HI,
                             preferred_element_type=F32)
        outs.append(jnp.where(lane < HN, zp[:c], zp[HN:HN + c]))
    return jnp.concatenate(outs, axis=1)


def _wkv_fwd(r, w, k, v, a, b):
    length = r.shape[0]
    c = min(WKV_CHUNK, length)
    nc = length // c
    ns = 4 * HN

    def body(r_ref, w_ref, k_ref, v_ref, a_ref, b_ref, y_ref, sst_ref, sast_ref, vst_ref, fin_ref,
             s_ref, acc_ref, lhs_ref, ysum_ref, rb, wb, kb, vb, ab, bb, wab, beb, kab):
        @pl.when(pl.program_id(0) == 0)
        def _():
            s_ref[...] = jnp.zeros_like(s_ref)

        a_next = pltpu.roll(a_ref[...], c - 1, 0)
        o512 = _block_ones(RW, HN)
        _wkv_prebroadcast((r_ref, w_ref, k_ref, v_ref, a_ref, b_ref, w_ref[...] * a_next,
                           _segsum_impl(b_ref[...] * a_next, o512), _segsum_impl(k_ref[...] * a_next, o512)),
                          (rb, wb, kb, vb, ab, bb, wab, beb, kab), c)
        lane_t, diag, ones = _wkv_consts()

        def fill_v(t, carry):
            lhs_ref[t] = _wkv_split(_wkv_full(vb, t) * diag)
            return carry

        lax.fori_loop(0, c, fill_v, 0)
        _wkv_rowseg_all(lhs_ref, ones, vst_ref)

        def pair(p, carry):
            t0, t1 = 2 * p, 2 * p + 1
            s0 = s_ref[...]
            both = _wkv_rowseg(jnp.concatenate([s0 * _wkv_full(ab, t0), s0 * _wkv_full(wab, t0)], axis=0), ones)
            sa0, v0 = both[:ns], vst_ref[t0]
            s1 = s0 * _wkv_full(wb, t0) + sa0 * _wkv_full(bb, t0) + v0 * _wkv_full(kb, t0)
            sa1 = both[ns:] + sa0 * _wkv_full(beb, t0) + v0 * _wkv_full(kab, t0)
            s2 = s1 * _wkv_full(wb, t1) + sa1 * _wkv_full(bb, t1) + vst_ref[t1] * _wkv_full(kb, t1)
            sst_ref[t0] = s0
            sast_ref[t0] = sa0
            sst_ref[t1] = s1
            sast_ref[t1] = sa1
            s_ref[...] = s2
            lhs_ref[p, :, 0:LANES] = (s1 * _wkv_full(rb, t0)).astype(BF16)
            lhs_ref[p, :, LANES:2 * LANES] = (s2 * _wkv_full(rb, t1)).astype(BF16)
            return carry

        lax.fori_loop(0, c // 2, pair, 0)
        _wkv_pair_sums(lhs_ref, ysum_ref)
        acc_ref[...] = jnp.zeros_like(acc_ref)

        def gather_y(p, carry):
            both = ysum_ref[p]
            acc = jnp.where(lane_t == 2 * p, both[:, :LANES], acc_ref[...])
            acc_ref[...] = jnp.where(lane_t == 2 * p + 1, both[:, LANES:], acc)
            return carry

        lax.fori_loop(0, c // 2, gather_y, 0)
        y_ref[...] = _wkv_transpose_out(acc_ref[...], c, diag)
        fin_ref[...] = s_ref[...]

    row = pl.BlockSpec((c, RW), lambda i: (i, 0))
    st = pl.BlockSpec((c, ns, LANES), lambda i: (i, 0, 0))
    state = pltpu.VMEM((ns, LANES), F32)
    return pl.pallas_call(
        body, grid=(nc,), in_specs=[row] * 6,
        out_specs=[row, st, st, st, pl.BlockSpec((ns, LANES), lambda i: (0, 0))],
        out_shape=[SDS((length, RW), F32)] + [SDS((length, ns, LANES), F32)] * 3 + [SDS((ns, LANES), F32)],
        scratch_shapes=[state] * 2 + [pltpu.VMEM((c, ns, 2 * LANES), BF16), pltpu.VMEM((c // 2, ns, 2 * LANES), F32)]
        + [pltpu.VMEM((c, SUBLANES, RW), F32)] * 9,
        compiler_params=_params(), name="wkv_fwd")(r, w, k, v, a, b)


def _wkv_bwd(r, w, k, a, b, dy, sst, sast, vst, fin):
    length = r.shape[0]
    c = min(WKV_CHUNK, length)
    nc = length // c
    ns = 4 * HN

    def body(r_ref, w_ref, k_ref, a_ref, b_ref, dy_ref, sst_ref, sast_ref, vst_ref, snext_ref, fin_ref,
             dr_ref, dw_ref, dk_ref, da_ref, db_ref, dv_ref,
             ds_ref, cur_ref, acc_ref, dyst, lhs_ref, dvsum_ref, rb, wb, kb, ab, bb, dyb, wbb, alb, rhb,
             p_r, p_w, p_k, p_a, p_b):
        @pl.when(pl.program_id(0) == 0)
        def _():
            ds_ref[...] = jnp.zeros_like(ds_ref)

        b_prev = pltpu.roll(b_ref[...], 1, 0)
        o512 = _block_ones(RW, HN)
        _wkv_prebroadcast((r_ref, w_ref, k_ref, a_ref, b_ref, dy_ref, w_ref[...] * b_prev,
                           _segsum_impl(a_ref[...] * b_prev, o512), _segsum_impl(r_ref[...] * b_ref[...], o512)),
                          (rb, wb, kb, ab, bb, dyb, wbb, alb, rhb), c)
        lane_t, diag, ones = _wkv_consts()

        def fill(q, carry):
            for half in range(2):
                lhs_ref[q, :, half * LANES:(half + 1) * LANES] = (_wkv_full(dyb, 2 * q + half) * diag).astype(BF16)
            return carry

        lax.fori_loop(0, c // 2, fill, 0)
        _wkv_pair_sums(lhs_ref, dyst)
        cur_ref[...] = jnp.where(pl.program_id(0) == 0, fin_ref[...], snext_ref[0])

        def pair(p, carry):
            ta = c - 1 - 2 * p
            tb = ta - 1
            s_a, s_b, s_c = cur_ref[...], sst_ref[ta], sst_ref[tb]
            dy_pair = dyst[c // 2 - 1 - p]
            dy_a, dy_b = dy_pair[:, LANES:], dy_pair[:, :LANES]
            ds_a = ds_ref[...] + dy_a * _wkv_full(rb, ta)
            both = _wkv_rowseg(jnp.concatenate([ds_a * _wkv_full(bb, ta), ds_a * _wkv_full(wbb, ta)], axis=0), ones)
            dsa_a = both[:ns]
            ds_b = ds_a * _wkv_full(wb, ta) + dsa_a * _wkv_full(ab, ta) + dy_b * _wkv_full(rb, tb)
            dsa_b = both[ns:] + dsa_a * _wkv_full(alb, ta) + dy_b * _wkv_full(rhb, tb)
            ds_ref[...] = ds_b * _wkv_full(wb, tb) + dsa_b * _wkv_full(ab, tb)
            cur_ref[...] = s_c
            lhs_ref[p, :, 0:LANES] = (ds_a * _wkv_full(kb, ta)).astype(BF16)
            lhs_ref[p, :, LANES:2 * LANES] = (ds_b * _wkv_full(kb, tb)).astype(BF16)
            for t, s_t, s_prev, ds, dsa, dyy in ((ta, s_a, s_b, ds_a, dsa_a, dy_a), (tb, s_b, s_c, ds_b, dsa_b, dy_b)):
                p_r[t] = _wkv_colsum8(s_t * dyy)
                p_k[t] = _wkv_colsum8(ds * vst_ref[t])
                p_b[t] = _wkv_colsum8(ds * sast_ref[t])
                p_w[t] = _wkv_colsum8(ds * s_prev)
                p_a[t] = _wkv_colsum8(s_prev * dsa)
            return carry

        lax.fori_loop(0, c // 2, pair, 0)
        _wkv_pair_sums(lhs_ref, dvsum_ref)
        acc_ref[...] = jnp.zeros_like(acc_ref)

        def gather_dv(p, carry):
            both = dvsum_ref[p]
            acc = jnp.where(lane_t == c - 1 - 2 * p, both[:, :LANES], acc_ref[...])
            acc_ref[...] = jnp.where(lane_t == c - 2 - 2 * p, both[:, LANES:], acc)
            return carry

        lax.fori_loop(0, c // 2, gather_dv, 0)
        sel = (lax.broadcasted_iota(jnp.int32, (c, c * SUBLANES), 1) // SUBLANES
               == lax.broadcasted_iota(jnp.int32, (c, c * SUBLANES), 0)).astype(F32)
        for out_ref, part in ((dr_ref, p_r), (dw_ref, p_w), (dk_ref, p_k), (da_ref, p_a), (db_ref, p_b)):
            out_ref[...] = _dot3(sel, part[...].reshape(c * SUBLANES, RW), 1, 0)
        dv_ref[...] = _wkv_transpose_out(acc_ref[...], c, diag)

    rmap = lambda i: nc - 1 - i
    row = pl.BlockSpec((c, RW), lambda i: (rmap(i), 0))
    st = pl.BlockSpec((c, ns, LANES), lambda i: (rmap(i), 0, 0))
    nxt = pl.BlockSpec((1, ns, LANES), lambda i: (jnp.minimum((rmap(i) + 1) * c, length - 1), 0, 0))
    state = pltpu.VMEM((ns, LANES), F32)
    return pl.pallas_call(
        body, grid=(nc,), in_specs=[row] * 6 + [st, st, st, nxt, pl.BlockSpec((ns, LANES), lambda i: (0, 0))],
        out_specs=[row] * 6, out_shape=[SDS((length, RW), F32)] * 6,
        scratch_shapes=[state] * 3 + [pltpu.VMEM((c // 2, ns, 2 * LANES), F32), pltpu.VMEM((c, ns, 2 * LANES), BF16),
                                       pltpu.VMEM((c // 2, ns, 2 * LANES), F32)]
        + [pltpu.VMEM((c, SUBLANES, RW), F32)] * 14,
        compiler_params=_params(), name="wkv_bwd")(r, w, k, a, b, dy, sst, sast, vst, sst, fin)


def _my_place():
    return lax.axis_index("x"), lax.axis_index("y"), lax.axis_index("c")


def _peer(x, y, c, k):
    return (x ^ ((k >> 2) & 1), y ^ ((k >> 1) & 1), c ^ (k & 1))


def _all_gather(shard, name):
    rows = shard.shape[0]

    def body(in_ref, out_ref, send_sems, recv_sems, local_sem):
        x, y, c = _my_place()
        me = 4 * x + 2 * y + c
        mine = pltpu.make_async_copy(in_ref, out_ref.at[me], local_sem)
        mine.start()
        sends = []
        for k in range(1, N_DEV):
            cp = pltpu.make_async_remote_copy(src_ref=in_ref, dst_ref=out_ref.at[me], send_sem=send_sems.at[k - 1],
                                              recv_sem=recv_sems.at[k - 1], device_id=_peer(x, y, c, k), device_id_type=MESH)
            cp.start()
            sends.append(cp)
        for k in range(1, N_DEV):
            px, py, pc = _peer(x, y, c, k)
            pltpu.make_async_remote_copy(src_ref=in_ref, dst_ref=out_ref.at[4 * px + 2 * py + pc], send_sem=send_sems.at[k - 1],
                                         recv_sem=recv_sems.at[k - 1], device_id=(px, py, pc), device_id_type=MESH).wait_recv()
        for cp in sends:
            cp.wait_send()
        mine.wait()

    return pl.pallas_call(
        body, out_shape=SDS((N_DEV, rows, LANES), shard.dtype),
        in_specs=[pl.BlockSpec(memory_space=pl.ANY)], out_specs=pl.BlockSpec(memory_space=pl.ANY),
        scratch_shapes=[pltpu.SemaphoreType.DMA((N_DEV - 1,)), pltpu.SemaphoreType.DMA((N_DEV - 1,)), pltpu.SemaphoreType.DMA],
        name=name)(shard)


_HBM_SPEC = pl.BlockSpec(memory_space=pltpu.HBM)
_SEM_SPEC = pl.BlockSpec(memory_space=pltpu.SEMAPHORE)
_DATAFLOW = pltpu.SideEffectType.DATAFLOW_SIDE_EFFECTING


def _exchange_start(srcs, per_peer, name):
    n = len(srcs)
    lands = [jnp.zeros((N_DEV, s.shape[-2], LANES), s.dtype) for s in srcs]

    def body(*refs):
        src_refs, land_refs = refs[:n], refs[n:2 * n]
        send_sems, recv_sems, token = refs[2 * n], refs[2 * n + 1], refs[-1]
        x, y, c = _my_place()
        me = 4 * x + 2 * y + c
        for a in range(n):
            for k in range(1, N_DEV):
                px, py, pc = _peer(x, y, c, k)
                mine = src_refs[a].at[4 * px + 2 * py + pc] if per_peer else src_refs[a]
                pltpu.make_async_remote_copy(src_ref=mine, dst_ref=land_refs[a].at[me], send_sem=send_sems.at[7 * a + k - 1],
                                             recv_sem=recv_sems.at[7 * a + k - 1], device_id=(px, py, pc),
                                             device_id_type=MESH).start()
        token[...] = jnp.zeros_like(token)

    n_sem = (N_DEV - 1) * n
    res = pl.pallas_call(
        body, name=name,
        out_shape=(pltpu.SemaphoreType.DMA((n_sem,)), pltpu.SemaphoreType.DMA((n_sem,)),
                   *[pltpu.HBM(a.shape, a.dtype) for a in srcs + lands], SDS((SUBLANES, LANES), F32)),
        in_specs=(_HBM_SPEC,) * (2 * n),
        out_specs=(_SEM_SPEC, _SEM_SPEC) + (_HBM_SPEC,) * (2 * n) + (pl.BlockSpec(memory_space=pltpu.VMEM),),
        input_output_aliases={i: 2 + i for i in range(2 * n)}, compiler_params=pltpu.CompilerParams(has_side_effects=_DATAFLOW),
    )(*[pltpu.with_memory_space_constraint(a, pltpu.HBM) for a in srcs + lands])
    return res[0], res[1], list(res[2:2 + n]), list(res[2 + n:2 + 2 * n]), res[-1]


def _exchange_wait(started, after, per_peer, name):
    send_sems, recv_sems, src_thrus, land_thrus, _ = started
    n = len(src_thrus)

    def body(*refs):
        src_refs, land_refs = refs[:n], refs[n:2 * n]
        send_sems, recv_sems = refs[2 * n], refs[2 * n + 1]
        x, y, c = _my_place()
        me = 4 * x + 2 * y + c
        for a in range(n):
            for k in range(1, N_DEV):
                px, py, pc = _peer(x, y, c, k)
                mine = src_refs[a].at[me] if per_peer else src_refs[a]
                copy = pltpu.make_async_remote_copy(src_ref=mine, dst_ref=land_refs[a].at[4 * px + 2 * py + pc],
                                                    send_sem=send_sems.at[7 * a + k - 1], recv_sem=recv_sems.at[7 * a + k - 1],
                                                    device_id=(px, py, pc), device_id_type=MESH)
                copy.wait_send()
                copy.wait_recv()

    res = pl.pallas_call(
        body, name=name, out_shape=tuple(pltpu.HBM(a.shape, a.dtype) for a in src_thrus + land_thrus),
        in_specs=(_HBM_SPEC,) * (2 * n) + (_SEM_SPEC, _SEM_SPEC, pl.BlockSpec(memory_space=pl.ANY)),
        out_specs=(_HBM_SPEC,) * (2 * n), input_output_aliases={i: i for i in range(2 * n)},
        compiler_params=pltpu.CompilerParams(has_side_effects=_DATAFLOW),
    )(*src_thrus, *land_thrus, send_sems, recv_sems, after)
    return list(res[n:])


SUM_ROWS = 256


def _sum_parts(parts, name, own=None):
    rows = parts.shape[1]

    def body(*refs):
        p_ref, g_out = refs[0], refs[-1]
        g = p_ref[0].astype(F32)
        for s in range(1, N_DEV):
            g = g + p_ref[s].astype(F32)
        if own is not None:
            g = g + refs[1][...].astype(F32)
        g_out[...] = g

    tr = min(SUM_ROWS, rows)
    blk = pl.BlockSpec((tr, LANES), lambda i: (i, 0))
    extra = [] if own is None else [own]
    return pl.pallas_call(
        body, grid=(rows // tr,),
        in_specs=[pl.BlockSpec((N_DEV, tr, LANES), lambda i: (0, i, 0))] + [blk] * len(extra),
        out_specs=blk, out_shape=SDS((rows, LANES), F32),
        compiler_params=_params(("parallel",)), name=name)(parts, *extra)


def _adamw(g, wgt, m, v):
    rows = wgt.shape[0]

    def body(g_ref, w_ref, m_ref, v_ref, d_out, m_out, v_out):
        g_ = g_ref[...]
        m_new = ADAM_B1 * m_ref[...] + (1.0 - ADAM_B1) * g_
        v_new = ADAM_B2 * v_ref[...] + (1.0 - ADAM_B2) * (g_ * g_)
        m_hat = m_new / (1.0 - ADAM_B1 ** ADAM_STEP)
        v_hat = v_new / (1.0 - ADAM_B2 ** ADAM_STEP)
        d_out[...] = -ADAM_LR * (m_hat / (jnp.sqrt(v_hat) + ADAM_EPS) + ADAM_WD * w_ref[...])
        m_out[...] = m_new
        v_out[...] = v_new

    blk = pl.BlockSpec((PACK_ROWS, LANES), lambda i: (i, 0))
    return pl.pallas_call(
        body, grid=(rows // PACK_ROWS,), in_specs=[blk] * 4, out_specs=[blk] * 3, out_shape=[SDS((rows, LANES), F32)] * 3,
        compiler_params=_params(("parallel",)), name="adamw")(g, wgt, m, v)


PACK_ALIGN = 2 * SUBLANES * LANES
PACK_ROWS = 512

ROW_SHARDED = ("s5_w_glu", "w_out", "ffn_w_down")
TRANSPOSED = ("w_in", "ffn_w_up", "w_branch_rwkv", "w_branch_s5")
SMALL_SHARDED = ("rwkv_w2", "rwkv_a2", "rwkv_g2", "ffn_conv_w")
SHARDED = ROW_SHARDED + TRANSPOSED + SMALL_SHARDED
F32_GATHER = ("ffn_conv_w",)
REPLICATED =("norm_mix_pre", "norm_mix_post", "norm_ffn_pre", "norm_ffn_post", "b_gate", "rwkv_shift_mu", "rwkv_w0",
              "rwkv_a0", "rwkv_k_k", "rwkv_k_a", "rwkv_r_k", "rwkv_lnx_w", "rwkv_lnx_b", "s5_a_re", "s5_a_im", "s5_b_re",
              "s5_b_im", "s5_c_re", "s5_c_im", "s5_d", "s5_log_step", "s5_b_glu", "ffn_conv_b")
WEIGHTS = ("norm_mix_pre", "norm_mix_post", "norm_ffn_pre", "norm_ffn_post", "w_in", "b_gate", "rwkv_shift_mu", "rwkv_w0",
           "rwkv_w2", "rwkv_a0", "rwkv_a2", "rwkv_g2", "rwkv_k_k", "rwkv_k_a", "rwkv_r_k", "rwkv_lnx_w", "rwkv_lnx_b",
           "s5_a_re", "s5_a_im", "s5_b_re", "s5_b_im", "s5_c_re", "s5_c_im", "s5_d", "s5_log_step", "s5_w_glu", "s5_b_glu",
           "w_branch_rwkv", "w_branch_s5", "w_out", "ffn_w_up", "ffn_conv_w", "ffn_conv_b", "ffn_w_down")


def _pad_flat(a):
    flat = a.reshape(-1)
    pad = (-flat.shape[0]) % PACK_ALIGN
    return jnp.pad(flat, (0, pad)) if pad else flat


def _pad_cols(a, mult):
    pad = (-a.shape[1]) % mult
    return jnp.pad(a, ((0, 0), (0, pad))) if pad else a


def _pack(pieces):
    flat = jnp.concatenate([_pad_flat(p) for p in pieces])
    pad = (-flat.shape[0]) % (PACK_ROWS * LANES)
    return (jnp.pad(flat, (0, pad)) if pad else flat).reshape(-1, LANES)


def _unpack(buf, shapes):
    lead = buf.shape[:-2]
    flat = buf.reshape(*lead, -1)
    out, off = [], 0
    for s in shapes:
        n = math.prod(s)
        out.append(flat[..., off:off + n].reshape(*lead, *s))
        off += n + (-n) % PACK_ALIGN
    return out


def _rows_to_full(stack):
    return stack.reshape(-1, stack.shape[2])


def _cols_to_full(stack):
    return stack.transpose(1, 0, 2).reshape(stack.shape[1], -1)


EARLY_GRADS = ("ffn_w_down", "ffn_w_up", "w_out")
REST_GRADS = ("s5_w_glu", "w_in", "w_branch_rwkv", "w_branch_s5")


def _local_step(x, tgt, wt, get_late, send_early, send_rest):
    wt = dict(wt)
    o512 = _block_ones(RW, HN)
    w_in = wt["w_in"]
    w_rw, w_u, w_g = w_in[:N_RWKV], w_in[N_RWKV:N_RWKV + S5W], w_in[N_RWKV + S5W:]
    b_gate = wt["b_gate"]
    b_r, b_s = b_gate[:, :D_MODEL], b_gate[:, D_MODEL:]
    zpad = lambda a, lo, n: jnp.pad(a.astype(F32), ((lo, n - lo - a.shape[0]), (0, 0)))
    w2p, a2p, g2p = zpad(wt["rwkv_w2"], 0, 256), zpad(wt["rwkv_a2"], 64, 256), zpad(wt["rwkv_g2"], 128, 256)
    pre_small = [wt["rwkv_w0"], wt["rwkv_a0"], wt["rwkv_k_k"], wt["rwkv_k_a"], w2p, a2p, g2p]
    post_prm = [wt["rwkv_lnx_w"], wt["rwkv_lnx_b"], wt["rwkv_r_k"].reshape(1, RW)]
    mu = wt["rwkv_shift_mu"]

    a_re, a_im = wt["s5_a_re"].reshape(1, S5N), wt["s5_a_im"].reshape(1, S5N)
    ls = jnp.repeat(wt["s5_log_step"].reshape(S5G), S5P).reshape(1, S5N)
    b_re_t = wt["s5_b_re"].reshape(S5N, S5C).T
    b_im_t = wt["s5_b_im"].reshape(S5N, S5C).T
    c_re, c_im = wt["s5_c_re"].reshape(S5G, S5C, S5P), wt["s5_c_im"].reshape(S5G, S5C, S5P)
    abr, abi, bbr, bbi = _s5_prep_fwd(a_re, a_im, ls, b_re_t, b_im_t)
    bblk = jnp.concatenate([_s5_blockdiag_in(bbr), _s5_blockdiag_in(bbi)], axis=2)
    cblk = jnp.concatenate([_s5_blockdiag_out(c_re), -_s5_blockdiag_out(c_im)], axis=1)
    s5_d = wt["s5_d"]

    h1 = _rms_fwd(x, wt["norm_mix_pre"], "rms1_fwd")
    p_rw = _mm(h1, w_rw, tb=True, name="mm_proj_rwkv")
    u = _mm(h1, w_u, tb=True, name="mm_proj_s5")
    gpre = _mm(h1, w_g, tb=True, name="mm_proj_gate")
    r, decay, k2, v, aa, bb, g = _rwkv_pre_fwd(p_rw, mu, pre_small, o512)
    y, sst, sast, vst, s_fin = _wkv_fwd(r, decay, k2, v, aa, bb)
    wt.update(get_late(y))
    y_r = _rwkv_post_fwd(y, r, k2, v, g, post_prm, o512)
    o_r = _mm(y_r, wt["w_branch_rwkv"], tb=True, name="mm_branch_rwkv")
    ys, kre, kim = _s5_fwd(u, bblk, cblk, abr, abi, s5_d)
    t_glu = _mm(ys, wt["s5_w_glu"], name="mm_glu")
    out5 = _glu_fwd(ys, t_glu, wt["s5_b_glu"])
    o_s = _mm(out5, wt["w_branch_s5"], tb=True, name="mm_branch_s5")
    mi = _merge_fwd(gpre, o_r, o_s, b_r, b_s)
    mixed = _mm(mi, wt["w_out"], name="mm_out")
    x2, h2 = _mid_fwd(x, mixed, wt["norm_mix_post"], wt["norm_ffn_pre"])
    z = _mm(h2, wt["ffn_w_up"], tb=True, name="mm_up")
    act = _conv_fwd(z, wt["ffn_conv_w"], wt["ffn_conv_b"])
    f = _mm(act, wt["ffn_w_down"], name="mm_down")
    dx3, df, d_norm_ffn_post, loss_part = _final(x2, f, tgt, wt["norm_ffn_post"])

    gr = {"norm_ffn_post": d_norm_ffn_post}
    dact = _mm(df, wt["ffn_w_down"], tb=True, name="mm_down_dx")
    gr["ffn_w_down"] = _mm(act, df, ta=True, out_dtype=BF16, name="mm_down_dw")
    dzg, dzv, gr["ffn_conv_w"], gr["ffn_conv_b"] = _conv_bwd(z, dact, wt["ffn_conv_w"], wt["ffn_conv_b"])
    dz = jnp.concatenate([dzg, dzv], axis=1)
    dh2 = _mm(dz, wt["ffn_w_up"], name="mm_up_dx")
    gr["ffn_w_up"] = _mm(dz, h2, ta=True, out_dtype=BF16, name="mm_up_dw")
    dx2, dmixed, gr["norm_mix_post"], gr["norm_ffn_pre"] = _mid_bwd(x2, mixed, dh2, dx3, wt["norm_mix_post"], wt["norm_ffn_pre"])
    dmi = _mm(dmixed, wt["w_out"], tb=True, name="mm_out_dx")
    gr["w_out"] = _mm(mi, dmixed, ta=True, out_dtype=BF16, name="mm_out_dw")
    token = send_early({n: gr[n] for n in EARLY_GRADS})
    b_r = b_r + jnp.tile(token[0:1, :], (1, D_MODEL // LANES))
    dgp_r, dgp_s, do_r, do_s, db_r, db_s = _merge_bwd(gpre, o_r, o_s, dmi, b_r, b_s)
    gr["b_gate"] = jnp.concatenate([db_r, db_s], axis=1)
    dout5 = _mm(do_s, wt["w_branch_s5"], name="mm_branch_s5_dx")
    gr["w_branch_s5"] = _mm(do_s, out5, ta=True, out_dtype=BF16, name="mm_branch_s5_dw")
    dys_a, dt_glu, gr["s5_b_glu"] = _glu_bwd(ys, t_glu, dout5, wt["s5_b_glu"])
    dys_b = _mm(dt_glu, wt["s5_w_glu"], tb=True, name="mm_glu_dx")
    gr["s5_w_glu"] = _mm(ys, dt_glu, ta=True, out_dtype=BF16, name="mm_glu_dw")
    du, dbblk, dcblk, dabr, dabi, gr["s5_d"] = _s5_bwd(u, dys_a, dys_b, kre, kim, bblk, cblk, abr, abi, s5_d)
    gr["s5_c_re"] = _s5_blockdiag_out_t(dcblk[:, :S5_BN, :]).reshape(wt["s5_c_re"].shape)
    gr["s5_c_im"] = (-_s5_blockdiag_out_t(dcblk[:, S5_BN:, :])).reshape(wt["s5_c_im"].shape)
    dbbr, dbbi = _s5_blockdiag_in_t(dbblk[:, :, :S5_BN]), _s5_blockdiag_in_t(dbblk[:, :, S5_BN:])
    gsel = (lax.broadcasted_iota(jnp.int32, (S5N, LANES), 0) // S5P == lax.broadcasted_iota(jnp.int32, (S5N, LANES), 1)).astype(F32)
    d_are, d_aim, d_ls, d_bre_t, d_bim_t = _s5_prep_bwd(a_re, a_im, ls, b_re_t, b_im_t, (dabr, dabi, dbbr, dbbi), gsel)
    gr["s5_a_re"] = d_are.reshape(wt["s5_a_re"].shape)
    gr["s5_a_im"] = d_aim.reshape(wt["s5_a_im"].shape)
    gr["s5_log_step"] = d_ls[0:1, :S5G]
    gr["s5_b_re"] = d_bre_t.T.reshape(wt["s5_b_re"].shape)
    gr["s5_b_im"] = d_bim_t.T.reshape(wt["s5_b_im"].shape)
    dy_r = _mm(do_r, wt["w_branch_rwkv"], name="mm_branch_rwkv_dx")
    gr["w_branch_rwkv"] = _mm(do_r, y_r, ta=True, out_dtype=BF16, name="mm_branch_rwkv_dw")
    dy, dr1, dk1, dv1, dg, gr["rwkv_lnx_w"], gr["rwkv_lnx_b"], d_rk = _rwkv_post_bwd(y, r, k2, v, g, dy_r, post_prm, o512)
    gr["rwkv_r_k"] = d_rk.reshape(wt["rwkv_r_k"].shape)
    dr2, ddecay, dk2, daa, dbb, dv2 = _wkv_bwd(r, decay, k2, aa, bb, dy, sst, sast, vst, s_fin)
    cots = [dr1 + dr2, ddecay, dk1 + dk2, dv1 + dv2, daa, dbb, dg]
    dp_rw, gr["rwkv_shift_mu"], gr["rwkv_w0"], gr["rwkv_a0"], gr["rwkv_k_k"], gr["rwkv_k_a"], dw2p, da2p, dg2p = \
        _rwkv_pre_bwd(p_rw, cots, mu, pre_small, o512)
    gr["rwkv_w2"], gr["rwkv_a2"], gr["rwkv_g2"] = dw2p[:64], da2p[64:128], dg2p[128:]
    dproj = jnp.concatenate([dp_rw, du, dgp_r, dgp_s], axis=1)
    gr["w_in"] = _mm(dproj, h1, ta=True, out_dtype=BF16, name="mm_proj_dw")
    token = send_rest(gr)
    dh1 = _mm(dproj, w_in + token[0, 0].astype(w_in.dtype), name="mm_proj_dx")
    dx, gr["norm_mix_pre"] = _rms_bwd(x, wt["norm_mix_pre"], dh1, dx2, "rms1_bwd")
    return loss_part[0, 0], dx, gr


def kernel(x, norm_mix_pre, norm_mix_post, norm_ffn_pre, norm_ffn_post, w_in, b_gate, rwkv_shift_mu, rwkv_w0, rwkv_w2, rwkv_a0, rwkv_a2, rwkv_g2, rwkv_k_k, rwkv_k_a, rwkv_r_k, rwkv_lnx_w, rwkv_lnx_b, s5_a_re, s5_a_im, s5_b_re, s5_b_im, s5_c_re, s5_c_im, s5_d, s5_log_step, s5_w_glu, s5_b_glu, w_branch_rwkv, w_branch_s5, w_out, ffn_w_up, ffn_conv_w, ffn_conv_b, ffn_w_down, loss_target, m_norm_mix_pre, m_norm_mix_post, m_norm_ffn_pre, m_norm_ffn_post, m_w_in, m_b_gate, m_rwkv_shift_mu, m_rwkv_w0, m_rwkv_w2, m_rwkv_a0, m_rwkv_a2, m_rwkv_g2, m_rwkv_k_k, m_rwkv_k_a, m_rwkv_r_k, m_rwkv_lnx_w, m_rwkv_lnx_b, m_s5_a_re, m_s5_a_im, m_s5_b_re, m_s5_b_im, m_s5_c_re, m_s5_c_im, m_s5_d, m_s5_log_step, m_s5_w_glu, m_s5_b_glu, m_w_branch_rwkv, m_w_branch_s5, m_w_out, m_ffn_w_up, m_ffn_conv_w, m_ffn_conv_b, m_ffn_w_down, v_norm_mix_pre, v_norm_mix_post, v_norm_ffn_pre, v_norm_ffn_post, v_w_in, v_b_gate, v_rwkv_shift_mu, v_rwkv_w0, v_rwkv_w2, v_rwkv_a0, v_rwkv_a2, v_rwkv_g2, v_rwkv_k_k, v_rwkv_k_a, v_rwkv_r_k, v_rwkv_lnx_w, v_rwkv_lnx_b, v_s5_a_re, v_s5_a_im, v_s5_b_re, v_s5_b_im, v_s5_c_re, v_s5_c_im, v_s5_d, v_s5_log_step, v_s5_w_glu, v_s5_b_glu, v_w_branch_rwkv, v_w_branch_s5, v_w_out, v_ffn_w_up, v_ffn_conv_w, v_ffn_conv_b, v_ffn_w_down):
    args = dict(locals())
    wgt = {n: args[n] for n in WEIGHTS}
    mom = {n: args["m_" + n] for n in WEIGHTS}
    var = {n: args["v_" + n] for n in WEIGHTS}
    me = 4 * lax.axis_index("x") + 2 * lax.axis_index("y") + lax.axis_index("c")
    shard_shapes = {n: wgt[n].shape[1:] for n in SHARDED}
    sent_shapes = {n: (shard_shapes[n][::-1] if n in TRANSPOSED else shard_shapes[n]) for n in SHARDED}
    sent = lambda n, a: a.T if n in TRANSPOSED else a

    unshard = lambda n, blocks: _cols_to_full(blocks) if n in SMALL_SHARDED else _rows_to_full(blocks)

    first = ("w_in", "rwkv_w2", "rwkv_a2", "rwkv_g2")
    late = tuple(n for n in SHARDED if n not in first and n not in F32_GATHER)
    late_payloads = [_pack([sent(n, wgt[n][0].astype(BF16)) for n in late]), _pack([wgt[n][0] for n in F32_GATHER])]
    late_started = _exchange_start(late_payloads, False, "gather_late_start")
    got_first = _unpack(_all_gather(_pack([sent(n, wgt[n][0].astype(BF16)) for n in first]), "all_gather_weights"),
                        [sent_shapes[n] for n in first])
    wt = {n: wgt[n] for n in REPLICATED}
    for n, blocks in zip(first, got_first, strict=True):
        wt[n] = unshard(n, blocks)

    def get_late(after):
        lands = _exchange_wait(late_started, after, False, "gather_late_wait")
        out = {}
        for names, land, mine in zip((late, F32_GATHER), lands, late_payloads, strict=True):
            blocks = _unpack(lax.dynamic_update_index_in_dim(land, mine, me, 0), [sent_shapes[n] for n in names])
            out.update({n: unshard(n, b_) for n, b_ in zip(names, blocks, strict=True)})
        return out

    whole = tuple(n for n in SMALL_SHARDED + REPLICATED if n != "norm_mix_pre")
    blocks_of = lambda g_: g_.reshape(N_DEV, -1, LANES)
    started = {}

    def send_early(grads_):
        started["early"] = _exchange_start([blocks_of(grads_[n]) for n in EARLY_GRADS], True, "exchange_early_start")
        return started["early"][4]

    def send_rest(grads_):
        started["rest"] = _exchange_start([blocks_of(grads_[n]) for n in REST_GRADS], True, "exchange_rest_start")
        started["small_mine"] = _pack([grads_[n] for n in whole])
        started["small"] = _exchange_start([started["small_mine"]], False, "gather_small_start")
        return started["rest"][4] + started["small"][4]

    loss_part, dx, gr = _local_step(x[0], loss_target[0], wt, get_late, send_early, send_rest)

    got = {}
    for key, names in (("early", EARLY_GRADS), ("rest", REST_GRADS)):
        lands = _exchange_wait(started[key], dx, True, "exchange_" + key + "_wait")
        for n, land in zip(names, lands, strict=True):
            own = lax.dynamic_index_in_dim(blocks_of(gr[n]), me, 0, keepdims=False)
            got[n] = _sum_parts(land, "sum_" + n, own=own).reshape(sent_shapes[n])
    small_land, = _exchange_wait(started["small"], dx, False, "gather_small_wait")
    small_sum = _sum_parts(lax.dynamic_update_index_in_dim(small_land, started["small_mine"], me, 0), "sum_small_grads")
    got.update(zip(whole, _unpack(small_sum, [gr[n].shape for n in whole]), strict=True))
    last = gr["norm_mix_pre"].reshape(SUBLANES, LANES)
    got["norm_mix_pre"] = _sum_parts(_all_gather(last, "gather_last_grad"), "sum_last_grad").reshape(gr["norm_mix_pre"].shape)
    grads = {}
    for n in WEIGHTS:
        if n in TRANSPOSED:
            grads[n] = got[n].T
        elif n in SMALL_SHARDED:
            cols = shard_shapes[n][1]
            grads[n] = lax.dynamic_slice_in_dim(got[n], me * cols, cols, axis=1)
        else:
            grads[n] = got[n]
        grads[n] = grads[n].reshape(wgt[n].shape)

    pack_local = lambda src: _pack([src[n].reshape(-1) for n in WEIGHTS])
    outs = _adamw(pack_local(grads), pack_local(wgt), pack_local(mom), pack_local(var))
    shapes = [wgt[n].shape for n in WEIGHTS]
    loss = lax.psum(loss_part, AXES)
    return (loss, dx[None], *[grads[n] for n in WEIGHTS], *_unpack(outs[0], shapes), *_unpack(outs[1], shapes),
            *_unpack(outs[2], shapes))
```

```python
import functools
import math

import jax
import jax.numpy as jnp
from jax import lax
from jax.experimental import pallas as pl
from jax.experimental.pallas import tpu as pltpu

F32 = jnp.float32
BF16 = jnp.bfloat16
SDS = jax.ShapeDtypeStruct
HI = lax.Precision.HIGHEST
MESH = pl.DeviceIdType.MESH
AXES = ("x", "y", "c")
N_DEV = 8

D_MODEL = 1024
RW = 512
HN = 64
N_RWKV = 1792
S5W = 512
S5G = 32
S5C = 16
S5P = 64
S5N = S5G * S5P
D_FF = 2816
NORM_EPS = 1e-6
LNX_EPS = 64e-5

ADAM_LR = 0.001
ADAM_B1 = 0.9
ADAM_B2 = 0.999
ADAM_EPS = 1e-08
ADAM_WD = 0.01
ADAM_STEP = 10

LANES = 128
SUBLANES = 8
VMEM_LIMIT = 56 * 1024 * 1024
WKV_CHUNK = 32


def _params(sem=("arbitrary",)):
    return pltpu.CompilerParams(dimension_semantics=sem, vmem_limit_bytes=VMEM_LIMIT)


def _pick(n, cap):
    best = None
    for t in range(LANES, min(n, cap) + 1, LANES):
        if n % t == 0:
            best = t
    return best or n


def _mm(a, b, *, ta=False, tb=False, out_dtype=F32, name):
    m = a.shape[1] if ta else a.shape[0]
    k = a.shape[0] if ta else a.shape[1]
    n = b.shape[0] if tb else b.shape[1]
    assert (b.shape[1] if tb else b.shape[0]) == k
    tm, tn, tk = _pick(m, 1024), _pick(n, 2304), _pick(k, 512)
    nk = k // tk
    dims = (((0 if ta else 1,), (1 if tb else 0,)), ((), ()))

    def body(a_ref, b_ref, o_ref, acc_ref):
        kk = pl.program_id(2)

        @pl.when(kk == 0)
        def _():
            acc_ref[...] = jnp.zeros_like(acc_ref)

        acc_ref[...] += lax.dot_general(a_ref[...].astype(BF16), b_ref[...].astype(BF16), dims,
                                        preferred_element_type=F32)

        @pl.when(kk == nk - 1)
        def _():
            o_ref[...] = acc_ref[...].astype(o_ref.dtype)

    a_spec = pl.BlockSpec((tk, tm), lambda i, j, kk: (kk, i)) if ta else pl.BlockSpec((tm, tk), lambda i, j, kk: (i, kk))
    b_spec = pl.BlockSpec((tn, tk), lambda i, j, kk: (j, kk)) if tb else pl.BlockSpec((tk, tn), lambda i, j, kk: (kk, j))
    return pl.pallas_call(
        body, grid=(m // tm, n // tn, nk), in_specs=[a_spec, b_spec],
        out_specs=pl.BlockSpec((tm, tn), lambda i, j, kk: (i, j)),
        out_shape=SDS((m, n), out_dtype), scratch_shapes=[pltpu.VMEM((tm, tn), F32)],
        compiler_params=_params(("parallel", "parallel", "arbitrary")), name=name)(a, b)


def _rows(fn, rows, params, out_rows, out_accs, *, name, tl, reverse=False, scratch=()):
    first = rows[0][0] if isinstance(rows[0], tuple) else rows[0]
    length = first.shape[0]
    tl = min(tl, length)
    nt = length // tl
    rmap = (lambda i: nt - 1 - i) if reverse else (lambda i: i)
    specs, arrs = [], []
    for r in rows:
        arr, wdt, cb = r if isinstance(r, tuple) else (r, r.shape[1], 0)
        specs.append(pl.BlockSpec((tl, wdt), lambda i, cb=cb: (rmap(i), cb)))
        arrs.append(arr)
    for p in params:
        specs.append(pl.BlockSpec(p.shape, lambda i, nd=p.ndim: (0,) * nd))
        arrs.append(p)
    out_shape = [SDS((length, c), dt) for c, dt in out_rows] + [SDS(s, F32) for s in out_accs]
    out_specs = [pl.BlockSpec((tl, c), lambda i: (rmap(i), 0)) for c, _ in out_rows]
    out_specs += [pl.BlockSpec(s, lambda i, nd=len(s): (0,) * nd) for s in out_accs]
    nr, npar, nor, noa = len(rows), len(params), len(out_rows), len(out_accs)

    def body(*refs):
        rin, pin = refs[:nr], refs[nr:nr + npar]
        rout = refs[nr + npar:nr + npar + nor]
        aout = refs[nr + npar + nor:nr + npar + nor + noa]
        scr = refs[nr + npar + nor + noa:]
        step = pl.program_id(0)
        outs_r, outs_a = fn(step, [r[...] for r in rin], [p[...] for p in pin], scr)
        for ref, val in zip(rout, outs_r, strict=True):
            ref[...] = val.astype(ref.dtype)

        @pl.when(step == 0)
        def _():
            for ref in aout:
                ref[...] = jnp.zeros_like(ref)

        for ref, val in zip(aout, outs_a, strict=True):
            ref[...] += val.astype(F32)

    res = pl.pallas_call(body, grid=(nt,), in_specs=specs, out_specs=out_specs, out_shape=out_shape,
                         scratch_shapes=list(scratch), compiler_params=_params(), name=name)(*arrs)
    return list(res)


def _rms(x, g):
    return x * lax.rsqrt(jnp.mean(x * x, axis=-1, keepdims=True) + NORM_EPS) * g


def _sig(x):
    return 0.5 * (jnp.tanh(0.5 * x) + 1.0)


def _softplus(x):
    return jnp.maximum(x, 0.0) + jnp.log(1.0 + jnp.exp(-jnp.abs(x)))


def _gelu(x):
    return x * (0.5 * (1.0 + jnp.tanh(math.sqrt(2.0 / math.pi) * (x + 0.044715 * (x * x * x)))))


def _bdot(a, b):
    return jnp.dot(a.astype(BF16), b.astype(BF16), preferred_element_type=F32)


def _hdot(a, b):
    return jnp.dot(a, b, precision=HI, preferred_element_type=F32)


def _segsum_impl(x, ones):
    hi = x.astype(BF16)
    lo = (x - hi.astype(F32)).astype(BF16)
    ones = ones.astype(BF16)
    return jnp.dot(jnp.concatenate([hi, lo], axis=1), jnp.concatenate([ones, ones], axis=0), preferred_element_type=F32)


@jax.custom_vjp
def _segsum(x, ones):
    return _segsum_impl(x, ones)


_segsum.defvjp(lambda x, ones: (_segsum_impl(x, ones), ones),
               lambda ones, g: (_segsum_impl(g, ones), jnp.zeros_like(ones)))


def _block_ones(n, blk):
    i = lax.broadcasted_iota(jnp.int32, (n, n), 0) // blk
    j = lax.broadcasted_iota(jnp.int32, (n, n), 1) // blk
    return (i == j).astype(F32)


def _rms_fwd(x, g, name):
    return _rows(lambda s, r, p, _: ([_rms(r[0], p[0])], []), [x], [g], [(x.shape[1], BF16)], [], name=name, tl=512)[0]


def _rms_bwd(x, g, dh, dres, name):
    def fn(s, r, p, _):
        _, vjp = jax.vjp(_rms, r[0], p[0])
        dx, dg = vjp(r[1])
        return [dx + r[2]], [dg]
    return _rows(fn, [x, dh, dres], [g], [(x.shape[1], F32)], [g.shape], name=name, tl=256)


def _rwkv_pre_math(k_, lr, w0, a0, k_k, k_a, w2p, a2p, g2p, o512):
    pre_w = w0 + _bdot(jnp.tanh(lr), w2p)
    w = -_softplus(-pre_w) - 0.5
    decay = jnp.exp(-jnp.exp(w))
    a = _sig(a0 + _bdot(lr, a2p))
    g = _bdot(_sig(lr), g2p)
    kr = k_ * k_k
    kk = kr / jnp.maximum(jnp.sqrt(_segsum(kr * kr, o512)), 1e-12)
    k2 = k_ * (1.0 + (a - 1.0) * k_a)
    return decay, k2, -kk, kk * a, g


def _shift_down(p, prev_row):
    row = lax.broadcasted_iota(jnp.int32, p.shape, 0)
    return jnp.where(row == 0, jnp.broadcast_to(prev_row, p.shape), pltpu.roll(p, 1, 0))


def _shift_up(q, next_row):
    n = q.shape[0]
    row = lax.broadcasted_iota(jnp.int32, q.shape, 0)
    return jnp.where(row == n - 1, jnp.broadcast_to(next_row, q.shape), pltpu.roll(q, n - 1, 0))


def _rwkv_pre_fwd(p, mu, small, o512):
    def fn(step, r, prm, scr):
        car = scr[0]

        @pl.when(step == 0)
        def _():
            car[...] = jnp.zeros_like(car)

        x = r[0]
        prev = _shift_down(x, car[SUBLANES - 1:SUBLANES, :])
        car[...] = x[x.shape[0] - SUBLANES:, :]
        xs = x + (prev - x) * prm[0]
        decay, k2, aa, bb, g = _rwkv_pre_math(xs[:, RW:2 * RW], xs[:, 3 * RW:], *prm[1:])
        return [xs[:, :RW], decay, k2, xs[:, 2 * RW:3 * RW], aa, bb, g], []
    return _rows(fn, [p], [mu, *small, o512], [(RW, F32)] * 7, [], name="rwkv_pre_fwd", tl=256,
                 scratch=[pltpu.VMEM((SUBLANES, N_RWKV), F32)])


def _rwkv_pre_bwd(p, cots, mu, small, o512):
    length = p.shape[0]
    tl = min(256, length)
    nt = length // tl
    rows_per = tl // SUBLANES
    params = [mu, *small, o512]
    acc_shapes = [mu.shape] + [q.shape for q in small]
    nr, npar, nacc = 2 + len(cots), len(params), len(acc_shapes)

    def body(*refs):
        rin, pin = refs[:nr], refs[nr:nr + npar]
        dp_ref = refs[nr + npar]
        aout = refs[nr + npar + 1:nr + npar + 1 + nacc]
        car_q = refs[nr + npar + 1 + nacc]
        step = pl.program_id(0)

        @pl.when(step == 0)
        def _():
            car_q[...] = jnp.zeros_like(car_q)
            for ref in aout:
                ref[...] = jnp.zeros_like(ref)

        x = rin[0][...]
        prev_row = jnp.where(step == nt - 1, 0.0, rin[1][SUBLANES - 1:SUBLANES, :])
        dr, ddecay, dk2, dv, daa, dbb, dg = [r[...] for r in rin[2:]]
        prm = [q[...] for q in pin]
        mu_, o512_ = prm[0], prm[-1]
        prev = _shift_down(x, prev_row)
        xs = x + (prev - x) * mu_
        _, vjp = jax.vjp(lambda k_, lr, *w: _rwkv_pre_math(k_, lr, *w, o512_), xs[:, RW:2 * RW], xs[:, 3 * RW:], *prm[1:-1])
        dk_, dlr, *dsmall = vjp((ddecay, dk2, daa, dbb, dg))
        dxs = jnp.concatenate([dr, dk_, dv, dlr], axis=1)
        q = dxs * mu_
        dp_ref[...] = (dxs - q + _shift_up(q, car_q[0:1, :])).astype(dp_ref.dtype)
        car_q[...] = q[:SUBLANES, :]
        aout[0][...] += jnp.sum((prev - x) * dxs, axis=0, keepdims=True)
        for ref, val in zip(aout[1:], dsmall, strict=True):
            ref[...] += val

    rmap = lambda i: nt - 1 - i
    specs = [pl.BlockSpec((tl, N_RWKV), lambda i: (rmap(i), 0)),
             pl.BlockSpec((SUBLANES, N_RWKV), lambda i: (jnp.maximum(rmap(i) * rows_per - 1, 0), 0))]
    specs += [pl.BlockSpec((tl, RW), lambda i: (rmap(i), 0)) for _ in cots]
    specs += [pl.BlockSpec(q.shape, lambda i, nd=q.ndim: (0,) * nd) for q in params]
    out_shape = [SDS((length, N_RWKV), BF16)] + [SDS(sh, F32) for sh in acc_shapes]
    out_specs = [pl.BlockSpec((tl, N_RWKV), lambda i: (rmap(i), 0))]
    out_specs += [pl.BlockSpec(sh, lambda i, nd=len(sh): (0,) * nd) for sh in acc_shapes]
    res = pl.pallas_call(body, grid=(nt,), in_specs=specs, out_specs=out_specs, out_shape=out_shape,
                         scratch_shapes=[pltpu.VMEM((SUBLANES, N_RWKV), F32)],
                         compiler_params=_params(), name="rwkv_pre_bwd")(p, p, *cots, *params)
    return list(res)


def _rwkv_post_math(y, r, k2, v, g, lnx_w, lnx_b, r_k, o512):
    mean = _segsum(y, o512) * (1.0 / HN)
    yc = y - mean
    var = _segsum(yc * yc, o512) * (1.0 / HN)
    yn = yc * lax.rsqrt(var + LNX_EPS) * lnx_w + lnx_b
    bonus = _segsum(r * k2 * r_k, o512) * v
    return (yn + bonus) * g


def _rwkv_post_fwd(y, r, k2, v, g, prm, o512):
    return _rows(lambda s, rr, p, _: ([_rwkv_post_math(*rr, *p)], []), [y, r, k2, v, g], [*prm, o512],
                 [(RW, BF16)], [], name="rwkv_post_fwd", tl=256)[0]


def _rwkv_post_bwd(y, r, k2, v, g, dout, prm, o512):
    def fn(s, rr, p, _):
        o = p[-1]
        _, vjp = jax.vjp(lambda *a: _rwkv_post_math(*a, o), *rr[:5], *p[:-1])
        gr = vjp(rr[5])
        return list(gr[:5]), list(gr[5:])
    return _rows(fn, [y, r, k2, v, g, dout], [*prm, o512], [(RW, F32)] * 5, [q.shape for q in prm],
                 name="rwkv_post_bwd", tl=256)


def _glu_math(ys, t, b):
    return ys * _sig(t + b)


def _glu_fwd(ys, t, b):
    return _rows(lambda s, r, p, _: ([_glu_math(r[0], r[1], p[0])], []), [ys, t], [b], [(S5W, BF16)], [],
                 name="s5_glu_fwd", tl=512)[0]


def _glu_bwd(ys, t, dout, b):
    def fn(s, r, p, _):
        _, vjp = jax.vjp(_glu_math, r[0], r[1], p[0])
        dys, dt, db = vjp(r[2])
        return [dys, dt], [db]
    return _rows(fn, [ys, t, dout], [b], [(S5W, F32), (S5W, BF16)], [b.shape], name="s5_glu_bwd", tl=512)


def _merge_math(gp_r, gp_s, o_r, o_s, b_r, b_s):
    return _sig(gp_r + b_r) * o_r + _sig(gp_s + b_s) * o_s


def _merge_fwd(gpre, o_r, o_s, b_r, b_s):
    return _rows(lambda s, r, p, _: ([_merge_math(*r, *p)], []),
                 [(gpre, D_MODEL, 0), (gpre, D_MODEL, 1), o_r, o_s], [b_r, b_s], [(D_MODEL, BF16)], [],
                 name="merge_fwd", tl=256)[0]


def _merge_bwd(gpre, o_r, o_s, dmi, b_r, b_s):
    def fn(s, r, p, _):
        _, vjp = jax.vjp(_merge_math, *r[:4], *p)
        dgr, dgs, dor, dos, dbr, dbs = vjp(r[4])
        return [dgr, dgs, dor, dos], [dbr, dbs]
    return _rows(fn, [(gpre, D_MODEL, 0), (gpre, D_MODEL, 1), o_r, o_s, dmi], [b_r, b_s], [(D_MODEL, BF16)] * 4,
                 [b_r.shape, b_s.shape], name="merge_bwd", tl=256)


def _mid_fwd(x, mixed, g_post, g_pre):
    def fn(s, r, p, _):
        x2 = r[0] + _rms(r[1], p[0])
        return [x2, _rms(x2, p[1])], []
    return _rows(fn, [x, mixed], [g_post, g_pre], [(D_MODEL, F32), (D_MODEL, BF16)], [], name="mid_fwd", tl=256)


def _mid_bwd(x2, mixed, dh2, dx3, g_post, g_pre):
    def fn(s, r, p, _):
        _, vjp1 = jax.vjp(_rms, r[0], p[1])
        dx2, dg_pre = vjp1(r[2])
        dx2 = dx2 + r[3]
        _, vjp2 = jax.vjp(_rms, r[1], p[0])
        dmixed, dg_post = vjp2(dx2)
        return [dx2, dmixed], [dg_post, dg_pre]
    return _rows(fn, [x2, mixed, dh2, dx3], [g_post, g_pre], [(D_MODEL, F32), (D_MODEL, BF16)], [g_post.shape, g_pre.shape],
                 name="mid_bwd", tl=256)


def _final(x2, f, tgt, g_post):
    def fn(s, r, p, _):
        y, vjp = jax.vjp(_rms, r[1], p[0])
        diff = r[0] + y - r[2]
        dx3 = diff * (1.0 / D_MODEL)
        df, dg = vjp(dx3)
        part = 0.5 * jnp.sum(jnp.sum(diff * diff, axis=1, keepdims=True), axis=0, keepdims=True) * (1.0 / D_MODEL)
        return [dx3, df], [dg, jnp.broadcast_to(part, (1, LANES))]
    return _rows(fn, [x2, f, tgt], [g_post], [(D_MODEL, F32), (D_MODEL, BF16)], [g_post.shape, (1, LANES)], name="final", tl=256)


def _conv_taps(z, car):
    row = lax.broadcasted_iota(jnp.int32, z.shape, 0)
    z1 = jnp.where(row == 0, jnp.broadcast_to(car[7:8, :], z.shape), pltpu.roll(z, 1, 0))
    z2 = pltpu.roll(z, 2, 0)
    z2 = jnp.where(row == 0, jnp.broadcast_to(car[6:7, :], z.shape), z2)
    z2 = jnp.where(row == 1, jnp.broadcast_to(car[7:8, :], z.shape), z2)
    return z1, z2


def _conv(z, car, w, b):
    z1, z2 = _conv_taps(z, car)
    return b + w[0:1, :] * z2 + w[1:2, :] * z1 + w[2:3, :] * z, z1, z2


def _conv_fwd(z, conv_w, conv_b):
    length = z.shape[0]
    tl = min(256, length)
    nt = length // tl
    tc = _pick(D_FF, 1536)
    nb = D_FF // tc

    def body(zg_ref, zv_ref, wg_ref, wv_ref, bg_ref, bv_ref, o_ref, cg, cv):
        @pl.when(pl.program_id(1) == 0)
        def _():
            cg[...] = jnp.zeros_like(cg)
            cv[...] = jnp.zeros_like(cv)

        zg, zv = zg_ref[...], zv_ref[...]
        gate, _, _ = _conv(zg, cg[...], wg_ref[...], bg_ref[...])
        val, _, _ = _conv(zv, cv[...], wv_ref[...], bv_ref[...])
        cg[...] = zg[tl - SUBLANES:, :]
        cv[...] = zv[tl - SUBLANES:, :]
        o_ref[...] = (_gelu(gate) * val).astype(o_ref.dtype)

    zspec = lambda off: pl.BlockSpec((tl, tc), lambda j, i: (i, j + off))
    wspec = lambda off, r: pl.BlockSpec((r, tc), lambda j, i: (0, j + off))
    return pl.pallas_call(
        body, grid=(nb, nt),
        in_specs=[zspec(0), zspec(nb), wspec(0, 3), wspec(nb, 3), wspec(0, 1), wspec(nb, 1)],
        out_specs=pl.BlockSpec((tl, tc), lambda j, i: (i, j)), out_shape=SDS((length, D_FF), BF16),
        scratch_shapes=[pltpu.VMEM((SUBLANES, tc), F32)] * 2,
        compiler_params=_params(("arbitrary", "arbitrary")), name="conv_fwd")(z, z, conv_w, conv_w, conv_b, conv_b)


def _conv_bwd(z, dact, conv_w, conv_b):
    length = z.shape[0]
    tl = min(256, length)
    nt = length // tl
    tc = _pick(D_FF, 1536)
    nb = D_FF // tc
    rows_per = tl // SUBLANES

    def half_bwd(dzc, z, z1, z2, w, dcar):
        n = tl
        row = lax.broadcasted_iota(jnp.int32, dzc.shape, 0)
        u1 = jnp.where(row == n - 1, jnp.broadcast_to(dcar[0:1, :], dzc.shape), pltpu.roll(dzc, n - 1, 0))
        u2 = pltpu.roll(dzc, n - 2, 0)
        u2 = jnp.where(row == n - 2, jnp.broadcast_to(dcar[0:1, :], dzc.shape), u2)
        u2 = jnp.where(row == n - 1, jnp.broadcast_to(dcar[1:2, :], dzc.shape), u2)
        dz = w[2:3, :] * dzc + w[1:2, :] * u1 + w[0:1, :] * u2
        dw = jnp.concatenate([jnp.sum(dzc * z2, axis=0, keepdims=True), jnp.sum(dzc * z1, axis=0, keepdims=True),
                              jnp.sum(dzc * z, axis=0, keepdims=True)], axis=0)
        return dz, dw, jnp.sum(dzc, axis=0, keepdims=True)

    def body(zg_ref, zv_ref, pg_ref, pv_ref, da_ref, wg_ref, wv_ref, bg_ref, bv_ref,
             dzg_ref, dzv_ref, dwg_ref, dwv_ref, dbg_ref, dbv_ref, cg, cv):
        step = pl.program_id(1)

        @pl.when(step == 0)
        def _():
            cg[...] = jnp.zeros_like(cg)
            cv[...] = jnp.zeros_like(cv)
            for ref in (dwg_ref, dwv_ref, dbg_ref, dbv_ref):
                ref[...] = jnp.zeros_like(ref)

        is_first_tile = step == nt - 1
        zg, zv = zg_ref[...], zv_ref[...]
        pg = jnp.where(is_first_tile, 0.0, pg_ref[...])
        pv = jnp.where(is_first_tile, 0.0, pv_ref[...])
        wg, wv = wg_ref[...], wv_ref[...]
        gate, zg1, zg2 = _conv(zg, pg, wg, bg_ref[...])
        val, zv1, zv2 = _conv(zv, pv, wv, bv_ref[...])
        act_g, vjp = jax.vjp(_gelu, gate)
        da = da_ref[...]
        dgate = vjp(da * val)[0]
        dval = da * act_g
        dzg, dwg, dbg = half_bwd(dgate, zg, zg1, zg2, wg, cg[...])
        dzv, dwv, dbv = half_bwd(dval, zv, zv1, zv2, wv, cv[...])
        cg[...] = dgate[:SUBLANES, :]
        cv[...] = dval[:SUBLANES, :]
        dzg_ref[...] = dzg.astype(dzg_ref.dtype)
        dzv_ref[...] = dzv.astype(dzv_ref.dtype)
        dwg_ref[...] += dwg
        dwv_ref[...] += dwv
        dbg_ref[...] += dbg
        dbv_ref[...] += dbv

    rmap = lambda i: nt - 1 - i
    zspec = lambda off: pl.BlockSpec((tl, tc), lambda j, i: (rmap(i), j + off))
    pspec = lambda off: pl.BlockSpec((SUBLANES, tc), lambda j, i: (jnp.maximum(rmap(i) * rows_per - 1, 0), j + off))
    wspec = lambda off, r: pl.BlockSpec((r, tc), lambda j, i: (0, j + off))
    out_w = lambda r: pl.BlockSpec((r, tc), lambda j, i: (0, j))
    dzg, dzv, dwg, dwv, dbg, dbv = pl.pallas_call(
        body, grid=(nb, nt),
        in_specs=[zspec(0), zspec(nb), pspec(0), pspec(nb), pl.BlockSpec((tl, tc), lambda j, i: (rmap(i), j)),
                  wspec(0, 3), wspec(nb, 3), wspec(0, 1), wspec(nb, 1)],
        out_specs=[pl.BlockSpec((tl, tc), lambda j, i: (rmap(i), j))] * 2 + [out_w(3), out_w(3), out_w(1), out_w(1)],
        out_shape=[SDS((length, D_FF), BF16)] * 2 + [SDS((3, D_FF), F32)] * 2 + [SDS((1, D_FF), F32)] * 2,
        scratch_shapes=[pltpu.VMEM((SUBLANES, tc), F32)] * 2,
        compiler_params=_params(("arbitrary", "arbitrary")), name="conv_bwd")(z, z, z, z, dact, conv_w, conv_w, conv_b, conv_b)
    return dzg, dzv, jnp.concatenate([dwg, dwv], axis=1), jnp.concatenate([dbg, dbv], axis=1)


def _s5_prep_math(a_re, a_im, ls, b_re, b_im):
    dt = jnp.exp(ls)
    er = jnp.exp(a_re * dt)
    ph = a_im * dt
    abr, abi = er * jnp.cos(ph), er * jnp.sin(ph)
    den = a_re * a_re + a_im * a_im
    nr = abr - 1.0
    cr = (nr * a_re + abi * a_im) / den
    ci = (abi * a_re - nr * a_im) / den
    return abr, abi, cr * b_re - ci * b_im, cr * b_im + ci * b_re


def _s5_prep_fwd(a_re, a_im, ls, b_re, b_im):
    def body(ar, ai, l, br, bi, o1, o2, o3, o4):
        for ref, val in zip((o1, o2, o3, o4), _s5_prep_math(ar[...], ai[...], l[...], br[...], bi[...]), strict=True):
            ref[...] = val
    return pl.pallas_call(body, out_shape=[SDS((1, S5N), F32)] * 2 + [SDS((S5C, S5N), F32)] * 2,
                          name="s5_prep_fwd")(a_re, a_im, ls, b_re, b_im)


def _s5_prep_bwd(a_re, a_im, ls, b_re, b_im, cots, gsel):
    def body(ar, ai, l, br, bi, c1, c2, c3, c4, g_ref, o1, o2, o3, o4, o5):
        _, vjp = jax.vjp(_s5_prep_math, ar[...], ai[...], l[...], br[...], bi[...])
        d_ar, d_ai, d_ls, d_br, d_bi = vjp((c1[...], c2[...], c3[...], c4[...]))
        o1[...] = d_ar
        o2[...] = d_ai
        o3[...] = _hdot(jnp.broadcast_to(d_ls, (SUBLANES, S5N)), g_ref[...])
        o4[...] = d_br
        o5[...] = d_bi
    return pl.pallas_call(body, out_shape=[SDS((1, S5N), F32)] * 2 + [SDS((SUBLANES, LANES), F32)] + [SDS((S5C, S5N), F32)] * 2,
                          name="s5_prep_bwd")(a_re, a_im, ls, b_re, b_im, *cots, gsel)


def _cmul(ar, ai, br, bi):
    return ar * br - ai * bi, ar * bi + ai * br


def _s5_powers(abr, abi):
    shp = (SUBLANES, S5N)
    a1 = (jnp.broadcast_to(abr, shp), jnp.broadcast_to(abi, shp))
    a2 = _cmul(*a1, *a1)
    a4 = _cmul(*a2, *a2)
    row = lax.broadcasted_iota(jnp.int32, shp, 0)
    pr, pi = a1
    cur = a1
    for i in range(1, SUBLANES):
        cur = _cmul(*cur, *a1)
        pr = jnp.where(row == i, cur[0], pr)
        pi = jnp.where(row == i, cur[1], pi)
    return a1, a2, a4, (pr, pi)


def _s5_scan(sre_ref, sim_ref, car_re, car_im, abr, abi, n_rows, reverse):
    a1, a2, a4, (pr, pi) = _s5_powers(abr, abi)
    sgn = -1.0 if reverse else 1.0
    row = lax.broadcasted_iota(jnp.int32, (SUBLANES, S5N), 0)
    if reverse:
        qr, qi = pr, pi
        for i in range(SUBLANES):
            src = SUBLANES - 1 - i
            qr = jnp.where(row == i, jnp.broadcast_to(pr[src:src + 1, :], pr.shape), qr)
            qi = jnp.where(row == i, jnp.broadcast_to(pi[src:src + 1, :], pi.shape), qi)
        pr, pi = qr, qi
    nblk = n_rows // SUBLANES

    def blk(i, carry):
        cr, ci = carry
        b = (nblk - 1 - i) if reverse else i
        sl = pl.ds(pl.multiple_of(b * SUBLANES, SUBLANES), SUBLANES)
        xr, xi = sre_ref[sl, :], sim_ref[sl, :]
        for kk, (er, ei) in ((1, a1), (2, a2), (4, a4)):
            if reverse:
                sr = jnp.where(row < SUBLANES - kk, pltpu.roll(xr, SUBLANES - kk, 0), 0.0)
                si = jnp.where(row < SUBLANES - kk, pltpu.roll(xi, SUBLANES - kk, 0), 0.0)
            else:
                sr = jnp.where(row >= kk, pltpu.roll(xr, kk, 0), 0.0)
                si = jnp.where(row >= kk, pltpu.roll(xi, kk, 0), 0.0)
            dr, di = _cmul(er, sgn * ei, sr, si)
            xr, xi = xr + dr, xi + di
        dr, di = _cmul(pr, sgn * pi, cr, ci)
        xr, xi = xr + dr, xi + di
        sre_ref[sl, :] = xr
        sim_ref[sl, :] = xi
        edge = 0 if reverse else SUBLANES - 1
        return (jnp.broadcast_to(xr[edge:edge + 1, :], xr.shape), jnp.broadcast_to(xi[edge:edge + 1, :], xi.shape))

    cr, ci = lax.fori_loop(0, nblk, blk, (car_re[...], car_im[...]))
    car_re[...] = cr
    car_im[...] = ci


S5_BLK = 4
S5_BN = S5N // S5_BLK


def _split2(x):
    hi = x.astype(BF16)
    return hi, (x - hi.astype(F32)).astype(BF16)


def _dot3(a, b, ca, cb):
    ah, al = _split2(a)
    bh, bl = _split2(b)
    return lax.dot_general(jnp.concatenate([ah, ah, al], axis=ca), jnp.concatenate([bh, bl, bh], axis=cb),
                           (((ca,), (cb,)), ((), ())), preferred_element_type=F32)


def _dot1(a, b, ca, cb):
    return lax.dot_general(a.astype(BF16), b.astype(BF16), (((ca,), (cb,)), ((), ())), preferred_element_type=F32)


def _s5_project_in(u, bblk_ref, sre_ref, sim_ref):
    for j in range(S5_BLK):
        bu = _dot1(u[:, j * LANES:(j + 1) * LANES], bblk_ref[j], 1, 0)
        sre_ref[:, j * S5_BN:(j + 1) * S5_BN] = bu[:, :S5_BN]
        sim_ref[:, j * S5_BN:(j + 1) * S5_BN] = bu[:, S5_BN:]


def _s5_project_out(u, d, cblk_ref, sre_ref, sim_ref):
    ys = []
    for j in range(S5_BLK):
        sl = slice(j * S5_BN, (j + 1) * S5_BN)
        ys.append(_dot1(sre_ref[:, sl], cblk_ref[j, :S5_BN, :], 1, 0) + _dot1(sim_ref[:, sl], cblk_ref[j, S5_BN:, :], 1, 0))
    return jnp.concatenate(ys, axis=1) + d * u


def _s5_fwd(u, bblk, cblk, abr, abi, d):
    length = u.shape[0]
    tl = min(256, length)
    nt = length // tl

    def body(u_ref, b_ref, c_ref, ar_ref, ai_ref, d_ref, ys_ref, kr_ref, ki_ref, sre, sim, car_re, car_im):
        @pl.when(pl.program_id(0) == 0)
        def _():
            car_re[...] = jnp.zeros_like(car_re)
            car_im[...] = jnp.zeros_like(car_im)

        kr_ref[0] = car_re[...]
        ki_ref[0] = car_im[...]
        u_ = u_ref[...]
        _s5_project_in(u_, b_ref, sre, sim)
        _s5_scan(sre, sim, car_re, car_im, ar_ref[...], ai_ref[...], tl, False)
        ys_ref[...] = _gelu(_s5_project_out(u_, d_ref[...], c_ref, sre, sim))

    full = lambda a: pl.BlockSpec(a.shape, lambda i, nd=a.ndim: (0,) * nd)
    chk = pl.BlockSpec((1, SUBLANES, S5N), lambda i: (i, 0, 0))
    return pl.pallas_call(
        body, grid=(nt,), in_specs=[pl.BlockSpec((tl, S5W), lambda i: (i, 0)), full(bblk), full(cblk), full(abr), full(abi), full(d)],
        out_specs=[pl.BlockSpec((tl, S5W), lambda i: (i, 0)), chk, chk],
        out_shape=[SDS((length, S5W), F32), SDS((nt, SUBLANES, S5N), F32), SDS((nt, SUBLANES, S5N), F32)],
        scratch_shapes=[pltpu.VMEM((tl, S5N), F32)] * 2 + [pltpu.VMEM((SUBLANES, S5N), F32)] * 2,
        compiler_params=_params(), name="s5_fwd")(u, bblk, cblk, abr, abi, d)


def _s5_bwd(u, dys_a, dys_b, kre, kim, bblk, cblk, abr, abi, d):
    length = u.shape[0]
    tl = min(256, length)
    nt = length // tl

    def body(u_ref, da_ref, db_ref, kr_ref, ki_ref, b_ref, c_ref, ar_ref, ai_ref, d_ref,
             du_ref, dB_ref, dC_ref, dar_ref, dai_ref, dd_ref, sre, sim, gre, gim, car_re, car_im, dcar_re, dcar_im):
        @pl.when(pl.program_id(0) == 0)
        def _():
            dcar_re[...] = jnp.zeros_like(dcar_re)
            dcar_im[...] = jnp.zeros_like(dcar_im)
            for ref in (dB_ref, dC_ref, dar_ref, dai_ref, dd_ref):
                ref[...] = jnp.zeros_like(ref)

        u_ = u_ref[...]
        abr_, abi_, d_ = ar_ref[...], ai_ref[...], d_ref[...]
        car_re[...] = kr_ref[0]
        car_im[...] = ki_ref[0]
        _s5_project_in(u_, b_ref, sre, sim)
        _s5_scan(sre, sim, car_re, car_im, abr_, abi_, tl, False)
        y = _s5_project_out(u_, d_, c_ref, sre, sim)
        _, vjp = jax.vjp(_gelu, y)
        dy = vjp(da_ref[...] + db_ref[...])[0]
        dd_ref[...] += jnp.sum(dy * u_, axis=0, keepdims=True)
        for j in range(S5_BLK):
            sl = slice(j * S5_BN, (j + 1) * S5_BN)
            dyj = dy[:, j * LANES:(j + 1) * LANES]
            gre[:, sl] = _dot1(dyj, c_ref[j, :S5_BN, :], 1, 1)
            gim[:, sl] = _dot1(dyj, c_ref[j, S5_BN:, :], 1, 1)
            dC_ref[j, :S5_BN, :] += _dot1(sre[:, sl], dyj, 0, 0)
            dC_ref[j, S5_BN:, :] += _dot1(sim[:, sl], dyj, 0, 0)
        _s5_scan(gre, gim, dcar_re, dcar_im, abr_, abi_, tl, True)
        gr, gi = gre[...], gim[...]
        pr = _shift_down(sre[...], kr_ref[0, 0:1, :])
        pi = _shift_down(sim[...], ki_ref[0, 0:1, :])
        dar_ref[...] += jnp.sum(gr * pr + gi * pi, axis=0, keepdims=True)
        dai_ref[...] += jnp.sum(gi * pr - gr * pi, axis=0, keepdims=True)
        dus = []
        for j in range(S5_BLK):
            sl = slice(j * S5_BN, (j + 1) * S5_BN)
            uj = u_[:, j * LANES:(j + 1) * LANES]
            dus.append(_dot1(gre[:, sl], b_ref[j, :, :S5_BN], 1, 1) + _dot1(gim[:, sl], b_ref[j, :, S5_BN:], 1, 1))
            dB_ref[j, :, :S5_BN] += _dot1(uj, gre[:, sl], 0, 0)
            dB_ref[j, :, S5_BN:] += _dot1(uj, gim[:, sl], 0, 0)
        du_ref[...] = (jnp.concatenate(dus, axis=1) + d_ * dy).astype(du_ref.dtype)

    rmap = lambda i: nt - 1 - i
    full = lambda a: pl.BlockSpec(a.shape, lambda i, nd=a.ndim: (0,) * nd)
    row = pl.BlockSpec((tl, S5W), lambda i: (rmap(i), 0))
    chk = pl.BlockSpec((1, SUBLANES, S5N), lambda i: (rmap(i), 0, 0))
    return pl.pallas_call(
        body, grid=(nt,), in_specs=[row, row, row, chk, chk, full(bblk), full(cblk), full(abr), full(abi), full(d)],
        out_specs=[row, full(bblk), full(cblk), full(abr), full(abi), full(d)],
        out_shape=[SDS((length, S5W), BF16), SDS(bblk.shape, F32), SDS(cblk.shape, F32), SDS(abr.shape, F32),
                   SDS(abi.shape, F32), SDS(d.shape, F32)],
        scratch_shapes=[pltpu.VMEM((tl, S5N), F32)] * 4 + [pltpu.VMEM((SUBLANES, S5N), F32)] * 4,
        compiler_params=_params(), name="s5_bwd")(u, dys_a, dys_b, kre, kim, bblk, cblk, abr, abi, d)


def _s5_blockdiag_in(bb):
    t = bb.reshape(S5C, S5_BLK, S5_BN).transpose(1, 0, 2)
    t = jnp.tile(t, (1, SUBLANES, 1))
    mask = (lax.broadcasted_iota(jnp.int32, (LANES, S5_BN), 0) // S5C) == (lax.broadcasted_iota(jnp.int32, (LANES, S5_BN), 1) // S5P)
    return jnp.where(mask[None], t, 0.0)


def _s5_blockdiag_in_t(dblk):
    t = dblk.reshape(S5_BLK, SUBLANES, S5C, SUBLANES, S5P)
    t = jnp.diagonal(t, axis1=1, axis2=3)
    return t.transpose(1, 0, 3, 2).reshape(S5C, S5N)


def _s5_blockdiag_out(c):
    t = c.reshape(S5_BLK, SUBLANES, S5C, S5P).transpose(0, 3, 1, 2).reshape(S5_BLK, S5P, LANES)
    t = jnp.tile(t, (1, SUBLANES, 1))
    mask = (lax.broadcasted_iota(jnp.int32, (S5_BN, LANES), 0) // S5P) == (lax.broadcasted_iota(jnp.int32, (S5_BN, LANES), 1) // S5C)
    return jnp.where(mask[None], t, 0.0)


def _s5_blockdiag_out_t(dblk):
    t = dblk.reshape(S5_BLK, SUBLANES, S5P, SUBLANES, S5C)
    t = jnp.diagonal(t, axis1=1, axis2=3)
    return t.transpose(0, 3, 2, 1).reshape(S5G, S5C, S5P)


def _wkv_consts():
    lane = lax.broadcasted_iota(jnp.int32, (4 * HN, LANES), 1)
    row = lax.broadcasted_iota(jnp.int32, (4 * HN, LANES), 0)
    diag = ((lane % HN) == (row % HN)).astype(F32)
    ones = _block_ones(LANES, HN).astype(BF16)
    return lane % HN, diag, jnp.concatenate([ones, ones], axis=0)


def _wkv_split(p):
    hi = p.astype(BF16)
    lo = (p - hi.astype(F32)).astype(BF16)
    return jnp.concatenate([hi, lo], axis=1)


def _wkv_rowseg(p, ones):
    return jnp.dot(_wkv_split(p), ones, preferred_element_type=F32)


def _wkv_full(rows_ref, t):
    q = jnp.broadcast_to(rows_ref[t], (SUBLANES, RW))
    return jnp.concatenate([jnp.tile(q[:, hp * LANES:(hp + 1) * LANES], (SUBLANES, 1)) for hp in range(4)], axis=0)


def _wkv_colsum8(x):
    return jnp.concatenate([x[hp * HN:(hp + 1) * HN].reshape(SUBLANES, SUBLANES, LANES).sum(axis=0) for hp in range(4)], axis=1)


def _wkv_prebroadcast(src_refs, dst_refs, n):
    for s, dref in zip(src_refs, dst_refs, strict=True):
        dref[...] = s[...].reshape(n, 1, RW)


def _wkv_pair_sums(lhs_ref, out_ref):
    n, ns = out_ref.shape[0], out_ref.shape[1]
    ones_pair = _block_ones(2 * LANES, HN).astype(BF16)
    out_ref[...] = jnp.dot(lhs_ref[0:n].reshape(n * ns, 2 * LANES), ones_pair,
                           preferred_element_type=F32).reshape(n, ns, 2 * LANES)


def _wkv_transpose_out(acc, c, diag):
    lane = lax.broadcasted_iota(jnp.int32, (c, LANES), 1)
    outs = []
    for hp in range(4):
        zp = lax.dot_general(acc[hp * HN:(hp + 1) * HN], diag[:HN], (((0,), (0,)), ((), ())), precision=HI,
                             preferred_element_type=F32)
        outs.append(jnp.where(lane < HN, zp[:c], zp[HN:HN + c]))
    return jnp.concatenate(outs, axis=1)


def _wkv_fwd(r, w, k, v, a, b):
    length = r.shape[0]
    c = min(WKV_CHUNK, length)
    nc = length // c
    ns = 4 * HN

    def body(r_ref, w_ref, k_ref, v_ref, a_ref, b_ref, y_ref, sst_ref, sast_ref, vst_ref, fin_ref,
             s_ref, acc_ref, lhs_ref, ysum_ref, rb, wb, kb, vb, ab, bb, wab, beb, kab):
        @pl.when(pl.program_id(0) == 0)
        def _():
            s_ref[...] = jnp.zeros_like(s_ref)

        a_next = pltpu.roll(a_ref[...], c - 1, 0)
        o512 = _block_ones(RW, HN)
        _wkv_prebroadcast((r_ref, w_ref, k_ref, v_ref, a_ref, b_ref, w_ref[...] * a_next,
                           _segsum_impl(b_ref[...] * a_next, o512), _segsum_impl(k_ref[...] * a_next, o512)),
                          (rb, wb, kb, vb, ab, bb, wab, beb, kab), c)
        lane_t, diag, ones = _wkv_consts()

        def fill_v(q, carry):
            for half in range(2):
                lhs_ref[q, :, half * LANES:(half + 1) * LANES] = (_wkv_full(vb, 2 * q + half) * diag).astype(BF16)
            return carry

        lax.fori_loop(0, c // 2, fill_v, 0)
        _wkv_pair_sums(lhs_ref, vst_ref)

        def pair(p, carry):
            t0, t1 = 2 * p, 2 * p + 1
            s0 = s_ref[...]
            both = _wkv_rowseg(jnp.concatenate([s0 * _wkv_full(ab, t0), s0 * _wkv_full(wab, t0)], axis=0), ones)
            v_pair = vst_ref[p]
            sa0, v0 = both[:ns], v_pair[:, :LANES]
            s1 = s0 * _wkv_full(wb, t0) + sa0 * _wkv_full(bb, t0) + v0 * _wkv_full(kb, t0)
            sa1 = both[ns:] + sa0 * _wkv_full(beb, t0) + v0 * _wkv_full(kab, t0)
            s2 = s1 * _wkv_full(wb, t1) + sa1 * _wkv_full(bb, t1) + v_pair[:, LANES:] * _wkv_full(kb, t1)
            sst_ref[t0] = s0
            sast_ref[t0] = sa0
            sst_ref[t1] = s1
            sast_ref[t1] = sa1
            s_ref[...] = s2
            lhs_ref[p, :, 0:LANES] = (s1 * _wkv_full(rb, t0)).astype(BF16)
            lhs_ref[p, :, LANES:2 * LANES] = (s2 * _wkv_full(rb, t1)).astype(BF16)
            return carry

        lax.fori_loop(0, c // 2, pair, 0)
        _wkv_pair_sums(lhs_ref, ysum_ref)
        acc_ref[...] = jnp.zeros_like(acc_ref)

        def gather_y(p, carry):
            both = ysum_ref[p]
            acc = jnp.where(lane_t == 2 * p, both[:, :LANES], acc_ref[...])
            acc_ref[...] = jnp.where(lane_t == 2 * p + 1, both[:, LANES:], acc)
            return carry

        lax.fori_loop(0, c // 2, gather_y, 0)
        y_ref[...] = _wkv_transpose_out(acc_ref[...], c, diag)
        fin_ref[...] = s_ref[...]

    row = pl.BlockSpec((c, RW), lambda i: (i, 0))
    st = pl.BlockSpec((c, ns, LANES), lambda i: (i, 0, 0))
    state = pltpu.VMEM((ns, LANES), F32)
    st_pair = pl.BlockSpec((c // 2, ns, 2 * LANES), lambda i: (i, 0, 0))
    return pl.pallas_call(
        body, grid=(nc,), in_specs=[row] * 6,
        out_specs=[row, st, st, st_pair, pl.BlockSpec((ns, LANES), lambda i: (0, 0))],
        out_shape=[SDS((length, RW), F32)] + [SDS((length, ns, LANES), F32)] * 2 + [SDS((length // 2, ns, 2 * LANES), F32)]
        + [SDS((ns, LANES), F32)],
        scratch_shapes=[state] * 2 + [pltpu.VMEM((c, ns, 2 * LANES), BF16), pltpu.VMEM((c // 2, ns, 2 * LANES), F32)]
        + [pltpu.VMEM((c, 1, RW), F32)] * 9,
        compiler_params=_params(), name="wkv_fwd")(r, w, k, v, a, b)


def _wkv_bwd(r, w, k, a, b, dy, sst, sast, vst, fin):
    length = r.shape[0]
    c = min(WKV_CHUNK, length)
    nc = length // c
    ns = 4 * HN

    def body(r_ref, w_ref, k_ref, a_ref, b_ref, dy_ref, sst_ref, sast_ref, vst_ref, snext_ref, fin_ref,
             dr_ref, dw_ref, dk_ref, da_ref, db_ref, dv_ref,
             ds_ref, cur_ref, acc_ref, dyst, lhs_ref, dvsum_ref, rb, wb, kb, ab, bb, dyb, wbb, alb, rhb,
             p_r, p_w, p_k, p_a, p_b):
        @pl.when(pl.program_id(0) == 0)
        def _():
            ds_ref[...] = jnp.zeros_like(ds_ref)

        b_prev = pltpu.roll(b_ref[...], 1, 0)
        o512 = _block_ones(RW, HN)
        _wkv_prebroadcast((r_ref, w_ref, k_ref, a_ref, b_ref, dy_ref, w_ref[...] * b_prev,
                           _segsum_impl(a_ref[...] * b_prev, o512), _segsum_impl(r_ref[...] * b_ref[...], o512)),
                          (rb, wb, kb, ab, bb, dyb, wbb, alb, rhb), c)
        lane_t, diag, ones = _wkv_consts()

        def fill(q, carry):
            for half in range(2):
                lhs_ref[q, :, half * LANES:(half + 1) * LANES] = (_wkv_full(dyb, 2 * q + half) * diag).astype(BF16)
            return carry

        lax.fori_loop(0, c // 2, fill, 0)
        _wkv_pair_sums(lhs_ref, dyst)
        cur_ref[...] = jnp.where(pl.program_id(0) == 0, fin_ref[...], snext_ref[0])

        def pair(p, carry):
            ta = c - 1 - 2 * p
            tb = ta - 1
            s_a, s_b, s_c = cur_ref[...], sst_ref[ta], sst_ref[tb]
            dy_pair = dyst[c // 2 - 1 - p]
            dy_a, dy_b = dy_pair[:, LANES:], dy_pair[:, :LANES]
            ds_a = ds_ref[...] + dy_a * _wkv_full(rb, ta)
            both = _wkv_rowseg(jnp.concatenate([ds_a * _wkv_full(bb, ta), ds_a * _wkv_full(wbb, ta)], axis=0), ones)
            dsa_a = both[:ns]
            ds_b = ds_a * _wkv_full(wb, ta) + dsa_a * _wkv_full(ab, ta) + dy_b * _wkv_full(rb, tb)
            dsa_b = both[ns:] + dsa_a * _wkv_full(alb, ta) + dy_b * _wkv_full(rhb, tb)
            ds_ref[...] = ds_b * _wkv_full(wb, tb) + dsa_b * _wkv_full(ab, tb)
            cur_ref[...] = s_c
            lhs_ref[p, :, 0:LANES] = (ds_a * _wkv_full(kb, ta)).astype(BF16)
            lhs_ref[p, :, LANES:2 * LANES] = (ds_b * _wkv_full(kb, tb)).astype(BF16)
            v_pair = vst_ref[c // 2 - 1 - p]
            for t, s_t, s_prev, ds, dsa, dyy, vv in ((ta, s_a, s_b, ds_a, dsa_a, dy_a, v_pair[:, LANES:]),
                                                     (tb, s_b, s_c, ds_b, dsa_b, dy_b, v_pair[:, :LANES])):
                p_r[t] = _wkv_colsum8(s_t * dyy)
                p_k[t] = _wkv_colsum8(ds * vv)
                p_b[t] = _wkv_colsum8(ds * sast_ref[t])
                p_w[t] = _wkv_colsum8(ds * s_prev)
                p_a[t] = _wkv_colsum8(s_prev * dsa)
            return carry

        lax.fori_loop(0, c // 2, pair, 0)
        _wkv_pair_sums(lhs_ref, dvsum_ref)
        acc_ref[...] = jnp.zeros_like(acc_ref)

        def gather_dv(p, carry):
            both = dvsum_ref[p]
            acc = jnp.where(lane_t == c - 1 - 2 * p, both[:, :LANES], acc_ref[...])
            acc_ref[...] = jnp.where(lane_t == c - 2 - 2 * p, both[:, LANES:], acc)
            return carry

        lax.fori_loop(0, c // 2, gather_dv, 0)
        sel = (lax.broadcasted_iota(jnp.int32, (c, c * SUBLANES), 1) // SUBLANES
               == lax.broadcasted_iota(jnp.int32, (c, c * SUBLANES), 0)).astype(F32)
        for out_ref, part in ((dr_ref, p_r), (dw_ref, p_w), (dk_ref, p_k), (da_ref, p_a), (db_ref, p_b)):
            out_ref[...] = _dot3(sel, part[...].reshape(c * SUBLANES, RW), 1, 0)
        dv_ref[...] = _wkv_transpose_out(acc_ref[...], c, diag)

    rmap = lambda i: nc - 1 - i
    row = pl.BlockSpec((c, RW), lambda i: (rmap(i), 0))
    st = pl.BlockSpec((c, ns, LANES), lambda i: (rmap(i), 0, 0))
    nxt = pl.BlockSpec((1, ns, LANES), lambda i: (jnp.minimum((rmap(i) + 1) * c, length - 1), 0, 0))
    state = pltpu.VMEM((ns, LANES), F32)
    st_pair = pl.BlockSpec((c // 2, ns, 2 * LANES), lambda i: (rmap(i), 0, 0))
    return pl.pallas_call(
        body, grid=(nc,), in_specs=[row] * 6 + [st, st, st_pair, nxt, pl.BlockSpec((ns, LANES), lambda i: (0, 0))],
        out_specs=[row] * 6, out_shape=[SDS((length, RW), F32)] * 6,
        scratch_shapes=[state] * 3 + [pltpu.VMEM((c // 2, ns, 2 * LANES), F32), pltpu.VMEM((c, ns, 2 * LANES), BF16),
                                       pltpu.VMEM((c // 2, ns, 2 * LANES), F32)]
        + [pltpu.VMEM((c, 1, RW), F32)] * 9 + [pltpu.VMEM((c, SUBLANES, RW), F32)] * 5,
        compiler_params=_params(), name="wkv_bwd")(r, w, k, a, b, dy, sst, sast, vst, sst, fin)


def _my_place():
    return lax.axis_index("x"), lax.axis_index("y"), lax.axis_index("c")


def _peer(x, y, c, k):
    return (x ^ ((k >> 2) & 1), y ^ ((k >> 1) & 1), c ^ (k & 1))


def _all_gather(shard, name):
    rows = shard.shape[0]

    def body(in_ref, out_ref, send_sems, recv_sems, local_sem):
        x, y, c = _my_place()
        me = 4 * x + 2 * y + c
        mine = pltpu.make_async_copy(in_ref, out_ref.at[me], local_sem)
        mine.start()
        sends = []
        for k in range(1, N_DEV):
            cp = pltpu.make_async_remote_copy(src_ref=in_ref, dst_ref=out_ref.at[me], send_sem=send_sems.at[k - 1],
                                              recv_sem=recv_sems.at[k - 1], device_id=_peer(x, y, c, k), device_id_type=MESH)
            cp.start()
            sends.append(cp)
        for k in range(1, N_DEV):
            px, py, pc = _peer(x, y, c, k)
            pltpu.make_async_remote_copy(src_ref=in_ref, dst_ref=out_ref.at[4 * px + 2 * py + pc], send_sem=send_sems.at[k - 1],
                                         recv_sem=recv_sems.at[k - 1], device_id=(px, py, pc), device_id_type=MESH).wait_recv()
        for cp in sends:
            cp.wait_send()
        mine.wait()

    return pl.pallas_call(
        body, out_shape=SDS((N_DEV, rows, LANES), shard.dtype),
        in_specs=[pl.BlockSpec(memory_space=pl.ANY)], out_specs=pl.BlockSpec(memory_space=pl.ANY),
        scratch_shapes=[pltpu.SemaphoreType.DMA((N_DEV - 1,)), pltpu.SemaphoreType.DMA((N_DEV - 1,)), pltpu.SemaphoreType.DMA],
        name=name)(shard)


_HBM_SPEC = pl.BlockSpec(memory_space=pltpu.HBM)
_SEM_SPEC = pl.BlockSpec(memory_space=pltpu.SEMAPHORE)
_DATAFLOW = pltpu.SideEffectType.DATAFLOW_SIDE_EFFECTING


def _exchange_start(srcs, per_peer, name):
    n = len(srcs)
    lands = [jnp.zeros((N_DEV, s.shape[-2], LANES), s.dtype) for s in srcs]

    def body(*refs):
        src_refs, land_refs = refs[:n], refs[n:2 * n]
        send_sems, recv_sems, token = refs[2 * n], refs[2 * n + 1], refs[-1]
        x, y, c = _my_place()
        me = 4 * x + 2 * y + c
        for a in range(n):
            for k in range(1, N_DEV):
                px, py, pc = _peer(x, y, c, k)
                mine = src_refs[a].at[4 * px + 2 * py + pc] if per_peer else src_refs[a]
                pltpu.make_async_remote_copy(src_ref=mine, dst_ref=land_refs[a].at[me], send_sem=send_sems.at[7 * a + k - 1],
                                             recv_sem=recv_sems.at[7 * a + k - 1], device_id=(px, py, pc),
                                             device_id_type=MESH).start()
        token[...] = jnp.zeros_like(token)

    n_sem = (N_DEV - 1) * n
    res = pl.pallas_call(
        body, name=name,
        out_shape=(pltpu.SemaphoreType.DMA((n_sem,)), pltpu.SemaphoreType.DMA((n_sem,)),
                   *[pltpu.HBM(a.shape, a.dtype) for a in srcs + lands], SDS((SUBLANES, LANES), F32)),
        in_specs=(_HBM_SPEC,) * (2 * n),
        out_specs=(_SEM_SPEC, _SEM_SPEC) + (_HBM_SPEC,) * (2 * n) + (pl.BlockSpec(memory_space=pltpu.VMEM),),
        input_output_aliases={i: 2 + i for i in range(2 * n)}, compiler_params=pltpu.CompilerParams(has_side_effects=_DATAFLOW),
    )(*[pltpu.with_memory_space_constraint(a, pltpu.HBM) for a in srcs + lands])
    return res[0], res[1], list(res[2:2 + n]), list(res[2 + n:2 + 2 * n]), res[-1]


def _exchange_wait(started, after, per_peer, name):
    send_sems, recv_sems, src_thrus, land_thrus, _ = started
    n = len(src_thrus)

    def body(*refs):
        src_refs, land_refs = refs[:n], refs[n:2 * n]
        send_sems, recv_sems = refs[2 * n], refs[2 * n + 1]
        x, y, c = _my_place()
        me = 4 * x + 2 * y + c
        for a in range(n):
            for k in range(1, N_DEV):
                px, py, pc = _peer(x, y, c, k)
                mine = src_refs[a].at[me] if per_peer else src_refs[a]
                copy = pltpu.make_async_remote_copy(src_ref=mine, dst_ref=land_refs[a].at[4 * px + 2 * py + pc],
                                                    send_sem=send_sems.at[7 * a + k - 1], recv_sem=recv_sems.at[7 * a + k - 1],
                                                    device_id=(px, py, pc), device_id_type=MESH)
                copy.wait_send()
                copy.wait_recv()

    res = pl.pallas_call(
        body, name=name, out_shape=tuple(pltpu.HBM(a.shape, a.dtype) for a in src_thrus + land_thrus),
        in_specs=(_HBM_SPEC,) * (2 * n) + (_SEM_SPEC, _SEM_SPEC, pl.BlockSpec(memory_space=pl.ANY)),
        out_specs=(_HBM_SPEC,) * (2 * n), input_output_aliases={i: i for i in range(2 * n)},
        compiler_params=pltpu.CompilerParams(has_side_effects=_DATAFLOW),
    )(*src_thrus, *land_thrus, send_sems, recv_sems, after)
    return list(res[n:])


SUM_ROWS = 256


def _sum_parts(parts, name, own=None):
    rows = parts.shape[1]

    def body(*refs):
        p_ref, g_out = refs[0], refs[-1]
        g = p_ref[0].astype(F32)
        for s in range(1, N_DEV):
            g = g + p_ref[s].astype(F32)
        if own is not None:
            g = g + refs[1][...].astype(F32)
        g_out[...] = g

    tr = min(SUM_ROWS, rows)
    blk = pl.BlockSpec((tr, LANES), lambda i: (i, 0))
    extra = [] if own is None else [own]
    return pl.pallas_call(
        body, grid=(rows // tr,),
        in_specs=[pl.BlockSpec((N_DEV, tr, LANES), lambda i: (0, i, 0))] + [blk] * len(extra),
        out_specs=blk, out_shape=SDS((rows, LANES), F32),
        compiler_params=_params(("parallel",)), name=name)(parts, *extra)


def _adamw(g, wgt, m, v):
    rows = wgt.shape[0]

    def body(g_ref, w_ref, m_ref, v_ref, d_out, m_out, v_out):
        g_ = g_ref[...]
        m_new = ADAM_B1 * m_ref[...] + (1.0 - ADAM_B1) * g_
        v_new = ADAM_B2 * v_ref[...] + (1.0 - ADAM_B2) * (g_ * g_)
        m_hat = m_new / (1.0 - ADAM_B1 ** ADAM_STEP)
        v_hat = v_new / (1.0 - ADAM_B2 ** ADAM_STEP)
        d_out[...] = -ADAM_LR * (m_hat / (jnp.sqrt(v_hat) + ADAM_EPS) + ADAM_WD * w_ref[...])
        m_out[...] = m_new
        v_out[...] = v_new

    blk = pl.BlockSpec((PACK_ROWS, LANES), lambda i: (i, 0))
    return pl.pallas_call(
        body, grid=(rows // PACK_ROWS,), in_specs=[blk] * 4, out_specs=[blk] * 3, out_shape=[SDS((rows, LANES), F32)] * 3,
        compiler_params=_params(("parallel",)), name="adamw")(g, wgt, m, v)


PACK_ALIGN = 2 * SUBLANES * LANES
PACK_ROWS = 512

ROW_SHARDED = ("s5_w_glu", "w_out", "ffn_w_down")
TRANSPOSED = ("w_in", "ffn_w_up", "w_branch_rwkv", "w_branch_s5")
SMALL_SHARDED = ("rwkv_w2", "rwkv_a2", "rwkv_g2", "ffn_conv_w")
SHARDED = ROW_SHARDED + TRANSPOSED + SMALL_SHARDED
F32_GATHER = ("ffn_conv_w",)
REPLICATED =("norm_mix_pre", "norm_mix_post", "norm_ffn_pre", "norm_ffn_post", "b_gate", "rwkv_shift_mu", "rwkv_w0",
              "rwkv_a0", "rwkv_k_k", "rwkv_k_a", "rwkv_r_k", "rwkv_lnx_w", "rwkv_lnx_b", "s5_a_re", "s5_a_im", "s5_b_re",
              "s5_b_im", "s5_c_re", "s5_c_im", "s5_d", "s5_log_step", "s5_b_glu", "ffn_conv_b")
WEIGHTS = ("norm_mix_pre", "norm_mix_post", "norm_ffn_pre", "norm_ffn_post", "w_in", "b_gate", "rwkv_shift_mu", "rwkv_w0",
           "rwkv_w2", "rwkv_a0", "rwkv_a2", "rwkv_g2", "rwkv_k_k", "rwkv_k_a", "rwkv_r_k", "rwkv_lnx_w", "rwkv_lnx_b",
           "s5_a_re", "s5_a_im", "s5_b_re", "s5_b_im", "s5_c_re", "s5_c_im", "s5_d", "s5_log_step", "s5_w_glu", "s5_b_glu",
           "w_branch_rwkv", "w_branch_s5", "w_out", "ffn_w_up", "ffn_conv_w", "ffn_conv_b", "ffn_w_down")


def _pad_flat(a):
    flat = a.reshape(-1)
    pad = (-flat.shape[0]) % PACK_ALIGN
    return jnp.pad(flat, (0, pad)) if pad else flat


def _pad_cols(a, mult):
    pad = (-a.shape[1]) % mult
    return jnp.pad(a, ((0, 0), (0, pad))) if pad else a


def _pack(pieces):
    flat = jnp.concatenate([_pad_flat(p) for p in pieces])
    pad = (-flat.shape[0]) % (PACK_ROWS * LANES)
    return (jnp.pad(flat, (0, pad)) if pad else flat).reshape(-1, LANES)


def _unpack(buf, shapes):
    lead = buf.shape[:-2]
    flat = buf.reshape(*lead, -1)
    out, off = [], 0
    for s in shapes:
        n = math.prod(s)
        out.append(flat[..., off:off + n].reshape(*lead, *s))
        off += n + (-n) % PACK_ALIGN
    return out


def _rows_to_full(stack):
    return stack.reshape(-1, stack.shape[2])


def _cols_to_full(stack):
    return stack.transpose(1, 0, 2).reshape(stack.shape[1], -1)


EARLY_GRADS = ("ffn_w_down", "ffn_w_up", "w_out")
REST_GRADS = ("s5_w_glu", "w_in", "w_branch_rwkv", "w_branch_s5")


def _local_step(x, tgt, wt, get_late, send_early, send_rest):
    wt = dict(wt)
    o512 = _block_ones(RW, HN)
    w_in = wt["w_in"]
    w_rw, w_u, w_g = w_in[:N_RWKV], w_in[N_RWKV:N_RWKV + S5W], w_in[N_RWKV + S5W:]
    b_gate = wt["b_gate"]
    b_r, b_s = b_gate[:, :D_MODEL], b_gate[:, D_MODEL:]
    zpad = lambda a, lo, n: jnp.pad(a.astype(F32), ((lo, n - lo - a.shape[0]), (0, 0)))
    w2p, a2p, g2p = zpad(wt["rwkv_w2"], 0, 256), zpad(wt["rwkv_a2"], 64, 256), zpad(wt["rwkv_g2"], 128, 256)
    pre_small = [wt["rwkv_w0"], wt["rwkv_a0"], wt["rwkv_k_k"], wt["rwkv_k_a"], w2p, a2p, g2p]
    post_prm = [wt["rwkv_lnx_w"], wt["rwkv_lnx_b"], wt["rwkv_r_k"].reshape(1, RW)]
    mu = wt["rwkv_shift_mu"]

    a_re, a_im = wt["s5_a_re"].reshape(1, S5N), wt["s5_a_im"].reshape(1, S5N)
    ls = jnp.repeat(wt["s5_log_step"].reshape(S5G), S5P).reshape(1, S5N)
    b_re_t = wt["s5_b_re"].reshape(S5N, S5C).T
    b_im_t = wt["s5_b_im"].reshape(S5N, S5C).T
    c_re, c_im = wt["s5_c_re"].reshape(S5G, S5C, S5P), wt["s5_c_im"].reshape(S5G, S5C, S5P)
    abr, abi, bbr, bbi = _s5_prep_fwd(a_re, a_im, ls, b_re_t, b_im_t)
    bblk = jnp.concatenate([_s5_blockdiag_in(bbr), _s5_blockdiag_in(bbi)], axis=2)
    cblk = jnp.concatenate([_s5_blockdiag_out(c_re), -_s5_blockdiag_out(c_im)], axis=1)
    s5_d = wt["s5_d"]

    h1 = _rms_fwd(x, wt["norm_mix_pre"], "rms1_fwd")
    p_rw = _mm(h1, w_rw, tb=True, name="mm_proj_rwkv")
    u = _mm(h1, w_u, tb=True, name="mm_proj_s5")
    gpre = _mm(h1, w_g, tb=True, name="mm_proj_gate")
    r, decay, k2, v, aa, bb, g = _rwkv_pre_fwd(p_rw, mu, pre_small, o512)
    y, sst, sast, vst, s_fin = _wkv_fwd(r, decay, k2, v, aa, bb)
    wt.update(get_late(y))
    y_r = _rwkv_post_fwd(y, r, k2, v, g, post_prm, o512)
    o_r = _mm(y_r, wt["w_branch_rwkv"], tb=True, name="mm_branch_rwkv")
    ys, kre, kim = _s5_fwd(u, bblk, cblk, abr, abi, s5_d)
    t_glu = _mm(ys, wt["s5_w_glu"], name="mm_glu")
    out5 = _glu_fwd(ys, t_glu, wt["s5_b_glu"])
    o_s = _mm(out5, wt["w_branch_s5"], tb=True, name="mm_branch_s5")
    mi = _merge_fwd(gpre, o_r, o_s, b_r, b_s)
    mixed = _mm(mi, wt["w_out"], name="mm_out")
    x2, h2 = _mid_fwd(x, mixed, wt["norm_mix_post"], wt["norm_ffn_pre"])
    z = _mm(h2, wt["ffn_w_up"], tb=True, name="mm_up")
    act = _conv_fwd(z, wt["ffn_conv_w"], wt["ffn_conv_b"])
    f = _mm(act, wt["ffn_w_down"], name="mm_down")
    dx3, df, d_norm_ffn_post, loss_part = _final(x2, f, tgt, wt["norm_ffn_post"])

    gr = {"norm_ffn_post": d_norm_ffn_post}
    dact = _mm(df, wt["ffn_w_down"], tb=True, name="mm_down_dx")
    gr["ffn_w_down"] = _mm(act, df, ta=True, out_dtype=BF16, name="mm_down_dw")
    dzg, dzv, gr["ffn_conv_w"], gr["ffn_conv_b"] = _conv_bwd(z, dact, wt["ffn_conv_w"], wt["ffn_conv_b"])
    dz = jnp.concatenate([dzg, dzv], axis=1)
    dh2 = _mm(dz, wt["ffn_w_up"], name="mm_up_dx")
    gr["ffn_w_up"] = _mm(dz, h2, ta=True, out_dtype=BF16, name="mm_up_dw")
    dx2, dmixed, gr["norm_mix_post"], gr["norm_ffn_pre"] = _mid_bwd(x2, mixed, dh2, dx3, wt["norm_mix_post"], wt["norm_ffn_pre"])
    dmi = _mm(dmixed, wt["w_out"], tb=True, name="mm_out_dx")
    gr["w_out"] = _mm(mi, dmixed, ta=True, out_dtype=BF16, name="mm_out_dw")
    token = send_early({n: gr[n] for n in EARLY_GRADS})
    b_r = b_r + jnp.tile(token[0:1, :], (1, D_MODEL // LANES))
    dgp_r, dgp_s, do_r, do_s, db_r, db_s = _merge_bwd(gpre, o_r, o_s, dmi, b_r, b_s)
    gr["b_gate"] = jnp.concatenate([db_r, db_s], axis=1)
    dout5 = _mm(do_s, wt["w_branch_s5"], name="mm_branch_s5_dx")
    gr["w_branch_s5"] = _mm(do_s, out5, ta=True, out_dtype=BF16, name="mm_branch_s5_dw")
    dys_a, dt_glu, gr["s5_b_glu"] = _glu_bwd(ys, t_glu, dout5, wt["s5_b_glu"])
    dys_b = _mm(dt_glu, wt["s5_w_glu"], tb=True, name="mm_glu_dx")
    gr["s5_w_glu"] = _mm(ys, dt_glu, ta=True, out_dtype=BF16, name="mm_glu_dw")
    du, dbblk, dcblk, dabr, dabi, gr["s5_d"] = _s5_bwd(u, dys_a, dys_b, kre, kim, bblk, cblk, abr, abi, s5_d)
    gr["s5_c_re"] = _s5_blockdiag_out_t(dcblk[:, :S5_BN, :]).reshape(wt["s5_c_re"].shape)
    gr["s5_c_im"] = (-_s5_blockdiag_out_t(dcblk[:, S5_BN:, :])).reshape(wt["s5_c_im"].shape)
    dbbr, dbbi = _s5_blockdiag_in_t(dbblk[:, :, :S5_BN]), _s5_blockdiag_in_t(dbblk[:, :, S5_BN:])
    gsel = (lax.broadcasted_iota(jnp.int32, (S5N, LANES), 0) // S5P == lax.broadcasted_iota(jnp.int32, (S5N, LANES), 1)).astype(F32)
    d_are, d_aim, d_ls, d_bre_t, d_bim_t = _s5_prep_bwd(a_re, a_im, ls, b_re_t, b_im_t, (dabr, dabi, dbbr, dbbi), gsel)
    gr["s5_a_re"] = d_are.reshape(wt["s5_a_re"].shape)
    gr["s5_a_im"] = d_aim.reshape(wt["s5_a_im"].shape)
    gr["s5_log_step"] = d_ls[0:1, :S5G]
    gr["s5_b_re"] = d_bre_t.T.reshape(wt["s5_b_re"].shape)
    gr["s5_b_im"] = d_bim_t.T.reshape(wt["s5_b_im"].shape)
    dy_r = _mm(do_r, wt["w_branch_rwkv"], name="mm_branch_rwkv_dx")
    gr["w_branch_rwkv"] = _mm(do_r, y_r, ta=True, out_dtype=BF16, name="mm_branch_rwkv_dw")
    dy, dr1, dk1, dv1, dg, gr["rwkv_lnx_w"], gr["rwkv_lnx_b"], d_rk = _rwkv_post_bwd(y, r, k2, v, g, dy_r, post_prm, o512)
    gr["rwkv_r_k"] = d_rk.reshape(wt["rwkv_r_k"].shape)
    dr2, ddecay, dk2, daa, dbb, dv2 = _wkv_bwd(r, decay, k2, aa, bb, dy, sst, sast, vst, s_fin)
    cots = [dr1 + dr2, ddecay, dk1 + dk2, dv1 + dv2, daa, dbb, dg]
    dp_rw, gr["rwkv_shift_mu"], gr["rwkv_w0"], gr["rwkv_a0"], gr["rwkv_k_k"], gr["rwkv_k_a"], dw2p, da2p, dg2p = \
        _rwkv_pre_bwd(p_rw, cots, mu, pre_small, o512)
    gr["rwkv_w2"], gr["rwkv_a2"], gr["rwkv_g2"] = dw2p[:64], da2p[64:128], dg2p[128:]
    dproj = jnp.concatenate([dp_rw, du, dgp_r, dgp_s], axis=1)
    gr["w_in"] = _mm(dproj, h1, ta=True, out_dtype=BF16, name="mm_proj_dw")
    token = send_rest(gr)
    dh1 = _mm(dproj, w_in + token[0, 0].astype(w_in.dtype), name="mm_proj_dx")
    dx, gr["norm_mix_pre"] = _rms_bwd(x, wt["norm_mix_pre"], dh1, dx2, "rms1_bwd")
    return loss_part[0, 0], dx, gr


def kernel(x, norm_mix_pre, norm_mix_post, norm_ffn_pre, norm_ffn_post, w_in, b_gate, rwkv_shift_mu, rwkv_w0, rwkv_w2, rwkv_a0, rwkv_a2, rwkv_g2, rwkv_k_k, rwkv_k_a, rwkv_r_k, rwkv_lnx_w, rwkv_lnx_b, s5_a_re, s5_a_im, s5_b_re, s5_b_im, s5_c_re, s5_c_im, s5_d, s5_log_step, s5_w_glu, s5_b_glu, w_branch_rwkv, w_branch_s5, w_out, ffn_w_up, ffn_conv_w, ffn_conv_b, ffn_w_down, loss_target, m_norm_mix_pre, m_norm_mix_post, m_norm_ffn_pre, m_norm_ffn_post, m_w_in, m_b_gate, m_rwkv_shift_mu, m_rwkv_w0, m_rwkv_w2, m_rwkv_a0, m_rwkv_a2, m_rwkv_g2, m_rwkv_k_k, m_rwkv_k_a, m_rwkv_r_k, m_rwkv_lnx_w, m_rwkv_lnx_b, m_s5_a_re, m_s5_a_im, m_s5_b_re, m_s5_b_im, m_s5_c_re, m_s5_c_im, m_s5_d, m_s5_log_step, m_s5_w_glu, m_s5_b_glu, m_w_branch_rwkv, m_w_branch_s5, m_w_out, m_ffn_w_up, m_ffn_conv_w, m_ffn_conv_b, m_ffn_w_down, v_norm_mix_pre, v_norm_mix_post, v_norm_ffn_pre, v_norm_ffn_post, v_w_in, v_b_gate, v_rwkv_shift_mu, v_rwkv_w0, v_rwkv_w2, v_rwkv_a0, v_rwkv_a2, v_rwkv_g2, v_rwkv_k_k, v_rwkv_k_a, v_rwkv_r_k, v_rwkv_lnx_w, v_rwkv_lnx_b, v_s5_a_re, v_s5_a_im, v_s5_b_re, v_s5_b_im, v_s5_c_re, v_s5_c_im, v_s5_d, v_s5_log_step, v_s5_w_glu, v_s5_b_glu, v_w_branch_rwkv, v_w_branch_s5, v_w_out, v_ffn_w_up, v_ffn_conv_w, v_ffn_conv_b, v_ffn_w_down):
    args = dict(locals())
    wgt = {n: args[n] for n in WEIGHTS}
    mom = {n: args["m_" + n] for n in WEIGHTS}
    var = {n: args["v_" + n] for n in WEIGHTS}
    me = 4 * lax.axis_index("x") + 2 * lax.axis_index("y") + lax.axis_index("c")
    shard_shapes = {n: wgt[n].shape[1:] for n in SHARDED}
    sent_shapes = {n: (shard_shapes[n][::-1] if n in TRANSPOSED else shard_shapes[n]) for n in SHARDED}
    sent = lambda n, a: a.T if n in TRANSPOSED else a

    unshard = lambda n, blocks: _cols_to_full(blocks) if n in SMALL_SHARDED else _rows_to_full(blocks)

    first = ("w_in", "rwkv_w2", "rwkv_a2", "rwkv_g2")
    late = tuple(n for n in SHARDED if n not in first and n not in F32_GATHER)
    late_payloads = [_pack([sent(n, wgt[n][0].astype(BF16)) for n in late]), _pack([wgt[n][0] for n in F32_GATHER])]
    late_started = _exchange_start(late_payloads, False, "gather_late_start")
    got_first = _unpack(_all_gather(_pack([sent(n, wgt[n][0].astype(BF16)) for n in first]), "all_gather_weights"),
                        [sent_shapes[n] for n in first])
    wt = {n: wgt[n] for n in REPLICATED}
    for n, blocks in zip(first, got_first, strict=True):
        wt[n] = unshard(n, blocks)

    def get_late(after):
        lands = _exchange_wait(late_started, after, False, "gather_late_wait")
        out = {}
        for names, land, mine in zip((late, F32_GATHER), lands, late_payloads, strict=True):
            blocks = _unpack(lax.dynamic_update_index_in_dim(land, mine, me, 0), [sent_shapes[n] for n in names])
            out.update({n: unshard(n, b_) for n, b_ in zip(names, blocks, strict=True)})
        return out

    whole = tuple(n for n in SMALL_SHARDED + REPLICATED if n != "norm_mix_pre")
    blocks_of = lambda g_: g_.reshape(N_DEV, -1, LANES)
    started = {}

    def send_early(grads_):
        started["early"] = _exchange_start([blocks_of(grads_[n]) for n in EARLY_GRADS], True, "exchange_early_start")
        return started["early"][4]

    def send_rest(grads_):
        started["rest"] = _exchange_start([blocks_of(grads_[n]) for n in REST_GRADS], True, "exchange_rest_start")
        started["small_mine"] = _pack([grads_[n] for n in whole])
        started["small"] = _exchange_start([started["small_mine"]], False, "gather_small_start")
        return started["rest"][4] + started["small"][4]

    loss_part, dx, gr = _local_step(x[0], loss_target[0], wt, get_late, send_early, send_rest)

    got = {}
    for key, names in (("early", EARLY_GRADS), ("rest", REST_GRADS)):
        lands = _exchange_wait(started[key], dx, True, "exchange_" + key + "_wait")
        for n, land in zip(names, lands, strict=True):
            own = lax.dynamic_index_in_dim(blocks_of(gr[n]), me, 0, keepdims=False)
            got[n] = _sum_parts(land, "sum_" + n, own=own).reshape(sent_shapes[n])
    small_land, = _exchange_wait(started["small"], dx, False, "gather_small_wait")
    small_sum = _sum_parts(lax.dynamic_update_index_in_dim(small_land, started["small_mine"], me, 0), "sum_small_grads")
    got.update(zip(whole, _unpack(small_sum, [gr[n].shape for n in whole]), strict=True))
    last = gr["norm_mix_pre"].reshape(SUBLANES, LANES)
    got["norm_mix_pre"] = _sum_parts(_all_gather(last, "gather_last_grad"), "sum_last_grad").reshape(gr["norm_mix_pre"].shape)
    grads = {}
    for n in WEIGHTS:
        if n in TRANSPOSED:
            grads[n] = got[n].T
        elif n in SMALL_SHARDED:
            cols = shard_shapes[n][1]
            grads[n] = lax.dynamic_slice_in_dim(got[n], me * cols, cols, axis=1)
        else:
            grads[n] = got[n]
        grads[n] = grads[n].reshape(wgt[n].shape)

    pack_local = lambda src: _pack([src[n].reshape(-1) for n in WEIGHTS])
    outs = _adamw(pack_local(grads), pack_local(wgt), pack_local(mom), pack_local(var))
    shapes = [wgt[n].shape for n in WEIGHTS]
    loss = lax.psum(loss_part, AXES)
    return (loss, dx[None], *[grads[n] for n in WEIGHTS], *_unpack(outs[0], shapes), *_unpack(outs[1], shapes),
            *_unpack(outs[2], shapes))
```

```python
import functools
import math

import jax
import jax.numpy as jnp
from jax import lax
from jax.experimental import pallas as pl
from jax.experimental.pallas import tpu as pltpu

F32 = jnp.float32
BF16 = jnp.bfloat16
SDS = jax.ShapeDtypeStruct
HI = lax.Precision.HIGHEST
MESH = pl.DeviceIdType.MESH
AXES = ("x", "y", "c")
N_DEV = 8

D_MODEL = 1024
RW = 512
HN = 64
N_RWKV = 1792
S5W = 512
S5G = 32
S5C = 16
S5P = 64
S5N = S5G * S5P
D_FF = 2816
NORM_EPS = 1e-6
LNX_EPS = 64e-5

ADAM_LR = 0.001
ADAM_B1 = 0.9
ADAM_B2 = 0.999
ADAM_EPS = 1e-08
ADAM_WD = 0.01
ADAM_STEP = 10

LANES = 128
SUBLANES = 8
VMEM_LIMIT = 56 * 1024 * 1024
WKV_CHUNK = 32


def _params(sem=("arbitrary",)):
    return pltpu.CompilerParams(dimension_semantics=sem, vmem_limit_bytes=VMEM_LIMIT)


def _pick(n, cap):
    best = None
    for t in range(LANES, min(n, cap) + 1, LANES):
        if n % t == 0:
            best = t
    return best or n


MM_VMEM_BUDGET = 40 * 1024 * 1024


def _mm_tiles(m, n, k, a_bytes, b_bytes, o_bytes):
    def divisors(d):
        return [t for t in range(LANES, d + 1, LANES) if d % t == 0] or [d]

    tn = _pick(n, 2304)
    best = None
    for tm in divisors(m):
        for tk in divisors(k):
            vmem = 2 * (tm * tk * a_bytes + tk * tn * b_bytes) + tm * tn * 4 + 2 * tm * tn * o_bytes
            if vmem > MM_VMEM_BUDGET:
                continue
            key = ((m // tm) * (n // tn) * (k // tk), -tm)
            if best is None or key < best[0]:
                best = (key, tm, tk)
    assert best is not None, (m, n, k)
    return best[1], tn, best[2]


def _mm(a, b, *, ta=False, tb=False, out_dtype=F32, name):
    m = a.shape[1] if ta else a.shape[0]
    k = a.shape[0] if ta else a.shape[1]
    n = b.shape[0] if tb else b.shape[1]
    assert (b.shape[1] if tb else b.shape[0]) == k
    tm, tn, tk = _mm_tiles(m, n, k, a.dtype.itemsize, b.dtype.itemsize, jnp.dtype(out_dtype).itemsize)
    nk = k // tk
    dims = (((0 if ta else 1,), (1 if tb else 0,)), ((), ()))

    def body(a_ref, b_ref, o_ref, acc_ref):
        kk = pl.program_id(2)

        @pl.when(kk == 0)
        def _():
            acc_ref[...] = jnp.zeros_like(acc_ref)

        acc_ref[...] += lax.dot_general(a_ref[...].astype(BF16), b_ref[...].astype(BF16), dims,
                                        preferred_element_type=F32)

        @pl.when(kk == nk - 1)
        def _():
            o_ref[...] = acc_ref[...].astype(o_ref.dtype)

    a_spec = pl.BlockSpec((tk, tm), lambda i, j, kk: (kk, i)) if ta else pl.BlockSpec((tm, tk), lambda i, j, kk: (i, kk))
    b_spec = pl.BlockSpec((tn, tk), lambda i, j, kk: (j, kk)) if tb else pl.BlockSpec((tk, tn), lambda i, j, kk: (kk, j))
    return pl.pallas_call(
        body, grid=(m // tm, n // tn, nk), in_specs=[a_spec, b_spec],
        out_specs=pl.BlockSpec((tm, tn), lambda i, j, kk: (i, j)),
        out_shape=SDS((m, n), out_dtype), scratch_shapes=[pltpu.VMEM((tm, tn), F32)],
        compiler_params=_params(("parallel", "parallel", "arbitrary")), name=name)(a, b)


def _rows(fn, rows, params, out_rows, out_accs, *, name, tl, reverse=False, scratch=()):
    first = rows[0][0] if isinstance(rows[0], tuple) else rows[0]
    length = first.shape[0]
    tl = min(tl, length)
    nt = length // tl
    rmap = (lambda i: nt - 1 - i) if reverse else (lambda i: i)
    specs, arrs = [], []
    for r in rows:
        arr, wdt, cb = r if isinstance(r, tuple) else (r, r.shape[1], 0)
        specs.append(pl.BlockSpec((tl, wdt), lambda i, cb=cb: (rmap(i), cb)))
        arrs.append(arr)
    for p in params:
        specs.append(pl.BlockSpec(p.shape, lambda i, nd=p.ndim: (0,) * nd))
        arrs.append(p)
    out_shape = [SDS((length, c), dt) for c, dt in out_rows] + [SDS(s, F32) for s in out_accs]
    out_specs = [pl.BlockSpec((tl, c), lambda i: (rmap(i), 0)) for c, _ in out_rows]
    out_specs += [pl.BlockSpec(s, lambda i, nd=len(s): (0,) * nd) for s in out_accs]
    nr, npar, nor, noa = len(rows), len(params), len(out_rows), len(out_accs)

    def body(*refs):
        rin, pin = refs[:nr], refs[nr:nr + npar]
        rout = refs[nr + npar:nr + npar + nor]
        aout = refs[nr + npar + nor:nr + npar + nor + noa]
        scr = refs[nr + npar + nor + noa:]
        step = pl.program_id(0)
        outs_r, outs_a = fn(step, [r[...] for r in rin], [p[...] for p in pin], scr)
        for ref, val in zip(rout, outs_r, strict=True):
            ref[...] = val.astype(ref.dtype)

        @pl.when(step == 0)
        def _():
            for ref in aout:
                ref[...] = jnp.zeros_like(ref)

        for ref, val in zip(aout, outs_a, strict=True):
            ref[...] += val.astype(F32)

    res = pl.pallas_call(body, grid=(nt,), in_specs=specs, out_specs=out_specs, out_shape=out_shape,
                         scratch_shapes=list(scratch), compiler_params=_params(), name=name)(*arrs)
    return list(res)


def _rms(x, g):
    return x * lax.rsqrt(jnp.mean(x * x, axis=-1, keepdims=True) + NORM_EPS) * g


def _sig(x):
    return 0.5 * (jnp.tanh(0.5 * x) + 1.0)


def _softplus(x):
    return jnp.maximum(x, 0.0) + jnp.log(1.0 + jnp.exp(-jnp.abs(x)))


def _gelu(x):
    return x * (0.5 * (1.0 + jnp.tanh(math.sqrt(2.0 / math.pi) * (x + 0.044715 * (x * x * x)))))


def _bdot(a, b):
    return jnp.dot(a.astype(BF16), b.astype(BF16), preferred_element_type=F32)


def _hdot(a, b):
    return jnp.dot(a, b, precision=HI, preferred_element_type=F32)


def _segsum_impl(x, ones):
    hi = x.astype(BF16)
    lo = (x - hi.astype(F32)).astype(BF16)
    ones = ones.astype(BF16)
    return jnp.dot(jnp.concatenate([hi, lo], axis=1), jnp.concatenate([ones, ones], axis=0), preferred_element_type=F32)


@jax.custom_vjp
def _segsum(x, ones):
    return _segsum_impl(x, ones)


_segsum.defvjp(lambda x, ones: (_segsum_impl(x, ones), ones),
               lambda ones, g: (_segsum_impl(g, ones), jnp.zeros_like(ones)))


def _block_ones(n, blk):
    i = lax.broadcasted_iota(jnp.int32, (n, n), 0) // blk
    j = lax.broadcasted_iota(jnp.int32, (n, n), 1) // blk
    return (i == j).astype(F32)


def _rms_fwd(x, g, name):
    return _rows(lambda s, r, p, _: ([_rms(r[0], p[0])], []), [x], [g], [(x.shape[1], BF16)], [], name=name, tl=512)[0]


def _rms_bwd(x, g, dh, dres, name):
    def fn(s, r, p, _):
        _, vjp = jax.vjp(_rms, r[0], p[0])
        dx, dg = vjp(r[1])
        return [dx + r[2]], [dg]
    return _rows(fn, [x, dh, dres], [g], [(x.shape[1], F32)], [g.shape], name=name, tl=256)


def _rwkv_pre_math(k_, lr, w0, a0, k_k, k_a, w2p, a2p, g2p, o512):
    pre_w = w0 + _bdot(jnp.tanh(lr), w2p)
    w = -_softplus(-pre_w) - 0.5
    decay = jnp.exp(-jnp.exp(w))
    a = _sig(a0 + _bdot(lr, a2p))
    g = _bdot(_sig(lr), g2p)
    kr = k_ * k_k
    kk = kr / jnp.maximum(jnp.sqrt(_segsum(kr * kr, o512)), 1e-12)
    k2 = k_ * (1.0 + (a - 1.0) * k_a)
    return decay, k2, -kk, kk * a, g


def _shift_down(p, prev_row):
    row = lax.broadcasted_iota(jnp.int32, p.shape, 0)
    return jnp.where(row == 0, jnp.broadcast_to(prev_row, p.shape), pltpu.roll(p, 1, 0))


def _shift_up(q, next_row):
    n = q.shape[0]
    row = lax.broadcasted_iota(jnp.int32, q.shape, 0)
    return jnp.where(row == n - 1, jnp.broadcast_to(next_row, q.shape), pltpu.roll(q, n - 1, 0))


def _rwkv_pre_fwd(p, mu, small, o512):
    def fn(step, r, prm, scr):
        car = scr[0]

        @pl.when(step == 0)
        def _():
            car[...] = jnp.zeros_like(car)

        x = r[0]
        prev = _shift_down(x, car[SUBLANES - 1:SUBLANES, :])
        car[...] = x[x.shape[0] - SUBLANES:, :]
        xs = x + (prev - x) * prm[0]
        decay, k2, aa, bb, g = _rwkv_pre_math(xs[:, RW:2 * RW], xs[:, 3 * RW:], *prm[1:])
        return [xs[:, :RW], decay, k2, xs[:, 2 * RW:3 * RW], aa, bb, g], []
    return _rows(fn, [p], [mu, *small, o512], [(RW, F32)] * 7, [], name="rwkv_pre_fwd", tl=256,
                 scratch=[pltpu.VMEM((SUBLANES, N_RWKV), F32)])


def _rwkv_pre_bwd(p, cots, mu, small, o512):
    length = p.shape[0]
    tl = min(256, length)
    nt = length // tl
    rows_per = tl // SUBLANES
    params = [mu, *small, o512]
    acc_shapes = [mu.shape] + [q.shape for q in small]
    nr, npar, nacc = 2 + len(cots), len(params), len(acc_shapes)

    def body(*refs):
        rin, pin = refs[:nr], refs[nr:nr + npar]
        dp_ref = refs[nr + npar]
        aout = refs[nr + npar + 1:nr + npar + 1 + nacc]
        car_q = refs[nr + npar + 1 + nacc]
        step = pl.program_id(0)

        @pl.when(step == 0)
        def _():
            car_q[...] = jnp.zeros_like(car_q)
            for ref in aout:
                ref[...] = jnp.zeros_like(ref)

        x = rin[0][...]
        prev_row = jnp.where(step == nt - 1, 0.0, rin[1][SUBLANES - 1:SUBLANES, :])
        dr, ddecay, dk2, dv, daa, dbb, dg = [r[...] for r in rin[2:]]
        prm = [q[...] for q in pin]
        mu_, o512_ = prm[0], prm[-1]
        prev = _shift_down(x, prev_row)
        xs = x + (prev - x) * mu_
        _, vjp = jax.vjp(lambda k_, lr, *w: _rwkv_pre_math(k_, lr, *w, o512_), xs[:, RW:2 * RW], xs[:, 3 * RW:], *prm[1:-1])
        dk_, dlr, *dsmall = vjp((ddecay, dk2, daa, dbb, dg))
        dxs = jnp.concatenate([dr, dk_, dv, dlr], axis=1)
        q = dxs * mu_
        dp_ref[...] = (dxs - q + _shift_up(q, car_q[0:1, :])).astype(dp_ref.dtype)
        car_q[...] = q[:SUBLANES, :]
        aout[0][...] += jnp.sum((prev - x) * dxs, axis=0, keepdims=True)
        for ref, val in zip(aout[1:], dsmall, strict=True):
            ref[...] += val

    rmap = lambda i: nt - 1 - i
    specs = [pl.BlockSpec((tl, N_RWKV), lambda i: (rmap(i), 0)),
             pl.BlockSpec((SUBLANES, N_RWKV), lambda i: (jnp.maximum(rmap(i) * rows_per - 1, 0), 0))]
    specs += [pl.BlockSpec((tl, RW), lambda i: (rmap(i), 0)) for _ in cots]
    specs += [pl.BlockSpec(q.shape, lambda i, nd=q.ndim: (0,) * nd) for q in params]
    out_shape = [SDS((length, N_RWKV), BF16)] + [SDS(sh, F32) for sh in acc_shapes]
    out_specs = [pl.BlockSpec((tl, N_RWKV), lambda i: (rmap(i), 0))]
    out_specs += [pl.BlockSpec(sh, lambda i, nd=len(sh): (0,) * nd) for sh in acc_shapes]
    res = pl.pallas_call(body, grid=(nt,), in_specs=specs, out_specs=out_specs, out_shape=out_shape,
                         scratch_shapes=[pltpu.VMEM((SUBLANES, N_RWKV), F32)],
                         compiler_params=_params(), name="rwkv_pre_bwd")(p, p, *cots, *params)
    return list(res)


def _rwkv_post_math(y, r, k2, v, g, lnx_w, lnx_b, r_k, o512):
    mean = _segsum(y, o512) * (1.0 / HN)
    yc = y - mean
    var = _segsum(yc * yc, o512) * (1.0 / HN)
    yn = yc * lax.rsqrt(var + LNX_EPS) * lnx_w + lnx_b
    bonus = _segsum(r * k2 * r_k, o512) * v
    return (yn + bonus) * g


def _rwkv_post_fwd(y, r, k2, v, g, prm, o512):
    return _rows(lambda s, rr, p, _: ([_rwkv_post_math(*rr, *p)], []), [y, r, k2, v, g], [*prm, o512],
                 [(RW, BF16)], [], name="rwkv_post_fwd", tl=256)[0]


def _rwkv_post_bwd(y, r, k2, v, g, dout, prm, o512):
    def fn(s, rr, p, _):
        o = p[-1]
        _, vjp = jax.vjp(lambda *a: _rwkv_post_math(*a, o), *rr[:5], *p[:-1])
        gr = vjp(rr[5])
        return list(gr[:5]), list(gr[5:])
    return _rows(fn, [y, r, k2, v, g, dout], [*prm, o512], [(RW, F32)] * 5, [q.shape for q in prm],
                 name="rwkv_post_bwd", tl=256)


def _glu_math(ys, t, b):
    return ys * _sig(t + b)


def _glu_fwd(ys, t, b):
    return _rows(lambda s, r, p, _: ([_glu_math(r[0], r[1], p[0])], []), [ys, t], [b], [(S5W, BF16)], [],
                 name="s5_glu_fwd", tl=512)[0]


def _glu_bwd(ys, t, dout, b):
    def fn(s, r, p, _):
        _, vjp = jax.vjp(_glu_math, r[0], r[1], p[0])
        dys, dt, db = vjp(r[2])
        return [dys, dt], [db]
    return _rows(fn, [ys, t, dout], [b], [(S5W, F32), (S5W, BF16)], [b.shape], name="s5_glu_bwd", tl=512)


def _merge_math(gp_r, gp_s, o_r, o_s, b_r, b_s):
    return _sig(gp_r + b_r) * o_r + _sig(gp_s + b_s) * o_s


def _merge_fwd(gpre, o_r, o_s, b_r, b_s):
    return _rows(lambda s, r, p, _: ([_merge_math(*r, *p)], []),
                 [(gpre, D_MODEL, 0), (gpre, D_MODEL, 1), o_r, o_s], [b_r, b_s], [(D_MODEL, BF16)], [],
                 name="merge_fwd", tl=256)[0]


def _merge_bwd(gpre, o_r, o_s, dmi, b_r, b_s):
    def fn(s, r, p, _):
        _, vjp = jax.vjp(_merge_math, *r[:4], *p)
        dgr, dgs, dor, dos, dbr, dbs = vjp(r[4])
        return [dgr, dgs, dor, dos], [dbr, dbs]
    return _rows(fn, [(gpre, D_MODEL, 0), (gpre, D_MODEL, 1), o_r, o_s, dmi], [b_r, b_s], [(D_MODEL, BF16)] * 4,
                 [b_r.shape, b_s.shape], name="merge_bwd", tl=256)


def _mid_fwd(x, mixed, g_post, g_pre):
    def fn(s, r, p, _):
        x2 = r[0] + _rms(r[1], p[0])
        return [x2, _rms(x2, p[1])], []
    return _rows(fn, [x, mixed], [g_post, g_pre], [(D_MODEL, F32), (D_MODEL, BF16)], [], name="mid_fwd", tl=256)


def _mid_bwd(x2, mixed, dh2, dx3, g_post, g_pre):
    def fn(s, r, p, _):
        _, vjp1 = jax.vjp(_rms, r[0], p[1])
        dx2, dg_pre = vjp1(r[2])
        dx2 = dx2 + r[3]
        _, vjp2 = jax.vjp(_rms, r[1], p[0])
        dmixed, dg_post = vjp2(dx2)
        return [dx2, dmixed], [dg_post, dg_pre]
    return _rows(fn, [x2, mixed, dh2, dx3], [g_post, g_pre], [(D_MODEL, F32), (D_MODEL, BF16)], [g_post.shape, g_pre.shape],
                 name="mid_bwd", tl=256)


def _final(x2, f, tgt, g_post):
    def fn(s, r, p, _):
        y, vjp = jax.vjp(_rms, r[1], p[0])
        diff = r[0] + y - r[2]
        dx3 = diff * (1.0 / D_MODEL)
        df, dg = vjp(dx3)
        part = 0.5 * jnp.sum(jnp.sum(diff * diff, axis=1, keepdims=True), axis=0, keepdims=True) * (1.0 / D_MODEL)
        return [dx3, df], [dg, jnp.broadcast_to(part, (1, LANES))]
    return _rows(fn, [x2, f, tgt], [g_post], [(D_MODEL, F32), (D_MODEL, BF16)], [g_post.shape, (1, LANES)], name="final", tl=256)


def _conv_taps(z, car):
    row = lax.broadcasted_iota(jnp.int32, z.shape, 0)
    z1 = jnp.where(row == 0, jnp.broadcast_to(car[7:8, :], z.shape), pltpu.roll(z, 1, 0))
    z2 = pltpu.roll(z, 2, 0)
    z2 = jnp.where(row == 0, jnp.broadcast_to(car[6:7, :], z.shape), z2)
    z2 = jnp.where(row == 1, jnp.broadcast_to(car[7:8, :], z.shape), z2)
    return z1, z2


def _conv(z, car, w, b):
    z1, z2 = _conv_taps(z, car)
    return b + w[0:1, :] * z2 + w[1:2, :] * z1 + w[2:3, :] * z, z1, z2


def _conv_fwd(z, conv_w, conv_b):
    length = z.shape[0]
    tl = min(256, length)
    nt = length // tl
    tc = _pick(D_FF, 1536)
    nb = D_FF // tc

    def body(zg_ref, zv_ref, wg_ref, wv_ref, bg_ref, bv_ref, o_ref, cg, cv):
        @pl.when(pl.program_id(1) == 0)
        def _():
            cg[...] = jnp.zeros_like(cg)
            cv[...] = jnp.zeros_like(cv)

        zg, zv = zg_ref[...], zv_ref[...]
        gate, _, _ = _conv(zg, cg[...], wg_ref[...], bg_ref[...])
        val, _, _ = _conv(zv, cv[...], wv_ref[...], bv_ref[...])
        cg[...] = zg[tl - SUBLANES:, :]
        cv[...] = zv[tl - SUBLANES:, :]
        o_ref[...] = (_gelu(gate) * val).astype(o_ref.dtype)

    zspec = lambda off: pl.BlockSpec((tl, tc), lambda j, i: (i, j + off))
    wspec = lambda off, r: pl.BlockSpec((r, tc), lambda j, i: (0, j + off))
    return pl.pallas_call(
        body, grid=(nb, nt),
        in_specs=[zspec(0), zspec(nb), wspec(0, 3), wspec(nb, 3), wspec(0, 1), wspec(nb, 1)],
        out_specs=pl.BlockSpec((tl, tc), lambda j, i: (i, j)), out_shape=SDS((length, D_FF), BF16),
        scratch_shapes=[pltpu.VMEM((SUBLANES, tc), F32)] * 2,
        compiler_params=_params(("arbitrary", "arbitrary")), name="conv_fwd")(z, z, conv_w, conv_w, conv_b, conv_b)


def _conv_bwd(z, dact, conv_w, conv_b):
    length = z.shape[0]
    tl = min(256, length)
    nt = length // tl
    tc = _pick(D_FF, 1536)
    nb = D_FF // tc
    rows_per = tl // SUBLANES

    def half_bwd(dzc, z, z1, z2, w, dcar):
        n = tl
        row = lax.broadcasted_iota(jnp.int32, dzc.shape, 0)
        u1 = jnp.where(row == n - 1, jnp.broadcast_to(dcar[0:1, :], dzc.shape), pltpu.roll(dzc, n - 1, 0))
        u2 = pltpu.roll(dzc, n - 2, 0)
        u2 = jnp.where(row == n - 2, jnp.broadcast_to(dcar[0:1, :], dzc.shape), u2)
        u2 = jnp.where(row == n - 1, jnp.broadcast_to(dcar[1:2, :], dzc.shape), u2)
        dz = w[2:3, :] * dzc + w[1:2, :] * u1 + w[0:1, :] * u2
        dw = jnp.concatenate([jnp.sum(dzc * z2, axis=0, keepdims=True), jnp.sum(dzc * z1, axis=0, keepdims=True),
                              jnp.sum(dzc * z, axis=0, keepdims=True)], axis=0)
        return dz, dw, jnp.sum(dzc, axis=0, keepdims=True)

    def body(zg_ref, zv_ref, pg_ref, pv_ref, da_ref, wg_ref, wv_ref, bg_ref, bv_ref,
             dzg_ref, dzv_ref, dwg_ref, dwv_ref, dbg_ref, dbv_ref, cg, cv):
        step = pl.program_id(1)

        @pl.when(step == 0)
        def _():
            cg[...] = jnp.zeros_like(cg)
            cv[...] = jnp.zeros_like(cv)
            for ref in (dwg_ref, dwv_ref, dbg_ref, dbv_ref):
                ref[...] = jnp.zeros_like(ref)

        is_first_tile = step == nt - 1
        zg, zv = zg_ref[...], zv_ref[...]
        pg = jnp.where(is_first_tile, 0.0, pg_ref[...])
        pv = jnp.where(is_first_tile, 0.0, pv_ref[...])
        wg, wv = wg_ref[...], wv_ref[...]
        gate, zg1, zg2 = _conv(zg, pg, wg, bg_ref[...])
        val, zv1, zv2 = _conv(zv, pv, wv, bv_ref[...])
        act_g, vjp = jax.vjp(_gelu, gate)
        da = da_ref[...]
        dgate = vjp(da * val)[0]
        dval = da * act_g
        dzg, dwg, dbg = half_bwd(dgate, zg, zg1, zg2, wg, cg[...])
        dzv, dwv, dbv = half_bwd(dval, zv, zv1, zv2, wv, cv[...])
        cg[...] = dgate[:SUBLANES, :]
        cv[...] = dval[:SUBLANES, :]
        dzg_ref[...] = dzg.astype(dzg_ref.dtype)
        dzv_ref[...] = dzv.astype(dzv_ref.dtype)
        dwg_ref[...] += dwg
        dwv_ref[...] += dwv
        dbg_ref[...] += dbg
        dbv_ref[...] += dbv

    rmap = lambda i: nt - 1 - i
    zspec = lambda off: pl.BlockSpec((tl, tc), lambda j, i: (rmap(i), j + off))
    pspec = lambda off: pl.BlockSpec((SUBLANES, tc), lambda j, i: (jnp.maximum(rmap(i) * rows_per - 1, 0), j + off))
    wspec = lambda off, r: pl.BlockSpec((r, tc), lambda j, i: (0, j + off))
    out_w = lambda r: pl.BlockSpec((r, tc), lambda j, i: (0, j))
    dzg, dzv, dwg, dwv, dbg, dbv = pl.pallas_call(
        body, grid=(nb, nt),
        in_specs=[zspec(0), zspec(nb), pspec(0), pspec(nb), pl.BlockSpec((tl, tc), lambda j, i: (rmap(i), j)),
                  wspec(0, 3), wspec(nb, 3), wspec(0, 1), wspec(nb, 1)],
        out_specs=[pl.BlockSpec((tl, tc), lambda j, i: (rmap(i), j))] * 2 + [out_w(3), out_w(3), out_w(1), out_w(1)],
        out_shape=[SDS((length, D_FF), BF16)] * 2 + [SDS((3, D_FF), F32)] * 2 + [SDS((1, D_FF), F32)] * 2,
        scratch_shapes=[pltpu.VMEM((SUBLANES, tc), F32)] * 2,
        compiler_params=_params(("arbitrary", "arbitrary")), name="conv_bwd")(z, z, z, z, dact, conv_w, conv_w, conv_b, conv_b)
    return dzg, dzv, jnp.concatenate([dwg, dwv], axis=1), jnp.concatenate([dbg, dbv], axis=1)


def _s5_prep_math(a_re, a_im, ls, b_re, b_im):
    dt = jnp.exp(ls)
    er = jnp.exp(a_re * dt)
    ph = a_im * dt
    abr, abi = er * jnp.cos(ph), er * jnp.sin(ph)
    den = a_re * a_re + a_im * a_im
    nr = abr - 1.0
    cr = (nr * a_re + abi * a_im) / den
    ci = (abi * a_re - nr * a_im) / den
    return abr, abi, cr * b_re - ci * b_im, cr * b_im + ci * b_re


def _s5_prep_fwd(a_re, a_im, ls, b_re, b_im):
    def body(ar, ai, l, br, bi, o1, o2, o3, o4):
        for ref, val in zip((o1, o2, o3, o4), _s5_prep_math(ar[...], ai[...], l[...], br[...], bi[...]), strict=True):
            ref[...] = val
    return pl.pallas_call(body, out_shape=[SDS((1, S5N), F32)] * 2 + [SDS((S5C, S5N), F32)] * 2,
                          name="s5_prep_fwd")(a_re, a_im, ls, b_re, b_im)


def _s5_prep_bwd(a_re, a_im, ls, b_re, b_im, cots, gsel):
    def body(ar, ai, l, br, bi, c1, c2, c3, c4, g_ref, o1, o2, o3, o4, o5):
        _, vjp = jax.vjp(_s5_prep_math, ar[...], ai[...], l[...], br[...], bi[...])
        d_ar, d_ai, d_ls, d_br, d_bi = vjp((c1[...], c2[...], c3[...], c4[...]))
        o1[...] = d_ar
        o2[...] = d_ai
        o3[...] = _hdot(jnp.broadcast_to(d_ls, (SUBLANES, S5N)), g_ref[...])
        o4[...] = d_br
        o5[...] = d_bi
    return pl.pallas_call(body, out_shape=[SDS((1, S5N), F32)] * 2 + [SDS((SUBLANES, LANES), F32)] + [SDS((S5C, S5N), F32)] * 2,
                          name="s5_prep_bwd")(a_re, a_im, ls, b_re, b_im, *cots, gsel)


def _cmul(ar, ai, br, bi):
    return ar * br - ai * bi, ar * bi + ai * br


def _s5_powers(abr, abi):
    shp = (SUBLANES, S5N)
    a1 = (jnp.broadcast_to(abr, shp), jnp.broadcast_to(abi, shp))
    a2 = _cmul(*a1, *a1)
    a4 = _cmul(*a2, *a2)
    row = lax.broadcasted_iota(jnp.int32, shp, 0)
    pr, pi = a1
    cur = a1
    for i in range(1, SUBLANES):
        cur = _cmul(*cur, *a1)
        pr = jnp.where(row == i, cur[0], pr)
        pi = jnp.where(row == i, cur[1], pi)
    return a1, a2, a4, (pr, pi)


def _s5_scan(sre_ref, sim_ref, car_re, car_im, abr, abi, n_rows, reverse):
    a1, a2, a4, (pr, pi) = _s5_powers(abr, abi)
    sgn = -1.0 if reverse else 1.0
    row = lax.broadcasted_iota(jnp.int32, (SUBLANES, S5N), 0)
    if reverse:
        qr, qi = pr, pi
        for i in range(SUBLANES):
            src = SUBLANES - 1 - i
            qr = jnp.where(row == i, jnp.broadcast_to(pr[src:src + 1, :], pr.shape), qr)
            qi = jnp.where(row == i, jnp.broadcast_to(pi[src:src + 1, :], pi.shape), qi)
        pr, pi = qr, qi
    nblk = n_rows // SUBLANES

    def blk(i, carry):
        cr, ci = carry
        b = (nblk - 1 - i) if reverse else i
        sl = pl.ds(pl.multiple_of(b * SUBLANES, SUBLANES), SUBLANES)
        xr, xi = sre_ref[sl, :], sim_ref[sl, :]
        for kk, (er, ei) in ((1, a1), (2, a2), (4, a4)):
            if reverse:
                sr = jnp.where(row < SUBLANES - kk, pltpu.roll(xr, SUBLANES - kk, 0), 0.0)
                si = jnp.where(row < SUBLANES - kk, pltpu.roll(xi, SUBLANES - kk, 0), 0.0)
            else:
                sr = jnp.where(row >= kk, pltpu.roll(xr, kk, 0), 0.0)
                si = jnp.where(row >= kk, pltpu.roll(xi, kk, 0), 0.0)
            dr, di = _cmul(er, sgn * ei, sr, si)
            xr, xi = xr + dr, xi + di
        dr, di = _cmul(pr, sgn * pi, cr, ci)
        xr, xi = xr + dr, xi + di
        sre_ref[sl, :] = xr
        sim_ref[sl, :] = xi
        edge = 0 if reverse else SUBLANES - 1
        return (jnp.broadcast_to(xr[edge:edge + 1, :], xr.shape), jnp.broadcast_to(xi[edge:edge + 1, :], xi.shape))

    cr, ci = lax.fori_loop(0, nblk, blk, (car_re[...], car_im[...]))
    car_re[...] = cr
    car_im[...] = ci


S5_BLK = 4
S5_BN = S5N // S5_BLK


def _split2(x):
    hi = x.astype(BF16)
    return hi, (x - hi.astype(F32)).astype(BF16)


def _dot3(a, b, ca, cb):
    ah, al = _split2(a)
    bh, bl = _split2(b)
    return lax.dot_general(jnp.concatenate([ah, ah, al], axis=ca), jnp.concatenate([bh, bl, bh], axis=cb),
                           (((ca,), (cb,)), ((), ())), preferred_element_type=F32)


def _dot1(a, b, ca, cb):
    return lax.dot_general(a.astype(BF16), b.astype(BF16), (((ca,), (cb,)), ((), ())), preferred_element_type=F32)


def _s5_project_in(u, bblk_ref, sre_ref, sim_ref):
    for j in range(S5_BLK):
        bu = _dot1(u[:, j * LANES:(j + 1) * LANES], bblk_ref[j], 1, 0)
        sre_ref[:, j * S5_BN:(j + 1) * S5_BN] = bu[:, :S5_BN]
        sim_ref[:, j * S5_BN:(j + 1) * S5_BN] = bu[:, S5_BN:]


def _s5_project_out(u, d, cblk_ref, sre_ref, sim_ref):
    ys = []
    for j in range(S5_BLK):
        sl = slice(j * S5_BN, (j + 1) * S5_BN)
        ys.append(_dot1(sre_ref[:, sl], cblk_ref[j, :S5_BN, :], 1, 0) + _dot1(sim_ref[:, sl], cblk_ref[j, S5_BN:, :], 1, 0))
    return jnp.concatenate(ys, axis=1) + d * u


def _s5_fwd(u, bblk, cblk, abr, abi, d):
    length = u.shape[0]
    tl = min(256, length)
    nt = length // tl

    def body(u_ref, b_ref, c_ref, ar_ref, ai_ref, d_ref, ys_ref, kr_ref, ki_ref, sre, sim, car_re, car_im):
        @pl.when(pl.program_id(0) == 0)
        def _():
            car_re[...] = jnp.zeros_like(car_re)
            car_im[...] = jnp.zeros_like(car_im)

        kr_ref[0] = car_re[...]
        ki_ref[0] = car_im[...]
        u_ = u_ref[...]
        _s5_project_in(u_, b_ref, sre, sim)
        _s5_scan(sre, sim, car_re, car_im, ar_ref[...], ai_ref[...], tl, False)
        ys_ref[...] = _gelu(_s5_project_out(u_, d_ref[...], c_ref, sre, sim))

    full = lambda a: pl.BlockSpec(a.shape, lambda i, nd=a.ndim: (0,) * nd)
    chk = pl.BlockSpec((1, SUBLANES, S5N), lambda i: (i, 0, 0))
    return pl.pallas_call(
        body, grid=(nt,), in_specs=[pl.BlockSpec((tl, S5W), lambda i: (i, 0)), full(bblk), full(cblk), full(abr), full(abi), full(d)],
        out_specs=[pl.BlockSpec((tl, S5W), lambda i: (i, 0)), chk, chk],
        out_shape=[SDS((length, S5W), F32), SDS((nt, SUBLANES, S5N), F32), SDS((nt, SUBLANES, S5N), F32)],
        scratch_shapes=[pltpu.VMEM((tl, S5N), F32)] * 2 + [pltpu.VMEM((SUBLANES, S5N), F32)] * 2,
        compiler_params=_params(), name="s5_fwd")(u, bblk, cblk, abr, abi, d)


def _s5_bwd(u, dys_a, dys_b, kre, kim, bblk, cblk, abr, abi, d):
    length = u.shape[0]
    tl = min(256, length)
    nt = length // tl

    def body(u_ref, da_ref, db_ref, kr_ref, ki_ref, b_ref, c_ref, ar_ref, ai_ref, d_ref,
             du_ref, dB_ref, dC_ref, dar_ref, dai_ref, dd_ref, sre, sim, gre, gim, car_re, car_im, dcar_re, dcar_im):
        @pl.when(pl.program_id(0) == 0)
        def _():
            dcar_re[...] = jnp.zeros_like(dcar_re)
            dcar_im[...] = jnp.zeros_like(dcar_im)
            for ref in (dB_ref, dC_ref, dar_ref, dai_ref, dd_ref):
                ref[...] = jnp.zeros_like(ref)

        u_ = u_ref[...]
        abr_, abi_, d_ = ar_ref[...], ai_ref[...], d_ref[...]
        car_re[...] = kr_ref[0]
        car_im[...] = ki_ref[0]
        _s5_project_in(u_, b_ref, sre, sim)
        _s5_scan(sre, sim, car_re, car_im, abr_, abi_, tl, False)
        y = _s5_project_out(u_, d_, c_ref, sre, sim)
        _, vjp = jax.vjp(_gelu, y)
        dy = vjp(da_ref[...] + db_ref[...])[0]
        dd_ref[...] += jnp.sum(dy * u_, axis=0, keepdims=True)
        for j in range(S5_BLK):
            sl = slice(j * S5_BN, (j + 1) * S5_BN)
            dyj = dy[:, j * LANES:(j + 1) * LANES]
            gre[:, sl] = _dot1(dyj, c_ref[j, :S5_BN, :], 1, 1)
            gim[:, sl] = _dot1(dyj, c_ref[j, S5_BN:, :], 1, 1)
            dC_ref[j, :S5_BN, :] += _dot1(sre[:, sl], dyj, 0, 0)
            dC_ref[j, S5_BN:, :] += _dot1(sim[:, sl], dyj, 0, 0)
        _s5_scan(gre, gim, dcar_re, dcar_im, abr_, abi_, tl, True)
        gr, gi = gre[...], gim[...]
        pr = _shift_down(sre[...], kr_ref[0, 0:1, :])
        pi = _shift_down(sim[...], ki_ref[0, 0:1, :])
        dar_ref[...] += jnp.sum(gr * pr + gi * pi, axis=0, keepdims=True)
        dai_ref[...] += jnp.sum(gi * pr - gr * pi, axis=0, keepdims=True)
        dus = []
        for j in range(S5_BLK):
            sl = slice(j * S5_BN, (j + 1) * S5_BN)
            uj = u_[:, j * LANES:(j + 1) * LANES]
            dus.append(_dot1(gre[:, sl], b_ref[j, :, :S5_BN], 1, 1) + _dot1(gim[:, sl], b_ref[j, :, S5_BN:], 1, 1))
            dB_ref[j, :, :S5_BN] += _dot1(uj, gre[:, sl], 0, 0)
            dB_ref[j, :, S5_BN:] += _dot1(uj, gim[:, sl], 0, 0)
        du_ref[...] = (jnp.concatenate(dus, axis=1) + d_ * dy).astype(du_ref.dtype)

    rmap = lambda i: nt - 1 - i
    full = lambda a: pl.BlockSpec(a.shape, lambda i, nd=a.ndim: (0,) * nd)
    row = pl.BlockSpec((tl, S5W), lambda i: (rmap(i), 0))
    chk = pl.BlockSpec((1, SUBLANES, S5N), lambda i: (rmap(i), 0, 0))
    return pl.pallas_call(
        body, grid=(nt,), in_specs=[row, row, row, chk, chk, full(bblk), full(cblk), full(abr), full(abi), full(d)],
        out_specs=[row, full(bblk), full(cblk), full(abr), full(abi), full(d)],
        out_shape=[SDS((length, S5W), BF16), SDS(bblk.shape, F32), SDS(cblk.shape, F32), SDS(abr.shape, F32),
                   SDS(abi.shape, F32), SDS(d.shape, F32)],
        scratch_shapes=[pltpu.VMEM((tl, S5N), F32)] * 4 + [pltpu.VMEM((SUBLANES, S5N), F32)] * 4,
        compiler_params=_params(), name="s5_bwd")(u, dys_a, dys_b, kre, kim, bblk, cblk, abr, abi, d)


def _s5_blockdiag_in(bb):
    t = bb.reshape(S5C, S5_BLK, S5_BN).transpose(1, 0, 2)
    t = jnp.tile(t, (1, SUBLANES, 1))
    mask = (lax.broadcasted_iota(jnp.int32, (LANES, S5_BN), 0) // S5C) == (lax.broadcasted_iota(jnp.int32, (LANES, S5_BN), 1) // S5P)
    return jnp.where(mask[None], t, 0.0)


def _s5_blockdiag_in_t(dblk):
    t = dblk.reshape(S5_BLK, SUBLANES, S5C, SUBLANES, S5P)
    t = jnp.diagonal(t, axis1=1, axis2=3)
    return t.transpose(1, 0, 3, 2).reshape(S5C, S5N)


def _s5_blockdiag_out(c):
    t = c.reshape(S5_BLK, SUBLANES, S5C, S5P).transpose(0, 3, 1, 2).reshape(S5_BLK, S5P, LANES)
    t = jnp.tile(t, (1, SUBLANES, 1))
    mask = (lax.broadcasted_iota(jnp.int32, (S5_BN, LANES), 0) // S5P) == (lax.broadcasted_iota(jnp.int32, (S5_BN, LANES), 1) // S5C)
    return jnp.where(mask[None], t, 0.0)


def _s5_blockdiag_out_t(dblk):
    t = dblk.reshape(S5_BLK, SUBLANES, S5P, SUBLANES, S5C)
    t = jnp.diagonal(t, axis1=1, axis2=3)
    return t.transpose(0, 3, 2, 1).reshape(S5G, S5C, S5P)


def _wkv_consts():
    lane = lax.broadcasted_iota(jnp.int32, (4 * HN, LANES), 1)
    row = lax.broadcasted_iota(jnp.int32, (4 * HN, LANES), 0)
    diag = ((lane % HN) == (row % HN)).astype(F32)
    ones = _block_ones(LANES, HN).astype(BF16)
    return lane % HN, diag, jnp.concatenate([ones, ones], axis=0)


def _wkv_split(p):
    hi = p.astype(BF16)
    lo = (p - hi.astype(F32)).astype(BF16)
    return jnp.concatenate([hi, lo], axis=1)


def _wkv_rowseg(p, ones):
    return jnp.dot(_wkv_split(p), ones, preferred_element_type=F32)


def _wkv_full(rows_ref, t):
    q = jnp.broadcast_to(rows_ref[t], (SUBLANES, RW))
    return jnp.concatenate([jnp.tile(q[:, hp * LANES:(hp + 1) * LANES], (SUBLANES, 1)) for hp in range(4)], axis=0)


def _wkv_colsum8(x):
    return jnp.concatenate([x[hp * HN:(hp + 1) * HN].reshape(SUBLANES, SUBLANES, LANES).sum(axis=0) for hp in range(4)], axis=1)


def _wkv_prebroadcast(src_refs, dst_refs, n):
    for s, dref in zip(src_refs, dst_refs, strict=True):
        dref[...] = s[...].reshape(n, 1, RW)


def _wkv_pair_sums(lhs_ref, out_ref):
    n, ns = out_ref.shape[0], out_ref.shape[1]
    ones_pair = _block_ones(2 * LANES, HN).astype(BF16)
    out_ref[...] = jnp.dot(lhs_ref[0:n].reshape(n * ns, 2 * LANES), ones_pair,
                           preferred_element_type=F32).reshape(n, ns, 2 * LANES)


def _wkv_transpose_out(acc, c, diag):
    lane = lax.broadcasted_iota(jnp.int32, (c, LANES), 1)
    outs = []
    for hp in range(4):
        zp = lax.dot_general(acc[hp * HN:(hp + 1) * HN], diag[:HN], (((0,), (0,)), ((), ())), precision=HI,
                             preferred_element_type=F32)
        outs.append(jnp.where(lane < HN, zp[:c], zp[HN:HN + c]))
    return jnp.concatenate(outs, axis=1)


def _wkv_fwd(r, w, k, v, a, b):
    length = r.shape[0]
    c = min(WKV_CHUNK, length)
    nc = length // c
    ns = 4 * HN

    def body(r_ref, w_ref, k_ref, v_ref, a_ref, b_ref, y_ref, sst_ref, sast_ref, vst_ref, fin_ref,
             s_ref, acc_ref, lhs_ref, ysum_ref, rb, wb, kb, vb, ab, bb, wab, beb, kab):
        @pl.when(pl.program_id(0) == 0)
        def _():
            s_ref[...] = jnp.zeros_like(s_ref)

        a_next = pltpu.roll(a_ref[...], c - 1, 0)
        o512 = _block_ones(RW, HN)
        _wkv_prebroadcast((r_ref, w_ref, k_ref, v_ref, a_ref, b_ref, w_ref[...] * a_next,
                           _segsum_impl(b_ref[...] * a_next, o512), _segsum_impl(k_ref[...] * a_next, o512)),
                          (rb, wb, kb, vb, ab, bb, wab, beb, kab), c)
        lane_t, diag, ones = _wkv_consts()

        def fill_v(q, carry):
            for half in range(2):
                lhs_ref[q, :, half * LANES:(half + 1) * LANES] = (_wkv_full(vb, 2 * q + half) * diag).astype(BF16)
            return carry

        lax.fori_loop(0, c // 2, fill_v, 0)
        _wkv_pair_sums(lhs_ref, vst_ref)

        def pair(p, carry):
            t0, t1 = 2 * p, 2 * p + 1
            s0 = s_ref[...]
            both = _wkv_rowseg(jnp.concatenate([s0 * _wkv_full(ab, t0), s0 * _wkv_full(wab, t0)], axis=0), ones)
            v_pair = vst_ref[p]
            sa0, v0 = both[:ns], v_pair[:, :LANES]
            s1 = s0 * _wkv_full(wb, t0) + sa0 * _wkv_full(bb, t0) + v0 * _wkv_full(kb, t0)
            sa1 = both[ns:] + sa0 * _wkv_full(beb, t0) + v0 * _wkv_full(kab, t0)
            s2 = s1 * _wkv_full(wb, t1) + sa1 * _wkv_full(bb, t1) + v_pair[:, LANES:] * _wkv_full(kb, t1)
            sst_ref[t0] = s0
            sast_ref[t0] = sa0
            sst_ref[t1] = s1
            sast_ref[t1] = sa1
            s_ref[...] = s2
            lhs_ref[p, :, 0:LANES] = (s1 * _wkv_full(rb, t0)).astype(BF16)
            lhs_ref[p, :, LANES:2 * LANES] = (s2 * _wkv_full(rb, t1)).astype(BF16)
            return carry

        lax.fori_loop(0, c // 2, pair, 0)
        _wkv_pair_sums(lhs_ref, ysum_ref)
        acc_ref[...] = jnp.zeros_like(acc_ref)

        def gather_y(p, carry):
            both = ysum_ref[p]
            acc = jnp.where(lane_t == 2 * p, both[:, :LANES], acc_ref[...])
            acc_ref[...] = jnp.where(lane_t == 2 * p + 1, both[:, LANES:], acc)
            return carry

        lax.fori_loop(0, c // 2, gather_y, 0)
        y_ref[...] = _wkv_transpose_out(acc_ref[...], c, diag)
        fin_ref[...] = s_ref[...]

    row = pl.BlockSpec((c, RW), lambda i: (i, 0))
    st = pl.BlockSpec((c, ns, LANES), lambda i: (i, 0, 0))
    state = pltpu.VMEM((ns, LANES), F32)
    st_pair = pl.BlockSpec((c // 2, ns, 2 * LANES), lambda i: (i, 0, 0))
    return pl.pallas_call(
        body, grid=(nc,), in_specs=[row] * 6,
        out_specs=[row, st, st, st_pair, pl.BlockSpec((ns, LANES), lambda i: (0, 0))],
        out_shape=[SDS((length, RW), F32)] + [SDS((length, ns, LANES), F32)] * 2 + [SDS((length // 2, ns, 2 * LANES), F32)]
        + [SDS((ns, LANES), F32)],
        scratch_shapes=[state] * 2 + [pltpu.VMEM((c, ns, 2 * LANES), BF16), pltpu.VMEM((c // 2, ns, 2 * LANES), F32)]
        + [pltpu.VMEM((c, 1, RW), F32)] * 9,
        compiler_params=_params(), name="wkv_fwd")(r, w, k, v, a, b)


def _wkv_bwd(r, w, k, a, b, dy, sst, sast, vst, fin):
    length = r.shape[0]
    c = min(WKV_CHUNK, length)
    nc = length // c
    ns = 4 * HN

    def body(r_ref, w_ref, k_ref, a_ref, b_ref, dy_ref, sst_ref, sast_ref, vst_ref, snext_ref, fin_ref,
             dr_ref, dw_ref, dk_ref, da_ref, db_ref, dv_ref,
             ds_ref, cur_ref, acc_ref, dyst, lhs_ref, dvsum_ref, rb, wb, kb, ab, bb, dyb, wbb, alb, rhb,
             p_r, p_w, p_k, p_a, p_b):
        @pl.when(pl.program_id(0) == 0)
        def _():
            ds_ref[...] = jnp.zeros_like(ds_ref)

        b_prev = pltpu.roll(b_ref[...], 1, 0)
        o512 = _block_ones(RW, HN)
        _wkv_prebroadcast((r_ref, w_ref, k_ref, a_ref, b_ref, dy_ref, w_ref[...] * b_prev,
                           _segsum_impl(a_ref[...] * b_prev, o512), _segsum_impl(r_ref[...] * b_ref[...], o512)),
                          (rb, wb, kb, ab, bb, dyb, wbb, alb, rhb), c)
        lane_t, diag, ones = _wkv_consts()

        def fill(q, carry):
            for half in range(2):
                lhs_ref[q, :, half * LANES:(half + 1) * LANES] = (_wkv_full(dyb, 2 * q + half) * diag).astype(BF16)
            return carry

        lax.fori_loop(0, c // 2, fill, 0)
        _wkv_pair_sums(lhs_ref, dyst)
        cur_ref[...] = jnp.where(pl.program_id(0) == 0, fin_ref[...], snext_ref[0])

        def pair(p, carry):
            ta = c - 1 - 2 * p
            tb = ta - 1
            s_a, s_b, s_c = cur_ref[...], sst_ref[ta], sst_ref[tb]
            dy_pair = dyst[c // 2 - 1 - p]
            dy_a, dy_b = dy_pair[:, LANES:], dy_pair[:, :LANES]
            ds_a = ds_ref[...] + dy_a * _wkv_full(rb, ta)
            both = _wkv_rowseg(jnp.concatenate([ds_a * _wkv_full(bb, ta), ds_a * _wkv_full(wbb, ta)], axis=0), ones)
            dsa_a = both[:ns]
            ds_b = ds_a * _wkv_full(wb, ta) + dsa_a * _wkv_full(ab, ta) + dy_b * _wkv_full(rb, tb)
            dsa_b = both[ns:] + dsa_a * _wkv_full(alb, ta) + dy_b * _wkv_full(rhb, tb)
            ds_ref[...] = ds_b * _wkv_full(wb, tb) + dsa_b * _wkv_full(ab, tb)
            cur_ref[...] = s_c
            lhs_ref[p, :, 0:LANES] = (ds_a * _wkv_full(kb, ta)).astype(BF16)
            lhs_ref[p, :, LANES:2 * LANES] = (ds_b * _wkv_full(kb, tb)).astype(BF16)
            v_pair = vst_ref[c // 2 - 1 - p]
            for t, s_t, s_prev, ds, dsa, dyy, vv in ((ta, s_a, s_b, ds_a, dsa_a, dy_a, v_pair[:, LANES:]),
                                                     (tb, s_b, s_c, ds_b, dsa_b, dy_b, v_pair[:, :LANES])):
                p_r[t] = _wkv_colsum8(s_t * dyy)
                p_k[t] = _wkv_colsum8(ds * vv)
                p_b[t] = _wkv_colsum8(ds * sast_ref[t])
                p_w[t] = _wkv_colsum8(ds * s_prev)
                p_a[t] = _wkv_colsum8(s_prev * dsa)
            return carry

        lax.fori_loop(0, c // 2, pair, 0)
        _wkv_pair_sums(lhs_ref, dvsum_ref)
        acc_ref[...] = jnp.zeros_like(acc_ref)

        def gather_dv(p, carry):
            both = dvsum_ref[p]
            acc = jnp.where(lane_t == c - 1 - 2 * p, both[:, :LANES], acc_ref[...])
            acc_ref[...] = jnp.where(lane_t == c - 2 - 2 * p, both[:, LANES:], acc)
            return carry

        lax.fori_loop(0, c // 2, gather_dv, 0)
        sel = (lax.broadcasted_iota(jnp.int32, (c, c * SUBLANES), 1) // SUBLANES
               == lax.broadcasted_iota(jnp.int32, (c, c * SUBLANES), 0)).astype(F32)
        for out_ref, part in ((dr_ref, p_r), (dw_ref, p_w), (dk_ref, p_k), (da_ref, p_a), (db_ref, p_b)):
            out_ref[...] = _dot3(sel, part[...].reshape(c * SUBLANES, RW), 1, 0)
        dv_ref[...] = _wkv_transpose_out(acc_ref[...], c, diag)

    rmap = lambda i: nc - 1 - i
    row = pl.BlockSpec((c, RW), lambda i: (rmap(i), 0))
    st = pl.BlockSpec((c, ns, LANES), lambda i: (rmap(i), 0, 0))
    nxt = pl.BlockSpec((1, ns, LANES), lambda i: (jnp.minimum((rmap(i) + 1) * c, length - 1), 0, 0))
    state = pltpu.VMEM((ns, LANES), F32)
    st_pair = pl.BlockSpec((c // 2, ns, 2 * LANES), lambda i: (rmap(i), 0, 0))
    return pl.pallas_call(
        body, grid=(nc,), in_specs=[row] * 6 + [st, st, st_pair, nxt, pl.BlockSpec((ns, LANES), lambda i: (0, 0))],
        out_specs=[row] * 6, out_shape=[SDS((length, RW), F32)] * 6,
        scratch_shapes=[state] * 3 + [pltpu.VMEM((c // 2, ns, 2 * LANES), F32), pltpu.VMEM((c, ns, 2 * LANES), BF16),
                                       pltpu.VMEM((c // 2, ns, 2 * LANES), F32)]
        + [pltpu.VMEM((c, 1, RW), F32)] * 9 + [pltpu.VMEM((c, SUBLANES, RW), F32)] * 5,
        compiler_params=_params(), name="wkv_bwd")(r, w, k, a, b, dy, sst, sast, vst, sst, fin)


def _my_place():
    return lax.axis_index("x"), lax.axis_index("y"), lax.axis_index("c")


def _peer(x, y, c, k):
    return (x ^ ((k >> 2) & 1), y ^ ((k >> 1) & 1), c ^ (k & 1))


def _all_gather(shard, name):
    rows = shard.shape[0]

    def body(in_ref, out_ref, send_sems, recv_sems, local_sem):
        x, y, c = _my_place()
        me = 4 * x + 2 * y + c
        mine = pltpu.make_async_copy(in_ref, out_ref.at[me], local_sem)
        mine.start()
        sends = []
        for k in range(1, N_DEV):
            cp = pltpu.make_async_remote_copy(src_ref=in_ref, dst_ref=out_ref.at[me], send_sem=send_sems.at[k - 1],
                                              recv_sem=recv_sems.at[k - 1], device_id=_peer(x, y, c, k), device_id_type=MESH)
            cp.start()
            sends.append(cp)
        for k in range(1, N_DEV):
            px, py, pc = _peer(x, y, c, k)
            pltpu.make_async_remote_copy(src_ref=in_ref, dst_ref=out_ref.at[4 * px + 2 * py + pc], send_sem=send_sems.at[k - 1],
                                         recv_sem=recv_sems.at[k - 1], device_id=(px, py, pc), device_id_type=MESH).wait_recv()
        for cp in sends:
            cp.wait_send()
        mine.wait()

    return pl.pallas_call(
        body, out_shape=SDS((N_DEV, rows, LANES), shard.dtype),
        in_specs=[pl.BlockSpec(memory_space=pl.ANY)], out_specs=pl.BlockSpec(memory_space=pl.ANY),
        scratch_shapes=[pltpu.SemaphoreType.DMA((N_DEV - 1,)), pltpu.SemaphoreType.DMA((N_DEV - 1,)), pltpu.SemaphoreType.DMA],
        name=name)(shard)


_HBM_SPEC = pl.BlockSpec(memory_space=pltpu.HBM)
_SEM_SPEC = pl.BlockSpec(memory_space=pltpu.SEMAPHORE)
_DATAFLOW = pltpu.SideEffectType.DATAFLOW_SIDE_EFFECTING


def _exchange_start(srcs, per_peer, name):
    n = len(srcs)
    lands = [jnp.zeros((N_DEV, s.shape[-2], LANES), s.dtype) for s in srcs]

    def body(*refs):
        src_refs, land_refs = refs[:n], refs[n:2 * n]
        send_sems, recv_sems, token = refs[2 * n], refs[2 * n + 1], refs[-1]
        x, y, c = _my_place()
        me = 4 * x + 2 * y + c
        for a in range(n):
            for k in range(1, N_DEV):
                px, py, pc = _peer(x, y, c, k)
                mine = src_refs[a].at[4 * px + 2 * py + pc] if per_peer else src_refs[a]
                pltpu.make_async_remote_copy(src_ref=mine, dst_ref=land_refs[a].at[me], send_sem=send_sems.at[7 * a + k - 1],
                                             recv_sem=recv_sems.at[7 * a + k - 1], device_id=(px, py, pc),
                                             device_id_type=MESH).start()
        token[...] = jnp.zeros_like(token)

    n_sem = (N_DEV - 1) * n
    res = pl.pallas_call(
        body, name=name,
        out_shape=(pltpu.SemaphoreType.DMA((n_sem,)), pltpu.SemaphoreType.DMA((n_sem,)),
                   *[pltpu.HBM(a.shape, a.dtype) for a in srcs + lands], SDS((SUBLANES, LANES), F32)),
        in_specs=(_HBM_SPEC,) * (2 * n),
        out_specs=(_SEM_SPEC, _SEM_SPEC) + (_HBM_SPEC,) * (2 * n) + (pl.BlockSpec(memory_space=pltpu.VMEM),),
        input_output_aliases={i: 2 + i for i in range(2 * n)}, compiler_params=pltpu.CompilerParams(has_side_effects=_DATAFLOW),
    )(*[pltpu.with_memory_space_constraint(a, pltpu.HBM) for a in srcs + lands])
    return res[0], res[1], list(res[2:2 + n]), list(res[2 + n:2 + 2 * n]), res[-1]


def _exchange_wait(started, after, per_peer, name):
    send_sems, recv_sems, src_thrus, land_thrus, _ = started
    n = len(src_thrus)

    def body(*refs):
        src_refs, land_refs = refs[:n], refs[n:2 * n]
        send_sems, recv_sems = refs[2 * n], refs[2 * n + 1]
        x, y, c = _my_place()
        me = 4 * x + 2 * y + c
        for a in range(n):
            for k in range(1, N_DEV):
                px, py, pc = _peer(x, y, c, k)
                mine = src_refs[a].at[me] if per_peer else src_refs[a]
                copy = pltpu.make_async_remote_copy(src_ref=mine, dst_ref=land_refs[a].at[4 * px + 2 * py + pc],
                                                    send_sem=send_sems.at[7 * a + k - 1], recv_sem=recv_sems.at[7 * a + k - 1],
                                                    device_id=(px, py, pc), device_id_type=MESH)
                copy.wait_send()
                copy.wait_recv()

    res = pl.pallas_call(
        body, name=name, out_shape=tuple(pltpu.HBM(a.shape, a.dtype) for a in src_thrus + land_thrus),
        in_specs=(_HBM_SPEC,) * (2 * n) + (_SEM_SPEC, _SEM_SPEC, pl.BlockSpec(memory_space=pl.ANY)),
        out_specs=(_HBM_SPEC,) * (2 * n), input_output_aliases={i: i for i in range(2 * n)},
        compiler_params=pltpu.CompilerParams(has_side_effects=_DATAFLOW),
    )(*src_thrus, *land_thrus, send_sems, recv_sems, after)
    return list(res[n:])


SUM_ROWS = 256


def _sum_parts(parts, name, own=None):
    rows = parts.shape[1]

    def body(*refs):
        p_ref, g_out = refs[0], refs[-1]
        g = p_ref[0].astype(F32)
        for s in range(1, N_DEV):
            g = g + p_ref[s].astype(F32)
        if own is not None:
            g = g + refs[1][...].astype(F32)
        g_out[...] = g

    tr = min(SUM_ROWS, rows)
    blk = pl.BlockSpec((tr, LANES), lambda i: (i, 0))
    extra = [] if own is None else [own]
    return pl.pallas_call(
        body, grid=(rows // tr,),
        in_specs=[pl.BlockSpec((N_DEV, tr, LANES), lambda i: (0, i, 0))] + [blk] * len(extra),
        out_specs=blk, out_shape=SDS((rows, LANES), F32),
        compiler_params=_params(("parallel",)), name=name)(parts, *extra)


def _adamw(g, wgt, m, v):
    rows = wgt.shape[0]

    def body(g_ref, w_ref, m_ref, v_ref, d_out, m_out, v_out):
        g_ = g_ref[...]
        m_new = ADAM_B1 * m_ref[...] + (1.0 - ADAM_B1) * g_
        v_new = ADAM_B2 * v_ref[...] + (1.0 - ADAM_B2) * (g_ * g_)
        m_hat = m_new / (1.0 - ADAM_B1 ** ADAM_STEP)
        v_hat = v_new / (1.0 - ADAM_B2 ** ADAM_STEP)
        d_out[...] = -ADAM_LR * (m_hat / (jnp.sqrt(v_hat) + ADAM_EPS) + ADAM_WD * w_ref[...])
        m_out[...] = m_new
        v_out[...] = v_new

    blk = pl.BlockSpec((PACK_ROWS, LANES), lambda i: (i, 0))
    return pl.pallas_call(
        body, grid=(rows // PACK_ROWS,), in_specs=[blk] * 4, out_specs=[blk] * 3, out_shape=[SDS((rows, LANES), F32)] * 3,
        compiler_params=_params(("parallel",)), name="adamw")(g, wgt, m, v)


PACK_ALIGN = 2 * SUBLANES * LANES
PACK_ROWS = 512

ROW_SHARDED = ("s5_w_glu", "w_out", "ffn_w_down")
TRANSPOSED = ("w_in", "ffn_w_up", "w_branch_rwkv", "w_branch_s5")
SMALL_SHARDED = ("rwkv_w2", "rwkv_a2", "rwkv_g2", "ffn_conv_w")
SHARDED = ROW_SHARDED + TRANSPOSED + SMALL_SHARDED
F32_GATHER = ("ffn_conv_w",)
REPLICATED =("norm_mix_pre", "norm_mix_post", "norm_ffn_pre", "norm_ffn_post", "b_gate", "rwkv_shift_mu", "rwkv_w0",
              "rwkv_a0", "rwkv_k_k", "rwkv_k_a", "rwkv_r_k", "rwkv_lnx_w", "rwkv_lnx_b", "s5_a_re", "s5_a_im", "s5_b_re",
              "s5_b_im", "s5_c_re", "s5_c_im", "s5_d", "s5_log_step", "s5_b_glu", "ffn_conv_b")
WEIGHTS = ("norm_mix_pre", "norm_mix_post", "norm_ffn_pre", "norm_ffn_post", "w_in", "b_gate", "rwkv_shift_mu", "rwkv_w0",
           "rwkv_w2", "rwkv_a0", "rwkv_a2", "rwkv_g2", "rwkv_k_k", "rwkv_k_a", "rwkv_r_k", "rwkv_lnx_w", "rwkv_lnx_b",
           "s5_a_re", "s5_a_im", "s5_b_re", "s5_b_im", "s5_c_re", "s5_c_im", "s5_d", "s5_log_step", "s5_w_glu", "s5_b_glu",
           "w_branch_rwkv", "w_branch_s5", "w_out", "ffn_w_up", "ffn_conv_w", "ffn_conv_b", "ffn_w_down")


def _pad_flat(a):
    flat = a.reshape(-1)
    pad = (-flat.shape[0]) % PACK_ALIGN
    return jnp.pad(flat, (0, pad)) if pad else flat


def _pad_cols(a, mult):
    pad = (-a.shape[1]) % mult
    return jnp.pad(a, ((0, 0), (0, pad))) if pad else a


def _pack(pieces):
    flat = jnp.concatenate([_pad_flat(p) for p in pieces])
    pad = (-flat.shape[0]) % (PACK_ROWS * LANES)
    return (jnp.pad(flat, (0, pad)) if pad else flat).reshape(-1, LANES)


def _unpack(buf, shapes):
    lead = buf.shape[:-2]
    flat = buf.reshape(*lead, -1)
    out, off = [], 0
    for s in shapes:
        n = math.prod(s)
        out.append(flat[..., off:off + n].reshape(*lead, *s))
        off += n + (-n) % PACK_ALIGN
    return out


def _rows_to_full(stack):
    return stack.reshape(-1, stack.shape[2])


def _cols_to_full(stack):
    return stack.transpose(1, 0, 2).reshape(stack.shape[1], -1)


EARLY_GRADS = ("ffn_w_down", "ffn_w_up", "w_out")
REST_GRADS = ("s5_w_glu", "w_in", "w_branch_rwkv", "w_branch_s5")


def _local_step(x, tgt, wt, get_late, send_early, send_rest):
    wt = dict(wt)
    o512 = _block_ones(RW, HN)
    w_in = wt["w_in"]
    w_rw, w_u, w_g = w_in[:N_RWKV], w_in[N_RWKV:N_RWKV + S5W], w_in[N_RWKV + S5W:]
    b_gate = wt["b_gate"]
    b_r, b_s = b_gate[:, :D_MODEL], b_gate[:, D_MODEL:]
    zpad = lambda a, lo, n: jnp.pad(a.astype(F32), ((lo, n - lo - a.shape[0]), (0, 0)))
    w2p, a2p, g2p = zpad(wt["rwkv_w2"], 0, 256), zpad(wt["rwkv_a2"], 64, 256), zpad(wt["rwkv_g2"], 128, 256)
    pre_small = [wt["rwkv_w0"], wt["rwkv_a0"], wt["rwkv_k_k"], wt["rwkv_k_a"], w2p, a2p, g2p]
    post_prm = [wt["rwkv_lnx_w"], wt["rwkv_lnx_b"], wt["rwkv_r_k"].reshape(1, RW)]
    mu = wt["rwkv_shift_mu"]

    a_re, a_im = wt["s5_a_re"].reshape(1, S5N), wt["s5_a_im"].reshape(1, S5N)
    ls = jnp.repeat(wt["s5_log_step"].reshape(S5G), S5P).reshape(1, S5N)
    b_re_t = wt["s5_b_re"].reshape(S5N, S5C).T
    b_im_t = wt["s5_b_im"].reshape(S5N, S5C).T
    c_re, c_im = wt["s5_c_re"].reshape(S5G, S5C, S5P), wt["s5_c_im"].reshape(S5G, S5C, S5P)
    abr, abi, bbr, bbi = _s5_prep_fwd(a_re, a_im, ls, b_re_t, b_im_t)
    bblk = jnp.concatenate([_s5_blockdiag_in(bbr), _s5_blockdiag_in(bbi)], axis=2)
    cblk = jnp.concatenate([_s5_blockdiag_out(c_re), -_s5_blockdiag_out(c_im)], axis=1)
    s5_d = wt["s5_d"]

    h1 = _rms_fwd(x, wt["norm_mix_pre"], "rms1_fwd")
    p_rw = _mm(h1, w_rw, tb=True, name="mm_proj_rwkv")
    u = _mm(h1, w_u, tb=True, name="mm_proj_s5")
    gpre = _mm(h1, w_g, tb=True, name="mm_proj_gate")
    r, decay, k2, v, aa, bb, g = _rwkv_pre_fwd(p_rw, mu, pre_small, o512)
    y, sst, sast, vst, s_fin = _wkv_fwd(r, decay, k2, v, aa, bb)
    wt.update(get_late(y))
    y_r = _rwkv_post_fwd(y, r, k2, v, g, post_prm, o512)
    o_r = _mm(y_r, wt["w_branch_rwkv"], tb=True, name="mm_branch_rwkv")
    ys, kre, kim = _s5_fwd(u, bblk, cblk, abr, abi, s5_d)
    t_glu = _mm(ys, wt["s5_w_glu"], name="mm_glu")
    out5 = _glu_fwd(ys, t_glu, wt["s5_b_glu"])
    o_s = _mm(out5, wt["w_branch_s5"], tb=True, name="mm_branch_s5")
    mi = _merge_fwd(gpre, o_r, o_s, b_r, b_s)
    mixed = _mm(mi, wt["w_out"], name="mm_out")
    x2, h2 = _mid_fwd(x, mixed, wt["norm_mix_post"], wt["norm_ffn_pre"])
    z = _mm(h2, wt["ffn_w_up"], tb=True, name="mm_up")
    act = _conv_fwd(z, wt["ffn_conv_w"], wt["ffn_conv_b"])
    f = _mm(act, wt["ffn_w_down"], name="mm_down")
    dx3, df, d_norm_ffn_post, loss_part = _final(x2, f, tgt, wt["norm_ffn_post"])

    gr = {"norm_ffn_post": d_norm_ffn_post}
    dact = _mm(df, wt["ffn_w_down"], tb=True, name="mm_down_dx")
    gr["ffn_w_down"] = _mm(act, df, ta=True, out_dtype=BF16, name="mm_down_dw")
    dzg, dzv, gr["ffn_conv_w"], gr["ffn_conv_b"] = _conv_bwd(z, dact, wt["ffn_conv_w"], wt["ffn_conv_b"])
    dz = jnp.concatenate([dzg, dzv], axis=1)
    dh2 = _mm(dz, wt["ffn_w_up"], name="mm_up_dx")
    gr["ffn_w_up"] = _mm(dz, h2, ta=True, out_dtype=BF16, name="mm_up_dw")
    dx2, dmixed, gr["norm_mix_post"], gr["norm_ffn_pre"] = _mid_bwd(x2, mixed, dh2, dx3, wt["norm_mix_post"], wt["norm_ffn_pre"])
    dmi = _mm(dmixed, wt["w_out"], tb=True, name="mm_out_dx")
    gr["w_out"] = _mm(mi, dmixed, ta=True, out_dtype=BF16, name="mm_out_dw")
    token = send_early({n: gr[n] for n in EARLY_GRADS})
    b_r = b_r + jnp.tile(token[0:1, :], (1, D_MODEL // LANES))
    dgp_r, dgp_s, do_r, do_s, db_r, db_s = _merge_bwd(gpre, o_r, o_s, dmi, b_r, b_s)
    gr["b_gate"] = jnp.concatenate([db_r, db_s], axis=1)
    dout5 = _mm(do_s, wt["w_branch_s5"], name="mm_branch_s5_dx")
    gr["w_branch_s5"] = _mm(do_s, out5, ta=True, out_dtype=BF16, name="mm_branch_s5_dw")
    dys_a, dt_glu, gr["s5_b_glu"] = _glu_bwd(ys, t_glu, dout5, wt["s5_b_glu"])
    dys_b = _mm(dt_glu, wt["s5_w_glu"], tb=True, name="mm_glu_dx")
    gr["s5_w_glu"] = _mm(ys, dt_glu, ta=True, out_dtype=BF16, name="mm_glu_dw")
    du, dbblk, dcblk, dabr, dabi, gr["s5_d"] = _s5_bwd(u, dys_a, dys_b, kre, kim, bblk, cblk, abr, abi, s5_d)
    gr["s5_c_re"] = _s5_blockdiag_out_t(dcblk[:, :S5_BN, :]).reshape(wt["s5_c_re"].shape)
    gr["s5_c_im"] = (-_s5_blockdiag_out_t(dcblk[:, S5_BN:, :])).reshape(wt["s5_c_im"].shape)
    dbbr, dbbi = _s5_blockdiag_in_t(dbblk[:, :, :S5_BN]), _s5_blockdiag_in_t(dbblk[:, :, S5_BN:])
    gsel = (lax.broadcasted_iota(jnp.int32, (S5N, LANES), 0) // S5P == lax.broadcasted_iota(jnp.int32, (S5N, LANES), 1)).astype(F32)
    d_are, d_aim, d_ls, d_bre_t, d_bim_t = _s5_prep_bwd(a_re, a_im, ls, b_re_t, b_im_t, (dabr, dabi, dbbr, dbbi), gsel)
    gr["s5_a_re"] = d_are.reshape(wt["s5_a_re"].shape)
    gr["s5_a_im"] = d_aim.reshape(wt["s5_a_im"].shape)
    gr["s5_log_step"] = d_ls[0:1, :S5G]
    gr["s5_b_re"] = d_bre_t.T.reshape(wt["s5_b_re"].shape)
    gr["s5_b_im"] = d_bim_t.T.reshape(wt["s5_b_im"].shape)
    dy_r = _mm(do_r, wt["w_branch_rwkv"], name="mm_branch_rwkv_dx")
    gr["w_branch_rwkv"] = _mm(do_r, y_r, ta=True, out_dtype=BF16, name="mm_branch_rwkv_dw")
    dy, dr1, dk1, dv1, dg, gr["rwkv_lnx_w"], gr["rwkv_lnx_b"], d_rk = _rwkv_post_bwd(y, r, k2, v, g, dy_r, post_prm, o512)
    gr["rwkv_r_k"] = d_rk.reshape(wt["rwkv_r_k"].shape)
    dr2, ddecay, dk2, daa, dbb, dv2 = _wkv_bwd(r, decay, k2, aa, bb, dy, sst, sast, vst, s_fin)
    cots = [dr1 + dr2, ddecay, dk1 + dk2, dv1 + dv2, daa, dbb, dg]
    dp_rw, gr["rwkv_shift_mu"], gr["rwkv_w0"], gr["rwkv_a0"], gr["rwkv_k_k"], gr["rwkv_k_a"], dw2p, da2p, dg2p = \
        _rwkv_pre_bwd(p_rw, cots, mu, pre_small, o512)
    gr["rwkv_w2"], gr["rwkv_a2"], gr["rwkv_g2"] = dw2p[:64], da2p[64:128], dg2p[128:]
    dproj = jnp.concatenate([dp_rw, du, dgp_r, dgp_s], axis=1)
    gr["w_in"] = _mm(dproj, h1, ta=True, out_dtype=BF16, name="mm_proj_dw")
    token = send_rest(gr)
    dh1 = _mm(dproj, w_in + token[0, 0].astype(w_in.dtype), name="mm_proj_dx")
    dx, gr["norm_mix_pre"] = _rms_bwd(x, wt["norm_mix_pre"], dh1, dx2, "rms1_bwd")
    return loss_part[0, 0], dx, gr


def kernel(x, norm_mix_pre, norm_mix_post, norm_ffn_pre, norm_ffn_post, w_in, b_gate, rwkv_shift_mu, rwkv_w0, rwkv_w2, rwkv_a0, rwkv_a2, rwkv_g2, rwkv_k_k, rwkv_k_a, rwkv_r_k, rwkv_lnx_w, rwkv_lnx_b, s5_a_re, s5_a_im, s5_b_re, s5_b_im, s5_c_re, s5_c_im, s5_d, s5_log_step, s5_w_glu, s5_b_glu, w_branch_rwkv, w_branch_s5, w_out, ffn_w_up, ffn_conv_w, ffn_conv_b, ffn_w_down, loss_target, m_norm_mix_pre, m_norm_mix_post, m_norm_ffn_pre, m_norm_ffn_post, m_w_in, m_b_gate, m_rwkv_shift_mu, m_rwkv_w0, m_rwkv_w2, m_rwkv_a0, m_rwkv_a2, m_rwkv_g2, m_rwkv_k_k, m_rwkv_k_a, m_rwkv_r_k, m_rwkv_lnx_w, m_rwkv_lnx_b, m_s5_a_re, m_s5_a_im, m_s5_b_re, m_s5_b_im, m_s5_c_re, m_s5_c_im, m_s5_d, m_s5_log_step, m_s5_w_glu, m_s5_b_glu, m_w_branch_rwkv, m_w_branch_s5, m_w_out, m_ffn_w_up, m_ffn_conv_w, m_ffn_conv_b, m_ffn_w_down, v_norm_mix_pre, v_norm_mix_post, v_norm_ffn_pre, v_norm_ffn_post, v_w_in, v_b_gate, v_rwkv_shift_mu, v_rwkv_w0, v_rwkv_w2, v_rwkv_a0, v_rwkv_a2, v_rwkv_g2, v_rwkv_k_k, v_rwkv_k_a, v_rwkv_r_k, v_rwkv_lnx_w, v_rwkv_lnx_b, v_s5_a_re, v_s5_a_im, v_s5_b_re, v_s5_b_im, v_s5_c_re, v_s5_c_im, v_s5_d, v_s5_log_step, v_s5_w_glu, v_s5_b_glu, v_w_branch_rwkv, v_w_branch_s5, v_w_out, v_ffn_w_up, v_ffn_conv_w, v_ffn_conv_b, v_ffn_w_down):
    args = dict(locals())
    wgt = {n: args[n] for n in WEIGHTS}
    mom = {n: args["m_" + n] for n in WEIGHTS}
    var = {n: args["v_" + n] for n in WEIGHTS}
    me = 4 * lax.axis_index("x") + 2 * lax.axis_index("y") + lax.axis_index("c")
    shard_shapes = {n: wgt[n].shape[1:] for n in SHARDED}
    sent_shapes = {n: (shard_shapes[n][::-1] if n in TRANSPOSED else shard_shapes[n]) for n in SHARDED}
    sent = lambda n, a: a.T if n in TRANSPOSED else a

    unshard = lambda n, blocks: _cols_to_full(blocks) if n in SMALL_SHARDED else _rows_to_full(blocks)

    first = ("w_in", "rwkv_w2", "rwkv_a2", "rwkv_g2")
    late = tuple(n for n in SHARDED if n not in first and n not in F32_GATHER)
    late_payloads = [_pack([sent(n, wgt[n][0].astype(BF16)) for n in late]), _pack([wgt[n][0] for n in F32_GATHER])]
    late_started = _exchange_start(late_payloads, False, "gather_late_start")
    got_first = _unpack(_all_gather(_pack([sent(n, wgt[n][0].astype(BF16)) for n in first]), "all_gather_weights"),
                        [sent_shapes[n] for n in first])
    wt = {n: wgt[n] for n in REPLICATED}
    for n, blocks in zip(first, got_first, strict=True):
        wt[n] = unshard(n, blocks)

    def get_late(after):
        lands = _exchange_wait(late_started, after, False, "gather_late_wait")
        out = {}
        for names, land, mine in zip((late, F32_GATHER), lands, late_payloads, strict=True):
            blocks = _unpack(lax.dynamic_update_index_in_dim(land, mine, me, 0), [sent_shapes[n] for n in names])
            out.update({n: unshard(n, b_) for n, b_ in zip(names, blocks, strict=True)})
        return out

    whole = tuple(n for n in SMALL_SHARDED + REPLICATED if n != "norm_mix_pre")
    blocks_of = lambda g_: g_.reshape(N_DEV, -1, LANES)
    started = {}

    def send_early(grads_):
        started["early"] = _exchange_start([blocks_of(grads_[n]) for n in EARLY_GRADS], True, "exchange_early_start")
        return started["early"][4]

    def send_rest(grads_):
        started["rest"] = _exchange_start([blocks_of(grads_[n]) for n in REST_GRADS], True, "exchange_rest_start")
        started["small_mine"] = _pack([grads_[n] for n in whole])
        started["small"] = _exchange_start([started["small_mine"]], False, "gather_small_start")
        return started["rest"][4] + started["small"][4]

    loss_part, dx, gr = _local_step(x[0], loss_target[0], wt, get_late, send_early, send_rest)

    got = {}
    for key, names in (("early", EARLY_GRADS), ("rest", REST_GRADS)):
        lands = _exchange_wait(started[key], dx, True, "exchange_" + key + "_wait")
        for n, land in zip(names, lands, strict=True):
            own = lax.dynamic_index_in_dim(blocks_of(gr[n]), me, 0, keepdims=False)
            got[n] = _sum_parts(land, "sum_" + n, own=own).reshape(sent_shapes[n])
    small_land, = _exchange_wait(started["small"], dx, False, "gather_small_wait")
    small_sum = _sum_parts(lax.dynamic_update_index_in_dim(small_land, started["small_mine"], me, 0), "sum_small_grads")
    got.update(zip(whole, _unpack(small_sum, [gr[n].shape for n in whole]), strict=True))
    last = gr["norm_mix_pre"].reshape(SUBLANES, LANES)
    got["norm_mix_pre"] = _sum_parts(_all_gather(last, "gather_last_grad"), "sum_last_grad").reshape(gr["norm_mix_pre"].shape)
    grads = {}
    for n in WEIGHTS:
        if n in TRANSPOSED:
            grads[n] = got[n].T
        elif n in SMALL_SHARDED:
            cols = shard_shapes[n][1]
            grads[n] = lax.dynamic_slice_in_dim(got[n], me * cols, cols, axis=1)
        else:
            grads[n] = got[n]
        grads[n] = grads[n].reshape(wgt[n].shape)

    pack_local = lambda src: _pack([src[n].reshape(-1) for n in WEIGHTS])
    outs = _adamw(pack_local(grads), pack_local(wgt), pack_local(mom), pack_local(var))
    shapes = [wgt[n].shape for n in WEIGHTS]
    loss = lax.psum(loss_part, AXES)
    return (loss, dx[None], *[grads[n] for n in WEIGHTS], *_unpack(outs[0], shapes), *_unpack(outs[1], shapes),
            *_unpack(outs[2], shapes))
```

```python
import functools
import math

import jax
import jax.numpy as jnp
from jax import lax
from jax.experimental import pallas as pl
from jax.experimental.pallas import tpu as pltpu

F32 = jnp.float32
BF16 = jnp.bfloat16
SDS = jax.ShapeDtypeStruct
HI = lax.Precision.HIGHEST
MESH = pl.DeviceIdType.MESH
AXES = ("x", "y", "c")
N_DEV = 8

D_MODEL = 1024
RW = 512
HN = 64
N_RWKV = 1792
S5W = 512
S5G = 32
S5C = 16
S5P = 64
S5N = S5G * S5P
D_FF = 2816
NORM_EPS = 1e-6
LNX_EPS = 64e-5

ADAM_LR = 0.001
ADAM_B1 = 0.9
ADAM_B2 = 0.999
ADAM_EPS = 1e-08
ADAM_WD = 0.01
ADAM_STEP = 10

LANES = 128
SUBLANES = 8
VMEM_LIMIT = 56 * 1024 * 1024
WKV_CHUNK = 32


def _params(sem=("arbitrary",)):
    return pltpu.CompilerParams(dimension_semantics=sem, vmem_limit_bytes=VMEM_LIMIT)


def _pick(n, cap):
    best = None
    for t in range(LANES, min(n, cap) + 1, LANES):
        if n % t == 0:
            best = t
    return best or n


MM_VMEM_BUDGET = 40 * 1024 * 1024


def _mm_tiles(m, n, k, a_bytes, b_bytes, o_bytes):
    def divisors(d):
        return [t for t in range(LANES, d + 1, LANES) if d % t == 0] or [d]

    tn = _pick(n, 2304)
    best = None
    for tm in divisors(m):
        for tk in divisors(k):
            vmem = 2 * (tm * tk * a_bytes + tk * tn * b_bytes) + tm * tn * 4 + 2 * tm * tn * o_bytes
            if vmem > MM_VMEM_BUDGET:
                continue
            key = ((m // tm) * (n // tn) * (k // tk), -tm)
            if best is None or key < best[0]:
                best = (key, tm, tk)
    assert best is not None, (m, n, k)
    return best[1], tn, best[2]


def _mm(a, b, *, ta=False, tb=False, out_dtype=F32, name):
    m = a.shape[1] if ta else a.shape[0]
    k = a.shape[0] if ta else a.shape[1]
    n = b.shape[0] if tb else b.shape[1]
    assert (b.shape[1] if tb else b.shape[0]) == k
    tm, tn, tk = _mm_tiles(m, n, k, a.dtype.itemsize, b.dtype.itemsize, jnp.dtype(out_dtype).itemsize)
    nk = k // tk
    dims = (((0 if ta else 1,), (1 if tb else 0,)), ((), ()))

    def body(a_ref, b_ref, o_ref, acc_ref):
        kk = pl.program_id(2)

        @pl.when(kk == 0)
        def _():
            acc_ref[...] = jnp.zeros_like(acc_ref)

        acc_ref[...] += lax.dot_general(a_ref[...].astype(BF16), b_ref[...].astype(BF16), dims,
                                        preferred_element_type=F32)

        @pl.when(kk == nk - 1)
        def _():
            o_ref[...] = acc_ref[...].astype(o_ref.dtype)

    a_spec = pl.BlockSpec((tk, tm), lambda i, j, kk: (kk, i)) if ta else pl.BlockSpec((tm, tk), lambda i, j, kk: (i, kk))
    b_spec = pl.BlockSpec((tn, tk), lambda i, j, kk: (j, kk)) if tb else pl.BlockSpec((tk, tn), lambda i, j, kk: (kk, j))
    return pl.pallas_call(
        body, grid=(m // tm, n // tn, nk), in_specs=[a_spec, b_spec],
        out_specs=pl.BlockSpec((tm, tn), lambda i, j, kk: (i, j)),
        out_shape=SDS((m, n), out_dtype), scratch_shapes=[pltpu.VMEM((tm, tn), F32)],
        compiler_params=_params(("parallel", "parallel", "arbitrary")), name=name)(a, b)


def _rows(fn, rows, params, out_rows, out_accs, *, name, tl, reverse=False, scratch=()):
    first = rows[0][0] if isinstance(rows[0], tuple) else rows[0]
    length = first.shape[0]
    tl = min(tl, length)
    nt = length // tl
    rmap = (lambda i: nt - 1 - i) if reverse else (lambda i: i)
    specs, arrs = [], []
    for r in rows:
        arr, wdt, cb = r if isinstance(r, tuple) else (r, r.shape[1], 0)
        specs.append(pl.BlockSpec((tl, wdt), lambda i, cb=cb: (rmap(i), cb)))
        arrs.append(arr)
    for p in params:
        specs.append(pl.BlockSpec(p.shape, lambda i, nd=p.ndim: (0,) * nd))
        arrs.append(p)
    out_shape = [SDS((length, c), dt) for c, dt in out_rows] + [SDS(s, F32) for s in out_accs]
    out_specs = [pl.BlockSpec((tl, c), lambda i: (rmap(i), 0)) for c, _ in out_rows]
    out_specs += [pl.BlockSpec(s, lambda i, nd=len(s): (0,) * nd) for s in out_accs]
    nr, npar, nor, noa = len(rows), len(params), len(out_rows), len(out_accs)

    def body(*refs):
        rin, pin = refs[:nr], refs[nr:nr + npar]
        rout = refs[nr + npar:nr + npar + nor]
        aout = refs[nr + npar + nor:nr + npar + nor + noa]
        scr = refs[nr + npar + nor + noa:]
        step = pl.program_id(0)
        outs_r, outs_a = fn(step, [r[...] for r in rin], [p[...] for p in pin], scr)
        for ref, val in zip(rout, outs_r, strict=True):
            ref[...] = val.astype(ref.dtype)

        @pl.when(step == 0)
        def _():
            for ref in aout:
                ref[...] = jnp.zeros_like(ref)

        for ref, val in zip(aout, outs_a, strict=True):
            ref[...] += val.astype(F32)

    res = pl.pallas_call(body, grid=(nt,), in_specs=specs, out_specs=out_specs, out_shape=out_shape,
                         scratch_shapes=list(scratch), compiler_params=_params(), name=name)(*arrs)
    return list(res)


def _rms(x, g):
    return x * lax.rsqrt(jnp.mean(x * x, axis=-1, keepdims=True) + NORM_EPS) * g


def _sig(x):
    return 0.5 * (jnp.tanh(0.5 * x) + 1.0)


def _softplus(x):
    return jnp.maximum(x, 0.0) + jnp.log(1.0 + jnp.exp(-jnp.abs(x)))


def _gelu(x):
    return x * (0.5 * (1.0 + jnp.tanh(math.sqrt(2.0 / math.pi) * (x + 0.044715 * (x * x * x)))))


def _bdot(a, b):
    return jnp.dot(a.astype(BF16), b.astype(BF16), preferred_element_type=F32)


def _hdot(a, b):
    return jnp.dot(a, b, precision=HI, preferred_element_type=F32)


def _segsum_impl(x, ones):
    hi = x.astype(BF16)
    lo = (x - hi.astype(F32)).astype(BF16)
    ones = ones.astype(BF16)
    return jnp.dot(jnp.concatenate([hi, lo], axis=1), jnp.concatenate([ones, ones], axis=0), preferred_element_type=F32)


@jax.custom_vjp
def _segsum(x, ones):
    return _segsum_impl(x, ones)


_segsum.defvjp(lambda x, ones: (_segsum_impl(x, ones), ones),
               lambda ones, g: (_segsum_impl(g, ones), jnp.zeros_like(ones)))


def _block_ones(n, blk):
    i = lax.broadcasted_iota(jnp.int32, (n, n), 0) // blk
    j = lax.broadcasted_iota(jnp.int32, (n, n), 1) // blk
    return (i == j).astype(F32)


def _rms_fwd(x, g, name):
    return _rows(lambda s, r, p, _: ([_rms(r[0], p[0])], []), [x], [g], [(x.shape[1], BF16)], [], name=name, tl=512)[0]


def _rms_bwd(x, g, dh, dres, name):
    def fn(s, r, p, _):
        _, vjp = jax.vjp(_rms, r[0], p[0])
        dx, dg = vjp(r[1])
        return [dx + r[2]], [dg]
    return _rows(fn, [x, dh, dres], [g], [(x.shape[1], F32)], [g.shape], name=name, tl=256)


def _rwkv_pre_math(k_, lr, w0, a0, k_k, k_a, w2p, a2p, g2p, o512):
    pre_w = w0 + _bdot(jnp.tanh(lr), w2p)
    w = -_softplus(-pre_w) - 0.5
    decay = jnp.exp(-jnp.exp(w))
    a = _sig(a0 + _bdot(lr, a2p))
    g = _bdot(_sig(lr), g2p)
    kr = k_ * k_k
    kk = kr / jnp.maximum(jnp.sqrt(_segsum(kr * kr, o512)), 1e-12)
    k2 = k_ * (1.0 + (a - 1.0) * k_a)
    return decay, k2, -kk, kk * a, g


def _shift_down(p, prev_row):
    row = lax.broadcasted_iota(jnp.int32, p.shape, 0)
    return jnp.where(row == 0, jnp.broadcast_to(prev_row, p.shape), pltpu.roll(p, 1, 0))


def _shift_up(q, next_row):
    n = q.shape[0]
    row = lax.broadcasted_iota(jnp.int32, q.shape, 0)
    return jnp.where(row == n - 1, jnp.broadcast_to(next_row, q.shape), pltpu.roll(q, n - 1, 0))


def _rwkv_pre_fwd(p, mu, small, o512):
    def fn(step, r, prm, scr):
        car = scr[0]

        @pl.when(step == 0)
        def _():
            car[...] = jnp.zeros_like(car)

        x = r[0]
        prev = _shift_down(x, car[SUBLANES - 1:SUBLANES, :])
        car[...] = x[x.shape[0] - SUBLANES:, :]
        xs = x + (prev - x) * prm[0]
        decay, k2, aa, bb, g = _rwkv_pre_math(xs[:, RW:2 * RW], xs[:, 3 * RW:], *prm[1:])
        return [xs[:, :RW], decay, k2, xs[:, 2 * RW:3 * RW], aa, bb, g], []
    return _rows(fn, [p], [mu, *small, o512], [(RW, F32)] * 7, [], name="rwkv_pre_fwd", tl=256,
                 scratch=[pltpu.VMEM((SUBLANES, N_RWKV), F32)])


def _rwkv_pre_bwd(p, cots, mu, small, o512):
    length = p.shape[0]
    tl = min(256, length)
    nt = length // tl
    rows_per = tl // SUBLANES
    params = [mu, *small, o512]
    acc_shapes = [mu.shape] + [q.shape for q in small]
    nr, npar, nacc = 2 + len(cots), len(params), len(acc_shapes)

    def body(*refs):
        rin, pin = refs[:nr], refs[nr:nr + npar]
        dp_ref = refs[nr + npar]
        aout = refs[nr + npar + 1:nr + npar + 1 + nacc]
        car_q = refs[nr + npar + 1 + nacc]
        step = pl.program_id(0)

        @pl.when(step == 0)
        def _():
            car_q[...] = jnp.zeros_like(car_q)
            for ref in aout:
                ref[...] = jnp.zeros_like(ref)

        x = rin[0][...]
        prev_row = jnp.where(step == nt - 1, 0.0, rin[1][SUBLANES - 1:SUBLANES, :])
        dr, ddecay, dk2, dv, daa, dbb, dg, dr_b, dk2_b, dv_b = [r[...] for r in rin[2:]]
        dr, dk2, dv = dr + dr_b, dk2 + dk2_b, dv + dv_b
        prm = [q[...] for q in pin]
        mu_, o512_ = prm[0], prm[-1]
        prev = _shift_down(x, prev_row)
        xs = x + (prev - x) * mu_
        _, vjp = jax.vjp(lambda k_, lr, *w: _rwkv_pre_math(k_, lr, *w, o512_), xs[:, RW:2 * RW], xs[:, 3 * RW:], *prm[1:-1])
        dk_, dlr, *dsmall = vjp((ddecay, dk2, daa, dbb, dg))
        dxs = jnp.concatenate([dr, dk_, dv, dlr], axis=1)
        q = dxs * mu_
        dp_ref[...] = (dxs - q + _shift_up(q, car_q[0:1, :])).astype(dp_ref.dtype)
        car_q[...] = q[:SUBLANES, :]
        aout[0][...] += jnp.sum((prev - x) * dxs, axis=0, keepdims=True)
        for ref, val in zip(aout[1:], dsmall, strict=True):
            ref[...] += val

    rmap = lambda i: nt - 1 - i
    specs = [pl.BlockSpec((tl, N_RWKV), lambda i: (rmap(i), 0)),
             pl.BlockSpec((SUBLANES, N_RWKV), lambda i: (jnp.maximum(rmap(i) * rows_per - 1, 0), 0))]
    specs += [pl.BlockSpec((tl, RW), lambda i: (rmap(i), 0)) for _ in cots]
    specs += [pl.BlockSpec(q.shape, lambda i, nd=q.ndim: (0,) * nd) for q in params]
    out_shape = [SDS((length, N_RWKV), BF16)] + [SDS(sh, F32) for sh in acc_shapes]
    out_specs = [pl.BlockSpec((tl, N_RWKV), lambda i: (rmap(i), 0))]
    out_specs += [pl.BlockSpec(sh, lambda i, nd=len(sh): (0,) * nd) for sh in acc_shapes]
    res = pl.pallas_call(body, grid=(nt,), in_specs=specs, out_specs=out_specs, out_shape=out_shape,
                         scratch_shapes=[pltpu.VMEM((SUBLANES, N_RWKV), F32)],
                         compiler_params=_params(), name="rwkv_pre_bwd")(p, p, *cots, *params)
    return list(res)


def _rwkv_post_math(y, r, k2, v, g, lnx_w, lnx_b, r_k, o512):
    mean = _segsum(y, o512) * (1.0 / HN)
    yc = y - mean
    var = _segsum(yc * yc, o512) * (1.0 / HN)
    yn = yc * lax.rsqrt(var + LNX_EPS) * lnx_w + lnx_b
    bonus = _segsum(r * k2 * r_k, o512) * v
    return (yn + bonus) * g


def _rwkv_post_fwd(y, r, k2, v, g, prm, o512):
    return _rows(lambda s, rr, p, _: ([_rwkv_post_math(*rr, *p)], []), [y, r, k2, v, g], [*prm, o512],
                 [(RW, BF16)], [], name="rwkv_post_fwd", tl=256)[0]


def _rwkv_post_bwd(y, r, k2, v, g, dout, prm, o512):
    def fn(s, rr, p, _):
        o = p[-1]
        _, vjp = jax.vjp(lambda *a: _rwkv_post_math(*a, o), *rr[:5], *p[:-1])
        gr = vjp(rr[5])
        return list(gr[:5]), list(gr[5:])
    return _rows(fn, [y, r, k2, v, g, dout], [*prm, o512], [(RW, F32)] * 5, [q.shape for q in prm],
                 name="rwkv_post_bwd", tl=256)


def _glu_math(ys, t, b):
    return ys * _sig(t + b)


def _glu_fwd(ys, t, b):
    return _rows(lambda s, r, p, _: ([_glu_math(r[0], r[1], p[0])], []), [ys, t], [b], [(S5W, BF16)], [],
                 name="s5_glu_fwd", tl=512)[0]


def _glu_bwd(ys, t, dout, b):
    def fn(s, r, p, _):
        _, vjp = jax.vjp(_glu_math, r[0], r[1], p[0])
        dys, dt, db = vjp(r[2])
        return [dys, dt], [db]
    return _rows(fn, [ys, t, dout], [b], [(S5W, F32), (S5W, BF16)], [b.shape], name="s5_glu_bwd", tl=512)


def _merge_math(gp_r, gp_s, o_r, o_s, b_r, b_s):
    return _sig(gp_r + b_r) * o_r + _sig(gp_s + b_s) * o_s


def _merge_fwd(gpre, o_r, o_s, b_r, b_s):
    return _rows(lambda s, r, p, _: ([_merge_math(*r, *p)], []),
                 [(gpre, D_MODEL, 0), (gpre, D_MODEL, 1), o_r, o_s], [b_r, b_s], [(D_MODEL, BF16)], [],
                 name="merge_fwd", tl=256)[0]


def _merge_bwd(gpre, o_r, o_s, dmi, b_r, b_s):
    def fn(s, r, p, _):
        _, vjp = jax.vjp(_merge_math, *r[:4], *p)
        dgr, dgs, dor, dos, dbr, dbs = vjp(r[4])
        return [dgr, dgs, dor, dos], [dbr, dbs]
    return _rows(fn, [(gpre, D_MODEL, 0), (gpre, D_MODEL, 1), o_r, o_s, dmi], [b_r, b_s], [(D_MODEL, BF16)] * 4,
                 [b_r.shape, b_s.shape], name="merge_bwd", tl=256)


def _mid_fwd(x, mixed, g_post, g_pre):
    def fn(s, r, p, _):
        x2 = r[0] + _rms(r[1], p[0])
        return [x2, _rms(x2, p[1])], []
    return _rows(fn, [x, mixed], [g_post, g_pre], [(D_MODEL, F32), (D_MODEL, BF16)], [], name="mid_fwd", tl=256)


def _mid_bwd(x2, mixed, dh2, dx3, g_post, g_pre):
    def fn(s, r, p, _):
        _, vjp1 = jax.vjp(_rms, r[0], p[1])
        dx2, dg_pre = vjp1(r[2])
        dx2 = dx2 + r[3]
        _, vjp2 = jax.vjp(_rms, r[1], p[0])
        dmixed, dg_post = vjp2(dx2)
        return [dx2, dmixed], [dg_post, dg_pre]
    return _rows(fn, [x2, mixed, dh2, dx3], [g_post, g_pre], [(D_MODEL, F32), (D_MODEL, BF16)], [g_post.shape, g_pre.shape],
                 name="mid_bwd", tl=256)


def _final(x2, f, tgt, g_post):
    def fn(s, r, p, _):
        y, vjp = jax.vjp(_rms, r[1], p[0])
        diff = r[0] + y - r[2]
        dx3 = diff * (1.0 / D_MODEL)
        df, dg = vjp(dx3)
        part = 0.5 * jnp.sum(jnp.sum(diff * diff, axis=1, keepdims=True), axis=0, keepdims=True) * (1.0 / D_MODEL)
        return [dx3, df], [dg, jnp.broadcast_to(part, (1, LANES))]
    return _rows(fn, [x2, f, tgt], [g_post], [(D_MODEL, F32), (D_MODEL, BF16)], [g_post.shape, (1, LANES)], name="final", tl=256)


def _conv_taps(z, car):
    row = lax.broadcasted_iota(jnp.int32, z.shape, 0)
    z1 = jnp.where(row == 0, jnp.broadcast_to(car[7:8, :], z.shape), pltpu.roll(z, 1, 0))
    z2 = pltpu.roll(z, 2, 0)
    z2 = jnp.where(row == 0, jnp.broadcast_to(car[6:7, :], z.shape), z2)
    z2 = jnp.where(row == 1, jnp.broadcast_to(car[7:8, :], z.shape), z2)
    return z1, z2


def _conv(z, car, w, b):
    z1, z2 = _conv_taps(z, car)
    return b + w[0:1, :] * z2 + w[1:2, :] * z1 + w[2:3, :] * z, z1, z2


def _conv_fwd(z, conv_w, conv_b):
    length = z.shape[0]
    tl = min(256, length)
    nt = length // tl
    tc = _pick(D_FF, 1536)
    nb = D_FF // tc

    def body(zg_ref, zv_ref, wg_ref, wv_ref, bg_ref, bv_ref, o_ref, cg, cv):
        @pl.when(pl.program_id(1) == 0)
        def _():
            cg[...] = jnp.zeros_like(cg)
            cv[...] = jnp.zeros_like(cv)

        zg, zv = zg_ref[...], zv_ref[...]
        gate, _, _ = _conv(zg, cg[...], wg_ref[...], bg_ref[...])
        val, _, _ = _conv(zv, cv[...], wv_ref[...], bv_ref[...])
        cg[...] = zg[tl - SUBLANES:, :]
        cv[...] = zv[tl - SUBLANES:, :]
        o_ref[...] = (_gelu(gate) * val).astype(o_ref.dtype)

    zspec = lambda off: pl.BlockSpec((tl, tc), lambda j, i: (i, j + off))
    wspec = lambda off, r: pl.BlockSpec((r, tc), lambda j, i: (0, j + off))
    return pl.pallas_call(
        body, grid=(nb, nt),
        in_specs=[zspec(0), zspec(nb), wspec(0, 3), wspec(nb, 3), wspec(0, 1), wspec(nb, 1)],
        out_specs=pl.BlockSpec((tl, tc), lambda j, i: (i, j)), out_shape=SDS((length, D_FF), BF16),
        scratch_shapes=[pltpu.VMEM((SUBLANES, tc), F32)] * 2,
        compiler_params=_params(("arbitrary", "arbitrary")), name="conv_fwd")(z, z, conv_w, conv_w, conv_b, conv_b)


def _conv_bwd(z, dact, conv_w, conv_b):
    length = z.shape[0]
    tl = min(256, length)
    nt = length // tl
    tc = _pick(D_FF, 1536)
    nb = D_FF // tc
    rows_per = tl // SUBLANES

    def half_bwd(dzc, z, z1, z2, w, dcar):
        n = tl
        row = lax.broadcasted_iota(jnp.int32, dzc.shape, 0)
        u1 = jnp.where(row == n - 1, jnp.broadcast_to(dcar[0:1, :], dzc.shape), pltpu.roll(dzc, n - 1, 0))
        u2 = pltpu.roll(dzc, n - 2, 0)
        u2 = jnp.where(row == n - 2, jnp.broadcast_to(dcar[0:1, :], dzc.shape), u2)
        u2 = jnp.where(row == n - 1, jnp.broadcast_to(dcar[1:2, :], dzc.shape), u2)
        dz = w[2:3, :] * dzc + w[1:2, :] * u1 + w[0:1, :] * u2
        dw = jnp.concatenate([jnp.sum(dzc * z2, axis=0, keepdims=True), jnp.sum(dzc * z1, axis=0, keepdims=True),
                              jnp.sum(dzc * z, axis=0, keepdims=True)], axis=0)
        return dz, dw, jnp.sum(dzc, axis=0, keepdims=True)

    def body(zg_ref, zv_ref, pg_ref, pv_ref, da_ref, wg_ref, wv_ref, bg_ref, bv_ref,
             dzg_ref, dzv_ref, dwg_ref, dwv_ref, dbg_ref, dbv_ref, cg, cv):
        step = pl.program_id(1)

        @pl.when(step == 0)
        def _():
            cg[...] = jnp.zeros_like(cg)
            cv[...] = jnp.zeros_like(cv)
            for ref in (dwg_ref, dwv_ref, dbg_ref, dbv_ref):
                ref[...] = jnp.zeros_like(ref)

        is_first_tile = step == nt - 1
        zg, zv = zg_ref[...], zv_ref[...]
        pg = jnp.where(is_first_tile, 0.0, pg_ref[...])
        pv = jnp.where(is_first_tile, 0.0, pv_ref[...])
        wg, wv = wg_ref[...], wv_ref[...]
        gate, zg1, zg2 = _conv(zg, pg, wg, bg_ref[...])
        val, zv1, zv2 = _conv(zv, pv, wv, bv_ref[...])
        act_g, vjp = jax.vjp(_gelu, gate)
        da = da_ref[...]
        dgate = vjp(da * val)[0]
        dval = da * act_g
        dzg, dwg, dbg = half_bwd(dgate, zg, zg1, zg2, wg, cg[...])
        dzv, dwv, dbv = half_bwd(dval, zv, zv1, zv2, wv, cv[...])
        cg[...] = dgate[:SUBLANES, :]
        cv[...] = dval[:SUBLANES, :]
        dzg_ref[...] = dzg.astype(dzg_ref.dtype)
        dzv_ref[...] = dzv.astype(dzv_ref.dtype)
        dwg_ref[...] += dwg
        dwv_ref[...] += dwv
        dbg_ref[...] += dbg
        dbv_ref[...] += dbv

    rmap = lambda i: nt - 1 - i
    zspec = lambda off: pl.BlockSpec((tl, tc), lambda j, i: (rmap(i), j + off))
    pspec = lambda off: pl.BlockSpec((SUBLANES, tc), lambda j, i: (jnp.maximum(rmap(i) * rows_per - 1, 0), j + off))
    wspec = lambda off, r: pl.BlockSpec((r, tc), lambda j, i: (0, j + off))
    out_w = lambda r: pl.BlockSpec((r, tc), lambda j, i: (0, j))
    dzg, dzv, dwg, dwv, dbg, dbv = pl.pallas_call(
        body, grid=(nb, nt),
        in_specs=[zspec(0), zspec(nb), pspec(0), pspec(nb), pl.BlockSpec((tl, tc), lambda j, i: (rmap(i), j)),
                  wspec(0, 3), wspec(nb, 3), wspec(0, 1), wspec(nb, 1)],
        out_specs=[pl.BlockSpec((tl, tc), lambda j, i: (rmap(i), j))] * 2 + [out_w(3), out_w(3), out_w(1), out_w(1)],
        out_shape=[SDS((length, D_FF), BF16)] * 2 + [SDS((3, D_FF), F32)] * 2 + [SDS((1, D_FF), F32)] * 2,
        scratch_shapes=[pltpu.VMEM((SUBLANES, tc), F32)] * 2,
        compiler_params=_params(("arbitrary", "arbitrary")), name="conv_bwd")(z, z, z, z, dact, conv_w, conv_w, conv_b, conv_b)
    return dzg, dzv, jnp.concatenate([dwg, dwv], axis=1), jnp.concatenate([dbg, dbv], axis=1)


def _s5_prep_math(a_re, a_im, ls, b_re, b_im):
    dt = jnp.exp(ls)
    er = jnp.exp(a_re * dt)
    ph = a_im * dt
    abr, abi = er * jnp.cos(ph), er * jnp.sin(ph)
    den = a_re * a_re + a_im * a_im
    nr = abr - 1.0
    cr = (nr * a_re + abi * a_im) / den
    ci = (abi * a_re - nr * a_im) / den
    return abr, abi, cr * b_re - ci * b_im, cr * b_im + ci * b_re


def _s5_prep_fwd(a_re, a_im, ls, b_re, b_im):
    def body(ar, ai, l, br, bi, o1, o2, o3, o4):
        for ref, val in zip((o1, o2, o3, o4), _s5_prep_math(ar[...], ai[...], l[...], br[...], bi[...]), strict=True):
            ref[...] = val
    return pl.pallas_call(body, out_shape=[SDS((1, S5N), F32)] * 2 + [SDS((S5C, S5N), F32)] * 2,
                          name="s5_prep_fwd")(a_re, a_im, ls, b_re, b_im)


def _s5_prep_bwd(a_re, a_im, ls, b_re, b_im, cots, gsel):
    def body(ar, ai, l, br, bi, c1, c2, c3, c4, g_ref, o1, o2, o3, o4, o5):
        _, vjp = jax.vjp(_s5_prep_math, ar[...], ai[...], l[...], br[...], bi[...])
        d_ar, d_ai, d_ls, d_br, d_bi = vjp((c1[...], c2[...], c3[...], c4[...]))
        o1[...] = d_ar
        o2[...] = d_ai
        o3[...] = _hdot(jnp.broadcast_to(d_ls, (SUBLANES, S5N)), g_ref[...])
        o4[...] = d_br
        o5[...] = d_bi
    return pl.pallas_call(body, out_shape=[SDS((1, S5N), F32)] * 2 + [SDS((SUBLANES, LANES), F32)] + [SDS((S5C, S5N), F32)] * 2,
                          name="s5_prep_bwd")(a_re, a_im, ls, b_re, b_im, *cots, gsel)


def _cmul(ar, ai, br, bi):
    return ar * br - ai * bi, ar * bi + ai * br


def _s5_powers(abr, abi):
    shp = (SUBLANES, S5N)
    a1 = (jnp.broadcast_to(abr, shp), jnp.broadcast_to(abi, shp))
    a2 = _cmul(*a1, *a1)
    a4 = _cmul(*a2, *a2)
    row = lax.broadcasted_iota(jnp.int32, shp, 0)
    pr, pi = a1
    cur = a1
    for i in range(1, SUBLANES):
        cur = _cmul(*cur, *a1)
        pr = jnp.where(row == i, cur[0], pr)
        pi = jnp.where(row == i, cur[1], pi)
    return a1, a2, a4, (pr, pi)


def _s5_scan(sre_ref, sim_ref, car_re, car_im, abr, abi, n_rows, reverse):
    a1, a2, a4, (pr, pi) = _s5_powers(abr, abi)
    sgn = -1.0 if reverse else 1.0
    row = lax.broadcasted_iota(jnp.int32, (SUBLANES, S5N), 0)
    if reverse:
        qr, qi = pr, pi
        for i in range(SUBLANES):
            src = SUBLANES - 1 - i
            qr = jnp.where(row == i, jnp.broadcast_to(pr[src:src + 1, :], pr.shape), qr)
            qi = jnp.where(row == i, jnp.broadcast_to(pi[src:src + 1, :], pi.shape), qi)
        pr, pi = qr, qi
    nblk = n_rows // SUBLANES

    def blk(i, carry):
        cr, ci = carry
        b = (nblk - 1 - i) if reverse else i
        sl = pl.ds(pl.multiple_of(b * SUBLANES, SUBLANES), SUBLANES)
        xr, xi = sre_ref[sl, :], sim_ref[sl, :]
        for kk, (er, ei) in ((1, a1), (2, a2), (4, a4)):
            if reverse:
                sr = jnp.where(row < SUBLANES - kk, pltpu.roll(xr, SUBLANES - kk, 0), 0.0)
                si = jnp.where(row < SUBLANES - kk, pltpu.roll(xi, SUBLANES - kk, 0), 0.0)
            else:
                sr = jnp.where(row >= kk, pltpu.roll(xr, kk, 0), 0.0)
                si = jnp.where(row >= kk, pltpu.roll(xi, kk, 0), 0.0)
            dr, di = _cmul(er, sgn * ei, sr, si)
            xr, xi = xr + dr, xi + di
        dr, di = _cmul(pr, sgn * pi, cr, ci)
        xr, xi = xr + dr, xi + di
        sre_ref[sl, :] = xr
        sim_ref[sl, :] = xi
        edge = 0 if reverse else SUBLANES - 1
        return (jnp.broadcast_to(xr[edge:edge + 1, :], xr.shape), jnp.broadcast_to(xi[edge:edge + 1, :], xi.shape))

    cr, ci = lax.fori_loop(0, nblk, blk, (car_re[...], car_im[...]))
    car_re[...] = cr
    car_im[...] = ci


S5_BLK = 4
S5_BN = S5N // S5_BLK


def _split2(x):
    hi = x.astype(BF16)
    return hi, (x - hi.astype(F32)).astype(BF16)


def _dot3(a, b, ca, cb):
    ah, al = _split2(a)
    bh, bl = _split2(b)
    return lax.dot_general(jnp.concatenate([ah, ah, al], axis=ca), jnp.concatenate([bh, bl, bh], axis=cb),
                           (((ca,), (cb,)), ((), ())), preferred_element_type=F32)


def _dot1(a, b, ca, cb):
    return lax.dot_general(a.astype(BF16), b.astype(BF16), (((ca,), (cb,)), ((), ())), preferred_element_type=F32)


def _s5_project_in(u, bblk_ref, sre_ref, sim_ref):
    for j in range(S5_BLK):
        bu = _dot1(u[:, j * LANES:(j + 1) * LANES], bblk_ref[j], 1, 0)
        sre_ref[:, j * S5_BN:(j + 1) * S5_BN] = bu[:, :S5_BN]
        sim_ref[:, j * S5_BN:(j + 1) * S5_BN] = bu[:, S5_BN:]


def _s5_project_out(u, d, cblk_ref, sre_ref, sim_ref):
    ys = []
    for j in range(S5_BLK):
        sl = slice(j * S5_BN, (j + 1) * S5_BN)
        ys.append(_dot1(sre_ref[:, sl], cblk_ref[j, :S5_BN, :], 1, 0) + _dot1(sim_ref[:, sl], cblk_ref[j, S5_BN:, :], 1, 0))
    return jnp.concatenate(ys, axis=1) + d * u


def _s5_fwd(u, bblk, cblk, abr, abi, d):
    length = u.shape[0]
    tl = min(256, length)
    nt = length // tl

    def body(u_ref, b_ref, c_ref, ar_ref, ai_ref, d_ref, ys_ref, kr_ref, ki_ref, sre, sim, car_re, car_im):
        @pl.when(pl.program_id(0) == 0)
        def _():
            car_re[...] = jnp.zeros_like(car_re)
            car_im[...] = jnp.zeros_like(car_im)

        kr_ref[0] = car_re[...]
        ki_ref[0] = car_im[...]
        u_ = u_ref[...]
        _s5_project_in(u_, b_ref, sre, sim)
        _s5_scan(sre, sim, car_re, car_im, ar_ref[...], ai_ref[...], tl, False)
        ys_ref[...] = _gelu(_s5_project_out(u_, d_ref[...], c_ref, sre, sim))

    full = lambda a: pl.BlockSpec(a.shape, lambda i, nd=a.ndim: (0,) * nd)
    chk = pl.BlockSpec((1, SUBLANES, S5N), lambda i: (i, 0, 0))
    return pl.pallas_call(
        body, grid=(nt,), in_specs=[pl.BlockSpec((tl, S5W), lambda i: (i, 0)), full(bblk), full(cblk), full(abr), full(abi), full(d)],
        out_specs=[pl.BlockSpec((tl, S5W), lambda i: (i, 0)), chk, chk],
        out_shape=[SDS((length, S5W), F32), SDS((nt, SUBLANES, S5N), F32), SDS((nt, SUBLANES, S5N), F32)],
        scratch_shapes=[pltpu.VMEM((tl, S5N), F32)] * 2 + [pltpu.VMEM((SUBLANES, S5N), F32)] * 2,
        compiler_params=_params(), name="s5_fwd")(u, bblk, cblk, abr, abi, d)


def _s5_bwd(u, dys_a, dys_b, kre, kim, bblk, cblk, abr, abi, d):
    length = u.shape[0]
    tl = min(256, length)
    nt = length // tl

    def body(u_ref, da_ref, db_ref, kr_ref, ki_ref, b_ref, c_ref, ar_ref, ai_ref, d_ref,
             du_ref, dB_ref, dC_ref, dar_ref, dai_ref, dd_ref, sre, sim, gre, gim, car_re, car_im, dcar_re, dcar_im):
        @pl.when(pl.program_id(0) == 0)
        def _():
            dcar_re[...] = jnp.zeros_like(dcar_re)
            dcar_im[...] = jnp.zeros_like(dcar_im)
            for ref in (dB_ref, dC_ref, dar_ref, dai_ref, dd_ref):
                ref[...] = jnp.zeros_like(ref)

        u_ = u_ref[...]
        abr_, abi_, d_ = ar_ref[...], ai_ref[...], d_ref[...]
        car_re[...] = kr_ref[0]
        car_im[...] = ki_ref[0]
        _s5_project_in(u_, b_ref, sre, sim)
        _s5_scan(sre, sim, car_re, car_im, abr_, abi_, tl, False)
        y = _s5_project_out(u_, d_, c_ref, sre, sim)
        _, vjp = jax.vjp(_gelu, y)
        dy = vjp(da_ref[...] + db_ref[...])[0]
        dd_ref[...] += jnp.sum(dy * u_, axis=0, keepdims=True)
        for j in range(S5_BLK):
            sl = slice(j * S5_BN, (j + 1) * S5_BN)
            dyj = dy[:, j * LANES:(j + 1) * LANES]
            gre[:, sl] = _dot1(dyj, c_ref[j, :S5_BN, :], 1, 1)
            gim[:, sl] = _dot1(dyj, c_ref[j, S5_BN:, :], 1, 1)
            dC_ref[j, :S5_BN, :] += _dot1(sre[:, sl], dyj, 0, 0)
            dC_ref[j, S5_BN:, :] += _dot1(sim[:, sl], dyj, 0, 0)
        _s5_scan(gre, gim, dcar_re, dcar_im, abr_, abi_, tl, True)
        gr, gi = gre[...], gim[...]
        pr = _shift_down(sre[...], kr_ref[0, 0:1, :])
        pi = _shift_down(sim[...], ki_ref[0, 0:1, :])
        dar_ref[...] += jnp.sum(gr * pr + gi * pi, axis=0, keepdims=True)
        dai_ref[...] += jnp.sum(gi * pr - gr * pi, axis=0, keepdims=True)
        dus = []
        for j in range(S5_BLK):
            sl = slice(j * S5_BN, (j + 1) * S5_BN)
            uj = u_[:, j * LANES:(j + 1) * LANES]
            dus.append(_dot1(gre[:, sl], b_ref[j, :, :S5_BN], 1, 1) + _dot1(gim[:, sl], b_ref[j, :, S5_BN:], 1, 1))
            dB_ref[j, :, :S5_BN] += _dot1(uj, gre[:, sl], 0, 0)
            dB_ref[j, :, S5_BN:] += _dot1(uj, gim[:, sl], 0, 0)
        du_ref[...] = (jnp.concatenate(dus, axis=1) + d_ * dy).astype(du_ref.dtype)

    rmap = lambda i: nt - 1 - i
    full = lambda a: pl.BlockSpec(a.shape, lambda i, nd=a.ndim: (0,) * nd)
    row = pl.BlockSpec((tl, S5W), lambda i: (rmap(i), 0))
    chk = pl.BlockSpec((1, SUBLANES, S5N), lambda i: (rmap(i), 0, 0))
    return pl.pallas_call(
        body, grid=(nt,), in_specs=[row, row, row, chk, chk, full(bblk), full(cblk), full(abr), full(abi), full(d)],
        out_specs=[row, full(bblk), full(cblk), full(abr), full(abi), full(d)],
        out_shape=[SDS((length, S5W), BF16), SDS(bblk.shape, F32), SDS(cblk.shape, F32), SDS(abr.shape, F32),
                   SDS(abi.shape, F32), SDS(d.shape, F32)],
        scratch_shapes=[pltpu.VMEM((tl, S5N), F32)] * 4 + [pltpu.VMEM((SUBLANES, S5N), F32)] * 4,
        compiler_params=_params(), name="s5_bwd")(u, dys_a, dys_b, kre, kim, bblk, cblk, abr, abi, d)


def _s5_blockdiag_in(bb):
    t = bb.reshape(S5C, S5_BLK, S5_BN).transpose(1, 0, 2)
    t = jnp.tile(t, (1, SUBLANES, 1))
    mask = (lax.broadcasted_iota(jnp.int32, (LANES, S5_BN), 0) // S5C) == (lax.broadcasted_iota(jnp.int32, (LANES, S5_BN), 1) // S5P)
    return jnp.where(mask[None], t, 0.0)


def _s5_blockdiag_in_t(dblk):
    t = dblk.reshape(S5_BLK, SUBLANES, S5C, SUBLANES, S5P)
    t = jnp.diagonal(t, axis1=1, axis2=3)
    return t.transpose(1, 0, 3, 2).reshape(S5C, S5N)


def _s5_blockdiag_out(c):
    t = c.reshape(S5_BLK, SUBLANES, S5C, S5P).transpose(0, 3, 1, 2).reshape(S5_BLK, S5P, LANES)
    t = jnp.tile(t, (1, SUBLANES, 1))
    mask = (lax.broadcasted_iota(jnp.int32, (S5_BN, LANES), 0) // S5P) == (lax.broadcasted_iota(jnp.int32, (S5_BN, LANES), 1) // S5C)
    return jnp.where(mask[None], t, 0.0)


def _s5_blockdiag_out_t(dblk):
    t = dblk.reshape(S5_BLK, SUBLANES, S5P, SUBLANES, S5C)
    t = jnp.diagonal(t, axis1=1, axis2=3)
    return t.transpose(0, 3, 2, 1).reshape(S5G, S5C, S5P)


def _wkv_consts():
    lane = lax.broadcasted_iota(jnp.int32, (4 * HN, LANES), 1)
    row = lax.broadcasted_iota(jnp.int32, (4 * HN, LANES), 0)
    diag = ((lane % HN) == (row % HN)).astype(F32)
    ones = _block_ones(LANES, HN).astype(BF16)
    return lane % HN, diag, jnp.concatenate([ones, ones], axis=0)


def _wkv_split(p):
    hi = p.astype(BF16)
    lo = (p - hi.astype(F32)).astype(BF16)
    return jnp.concatenate([hi, lo], axis=1)


def _wkv_rowseg(p, ones):
    return jnp.dot(_wkv_split(p), ones, preferred_element_type=F32)


def _wkv_full(rows_ref, t):
    q = jnp.broadcast_to(rows_ref[t], (SUBLANES, RW))
    return jnp.concatenate([jnp.tile(q[:, hp * LANES:(hp + 1) * LANES], (SUBLANES, 1)) for hp in range(4)], axis=0)


def _wkv_colsum8(x):
    return jnp.concatenate([x[hp * HN:(hp + 1) * HN].reshape(SUBLANES, SUBLANES, LANES).sum(axis=0) for hp in range(4)], axis=1)


def _wkv_prebroadcast(src_refs, dst_refs, n):
    for s, dref in zip(src_refs, dst_refs, strict=True):
        dref[...] = s[...].reshape(n, 1, RW)


def _wkv_pair_sums(lhs_ref, out_ref):
    n, ns = out_ref.shape[0], out_ref.shape[1]
    ones_pair = _block_ones(2 * LANES, HN).astype(BF16)
    out_ref[...] = jnp.dot(lhs_ref[0:n].reshape(n * ns, 2 * LANES), ones_pair,
                           preferred_element_type=F32).reshape(n, ns, 2 * LANES)


def _wkv_transpose_out(acc, c, diag):
    lane = lax.broadcasted_iota(jnp.int32, (c, LANES), 1)
    outs = []
    for hp in range(4):
        zp = lax.dot_general(acc[hp * HN:(hp + 1) * HN], diag[:HN], (((0,), (0,)), ((), ())), precision=HI,
                             preferred_element_type=F32)
        outs.append(jnp.where(lane < HN, zp[:c], zp[HN:HN + c]))
    return jnp.concatenate(outs, axis=1)


def _wkv_fwd(r, w, k, v, a, b):
    length = r.shape[0]
    c = min(WKV_CHUNK, length)
    nc = length // c
    ns = 4 * HN

    def body(r_ref, w_ref, k_ref, v_ref, a_ref, b_ref, y_ref, sst_ref, sast_ref, vst_ref, fin_ref,
             s_ref, acc_ref, lhs_ref, ysum_ref, rb, wb, kb, vb, ab, bb, wab, beb, kab):
        @pl.when(pl.program_id(0) == 0)
        def _():
            s_ref[...] = jnp.zeros_like(s_ref)

        a_next = pltpu.roll(a_ref[...], c - 1, 0)
        o512 = _block_ones(RW, HN)
        _wkv_prebroadcast((r_ref, w_ref, k_ref, v_ref, a_ref, b_ref, w_ref[...] * a_next,
                           _segsum_impl(b_ref[...] * a_next, o512), _segsum_impl(k_ref[...] * a_next, o512)),
                          (rb, wb, kb, vb, ab, bb, wab, beb, kab), c)
        lane_t, diag, ones = _wkv_consts()

        def fill_v(q, carry):
            for half in range(2):
                lhs_ref[q, :, half * LANES:(half + 1) * LANES] = (_wkv_full(vb, 2 * q + half) * diag).astype(BF16)
            return carry

        lax.fori_loop(0, c // 2, fill_v, 0)
        _wkv_pair_sums(lhs_ref, vst_ref)

        def pair(p, carry):
            t0, t1 = 2 * p, 2 * p + 1
            s0 = s_ref[...]
            both = _wkv_rowseg(jnp.concatenate([s0 * _wkv_full(ab, t0), s0 * _wkv_full(wab, t0)], axis=0), ones)
            v_pair = vst_ref[p]
            sa0, v0 = both[:ns], v_pair[:, :LANES]
            s1 = s0 * _wkv_full(wb, t0) + sa0 * _wkv_full(bb, t0) + v0 * _wkv_full(kb, t0)
            sa1 = both[ns:] + sa0 * _wkv_full(beb, t0) + v0 * _wkv_full(kab, t0)
            s2 = s1 * _wkv_full(wb, t1) + sa1 * _wkv_full(bb, t1) + v_pair[:, LANES:] * _wkv_full(kb, t1)
            sst_ref[t0] = s0
            sast_ref[t0] = sa0
            sst_ref[t1] = s1
            sast_ref[t1] = sa1
            s_ref[...] = s2
            lhs_ref[p, :, 0:LANES] = (s1 * _wkv_full(rb, t0)).astype(BF16)
            lhs_ref[p, :, LANES:2 * LANES] = (s2 * _wkv_full(rb, t1)).astype(BF16)
            return carry

        lax.fori_loop(0, c // 2, pair, 0)
        _wkv_pair_sums(lhs_ref, ysum_ref)
        acc_ref[...] = jnp.zeros_like(acc_ref)

        def gather_y(p, carry):
            both = ysum_ref[p]
            acc = jnp.where(lane_t == 2 * p, both[:, :LANES], acc_ref[...])
            acc_ref[...] = jnp.where(lane_t == 2 * p + 1, both[:, LANES:], acc)
            return carry

        lax.fori_loop(0, c // 2, gather_y, 0)
        y_ref[...] = _wkv_transpose_out(acc_ref[...], c, diag)
        fin_ref[...] = s_ref[...]

    row = pl.BlockSpec((c, RW), lambda i: (i, 0))
    st = pl.BlockSpec((c, ns, LANES), lambda i: (i, 0, 0))
    state = pltpu.VMEM((ns, LANES), F32)
    st_pair = pl.BlockSpec((c // 2, ns, 2 * LANES), lambda i: (i, 0, 0))
    return pl.pallas_call(
        body, grid=(nc,), in_specs=[row] * 6,
        out_specs=[row, st, st, st_pair, pl.BlockSpec((ns, LANES), lambda i: (0, 0))],
        out_shape=[SDS((length, RW), F32)] + [SDS((length, ns, LANES), F32)] * 2 + [SDS((length // 2, ns, 2 * LANES), F32)]
        + [SDS((ns, LANES), F32)],
        scratch_shapes=[state] * 2 + [pltpu.VMEM((c, ns, 2 * LANES), BF16), pltpu.VMEM((c // 2, ns, 2 * LANES), F32)]
        + [pltpu.VMEM((c, 1, RW), F32)] * 9,
        compiler_params=_params(), name="wkv_fwd")(r, w, k, v, a, b)


def _wkv_bwd(r, w, k, a, b, dy, sst, sast, vst, fin):
    length = r.shape[0]
    c = min(WKV_CHUNK, length)
    nc = length // c
    ns = 4 * HN

    def body(r_ref, w_ref, k_ref, a_ref, b_ref, dy_ref, sst_ref, sast_ref, vst_ref, snext_ref, fin_ref,
             dr_ref, dw_ref, dk_ref, da_ref, db_ref, dv_ref,
             ds_ref, cur_ref, acc_ref, dyst, lhs_ref, dvsum_ref, rb, wb, kb, ab, bb, dyb, wbb, alb, rhb,
             p_r, p_w, p_k, p_a, p_b):
        @pl.when(pl.program_id(0) == 0)
        def _():
            ds_ref[...] = jnp.zeros_like(ds_ref)

        b_prev = pltpu.roll(b_ref[...], 1, 0)
        o512 = _block_ones(RW, HN)
        _wkv_prebroadcast((r_ref, w_ref, k_ref, a_ref, b_ref, dy_ref, w_ref[...] * b_prev,
                           _segsum_impl(a_ref[...] * b_prev, o512), _segsum_impl(r_ref[...] * b_ref[...], o512)),
                          (rb, wb, kb, ab, bb, dyb, wbb, alb, rhb), c)
        lane_t, diag, ones = _wkv_consts()

        def fill(q, carry):
            for half in range(2):
                lhs_ref[q, :, half * LANES:(half + 1) * LANES] = (_wkv_full(dyb, 2 * q + half) * diag).astype(BF16)
            return carry

        lax.fori_loop(0, c // 2, fill, 0)
        _wkv_pair_sums(lhs_ref, dyst)
        cur_ref[...] = jnp.where(pl.program_id(0) == 0, fin_ref[...], snext_ref[0])

        def pair(p, carry):
            ta = c - 1 - 2 * p
            tb = ta - 1
            s_a, s_b, s_c = cur_ref[...], sst_ref[ta], sst_ref[tb]
            dy_pair = dyst[c // 2 - 1 - p]
            dy_a, dy_b = dy_pair[:, LANES:], dy_pair[:, :LANES]
            ds_a = ds_ref[...] + dy_a * _wkv_full(rb, ta)
            both = _wkv_rowseg(jnp.concatenate([ds_a * _wkv_full(bb, ta), ds_a * _wkv_full(wbb, ta)], axis=0), ones)
            dsa_a = both[:ns]
            ds_b = ds_a * _wkv_full(wb, ta) + dsa_a * _wkv_full(ab, ta) + dy_b * _wkv_full(rb, tb)
            dsa_b = both[ns:] + dsa_a * _wkv_full(alb, ta) + dy_b * _wkv_full(rhb, tb)
            ds_ref[...] = ds_b * _wkv_full(wb, tb) + dsa_b * _wkv_full(ab, tb)
            cur_ref[...] = s_c
            lhs_ref[p, :, 0:LANES] = (ds_a * _wkv_full(kb, ta)).astype(BF16)
            lhs_ref[p, :, LANES:2 * LANES] = (ds_b * _wkv_full(kb, tb)).astype(BF16)
            v_pair = vst_ref[c // 2 - 1 - p]
            for t, s_t, s_prev, ds, dsa, dyy, vv in ((ta, s_a, s_b, ds_a, dsa_a, dy_a, v_pair[:, LANES:]),
                                                     (tb, s_b, s_c, ds_b, dsa_b, dy_b, v_pair[:, :LANES])):
                p_r[t] = _wkv_colsum8(s_t * dyy)
                p_k[t] = _wkv_colsum8(ds * vv)
                p_b[t] = _wkv_colsum8(ds * sast_ref[t])
                p_w[t] = _wkv_colsum8(ds * s_prev)
                p_a[t] = _wkv_colsum8(s_prev * dsa)
            return carry

        lax.fori_loop(0, c // 2, pair, 0)
        _wkv_pair_sums(lhs_ref, dvsum_ref)
        acc_ref[...] = jnp.zeros_like(acc_ref)

        def gather_dv(p, carry):
            both = dvsum_ref[p]
            acc = jnp.where(lane_t == c - 1 - 2 * p, both[:, :LANES], acc_ref[...])
            acc_ref[...] = jnp.where(lane_t == c - 2 - 2 * p, both[:, LANES:], acc)
            return carry

        lax.fori_loop(0, c // 2, gather_dv, 0)
        sel = (lax.broadcasted_iota(jnp.int32, (c, c * SUBLANES), 1) // SUBLANES
               == lax.broadcasted_iota(jnp.int32, (c, c * SUBLANES), 0)).astype(F32)
        for out_ref, part in ((dr_ref, p_r), (dw_ref, p_w), (dk_ref, p_k), (da_ref, p_a), (db_ref, p_b)):
            out_ref[...] = _dot3(sel, part[...].reshape(c * SUBLANES, RW), 1, 0)
        dv_ref[...] = _wkv_transpose_out(acc_ref[...], c, diag)

    rmap = lambda i: nc - 1 - i
    row = pl.BlockSpec((c, RW), lambda i: (rmap(i), 0))
    st = pl.BlockSpec((c, ns, LANES), lambda i: (rmap(i), 0, 0))
    nxt = pl.BlockSpec((1, ns, LANES), lambda i: (jnp.minimum((rmap(i) + 1) * c, length - 1), 0, 0))
    state = pltpu.VMEM((ns, LANES), F32)
    st_pair = pl.BlockSpec((c // 2, ns, 2 * LANES), lambda i: (rmap(i), 0, 0))
    return pl.pallas_call(
        body, grid=(nc,), in_specs=[row] * 6 + [st, st, st_pair, nxt, pl.BlockSpec((ns, LANES), lambda i: (0, 0))],
        out_specs=[row] * 6, out_shape=[SDS((length, RW), F32)] * 6,
        scratch_shapes=[state] * 3 + [pltpu.VMEM((c // 2, ns, 2 * LANES), F32), pltpu.VMEM((c, ns, 2 * LANES), BF16),
                                       pltpu.VMEM((c // 2, ns, 2 * LANES), F32)]
        + [pltpu.VMEM((c, 1, RW), F32)] * 9 + [pltpu.VMEM((c, SUBLANES, RW), F32)] * 5,
        compiler_params=_params(), name="wkv_bwd")(r, w, k, a, b, dy, sst, sast, vst, sst, fin)


def _my_place():
    return lax.axis_index("x"), lax.axis_index("y"), lax.axis_index("c")


def _peer(x, y, c, k):
    return (x ^ ((k >> 2) & 1), y ^ ((k >> 1) & 1), c ^ (k & 1))


def _all_gather(shard, name):
    rows = shard.shape[0]

    def body(in_ref, out_ref, send_sems, recv_sems, local_sem):
        x, y, c = _my_place()
        me = 4 * x + 2 * y + c
        mine = pltpu.make_async_copy(in_ref, out_ref.at[me], local_sem)
        mine.start()
        sends = []
        for k in range(1, N_DEV):
            cp = pltpu.make_async_remote_copy(src_ref=in_ref, dst_ref=out_ref.at[me], send_sem=send_sems.at[k - 1],
                                              recv_sem=recv_sems.at[k - 1], device_id=_peer(x, y, c, k), device_id_type=MESH)
            cp.start()
            sends.append(cp)
        for k in range(1, N_DEV):
            px, py, pc = _peer(x, y, c, k)
            pltpu.make_async_remote_copy(src_ref=in_ref, dst_ref=out_ref.at[4 * px + 2 * py + pc], send_sem=send_sems.at[k - 1],
                                         recv_sem=recv_sems.at[k - 1], device_id=(px, py, pc), device_id_type=MESH).wait_recv()
        for cp in sends:
            cp.wait_send()
        mine.wait()

    return pl.pallas_call(
        body, out_shape=SDS((N_DEV, rows, LANES), shard.dtype),
        in_specs=[pl.BlockSpec(memory_space=pl.ANY)], out_specs=pl.BlockSpec(memory_space=pl.ANY),
        scratch_shapes=[pltpu.SemaphoreType.DMA((N_DEV - 1,)), pltpu.SemaphoreType.DMA((N_DEV - 1,)), pltpu.SemaphoreType.DMA],
        name=name)(shard)


_HBM_SPEC = pl.BlockSpec(memory_space=pltpu.HBM)
_SEM_SPEC = pl.BlockSpec(memory_space=pltpu.SEMAPHORE)
_DATAFLOW = pltpu.SideEffectType.DATAFLOW_SIDE_EFFECTING


def _exchange_start(srcs, per_peer, name):
    n = len(srcs)
    lands = [jnp.zeros((N_DEV, *s.shape[-2:]), s.dtype) for s in srcs]

    def body(*refs):
        src_refs, land_refs = refs[:n], refs[n:2 * n]
        send_sems, recv_sems, token = refs[2 * n], refs[2 * n + 1], refs[-1]
        x, y, c = _my_place()
        me = 4 * x + 2 * y + c
        for a in range(n):
            for k in range(1, N_DEV):
                px, py, pc = _peer(x, y, c, k)
                mine = src_refs[a].at[4 * px + 2 * py + pc] if per_peer else src_refs[a]
                pltpu.make_async_remote_copy(src_ref=mine, dst_ref=land_refs[a].at[me], send_sem=send_sems.at[7 * a + k - 1],
                                             recv_sem=recv_sems.at[7 * a + k - 1], device_id=(px, py, pc),
                                             device_id_type=MESH).start()
        token[...] = jnp.zeros_like(token)

    n_sem = (N_DEV - 1) * n
    res = pl.pallas_call(
        body, name=name,
        out_shape=(pltpu.SemaphoreType.DMA((n_sem,)), pltpu.SemaphoreType.DMA((n_sem,)),
                   *[pltpu.HBM(a.shape, a.dtype) for a in srcs + lands], SDS((SUBLANES, LANES), F32)),
        in_specs=(_HBM_SPEC,) * (2 * n),
        out_specs=(_SEM_SPEC, _SEM_SPEC) + (_HBM_SPEC,) * (2 * n) + (pl.BlockSpec(memory_space=pltpu.VMEM),),
        input_output_aliases={i: 2 + i for i in range(2 * n)}, compiler_params=pltpu.CompilerParams(has_side_effects=_DATAFLOW),
    )(*[pltpu.with_memory_space_constraint(a, pltpu.HBM) for a in srcs + lands])
    return res[0], res[1], list(res[2:2 + n]), list(res[2 + n:2 + 2 * n]), res[-1]


def _exchange_wait(started, after, per_peer, name):
    send_sems, recv_sems, src_thrus, land_thrus, _ = started
    n = len(src_thrus)

    def body(*refs):
        src_refs, land_refs = refs[:n], refs[n:2 * n]
        send_sems, recv_sems = refs[2 * n], refs[2 * n + 1]
        x, y, c = _my_place()
        me = 4 * x + 2 * y + c
        for a in range(n):
            for k in range(1, N_DEV):
                px, py, pc = _peer(x, y, c, k)
                mine = src_refs[a].at[me] if per_peer else src_refs[a]
                copy = pltpu.make_async_remote_copy(src_ref=mine, dst_ref=land_refs[a].at[4 * px + 2 * py + pc],
                                                    send_sem=send_sems.at[7 * a + k - 1], recv_sem=recv_sems.at[7 * a + k - 1],
                                                    device_id=(px, py, pc), device_id_type=MESH)
                copy.wait_send()
                copy.wait_recv()

    res = pl.pallas_call(
        body, name=name, out_shape=tuple(pltpu.HBM(a.shape, a.dtype) for a in src_thrus + land_thrus),
        in_specs=(_HBM_SPEC,) * (2 * n) + (_SEM_SPEC, _SEM_SPEC, pl.BlockSpec(memory_space=pl.ANY)),
        out_specs=(_HBM_SPEC,) * (2 * n), input_output_aliases={i: i for i in range(2 * n)},
        compiler_params=pltpu.CompilerParams(has_side_effects=_DATAFLOW),
    )(*src_thrus, *land_thrus, send_sems, recv_sems, after)
    return list(res[n:])


def _row_tile(rows, cap):
    best = None
    for t in range(2 * SUBLANES, min(rows, cap) + 1, 2 * SUBLANES):
        if rows % t == 0:
            best = t
    return best or rows


def _sum_parts(parts, name, own=None):
    rows, cols = parts.shape[1:]

    def body(*refs):
        p_ref, g_out = refs[0], refs[-1]
        g = p_ref[0].astype(F32)
        for s in range(1, N_DEV):
            g = g + p_ref[s].astype(F32)
        if own is not None:
            g = g + refs[1][...].astype(F32)
        g_out[...] = g

    tr = _row_tile(rows, 384 * 1024 // cols)
    blk = pl.BlockSpec((tr, cols), lambda i: (i, 0))
    extra = [] if own is None else [own]
    return pl.pallas_call(
        body, grid=(rows // tr,),
        in_specs=[pl.BlockSpec((N_DEV, tr, cols), lambda i: (0, i, 0))] + [blk] * len(extra),
        out_specs=blk, out_shape=SDS((rows, cols), F32),
        compiler_params=_params(("parallel",)), name=name)(parts, *extra)


def _adamw(g, wgt, m, v, name):
    rows, cols = wgt.shape

    def body(g_ref, w_ref, m_ref, v_ref, d_out, m_out, v_out):
        g_ = g_ref[...]
        m_new = ADAM_B1 * m_ref[...] + (1.0 - ADAM_B1) * g_
        v_new = ADAM_B2 * v_ref[...] + (1.0 - ADAM_B2) * (g_ * g_)
        m_hat = m_new / (1.0 - ADAM_B1 ** ADAM_STEP)
        v_hat = v_new / (1.0 - ADAM_B2 ** ADAM_STEP)
        d_out[...] = -ADAM_LR * (m_hat / (jnp.sqrt(v_hat) + ADAM_EPS) + ADAM_WD * w_ref[...])
        m_out[...] = m_new
        v_out[...] = v_new

    tr = _row_tile(rows, 512 * 1024 // cols)
    blk = pl.BlockSpec((tr, cols), lambda i: (i, 0))
    return pl.pallas_call(
        body, grid=(rows // tr,), in_specs=[blk] * 4, out_specs=[blk] * 3, out_shape=[SDS((rows, cols), F32)] * 3,
        compiler_params=_params(("parallel",)), name=name)(g, wgt, m, v)


PACK_ALIGN = 2 * SUBLANES * LANES
PACK_ROWS = 512

ROW_SHARDED = ("s5_w_glu", "w_out", "ffn_w_down")
TRANSPOSED = ("w_in", "ffn_w_up", "w_branch_rwkv", "w_branch_s5")
SMALL_SHARDED = ("rwkv_w2", "rwkv_a2", "rwkv_g2", "ffn_conv_w")
SHARDED = ROW_SHARDED + TRANSPOSED + SMALL_SHARDED
F32_GATHER = ("ffn_conv_w",)
REPLICATED =("norm_mix_pre", "norm_mix_post", "norm_ffn_pre", "norm_ffn_post", "b_gate", "rwkv_shift_mu", "rwkv_w0",
              "rwkv_a0", "rwkv_k_k", "rwkv_k_a", "rwkv_r_k", "rwkv_lnx_w", "rwkv_lnx_b", "s5_a_re", "s5_a_im", "s5_b_re",
              "s5_b_im", "s5_c_re", "s5_c_im", "s5_d", "s5_log_step", "s5_b_glu", "ffn_conv_b")
WEIGHTS = ("norm_mix_pre", "norm_mix_post", "norm_ffn_pre", "norm_ffn_post", "w_in", "b_gate", "rwkv_shift_mu", "rwkv_w0",
           "rwkv_w2", "rwkv_a0", "rwkv_a2", "rwkv_g2", "rwkv_k_k", "rwkv_k_a", "rwkv_r_k", "rwkv_lnx_w", "rwkv_lnx_b",
           "s5_a_re", "s5_a_im", "s5_b_re", "s5_b_im", "s5_c_re", "s5_c_im", "s5_d", "s5_log_step", "s5_w_glu", "s5_b_glu",
           "w_branch_rwkv", "w_branch_s5", "w_out", "ffn_w_up", "ffn_conv_w", "ffn_conv_b", "ffn_w_down")


def _pad_flat(a):
    flat = a.reshape(-1)
    pad = (-flat.shape[0]) % PACK_ALIGN
    return jnp.pad(flat, (0, pad)) if pad else flat


def _pad_cols(a, mult):
    pad = (-a.shape[1]) % mult
    return jnp.pad(a, ((0, 0), (0, pad))) if pad else a


def _pack(pieces):
    flat = jnp.concatenate([_pad_flat(p) for p in pieces])
    pad = (-flat.shape[0]) % (PACK_ROWS * LANES)
    return (jnp.pad(flat, (0, pad)) if pad else flat).reshape(-1, LANES)


def _unpack(buf, shapes):
    lead = buf.shape[:-2]
    flat = buf.reshape(*lead, -1)
    out, off = [], 0
    for s in shapes:
        n = math.prod(s)
        out.append(flat[..., off:off + n].reshape(*lead, *s))
        off += n + (-n) % PACK_ALIGN
    return out


def _rows_to_full(stack):
    return stack.reshape(-1, stack.shape[2])


def _cols_to_full(stack):
    return stack.transpose(1, 0, 2).reshape(stack.shape[1], -1)


EARLY_GRADS = ("ffn_w_down", "ffn_w_up", "w_out")
REST_GRADS = ("s5_w_glu", "w_in", "w_branch_rwkv", "w_branch_s5")


def _local_step(x, tgt, wt, get_late, send_early, send_rest):
    wt = dict(wt)
    o512 = _block_ones(RW, HN)
    w_in = wt["w_in"]
    w_rw, w_u, w_g = w_in[:N_RWKV], w_in[N_RWKV:N_RWKV + S5W], w_in[N_RWKV + S5W:]
    b_gate = wt["b_gate"]
    b_r, b_s = b_gate[:, :D_MODEL], b_gate[:, D_MODEL:]
    zpad = lambda a, lo, n: jnp.pad(a.astype(F32), ((lo, n - lo - a.shape[0]), (0, 0)))
    w2p, a2p, g2p = zpad(wt["rwkv_w2"], 0, 256), zpad(wt["rwkv_a2"], 64, 256), zpad(wt["rwkv_g2"], 128, 256)
    pre_small = [wt["rwkv_w0"], wt["rwkv_a0"], wt["rwkv_k_k"], wt["rwkv_k_a"], w2p, a2p, g2p]
    post_prm = [wt["rwkv_lnx_w"], wt["rwkv_lnx_b"], wt["rwkv_r_k"].reshape(1, RW)]
    mu = wt["rwkv_shift_mu"]

    a_re, a_im = wt["s5_a_re"].reshape(1, S5N), wt["s5_a_im"].reshape(1, S5N)
    ls = jnp.repeat(wt["s5_log_step"].reshape(S5G), S5P).reshape(1, S5N)
    b_re_t = wt["s5_b_re"].reshape(S5N, S5C).T
    b_im_t = wt["s5_b_im"].reshape(S5N, S5C).T
    c_re, c_im = wt["s5_c_re"].reshape(S5G, S5C, S5P), wt["s5_c_im"].reshape(S5G, S5C, S5P)
    abr, abi, bbr, bbi = _s5_prep_fwd(a_re, a_im, ls, b_re_t, b_im_t)
    bblk = jnp.concatenate([_s5_blockdiag_in(bbr), _s5_blockdiag_in(bbi)], axis=2)
    cblk = jnp.concatenate([_s5_blockdiag_out(c_re), -_s5_blockdiag_out(c_im)], axis=1)
    s5_d = wt["s5_d"]

    h1 = _rms_fwd(x, wt["norm_mix_pre"], "rms1_fwd")
    p_rw = _mm(h1, w_rw, tb=True, name="mm_proj_rwkv")
    u = _mm(h1, w_u, tb=True, name="mm_proj_s5")
    gpre = _mm(h1, w_g, tb=True, name="mm_proj_gate")
    r, decay, k2, v, aa, bb, g = _rwkv_pre_fwd(p_rw, mu, pre_small, o512)
    y, sst, sast, vst, s_fin = _wkv_fwd(r, decay, k2, v, aa, bb)
    wt.update(get_late(y))
    y_r = _rwkv_post_fwd(y, r, k2, v, g, post_prm, o512)
    o_r = _mm(y_r, wt["w_branch_rwkv"], tb=True, name="mm_branch_rwkv")
    ys, kre, kim = _s5_fwd(u, bblk, cblk, abr, abi, s5_d)
    t_glu = _mm(ys, wt["s5_w_glu"], name="mm_glu")
    out5 = _glu_fwd(ys, t_glu, wt["s5_b_glu"])
    o_s = _mm(out5, wt["w_branch_s5"], tb=True, name="mm_branch_s5")
    mi = _merge_fwd(gpre, o_r, o_s, b_r, b_s)
    mixed = _mm(mi, wt["w_out"], name="mm_out")
    x2, h2 = _mid_fwd(x, mixed, wt["norm_mix_post"], wt["norm_ffn_pre"])
    z = _mm(h2, wt["ffn_w_up"], tb=True, name="mm_up")
    act = _conv_fwd(z, wt["ffn_conv_w"], wt["ffn_conv_b"])
    f = _mm(act, wt["ffn_w_down"], name="mm_down")
    dx3, df, d_norm_ffn_post, loss_part = _final(x2, f, tgt, wt["norm_ffn_post"])

    gr = {"norm_ffn_post": d_norm_ffn_post}
    dact = _mm(df, wt["ffn_w_down"], tb=True, name="mm_down_dx")
    gr["ffn_w_down"] = _mm(act, df, ta=True, out_dtype=BF16, name="mm_down_dw")
    dzg, dzv, gr["ffn_conv_w"], gr["ffn_conv_b"] = _conv_bwd(z, dact, wt["ffn_conv_w"], wt["ffn_conv_b"])
    dz = jnp.concatenate([dzg, dzv], axis=1)
    dh2 = _mm(dz, wt["ffn_w_up"], name="mm_up_dx")
    gr["ffn_w_up"] = _mm(dz, h2, ta=True, out_dtype=BF16, name="mm_up_dw")
    dx2, dmixed, gr["norm_mix_post"], gr["norm_ffn_pre"] = _mid_bwd(x2, mixed, dh2, dx3, wt["norm_mix_post"], wt["norm_ffn_pre"])
    dmi = _mm(dmixed, wt["w_out"], tb=True, name="mm_out_dx")
    gr["w_out"] = _mm(mi, dmixed, ta=True, out_dtype=BF16, name="mm_out_dw")
    token = send_early({n: gr[n] for n in EARLY_GRADS})
    b_r = b_r + jnp.tile(token[0:1, :], (1, D_MODEL // LANES))
    dgp_r, dgp_s, do_r, do_s, db_r, db_s = _merge_bwd(gpre, o_r, o_s, dmi, b_r, b_s)
    gr["b_gate"] = jnp.concatenate([db_r, db_s], axis=1)
    dout5 = _mm(do_s, wt["w_branch_s5"], name="mm_branch_s5_dx")
    gr["w_branch_s5"] = _mm(do_s, out5, ta=True, out_dtype=BF16, name="mm_branch_s5_dw")
    dys_a, dt_glu, gr["s5_b_glu"] = _glu_bwd(ys, t_glu, dout5, wt["s5_b_glu"])
    dys_b = _mm(dt_glu, wt["s5_w_glu"], tb=True, name="mm_glu_dx")
    gr["s5_w_glu"] = _mm(ys, dt_glu, ta=True, out_dtype=BF16, name="mm_glu_dw")
    du, dbblk, dcblk, dabr, dabi, gr["s5_d"] = _s5_bwd(u, dys_a, dys_b, kre, kim, bblk, cblk, abr, abi, s5_d)
    gr["s5_c_re"] = _s5_blockdiag_out_t(dcblk[:, :S5_BN, :]).reshape(wt["s5_c_re"].shape)
    gr["s5_c_im"] = (-_s5_blockdiag_out_t(dcblk[:, S5_BN:, :])).reshape(wt["s5_c_im"].shape)
    dbbr, dbbi = _s5_blockdiag_in_t(dbblk[:, :, :S5_BN]), _s5_blockdiag_in_t(dbblk[:, :, S5_BN:])
    gsel = (lax.broadcasted_iota(jnp.int32, (S5N, LANES), 0) // S5P == lax.broadcasted_iota(jnp.int32, (S5N, LANES), 1)).astype(F32)
    d_are, d_aim, d_ls, d_bre_t, d_bim_t = _s5_prep_bwd(a_re, a_im, ls, b_re_t, b_im_t, (dabr, dabi, dbbr, dbbi), gsel)
    gr["s5_a_re"] = d_are.reshape(wt["s5_a_re"].shape)
    gr["s5_a_im"] = d_aim.reshape(wt["s5_a_im"].shape)
    gr["s5_log_step"] = d_ls[0:1, :S5G]
    gr["s5_b_re"] = d_bre_t.T.reshape(wt["s5_b_re"].shape)
    gr["s5_b_im"] = d_bim_t.T.reshape(wt["s5_b_im"].shape)
    dy_r = _mm(do_r, wt["w_branch_rwkv"], name="mm_branch_rwkv_dx")
    gr["w_branch_rwkv"] = _mm(do_r, y_r, ta=True, out_dtype=BF16, name="mm_branch_rwkv_dw")
    dy, dr1, dk1, dv1, dg, gr["rwkv_lnx_w"], gr["rwkv_lnx_b"], d_rk = _rwkv_post_bwd(y, r, k2, v, g, dy_r, post_prm, o512)
    gr["rwkv_r_k"] = d_rk.reshape(wt["rwkv_r_k"].shape)
    dr2, ddecay, dk2, daa, dbb, dv2 = _wkv_bwd(r, decay, k2, aa, bb, dy, sst, sast, vst, s_fin)
    cots = [dr1, ddecay, dk1, dv1, daa, dbb, dg, dr2, dk2, dv2]
    dp_rw, gr["rwkv_shift_mu"], gr["rwkv_w0"], gr["rwkv_a0"], gr["rwkv_k_k"], gr["rwkv_k_a"], dw2p, da2p, dg2p = \
        _rwkv_pre_bwd(p_rw, cots, mu, pre_small, o512)
    gr["rwkv_w2"], gr["rwkv_a2"], gr["rwkv_g2"] = dw2p[:64], da2p[64:128], dg2p[128:]
    dproj = jnp.concatenate([dp_rw, du, dgp_r, dgp_s], axis=1)
    gr["w_in"] = _mm(dproj, h1, ta=True, out_dtype=BF16, name="mm_proj_dw")
    token = send_rest(gr)
    dh1 = _mm(dproj, w_in + token[0, 0].astype(w_in.dtype), name="mm_proj_dx")
    dx, gr["norm_mix_pre"] = _rms_bwd(x, wt["norm_mix_pre"], dh1, dx2, "rms1_bwd")
    return loss_part[0, 0], dx, gr


def kernel(x, norm_mix_pre, norm_mix_post, norm_ffn_pre, norm_ffn_post, w_in, b_gate, rwkv_shift_mu, rwkv_w0, rwkv_w2, rwkv_a0, rwkv_a2, rwkv_g2, rwkv_k_k, rwkv_k_a, rwkv_r_k, rwkv_lnx_w, rwkv_lnx_b, s5_a_re, s5_a_im, s5_b_re, s5_b_im, s5_c_re, s5_c_im, s5_d, s5_log_step, s5_w_glu, s5_b_glu, w_branch_rwkv, w_branch_s5, w_out, ffn_w_up, ffn_conv_w, ffn_conv_b, ffn_w_down, loss_target, m_norm_mix_pre, m_norm_mix_post, m_norm_ffn_pre, m_norm_ffn_post, m_w_in, m_b_gate, m_rwkv_shift_mu, m_rwkv_w0, m_rwkv_w2, m_rwkv_a0, m_rwkv_a2, m_rwkv_g2, m_rwkv_k_k, m_rwkv_k_a, m_rwkv_r_k, m_rwkv_lnx_w, m_rwkv_lnx_b, m_s5_a_re, m_s5_a_im, m_s5_b_re, m_s5_b_im, m_s5_c_re, m_s5_c_im, m_s5_d, m_s5_log_step, m_s5_w_glu, m_s5_b_glu, m_w_branch_rwkv, m_w_branch_s5, m_w_out, m_ffn_w_up, m_ffn_conv_w, m_ffn_conv_b, m_ffn_w_down, v_norm_mix_pre, v_norm_mix_post, v_norm_ffn_pre, v_norm_ffn_post, v_w_in, v_b_gate, v_rwkv_shift_mu, v_rwkv_w0, v_rwkv_w2, v_rwkv_a0, v_rwkv_a2, v_rwkv_g2, v_rwkv_k_k, v_rwkv_k_a, v_rwkv_r_k, v_rwkv_lnx_w, v_rwkv_lnx_b, v_s5_a_re, v_s5_a_im, v_s5_b_re, v_s5_b_im, v_s5_c_re, v_s5_c_im, v_s5_d, v_s5_log_step, v_s5_w_glu, v_s5_b_glu, v_w_branch_rwkv, v_w_branch_s5, v_w_out, v_ffn_w_up, v_ffn_conv_w, v_ffn_conv_b, v_ffn_w_down):
    args = dict(locals())
    wgt = {n: args[n] for n in WEIGHTS}
    mom = {n: args["m_" + n] for n in WEIGHTS}
    var = {n: args["v_" + n] for n in WEIGHTS}
    me = 4 * lax.axis_index("x") + 2 * lax.axis_index("y") + lax.axis_index("c")
    shard_shapes = {n: wgt[n].shape[1:] for n in SHARDED}
    sent_shapes = {n: (shard_shapes[n][::-1] if n in TRANSPOSED else shard_shapes[n]) for n in SHARDED}
    sent = lambda n, a: a.T if n in TRANSPOSED else a

    unshard = lambda n, blocks: _cols_to_full(blocks) if n in SMALL_SHARDED else _rows_to_full(blocks)

    first = ("w_in", "rwkv_w2", "rwkv_a2", "rwkv_g2")
    late = tuple(n for n in SHARDED if n not in first and n not in F32_GATHER)
    late_payloads = [_pack([sent(n, wgt[n][0].astype(BF16)) for n in late]), _pack([wgt[n][0] for n in F32_GATHER])]
    late_started = _exchange_start(late_payloads, False, "gather_late_start")
    got_first = _unpack(_all_gather(_pack([sent(n, wgt[n][0].astype(BF16)) for n in first]), "all_gather_weights"),
                        [sent_shapes[n] for n in first])
    wt = {n: wgt[n] for n in REPLICATED}
    for n, blocks in zip(first, got_first, strict=True):
        wt[n] = unshard(n, blocks)

    def get_late(after):
        lands = _exchange_wait(late_started, after, False, "gather_late_wait")
        out = {}
        for names, land, mine in zip((late, F32_GATHER), lands, late_payloads, strict=True):
            blocks = _unpack(lax.dynamic_update_index_in_dim(land, mine, me, 0), [sent_shapes[n] for n in names])
            out.update({n: unshard(n, b_) for n, b_ in zip(names, blocks, strict=True)})
        return out

    whole = tuple(n for n in SMALL_SHARDED + REPLICATED if n != "norm_mix_pre")
    blocks_of = lambda g_: g_.reshape(N_DEV, -1, g_.shape[-1])
    started = {}

    def send_early(grads_):
        started["early"] = _exchange_start([blocks_of(grads_[n]) for n in EARLY_GRADS], True, "exchange_early_start")
        return started["early"][4]

    def send_rest(grads_):
        started["rest"] = _exchange_start([blocks_of(grads_[n]) for n in REST_GRADS], True, "exchange_rest_start")
        started["small_mine"] = _pack([grads_[n] for n in whole])
        started["small"] = _exchange_start([started["small_mine"]], False, "gather_small_start")
        return started["rest"][4] + started["small"][4]

    loss_part, dx, gr = _local_step(x[0], loss_target[0], wt, get_late, send_early, send_rest)

    got = {}
    for key, names in (("early", EARLY_GRADS), ("rest", REST_GRADS)):
        lands = _exchange_wait(started[key], dx, True, "exchange_" + key + "_wait")
        for n, land in zip(names, lands, strict=True):
            own = lax.dynamic_index_in_dim(blocks_of(gr[n]), me, 0, keepdims=False)
            got[n] = _sum_parts(land, "sum_" + n, own=own)
    small_land, = _exchange_wait(started["small"], dx, False, "gather_small_wait")
    small_sum = _sum_parts(lax.dynamic_update_index_in_dim(small_land, started["small_mine"], me, 0), "sum_small_grads")
    got.update(zip(whole, _unpack(small_sum, [gr[n].shape for n in whole]), strict=True))
    last = gr["norm_mix_pre"].reshape(SUBLANES, LANES)
    got["norm_mix_pre"] = _sum_parts(_all_gather(last, "gather_last_grad"), "sum_last_grad").reshape(gr["norm_mix_pre"].shape)
    grads = {}
    for n in WEIGHTS:
        if n in TRANSPOSED:
            grads[n] = got[n].T
        elif n in SMALL_SHARDED:
            cols = shard_shapes[n][1]
            grads[n] = lax.dynamic_slice_in_dim(got[n], me * cols, cols, axis=1)
        else:
            grads[n] = got[n]
        grads[n] = grads[n].reshape(wgt[n].shape)

    large = EARLY_GRADS + REST_GRADS
    packed = tuple(n for n in WEIGHTS if n not in large)
    pack_local = lambda src: _pack([src[n].reshape(-1) for n in packed])
    outs = _adamw(pack_local(grads), pack_local(wgt), pack_local(mom), pack_local(var), "adamw_small")
    shapes = [wgt[n].shape for n in packed]
    upd = {n: trio for n, trio in zip(packed, zip(*[_unpack(o, shapes) for o in outs], strict=True), strict=True)}
    for n in large:
        flat2d = lambda a: a.reshape(a.shape[-2:])
        trio = _adamw(flat2d(grads[n]), flat2d(wgt[n]), flat2d(mom[n]), flat2d(var[n]), "adamw_" + n)
        upd[n] = tuple(t.reshape(wgt[n].shape) for t in trio)
    loss = lax.psum(loss_part, AXES)
    return (loss, dx[None], *[grads[n] for n in WEIGHTS], *[upd[n][0] for n in WEIGHTS], *[upd[n][1] for n in WEIGHTS],
            *[upd[n][2] for n in WEIGHTS])
```

```python
import functools
import math

import jax
import jax.numpy as jnp
from jax import lax
from jax.experimental import pallas as pl
from jax.experimental.pallas import tpu as pltpu

F32 = jnp.float32
BF16 = jnp.bfloat16
SDS = jax.ShapeDtypeStruct
HI = lax.Precision.HIGHEST
MESH = pl.DeviceIdType.MESH
AXES = ("x", "y", "c")
N_DEV = 8

D_MODEL = 1024
RW = 512
HN = 64
N_RWKV = 1792
S5W = 512
S5G = 32
S5C = 16
S5P = 64
S5N = S5G * S5P
D_FF = 2816
NORM_EPS = 1e-6
LNX_EPS = 64e-5

ADAM_LR = 0.001
ADAM_B1 = 0.9
ADAM_B2 = 0.999
ADAM_EPS = 1e-08
ADAM_WD = 0.01
ADAM_STEP = 10

LANES = 128
SUBLANES = 8
VMEM_LIMIT = 56 * 1024 * 1024
WKV_CHUNK = 32


def _params(sem=("arbitrary",)):
    return pltpu.CompilerParams(dimension_semantics=sem, vmem_limit_bytes=VMEM_LIMIT)


def _pick(n, cap):
    best = None
    for t in range(LANES, min(n, cap) + 1, LANES):
        if n % t == 0:
            best = t
    return best or n


MM_VMEM_BUDGET = 40 * 1024 * 1024


def _mm_tiles(m, n, k, a_bytes, b_bytes, o_bytes):
    def divisors(d):
        return [t for t in range(LANES, d + 1, LANES) if d % t == 0] or [d]

    tn = _pick(n, 2304)
    best = None
    for tm in divisors(m):
        for tk in divisors(k):
            vmem = 2 * (tm * tk * a_bytes + tk * tn * b_bytes) + tm * tn * 4 + 2 * tm * tn * o_bytes
            if vmem > MM_VMEM_BUDGET:
                continue
            key = ((m // tm) * (n // tn) * (k // tk), -tm)
            if best is None or key < best[0]:
                best = (key, tm, tk)
    assert best is not None, (m, n, k)
    return best[1], tn, best[2]


def _mm(a, b, *, ta=False, tb=False, out_dtype=F32, name):
    m = a.shape[1] if ta else a.shape[0]
    k = a.shape[0] if ta else a.shape[1]
    n = b.shape[0] if tb else b.shape[1]
    assert (b.shape[1] if tb else b.shape[0]) == k
    tm, tn, tk = _mm_tiles(m, n, k, a.dtype.itemsize, b.dtype.itemsize, jnp.dtype(out_dtype).itemsize)
    nk = k // tk
    dims = (((0 if ta else 1,), (1 if tb else 0,)), ((), ()))

    def body(a_ref, b_ref, o_ref, acc_ref):
        kk = pl.program_id(2)

        @pl.when(kk == 0)
        def _():
            acc_ref[...] = jnp.zeros_like(acc_ref)

        acc_ref[...] += lax.dot_general(a_ref[...].astype(BF16), b_ref[...].astype(BF16), dims,
                                        preferred_element_type=F32)

        @pl.when(kk == nk - 1)
        def _():
            o_ref[...] = acc_ref[...].astype(o_ref.dtype)

    a_spec = pl.BlockSpec((tk, tm), lambda i, j, kk: (kk, i)) if ta else pl.BlockSpec((tm, tk), lambda i, j, kk: (i, kk))
    b_spec = pl.BlockSpec((tn, tk), lambda i, j, kk: (j, kk)) if tb else pl.BlockSpec((tk, tn), lambda i, j, kk: (kk, j))
    return pl.pallas_call(
        body, grid=(m // tm, n // tn, nk), in_specs=[a_spec, b_spec],
        out_specs=pl.BlockSpec((tm, tn), lambda i, j, kk: (i, j)),
        out_shape=SDS((m, n), out_dtype), scratch_shapes=[pltpu.VMEM((tm, tn), F32)],
        compiler_params=_params(("parallel", "parallel", "arbitrary")), name=name)(a, b)


def _rows(fn, rows, params, out_rows, out_accs, *, name, tl, reverse=False, scratch=()):
    first = rows[0][0] if isinstance(rows[0], tuple) else rows[0]
    length = first.shape[0]
    tl = min(tl, length)
    nt = length // tl
    rmap = (lambda i: nt - 1 - i) if reverse else (lambda i: i)
    specs, arrs = [], []
    for r in rows:
        arr, wdt, cb = r if isinstance(r, tuple) else (r, r.shape[1], 0)
        specs.append(pl.BlockSpec((tl, wdt), lambda i, cb=cb: (rmap(i), cb)))
        arrs.append(arr)
    for p in params:
        specs.append(pl.BlockSpec(p.shape, lambda i, nd=p.ndim: (0,) * nd))
        arrs.append(p)
    out_shape = [SDS((length, c), dt) for c, dt in out_rows] + [SDS(s, F32) for s in out_accs]
    out_specs = [pl.BlockSpec((tl, c), lambda i: (rmap(i), 0)) for c, _ in out_rows]
    out_specs += [pl.BlockSpec(s, lambda i, nd=len(s): (0,) * nd) for s in out_accs]
    nr, npar, nor, noa = len(rows), len(params), len(out_rows), len(out_accs)

    def body(*refs):
        rin, pin = refs[:nr], refs[nr:nr + npar]
        rout = refs[nr + npar:nr + npar + nor]
        aout = refs[nr + npar + nor:nr + npar + nor + noa]
        scr = refs[nr + npar + nor + noa:]
        step = pl.program_id(0)
        outs_r, outs_a = fn(step, [r[...] for r in rin], [p[...] for p in pin], scr)
        for ref, val in zip(rout, outs_r, strict=True):
            ref[...] = val.astype(ref.dtype)

        @pl.when(step == 0)
        def _():
            for ref in aout:
                ref[...] = jnp.zeros_like(ref)

        for ref, val in zip(aout, outs_a, strict=True):
            ref[...] += val.astype(F32)

    res = pl.pallas_call(body, grid=(nt,), in_specs=specs, out_specs=out_specs, out_shape=out_shape,
                         scratch_shapes=list(scratch), compiler_params=_params(), name=name)(*arrs)
    return list(res)


def _rms(x, g):
    return x * lax.rsqrt(jnp.mean(x * x, axis=-1, keepdims=True) + NORM_EPS) * g


def _sig(x):
    return 0.5 * (jnp.tanh(0.5 * x) + 1.0)


def _softplus(x):
    return jnp.maximum(x, 0.0) + jnp.log(1.0 + jnp.exp(-jnp.abs(x)))


def _gelu(x):
    return x * (0.5 * (1.0 + jnp.tanh(math.sqrt(2.0 / math.pi) * (x + 0.044715 * (x * x * x)))))


def _bdot(a, b):
    return jnp.dot(a.astype(BF16), b.astype(BF16), preferred_element_type=F32)


def _hdot(a, b):
    return jnp.dot(a, b, precision=HI, preferred_element_type=F32)


def _segsum_impl(x, ones):
    hi = x.astype(BF16)
    lo = (x - hi.astype(F32)).astype(BF16)
    ones = ones.astype(BF16)
    return jnp.dot(jnp.concatenate([hi, lo], axis=1), jnp.concatenate([ones, ones], axis=0), preferred_element_type=F32)


@jax.custom_vjp
def _segsum(x, ones):
    return _segsum_impl(x, ones)


_segsum.defvjp(lambda x, ones: (_segsum_impl(x, ones), ones),
               lambda ones, g: (_segsum_impl(g, ones), jnp.zeros_like(ones)))


def _block_ones(n, blk):
    i = lax.broadcasted_iota(jnp.int32, (n, n), 0) // blk
    j = lax.broadcasted_iota(jnp.int32, (n, n), 1) // blk
    return (i == j).astype(F32)


def _rms_fwd(x, g, name):
    return _rows(lambda s, r, p, _: ([_rms(r[0], p[0])], []), [x], [g], [(x.shape[1], BF16)], [], name=name, tl=512)[0]


def _rms_bwd(x, g, dh, dres, name):
    def fn(s, r, p, _):
        _, vjp = jax.vjp(_rms, r[0], p[0])
        dx, dg = vjp(r[1])
        return [dx + r[2]], [dg]
    return _rows(fn, [x, dh, dres], [g], [(x.shape[1], F32)], [g.shape], name=name, tl=256)


def _rwkv_pre_math(k_, lr, w0, a0, k_k, k_a, w2p, a2p, g2p, o512):
    pre_w = w0 + _bdot(jnp.tanh(lr), w2p)
    w = -_softplus(-pre_w) - 0.5
    decay = jnp.exp(-jnp.exp(w))
    a = _sig(a0 + _bdot(lr, a2p))
    g = _bdot(_sig(lr), g2p)
    kr = k_ * k_k
    kk = kr / jnp.maximum(jnp.sqrt(_segsum(kr * kr, o512)), 1e-12)
    k2 = k_ * (1.0 + (a - 1.0) * k_a)
    return decay, k2, -kk, kk * a, g


def _shift_down(p, prev_row):
    row = lax.broadcasted_iota(jnp.int32, p.shape, 0)
    return jnp.where(row == 0, jnp.broadcast_to(prev_row, p.shape), pltpu.roll(p, 1, 0))


def _shift_up(q, next_row):
    n = q.shape[0]
    row = lax.broadcasted_iota(jnp.int32, q.shape, 0)
    return jnp.where(row == n - 1, jnp.broadcast_to(next_row, q.shape), pltpu.roll(q, n - 1, 0))


def _rwkv_pre_fwd(p, mu, small, o512):
    def fn(step, r, prm, scr):
        car = scr[0]

        @pl.when(step == 0)
        def _():
            car[...] = jnp.zeros_like(car)

        x = r[0]
        prev = _shift_down(x, car[SUBLANES - 1:SUBLANES, :])
        car[...] = x[x.shape[0] - SUBLANES:, :]
        xs = x + (prev - x) * prm[0]
        decay, k2, aa, bb, g = _rwkv_pre_math(xs[:, RW:2 * RW], xs[:, 3 * RW:], *prm[1:])
        return [xs[:, :RW], decay, k2, xs[:, 2 * RW:3 * RW], aa, bb, g], []
    return _rows(fn, [p], [mu, *small, o512], [(RW, F32)] * 7, [], name="rwkv_pre_fwd", tl=256,
                 scratch=[pltpu.VMEM((SUBLANES, N_RWKV), F32)])


def _rwkv_pre_bwd(p, cots, mu, small, o512):
    length = p.shape[0]
    tl = min(256, length)
    nt = length // tl
    rows_per = tl // SUBLANES
    params = [mu, *small, o512]
    acc_shapes = [mu.shape] + [q.shape for q in small]
    nr, npar, nacc = 2 + len(cots), len(params), len(acc_shapes)

    def body(*refs):
        rin, pin = refs[:nr], refs[nr:nr + npar]
        dp_ref = refs[nr + npar]
        aout = refs[nr + npar + 1:nr + npar + 1 + nacc]
        car_q = refs[nr + npar + 1 + nacc]
        step = pl.program_id(0)

        @pl.when(step == 0)
        def _():
            car_q[...] = jnp.zeros_like(car_q)
            for ref in aout:
                ref[...] = jnp.zeros_like(ref)

        x = rin[0][...]
        prev_row = jnp.where(step == nt - 1, 0.0, rin[1][SUBLANES - 1:SUBLANES, :])
        dr, ddecay, dk2, dv, daa, dbb, dg, dr_b, dk2_b, dv_b = [r[...] for r in rin[2:]]
        dr, dk2, dv = dr + dr_b, dk2 + dk2_b, dv + dv_b
        prm = [q[...] for q in pin]
        mu_, o512_ = prm[0], prm[-1]
        prev = _shift_down(x, prev_row)
        xs = x + (prev - x) * mu_
        _, vjp = jax.vjp(lambda k_, lr, *w: _rwkv_pre_math(k_, lr, *w, o512_), xs[:, RW:2 * RW], xs[:, 3 * RW:], *prm[1:-1])
        dk_, dlr, *dsmall = vjp((ddecay, dk2, daa, dbb, dg))
        dxs = jnp.concatenate([dr, dk_, dv, dlr], axis=1)
        q = dxs * mu_
        dp_ref[...] = (dxs - q + _shift_up(q, car_q[0:1, :])).astype(dp_ref.dtype)
        car_q[...] = q[:SUBLANES, :]
        aout[0][...] += jnp.sum((prev - x) * dxs, axis=0, keepdims=True)
        for ref, val in zip(aout[1:], dsmall, strict=True):
            ref[...] += val

    rmap = lambda i: nt - 1 - i
    specs = [pl.BlockSpec((tl, N_RWKV), lambda i: (rmap(i), 0)),
             pl.BlockSpec((SUBLANES, N_RWKV), lambda i: (jnp.maximum(rmap(i) * rows_per - 1, 0), 0))]
    specs += [pl.BlockSpec((tl, RW), lambda i: (rmap(i), 0)) for _ in cots]
    specs += [pl.BlockSpec(q.shape, lambda i, nd=q.ndim: (0,) * nd) for q in params]
    out_shape = [SDS((length, N_RWKV), BF16)] + [SDS(sh, F32) for sh in acc_shapes]
    out_specs = [pl.BlockSpec((tl, N_RWKV), lambda i: (rmap(i), 0))]
    out_specs += [pl.BlockSpec(sh, lambda i, nd=len(sh): (0,) * nd) for sh in acc_shapes]
    res = pl.pallas_call(body, grid=(nt,), in_specs=specs, out_specs=out_specs, out_shape=out_shape,
                         scratch_shapes=[pltpu.VMEM((SUBLANES, N_RWKV), F32)],
                         compiler_params=_params(), name="rwkv_pre_bwd")(p, p, *cots, *params)
    return list(res)


def _rwkv_post_math(y, r, k2, v, g, lnx_w, lnx_b, r_k, o512):
    mean = _segsum(y, o512) * (1.0 / HN)
    yc = y - mean
    var = _segsum(yc * yc, o512) * (1.0 / HN)
    yn = yc * lax.rsqrt(var + LNX_EPS) * lnx_w + lnx_b
    bonus = _segsum(r * k2 * r_k, o512) * v
    return (yn + bonus) * g


def _rwkv_post_fwd(y, r, k2, v, g, prm, o512):
    return _rows(lambda s, rr, p, _: ([_rwkv_post_math(*rr, *p)], []), [y, r, k2, v, g], [*prm, o512],
                 [(RW, BF16)], [], name="rwkv_post_fwd", tl=256)[0]


def _rwkv_post_bwd(y, r, k2, v, g, dout, prm, o512):
    def fn(s, rr, p, _):
        o = p[-1]
        _, vjp = jax.vjp(lambda *a: _rwkv_post_math(*a, o), *rr[:5], *p[:-1])
        gr = vjp(rr[5])
        return list(gr[:5]), list(gr[5:])
    return _rows(fn, [y, r, k2, v, g, dout], [*prm, o512], [(RW, F32)] * 5, [q.shape for q in prm],
                 name="rwkv_post_bwd", tl=256)


def _glu_math(ys, t, b):
    return ys * _sig(t + b)


def _glu_fwd(ys, t, b):
    return _rows(lambda s, r, p, _: ([_glu_math(r[0], r[1], p[0])], []), [ys, t], [b], [(S5W, BF16)], [],
                 name="s5_glu_fwd", tl=512)[0]


def _glu_bwd(ys, t, dout, b):
    def fn(s, r, p, _):
        _, vjp = jax.vjp(_glu_math, r[0], r[1], p[0])
        dys, dt, db = vjp(r[2])
        return [dys, dt], [db]
    return _rows(fn, [ys, t, dout], [b], [(S5W, F32), (S5W, BF16)], [b.shape], name="s5_glu_bwd", tl=512)


def _merge_math(gp_r, gp_s, o_r, o_s, b_r, b_s):
    return _sig(gp_r + b_r) * o_r + _sig(gp_s + b_s) * o_s


def _merge_fwd(gpre, o_r, o_s, b_r, b_s):
    return _rows(lambda s, r, p, _: ([_merge_math(*r, *p)], []),
                 [(gpre, D_MODEL, 0), (gpre, D_MODEL, 1), o_r, o_s], [b_r, b_s], [(D_MODEL, BF16)], [],
                 name="merge_fwd", tl=256)[0]


def _merge_bwd(gpre, o_r, o_s, dmi, b_r, b_s):
    def fn(s, r, p, _):
        _, vjp = jax.vjp(_merge_math, *r[:4], *p)
        dgr, dgs, dor, dos, dbr, dbs = vjp(r[4])
        return [dgr, dgs, dor, dos], [dbr, dbs]
    return _rows(fn, [(gpre, D_MODEL, 0), (gpre, D_MODEL, 1), o_r, o_s, dmi], [b_r, b_s], [(D_MODEL, BF16)] * 4,
                 [b_r.shape, b_s.shape], name="merge_bwd", tl=256)


def _mid_fwd(x, mixed, g_post, g_pre):
    def fn(s, r, p, _):
        x2 = r[0] + _rms(r[1], p[0])
        return [x2, _rms(x2, p[1])], []
    return _rows(fn, [x, mixed], [g_post, g_pre], [(D_MODEL, F32), (D_MODEL, BF16)], [], name="mid_fwd", tl=256)


def _mid_bwd(x2, mixed, dh2, dx3, g_post, g_pre):
    def fn(s, r, p, _):
        _, vjp1 = jax.vjp(_rms, r[0], p[1])
        dx2, dg_pre = vjp1(r[2])
        dx2 = dx2 + r[3]
        _, vjp2 = jax.vjp(_rms, r[1], p[0])
        dmixed, dg_post = vjp2(dx2)
        return [dx2, dmixed], [dg_post, dg_pre]
    return _rows(fn, [x2, mixed, dh2, dx3], [g_post, g_pre], [(D_MODEL, F32), (D_MODEL, BF16)], [g_post.shape, g_pre.shape],
                 name="mid_bwd", tl=256)


def _final(x2, f, tgt, g_post):
    def fn(s, r, p, _):
        y, vjp = jax.vjp(_rms, r[1], p[0])
        diff = r[0] + y - r[2]
        dx3 = diff * (1.0 / D_MODEL)
        df, dg = vjp(dx3)
        part = 0.5 * jnp.sum(jnp.sum(diff * diff, axis=1, keepdims=True), axis=0, keepdims=True) * (1.0 / D_MODEL)
        return [dx3, df], [dg, jnp.broadcast_to(part, (1, LANES))]
    return _rows(fn, [x2, f, tgt], [g_post], [(D_MODEL, F32), (D_MODEL, BF16)], [g_post.shape, (1, LANES)], name="final", tl=256)


def _conv_taps(z, car):
    row = lax.broadcasted_iota(jnp.int32, z.shape, 0)
    z1 = jnp.where(row == 0, jnp.broadcast_to(car[7:8, :], z.shape), pltpu.roll(z, 1, 0))
    z2 = pltpu.roll(z, 2, 0)
    z2 = jnp.where(row == 0, jnp.broadcast_to(car[6:7, :], z.shape), z2)
    z2 = jnp.where(row == 1, jnp.broadcast_to(car[7:8, :], z.shape), z2)
    return z1, z2


def _conv(z, car, w, b):
    z1, z2 = _conv_taps(z, car)
    return b + w[0:1, :] * z2 + w[1:2, :] * z1 + w[2:3, :] * z, z1, z2


def _conv_fwd(z, conv_w, conv_b):
    length = z.shape[0]
    tl = min(256, length)
    nt = length // tl
    tc = _pick(D_FF, 1536)
    nb = D_FF // tc

    def body(zg_ref, zv_ref, wg_ref, wv_ref, bg_ref, bv_ref, o_ref, cg, cv):
        @pl.when(pl.program_id(1) == 0)
        def _():
            cg[...] = jnp.zeros_like(cg)
            cv[...] = jnp.zeros_like(cv)

        zg, zv = zg_ref[...], zv_ref[...]
        gate, _, _ = _conv(zg, cg[...], wg_ref[...], bg_ref[...])
        val, _, _ = _conv(zv, cv[...], wv_ref[...], bv_ref[...])
        cg[...] = zg[tl - SUBLANES:, :]
        cv[...] = zv[tl - SUBLANES:, :]
        o_ref[...] = (_gelu(gate) * val).astype(o_ref.dtype)

    zspec = lambda off: pl.BlockSpec((tl, tc), lambda j, i: (i, j + off))
    wspec = lambda off, r: pl.BlockSpec((r, tc), lambda j, i: (0, j + off))
    return pl.pallas_call(
        body, grid=(nb, nt),
        in_specs=[zspec(0), zspec(nb), wspec(0, 3), wspec(nb, 3), wspec(0, 1), wspec(nb, 1)],
        out_specs=pl.BlockSpec((tl, tc), lambda j, i: (i, j)), out_shape=SDS((length, D_FF), BF16),
        scratch_shapes=[pltpu.VMEM((SUBLANES, tc), F32)] * 2,
        compiler_params=_params(("arbitrary", "arbitrary")), name="conv_fwd")(z, z, conv_w, conv_w, conv_b, conv_b)


def _conv_bwd(z, dact, conv_w, conv_b):
    length = z.shape[0]
    tl = min(256, length)
    nt = length // tl
    tc = _pick(D_FF, 1536)
    nb = D_FF // tc
    rows_per = tl // SUBLANES

    def half_bwd(dzc, z, z1, z2, w, dcar):
        n = tl
        row = lax.broadcasted_iota(jnp.int32, dzc.shape, 0)
        u1 = jnp.where(row == n - 1, jnp.broadcast_to(dcar[0:1, :], dzc.shape), pltpu.roll(dzc, n - 1, 0))
        u2 = pltpu.roll(dzc, n - 2, 0)
        u2 = jnp.where(row == n - 2, jnp.broadcast_to(dcar[0:1, :], dzc.shape), u2)
        u2 = jnp.where(row == n - 1, jnp.broadcast_to(dcar[1:2, :], dzc.shape), u2)
        dz = w[2:3, :] * dzc + w[1:2, :] * u1 + w[0:1, :] * u2
        dw = jnp.concatenate([jnp.sum(dzc * z2, axis=0, keepdims=True), jnp.sum(dzc * z1, axis=0, keepdims=True),
                              jnp.sum(dzc * z, axis=0, keepdims=True)], axis=0)
        return dz, dw, jnp.sum(dzc, axis=0, keepdims=True)

    def body(zg_ref, zv_ref, pg_ref, pv_ref, da_ref, wg_ref, wv_ref, bg_ref, bv_ref,
             dzg_ref, dzv_ref, dwg_ref, dwv_ref, dbg_ref, dbv_ref, cg, cv):
        step = pl.program_id(1)

        @pl.when(step == 0)
        def _():
            cg[...] = jnp.zeros_like(cg)
            cv[...] = jnp.zeros_like(cv)
            for ref in (dwg_ref, dwv_ref, dbg_ref, dbv_ref):
                ref[...] = jnp.zeros_like(ref)

        is_first_tile = step == nt - 1
        zg, zv = zg_ref[...], zv_ref[...]
        pg = jnp.where(is_first_tile, 0.0, pg_ref[...])
        pv = jnp.where(is_first_tile, 0.0, pv_ref[...])
        wg, wv = wg_ref[...], wv_ref[...]
        gate, zg1, zg2 = _conv(zg, pg, wg, bg_ref[...])
        val, zv1, zv2 = _conv(zv, pv, wv, bv_ref[...])
        act_g, vjp = jax.vjp(_gelu, gate)
        da = da_ref[...]
        dgate = vjp(da * val)[0]
        dval = da * act_g
        dzg, dwg, dbg = half_bwd(dgate, zg, zg1, zg2, wg, cg[...])
        dzv, dwv, dbv = half_bwd(dval, zv, zv1, zv2, wv, cv[...])
        cg[...] = dgate[:SUBLANES, :]
        cv[...] = dval[:SUBLANES, :]
        dzg_ref[...] = dzg.astype(dzg_ref.dtype)
        dzv_ref[...] = dzv.astype(dzv_ref.dtype)
        dwg_ref[...] += dwg
        dwv_ref[...] += dwv
        dbg_ref[...] += dbg
        dbv_ref[...] += dbv

    rmap = lambda i: nt - 1 - i
    zspec = lambda off: pl.BlockSpec((tl, tc), lambda j, i: (rmap(i), j + off))
    pspec = lambda off: pl.BlockSpec((SUBLANES, tc), lambda j, i: (jnp.maximum(rmap(i) * rows_per - 1, 0), j + off))
    wspec = lambda off, r: pl.BlockSpec((r, tc), lambda j, i: (0, j + off))
    out_w = lambda r: pl.BlockSpec((r, tc), lambda j, i: (0, j))
    dzg, dzv, dwg, dwv, dbg, dbv = pl.pallas_call(
        body, grid=(nb, nt),
        in_specs=[zspec(0), zspec(nb), pspec(0), pspec(nb), pl.BlockSpec((tl, tc), lambda j, i: (rmap(i), j)),
                  wspec(0, 3), wspec(nb, 3), wspec(0, 1), wspec(nb, 1)],
        out_specs=[pl.BlockSpec((tl, tc), lambda j, i: (rmap(i), j))] * 2 + [out_w(3), out_w(3), out_w(1), out_w(1)],
        out_shape=[SDS((length, D_FF), BF16)] * 2 + [SDS((3, D_FF), F32)] * 2 + [SDS((1, D_FF), F32)] * 2,
        scratch_shapes=[pltpu.VMEM((SUBLANES, tc), F32)] * 2,
        compiler_params=_params(("arbitrary", "arbitrary")), name="conv_bwd")(z, z, z, z, dact, conv_w, conv_w, conv_b, conv_b)
    return dzg, dzv, jnp.concatenate([dwg, dwv], axis=1), jnp.concatenate([dbg, dbv], axis=1)


def _s5_prep_math(a_re, a_im, ls, b_re, b_im):
    dt = jnp.exp(ls)
    er = jnp.exp(a_re * dt)
    ph = a_im * dt
    abr, abi = er * jnp.cos(ph), er * jnp.sin(ph)
    den = a_re * a_re + a_im * a_im
    nr = abr - 1.0
    cr = (nr * a_re + abi * a_im) / den
    ci = (abi * a_re - nr * a_im) / den
    return abr, abi, cr * b_re - ci * b_im, cr * b_im + ci * b_re


def _s5_prep_fwd(a_re, a_im, ls, b_re, b_im):
    def body(ar, ai, l, br, bi, o1, o2, o3, o4):
        for ref, val in zip((o1, o2, o3, o4), _s5_prep_math(ar[...], ai[...], l[...], br[...], bi[...]), strict=True):
            ref[...] = val
    return pl.pallas_call(body, out_shape=[SDS((1, S5N), F32)] * 2 + [SDS((S5C, S5N), F32)] * 2,
                          name="s5_prep_fwd")(a_re, a_im, ls, b_re, b_im)


def _s5_prep_bwd(a_re, a_im, ls, b_re, b_im, cots, gsel):
    def body(ar, ai, l, br, bi, c1, c2, c3, c4, g_ref, o1, o2, o3, o4, o5):
        _, vjp = jax.vjp(_s5_prep_math, ar[...], ai[...], l[...], br[...], bi[...])
        d_ar, d_ai, d_ls, d_br, d_bi = vjp((c1[...], c2[...], c3[...], c4[...]))
        o1[...] = d_ar
        o2[...] = d_ai
        o3[...] = _hdot(jnp.broadcast_to(d_ls, (SUBLANES, S5N)), g_ref[...])
        o4[...] = d_br
        o5[...] = d_bi
    return pl.pallas_call(body, out_shape=[SDS((1, S5N), F32)] * 2 + [SDS((SUBLANES, LANES), F32)] + [SDS((S5C, S5N), F32)] * 2,
                          name="s5_prep_bwd")(a_re, a_im, ls, b_re, b_im, *cots, gsel)


def _cmul(ar, ai, br, bi):
    return ar * br - ai * bi, ar * bi + ai * br


def _s5_powers(abr, abi):
    shp = (SUBLANES, S5N)
    a1 = (jnp.broadcast_to(abr, shp), jnp.broadcast_to(abi, shp))
    a2 = _cmul(*a1, *a1)
    a4 = _cmul(*a2, *a2)
    row = lax.broadcasted_iota(jnp.int32, shp, 0)
    pr, pi = a1
    cur = a1
    for i in range(1, SUBLANES):
        cur = _cmul(*cur, *a1)
        pr = jnp.where(row == i, cur[0], pr)
        pi = jnp.where(row == i, cur[1], pi)
    return a1, a2, a4, (pr, pi)


def _s5_scan(sre_ref, sim_ref, car_re, car_im, abr, abi, n_rows, reverse):
    a1, a2, a4, (pr, pi) = _s5_powers(abr, abi)
    sgn = -1.0 if reverse else 1.0
    row = lax.broadcasted_iota(jnp.int32, (SUBLANES, S5N), 0)
    if reverse:
        qr, qi = pr, pi
        for i in range(SUBLANES):
            src = SUBLANES - 1 - i
            qr = jnp.where(row == i, jnp.broadcast_to(pr[src:src + 1, :], pr.shape), qr)
            qi = jnp.where(row == i, jnp.broadcast_to(pi[src:src + 1, :], pi.shape), qi)
        pr, pi = qr, qi
    nblk = n_rows // SUBLANES

    def blk(i, carry):
        cr, ci = carry
        b = (nblk - 1 - i) if reverse else i
        sl = pl.ds(pl.multiple_of(b * SUBLANES, SUBLANES), SUBLANES)
        xr, xi = sre_ref[sl, :], sim_ref[sl, :]
        for kk, (er, ei) in ((1, a1), (2, a2), (4, a4)):
            if reverse:
                sr = jnp.where(row < SUBLANES - kk, pltpu.roll(xr, SUBLANES - kk, 0), 0.0)
                si = jnp.where(row < SUBLANES - kk, pltpu.roll(xi, SUBLANES - kk, 0), 0.0)
            else:
                sr = jnp.where(row >= kk, pltpu.roll(xr, kk, 0), 0.0)
                si = jnp.where(row >= kk, pltpu.roll(xi, kk, 0), 0.0)
            dr, di = _cmul(er, sgn * ei, sr, si)
            xr, xi = xr + dr, xi + di
        dr, di = _cmul(pr, sgn * pi, cr, ci)
        xr, xi = xr + dr, xi + di
        sre_ref[sl, :] = xr
        sim_ref[sl, :] = xi
        edge = 0 if reverse else SUBLANES - 1
        return (jnp.broadcast_to(xr[edge:edge + 1, :], xr.shape), jnp.broadcast_to(xi[edge:edge + 1, :], xi.shape))

    cr, ci = lax.fori_loop(0, nblk, blk, (car_re[...], car_im[...]))
    car_re[...] = cr
    car_im[...] = ci


S5_BLK = 4
S5_BN = S5N // S5_BLK


def _split2(x):
    hi = x.astype(BF16)
    return hi, (x - hi.astype(F32)).astype(BF16)


def _dot3(a, b, ca, cb):
    ah, al = _split2(a)
    bh, bl = _split2(b)
    return lax.dot_general(jnp.concatenate([ah, ah, al], axis=ca), jnp.concatenate([bh, bl, bh], axis=cb),
                           (((ca,), (cb,)), ((), ())), preferred_element_type=F32)


def _dot1(a, b, ca, cb):
    return lax.dot_general(a.astype(BF16), b.astype(BF16), (((ca,), (cb,)), ((), ())), preferred_element_type=F32)


def _s5_project_in(u, bblk_ref, sre_ref, sim_ref):
    for j in range(S5_BLK):
        bu = _dot1(u[:, j * LANES:(j + 1) * LANES], bblk_ref[j], 1, 0)
        sre_ref[:, j * S5_BN:(j + 1) * S5_BN] = bu[:, :S5_BN]
        sim_ref[:, j * S5_BN:(j + 1) * S5_BN] = bu[:, S5_BN:]


def _s5_project_out(u, d, cblk_ref, sre_ref, sim_ref):
    ys = []
    for j in range(S5_BLK):
        sl = slice(j * S5_BN, (j + 1) * S5_BN)
        ys.append(_dot1(sre_ref[:, sl], cblk_ref[j, :S5_BN, :], 1, 0) + _dot1(sim_ref[:, sl], cblk_ref[j, S5_BN:, :], 1, 0))
    return jnp.concatenate(ys, axis=1) + d * u


def _s5_fwd(u, bblk, cblk, abr, abi, d):
    length = u.shape[0]
    tl = min(256, length)
    nt = length // tl

    def body(u_ref, b_ref, c_ref, ar_ref, ai_ref, d_ref, ys_ref, kr_ref, ki_ref, sre, sim, car_re, car_im):
        @pl.when(pl.program_id(0) == 0)
        def _():
            car_re[...] = jnp.zeros_like(car_re)
            car_im[...] = jnp.zeros_like(car_im)

        kr_ref[0] = car_re[...]
        ki_ref[0] = car_im[...]
        u_ = u_ref[...]
        _s5_project_in(u_, b_ref, sre, sim)
        _s5_scan(sre, sim, car_re, car_im, ar_ref[...], ai_ref[...], tl, False)
        ys_ref[...] = _gelu(_s5_project_out(u_, d_ref[...], c_ref, sre, sim))

    full = lambda a: pl.BlockSpec(a.shape, lambda i, nd=a.ndim: (0,) * nd)
    chk = pl.BlockSpec((1, SUBLANES, S5N), lambda i: (i, 0, 0))
    return pl.pallas_call(
        body, grid=(nt,), in_specs=[pl.BlockSpec((tl, S5W), lambda i: (i, 0)), full(bblk), full(cblk), full(abr), full(abi), full(d)],
        out_specs=[pl.BlockSpec((tl, S5W), lambda i: (i, 0)), chk, chk],
        out_shape=[SDS((length, S5W), F32), SDS((nt, SUBLANES, S5N), F32), SDS((nt, SUBLANES, S5N), F32)],
        scratch_shapes=[pltpu.VMEM((tl, S5N), F32)] * 2 + [pltpu.VMEM((SUBLANES, S5N), F32)] * 2,
        compiler_params=_params(), name="s5_fwd")(u, bblk, cblk, abr, abi, d)


def _s5_bwd(u, dys_a, dys_b, kre, kim, bblk, cblk, abr, abi, d):
    length = u.shape[0]
    tl = min(256, length)
    nt = length // tl

    def body(u_ref, da_ref, db_ref, kr_ref, ki_ref, b_ref, c_ref, ar_ref, ai_ref, d_ref,
             du_ref, dB_ref, dC_ref, dar_ref, dai_ref, dd_ref, sre, sim, gre, gim, car_re, car_im, dcar_re, dcar_im):
        @pl.when(pl.program_id(0) == 0)
        def _():
            dcar_re[...] = jnp.zeros_like(dcar_re)
            dcar_im[...] = jnp.zeros_like(dcar_im)
            for ref in (dB_ref, dC_ref, dar_ref, dai_ref, dd_ref):
                ref[...] = jnp.zeros_like(ref)

        u_ = u_ref[...]
        abr_, abi_, d_ = ar_ref[...], ai_ref[...], d_ref[...]
        car_re[...] = kr_ref[0]
        car_im[...] = ki_ref[0]
        _s5_project_in(u_, b_ref, sre, sim)
        _s5_scan(sre, sim, car_re, car_im, abr_, abi_, tl, False)
        y = _s5_project_out(u_, d_, c_ref, sre, sim)
        _, vjp = jax.vjp(_gelu, y)
        dy = vjp(da_ref[...] + db_ref[...])[0]
        dd_ref[...] += jnp.sum(dy * u_, axis=0, keepdims=True)
        for j in range(S5_BLK):
            sl = slice(j * S5_BN, (j + 1) * S5_BN)
            dyj = dy[:, j * LANES:(j + 1) * LANES]
            gre[:, sl] = _dot1(dyj, c_ref[j, :S5_BN, :], 1, 1)
            gim[:, sl] = _dot1(dyj, c_ref[j, S5_BN:, :], 1, 1)
            dC_ref[j, :S5_BN, :] += _dot1(sre[:, sl], dyj, 0, 0)
            dC_ref[j, S5_BN:, :] += _dot1(sim[:, sl], dyj, 0, 0)
        _s5_scan(gre, gim, dcar_re, dcar_im, abr_, abi_, tl, True)
        gr, gi = gre[...], gim[...]
        pr = _shift_down(sre[...], kr_ref[0, 0:1, :])
        pi = _shift_down(sim[...], ki_ref[0, 0:1, :])
        dar_ref[...] += jnp.sum(gr * pr + gi * pi, axis=0, keepdims=True)
        dai_ref[...] += jnp.sum(gi * pr - gr * pi, axis=0, keepdims=True)
        dus = []
        for j in range(S5_BLK):
            sl = slice(j * S5_BN, (j + 1) * S5_BN)
            uj = u_[:, j * LANES:(j + 1) * LANES]
            dus.append(_dot1(gre[:, sl], b_ref[j, :, :S5_BN], 1, 1) + _dot1(gim[:, sl], b_ref[j, :, S5_BN:], 1, 1))
            dB_ref[j, :, :S5_BN] += _dot1(uj, gre[:, sl], 0, 0)
            dB_ref[j, :, S5_BN:] += _dot1(uj, gim[:, sl], 0, 0)
        du_ref[...] = (jnp.concatenate(dus, axis=1) + d_ * dy).astype(du_ref.dtype)

    rmap = lambda i: nt - 1 - i
    full = lambda a: pl.BlockSpec(a.shape, lambda i, nd=a.ndim: (0,) * nd)
    row = pl.BlockSpec((tl, S5W), lambda i: (rmap(i), 0))
    chk = pl.BlockSpec((1, SUBLANES, S5N), lambda i: (rmap(i), 0, 0))
    return pl.pallas_call(
        body, grid=(nt,), in_specs=[row, row, row, chk, chk, full(bblk), full(cblk), full(abr), full(abi), full(d)],
        out_specs=[row, full(bblk), full(cblk), full(abr), full(abi), full(d)],
        out_shape=[SDS((length, S5W), BF16), SDS(bblk.shape, F32), SDS(cblk.shape, F32), SDS(abr.shape, F32),
                   SDS(abi.shape, F32), SDS(d.shape, F32)],
        scratch_shapes=[pltpu.VMEM((tl, S5N), F32)] * 4 + [pltpu.VMEM((SUBLANES, S5N), F32)] * 4,
        compiler_params=_params(), name="s5_bwd")(u, dys_a, dys_b, kre, kim, bblk, cblk, abr, abi, d)


def _s5_blockdiag_in(bb):
    t = bb.reshape(S5C, S5_BLK, S5_BN).transpose(1, 0, 2)
    t = jnp.tile(t, (1, SUBLANES, 1))
    mask = (lax.broadcasted_iota(jnp.int32, (LANES, S5_BN), 0) // S5C) == (lax.broadcasted_iota(jnp.int32, (LANES, S5_BN), 1) // S5P)
    return jnp.where(mask[None], t, 0.0)


def _s5_blockdiag_in_t(dblk):
    t = dblk.reshape(S5_BLK, SUBLANES, S5C, SUBLANES, S5P)
    t = jnp.diagonal(t, axis1=1, axis2=3)
    return t.transpose(1, 0, 3, 2).reshape(S5C, S5N)


def _s5_blockdiag_out(c):
    t = c.reshape(S5_BLK, SUBLANES, S5C, S5P).transpose(0, 3, 1, 2).reshape(S5_BLK, S5P, LANES)
    t = jnp.tile(t, (1, SUBLANES, 1))
    mask = (lax.broadcasted_iota(jnp.int32, (S5_BN, LANES), 0) // S5P) == (lax.broadcasted_iota(jnp.int32, (S5_BN, LANES), 1) // S5C)
    return jnp.where(mask[None], t, 0.0)


def _s5_blockdiag_out_t(dblk):
    t = dblk.reshape(S5_BLK, SUBLANES, S5P, SUBLANES, S5C)
    t = jnp.diagonal(t, axis1=1, axis2=3)
    return t.transpose(0, 3, 2, 1).reshape(S5G, S5C, S5P)


def _wkv_consts():
    lane = lax.broadcasted_iota(jnp.int32, (4 * HN, LANES), 1)
    row = lax.broadcasted_iota(jnp.int32, (4 * HN, LANES), 0)
    diag = ((lane % HN) == (row % HN)).astype(F32)
    ones = _block_ones(LANES, HN).astype(BF16)
    return lane % HN, diag, jnp.concatenate([ones, ones], axis=0)


def _wkv_split(p):
    hi = p.astype(BF16)
    lo = (p - hi.astype(F32)).astype(BF16)
    return jnp.concatenate([hi, lo], axis=1)


def _wkv_rowseg(p, ones):
    return jnp.dot(_wkv_split(p), ones, preferred_element_type=F32)


def _wkv_full(rows_ref, t):
    q = jnp.broadcast_to(rows_ref[t], (SUBLANES, RW))
    return jnp.concatenate([jnp.tile(q[:, hp * LANES:(hp + 1) * LANES], (SUBLANES, 1)) for hp in range(4)], axis=0)


def _wkv_colsum8(x):
    return jnp.concatenate([x[hp * HN:(hp + 1) * HN].reshape(SUBLANES, SUBLANES, LANES).sum(axis=0) for hp in range(4)], axis=1)


def _wkv_prebroadcast(src_refs, dst_refs, n):
    for s, dref in zip(src_refs, dst_refs, strict=True):
        dref[...] = s[...].reshape(n, 1, RW)


def _wkv_pair_sums(lhs_ref, out_ref):
    n, ns = out_ref.shape[0], out_ref.shape[1]
    ones_pair = _block_ones(2 * LANES, HN).astype(BF16)
    out_ref[...] = jnp.dot(lhs_ref[0:n].reshape(n * ns, 2 * LANES), ones_pair,
                           preferred_element_type=F32).reshape(n, ns, 2 * LANES)


def _wkv_transpose_out(acc, c, diag):
    lane = lax.broadcasted_iota(jnp.int32, (c, LANES), 1)
    outs = []
    for hp in range(4):
        zp = lax.dot_general(acc[hp * HN:(hp + 1) * HN], diag[:HN], (((0,), (0,)), ((), ())), precision=HI,
                             preferred_element_type=F32)
        outs.append(jnp.where(lane < HN, zp[:c], zp[HN:HN + c]))
    return jnp.concatenate(outs, axis=1)


def _wkv_fwd(r, w, k, v, a, b):
    length = r.shape[0]
    c = min(WKV_CHUNK, length)
    nc = length // c
    ns = 4 * HN

    def body(r_ref, w_ref, k_ref, v_ref, a_ref, b_ref, y_ref, sst_ref, sast_ref, vst_ref, fin_ref,
             s_ref, acc_ref, lhs_ref, ysum_ref, rb, wb, kb, vb, ab, bb, wab, beb, kab):
        @pl.when(pl.program_id(0) == 0)
        def _():
            s_ref[...] = jnp.zeros_like(s_ref)

        a_next = pltpu.roll(a_ref[...], c - 1, 0)
        o512 = _block_ones(RW, HN)
        _wkv_prebroadcast((r_ref, w_ref, k_ref, v_ref, a_ref, b_ref, w_ref[...] * a_next,
                           _segsum_impl(b_ref[...] * a_next, o512), _segsum_impl(k_ref[...] * a_next, o512)),
                          (rb, wb, kb, vb, ab, bb, wab, beb, kab), c)
        lane_t, diag, ones = _wkv_consts()

        def fill_v(q, carry):
            for half in range(2):
                lhs_ref[q, :, half * LANES:(half + 1) * LANES] = (_wkv_full(vb, 2 * q + half) * diag).astype(BF16)
            return carry

        lax.fori_loop(0, c // 2, fill_v, 0)
        _wkv_pair_sums(lhs_ref, vst_ref)

        def pair(p, carry):
            t0, t1 = 2 * p, 2 * p + 1
            s0 = s_ref[...]
            both = _wkv_rowseg(jnp.concatenate([s0 * _wkv_full(ab, t0), s0 * _wkv_full(wab, t0)], axis=0), ones)
            v_pair = vst_ref[p]
            sa0, v0 = both[:ns], v_pair[:, :LANES]
            s1 = s0 * _wkv_full(wb, t0) + sa0 * _wkv_full(bb, t0) + v0 * _wkv_full(kb, t0)
            sa1 = both[ns:] + sa0 * _wkv_full(beb, t0) + v0 * _wkv_full(kab, t0)
            s2 = s1 * _wkv_full(wb, t1) + sa1 * _wkv_full(bb, t1) + v_pair[:, LANES:] * _wkv_full(kb, t1)
            sst_ref[t0] = s0
            sast_ref[t0] = sa0
            sst_ref[t1] = s1
            sast_ref[t1] = sa1
            s_ref[...] = s2
            lhs_ref[p, :, 0:LANES] = (s1 * _wkv_full(rb, t0)).astype(BF16)
            lhs_ref[p, :, LANES:2 * LANES] = (s2 * _wkv_full(rb, t1)).astype(BF16)
            return carry

        lax.fori_loop(0, c // 2, pair, 0)
        _wkv_pair_sums(lhs_ref, ysum_ref)
        acc_ref[...] = jnp.zeros_like(acc_ref)

        def gather_y(p, carry):
            both = ysum_ref[p]
            acc = jnp.where(lane_t == 2 * p, both[:, :LANES], acc_ref[...])
            acc_ref[...] = jnp.where(lane_t == 2 * p + 1, both[:, LANES:], acc)
            return carry

        lax.fori_loop(0, c // 2, gather_y, 0)
        y_ref[...] = _wkv_transpose_out(acc_ref[...], c, diag)
        fin_ref[...] = s_ref[...]

    row = pl.BlockSpec((c, RW), lambda i: (i, 0))
    st = pl.BlockSpec((c, ns, LANES), lambda i: (i, 0, 0))
    state = pltpu.VMEM((ns, LANES), F32)
    st_pair = pl.BlockSpec((c // 2, ns, 2 * LANES), lambda i: (i, 0, 0))
    return pl.pallas_call(
        body, grid=(nc,), in_specs=[row] * 6,
        out_specs=[row, st, st, st_pair, pl.BlockSpec((ns, LANES), lambda i: (0, 0))],
        out_shape=[SDS((length, RW), F32)] + [SDS((length, ns, LANES), F32)] * 2 + [SDS((length // 2, ns, 2 * LANES), F32)]
        + [SDS((ns, LANES), F32)],
        scratch_shapes=[state] * 2 + [pltpu.VMEM((c, ns, 2 * LANES), BF16), pltpu.VMEM((c // 2, ns, 2 * LANES), F32)]
        + [pltpu.VMEM((c, 1, RW), F32)] * 9,
        compiler_params=_params(), name="wkv_fwd")(r, w, k, v, a, b)


def _wkv_bwd(r, w, k, a, b, dy, sst, sast, vst, fin):
    length = r.shape[0]
    c = min(WKV_CHUNK, length)
    nc = length // c
    ns = 4 * HN

    def body(r_ref, w_ref, k_ref, a_ref, b_ref, dy_ref, sst_ref, sast_ref, vst_ref, snext_ref, fin_ref,
             dr_ref, dw_ref, dk_ref, da_ref, db_ref, dv_ref,
             ds_ref, cur_ref, acc_ref, dyst, lhs_ref, dvsum_ref, rb, wb, kb, ab, bb, dyb, wbb, alb, rhb,
             p_r, p_w, p_k, p_a, p_b):
        @pl.when(pl.program_id(0) == 0)
        def _():
            ds_ref[...] = jnp.zeros_like(ds_ref)

        b_prev = pltpu.roll(b_ref[...], 1, 0)
        o512 = _block_ones(RW, HN)
        _wkv_prebroadcast((r_ref, w_ref, k_ref, a_ref, b_ref, dy_ref, w_ref[...] * b_prev,
                           _segsum_impl(a_ref[...] * b_prev, o512), _segsum_impl(r_ref[...] * b_ref[...], o512)),
                          (rb, wb, kb, ab, bb, dyb, wbb, alb, rhb), c)
        lane_t, diag, ones = _wkv_consts()

        def fill(q, carry):
            for half in range(2):
                lhs_ref[q, :, half * LANES:(half + 1) * LANES] = (_wkv_full(dyb, 2 * q + half) * diag).astype(BF16)
            return carry

        lax.fori_loop(0, c // 2, fill, 0)
        _wkv_pair_sums(lhs_ref, dyst)
        cur_ref[...] = jnp.where(pl.program_id(0) == 0, fin_ref[...], snext_ref[0])

        def pair(p, carry):
            ta = c - 1 - 2 * p
            tb = ta - 1
            s_a, s_b, s_c = cur_ref[...], sst_ref[ta], sst_ref[tb]
            dy_pair = dyst[c // 2 - 1 - p]
            dy_a, dy_b = dy_pair[:, LANES:], dy_pair[:, :LANES]
            ds_a = ds_ref[...] + dy_a * _wkv_full(rb, ta)
            both = _wkv_rowseg(jnp.concatenate([ds_a * _wkv_full(bb, ta), ds_a * _wkv_full(wbb, ta)], axis=0), ones)
            dsa_a = both[:ns]
            ds_b = ds_a * _wkv_full(wb, ta) + dsa_a * _wkv_full(ab, ta) + dy_b * _wkv_full(rb, tb)
            dsa_b = both[ns:] + dsa_a * _wkv_full(alb, ta) + dy_b * _wkv_full(rhb, tb)
            ds_ref[...] = ds_b * _wkv_full(wb, tb) + dsa_b * _wkv_full(ab, tb)
            cur_ref[...] = s_c
            lhs_ref[p, :, 0:LANES] = (ds_a * _wkv_full(kb, ta)).astype(BF16)
            lhs_ref[p, :, LANES:2 * LANES] = (ds_b * _wkv_full(kb, tb)).astype(BF16)
            v_pair = vst_ref[c // 2 - 1 - p]
            for t, s_t, s_prev, ds, dsa, dyy, vv in ((ta, s_a, s_b, ds_a, dsa_a, dy_a, v_pair[:, LANES:]),
                                                     (tb, s_b, s_c, ds_b, dsa_b, dy_b, v_pair[:, :LANES])):
                p_r[t] = _wkv_colsum8(s_t * dyy)
                p_k[t] = _wkv_colsum8(ds * vv)
                p_b[t] = _wkv_colsum8(ds * sast_ref[t])
                p_w[t] = _wkv_colsum8(ds * s_prev)
                p_a[t] = _wkv_colsum8(s_prev * dsa)
            return carry

        lax.fori_loop(0, c // 2, pair, 0)
        _wkv_pair_sums(lhs_ref, dvsum_ref)
        acc_ref[...] = jnp.zeros_like(acc_ref)

        def gather_dv(p, carry):
            both = dvsum_ref[p]
            acc = jnp.where(lane_t == c - 1 - 2 * p, both[:, :LANES], acc_ref[...])
            acc_ref[...] = jnp.where(lane_t == c - 2 - 2 * p, both[:, LANES:], acc)
            return carry

        lax.fori_loop(0, c // 2, gather_dv, 0)
        sel = (lax.broadcasted_iota(jnp.int32, (c, c * SUBLANES), 1) // SUBLANES
               == lax.broadcasted_iota(jnp.int32, (c, c * SUBLANES), 0)).astype(F32)
        for out_ref, part in ((dr_ref, p_r), (dw_ref, p_w), (dk_ref, p_k), (da_ref, p_a), (db_ref, p_b)):
            out_ref[...] = _dot3(sel, part[...].reshape(c * SUBLANES, RW), 1, 0)
        dv_ref[...] = _wkv_transpose_out(acc_ref[...], c, diag)

    rmap = lambda i: nc - 1 - i
    row = pl.BlockSpec((c, RW), lambda i: (rmap(i), 0))
    st = pl.BlockSpec((c, ns, LANES), lambda i: (rmap(i), 0, 0))
    nxt = pl.BlockSpec((1, ns, LANES), lambda i: (jnp.minimum((rmap(i) + 1) * c, length - 1), 0, 0))
    state = pltpu.VMEM((ns, LANES), F32)
    st_pair = pl.BlockSpec((c // 2, ns, 2 * LANES), lambda i: (rmap(i), 0, 0))
    return pl.pallas_call(
        body, grid=(nc,), in_specs=[row] * 6 + [st, st, st_pair, nxt, pl.BlockSpec((ns, LANES), lambda i: (0, 0))],
        out_specs=[row] * 6, out_shape=[SDS((length, RW), F32)] * 6,
        scratch_shapes=[state] * 3 + [pltpu.VMEM((c // 2, ns, 2 * LANES), F32), pltpu.VMEM((c, ns, 2 * LANES), BF16),
                                       pltpu.VMEM((c // 2, ns, 2 * LANES), F32)]
        + [pltpu.VMEM((c, 1, RW), F32)] * 9 + [pltpu.VMEM((c, SUBLANES, RW), F32)] * 5,
        compiler_params=_params(), name="wkv_bwd")(r, w, k, a, b, dy, sst, sast, vst, sst, fin)


def _my_place():
    return lax.axis_index("x"), lax.axis_index("y"), lax.axis_index("c")


def _peer(x, y, c, k):
    return (x ^ ((k >> 2) & 1), y ^ ((k >> 1) & 1), c ^ (k & 1))


def _all_gather(shard, name):
    def body(in_ref, out_ref, send_sems, recv_sems, local_sem):
        x, y, c = _my_place()
        me = 4 * x + 2 * y + c
        mine = pltpu.make_async_copy(in_ref, out_ref.at[me], local_sem)
        mine.start()
        sends = []
        for k in range(1, N_DEV):
            cp = pltpu.make_async_remote_copy(src_ref=in_ref, dst_ref=out_ref.at[me], send_sem=send_sems.at[k - 1],
                                              recv_sem=recv_sems.at[k - 1], device_id=_peer(x, y, c, k), device_id_type=MESH)
            cp.start()
            sends.append(cp)
        for k in range(1, N_DEV):
            px, py, pc = _peer(x, y, c, k)
            pltpu.make_async_remote_copy(src_ref=in_ref, dst_ref=out_ref.at[4 * px + 2 * py + pc], send_sem=send_sems.at[k - 1],
                                         recv_sem=recv_sems.at[k - 1], device_id=(px, py, pc), device_id_type=MESH).wait_recv()
        for cp in sends:
            cp.wait_send()
        mine.wait()

    return pl.pallas_call(
        body, out_shape=SDS((N_DEV, *shard.shape), shard.dtype),
        in_specs=[pl.BlockSpec(memory_space=pl.ANY)], out_specs=pl.BlockSpec(memory_space=pl.ANY),
        scratch_shapes=[pltpu.SemaphoreType.DMA((N_DEV - 1,)), pltpu.SemaphoreType.DMA((N_DEV - 1,)), pltpu.SemaphoreType.DMA],
        name=name)(shard)


_HBM_SPEC = pl.BlockSpec(memory_space=pltpu.HBM)
_SEM_SPEC = pl.BlockSpec(memory_space=pltpu.SEMAPHORE)
_DATAFLOW = pltpu.SideEffectType.DATAFLOW_SIDE_EFFECTING


def _exchange_start(srcs, per_peer, name):
    n = len(srcs)
    lands = [jnp.zeros((N_DEV, *s.shape[-2:]), s.dtype) for s in srcs]

    def body(*refs):
        src_refs, land_refs = refs[:n], refs[n:2 * n]
        send_sems, recv_sems, token = refs[2 * n], refs[2 * n + 1], refs[-1]
        x, y, c = _my_place()
        me = 4 * x + 2 * y + c
        for a in range(n):
            for k in range(1, N_DEV):
                px, py, pc = _peer(x, y, c, k)
                mine = src_refs[a].at[4 * px + 2 * py + pc] if per_peer else src_refs[a]
                pltpu.make_async_remote_copy(src_ref=mine, dst_ref=land_refs[a].at[me], send_sem=send_sems.at[7 * a + k - 1],
                                             recv_sem=recv_sems.at[7 * a + k - 1], device_id=(px, py, pc),
                                             device_id_type=MESH).start()
        token[...] = jnp.zeros_like(token)

    n_sem = (N_DEV - 1) * n
    res = pl.pallas_call(
        body, name=name,
        out_shape=(pltpu.SemaphoreType.DMA((n_sem,)), pltpu.SemaphoreType.DMA((n_sem,)),
                   *[pltpu.HBM(a.shape, a.dtype) for a in srcs + lands], SDS((SUBLANES, LANES), F32)),
        in_specs=(_HBM_SPEC,) * (2 * n),
        out_specs=(_SEM_SPEC, _SEM_SPEC) + (_HBM_SPEC,) * (2 * n) + (pl.BlockSpec(memory_space=pltpu.VMEM),),
        input_output_aliases={i: 2 + i for i in range(2 * n)}, compiler_params=pltpu.CompilerParams(has_side_effects=_DATAFLOW),
    )(*[pltpu.with_memory_space_constraint(a, pltpu.HBM) for a in srcs + lands])
    return res[0], res[1], list(res[2:2 + n]), list(res[2 + n:2 + 2 * n]), res[-1]


def _exchange_wait(started, after, per_peer, name):
    send_sems, recv_sems, src_thrus, land_thrus, _ = started
    n = len(src_thrus)

    def body(*refs):
        src_refs, land_refs = refs[:n], refs[n:2 * n]
        send_sems, recv_sems = refs[2 * n], refs[2 * n + 1]
        x, y, c = _my_place()
        me = 4 * x + 2 * y + c
        for a in range(n):
            for k in range(1, N_DEV):
                px, py, pc = _peer(x, y, c, k)
                mine = src_refs[a].at[me] if per_peer else src_refs[a]
                copy = pltpu.make_async_remote_copy(src_ref=mine, dst_ref=land_refs[a].at[4 * px + 2 * py + pc],
                                                    send_sem=send_sems.at[7 * a + k - 1], recv_sem=recv_sems.at[7 * a + k - 1],
                                                    device_id=(px, py, pc), device_id_type=MESH)
                copy.wait_send()
                copy.wait_recv()

    res = pl.pallas_call(
        body, name=name, out_shape=tuple(pltpu.HBM(a.shape, a.dtype) for a in src_thrus + land_thrus),
        in_specs=(_HBM_SPEC,) * (2 * n) + (_SEM_SPEC, _SEM_SPEC, pl.BlockSpec(memory_space=pl.ANY)),
        out_specs=(_HBM_SPEC,) * (2 * n), input_output_aliases={i: i for i in range(2 * n)},
        compiler_params=pltpu.CompilerParams(has_side_effects=_DATAFLOW),
    )(*src_thrus, *land_thrus, send_sems, recv_sems, after)
    return list(res[n:])


def _row_tile(rows, cap):
    best = None
    for t in range(2 * SUBLANES, min(rows, cap) + 1, 2 * SUBLANES):
        if rows % t == 0:
            best = t
    return best or rows


def _sum_parts(parts, name, own=None):
    rows, cols = parts.shape[1:]

    def body(*refs):
        p_ref, g_out = refs[0], refs[-1]
        g = p_ref[0].astype(F32)
        for s in range(1, N_DEV):
            g = g + p_ref[s].astype(F32)
        if own is not None:
            g = g + refs[1][...].astype(F32)
        g_out[...] = g

    tr = _row_tile(rows, 384 * 1024 // cols)
    blk = pl.BlockSpec((tr, cols), lambda i: (i, 0))
    extra = [] if own is None else [own]
    return pl.pallas_call(
        body, grid=(rows // tr,),
        in_specs=[pl.BlockSpec((N_DEV, tr, cols), lambda i: (0, i, 0))] + [blk] * len(extra),
        out_specs=blk, out_shape=SDS((rows, cols), F32),
        compiler_params=_params(("parallel",)), name=name)(parts, *extra)


def _adamw(g, wgt, m, v, name):
    rows, cols = wgt.shape

    def body(g_ref, w_ref, m_ref, v_ref, d_out, m_out, v_out):
        g_ = g_ref[...]
        m_new = ADAM_B1 * m_ref[...] + (1.0 - ADAM_B1) * g_
        v_new = ADAM_B2 * v_ref[...] + (1.0 - ADAM_B2) * (g_ * g_)
        m_hat = m_new / (1.0 - ADAM_B1 ** ADAM_STEP)
        v_hat = v_new / (1.0 - ADAM_B2 ** ADAM_STEP)
        d_out[...] = -ADAM_LR * (m_hat / (jnp.sqrt(v_hat) + ADAM_EPS) + ADAM_WD * w_ref[...])
        m_out[...] = m_new
        v_out[...] = v_new

    tr = _row_tile(rows, 512 * 1024 // cols)
    blk = pl.BlockSpec((tr, cols), lambda i: (i, 0))
    return pl.pallas_call(
        body, grid=(rows // tr,), in_specs=[blk] * 4, out_specs=[blk] * 3, out_shape=[SDS((rows, cols), F32)] * 3,
        compiler_params=_params(("parallel",)), name=name)(g, wgt, m, v)


PACK_ALIGN = 2 * SUBLANES * LANES
PACK_ROWS = 512

ROW_SHARDED = ("s5_w_glu", "w_out", "ffn_w_down")
TRANSPOSED = ("w_in", "ffn_w_up", "w_branch_rwkv", "w_branch_s5")
SMALL_SHARDED = ("rwkv_w2", "rwkv_a2", "rwkv_g2", "ffn_conv_w")
SHARDED = ROW_SHARDED + TRANSPOSED + SMALL_SHARDED
F32_GATHER = ("ffn_conv_w",)
REPLICATED =("norm_mix_pre", "norm_mix_post", "norm_ffn_pre", "norm_ffn_post", "b_gate", "rwkv_shift_mu", "rwkv_w0",
              "rwkv_a0", "rwkv_k_k", "rwkv_k_a", "rwkv_r_k", "rwkv_lnx_w", "rwkv_lnx_b", "s5_a_re", "s5_a_im", "s5_b_re",
              "s5_b_im", "s5_c_re", "s5_c_im", "s5_d", "s5_log_step", "s5_b_glu", "ffn_conv_b")
WEIGHTS = ("norm_mix_pre", "norm_mix_post", "norm_ffn_pre", "norm_ffn_post", "w_in", "b_gate", "rwkv_shift_mu", "rwkv_w0",
           "rwkv_w2", "rwkv_a0", "rwkv_a2", "rwkv_g2", "rwkv_k_k", "rwkv_k_a", "rwkv_r_k", "rwkv_lnx_w", "rwkv_lnx_b",
           "s5_a_re", "s5_a_im", "s5_b_re", "s5_b_im", "s5_c_re", "s5_c_im", "s5_d", "s5_log_step", "s5_w_glu", "s5_b_glu",
           "w_branch_rwkv", "w_branch_s5", "w_out", "ffn_w_up", "ffn_conv_w", "ffn_conv_b", "ffn_w_down")


def _pad_flat(a):
    flat = a.reshape(-1)
    pad = (-flat.shape[0]) % PACK_ALIGN
    return jnp.pad(flat, (0, pad)) if pad else flat


def _pad_cols(a, mult):
    pad = (-a.shape[1]) % mult
    return jnp.pad(a, ((0, 0), (0, pad))) if pad else a


def _pack(pieces):
    flat = jnp.concatenate([_pad_flat(p) for p in pieces])
    pad = (-flat.shape[0]) % (PACK_ROWS * LANES)
    return (jnp.pad(flat, (0, pad)) if pad else flat).reshape(-1, LANES)


def _unpack(buf, shapes):
    lead = buf.shape[:-2]
    flat = buf.reshape(*lead, -1)
    out, off = [], 0
    for s in shapes:
        n = math.prod(s)
        out.append(flat[..., off:off + n].reshape(*lead, *s))
        off += n + (-n) % PACK_ALIGN
    return out


def _rows_to_full(stack):
    return stack.reshape(-1, stack.shape[2])


def _cols_to_full(stack):
    return stack.transpose(1, 0, 2).reshape(stack.shape[1], -1)


EARLY_GRADS = ("ffn_w_down", "ffn_w_up", "w_out")
REST_GRADS = ("s5_w_glu", "w_in", "w_branch_rwkv", "w_branch_s5")


def _local_step(x, tgt, wt, get_late, send_early, send_rest):
    wt = dict(wt)
    o512 = _block_ones(RW, HN)
    w_in = wt["w_in"]
    w_rw, w_u, w_g = w_in[:N_RWKV], w_in[N_RWKV:N_RWKV + S5W], w_in[N_RWKV + S5W:]
    b_gate = wt["b_gate"]
    b_r, b_s = b_gate[:, :D_MODEL], b_gate[:, D_MODEL:]
    zpad = lambda a, lo, n: jnp.pad(a.astype(F32), ((lo, n - lo - a.shape[0]), (0, 0)))
    w2p, a2p, g2p = zpad(wt["rwkv_w2"], 0, 256), zpad(wt["rwkv_a2"], 64, 256), zpad(wt["rwkv_g2"], 128, 256)
    pre_small = [wt["rwkv_w0"], wt["rwkv_a0"], wt["rwkv_k_k"], wt["rwkv_k_a"], w2p, a2p, g2p]
    post_prm = [wt["rwkv_lnx_w"], wt["rwkv_lnx_b"], wt["rwkv_r_k"].reshape(1, RW)]
    mu = wt["rwkv_shift_mu"]

    a_re, a_im = wt["s5_a_re"].reshape(1, S5N), wt["s5_a_im"].reshape(1, S5N)
    ls = jnp.repeat(wt["s5_log_step"].reshape(S5G), S5P).reshape(1, S5N)
    b_re_t = wt["s5_b_re"].reshape(S5N, S5C).T
    b_im_t = wt["s5_b_im"].reshape(S5N, S5C).T
    c_re, c_im = wt["s5_c_re"].reshape(S5G, S5C, S5P), wt["s5_c_im"].reshape(S5G, S5C, S5P)
    abr, abi, bbr, bbi = _s5_prep_fwd(a_re, a_im, ls, b_re_t, b_im_t)
    bblk = jnp.concatenate([_s5_blockdiag_in(bbr), _s5_blockdiag_in(bbi)], axis=2)
    cblk = jnp.concatenate([_s5_blockdiag_out(c_re), -_s5_blockdiag_out(c_im)], axis=1)
    s5_d = wt["s5_d"]

    h1 = _rms_fwd(x, wt["norm_mix_pre"], "rms1_fwd")
    p_rw = _mm(h1, w_rw, tb=True, name="mm_proj_rwkv")
    u = _mm(h1, w_u, tb=True, name="mm_proj_s5")
    gpre = _mm(h1, w_g, tb=True, name="mm_proj_gate")
    r, decay, k2, v, aa, bb, g = _rwkv_pre_fwd(p_rw, mu, pre_small, o512)
    y, sst, sast, vst, s_fin = _wkv_fwd(r, decay, k2, v, aa, bb)
    wt.update(get_late(y))
    y_r = _rwkv_post_fwd(y, r, k2, v, g, post_prm, o512)
    o_r = _mm(y_r, wt["w_branch_rwkv"], tb=True, name="mm_branch_rwkv")
    ys, kre, kim = _s5_fwd(u, bblk, cblk, abr, abi, s5_d)
    t_glu = _mm(ys, wt["s5_w_glu"], name="mm_glu")
    out5 = _glu_fwd(ys, t_glu, wt["s5_b_glu"])
    o_s = _mm(out5, wt["w_branch_s5"], tb=True, name="mm_branch_s5")
    mi = _merge_fwd(gpre, o_r, o_s, b_r, b_s)
    mixed = _mm(mi, wt["w_out"], name="mm_out")
    x2, h2 = _mid_fwd(x, mixed, wt["norm_mix_post"], wt["norm_ffn_pre"])
    z = _mm(h2, wt["ffn_w_up"], tb=True, name="mm_up")
    act = _conv_fwd(z, wt["ffn_conv_w"], wt["ffn_conv_b"])
    f = _mm(act, wt["ffn_w_down"], name="mm_down")
    dx3, df, d_norm_ffn_post, loss_part = _final(x2, f, tgt, wt["norm_ffn_post"])

    gr = {"norm_ffn_post": d_norm_ffn_post}
    dact = _mm(df, wt["ffn_w_down"], tb=True, name="mm_down_dx")
    gr["ffn_w_down"] = _mm(act, df, ta=True, out_dtype=BF16, name="mm_down_dw")
    dzg, dzv, gr["ffn_conv_w"], gr["ffn_conv_b"] = _conv_bwd(z, dact, wt["ffn_conv_w"], wt["ffn_conv_b"])
    dz = jnp.concatenate([dzg, dzv], axis=1)
    dh2 = _mm(dz, wt["ffn_w_up"], name="mm_up_dx")
    gr["ffn_w_up"] = _mm(dz, h2, ta=True, out_dtype=BF16, name="mm_up_dw")
    dx2, dmixed, gr["norm_mix_post"], gr["norm_ffn_pre"] = _mid_bwd(x2, mixed, dh2, dx3, wt["norm_mix_post"], wt["norm_ffn_pre"])
    dmi = _mm(dmixed, wt["w_out"], tb=True, name="mm_out_dx")
    gr["w_out"] = _mm(mi, dmixed, ta=True, out_dtype=BF16, name="mm_out_dw")
    token = send_early({n: gr[n] for n in EARLY_GRADS})
    b_r = b_r + jnp.tile(token[0:1, :], (1, D_MODEL // LANES))
    dgp_r, dgp_s, do_r, do_s, db_r, db_s = _merge_bwd(gpre, o_r, o_s, dmi, b_r, b_s)
    gr["b_gate"] = jnp.concatenate([db_r, db_s], axis=1)
    dout5 = _mm(do_s, wt["w_branch_s5"], name="mm_branch_s5_dx")
    gr["w_branch_s5"] = _mm(do_s, out5, ta=True, out_dtype=BF16, name="mm_branch_s5_dw")
    dys_a, dt_glu, gr["s5_b_glu"] = _glu_bwd(ys, t_glu, dout5, wt["s5_b_glu"])
    dys_b = _mm(dt_glu, wt["s5_w_glu"], tb=True, name="mm_glu_dx")
    gr["s5_w_glu"] = _mm(ys, dt_glu, ta=True, out_dtype=BF16, name="mm_glu_dw")
    du, dbblk, dcblk, dabr, dabi, gr["s5_d"] = _s5_bwd(u, dys_a, dys_b, kre, kim, bblk, cblk, abr, abi, s5_d)
    gr["s5_c_re"] = _s5_blockdiag_out_t(dcblk[:, :S5_BN, :]).reshape(wt["s5_c_re"].shape)
    gr["s5_c_im"] = (-_s5_blockdiag_out_t(dcblk[:, S5_BN:, :])).reshape(wt["s5_c_im"].shape)
    dbbr, dbbi = _s5_blockdiag_in_t(dbblk[:, :, :S5_BN]), _s5_blockdiag_in_t(dbblk[:, :, S5_BN:])
    gsel = (lax.broadcasted_iota(jnp.int32, (S5N, LANES), 0) // S5P == lax.broadcasted_iota(jnp.int32, (S5N, LANES), 1)).astype(F32)
    d_are, d_aim, d_ls, d_bre_t, d_bim_t = _s5_prep_bwd(a_re, a_im, ls, b_re_t, b_im_t, (dabr, dabi, dbbr, dbbi), gsel)
    gr["s5_a_re"] = d_are.reshape(wt["s5_a_re"].shape)
    gr["s5_a_im"] = d_aim.reshape(wt["s5_a_im"].shape)
    gr["s5_log_step"] = d_ls[0:1, :S5G]
    gr["s5_b_re"] = d_bre_t.T.reshape(wt["s5_b_re"].shape)
    gr["s5_b_im"] = d_bim_t.T.reshape(wt["s5_b_im"].shape)
    dy_r = _mm(do_r, wt["w_branch_rwkv"], name="mm_branch_rwkv_dx")
    gr["w_branch_rwkv"] = _mm(do_r, y_r, ta=True, out_dtype=BF16, name="mm_branch_rwkv_dw")
    dy, dr1, dk1, dv1, dg, gr["rwkv_lnx_w"], gr["rwkv_lnx_b"], d_rk = _rwkv_post_bwd(y, r, k2, v, g, dy_r, post_prm, o512)
    gr["rwkv_r_k"] = d_rk.reshape(wt["rwkv_r_k"].shape)
    dr2, ddecay, dk2, daa, dbb, dv2 = _wkv_bwd(r, decay, k2, aa, bb, dy, sst, sast, vst, s_fin)
    cots = [dr1, ddecay, dk1, dv1, daa, dbb, dg, dr2, dk2, dv2]
    dp_rw, gr["rwkv_shift_mu"], gr["rwkv_w0"], gr["rwkv_a0"], gr["rwkv_k_k"], gr["rwkv_k_a"], dw2p, da2p, dg2p = \
        _rwkv_pre_bwd(p_rw, cots, mu, pre_small, o512)
    gr["rwkv_w2"], gr["rwkv_a2"], gr["rwkv_g2"] = dw2p[:64], da2p[64:128], dg2p[128:]
    dproj = jnp.concatenate([dp_rw, du, dgp_r, dgp_s], axis=1)
    gr["w_in"] = _mm(dproj, h1, ta=True, out_dtype=BF16, name="mm_proj_dw")
    token = send_rest(gr)
    dh1 = _mm(dproj, w_in + token[0, 0].astype(w_in.dtype), name="mm_proj_dx")
    dx, gr["norm_mix_pre"] = _rms_bwd(x, wt["norm_mix_pre"], dh1, dx2, "rms1_bwd")
    return loss_part[0, 0], dx, gr


def kernel(x, norm_mix_pre, norm_mix_post, norm_ffn_pre, norm_ffn_post, w_in, b_gate, rwkv_shift_mu, rwkv_w0, rwkv_w2, rwkv_a0, rwkv_a2, rwkv_g2, rwkv_k_k, rwkv_k_a, rwkv_r_k, rwkv_lnx_w, rwkv_lnx_b, s5_a_re, s5_a_im, s5_b_re, s5_b_im, s5_c_re, s5_c_im, s5_d, s5_log_step, s5_w_glu, s5_b_glu, w_branch_rwkv, w_branch_s5, w_out, ffn_w_up, ffn_conv_w, ffn_conv_b, ffn_w_down, loss_target, m_norm_mix_pre, m_norm_mix_post, m_norm_ffn_pre, m_norm_ffn_post, m_w_in, m_b_gate, m_rwkv_shift_mu, m_rwkv_w0, m_rwkv_w2, m_rwkv_a0, m_rwkv_a2, m_rwkv_g2, m_rwkv_k_k, m_rwkv_k_a, m_rwkv_r_k, m_rwkv_lnx_w, m_rwkv_lnx_b, m_s5_a_re, m_s5_a_im, m_s5_b_re, m_s5_b_im, m_s5_c_re, m_s5_c_im, m_s5_d, m_s5_log_step, m_s5_w_glu, m_s5_b_glu, m_w_branch_rwkv, m_w_branch_s5, m_w_out, m_ffn_w_up, m_ffn_conv_w, m_ffn_conv_b, m_ffn_w_down, v_norm_mix_pre, v_norm_mix_post, v_norm_ffn_pre, v_norm_ffn_post, v_w_in, v_b_gate, v_rwkv_shift_mu, v_rwkv_w0, v_rwkv_w2, v_rwkv_a0, v_rwkv_a2, v_rwkv_g2, v_rwkv_k_k, v_rwkv_k_a, v_rwkv_r_k, v_rwkv_lnx_w, v_rwkv_lnx_b, v_s5_a_re, v_s5_a_im, v_s5_b_re, v_s5_b_im, v_s5_c_re, v_s5_c_im, v_s5_d, v_s5_log_step, v_s5_w_glu, v_s5_b_glu, v_w_branch_rwkv, v_w_branch_s5, v_w_out, v_ffn_w_up, v_ffn_conv_w, v_ffn_conv_b, v_ffn_w_down):
    args = dict(locals())
    wgt = {n: args[n] for n in WEIGHTS}
    mom = {n: args["m_" + n] for n in WEIGHTS}
    var = {n: args["v_" + n] for n in WEIGHTS}
    me = 4 * lax.axis_index("x") + 2 * lax.axis_index("y") + lax.axis_index("c")
    shard_shapes = {n: wgt[n].shape[1:] for n in SHARDED}
    sent_shapes = {n: (shard_shapes[n][::-1] if n in TRANSPOSED else shard_shapes[n]) for n in SHARDED}
    sent = lambda n, a: a.T if n in TRANSPOSED else a

    unshard = lambda n, blocks: _cols_to_full(blocks) if n in SMALL_SHARDED else _rows_to_full(blocks)

    first = ("w_in", "rwkv_w2", "rwkv_a2", "rwkv_g2")
    late = tuple(n for n in SHARDED if n not in first and n not in F32_GATHER)
    late_payloads = [sent(n, wgt[n][0].astype(BF16)) for n in late] + [_pack([wgt[n][0] for n in F32_GATHER])]
    late_started = _exchange_start(late_payloads, False, "gather_late_start")
    got_first = _unpack(_all_gather(_pack([sent(n, wgt[n][0].astype(BF16)) for n in first]), "all_gather_weights"),
                        [sent_shapes[n] for n in first])
    wt = {n: wgt[n] for n in REPLICATED}
    for n, blocks in zip(first, got_first, strict=True):
        wt[n] = unshard(n, blocks)

    def get_late(after):
        lands = _exchange_wait(late_started, after, False, "gather_late_wait")
        whole_ = [lax.dynamic_update_index_in_dim(land, mine, me, 0) for land, mine in zip(lands, late_payloads, strict=True)]
        out = {n: _rows_to_full(blocks) for n, blocks in zip(late, whole_[:-1], strict=True)}
        taps = _unpack(whole_[-1], [sent_shapes[n] for n in F32_GATHER])
        out.update({n: unshard(n, b_) for n, b_ in zip(F32_GATHER, taps, strict=True)})
        return out

    whole = tuple(n for n in SMALL_SHARDED + REPLICATED if n != "norm_mix_pre")
    blocks_of = lambda g_: g_.reshape(N_DEV, -1, g_.shape[-1])
    started = {}

    def send_early(grads_):
        started["early"] = _exchange_start([blocks_of(grads_[n]) for n in EARLY_GRADS], True, "exchange_early_start")
        return started["early"][4]

    def send_rest(grads_):
        started["rest"] = _exchange_start([blocks_of(grads_[n]) for n in REST_GRADS], True, "exchange_rest_start")
        started["small_mine"] = _pack([grads_[n] for n in whole])
        started["small"] = _exchange_start([started["small_mine"]], False, "gather_small_start")
        return started["rest"][4] + started["small"][4]

    loss_part, dx, gr = _local_step(x[0], loss_target[0], wt, get_late, send_early, send_rest)

    grads, upd = {}, {}
    flat2d = lambda a: a.reshape(a.shape[-2:])

    def finish_large(key, names, after):
        lands = _exchange_wait(started[key], after, True, "exchange_" + key + "_wait")
        for n, land in zip(names, lands, strict=True):
            own = lax.dynamic_index_in_dim(blocks_of(gr[n]), me, 0, keepdims=False)
            total = _sum_parts(land, "sum_" + n, own=own)
            grads[n] = (total.T if n in TRANSPOSED else total).reshape(wgt[n].shape)
            trio = _adamw(flat2d(grads[n]), flat2d(wgt[n]), flat2d(mom[n]), flat2d(var[n]), "adamw_" + n)
            upd[n] = tuple(t.reshape(wgt[n].shape) for t in trio)
        return upd[names[-1]][0]

    done = finish_large("early", EARLY_GRADS, dx)
    small_land, = _exchange_wait(started["small"], done, False, "gather_small_wait")
    small_sum = _sum_parts(lax.dynamic_update_index_in_dim(small_land, started["small_mine"], me, 0), "sum_small_grads")
    got = dict(zip(whole, _unpack(small_sum, [gr[n].shape for n in whole]), strict=True))
    done = finish_large("rest", REST_GRADS, small_sum)
    last, _ = lax.optimization_barrier((gr["norm_mix_pre"].reshape(SUBLANES, LANES), done))
    got["norm_mix_pre"] = _sum_parts(_all_gather(last, "gather_last_grad"), "sum_last_grad").reshape(gr["norm_mix_pre"].shape)
    packed = tuple(n for n in WEIGHTS if n not in EARLY_GRADS + REST_GRADS)
    for n in packed:
        if n in SMALL_SHARDED:
            cols = shard_shapes[n][1]
            grads[n] = lax.dynamic_slice_in_dim(got[n], me * cols, cols, axis=1).reshape(wgt[n].shape)
        else:
            grads[n] = got[n].reshape(wgt[n].shape)

    pack_local = lambda src: _pack([src[n].reshape(-1) for n in packed])
    outs = _adamw(pack_local(grads), pack_local(wgt), pack_local(mom), pack_local(var), "adamw_small")
    shapes = [wgt[n].shape for n in packed]
    upd.update(zip(packed, zip(*[_unpack(o, shapes) for o in outs], strict=True), strict=True))
    loss = lax.psum(loss_part, AXES)
    return (loss, dx[None], *[grads[n] for n in WEIGHTS], *[upd[n][0] for n in WEIGHTS], *[upd[n][1] for n in WEIGHTS],
            *[upd[n][2] for n in WEIGHTS])
```

```python
import functools
import math

import jax
import jax.numpy as jnp
from jax import lax
from jax.experimental import pallas as pl
from jax.experimental.pallas import tpu as pltpu

F32 = jnp.float32
BF16 = jnp.bfloat16
SDS = jax.ShapeDtypeStruct
HI = lax.Precision.HIGHEST
MESH = pl.DeviceIdType.MESH
AXES = ("x", "y", "c")
N_DEV = 8

D_MODEL = 1024
RW = 512
HN = 64
N_RWKV = 1792
S5W = 512
S5G = 32
S5C = 16
S5P = 64
S5N = S5G * S5P
D_FF = 2816
NORM_EPS = 1e-6
LNX_EPS = 64e-5

ADAM_LR = 0.001
ADAM_B1 = 0.9
ADAM_B2 = 0.999
ADAM_EPS = 1e-08
ADAM_WD = 0.01
ADAM_STEP = 10

LANES = 128
SUBLANES = 8
VMEM_LIMIT = 56 * 1024 * 1024
WKV_CHUNK = 32


def _params(sem=("arbitrary",)):
    return pltpu.CompilerParams(dimension_semantics=sem, vmem_limit_bytes=VMEM_LIMIT)


def _pick(n, cap):
    best = None
    for t in range(LANES, min(n, cap) + 1, LANES):
        if n % t == 0:
            best = t
    return best or n


MM_VMEM_BUDGET = 40 * 1024 * 1024


def _mm_tiles(m, n, k, a_bytes, b_bytes, o_bytes):
    def divisors(d):
        return [t for t in range(LANES, d + 1, LANES) if d % t == 0] or [d]

    tn = _pick(n, 2304)
    best = None
    for tm in divisors(m):
        for tk in divisors(k):
            vmem = 2 * (tm * tk * a_bytes + tk * tn * b_bytes) + tm * tn * 4 + 2 * tm * tn * o_bytes
            if vmem > MM_VMEM_BUDGET:
                continue
            key = ((m // tm) * (n // tn) * (k // tk), -tm)
            if best is None or key < best[0]:
                best = (key, tm, tk)
    assert best is not None, (m, n, k)
    return best[1], tn, best[2]


def _mm(a, b, *, ta=False, tb=False, out_dtype=F32, name):
    m = a.shape[1] if ta else a.shape[0]
    k = a.shape[0] if ta else a.shape[1]
    n = b.shape[0] if tb else b.shape[1]
    assert (b.shape[1] if tb else b.shape[0]) == k
    tm, tn, tk = _mm_tiles(m, n, k, a.dtype.itemsize, b.dtype.itemsize, jnp.dtype(out_dtype).itemsize)
    nk = k // tk
    dims = (((0 if ta else 1,), (1 if tb else 0,)), ((), ()))

    def body(a_ref, b_ref, o_ref, acc_ref):
        kk = pl.program_id(2)

        @pl.when(kk == 0)
        def _():
            acc_ref[...] = jnp.zeros_like(acc_ref)

        acc_ref[...] += lax.dot_general(a_ref[...].astype(BF16), b_ref[...].astype(BF16), dims,
                                        preferred_element_type=F32)

        @pl.when(kk == nk - 1)
        def _():
            o_ref[...] = acc_ref[...].astype(o_ref.dtype)

    a_spec = pl.BlockSpec((tk, tm), lambda i, j, kk: (kk, i)) if ta else pl.BlockSpec((tm, tk), lambda i, j, kk: (i, kk))
    b_spec = pl.BlockSpec((tn, tk), lambda i, j, kk: (j, kk)) if tb else pl.BlockSpec((tk, tn), lambda i, j, kk: (kk, j))
    return pl.pallas_call(
        body, grid=(m // tm, n // tn, nk), in_specs=[a_spec, b_spec],
        out_specs=pl.BlockSpec((tm, tn), lambda i, j, kk: (i, j)),
        out_shape=SDS((m, n), out_dtype), scratch_shapes=[pltpu.VMEM((tm, tn), F32)],
        compiler_params=_params(("parallel", "parallel", "arbitrary")), name=name)(a, b)


def _rows(fn, rows, params, out_rows, out_accs, *, name, tl, reverse=False, scratch=()):
    first = rows[0][0] if isinstance(rows[0], tuple) else rows[0]
    length = first.shape[0]
    tl = min(tl, length)
    nt = length // tl
    rmap = (lambda i: nt - 1 - i) if reverse else (lambda i: i)
    specs, arrs = [], []
    for r in rows:
        arr, wdt, cb = r if isinstance(r, tuple) else (r, r.shape[1], 0)
        specs.append(pl.BlockSpec((tl, wdt), lambda i, cb=cb: (rmap(i), cb)))
        arrs.append(arr)
    for p in params:
        specs.append(pl.BlockSpec(p.shape, lambda i, nd=p.ndim: (0,) * nd))
        arrs.append(p)
    out_shape = [SDS((length, c), dt) for c, dt in out_rows] + [SDS(s, F32) for s in out_accs]
    out_specs = [pl.BlockSpec((tl, c), lambda i: (rmap(i), 0)) for c, _ in out_rows]
    out_specs += [pl.BlockSpec(s, lambda i, nd=len(s): (0,) * nd) for s in out_accs]
    nr, npar, nor, noa = len(rows), len(params), len(out_rows), len(out_accs)

    def body(*refs):
        rin, pin = refs[:nr], refs[nr:nr + npar]
        rout = refs[nr + npar:nr + npar + nor]
        aout = refs[nr + npar + nor:nr + npar + nor + noa]
        scr = refs[nr + npar + nor + noa:]
        step = pl.program_id(0)
        outs_r, outs_a = fn(step, [r[...] for r in rin], [p[...] for p in pin], scr)
        for ref, val in zip(rout, outs_r, strict=True):
            ref[...] = val.astype(ref.dtype)

        @pl.when(step == 0)
        def _():
            for ref in aout:
                ref[...] = jnp.zeros_like(ref)

        for ref, val in zip(aout, outs_a, strict=True):
            ref[...] += val.astype(F32)

    res = pl.pallas_call(body, grid=(nt,), in_specs=specs, out_specs=out_specs, out_shape=out_shape,
                         scratch_shapes=list(scratch), compiler_params=_params(), name=name)(*arrs)
    return list(res)


def _rms(x, g):
    return x * lax.rsqrt(jnp.mean(x * x, axis=-1, keepdims=True) + NORM_EPS) * g


def _sig(x):
    return 0.5 * (jnp.tanh(0.5 * x) + 1.0)


def _softplus(x):
    return jnp.maximum(x, 0.0) + jnp.log(1.0 + jnp.exp(-jnp.abs(x)))


def _gelu(x):
    return x * (0.5 * (1.0 + jnp.tanh(math.sqrt(2.0 / math.pi) * (x + 0.044715 * (x * x * x)))))


def _bdot(a, b):
    return jnp.dot(a.astype(BF16), b.astype(BF16), preferred_element_type=F32)


def _hdot(a, b):
    return jnp.dot(a, b, precision=HI, preferred_element_type=F32)


def _segsum_impl(x, ones):
    hi = x.astype(BF16)
    lo = (x - hi.astype(F32)).astype(BF16)
    ones = ones.astype(BF16)
    return jnp.dot(jnp.concatenate([hi, lo], axis=1), jnp.concatenate([ones, ones], axis=0), preferred_element_type=F32)


@jax.custom_vjp
def _segsum(x, ones):
    return _segsum_impl(x, ones)


_segsum.defvjp(lambda x, ones: (_segsum_impl(x, ones), ones),
               lambda ones, g: (_segsum_impl(g, ones), jnp.zeros_like(ones)))


def _block_ones(n, blk):
    i = lax.broadcasted_iota(jnp.int32, (n, n), 0) // blk
    j = lax.broadcasted_iota(jnp.int32, (n, n), 1) // blk
    return (i == j).astype(F32)


def _rms_fwd(x, g, name):
    return _rows(lambda s, r, p, _: ([_rms(r[0], p[0])], []), [x], [g], [(x.shape[1], BF16)], [], name=name, tl=512)[0]


def _rms_bwd(x, g, dh, dres, name):
    def fn(s, r, p, _):
        _, vjp = jax.vjp(_rms, r[0], p[0])
        dx, dg = vjp(r[1])
        return [dx + r[2]], [dg]
    return _rows(fn, [x, dh, dres], [g], [(x.shape[1], F32)], [g.shape], name=name, tl=256)


def _rwkv_pre_math(k_, lr, w0, a0, k_k, k_a, w2p, a2p, g2p, o512):
    pre_w = w0 + _bdot(jnp.tanh(lr), w2p)
    w = -_softplus(-pre_w) - 0.5
    decay = jnp.exp(-jnp.exp(w))
    a = _sig(a0 + _bdot(lr, a2p))
    g = _bdot(_sig(lr), g2p)
    kr = k_ * k_k
    kk = kr / jnp.maximum(jnp.sqrt(_segsum(kr * kr, o512)), 1e-12)
    k2 = k_ * (1.0 + (a - 1.0) * k_a)
    return decay, k2, -kk, kk * a, g


def _shift_down(p, prev_row):
    row = lax.broadcasted_iota(jnp.int32, p.shape, 0)
    return jnp.where(row == 0, jnp.broadcast_to(prev_row, p.shape), pltpu.roll(p, 1, 0))


def _shift_up(q, next_row):
    n = q.shape[0]
    row = lax.broadcasted_iota(jnp.int32, q.shape, 0)
    return jnp.where(row == n - 1, jnp.broadcast_to(next_row, q.shape), pltpu.roll(q, n - 1, 0))


def _rwkv_pre_fwd(p, mu, small, o512):
    def fn(step, r, prm, scr):
        car = scr[0]

        @pl.when(step == 0)
        def _():
            car[...] = jnp.zeros_like(car)

        x = r[0]
        prev = _shift_down(x, car[SUBLANES - 1:SUBLANES, :])
        car[...] = x[x.shape[0] - SUBLANES:, :]
        xs = x + (prev - x) * prm[0]
        decay, k2, aa, bb, g = _rwkv_pre_math(xs[:, RW:2 * RW], xs[:, 3 * RW:], *prm[1:])
        return [xs[:, :RW], decay, k2, xs[:, 2 * RW:3 * RW], aa, bb, g], []
    return _rows(fn, [p], [mu, *small, o512], [(RW, F32)] * 7, [], name="rwkv_pre_fwd", tl=256,
                 scratch=[pltpu.VMEM((SUBLANES, N_RWKV), F32)])


def _rwkv_pre_bwd(p, cots, mu, small, o512):
    length = p.shape[0]
    tl = min(256, length)
    nt = length // tl
    rows_per = tl // SUBLANES
    params = [mu, *small, o512]
    acc_shapes = [mu.shape] + [q.shape for q in small]
    nr, npar, nacc = 2 + len(cots), len(params), len(acc_shapes)

    def body(*refs):
        rin, pin = refs[:nr], refs[nr:nr + npar]
        dp_ref = refs[nr + npar]
        aout = refs[nr + npar + 1:nr + npar + 1 + nacc]
        car_q = refs[nr + npar + 1 + nacc]
        step = pl.program_id(0)

        @pl.when(step == 0)
        def _():
            car_q[...] = jnp.zeros_like(car_q)
            for ref in aout:
                ref[...] = jnp.zeros_like(ref)

        x = rin[0][...]
        prev_row = jnp.where(step == nt - 1, 0.0, rin[1][SUBLANES - 1:SUBLANES, :])
        dr, ddecay, dk2, dv, daa, dbb, dg, dr_b, dk2_b, dv_b = [r[...] for r in rin[2:]]
        dr, dk2, dv = dr + dr_b, dk2 + dk2_b, dv + dv_b
        prm = [q[...] for q in pin]
        mu_, o512_ = prm[0], prm[-1]
        prev = _shift_down(x, prev_row)
        xs = x + (prev - x) * mu_
        _, vjp = jax.vjp(lambda k_, lr, *w: _rwkv_pre_math(k_, lr, *w, o512_), xs[:, RW:2 * RW], xs[:, 3 * RW:], *prm[1:-1])
        dk_, dlr, *dsmall = vjp((ddecay, dk2, daa, dbb, dg))
        dxs = jnp.concatenate([dr, dk_, dv, dlr], axis=1)
        q = dxs * mu_
        dp_ref[...] = (dxs - q + _shift_up(q, car_q[0:1, :])).astype(dp_ref.dtype)
        car_q[...] = q[:SUBLANES, :]
        aout[0][...] += jnp.sum((prev - x) * dxs, axis=0, keepdims=True)
        for ref, val in zip(aout[1:], dsmall, strict=True):
            ref[...] += val

    rmap = lambda i: nt - 1 - i
    specs = [pl.BlockSpec((tl, N_RWKV), lambda i: (rmap(i), 0)),
             pl.BlockSpec((SUBLANES, N_RWKV), lambda i: (jnp.maximum(rmap(i) * rows_per - 1, 0), 0))]
    specs += [pl.BlockSpec((tl, RW), lambda i: (rmap(i), 0)) for _ in cots]
    specs += [pl.BlockSpec(q.shape, lambda i, nd=q.ndim: (0,) * nd) for q in params]
    out_shape = [SDS((length, N_RWKV), BF16)] + [SDS(sh, F32) for sh in acc_shapes]
    out_specs = [pl.BlockSpec((tl, N_RWKV), lambda i: (rmap(i), 0))]
    out_specs += [pl.BlockSpec(sh, lambda i, nd=len(sh): (0,) * nd) for sh in acc_shapes]
    res = pl.pallas_call(body, grid=(nt,), in_specs=specs, out_specs=out_specs, out_shape=out_shape,
                         scratch_shapes=[pltpu.VMEM((SUBLANES, N_RWKV), F32)],
                         compiler_params=_params(), name="rwkv_pre_bwd")(p, p, *cots, *params)
    return list(res)


def _rwkv_post_math(y, r, k2, v, g, lnx_w, lnx_b, r_k, o512):
    mean = _segsum(y, o512) * (1.0 / HN)
    yc = y - mean
    var = _segsum(yc * yc, o512) * (1.0 / HN)
    yn = yc * lax.rsqrt(var + LNX_EPS) * lnx_w + lnx_b
    bonus = _segsum(r * k2 * r_k, o512) * v
    return (yn + bonus) * g


def _rwkv_post_fwd(y, r, k2, v, g, prm, o512):
    return _rows(lambda s, rr, p, _: ([_rwkv_post_math(*rr, *p)], []), [y, r, k2, v, g], [*prm, o512],
                 [(RW, BF16)], [], name="rwkv_post_fwd", tl=256)[0]


def _rwkv_post_bwd(y, r, k2, v, g, dout, prm, o512):
    def fn(s, rr, p, _):
        o = p[-1]
        _, vjp = jax.vjp(lambda *a: _rwkv_post_math(*a, o), *rr[:5], *p[:-1])
        gr = vjp(rr[5])
        return list(gr[:5]), list(gr[5:])
    return _rows(fn, [y, r, k2, v, g, dout], [*prm, o512], [(RW, F32)] * 5, [q.shape for q in prm],
                 name="rwkv_post_bwd", tl=256)


def _glu_math(ys, t, b):
    return ys * _sig(t + b)


def _glu_fwd(ys, t, b):
    return _rows(lambda s, r, p, _: ([_glu_math(r[0], r[1], p[0])], []), [ys, t], [b], [(S5W, BF16)], [],
                 name="s5_glu_fwd", tl=512)[0]


def _glu_bwd(ys, t, dout, b):
    def fn(s, r, p, _):
        _, vjp = jax.vjp(_glu_math, r[0], r[1], p[0])
        dys, dt, db = vjp(r[2])
        return [dys, dt], [db]
    return _rows(fn, [ys, t, dout], [b], [(S5W, F32), (S5W, BF16)], [b.shape], name="s5_glu_bwd", tl=512)


def _merge_math(gp_r, gp_s, o_r, o_s, b_r, b_s):
    return _sig(gp_r + b_r) * o_r + _sig(gp_s + b_s) * o_s


def _merge_fwd(gpre, o_r, o_s, b_r, b_s):
    return _rows(lambda s, r, p, _: ([_merge_math(*r, *p)], []),
                 [(gpre, D_MODEL, 0), (gpre, D_MODEL, 1), o_r, o_s], [b_r, b_s], [(D_MODEL, BF16)], [],
                 name="merge_fwd", tl=256)[0]


def _merge_bwd(gpre, o_r, o_s, dmi, b_r, b_s):
    def fn(s, r, p, _):
        _, vjp = jax.vjp(_merge_math, *r[:4], *p)
        dgr, dgs, dor, dos, dbr, dbs = vjp(r[4])
        return [dgr, dgs, dor, dos], [dbr, dbs]
    return _rows(fn, [(gpre, D_MODEL, 0), (gpre, D_MODEL, 1), o_r, o_s, dmi], [b_r, b_s], [(D_MODEL, BF16)] * 4,
                 [b_r.shape, b_s.shape], name="merge_bwd", tl=256)


def _mid_fwd(x, mixed, g_post, g_pre):
    def fn(s, r, p, _):
        x2 = r[0] + _rms(r[1], p[0])
        return [x2, _rms(x2, p[1])], []
    return _rows(fn, [x, mixed], [g_post, g_pre], [(D_MODEL, F32), (D_MODEL, BF16)], [], name="mid_fwd", tl=256)


def _mid_bwd(x2, mixed, dh2, dx3, g_post, g_pre):
    def fn(s, r, p, _):
        _, vjp1 = jax.vjp(_rms, r[0], p[1])
        dx2, dg_pre = vjp1(r[2])
        dx2 = dx2 + r[3]
        _, vjp2 = jax.vjp(_rms, r[1], p[0])
        dmixed, dg_post = vjp2(dx2)
        return [dx2, dmixed], [dg_post, dg_pre]
    return _rows(fn, [x2, mixed, dh2, dx3], [g_post, g_pre], [(D_MODEL, F32), (D_MODEL, BF16)], [g_post.shape, g_pre.shape],
                 name="mid_bwd", tl=256)


def _final(x2, f, tgt, g_post):
    def fn(s, r, p, _):
        y, vjp = jax.vjp(_rms, r[1], p[0])
        diff = r[0] + y - r[2]
        dx3 = diff * (1.0 / D_MODEL)
        df, dg = vjp(dx3)
        part = 0.5 * jnp.sum(jnp.sum(diff * diff, axis=1, keepdims=True), axis=0, keepdims=True) * (1.0 / D_MODEL)
        return [dx3, df], [dg, jnp.broadcast_to(part, (1, LANES))]
    return _rows(fn, [x2, f, tgt], [g_post], [(D_MODEL, F32), (D_MODEL, BF16)], [g_post.shape, (1, LANES)], name="final", tl=256)


def _conv_taps(z, car):
    row = lax.broadcasted_iota(jnp.int32, z.shape, 0)
    z1 = jnp.where(row == 0, jnp.broadcast_to(car[7:8, :], z.shape), pltpu.roll(z, 1, 0))
    z2 = pltpu.roll(z, 2, 0)
    z2 = jnp.where(row == 0, jnp.broadcast_to(car[6:7, :], z.shape), z2)
    z2 = jnp.where(row == 1, jnp.broadcast_to(car[7:8, :], z.shape), z2)
    return z1, z2


def _conv(z, car, w, b):
    z1, z2 = _conv_taps(z, car)
    return b + w[0:1, :] * z2 + w[1:2, :] * z1 + w[2:3, :] * z, z1, z2


def _conv_fwd(z, conv_w, conv_b):
    length = z.shape[0]
    tl = min(256, length)
    nt = length // tl
    tc = _pick(D_FF, 1536)
    nb = D_FF // tc

    def body(zg_ref, zv_ref, wg_ref, wv_ref, bg_ref, bv_ref, o_ref, cg, cv):
        @pl.when(pl.program_id(1) == 0)
        def _():
            cg[...] = jnp.zeros_like(cg)
            cv[...] = jnp.zeros_like(cv)

        zg, zv = zg_ref[...], zv_ref[...]
        gate, _, _ = _conv(zg, cg[...], wg_ref[...], bg_ref[...])
        val, _, _ = _conv(zv, cv[...], wv_ref[...], bv_ref[...])
        cg[...] = zg[tl - SUBLANES:, :]
        cv[...] = zv[tl - SUBLANES:, :]
        o_ref[...] = (_gelu(gate) * val).astype(o_ref.dtype)

    zspec = lambda off: pl.BlockSpec((tl, tc), lambda j, i: (i, j + off))
    wspec = lambda off, r: pl.BlockSpec((r, tc), lambda j, i: (0, j + off))
    return pl.pallas_call(
        body, grid=(nb, nt),
        in_specs=[zspec(0), zspec(nb), wspec(0, 3), wspec(nb, 3), wspec(0, 1), wspec(nb, 1)],
        out_specs=pl.BlockSpec((tl, tc), lambda j, i: (i, j)), out_shape=SDS((length, D_FF), BF16),
        scratch_shapes=[pltpu.VMEM((SUBLANES, tc), F32)] * 2,
        compiler_params=_params(("arbitrary", "arbitrary")), name="conv_fwd")(z, z, conv_w, conv_w, conv_b, conv_b)


def _conv_bwd(z, dact, conv_w, conv_b):
    length = z.shape[0]
    tl = min(256, length)
    nt = length // tl
    tc = _pick(D_FF, 1536)
    nb = D_FF // tc
    rows_per = tl // SUBLANES

    def half_bwd(dzc, z, z1, z2, w, dcar):
        n = tl
        row = lax.broadcasted_iota(jnp.int32, dzc.shape, 0)
        u1 = jnp.where(row == n - 1, jnp.broadcast_to(dcar[0:1, :], dzc.shape), pltpu.roll(dzc, n - 1, 0))
        u2 = pltpu.roll(dzc, n - 2, 0)
        u2 = jnp.where(row == n - 2, jnp.broadcast_to(dcar[0:1, :], dzc.shape), u2)
        u2 = jnp.where(row == n - 1, jnp.broadcast_to(dcar[1:2, :], dzc.shape), u2)
        dz = w[2:3, :] * dzc + w[1:2, :] * u1 + w[0:1, :] * u2
        dw = jnp.concatenate([jnp.sum(dzc * z2, axis=0, keepdims=True), jnp.sum(dzc * z1, axis=0, keepdims=True),
                              jnp.sum(dzc * z, axis=0, keepdims=True)], axis=0)
        return dz, dw, jnp.sum(dzc, axis=0, keepdims=True)

    def body(zg_ref, zv_ref, pg_ref, pv_ref, da_ref, wg_ref, wv_ref, bg_ref, bv_ref,
             dzg_ref, dzv_ref, dwg_ref, dwv_ref, dbg_ref, dbv_ref, cg, cv):
        step = pl.program_id(1)

        @pl.when(step == 0)
        def _():
            cg[...] = jnp.zeros_like(cg)
            cv[...] = jnp.zeros_like(cv)
            for ref in (dwg_ref, dwv_ref, dbg_ref, dbv_ref):
                ref[...] = jnp.zeros_like(ref)

        is_first_tile = step == nt - 1
        zg, zv = zg_ref[...], zv_ref[...]
        pg = jnp.where(is_first_tile, 0.0, pg_ref[...])
        pv = jnp.where(is_first_tile, 0.0, pv_ref[...])
        wg, wv = wg_ref[...], wv_ref[...]
        gate, zg1, zg2 = _conv(zg, pg, wg, bg_ref[...])
        val, zv1, zv2 = _conv(zv, pv, wv, bv_ref[...])
        act_g, vjp = jax.vjp(_gelu, gate)
        da = da_ref[...]
        dgate = vjp(da * val)[0]
        dval = da * act_g
        dzg, dwg, dbg = half_bwd(dgate, zg, zg1, zg2, wg, cg[...])
        dzv, dwv, dbv = half_bwd(dval, zv, zv1, zv2, wv, cv[...])
        cg[...] = dgate[:SUBLANES, :]
        cv[...] = dval[:SUBLANES, :]
        dzg_ref[...] = dzg.astype(dzg_ref.dtype)
        dzv_ref[...] = dzv.astype(dzv_ref.dtype)
        dwg_ref[...] += dwg
        dwv_ref[...] += dwv
        dbg_ref[...] += dbg
        dbv_ref[...] += dbv

    rmap = lambda i: nt - 1 - i
    zspec = lambda off: pl.BlockSpec((tl, tc), lambda j, i: (rmap(i), j + off))
    pspec = lambda off: pl.BlockSpec((SUBLANES, tc), lambda j, i: (jnp.maximum(rmap(i) * rows_per - 1, 0), j + off))
    wspec = lambda off, r: pl.BlockSpec((r, tc), lambda j, i: (0, j + off))
    out_w = lambda r: pl.BlockSpec((r, tc), lambda j, i: (0, j))
    dzg, dzv, dwg, dwv, dbg, dbv = pl.pallas_call(
        body, grid=(nb, nt),
        in_specs=[zspec(0), zspec(nb), pspec(0), pspec(nb), pl.BlockSpec((tl, tc), lambda j, i: (rmap(i), j)),
                  wspec(0, 3), wspec(nb, 3), wspec(0, 1), wspec(nb, 1)],
        out_specs=[pl.BlockSpec((tl, tc), lambda j, i: (rmap(i), j))] * 2 + [out_w(3), out_w(3), out_w(1), out_w(1)],
        out_shape=[SDS((length, D_FF), BF16)] * 2 + [SDS((3, D_FF), F32)] * 2 + [SDS((1, D_FF), F32)] * 2,
        scratch_shapes=[pltpu.VMEM((SUBLANES, tc), F32)] * 2,
        compiler_params=_params(("arbitrary", "arbitrary")), name="conv_bwd")(z, z, z, z, dact, conv_w, conv_w, conv_b, conv_b)
    return dzg, dzv, jnp.concatenate([dwg, dwv], axis=1), jnp.concatenate([dbg, dbv], axis=1)


def _s5_prep_math(a_re, a_im, ls, b_re, b_im):
    dt = jnp.exp(ls)
    er = jnp.exp(a_re * dt)
    ph = a_im * dt
    abr, abi = er * jnp.cos(ph), er * jnp.sin(ph)
    den = a_re * a_re + a_im * a_im
    nr = abr - 1.0
    cr = (nr * a_re + abi * a_im) / den
    ci = (abi * a_re - nr * a_im) / den
    return abr, abi, cr * b_re - ci * b_im, cr * b_im + ci * b_re


def _s5_prep_fwd(a_re, a_im, ls, b_re, b_im):
    def body(ar, ai, l, br, bi, o1, o2, o3, o4):
        for ref, val in zip((o1, o2, o3, o4), _s5_prep_math(ar[...], ai[...], l[...], br[...], bi[...]), strict=True):
            ref[...] = val
    return pl.pallas_call(body, out_shape=[SDS((1, S5N), F32)] * 2 + [SDS((S5C, S5N), F32)] * 2,
                          name="s5_prep_fwd")(a_re, a_im, ls, b_re, b_im)


def _s5_prep_bwd(a_re, a_im, ls, b_re, b_im, cots, gsel):
    def body(ar, ai, l, br, bi, c1, c2, c3, c4, g_ref, o1, o2, o3, o4, o5):
        _, vjp = jax.vjp(_s5_prep_math, ar[...], ai[...], l[...], br[...], bi[...])
        d_ar, d_ai, d_ls, d_br, d_bi = vjp((c1[...], c2[...], c3[...], c4[...]))
        o1[...] = d_ar
        o2[...] = d_ai
        o3[...] = _hdot(jnp.broadcast_to(d_ls, (SUBLANES, S5N)), g_ref[...])
        o4[...] = d_br
        o5[...] = d_bi
    return pl.pallas_call(body, out_shape=[SDS((1, S5N), F32)] * 2 + [SDS((SUBLANES, LANES), F32)] + [SDS((S5C, S5N), F32)] * 2,
                          name="s5_prep_bwd")(a_re, a_im, ls, b_re, b_im, *cots, gsel)


def _cmul(ar, ai, br, bi):
    return ar * br - ai * bi, ar * bi + ai * br


def _s5_powers(abr, abi):
    shp = (SUBLANES, S5N)
    a1 = (jnp.broadcast_to(abr, shp), jnp.broadcast_to(abi, shp))
    a2 = _cmul(*a1, *a1)
    a4 = _cmul(*a2, *a2)
    row = lax.broadcasted_iota(jnp.int32, shp, 0)
    pr, pi = a1
    cur = a1
    for i in range(1, SUBLANES):
        cur = _cmul(*cur, *a1)
        pr = jnp.where(row == i, cur[0], pr)
        pi = jnp.where(row == i, cur[1], pi)
    return a1, a2, a4, (pr, pi)


def _s5_scan(sre_ref, sim_ref, car_re, car_im, abr, abi, n_rows, reverse):
    a1, a2, a4, (pr, pi) = _s5_powers(abr, abi)
    sgn = -1.0 if reverse else 1.0
    row = lax.broadcasted_iota(jnp.int32, (SUBLANES, S5N), 0)
    if reverse:
        qr, qi = pr, pi
        for i in range(SUBLANES):
            src = SUBLANES - 1 - i
            qr = jnp.where(row == i, jnp.broadcast_to(pr[src:src + 1, :], pr.shape), qr)
            qi = jnp.where(row == i, jnp.broadcast_to(pi[src:src + 1, :], pi.shape), qi)
        pr, pi = qr, qi
    nblk = n_rows // SUBLANES

    def blk(i, carry):
        cr, ci = carry
        b = (nblk - 1 - i) if reverse else i
        sl = pl.ds(pl.multiple_of(b * SUBLANES, SUBLANES), SUBLANES)
        xr, xi = sre_ref[sl, :], sim_ref[sl, :]
        for kk, (er, ei) in ((1, a1), (2, a2), (4, a4)):
            if reverse:
                sr = jnp.where(row < SUBLANES - kk, pltpu.roll(xr, SUBLANES - kk, 0), 0.0)
                si = jnp.where(row < SUBLANES - kk, pltpu.roll(xi, SUBLANES - kk, 0), 0.0)
            else:
                sr = jnp.where(row >= kk, pltpu.roll(xr, kk, 0), 0.0)
                si = jnp.where(row >= kk, pltpu.roll(xi, kk, 0), 0.0)
            dr, di = _cmul(er, sgn * ei, sr, si)
            xr, xi = xr + dr, xi + di
        dr, di = _cmul(pr, sgn * pi, cr, ci)
        xr, xi = xr + dr, xi + di
        sre_ref[sl, :] = xr
        sim_ref[sl, :] = xi
        edge = 0 if reverse else SUBLANES - 1
        return (jnp.broadcast_to(xr[edge:edge + 1, :], xr.shape), jnp.broadcast_to(xi[edge:edge + 1, :], xi.shape))

    cr, ci = lax.fori_loop(0, nblk, blk, (car_re[...], car_im[...]))
    car_re[...] = cr
    car_im[...] = ci


S5_BLK = 4
S5_BN = S5N // S5_BLK


def _split2(x):
    hi = x.astype(BF16)
    return hi, (x - hi.astype(F32)).astype(BF16)


def _dot3(a, b, ca, cb):
    ah, al = _split2(a)
    bh, bl = _split2(b)
    return lax.dot_general(jnp.concatenate([ah, ah, al], axis=ca), jnp.concatenate([bh, bl, bh], axis=cb),
                           (((ca,), (cb,)), ((), ())), preferred_element_type=F32)


def _dot1(a, b, ca, cb):
    return lax.dot_general(a.astype(BF16), b.astype(BF16), (((ca,), (cb,)), ((), ())), preferred_element_type=F32)


def _s5_project_in(u, bblk_ref, sre_ref, sim_ref):
    for j in range(S5_BLK):
        bu = _dot1(u[:, j * LANES:(j + 1) * LANES], bblk_ref[j], 1, 0)
        sre_ref[:, j * S5_BN:(j + 1) * S5_BN] = bu[:, :S5_BN]
        sim_ref[:, j * S5_BN:(j + 1) * S5_BN] = bu[:, S5_BN:]


def _s5_project_out(u, d, cblk_ref, sre_ref, sim_ref):
    ys = []
    for j in range(S5_BLK):
        sl = slice(j * S5_BN, (j + 1) * S5_BN)
        ys.append(_dot1(sre_ref[:, sl], cblk_ref[j, :S5_BN, :], 1, 0) + _dot1(sim_ref[:, sl], cblk_ref[j, S5_BN:, :], 1, 0))
    return jnp.concatenate(ys, axis=1) + d * u


def _s5_fwd(u, bblk, cblk, abr, abi, d):
    length = u.shape[0]
    tl = min(256, length)
    nt = length // tl

    def body(u_ref, b_ref, c_ref, ar_ref, ai_ref, d_ref, ys_ref, kr_ref, ki_ref, sre, sim, car_re, car_im):
        @pl.when(pl.program_id(0) == 0)
        def _():
            car_re[...] = jnp.zeros_like(car_re)
            car_im[...] = jnp.zeros_like(car_im)

        kr_ref[0] = car_re[...]
        ki_ref[0] = car_im[...]
        u_ = u_ref[...]
        _s5_project_in(u_, b_ref, sre, sim)
        _s5_scan(sre, sim, car_re, car_im, ar_ref[...], ai_ref[...], tl, False)
        ys_ref[...] = _gelu(_s5_project_out(u_, d_ref[...], c_ref, sre, sim))

    full = lambda a: pl.BlockSpec(a.shape, lambda i, nd=a.ndim: (0,) * nd)
    chk = pl.BlockSpec((1, SUBLANES, S5N), lambda i: (i, 0, 0))
    return pl.pallas_call(
        body, grid=(nt,), in_specs=[pl.BlockSpec((tl, S5W), lambda i: (i, 0)), full(bblk), full(cblk), full(abr), full(abi), full(d)],
        out_specs=[pl.BlockSpec((tl, S5W), lambda i: (i, 0)), chk, chk],
        out_shape=[SDS((length, S5W), F32), SDS((nt, SUBLANES, S5N), F32), SDS((nt, SUBLANES, S5N), F32)],
        scratch_shapes=[pltpu.VMEM((tl, S5N), F32)] * 2 + [pltpu.VMEM((SUBLANES, S5N), F32)] * 2,
        compiler_params=_params(), name="s5_fwd")(u, bblk, cblk, abr, abi, d)


def _s5_bwd(u, dys_a, dys_b, kre, kim, bblk, cblk, abr, abi, d):
    length = u.shape[0]
    tl = min(256, length)
    nt = length // tl

    def body(u_ref, da_ref, db_ref, kr_ref, ki_ref, b_ref, c_ref, ar_ref, ai_ref, d_ref,
             du_ref, dB_ref, dC_ref, dar_ref, dai_ref, dd_ref, sre, sim, gre, gim, car_re, car_im, dcar_re, dcar_im):
        @pl.when(pl.program_id(0) == 0)
        def _():
            dcar_re[...] = jnp.zeros_like(dcar_re)
            dcar_im[...] = jnp.zeros_like(dcar_im)
            for ref in (dB_ref, dC_ref, dar_ref, dai_ref, dd_ref):
                ref[...] = jnp.zeros_like(ref)

        u_ = u_ref[...]
        abr_, abi_, d_ = ar_ref[...], ai_ref[...], d_ref[...]
        car_re[...] = kr_ref[0]
        car_im[...] = ki_ref[0]
        _s5_project_in(u_, b_ref, sre, sim)
        _s5_scan(sre, sim, car_re, car_im, abr_, abi_, tl, False)
        y = _s5_project_out(u_, d_, c_ref, sre, sim)
        _, vjp = jax.vjp(_gelu, y)
        dy = vjp(da_ref[...] + db_ref[...])[0]
        dd_ref[...] += jnp.sum(dy * u_, axis=0, keepdims=True)
        for j in range(S5_BLK):
            sl = slice(j * S5_BN, (j + 1) * S5_BN)
            dyj = dy[:, j * LANES:(j + 1) * LANES]
            gre[:, sl] = _dot1(dyj, c_ref[j, :S5_BN, :], 1, 1)
            gim[:, sl] = _dot1(dyj, c_ref[j, S5_BN:, :], 1, 1)
            dC_ref[j, :S5_BN, :] += _dot1(sre[:, sl], dyj, 0, 0)
            dC_ref[j, S5_BN:, :] += _dot1(sim[:, sl], dyj, 0, 0)
        _s5_scan(gre, gim, dcar_re, dcar_im, abr_, abi_, tl, True)
        gr, gi = gre[...], gim[...]
        pr = _shift_down(sre[...], kr_ref[0, 0:1, :])
        pi = _shift_down(sim[...], ki_ref[0, 0:1, :])
        dar_ref[...] += jnp.sum(gr * pr + gi * pi, axis=0, keepdims=True)
        dai_ref[...] += jnp.sum(gi * pr - gr * pi, axis=0, keepdims=True)
        dus = []
        for j in range(S5_BLK):
            sl = slice(j * S5_BN, (j + 1) * S5_BN)
            uj = u_[:, j * LANES:(j + 1) * LANES]
            dus.append(_dot1(gre[:, sl], b_ref[j, :, :S5_BN], 1, 1) + _dot1(gim[:, sl], b_ref[j, :, S5_BN:], 1, 1))
            dB_ref[j, :, :S5_BN] += _dot1(uj, gre[:, sl], 0, 0)
            dB_ref[j, :, S5_BN:] += _dot1(uj, gim[:, sl], 0, 0)
        du_ref[...] = (jnp.concatenate(dus, axis=1) + d_ * dy).astype(du_ref.dtype)

    rmap = lambda i: nt - 1 - i
    full = lambda a: pl.BlockSpec(a.shape, lambda i, nd=a.ndim: (0,) * nd)
    row = pl.BlockSpec((tl, S5W), lambda i: (rmap(i), 0))
    chk = pl.BlockSpec((1, SUBLANES, S5N), lambda i: (rmap(i), 0, 0))
    return pl.pallas_call(
        body, grid=(nt,), in_specs=[row, row, row, chk, chk, full(bblk), full(cblk), full(abr), full(abi), full(d)],
        out_specs=[row, full(bblk), full(cblk), full(abr), full(abi), full(d)],
        out_shape=[SDS((length, S5W), BF16), SDS(bblk.shape, F32), SDS(cblk.shape, F32), SDS(abr.shape, F32),
                   SDS(abi.shape, F32), SDS(d.shape, F32)],
        scratch_shapes=[pltpu.VMEM((tl, S5N), F32)] * 4 + [pltpu.VMEM((SUBLANES, S5N), F32)] * 4,
        compiler_params=_params(), name="s5_bwd")(u, dys_a, dys_b, kre, kim, bblk, cblk, abr, abi, d)


def _s5_blockdiag_in(bb):
    t = bb.reshape(S5C, S5_BLK, S5_BN).transpose(1, 0, 2)
    t = jnp.tile(t, (1, SUBLANES, 1))
    mask = (lax.broadcasted_iota(jnp.int32, (LANES, S5_BN), 0) // S5C) == (lax.broadcasted_iota(jnp.int32, (LANES, S5_BN), 1) // S5P)
    return jnp.where(mask[None], t, 0.0)


def _s5_blockdiag_in_t(dblk):
    t = dblk.reshape(S5_BLK, SUBLANES, S5C, SUBLANES, S5P)
    t = jnp.diagonal(t, axis1=1, axis2=3)
    return t.transpose(1, 0, 3, 2).reshape(S5C, S5N)


def _s5_blockdiag_out(c):
    t = c.reshape(S5_BLK, SUBLANES, S5C, S5P).transpose(0, 3, 1, 2).reshape(S5_BLK, S5P, LANES)
    t = jnp.tile(t, (1, SUBLANES, 1))
    mask = (lax.broadcasted_iota(jnp.int32, (S5_BN, LANES), 0) // S5P) == (lax.broadcasted_iota(jnp.int32, (S5_BN, LANES), 1) // S5C)
    return jnp.where(mask[None], t, 0.0)


def _s5_blockdiag_out_t(dblk):
    t = dblk.reshape(S5_BLK, SUBLANES, S5P, SUBLANES, S5C)
    t = jnp.diagonal(t, axis1=1, axis2=3)
    return t.transpose(0, 3, 2, 1).reshape(S5G, S5C, S5P)


def _wkv_consts():
    lane = lax.broadcasted_iota(jnp.int32, (4 * HN, LANES), 1)
    row = lax.broadcasted_iota(jnp.int32, (4 * HN, LANES), 0)
    diag = ((lane % HN) == (row % HN)).astype(F32)
    ones = _block_ones(LANES, HN).astype(BF16)
    return lane % HN, diag, jnp.concatenate([ones, ones], axis=0)


def _wkv_split(p):
    hi = p.astype(BF16)
    lo = (p - hi.astype(F32)).astype(BF16)
    return jnp.concatenate([hi, lo], axis=1)


def _wkv_rowseg(p, ones):
    return jnp.dot(_wkv_split(p), ones, preferred_element_type=F32)


def _wkv_full(rows_ref, t):
    q = jnp.broadcast_to(rows_ref[t], (SUBLANES, RW))
    return jnp.concatenate([jnp.tile(q[:, hp * LANES:(hp + 1) * LANES], (SUBLANES, 1)) for hp in range(4)], axis=0)


def _wkv_colsum8(x):
    return jnp.concatenate([x[hp * HN:(hp + 1) * HN].reshape(SUBLANES, SUBLANES, LANES).sum(axis=0) for hp in range(4)], axis=1)


def _wkv_prebroadcast(src_refs, dst_refs, n):
    for s, dref in zip(src_refs, dst_refs, strict=True):
        dref[...] = s[...].reshape(n, 1, RW)


def _wkv_pair_sums(lhs_ref, out_ref):
    n, ns = out_ref.shape[0], out_ref.shape[1]
    ones_pair = _block_ones(2 * LANES, HN).astype(BF16)
    out_ref[...] = jnp.dot(lhs_ref[0:n].reshape(n * ns, 2 * LANES), ones_pair,
                           preferred_element_type=F32).reshape(n, ns, 2 * LANES)


def _wkv_transpose_out(acc, c, diag):
    lane = lax.broadcasted_iota(jnp.int32, (c, LANES), 1)
    outs = []
    for hp in range(4):
        zp = lax.dot_general(acc[hp * HN:(hp + 1) * HN], diag[:HN], (((0,), (0,)), ((), ())), precision=HI,
                             preferred_element_type=F32)
        outs.append(jnp.where(lane < HN, zp[:c], zp[HN:HN + c]))
    return jnp.concatenate(outs, axis=1)


def _wkv_fwd(r, w, k, v, a, b):
    length = r.shape[0]
    c = min(WKV_CHUNK, length)
    nc = length // c
    ns = 4 * HN

    def body(r_ref, w_ref, k_ref, v_ref, a_ref, b_ref, y_ref, sst_ref, sast_ref, vst_ref, fin_ref,
             s_ref, acc_ref, lhs_ref, ysum_ref, rb, wb, kb, vb, ab, bb, wab, beb, kab):
        @pl.when(pl.program_id(0) == 0)
        def _():
            s_ref[...] = jnp.zeros_like(s_ref)

        a_next = pltpu.roll(a_ref[...], c - 1, 0)
        o512 = _block_ones(RW, HN)
        _wkv_prebroadcast((r_ref, w_ref, k_ref, v_ref, a_ref, b_ref, w_ref[...] * a_next,
                           _segsum_impl(b_ref[...] * a_next, o512), _segsum_impl(k_ref[...] * a_next, o512)),
                          (rb, wb, kb, vb, ab, bb, wab, beb, kab), c)
        lane_t, diag, ones = _wkv_consts()

        def fill_v(q, carry):
            for half in range(2):
                lhs_ref[q, :, half * LANES:(half + 1) * LANES] = (_wkv_full(vb, 2 * q + half) * diag).astype(BF16)
            return carry

        lax.fori_loop(0, c // 2, fill_v, 0)
        _wkv_pair_sums(lhs_ref, vst_ref)

        def pair(p, carry):
            t0, t1 = 2 * p, 2 * p + 1
            s0 = s_ref[...]
            both = _wkv_rowseg(jnp.concatenate([s0 * _wkv_full(ab, t0), s0 * _wkv_full(wab, t0)], axis=0), ones)
            v_pair = vst_ref[p]
            sa0, v0 = both[:ns], v_pair[:, :LANES]
            s1 = s0 * _wkv_full(wb, t0) + sa0 * _wkv_full(bb, t0) + v0 * _wkv_full(kb, t0)
            sa1 = both[ns:] + sa0 * _wkv_full(beb, t0) + v0 * _wkv_full(kab, t0)
            s2 = s1 * _wkv_full(wb, t1) + sa1 * _wkv_full(bb, t1) + v_pair[:, LANES:] * _wkv_full(kb, t1)
            sst_ref[t0] = s0
            sast_ref[t0] = sa0
            sst_ref[t1] = s1
            sast_ref[t1] = sa1
            s_ref[...] = s2
            lhs_ref[p, :, 0:LANES] = (s1 * _wkv_full(rb, t0)).astype(BF16)
            lhs_ref[p, :, LANES:2 * LANES] = (s2 * _wkv_full(rb, t1)).astype(BF16)
            return carry

        lax.fori_loop(0, c // 2, pair, 0)
        _wkv_pair_sums(lhs_ref, ysum_ref)
        acc_ref[...] = jnp.zeros_like(acc_ref)

        def gather_y(p, carry):
            both = ysum_ref[p]
            acc = jnp.where(lane_t == 2 * p, both[:, :LANES], acc_ref[...])
            acc_ref[...] = jnp.where(lane_t == 2 * p + 1, both[:, LANES:], acc)
            return carry

        lax.fori_loop(0, c // 2, gather_y, 0)
        y_ref[...] = _wkv_transpose_out(acc_ref[...], c, diag)
        fin_ref[...] = s_ref[...]

    row = pl.BlockSpec((c, RW), lambda i: (i, 0))
    st = pl.BlockSpec((c, ns, LANES), lambda i: (i, 0, 0))
    state = pltpu.VMEM((ns, LANES), F32)
    st_pair = pl.BlockSpec((c // 2, ns, 2 * LANES), lambda i: (i, 0, 0))
    return pl.pallas_call(
        body, grid=(nc,), in_specs=[row] * 6,
        out_specs=[row, st, st, st_pair, pl.BlockSpec((ns, LANES), lambda i: (0, 0))],
        out_shape=[SDS((length, RW), F32)] + [SDS((length, ns, LANES), F32)] * 2 + [SDS((length // 2, ns, 2 * LANES), F32)]
        + [SDS((ns, LANES), F32)],
        scratch_shapes=[state] * 2 + [pltpu.VMEM((c, ns, 2 * LANES), BF16), pltpu.VMEM((c // 2, ns, 2 * LANES), F32)]
        + [pltpu.VMEM((c, 1, RW), F32)] * 9,
        compiler_params=_params(), name="wkv_fwd")(r, w, k, v, a, b)


def _wkv_bwd(r, w, k, a, b, dy, sst, sast, vst, fin):
    length = r.shape[0]
    c = min(WKV_CHUNK, length)
    nc = length // c
    ns = 4 * HN

    def body(r_ref, w_ref, k_ref, a_ref, b_ref, dy_ref, sst_ref, sast_ref, vst_ref, snext_ref, fin_ref,
             dr_ref, dw_ref, dk_ref, da_ref, db_ref, dv_ref,
             ds_ref, cur_ref, acc_ref, dyst, lhs_ref, dvsum_ref, rb, wb, kb, ab, bb, dyb, wbb, alb, rhb,
             p_r, p_w, p_k, p_a, p_b):
        @pl.when(pl.program_id(0) == 0)
        def _():
            ds_ref[...] = jnp.zeros_like(ds_ref)

        b_prev = pltpu.roll(b_ref[...], 1, 0)
        o512 = _block_ones(RW, HN)
        _wkv_prebroadcast((r_ref, w_ref, k_ref, a_ref, b_ref, dy_ref, w_ref[...] * b_prev,
                           _segsum_impl(a_ref[...] * b_prev, o512), _segsum_impl(r_ref[...] * b_ref[...], o512)),
                          (rb, wb, kb, ab, bb, dyb, wbb, alb, rhb), c)
        lane_t, diag, ones = _wkv_consts()

        def fill(q, carry):
            for half in range(2):
                lhs_ref[q, :, half * LANES:(half + 1) * LANES] = (_wkv_full(dyb, 2 * q + half) * diag).astype(BF16)
            return carry

        lax.fori_loop(0, c // 2, fill, 0)
        _wkv_pair_sums(lhs_ref, dyst)
        cur_ref[...] = jnp.where(pl.program_id(0) == 0, fin_ref[...], snext_ref[0])

        def pair(p, carry):
            ta = c - 1 - 2 * p
            tb = ta - 1
            s_a, s_b, s_c = cur_ref[...], sst_ref[ta], sst_ref[tb]
            dy_pair = dyst[c // 2 - 1 - p]
            dy_a, dy_b = dy_pair[:, LANES:], dy_pair[:, :LANES]
            ds_a = ds_ref[...] + dy_a * _wkv_full(rb, ta)
            both = _wkv_rowseg(jnp.concatenate([ds_a * _wkv_full(bb, ta), ds_a * _wkv_full(wbb, ta)], axis=0), ones)
            dsa_a = both[:ns]
            ds_b = ds_a * _wkv_full(wb, ta) + dsa_a * _wkv_full(ab, ta) + dy_b * _wkv_full(rb, tb)
            dsa_b = both[ns:] + dsa_a * _wkv_full(alb, ta) + dy_b * _wkv_full(rhb, tb)
            ds_ref[...] = ds_b * _wkv_full(wb, tb) + dsa_b * _wkv_full(ab, tb)
            cur_ref[...] = s_c
            lhs_ref[p, :, 0:LANES] = (ds_a * _wkv_full(kb, ta)).astype(BF16)
            lhs_ref[p, :, LANES:2 * LANES] = (ds_b * _wkv_full(kb, tb)).astype(BF16)
            v_pair = vst_ref[c // 2 - 1 - p]
            for t, s_t, s_prev, ds, dsa, dyy, vv in ((ta, s_a, s_b, ds_a, dsa_a, dy_a, v_pair[:, LANES:]),
                                                     (tb, s_b, s_c, ds_b, dsa_b, dy_b, v_pair[:, :LANES])):
                p_r[t] = _wkv_colsum8(s_t * dyy)
                p_k[t] = _wkv_colsum8(ds * vv)
                p_b[t] = _wkv_colsum8(ds * sast_ref[t])
                p_w[t] = _wkv_colsum8(ds * s_prev)
                p_a[t] = _wkv_colsum8(s_prev * dsa)
            return carry

        lax.fori_loop(0, c // 2, pair, 0)
        _wkv_pair_sums(lhs_ref, dvsum_ref)
        acc_ref[...] = jnp.zeros_like(acc_ref)

        def gather_dv(p, carry):
            both = dvsum_ref[p]
            acc = jnp.where(lane_t == c - 1 - 2 * p, both[:, :LANES], acc_ref[...])
            acc_ref[...] = jnp.where(lane_t == c - 2 - 2 * p, both[:, LANES:], acc)
            return carry

        lax.fori_loop(0, c // 2, gather_dv, 0)
        sel = (lax.broadcasted_iota(jnp.int32, (c, c * SUBLANES), 1) // SUBLANES
               == lax.broadcasted_iota(jnp.int32, (c, c * SUBLANES), 0)).astype(F32)
        for out_ref, part in ((dr_ref, p_r), (dw_ref, p_w), (dk_ref, p_k), (da_ref, p_a), (db_ref, p_b)):
            out_ref[...] = _dot3(sel, part[...].reshape(c * SUBLANES, RW), 1, 0)
        dv_ref[...] = _wkv_transpose_out(acc_ref[...], c, diag)

    rmap = lambda i: nc - 1 - i
    row = pl.BlockSpec((c, RW), lambda i: (rmap(i), 0))
    st = pl.BlockSpec((c, ns, LANES), lambda i: (rmap(i), 0, 0))
    nxt = pl.BlockSpec((1, ns, LANES), lambda i: (jnp.minimum((rmap(i) + 1) * c, length - 1), 0, 0))
    state = pltpu.VMEM((ns, LANES), F32)
    st_pair = pl.BlockSpec((c // 2, ns, 2 * LANES), lambda i: (rmap(i), 0, 0))
    return pl.pallas_call(
        body, grid=(nc,), in_specs=[row] * 6 + [st, st, st_pair, nxt, pl.BlockSpec((ns, LANES), lambda i: (0, 0))],
        out_specs=[row] * 6, out_shape=[SDS((length, RW), F32)] * 6,
        scratch_shapes=[state] * 3 + [pltpu.VMEM((c // 2, ns, 2 * LANES), F32), pltpu.VMEM((c, ns, 2 * LANES), BF16),
                                       pltpu.VMEM((c // 2, ns, 2 * LANES), F32)]
        + [pltpu.VMEM((c, 1, RW), F32)] * 9 + [pltpu.VMEM((c, SUBLANES, RW), F32)] * 5,
        compiler_params=_params(), name="wkv_bwd")(r, w, k, a, b, dy, sst, sast, vst, sst, fin)


def _my_place():
    return lax.axis_index("x"), lax.axis_index("y"), lax.axis_index("c")


def _peer(x, y, c, k):
    return (x ^ ((k >> 2) & 1), y ^ ((k >> 1) & 1), c ^ (k & 1))


def _all_gather(shard, name):
    def body(in_ref, out_ref, send_sems, recv_sems, local_sem):
        x, y, c = _my_place()
        me = 4 * x + 2 * y + c
        mine = pltpu.make_async_copy(in_ref, out_ref.at[me], local_sem)
        mine.start()
        sends = []
        for k in range(1, N_DEV):
            cp = pltpu.make_async_remote_copy(src_ref=in_ref, dst_ref=out_ref.at[me], send_sem=send_sems.at[k - 1],
                                              recv_sem=recv_sems.at[k - 1], device_id=_peer(x, y, c, k), device_id_type=MESH)
            cp.start()
            sends.append(cp)
        for k in range(1, N_DEV):
            px, py, pc = _peer(x, y, c, k)
            pltpu.make_async_remote_copy(src_ref=in_ref, dst_ref=out_ref.at[4 * px + 2 * py + pc], send_sem=send_sems.at[k - 1],
                                         recv_sem=recv_sems.at[k - 1], device_id=(px, py, pc), device_id_type=MESH).wait_recv()
        for cp in sends:
            cp.wait_send()
        mine.wait()

    return pl.pallas_call(
        body, out_shape=SDS((N_DEV, *shard.shape), shard.dtype),
        in_specs=[pl.BlockSpec(memory_space=pl.ANY)], out_specs=pl.BlockSpec(memory_space=pl.ANY),
        scratch_shapes=[pltpu.SemaphoreType.DMA((N_DEV - 1,)), pltpu.SemaphoreType.DMA((N_DEV - 1,)), pltpu.SemaphoreType.DMA],
        name=name)(shard)


_HBM_SPEC = pl.BlockSpec(memory_space=pltpu.HBM)
_SEM_SPEC = pl.BlockSpec(memory_space=pltpu.SEMAPHORE)
_DATAFLOW = pltpu.SideEffectType.DATAFLOW_SIDE_EFFECTING


def _exchange_start(srcs, per_peer, name):
    n = len(srcs)
    lands = [jnp.zeros((N_DEV, *s.shape[-2:]), s.dtype) for s in srcs]

    def body(*refs):
        src_refs, land_refs = refs[:n], refs[n:2 * n]
        send_sems, recv_sems, token = refs[2 * n], refs[2 * n + 1], refs[-1]
        x, y, c = _my_place()
        me = 4 * x + 2 * y + c
        for a in range(n):
            for k in range(1, N_DEV):
                px, py, pc = _peer(x, y, c, k)
                mine = src_refs[a].at[4 * px + 2 * py + pc] if per_peer else src_refs[a]
                pltpu.make_async_remote_copy(src_ref=mine, dst_ref=land_refs[a].at[me], send_sem=send_sems.at[7 * a + k - 1],
                                             recv_sem=recv_sems.at[7 * a + k - 1], device_id=(px, py, pc),
                                             device_id_type=MESH).start()
        token[...] = jnp.zeros_like(token)

    n_sem = (N_DEV - 1) * n
    res = pl.pallas_call(
        body, name=name,
        out_shape=(pltpu.SemaphoreType.DMA((n_sem,)), pltpu.SemaphoreType.DMA((n_sem,)),
                   *[pltpu.HBM(a.shape, a.dtype) for a in srcs + lands], SDS((SUBLANES, LANES), F32)),
        in_specs=(_HBM_SPEC,) * (2 * n),
        out_specs=(_SEM_SPEC, _SEM_SPEC) + (_HBM_SPEC,) * (2 * n) + (pl.BlockSpec(memory_space=pltpu.VMEM),),
        input_output_aliases={i: 2 + i for i in range(2 * n)}, compiler_params=pltpu.CompilerParams(has_side_effects=_DATAFLOW),
    )(*[pltpu.with_memory_space_constraint(a, pltpu.HBM) for a in srcs + lands])
    return res[0], res[1], list(res[2:2 + n]), list(res[2 + n:2 + 2 * n]), res[-1]


def _exchange_wait(started, after, per_peer, name):
    send_sems, recv_sems, src_thrus, land_thrus, _ = started
    n = len(src_thrus)

    def body(*refs):
        src_refs, land_refs = refs[:n], refs[n:2 * n]
        send_sems, recv_sems = refs[2 * n], refs[2 * n + 1]
        x, y, c = _my_place()
        me = 4 * x + 2 * y + c
        for a in range(n):
            for k in range(1, N_DEV):
                px, py, pc = _peer(x, y, c, k)
                mine = src_refs[a].at[me] if per_peer else src_refs[a]
                copy = pltpu.make_async_remote_copy(src_ref=mine, dst_ref=land_refs[a].at[4 * px + 2 * py + pc],
                                                    send_sem=send_sems.at[7 * a + k - 1], recv_sem=recv_sems.at[7 * a + k - 1],
                                                    device_id=(px, py, pc), device_id_type=MESH)
                copy.wait_send()
                copy.wait_recv()

    res = pl.pallas_call(
        body, name=name, out_shape=tuple(pltpu.HBM(a.shape, a.dtype) for a in src_thrus + land_thrus),
        in_specs=(_HBM_SPEC,) * (2 * n) + (_SEM_SPEC, _SEM_SPEC, pl.BlockSpec(memory_space=pl.ANY)),
        out_specs=(_HBM_SPEC,) * (2 * n), input_output_aliases={i: i for i in range(2 * n)},
        compiler_params=pltpu.CompilerParams(has_side_effects=_DATAFLOW),
    )(*src_thrus, *land_thrus, send_sems, recv_sems, after)
    return list(res[n:])


def _row_tile(rows, cap):
    best = None
    for t in range(2 * SUBLANES, min(rows, cap) + 1, 2 * SUBLANES):
        if rows % t == 0:
            best = t
    return best or rows


def _sum_parts(parts, name, own=None):
    rows, cols = parts.shape[1:]

    def body(*refs):
        p_ref, g_out = refs[0], refs[-1]
        g = p_ref[0].astype(F32)
        for s in range(1, N_DEV):
            g = g + p_ref[s].astype(F32)
        if own is not None:
            g = g + refs[1][...].astype(F32)
        g_out[...] = g

    tr = _row_tile(rows, 384 * 1024 // cols)
    blk = pl.BlockSpec((tr, cols), lambda i: (i, 0))
    extra = [] if own is None else [own]
    return pl.pallas_call(
        body, grid=(rows // tr,),
        in_specs=[pl.BlockSpec((N_DEV, tr, cols), lambda i: (0, i, 0))] + [blk] * len(extra),
        out_specs=blk, out_shape=SDS((rows, cols), F32),
        compiler_params=_params(("parallel",)), name=name)(parts, *extra)


def _adamw(g, wgt, m, v, name):
    rows, cols = wgt.shape

    def body(g_ref, w_ref, m_ref, v_ref, d_out, m_out, v_out):
        g_ = g_ref[...]
        m_new = ADAM_B1 * m_ref[...] + (1.0 - ADAM_B1) * g_
        v_new = ADAM_B2 * v_ref[...] + (1.0 - ADAM_B2) * (g_ * g_)
        m_hat = m_new / (1.0 - ADAM_B1 ** ADAM_STEP)
        v_hat = v_new / (1.0 - ADAM_B2 ** ADAM_STEP)
        d_out[...] = -ADAM_LR * (m_hat / (jnp.sqrt(v_hat) + ADAM_EPS) + ADAM_WD * w_ref[...])
        m_out[...] = m_new
        v_out[...] = v_new

    tr = _row_tile(rows, 512 * 1024 // cols)
    blk = pl.BlockSpec((tr, cols), lambda i: (i, 0))
    return pl.pallas_call(
        body, grid=(rows // tr,), in_specs=[blk] * 4, out_specs=[blk] * 3, out_shape=[SDS((rows, cols), F32)] * 3,
        compiler_params=_params(("parallel",)), name=name)(g, wgt, m, v)


PACK_ALIGN = 2 * SUBLANES * LANES
PACK_ROWS = 512

ROW_SHARDED = ("s5_w_glu", "w_out", "ffn_w_down")
TRANSPOSED = ("w_in", "ffn_w_up", "w_branch_rwkv", "w_branch_s5")
SMALL_SHARDED = ("rwkv_w2", "rwkv_a2", "rwkv_g2", "ffn_conv_w")
SHARDED = ROW_SHARDED + TRANSPOSED + SMALL_SHARDED
F32_GATHER = ("ffn_conv_w",)
REPLICATED =("norm_mix_pre", "norm_mix_post", "norm_ffn_pre", "norm_ffn_post", "b_gate", "rwkv_shift_mu", "rwkv_w0",
              "rwkv_a0", "rwkv_k_k", "rwkv_k_a", "rwkv_r_k", "rwkv_lnx_w", "rwkv_lnx_b", "s5_a_re", "s5_a_im", "s5_b_re",
              "s5_b_im", "s5_c_re", "s5_c_im", "s5_d", "s5_log_step", "s5_b_glu", "ffn_conv_b")
WEIGHTS = ("norm_mix_pre", "norm_mix_post", "norm_ffn_pre", "norm_ffn_post", "w_in", "b_gate", "rwkv_shift_mu", "rwkv_w0",
           "rwkv_w2", "rwkv_a0", "rwkv_a2", "rwkv_g2", "rwkv_k_k", "rwkv_k_a", "rwkv_r_k", "rwkv_lnx_w", "rwkv_lnx_b",
           "s5_a_re", "s5_a_im", "s5_b_re", "s5_b_im", "s5_c_re", "s5_c_im", "s5_d", "s5_log_step", "s5_w_glu", "s5_b_glu",
           "w_branch_rwkv", "w_branch_s5", "w_out", "ffn_w_up", "ffn_conv_w", "ffn_conv_b", "ffn_w_down")


def _pad_flat(a):
    flat = a.reshape(-1)
    pad = (-flat.shape[0]) % PACK_ALIGN
    return jnp.pad(flat, (0, pad)) if pad else flat


def _pad_cols(a, mult):
    pad = (-a.shape[1]) % mult
    return jnp.pad(a, ((0, 0), (0, pad))) if pad else a


def _pack(pieces):
    flat = jnp.concatenate([_pad_flat(p) for p in pieces])
    pad = (-flat.shape[0]) % (PACK_ROWS * LANES)
    return (jnp.pad(flat, (0, pad)) if pad else flat).reshape(-1, LANES)


def _unpack(buf, shapes):
    lead = buf.shape[:-2]
    flat = buf.reshape(*lead, -1)
    out, off = [], 0
    for s in shapes:
        n = math.prod(s)
        out.append(flat[..., off:off + n].reshape(*lead, *s))
        off += n + (-n) % PACK_ALIGN
    return out


def _rows_to_full(stack):
    return stack.reshape(-1, stack.shape[2])


def _cols_to_full(stack):
    return stack.transpose(1, 0, 2).reshape(stack.shape[1], -1)


EARLY_GRADS = ("ffn_w_down", "ffn_w_up", "w_out")
REST_GRADS = ("s5_w_glu", "w_in", "w_branch_rwkv", "w_branch_s5")


def _local_step(x, tgt, wt, get_late, send_early, send_mid, send_rest):
    wt = dict(wt)
    o512 = _block_ones(RW, HN)
    w_in = wt["w_in"]
    w_rw, w_u, w_g = w_in[:N_RWKV], w_in[N_RWKV:N_RWKV + S5W], w_in[N_RWKV + S5W:]
    b_gate = wt["b_gate"]
    b_r, b_s = b_gate[:, :D_MODEL], b_gate[:, D_MODEL:]
    zpad = lambda a, lo, n: jnp.pad(a.astype(F32), ((lo, n - lo - a.shape[0]), (0, 0)))
    w2p, a2p, g2p = zpad(wt["rwkv_w2"], 0, 256), zpad(wt["rwkv_a2"], 64, 256), zpad(wt["rwkv_g2"], 128, 256)
    pre_small = [wt["rwkv_w0"], wt["rwkv_a0"], wt["rwkv_k_k"], wt["rwkv_k_a"], w2p, a2p, g2p]
    post_prm = [wt["rwkv_lnx_w"], wt["rwkv_lnx_b"], wt["rwkv_r_k"].reshape(1, RW)]
    mu = wt["rwkv_shift_mu"]

    a_re, a_im = wt["s5_a_re"].reshape(1, S5N), wt["s5_a_im"].reshape(1, S5N)
    ls = jnp.repeat(wt["s5_log_step"].reshape(S5G), S5P).reshape(1, S5N)
    b_re_t = wt["s5_b_re"].reshape(S5N, S5C).T
    b_im_t = wt["s5_b_im"].reshape(S5N, S5C).T
    c_re, c_im = wt["s5_c_re"].reshape(S5G, S5C, S5P), wt["s5_c_im"].reshape(S5G, S5C, S5P)
    abr, abi, bbr, bbi = _s5_prep_fwd(a_re, a_im, ls, b_re_t, b_im_t)
    bblk = jnp.concatenate([_s5_blockdiag_in(bbr), _s5_blockdiag_in(bbi)], axis=2)
    cblk = jnp.concatenate([_s5_blockdiag_out(c_re), -_s5_blockdiag_out(c_im)], axis=1)
    s5_d = wt["s5_d"]

    h1 = _rms_fwd(x, wt["norm_mix_pre"], "rms1_fwd")
    p_rw = _mm(h1, w_rw, tb=True, name="mm_proj_rwkv")
    u = _mm(h1, w_u, tb=True, name="mm_proj_s5")
    gpre = _mm(h1, w_g, tb=True, name="mm_proj_gate")
    r, decay, k2, v, aa, bb, g = _rwkv_pre_fwd(p_rw, mu, pre_small, o512)
    y, sst, sast, vst, s_fin = _wkv_fwd(r, decay, k2, v, aa, bb)
    wt.update(get_late(y))
    y_r = _rwkv_post_fwd(y, r, k2, v, g, post_prm, o512)
    o_r = _mm(y_r, wt["w_branch_rwkv"], tb=True, name="mm_branch_rwkv")
    ys, kre, kim = _s5_fwd(u, bblk, cblk, abr, abi, s5_d)
    t_glu = _mm(ys, wt["s5_w_glu"], name="mm_glu")
    out5 = _glu_fwd(ys, t_glu, wt["s5_b_glu"])
    o_s = _mm(out5, wt["w_branch_s5"], tb=True, name="mm_branch_s5")
    mi = _merge_fwd(gpre, o_r, o_s, b_r, b_s)
    mixed = _mm(mi, wt["w_out"], name="mm_out")
    x2, h2 = _mid_fwd(x, mixed, wt["norm_mix_post"], wt["norm_ffn_pre"])
    z = _mm(h2, wt["ffn_w_up"], tb=True, name="mm_up")
    act = _conv_fwd(z, wt["ffn_conv_w"], wt["ffn_conv_b"])
    f = _mm(act, wt["ffn_w_down"], name="mm_down")
    dx3, df, d_norm_ffn_post, loss_part = _final(x2, f, tgt, wt["norm_ffn_post"])

    gr = {"norm_ffn_post": d_norm_ffn_post}
    dact = _mm(df, wt["ffn_w_down"], tb=True, name="mm_down_dx")
    gr["ffn_w_down"] = _mm(act, df, ta=True, out_dtype=BF16, name="mm_down_dw")
    dzg, dzv, gr["ffn_conv_w"], gr["ffn_conv_b"] = _conv_bwd(z, dact, wt["ffn_conv_w"], wt["ffn_conv_b"])
    dz = jnp.concatenate([dzg, dzv], axis=1)
    dh2 = _mm(dz, wt["ffn_w_up"], name="mm_up_dx")
    gr["ffn_w_up"] = _mm(dz, h2, ta=True, out_dtype=BF16, name="mm_up_dw")
    dx2, dmixed, gr["norm_mix_post"], gr["norm_ffn_pre"] = _mid_bwd(x2, mixed, dh2, dx3, wt["norm_mix_post"], wt["norm_ffn_pre"])
    dmi = _mm(dmixed, wt["w_out"], tb=True, name="mm_out_dx")
    gr["w_out"] = _mm(mi, dmixed, ta=True, out_dtype=BF16, name="mm_out_dw")
    token = send_early({n: gr[n] for n in EARLY_GRADS})
    b_r = b_r + jnp.tile(token[0:1, :], (1, D_MODEL // LANES))
    dgp_r, dgp_s, do_r, do_s, db_r, db_s = _merge_bwd(gpre, o_r, o_s, dmi, b_r, b_s)
    gr["b_gate"] = jnp.concatenate([db_r, db_s], axis=1)
    dout5 = _mm(do_s, wt["w_branch_s5"], name="mm_branch_s5_dx")
    gr["w_branch_s5"] = _mm(do_s, out5, ta=True, out_dtype=BF16, name="mm_branch_s5_dw")
    dys_a, dt_glu, gr["s5_b_glu"] = _glu_bwd(ys, t_glu, dout5, wt["s5_b_glu"])
    dys_b = _mm(dt_glu, wt["s5_w_glu"], tb=True, name="mm_glu_dx")
    gr["s5_w_glu"] = _mm(ys, dt_glu, ta=True, out_dtype=BF16, name="mm_glu_dw")
    du, dbblk, dcblk, dabr, dabi, gr["s5_d"] = _s5_bwd(u, dys_a, dys_b, kre, kim, bblk, cblk, abr, abi, s5_d)
    gr["s5_c_re"] = _s5_blockdiag_out_t(dcblk[:, :S5_BN, :]).reshape(wt["s5_c_re"].shape)
    gr["s5_c_im"] = (-_s5_blockdiag_out_t(dcblk[:, S5_BN:, :])).reshape(wt["s5_c_im"].shape)
    dbbr, dbbi = _s5_blockdiag_in_t(dbblk[:, :, :S5_BN]), _s5_blockdiag_in_t(dbblk[:, :, S5_BN:])
    gsel = (lax.broadcasted_iota(jnp.int32, (S5N, LANES), 0) // S5P == lax.broadcasted_iota(jnp.int32, (S5N, LANES), 1)).astype(F32)
    d_are, d_aim, d_ls, d_bre_t, d_bim_t = _s5_prep_bwd(a_re, a_im, ls, b_re_t, b_im_t, (dabr, dabi, dbbr, dbbi), gsel)
    gr["s5_a_re"] = d_are.reshape(wt["s5_a_re"].shape)
    gr["s5_a_im"] = d_aim.reshape(wt["s5_a_im"].shape)
    gr["s5_log_step"] = d_ls[0:1, :S5G]
    gr["s5_b_re"] = d_bre_t.T.reshape(wt["s5_b_re"].shape)
    gr["s5_b_im"] = d_bim_t.T.reshape(wt["s5_b_im"].shape)
    dy_r = _mm(do_r, wt["w_branch_rwkv"], name="mm_branch_rwkv_dx")
    gr["w_branch_rwkv"] = _mm(do_r, y_r, ta=True, out_dtype=BF16, name="mm_branch_rwkv_dw")
    token = send_mid(gr)
    post_prm = [post_prm[0] + jnp.tile(token[0:1, :], (1, RW // LANES))] + post_prm[1:]
    dy, dr1, dk1, dv1, dg, gr["rwkv_lnx_w"], gr["rwkv_lnx_b"], d_rk = _rwkv_post_bwd(y, r, k2, v, g, dy_r, post_prm, o512)
    gr["rwkv_r_k"] = d_rk.reshape(wt["rwkv_r_k"].shape)
    dr2, ddecay, dk2, daa, dbb, dv2 = _wkv_bwd(r, decay, k2, aa, bb, dy, sst, sast, vst, s_fin)
    cots = [dr1, ddecay, dk1, dv1, daa, dbb, dg, dr2, dk2, dv2]
    dp_rw, gr["rwkv_shift_mu"], gr["rwkv_w0"], gr["rwkv_a0"], gr["rwkv_k_k"], gr["rwkv_k_a"], dw2p, da2p, dg2p = \
        _rwkv_pre_bwd(p_rw, cots, mu, pre_small, o512)
    gr["rwkv_w2"], gr["rwkv_a2"], gr["rwkv_g2"] = dw2p[:64], da2p[64:128], dg2p[128:]
    dproj = jnp.concatenate([dp_rw, du, dgp_r, dgp_s], axis=1)
    gr["w_in"] = _mm(dproj, h1, ta=True, out_dtype=BF16, name="mm_proj_dw")
    token = send_rest(gr)
    dh1 = _mm(dproj, w_in + token[0, 0].astype(w_in.dtype), name="mm_proj_dx")
    dx, gr["norm_mix_pre"] = _rms_bwd(x, wt["norm_mix_pre"], dh1, dx2, "rms1_bwd")
    return loss_part[0, 0], dx, gr


def kernel(x, norm_mix_pre, norm_mix_post, norm_ffn_pre, norm_ffn_post, w_in, b_gate, rwkv_shift_mu, rwkv_w0, rwkv_w2, rwkv_a0, rwkv_a2, rwkv_g2, rwkv_k_k, rwkv_k_a, rwkv_r_k, rwkv_lnx_w, rwkv_lnx_b, s5_a_re, s5_a_im, s5_b_re, s5_b_im, s5_c_re, s5_c_im, s5_d, s5_log_step, s5_w_glu, s5_b_glu, w_branch_rwkv, w_branch_s5, w_out, ffn_w_up, ffn_conv_w, ffn_conv_b, ffn_w_down, loss_target, m_norm_mix_pre, m_norm_mix_post, m_norm_ffn_pre, m_norm_ffn_post, m_w_in, m_b_gate, m_rwkv_shift_mu, m_rwkv_w0, m_rwkv_w2, m_rwkv_a0, m_rwkv_a2, m_rwkv_g2, m_rwkv_k_k, m_rwkv_k_a, m_rwkv_r_k, m_rwkv_lnx_w, m_rwkv_lnx_b, m_s5_a_re, m_s5_a_im, m_s5_b_re, m_s5_b_im, m_s5_c_re, m_s5_c_im, m_s5_d, m_s5_log_step, m_s5_w_glu, m_s5_b_glu, m_w_branch_rwkv, m_w_branch_s5, m_w_out, m_ffn_w_up, m_ffn_conv_w, m_ffn_conv_b, m_ffn_w_down, v_norm_mix_pre, v_norm_mix_post, v_norm_ffn_pre, v_norm_ffn_post, v_w_in, v_b_gate, v_rwkv_shift_mu, v_rwkv_w0, v_rwkv_w2, v_rwkv_a0, v_rwkv_a2, v_rwkv_g2, v_rwkv_k_k, v_rwkv_k_a, v_rwkv_r_k, v_rwkv_lnx_w, v_rwkv_lnx_b, v_s5_a_re, v_s5_a_im, v_s5_b_re, v_s5_b_im, v_s5_c_re, v_s5_c_im, v_s5_d, v_s5_log_step, v_s5_w_glu, v_s5_b_glu, v_w_branch_rwkv, v_w_branch_s5, v_w_out, v_ffn_w_up, v_ffn_conv_w, v_ffn_conv_b, v_ffn_w_down):
    args = dict(locals())
    wgt = {n: args[n] for n in WEIGHTS}
    mom = {n: args["m_" + n] for n in WEIGHTS}
    var = {n: args["v_" + n] for n in WEIGHTS}
    me = 4 * lax.axis_index("x") + 2 * lax.axis_index("y") + lax.axis_index("c")
    shard_shapes = {n: wgt[n].shape[1:] for n in SHARDED}
    sent_shapes = {n: (shard_shapes[n][::-1] if n in TRANSPOSED else shard_shapes[n]) for n in SHARDED}
    sent = lambda n, a: a.T if n in TRANSPOSED else a

    unshard = lambda n, blocks: _cols_to_full(blocks) if n in SMALL_SHARDED else _rows_to_full(blocks)

    first = ("w_in", "rwkv_w2", "rwkv_a2", "rwkv_g2")
    late = tuple(n for n in SHARDED if n not in first and n not in F32_GATHER)
    late_payloads = [sent(n, wgt[n][0].astype(BF16)) for n in late] + [_pack([wgt[n][0] for n in F32_GATHER])]
    late_started = _exchange_start(late_payloads, False, "gather_late_start")
    got_first = _unpack(_all_gather(_pack([sent(n, wgt[n][0].astype(BF16)) for n in first]), "all_gather_weights"),
                        [sent_shapes[n] for n in first])
    wt = {n: wgt[n] for n in REPLICATED}
    for n, blocks in zip(first, got_first, strict=True):
        wt[n] = unshard(n, blocks)

    def get_late(after):
        lands = _exchange_wait(late_started, after, False, "gather_late_wait")
        whole_ = [lax.dynamic_update_index_in_dim(land, mine, me, 0) for land, mine in zip(lands, late_payloads, strict=True)]
        out = {n: _rows_to_full(blocks) for n, blocks in zip(late, whole_[:-1], strict=True)}
        taps = _unpack(whole_[-1], [sent_shapes[n] for n in F32_GATHER])
        out.update({n: unshard(n, b_) for n, b_ in zip(F32_GATHER, taps, strict=True)})
        return out

    lowrank = ("rwkv_w2", "rwkv_a2", "rwkv_g2")
    rest_names = REST_GRADS + lowrank
    whole = tuple(n for n in ("ffn_conv_w",) + REPLICATED if n != "norm_mix_pre")
    whole_mid = tuple(n for n in whole if not n.startswith("rwkv_"))
    whole_late = tuple(n for n in whole if n.startswith("rwkv_"))
    blocks_of = lambda g_: g_.reshape(N_DEV, -1, g_.shape[-1])
    as_sent = lambda n, g_: blocks_of(g_.T if n in lowrank else g_)
    started = {}

    def send_early(grads_):
        started["early"] = _exchange_start([as_sent(n, grads_[n]) for n in EARLY_GRADS], True, "exchange_early_start")
        return started["early"][4]

    def send_whole(key, names):
        def send(grads_):
            started[key + "_mine"] = _pack([grads_[n] for n in names])
            started[key] = _exchange_start([started[key + "_mine"]], False, "gather_" + key + "_start")
            return started[key][4]
        return send

    def send_rest(grads_):
        started["rest"] = _exchange_start([as_sent(n, grads_[n]) for n in rest_names], True, "exchange_rest_start")
        return started["rest"][4] + send_whole("late", whole_late)(grads_)

    loss_part, dx, gr = _local_step(x[0], loss_target[0], wt, get_late, send_early, send_whole("mid", whole_mid), send_rest)

    grads, upd = {}, {}
    flat2d = lambda a: a.reshape(a.shape[-2:])

    def finish_large(key, names, after):
        lands = _exchange_wait(started[key], after, True, "exchange_" + key + "_wait")
        for n, land in zip(names, lands, strict=True):
            own = lax.dynamic_index_in_dim(as_sent(n, gr[n]), me, 0, keepdims=False)
            total = _sum_parts(land, "sum_" + n, own=own)
            grads[n] = (total.T if n in TRANSPOSED + lowrank else total).reshape(wgt[n].shape)
            trio = _adamw(flat2d(grads[n]), flat2d(wgt[n]), flat2d(mom[n]), flat2d(var[n]), "adamw_" + n)
            upd[n] = tuple(t.reshape(wgt[n].shape) for t in trio)
        return upd[names[-1]][0]

    got = {}

    def finish_whole(key, names, after):
        land, = _exchange_wait(started[key], after, False, "gather_" + key + "_wait")
        total = _sum_parts(lax.dynamic_update_index_in_dim(land, started[key + "_mine"], me, 0), "sum_" + key + "_grads")
        got.update(zip(names, _unpack(total, [gr[n].shape for n in names]), strict=True))
        return total

    done = finish_large("early", EARLY_GRADS, dx)
    done = finish_whole("mid", whole_mid, done)
    done = finish_large("rest", rest_names, done)
    done = finish_whole("late", whole_late, done)
    last, _ = lax.optimization_barrier((gr["norm_mix_pre"].reshape(SUBLANES, LANES), done))
    got["norm_mix_pre"] = _sum_parts(_all_gather(last, "gather_last_grad"), "sum_last_grad").reshape(gr["norm_mix_pre"].shape)
    packed = tuple(n for n in WEIGHTS if n not in EARLY_GRADS + rest_names)
    for n in packed:
        if n in SMALL_SHARDED:
            cols = shard_shapes[n][1]
            grads[n] = lax.dynamic_slice_in_dim(got[n], me * cols, cols, axis=1).reshape(wgt[n].shape)
        else:
            grads[n] = got[n].reshape(wgt[n].shape)

    pack_local = lambda src: _pack([src[n].reshape(-1) for n in packed])
    outs = _adamw(pack_local(grads), pack_local(wgt), pack_local(mom), pack_local(var), "adamw_small")
    shapes = [wgt[n].shape for n in packed]
    upd.update(zip(packed, zip(*[_unpack(o, shapes) for o in outs], strict=True), strict=True))
    loss = lax.psum(loss_part, AXES)
    return (loss, dx[None], *[grads[n] for n in WEIGHTS], *[upd[n][0] for n in WEIGHTS], *[upd[n][1] for n in WEIGHTS],
            *[upd[n][2] for n in WEIGHTS])
```

```python
import functools
import math

import jax
import jax.numpy as jnp
from jax import lax
from jax.experimental import pallas as pl
from jax.experimental.pallas import tpu as pltpu

F32 = jnp.float32
BF16 = jnp.bfloat16
SDS = jax.ShapeDtypeStruct
HI = lax.Precision.HIGHEST
MESH = pl.DeviceIdType.MESH
AXES = ("x", "y", "c")
N_DEV = 8

D_MODEL = 1024
RW = 512
HN = 64
N_RWKV = 1792
S5W = 512
S5G = 32
S5C = 16
S5P = 64
S5N = S5G * S5P
D_FF = 2816
NORM_EPS = 1e-6
LNX_EPS = 64e-5

ADAM_LR = 0.001
ADAM_B1 = 0.9
ADAM_B2 = 0.999
ADAM_EPS = 1e-08
ADAM_WD = 0.01
ADAM_STEP = 10

LANES = 128
SUBLANES = 8
VMEM_LIMIT = 56 * 1024 * 1024
WKV_CHUNK = 32


def _params(sem=("arbitrary",)):
    return pltpu.CompilerParams(dimension_semantics=sem, vmem_limit_bytes=VMEM_LIMIT)


def _pick(n, cap):
    best = None
    for t in range(LANES, min(n, cap) + 1, LANES):
        if n % t == 0:
            best = t
    return best or n


MM_VMEM_BUDGET = 40 * 1024 * 1024


def _mm_tiles(m, n, k, a_bytes, b_bytes, o_bytes):
    def divisors(d):
        return [t for t in range(LANES, d + 1, LANES) if d % t == 0] or [d]

    tn = _pick(n, 2304)
    best = None
    for tm in divisors(m):
        for tk in divisors(k):
            vmem = 2 * (tm * tk * a_bytes + tk * tn * b_bytes) + tm * tn * 4 + 2 * tm * tn * o_bytes
            if vmem > MM_VMEM_BUDGET:
                continue
            key = ((m // tm) * (n // tn) * (k // tk), -tm)
            if best is None or key < best[0]:
                best = (key, tm, tk)
    assert best is not None, (m, n, k)
    return best[1], tn, best[2]


def _mm(a, b, *, ta=False, tb=False, out_dtype=F32, name):
    m = a.shape[1] if ta else a.shape[0]
    k = a.shape[0] if ta else a.shape[1]
    n = b.shape[0] if tb else b.shape[1]
    assert (b.shape[1] if tb else b.shape[0]) == k
    tm, tn, tk = _mm_tiles(m, n, k, a.dtype.itemsize, b.dtype.itemsize, jnp.dtype(out_dtype).itemsize)
    nk = k // tk
    dims = (((0 if ta else 1,), (1 if tb else 0,)), ((), ()))

    def body(a_ref, b_ref, o_ref, acc_ref):
        kk = pl.program_id(2)

        @pl.when(kk == 0)
        def _():
            acc_ref[...] = jnp.zeros_like(acc_ref)

        acc_ref[...] += lax.dot_general(a_ref[...].astype(BF16), b_ref[...].astype(BF16), dims,
                                        preferred_element_type=F32)

        @pl.when(kk == nk - 1)
        def _():
            o_ref[...] = acc_ref[...].astype(o_ref.dtype)

    a_spec = pl.BlockSpec((tk, tm), lambda i, j, kk: (kk, i)) if ta else pl.BlockSpec((tm, tk), lambda i, j, kk: (i, kk))
    b_spec = pl.BlockSpec((tn, tk), lambda i, j, kk: (j, kk)) if tb else pl.BlockSpec((tk, tn), lambda i, j, kk: (kk, j))
    return pl.pallas_call(
        body, grid=(m // tm, n // tn, nk), in_specs=[a_spec, b_spec],
        out_specs=pl.BlockSpec((tm, tn), lambda i, j, kk: (i, j)),
        out_shape=SDS((m, n), out_dtype), scratch_shapes=[pltpu.VMEM((tm, tn), F32)],
        compiler_params=_params(("parallel", "parallel", "arbitrary")), name=name)(a, b)


def _rows(fn, rows, params, out_rows, out_accs, *, name, tl, reverse=False, scratch=()):
    first = rows[0][0] if isinstance(rows[0], tuple) else rows[0]
    length = first.shape[0]
    tl = min(tl, length)
    nt = length // tl
    rmap = (lambda i: nt - 1 - i) if reverse else (lambda i: i)
    specs, arrs = [], []
    for r in rows:
        arr, wdt, cb = r if isinstance(r, tuple) else (r, r.shape[1], 0)
        specs.append(pl.BlockSpec((tl, wdt), lambda i, cb=cb: (rmap(i), cb)))
        arrs.append(arr)
    for p in params:
        specs.append(pl.BlockSpec(p.shape, lambda i, nd=p.ndim: (0,) * nd))
        arrs.append(p)
    out_shape = [SDS((length, c), dt) for c, dt in out_rows] + [SDS(s, F32) for s in out_accs]
    out_specs = [pl.BlockSpec((tl, c), lambda i: (rmap(i), 0)) for c, _ in out_rows]
    out_specs += [pl.BlockSpec(s, lambda i, nd=len(s): (0,) * nd) for s in out_accs]
    nr, npar, nor, noa = len(rows), len(params), len(out_rows), len(out_accs)

    def body(*refs):
        rin, pin = refs[:nr], refs[nr:nr + npar]
        rout = refs[nr + npar:nr + npar + nor]
        aout = refs[nr + npar + nor:nr + npar + nor + noa]
        scr = refs[nr + npar + nor + noa:]
        step = pl.program_id(0)
        outs_r, outs_a = fn(step, [r[...] for r in rin], [p[...] for p in pin], scr)
        for ref, val in zip(rout, outs_r, strict=True):
            ref[...] = val.astype(ref.dtype)

        @pl.when(step == 0)
        def _():
            for ref in aout:
                ref[...] = jnp.zeros_like(ref)

        for ref, val in zip(aout, outs_a, strict=True):
            ref[...] += val.astype(F32)

    res = pl.pallas_call(body, grid=(nt,), in_specs=specs, out_specs=out_specs, out_shape=out_shape,
                         scratch_shapes=list(scratch), compiler_params=_params(), name=name)(*arrs)
    return list(res)


def _rms(x, g):
    return x * lax.rsqrt(jnp.mean(x * x, axis=-1, keepdims=True) + NORM_EPS) * g


def _sig(x):
    return 0.5 * (jnp.tanh(0.5 * x) + 1.0)


def _softplus(x):
    return jnp.maximum(x, 0.0) + jnp.log(1.0 + jnp.exp(-jnp.abs(x)))


def _gelu(x):
    return x * (0.5 * (1.0 + jnp.tanh(math.sqrt(2.0 / math.pi) * (x + 0.044715 * (x * x * x)))))


def _bdot(a, b):
    return jnp.dot(a.astype(BF16), b.astype(BF16), preferred_element_type=F32)


def _hdot(a, b):
    return jnp.dot(a, b, precision=HI, preferred_element_type=F32)


def _segsum_impl(x, ones):
    hi = x.astype(BF16)
    lo = (x - hi.astype(F32)).astype(BF16)
    ones = ones.astype(BF16)
    return jnp.dot(jnp.concatenate([hi, lo], axis=1), jnp.concatenate([ones, ones], axis=0), preferred_element_type=F32)


@jax.custom_vjp
def _segsum(x, ones):
    return _segsum_impl(x, ones)


_segsum.defvjp(lambda x, ones: (_segsum_impl(x, ones), ones),
               lambda ones, g: (_segsum_impl(g, ones), jnp.zeros_like(ones)))


def _block_ones(n, blk):
    i = lax.broadcasted_iota(jnp.int32, (n, n), 0) // blk
    j = lax.broadcasted_iota(jnp.int32, (n, n), 1) // blk
    return (i == j).astype(F32)


def _rms_fwd(x, g, name):
    return _rows(lambda s, r, p, _: ([_rms(r[0], p[0])], []), [x], [g], [(x.shape[1], BF16)], [], name=name, tl=512)[0]


def _rms_bwd(x, g, dh, dres, name):
    def fn(s, r, p, _):
        _, vjp = jax.vjp(_rms, r[0], p[0])
        dx, dg = vjp(r[1])
        return [dx + r[2]], [dg]
    return _rows(fn, [x, dh, dres], [g], [(x.shape[1], F32)], [g.shape], name=name, tl=256)


def _rwkv_pre_math(k_, lr, w0, a0, k_k, k_a, w2p, a2p, g2p, o512):
    pre_w = w0 + _bdot(jnp.tanh(lr), w2p)
    w = -_softplus(-pre_w) - 0.5
    decay = jnp.exp(-jnp.exp(w))
    a = _sig(a0 + _bdot(lr, a2p))
    g = _bdot(_sig(lr), g2p)
    kr = k_ * k_k
    kk = kr / jnp.maximum(jnp.sqrt(_segsum(kr * kr, o512)), 1e-12)
    k2 = k_ * (1.0 + (a - 1.0) * k_a)
    return decay, k2, -kk, kk * a, g


def _shift_down(p, prev_row):
    row = lax.broadcasted_iota(jnp.int32, p.shape, 0)
    return jnp.where(row == 0, jnp.broadcast_to(prev_row, p.shape), pltpu.roll(p, 1, 0))


def _shift_up(q, next_row):
    n = q.shape[0]
    row = lax.broadcasted_iota(jnp.int32, q.shape, 0)
    return jnp.where(row == n - 1, jnp.broadcast_to(next_row, q.shape), pltpu.roll(q, n - 1, 0))


def _rwkv_pre_fwd(p, mu, small, o512):
    def fn(step, r, prm, scr):
        car = scr[0]

        @pl.when(step == 0)
        def _():
            car[...] = jnp.zeros_like(car)

        x = r[0]
        prev = _shift_down(x, car[SUBLANES - 1:SUBLANES, :])
        car[...] = x[x.shape[0] - SUBLANES:, :]
        xs = x + (prev - x) * prm[0]
        decay, k2, aa, bb, g = _rwkv_pre_math(xs[:, RW:2 * RW], xs[:, 3 * RW:], *prm[1:])
        return [xs[:, :RW], decay, k2, xs[:, 2 * RW:3 * RW], aa, bb, g], []
    return _rows(fn, [p], [mu, *small, o512], [(RW, F32)] * 7, [], name="rwkv_pre_fwd", tl=256,
                 scratch=[pltpu.VMEM((SUBLANES, N_RWKV), F32)])


def _rwkv_pre_bwd(p, cots, mu, small, o512):
    length = p.shape[0]
    tl = min(256, length)
    nt = length // tl
    rows_per = tl // SUBLANES
    params = [mu, *small, o512]
    acc_shapes = [mu.shape] + [q.shape for q in small]
    nr, npar, nacc = 2 + len(cots), len(params), len(acc_shapes)

    def body(*refs):
        rin, pin = refs[:nr], refs[nr:nr + npar]
        dp_ref = refs[nr + npar]
        aout = refs[nr + npar + 1:nr + npar + 1 + nacc]
        car_q = refs[nr + npar + 1 + nacc]
        step = pl.program_id(0)

        @pl.when(step == 0)
        def _():
            car_q[...] = jnp.zeros_like(car_q)
            for ref in aout:
                ref[...] = jnp.zeros_like(ref)

        x = rin[0][...]
        prev_row = jnp.where(step == nt - 1, 0.0, rin[1][SUBLANES - 1:SUBLANES, :])
        dr, ddecay, dk2, dv, daa, dbb, dg, dr_b, dk2_b, dv_b = [r[...] for r in rin[2:]]
        dr, dk2, dv = dr + dr_b, dk2 + dk2_b, dv + dv_b
        prm = [q[...] for q in pin]
        mu_, o512_ = prm[0], prm[-1]
        prev = _shift_down(x, prev_row)
        xs = x + (prev - x) * mu_
        _, vjp = jax.vjp(lambda k_, lr, *w: _rwkv_pre_math(k_, lr, *w, o512_), xs[:, RW:2 * RW], xs[:, 3 * RW:], *prm[1:-1])
        dk_, dlr, *dsmall = vjp((ddecay, dk2, daa, dbb, dg))
        dxs = jnp.concatenate([dr, dk_, dv, dlr], axis=1)
        q = dxs * mu_
        dp_ref[...] = (dxs - q + _shift_up(q, car_q[0:1, :])).astype(dp_ref.dtype)
        car_q[...] = q[:SUBLANES, :]
        aout[0][...] += jnp.sum((prev - x) * dxs, axis=0, keepdims=True)
        for ref, val in zip(aout[1:], dsmall, strict=True):
            ref[...] += val

    rmap = lambda i: nt - 1 - i
    specs = [pl.BlockSpec((tl, N_RWKV), lambda i: (rmap(i), 0)),
             pl.BlockSpec((SUBLANES, N_RWKV), lambda i: (jnp.maximum(rmap(i) * rows_per - 1, 0), 0))]
    specs += [pl.BlockSpec((tl, RW), lambda i: (rmap(i), 0)) for _ in cots]
    specs += [pl.BlockSpec(q.shape, lambda i, nd=q.ndim: (0,) * nd) for q in params]
    out_shape = [SDS((length, N_RWKV), BF16)] + [SDS(sh, F32) for sh in acc_shapes]
    out_specs = [pl.BlockSpec((tl, N_RWKV), lambda i: (rmap(i), 0))]
    out_specs += [pl.BlockSpec(sh, lambda i, nd=len(sh): (0,) * nd) for sh in acc_shapes]
    res = pl.pallas_call(body, grid=(nt,), in_specs=specs, out_specs=out_specs, out_shape=out_shape,
                         scratch_shapes=[pltpu.VMEM((SUBLANES, N_RWKV), F32)],
                         compiler_params=_params(), name="rwkv_pre_bwd")(p, p, *cots, *params)
    return list(res)


def _rwkv_post_math(y, r, k2, v, g, lnx_w, lnx_b, r_k, o512):
    mean = _segsum(y, o512) * (1.0 / HN)
    yc = y - mean
    var = _segsum(yc * yc, o512) * (1.0 / HN)
    yn = yc * lax.rsqrt(var + LNX_EPS) * lnx_w + lnx_b
    bonus = _segsum(r * k2 * r_k, o512) * v
    return (yn + bonus) * g


def _rwkv_post_fwd(y, r, k2, v, g, prm, o512):
    return _rows(lambda s, rr, p, _: ([_rwkv_post_math(*rr, *p)], []), [y, r, k2, v, g], [*prm, o512],
                 [(RW, BF16)], [], name="rwkv_post_fwd", tl=256)[0]


def _rwkv_post_bwd(y, r, k2, v, g, dout, prm, o512):
    def fn(s, rr, p, _):
        o = p[-1]
        _, vjp = jax.vjp(lambda *a: _rwkv_post_math(*a, o), *rr[:5], *p[:-1])
        gr = vjp(rr[5])
        return list(gr[:5]), list(gr[5:])
    return _rows(fn, [y, r, k2, v, g, dout], [*prm, o512], [(RW, F32)] * 5, [q.shape for q in prm],
                 name="rwkv_post_bwd", tl=256)


def _glu_math(ys, t, b):
    return ys * _sig(t + b)


def _glu_fwd(ys, t, b):
    return _rows(lambda s, r, p, _: ([_glu_math(r[0], r[1], p[0])], []), [ys, t], [b], [(S5W, BF16)], [],
                 name="s5_glu_fwd", tl=512)[0]


def _glu_bwd(ys, t, dout, b):
    def fn(s, r, p, _):
        _, vjp = jax.vjp(_glu_math, r[0], r[1], p[0])
        dys, dt, db = vjp(r[2])
        return [dys, dt], [db]
    return _rows(fn, [ys, t, dout], [b], [(S5W, F32), (S5W, BF16)], [b.shape], name="s5_glu_bwd", tl=512)


def _merge_math(gp_r, gp_s, o_r, o_s, b_r, b_s):
    return _sig(gp_r + b_r) * o_r + _sig(gp_s + b_s) * o_s


def _merge_fwd(gpre, o_r, o_s, b_r, b_s):
    return _rows(lambda s, r, p, _: ([_merge_math(*r, *p)], []),
                 [(gpre, D_MODEL, 0), (gpre, D_MODEL, 1), o_r, o_s], [b_r, b_s], [(D_MODEL, BF16)], [],
                 name="merge_fwd", tl=256)[0]


def _merge_bwd(gpre, o_r, o_s, dmi, b_r, b_s):
    def fn(s, r, p, _):
        _, vjp = jax.vjp(_merge_math, *r[:4], *p)
        dgr, dgs, dor, dos, dbr, dbs = vjp(r[4])
        return [dgr, dgs, dor, dos], [dbr, dbs]
    return _rows(fn, [(gpre, D_MODEL, 0), (gpre, D_MODEL, 1), o_r, o_s, dmi], [b_r, b_s], [(D_MODEL, BF16)] * 4,
                 [b_r.shape, b_s.shape], name="merge_bwd", tl=256)


def _mid_fwd(x, mixed, g_post, g_pre):
    def fn(s, r, p, _):
        x2 = r[0] + _rms(r[1], p[0])
        return [x2, _rms(x2, p[1])], []
    return _rows(fn, [x, mixed], [g_post, g_pre], [(D_MODEL, F32), (D_MODEL, BF16)], [], name="mid_fwd", tl=256)


def _mid_bwd(x2, mixed, dh2, dx3, g_post, g_pre):
    def fn(s, r, p, _):
        _, vjp1 = jax.vjp(_rms, r[0], p[1])
        dx2, dg_pre = vjp1(r[2])
        dx2 = dx2 + r[3]
        _, vjp2 = jax.vjp(_rms, r[1], p[0])
        dmixed, dg_post = vjp2(dx2)
        return [dx2, dmixed], [dg_post, dg_pre]
    return _rows(fn, [x2, mixed, dh2, dx3], [g_post, g_pre], [(D_MODEL, F32), (D_MODEL, BF16)], [g_post.shape, g_pre.shape],
                 name="mid_bwd", tl=256)


def _final(x2, f, tgt, g_post):
    def fn(s, r, p, _):
        y, vjp = jax.vjp(_rms, r[1], p[0])
        diff = r[0] + y - r[2]
        dx3 = diff * (1.0 / D_MODEL)
        df, dg = vjp(dx3)
        part = 0.5 * jnp.sum(jnp.sum(diff * diff, axis=1, keepdims=True), axis=0, keepdims=True) * (1.0 / D_MODEL)
        return [dx3, df], [dg, jnp.broadcast_to(part, (1, LANES))]
    return _rows(fn, [x2, f, tgt], [g_post], [(D_MODEL, F32), (D_MODEL, BF16)], [g_post.shape, (1, LANES)], name="final", tl=256)


def _conv_taps(z, car):
    row = lax.broadcasted_iota(jnp.int32, z.shape, 0)
    z1 = jnp.where(row == 0, jnp.broadcast_to(car[7:8, :], z.shape), pltpu.roll(z, 1, 0))
    z2 = pltpu.roll(z, 2, 0)
    z2 = jnp.where(row == 0, jnp.broadcast_to(car[6:7, :], z.shape), z2)
    z2 = jnp.where(row == 1, jnp.broadcast_to(car[7:8, :], z.shape), z2)
    return z1, z2


def _conv(z, car, w, b):
    z1, z2 = _conv_taps(z, car)
    return b + w[0:1, :] * z2 + w[1:2, :] * z1 + w[2:3, :] * z, z1, z2


def _conv_fwd(z, conv_w, conv_b):
    length = z.shape[0]
    tl = min(256, length)
    nt = length // tl
    tc = _pick(D_FF, 1536)
    nb = D_FF // tc

    def body(zg_ref, zv_ref, wg_ref, wv_ref, bg_ref, bv_ref, o_ref, cg, cv):
        @pl.when(pl.program_id(1) == 0)
        def _():
            cg[...] = jnp.zeros_like(cg)
            cv[...] = jnp.zeros_like(cv)

        zg, zv = zg_ref[...], zv_ref[...]
        gate, _, _ = _conv(zg, cg[...], wg_ref[...], bg_ref[...])
        val, _, _ = _conv(zv, cv[...], wv_ref[...], bv_ref[...])
        cg[...] = zg[tl - SUBLANES:, :]
        cv[...] = zv[tl - SUBLANES:, :]
        o_ref[...] = (_gelu(gate) * val).astype(o_ref.dtype)

    zspec = lambda off: pl.BlockSpec((tl, tc), lambda j, i: (i, j + off))
    wspec = lambda off, r: pl.BlockSpec((r, tc), lambda j, i: (0, j + off))
    return pl.pallas_call(
        body, grid=(nb, nt),
        in_specs=[zspec(0), zspec(nb), wspec(0, 3), wspec(nb, 3), wspec(0, 1), wspec(nb, 1)],
        out_specs=pl.BlockSpec((tl, tc), lambda j, i: (i, j)), out_shape=SDS((length, D_FF), BF16),
        scratch_shapes=[pltpu.VMEM((SUBLANES, tc), F32)] * 2,
        compiler_params=_params(("arbitrary", "arbitrary")), name="conv_fwd")(z, z, conv_w, conv_w, conv_b, conv_b)


def _conv_bwd(z, dact, conv_w, conv_b):
    length = z.shape[0]
    tl = min(256, length)
    nt = length // tl
    tc = _pick(D_FF, 1536)
    nb = D_FF // tc
    rows_per = tl // SUBLANES

    def half_bwd(dzc, z, z1, z2, w, dcar):
        n = tl
        row = lax.broadcasted_iota(jnp.int32, dzc.shape, 0)
        u1 = jnp.where(row == n - 1, jnp.broadcast_to(dcar[0:1, :], dzc.shape), pltpu.roll(dzc, n - 1, 0))
        u2 = pltpu.roll(dzc, n - 2, 0)
        u2 = jnp.where(row == n - 2, jnp.broadcast_to(dcar[0:1, :], dzc.shape), u2)
        u2 = jnp.where(row == n - 1, jnp.broadcast_to(dcar[1:2, :], dzc.shape), u2)
        dz = w[2:3, :] * dzc + w[1:2, :] * u1 + w[0:1, :] * u2
        dw = jnp.concatenate([jnp.sum(dzc * z2, axis=0, keepdims=True), jnp.sum(dzc * z1, axis=0, keepdims=True),
                              jnp.sum(dzc * z, axis=0, keepdims=True)], axis=0)
        return dz, dw, jnp.sum(dzc, axis=0, keepdims=True)

    def body(zg_ref, zv_ref, pg_ref, pv_ref, da_ref, wg_ref, wv_ref, bg_ref, bv_ref,
             dzg_ref, dzv_ref, dwg_ref, dwv_ref, dbg_ref, dbv_ref, cg, cv):
        step = pl.program_id(1)

        @pl.when(step == 0)
        def _():
            cg[...] = jnp.zeros_like(cg)
            cv[...] = jnp.zeros_like(cv)
            for ref in (dwg_ref, dwv_ref, dbg_ref, dbv_ref):
                ref[...] = jnp.zeros_like(ref)

        is_first_tile = step == nt - 1
        zg, zv = zg_ref[...], zv_ref[...]
        pg = jnp.where(is_first_tile, 0.0, pg_ref[...])
        pv = jnp.where(is_first_tile, 0.0, pv_ref[...])
        wg, wv = wg_ref[...], wv_ref[...]
        gate, zg1, zg2 = _conv(zg, pg, wg, bg_ref[...])
        val, zv1, zv2 = _conv(zv, pv, wv, bv_ref[...])
        act_g, vjp = jax.vjp(_gelu, gate)
        da = da_ref[...]
        dgate = vjp(da * val)[0]
        dval = da * act_g
        dzg, dwg, dbg = half_bwd(dgate, zg, zg1, zg2, wg, cg[...])
        dzv, dwv, dbv = half_bwd(dval, zv, zv1, zv2, wv, cv[...])
        cg[...] = dgate[:SUBLANES, :]
        cv[...] = dval[:SUBLANES, :]
        dzg_ref[...] = dzg.astype(dzg_ref.dtype)
        dzv_ref[...] = dzv.astype(dzv_ref.dtype)
        dwg_ref[...] += dwg
        dwv_ref[...] += dwv
        dbg_ref[...] += dbg
        dbv_ref[...] += dbv

    rmap = lambda i: nt - 1 - i
    zspec = lambda off: pl.BlockSpec((tl, tc), lambda j, i: (rmap(i), j + off))
    pspec = lambda off: pl.BlockSpec((SUBLANES, tc), lambda j, i: (jnp.maximum(rmap(i) * rows_per - 1, 0), j + off))
    wspec = lambda off, r: pl.BlockSpec((r, tc), lambda j, i: (0, j + off))
    out_w = lambda r: pl.BlockSpec((r, tc), lambda j, i: (0, j))
    dzg, dzv, dwg, dwv, dbg, dbv = pl.pallas_call(
        body, grid=(nb, nt),
        in_specs=[zspec(0), zspec(nb), pspec(0), pspec(nb), pl.BlockSpec((tl, tc), lambda j, i: (rmap(i), j)),
                  wspec(0, 3), wspec(nb, 3), wspec(0, 1), wspec(nb, 1)],
        out_specs=[pl.BlockSpec((tl, tc), lambda j, i: (rmap(i), j))] * 2 + [out_w(3), out_w(3), out_w(1), out_w(1)],
        out_shape=[SDS((length, D_FF), BF16)] * 2 + [SDS((3, D_FF), F32)] * 2 + [SDS((1, D_FF), F32)] * 2,
        scratch_shapes=[pltpu.VMEM((SUBLANES, tc), F32)] * 2,
        compiler_params=_params(("arbitrary", "arbitrary")), name="conv_bwd")(z, z, z, z, dact, conv_w, conv_w, conv_b, conv_b)
    return dzg, dzv, jnp.concatenate([dwg, dwv], axis=1), jnp.concatenate([dbg, dbv], axis=1)


def _s5_prep_math(a_re, a_im, ls, b_re, b_im):
    dt = jnp.exp(ls)
    er = jnp.exp(a_re * dt)
    ph = a_im * dt
    abr, abi = er * jnp.cos(ph), er * jnp.sin(ph)
    den = a_re * a_re + a_im * a_im
    nr = abr - 1.0
    cr = (nr * a_re + abi * a_im) / den
    ci = (abi * a_re - nr * a_im) / den
    return abr, abi, cr * b_re - ci * b_im, cr * b_im + ci * b_re


def _s5_prep_fwd(a_re, a_im, ls, b_re, b_im):
    def body(ar, ai, l, br, bi, o1, o2, o3, o4):
        for ref, val in zip((o1, o2, o3, o4), _s5_prep_math(ar[...], ai[...], l[...], br[...], bi[...]), strict=True):
            ref[...] = val
    return pl.pallas_call(body, out_shape=[SDS((1, S5N), F32)] * 2 + [SDS((S5C, S5N), F32)] * 2,
                          name="s5_prep_fwd")(a_re, a_im, ls, b_re, b_im)


def _s5_prep_bwd(a_re, a_im, ls, b_re, b_im, cots, gsel):
    def body(ar, ai, l, br, bi, c1, c2, c3, c4, g_ref, o1, o2, o3, o4, o5):
        _, vjp = jax.vjp(_s5_prep_math, ar[...], ai[...], l[...], br[...], bi[...])
        d_ar, d_ai, d_ls, d_br, d_bi = vjp((c1[...], c2[...], c3[...], c4[...]))
        o1[...] = d_ar
        o2[...] = d_ai
        o3[...] = _hdot(jnp.broadcast_to(d_ls, (SUBLANES, S5N)), g_ref[...])
        o4[...] = d_br
        o5[...] = d_bi
    return pl.pallas_call(body, out_shape=[SDS((1, S5N), F32)] * 2 + [SDS((SUBLANES, LANES), F32)] + [SDS((S5C, S5N), F32)] * 2,
                          name="s5_prep_bwd")(a_re, a_im, ls, b_re, b_im, *cots, gsel)


def _cmul(ar, ai, br, bi):
    return ar * br - ai * bi, ar * bi + ai * br


def _s5_powers(abr, abi):
    shp = (SUBLANES, S5N)
    a1 = (jnp.broadcast_to(abr, shp), jnp.broadcast_to(abi, shp))
    a2 = _cmul(*a1, *a1)
    a4 = _cmul(*a2, *a2)
    row = lax.broadcasted_iota(jnp.int32, shp, 0)
    pr, pi = a1
    cur = a1
    for i in range(1, SUBLANES):
        cur = _cmul(*cur, *a1)
        pr = jnp.where(row == i, cur[0], pr)
        pi = jnp.where(row == i, cur[1], pi)
    return a1, a2, a4, (pr, pi)


def _s5_scan(sre_ref, sim_ref, car_re, car_im, abr, abi, n_rows, reverse):
    a1, a2, a4, (pr, pi) = _s5_powers(abr, abi)
    sgn = -1.0 if reverse else 1.0
    row = lax.broadcasted_iota(jnp.int32, (SUBLANES, S5N), 0)
    if reverse:
        qr, qi = pr, pi
        for i in range(SUBLANES):
            src = SUBLANES - 1 - i
            qr = jnp.where(row == i, jnp.broadcast_to(pr[src:src + 1, :], pr.shape), qr)
            qi = jnp.where(row == i, jnp.broadcast_to(pi[src:src + 1, :], pi.shape), qi)
        pr, pi = qr, qi
    nblk = n_rows // SUBLANES

    def blk(i, carry):
        cr, ci = carry
        b = (nblk - 1 - i) if reverse else i
        sl = pl.ds(pl.multiple_of(b * SUBLANES, SUBLANES), SUBLANES)
        xr, xi = sre_ref[sl, :], sim_ref[sl, :]
        for kk, (er, ei) in ((1, a1), (2, a2), (4, a4)):
            if reverse:
                sr = jnp.where(row < SUBLANES - kk, pltpu.roll(xr, SUBLANES - kk, 0), 0.0)
                si = jnp.where(row < SUBLANES - kk, pltpu.roll(xi, SUBLANES - kk, 0), 0.0)
            else:
                sr = jnp.where(row >= kk, pltpu.roll(xr, kk, 0), 0.0)
                si = jnp.where(row >= kk, pltpu.roll(xi, kk, 0), 0.0)
            dr, di = _cmul(er, sgn * ei, sr, si)
            xr, xi = xr + dr, xi + di
        dr, di = _cmul(pr, sgn * pi, cr, ci)
        xr, xi = xr + dr, xi + di
        sre_ref[sl, :] = xr
        sim_ref[sl, :] = xi
        edge = 0 if reverse else SUBLANES - 1
        return (jnp.broadcast_to(xr[edge:edge + 1, :], xr.shape), jnp.broadcast_to(xi[edge:edge + 1, :], xi.shape))

    cr, ci = lax.fori_loop(0, nblk, blk, (car_re[...], car_im[...]))
    car_re[...] = cr
    car_im[...] = ci


S5_BLK = 4
S5_BN = S5N // S5_BLK


def _split2(x):
    hi = x.astype(BF16)
    return hi, (x - hi.astype(F32)).astype(BF16)


def _dot3(a, b, ca, cb):
    ah, al = _split2(a)
    bh, bl = _split2(b)
    return lax.dot_general(jnp.concatenate([ah, ah, al], axis=ca), jnp.concatenate([bh, bl, bh], axis=cb),
                           (((ca,), (cb,)), ((), ())), preferred_element_type=F32)


def _dot1(a, b, ca, cb):
    return lax.dot_general(a.astype(BF16), b.astype(BF16), (((ca,), (cb,)), ((), ())), preferred_element_type=F32)


def _s5_project_in(u, bblk_ref, sre_ref, sim_ref):
    for j in range(S5_BLK):
        bu = _dot1(u[:, j * LANES:(j + 1) * LANES], bblk_ref[j], 1, 0)
        sre_ref[:, j * S5_BN:(j + 1) * S5_BN] = bu[:, :S5_BN]
        sim_ref[:, j * S5_BN:(j + 1) * S5_BN] = bu[:, S5_BN:]


def _s5_project_out(u, d, cblk_ref, sre_ref, sim_ref):
    ys = []
    for j in range(S5_BLK):
        sl = slice(j * S5_BN, (j + 1) * S5_BN)
        ys.append(_dot1(sre_ref[:, sl], cblk_ref[j, :S5_BN, :], 1, 0) + _dot1(sim_ref[:, sl], cblk_ref[j, S5_BN:, :], 1, 0))
    return jnp.concatenate(ys, axis=1) + d * u


def _s5_fwd(u, bblk, cblk, abr, abi, d):
    length = u.shape[0]
    tl = min(256, length)
    nt = length // tl

    def body(u_ref, b_ref, c_ref, ar_ref, ai_ref, d_ref, ys_ref, kr_ref, ki_ref, sre, sim, car_re, car_im):
        @pl.when(pl.program_id(0) == 0)
        def _():
            car_re[...] = jnp.zeros_like(car_re)
            car_im[...] = jnp.zeros_like(car_im)

        kr_ref[0] = car_re[...]
        ki_ref[0] = car_im[...]
        u_ = u_ref[...]
        _s5_project_in(u_, b_ref, sre, sim)
        _s5_scan(sre, sim, car_re, car_im, ar_ref[...], ai_ref[...], tl, False)
        ys_ref[...] = _gelu(_s5_project_out(u_, d_ref[...], c_ref, sre, sim))

    full = lambda a: pl.BlockSpec(a.shape, lambda i, nd=a.ndim: (0,) * nd)
    chk = pl.BlockSpec((1, SUBLANES, S5N), lambda i: (i, 0, 0))
    return pl.pallas_call(
        body, grid=(nt,), in_specs=[pl.BlockSpec((tl, S5W), lambda i: (i, 0)), full(bblk), full(cblk), full(abr), full(abi), full(d)],
        out_specs=[pl.BlockSpec((tl, S5W), lambda i: (i, 0)), chk, chk],
        out_shape=[SDS((length, S5W), F32), SDS((nt, SUBLANES, S5N), F32), SDS((nt, SUBLANES, S5N), F32)],
        scratch_shapes=[pltpu.VMEM((tl, S5N), F32)] * 2 + [pltpu.VMEM((SUBLANES, S5N), F32)] * 2,
        compiler_params=_params(), name="s5_fwd")(u, bblk, cblk, abr, abi, d)


def _s5_bwd(u, dys_a, dys_b, kre, kim, bblk, cblk, abr, abi, d):
    length = u.shape[0]
    tl = min(256, length)
    nt = length // tl

    def body(u_ref, da_ref, db_ref, kr_ref, ki_ref, b_ref, c_ref, ar_ref, ai_ref, d_ref,
             du_ref, dB_ref, dC_ref, dar_ref, dai_ref, dd_ref, sre, sim, gre, gim, car_re, car_im, dcar_re, dcar_im):
        @pl.when(pl.program_id(0) == 0)
        def _():
            dcar_re[...] = jnp.zeros_like(dcar_re)
            dcar_im[...] = jnp.zeros_like(dcar_im)
            for ref in (dB_ref, dC_ref, dar_ref, dai_ref, dd_ref):
                ref[...] = jnp.zeros_like(ref)

        u_ = u_ref[...]
        abr_, abi_, d_ = ar_ref[...], ai_ref[...], d_ref[...]
        car_re[...] = kr_ref[0]
        car_im[...] = ki_ref[0]
        _s5_project_in(u_, b_ref, sre, sim)
        _s5_scan(sre, sim, car_re, car_im, abr_, abi_, tl, False)
        y = _s5_project_out(u_, d_, c_ref, sre, sim)
        _, vjp = jax.vjp(_gelu, y)
        dy = vjp(da_ref[...] + db_ref[...])[0]
        dd_ref[...] += jnp.sum(dy * u_, axis=0, keepdims=True)
        for j in range(S5_BLK):
            sl = slice(j * S5_BN, (j + 1) * S5_BN)
            dyj = dy[:, j * LANES:(j + 1) * LANES]
            gre[:, sl] = _dot1(dyj, c_ref[j, :S5_BN, :], 1, 1)
            gim[:, sl] = _dot1(dyj, c_ref[j, S5_BN:, :], 1, 1)
            dC_ref[j, :S5_BN, :] += _dot1(sre[:, sl], dyj, 0, 0)
            dC_ref[j, S5_BN:, :] += _dot1(sim[:, sl], dyj, 0, 0)
        _s5_scan(gre, gim, dcar_re, dcar_im, abr_, abi_, tl, True)
        gr, gi = gre[...], gim[...]
        pr = _shift_down(sre[...], kr_ref[0, 0:1, :])
        pi = _shift_down(sim[...], ki_ref[0, 0:1, :])
        dar_ref[...] += jnp.sum(gr * pr + gi * pi, axis=0, keepdims=True)
        dai_ref[...] += jnp.sum(gi * pr - gr * pi, axis=0, keepdims=True)
        dus = []
        for j in range(S5_BLK):
            sl = slice(j * S5_BN, (j + 1) * S5_BN)
            uj = u_[:, j * LANES:(j + 1) * LANES]
            dus.append(_dot1(gre[:, sl], b_ref[j, :, :S5_BN], 1, 1) + _dot1(gim[:, sl], b_ref[j, :, S5_BN:], 1, 1))
            dB_ref[j, :, :S5_BN] += _dot1(uj, gre[:, sl], 0, 0)
            dB_ref[j, :, S5_BN:] += _dot1(uj, gim[:, sl], 0, 0)
        du_ref[...] = (jnp.concatenate(dus, axis=1) + d_ * dy).astype(du_ref.dtype)

    rmap = lambda i: nt - 1 - i
    full = lambda a: pl.BlockSpec(a.shape, lambda i, nd=a.ndim: (0,) * nd)
    row = pl.BlockSpec((tl, S5W), lambda i: (rmap(i), 0))
    chk = pl.BlockSpec((1, SUBLANES, S5N), lambda i: (rmap(i), 0, 0))
    return pl.pallas_call(
        body, grid=(nt,), in_specs=[row, row, row, chk, chk, full(bblk), full(cblk), full(abr), full(abi), full(d)],
        out_specs=[row, full(bblk), full(cblk), full(abr), full(abi), full(d)],
        out_shape=[SDS((length, S5W), BF16), SDS(bblk.shape, F32), SDS(cblk.shape, F32), SDS(abr.shape, F32),
                   SDS(abi.shape, F32), SDS(d.shape, F32)],
        scratch_shapes=[pltpu.VMEM((tl, S5N), F32)] * 4 + [pltpu.VMEM((SUBLANES, S5N), F32)] * 4,
        compiler_params=_params(), name="s5_bwd")(u, dys_a, dys_b, kre, kim, bblk, cblk, abr, abi, d)


def _s5_blockdiag_in(bb):
    t = bb.reshape(S5C, S5_BLK, S5_BN).transpose(1, 0, 2)
    t = jnp.tile(t, (1, SUBLANES, 1))
    mask = (lax.broadcasted_iota(jnp.int32, (LANES, S5_BN), 0) // S5C) == (lax.broadcasted_iota(jnp.int32, (LANES, S5_BN), 1) // S5P)
    return jnp.where(mask[None], t, 0.0)


def _s5_blockdiag_in_t(dblk):
    t = dblk.reshape(S5_BLK, SUBLANES, S5C, SUBLANES, S5P)
    t = jnp.diagonal(t, axis1=1, axis2=3)
    return t.transpose(1, 0, 3, 2).reshape(S5C, S5N)


def _s5_blockdiag_out(c):
    t = c.reshape(S5_BLK, SUBLANES, S5C, S5P).transpose(0, 3, 1, 2).reshape(S5_BLK, S5P, LANES)
    t = jnp.tile(t, (1, SUBLANES, 1))
    mask = (lax.broadcasted_iota(jnp.int32, (S5_BN, LANES), 0) // S5P) == (lax.broadcasted_iota(jnp.int32, (S5_BN, LANES), 1) // S5C)
    return jnp.where(mask[None], t, 0.0)


def _s5_blockdiag_out_t(dblk):
    t = dblk.reshape(S5_BLK, SUBLANES, S5P, SUBLANES, S5C)
    t = jnp.diagonal(t, axis1=1, axis2=3)
    return t.transpose(0, 3, 2, 1).reshape(S5G, S5C, S5P)


def _wkv_consts():
    lane = lax.broadcasted_iota(jnp.int32, (4 * HN, LANES), 1)
    row = lax.broadcasted_iota(jnp.int32, (4 * HN, LANES), 0)
    diag = ((lane % HN) == (row % HN)).astype(F32)
    ones = _block_ones(LANES, HN).astype(BF16)
    return lane % HN, diag, jnp.concatenate([ones, ones], axis=0)


def _wkv_split(p):
    hi = p.astype(BF16)
    lo = (p - hi.astype(F32)).astype(BF16)
    return jnp.concatenate([hi, lo], axis=1)


def _wkv_rowseg(p, ones):
    return jnp.dot(_wkv_split(p), ones, preferred_element_type=F32)


def _wkv_full(rows_ref, t):
    q = jnp.broadcast_to(rows_ref[t], (SUBLANES, RW))
    return jnp.concatenate([jnp.tile(q[:, hp * LANES:(hp + 1) * LANES], (SUBLANES, 1)) for hp in range(4)], axis=0)


def _wkv_colsum8(x):
    return jnp.concatenate([x[hp * HN:(hp + 1) * HN].reshape(SUBLANES, SUBLANES, LANES).sum(axis=0) for hp in range(4)], axis=1)


def _wkv_prebroadcast(src_refs, dst_refs, n):
    for s, dref in zip(src_refs, dst_refs, strict=True):
        dref[...] = s[...].reshape(n, 1, RW)


def _wkv_pair_sums(lhs_ref, out_ref):
    n, ns = out_ref.shape[0], out_ref.shape[1]
    ones_pair = _block_ones(2 * LANES, HN).astype(BF16)
    out_ref[...] = jnp.dot(lhs_ref[0:n].reshape(n * ns, 2 * LANES), ones_pair,
                           preferred_element_type=F32).reshape(n, ns, 2 * LANES)


def _wkv_transpose_out(acc, c, diag):
    lane = lax.broadcasted_iota(jnp.int32, (c, LANES), 1)
    outs = []
    for hp in range(4):
        zp = lax.dot_general(acc[hp * HN:(hp + 1) * HN], diag[:HN], (((0,), (0,)), ((), ())), precision=HI,
                             preferred_element_type=F32)
        outs.append(jnp.where(lane < HN, zp[:c], zp[HN:HN + c]))
    return jnp.concatenate(outs, axis=1)


def _wkv_fwd(r, w, k, v, a, b):
    length = r.shape[0]
    c = min(WKV_CHUNK, length)
    nc = length // c
    ns = 4 * HN

    def body(r_ref, w_ref, k_ref, v_ref, a_ref, b_ref, y_ref, sst_ref, sast_ref, vst_ref, fin_ref,
             s_ref, acc_ref, lhs_ref, ysum_ref, rb, wb, kb, vb, ab, bb, wab, beb, kab):
        @pl.when(pl.program_id(0) == 0)
        def _():
            s_ref[...] = jnp.zeros_like(s_ref)

        a_next = pltpu.roll(a_ref[...], c - 1, 0)
        o512 = _block_ones(RW, HN)
        _wkv_prebroadcast((r_ref, w_ref, k_ref, v_ref, a_ref, b_ref, w_ref[...] * a_next,
                           _segsum_impl(b_ref[...] * a_next, o512), _segsum_impl(k_ref[...] * a_next, o512)),
                          (rb, wb, kb, vb, ab, bb, wab, beb, kab), c)
        lane_t, diag, ones = _wkv_consts()

        def fill_v(q, carry):
            for half in range(2):
                lhs_ref[q, :, half * LANES:(half + 1) * LANES] = (_wkv_full(vb, 2 * q + half) * diag).astype(BF16)
            return carry

        lax.fori_loop(0, c // 2, fill_v, 0)
        _wkv_pair_sums(lhs_ref, vst_ref)

        def pair(p, carry):
            t0, t1 = 2 * p, 2 * p + 1
            s0 = s_ref[...]
            both = _wkv_rowseg(jnp.concatenate([s0 * _wkv_full(ab, t0), s0 * _wkv_full(wab, t0)], axis=0), ones)
            v_pair = vst_ref[p]
            sa0, v0 = both[:ns], v_pair[:, :LANES]
            s1 = s0 * _wkv_full(wb, t0) + sa0 * _wkv_full(bb, t0) + v0 * _wkv_full(kb, t0)
            sa1 = both[ns:] + sa0 * _wkv_full(beb, t0) + v0 * _wkv_full(kab, t0)
            s2 = s1 * _wkv_full(wb, t1) + sa1 * _wkv_full(bb, t1) + v_pair[:, LANES:] * _wkv_full(kb, t1)
            sst_ref[t0] = s0
            sast_ref[t0] = sa0
            sst_ref[t1] = s1
            sast_ref[t1] = sa1
            s_ref[...] = s2
            lhs_ref[p, :, 0:LANES] = (s1 * _wkv_full(rb, t0)).astype(BF16)
            lhs_ref[p, :, LANES:2 * LANES] = (s2 * _wkv_full(rb, t1)).astype(BF16)
            return carry

        lax.fori_loop(0, c // 2, pair, 0)
        _wkv_pair_sums(lhs_ref, ysum_ref)
        acc_ref[...] = jnp.zeros_like(acc_ref)

        def gather_y(p, carry):
            both = ysum_ref[p]
            acc = jnp.where(lane_t == 2 * p, both[:, :LANES], acc_ref[...])
            acc_ref[...] = jnp.where(lane_t == 2 * p + 1, both[:, LANES:], acc)
            return carry

        lax.fori_loop(0, c // 2, gather_y, 0)
        y_ref[...] = _wkv_transpose_out(acc_ref[...], c, diag)
        fin_ref[...] = s_ref[...]

    row = pl.BlockSpec((c, RW), lambda i: (i, 0))
    st = pl.BlockSpec((c, ns, LANES), lambda i: (i, 0, 0))
    state = pltpu.VMEM((ns, LANES), F32)
    st_pair = pl.BlockSpec((c // 2, ns, 2 * LANES), lambda i: (i, 0, 0))
    return pl.pallas_call(
        body, grid=(nc,), in_specs=[row] * 6,
        out_specs=[row, st, st, st_pair, pl.BlockSpec((ns, LANES), lambda i: (0, 0))],
        out_shape=[SDS((length, RW), F32)] + [SDS((length, ns, LANES), F32)] * 2 + [SDS((length // 2, ns, 2 * LANES), F32)]
        + [SDS((ns, LANES), F32)],
        scratch_shapes=[state] * 2 + [pltpu.VMEM((c, ns, 2 * LANES), BF16), pltpu.VMEM((c // 2, ns, 2 * LANES), F32)]
        + [pltpu.VMEM((c, 1, RW), F32)] * 9,
        compiler_params=_params(), name="wkv_fwd")(r, w, k, v, a, b)


def _wkv_bwd(r, w, k, a, b, dy, sst, sast, vst, fin):
    length = r.shape[0]
    c = min(WKV_CHUNK, length)
    nc = length // c
    ns = 4 * HN

    def body(r_ref, w_ref, k_ref, a_ref, b_ref, dy_ref, sst_ref, sast_ref, vst_ref, snext_ref, fin_ref,
             dr_ref, dw_ref, dk_ref, da_ref, db_ref, dv_ref,
             ds_ref, cur_ref, acc_ref, dyst, lhs_ref, dvsum_ref, rb, wb, kb, ab, bb, dyb, wbb, alb, rhb,
             p_r, p_w, p_k, p_a, p_b):
        @pl.when(pl.program_id(0) == 0)
        def _():
            ds_ref[...] = jnp.zeros_like(ds_ref)

        b_prev = pltpu.roll(b_ref[...], 1, 0)
        o512 = _block_ones(RW, HN)
        _wkv_prebroadcast((r_ref, w_ref, k_ref, a_ref, b_ref, dy_ref, w_ref[...] * b_prev,
                           _segsum_impl(a_ref[...] * b_prev, o512), _segsum_impl(r_ref[...] * b_ref[...], o512)),
                          (rb, wb, kb, ab, bb, dyb, wbb, alb, rhb), c)
        lane_t, diag, ones = _wkv_consts()

        def fill(q, carry):
            for half in range(2):
                lhs_ref[q, :, half * LANES:(half + 1) * LANES] = (_wkv_full(dyb, 2 * q + half) * diag).astype(BF16)
            return carry

        lax.fori_loop(0, c // 2, fill, 0)
        _wkv_pair_sums(lhs_ref, dyst)
        cur_ref[...] = jnp.where(pl.program_id(0) == 0, fin_ref[...], snext_ref[0])

        def pair(p, carry):
            ta = c - 1 - 2 * p
            tb = ta - 1
            s_a, s_b, s_c = cur_ref[...], sst_ref[ta], sst_ref[tb]
            dy_pair = dyst[c // 2 - 1 - p]
            dy_a, dy_b = dy_pair[:, LANES:], dy_pair[:, :LANES]
            ds_a = ds_ref[...] + dy_a * _wkv_full(rb, ta)
            both = _wkv_rowseg(jnp.concatenate([ds_a * _wkv_full(bb, ta), ds_a * _wkv_full(wbb, ta)], axis=0), ones)
            dsa_a = both[:ns]
            ds_b = ds_a * _wkv_full(wb, ta) + dsa_a * _wkv_full(ab, ta) + dy_b * _wkv_full(rb, tb)
            dsa_b = both[ns:] + dsa_a * _wkv_full(alb, ta) + dy_b * _wkv_full(rhb, tb)
            ds_ref[...] = ds_b * _wkv_full(wb, tb) + dsa_b * _wkv_full(ab, tb)
            cur_ref[...] = s_c
            lhs_ref[p, :, 0:LANES] = (ds_a * _wkv_full(kb, ta)).astype(BF16)
            lhs_ref[p, :, LANES:2 * LANES] = (ds_b * _wkv_full(kb, tb)).astype(BF16)
            v_pair = vst_ref[c // 2 - 1 - p]
            for t, s_t, s_prev, ds, dsa, dyy, vv in ((ta, s_a, s_b, ds_a, dsa_a, dy_a, v_pair[:, LANES:]),
                                                     (tb, s_b, s_c, ds_b, dsa_b, dy_b, v_pair[:, :LANES])):
                p_r[t] = _wkv_colsum8(s_t * dyy)
                p_k[t] = _wkv_colsum8(ds * vv)
                p_b[t] = _wkv_colsum8(ds * sast_ref[t])
                p_w[t] = _wkv_colsum8(ds * s_prev)
                p_a[t] = _wkv_colsum8(s_prev * dsa)
            return carry

        lax.fori_loop(0, c // 2, pair, 0)
        _wkv_pair_sums(lhs_ref, dvsum_ref)
        acc_ref[...] = jnp.zeros_like(acc_ref)

        def gather_dv(p, carry):
            both = dvsum_ref[p]
            acc = jnp.where(lane_t == c - 1 - 2 * p, both[:, :LANES], acc_ref[...])
            acc_ref[...] = jnp.where(lane_t == c - 2 - 2 * p, both[:, LANES:], acc)
            return carry

        lax.fori_loop(0, c // 2, gather_dv, 0)
        sel = (lax.broadcasted_iota(jnp.int32, (c, c * SUBLANES), 1) // SUBLANES
               == lax.broadcasted_iota(jnp.int32, (c, c * SUBLANES), 0)).astype(F32)
        for out_ref, part in ((dr_ref, p_r), (dw_ref, p_w), (dk_ref, p_k), (da_ref, p_a), (db_ref, p_b)):
            out_ref[...] = _dot3(sel, part[...].reshape(c * SUBLANES, RW), 1, 0)
        dv_ref[...] = _wkv_transpose_out(acc_ref[...], c, diag)

    rmap = lambda i: nc - 1 - i
    row = pl.BlockSpec((c, RW), lambda i: (rmap(i), 0))
    st = pl.BlockSpec((c, ns, LANES), lambda i: (rmap(i), 0, 0))
    nxt = pl.BlockSpec((1, ns, LANES), lambda i: (jnp.minimum((rmap(i) + 1) * c, length - 1), 0, 0))
    state = pltpu.VMEM((ns, LANES), F32)
    st_pair = pl.BlockSpec((c // 2, ns, 2 * LANES), lambda i: (rmap(i), 0, 0))
    return pl.pallas_call(
        body, grid=(nc,), in_specs=[row] * 6 + [st, st, st_pair, nxt, pl.BlockSpec((ns, LANES), lambda i: (0, 0))],
        out_specs=[row] * 6, out_shape=[SDS((length, RW), F32)] * 6,
        scratch_shapes=[state] * 3 + [pltpu.VMEM((c // 2, ns, 2 * LANES), F32), pltpu.VMEM((c, ns, 2 * LANES), BF16),
                                       pltpu.VMEM((c // 2, ns, 2 * LANES), F32)]
        + [pltpu.VMEM((c, 1, RW), F32)] * 9 + [pltpu.VMEM((c, SUBLANES, RW), F32)] * 5,
        compiler_params=_params(), name="wkv_bwd")(r, w, k, a, b, dy, sst, sast, vst, sst, fin)


def _my_place():
    return lax.axis_index("x"), lax.axis_index("y"), lax.axis_index("c")


def _peer(x, y, c, k):
    return (x ^ ((k >> 2) & 1), y ^ ((k >> 1) & 1), c ^ (k & 1))


def _all_gather(shard, name, first_half_only=False):
    n_blocks = N_DEV // 2 if first_half_only else N_DEV

    def body(in_ref, out_ref, send_sems, recv_sems, local_sem):
        x, y, c = _my_place()
        me = 4 * x + 2 * y + c
        i_send = (x == 0) if first_half_only else True

        def copy_to(k):
            return pltpu.make_async_remote_copy(src_ref=in_ref, dst_ref=out_ref.at[me], send_sem=send_sems.at[k - 1],
                                                recv_sem=recv_sems.at[k - 1], device_id=_peer(x, y, c, k), device_id_type=MESH)

        when = (lambda cond: pl.when(cond)) if first_half_only else (lambda cond: (lambda f: f()))

        @when(i_send)
        def _():
            pltpu.make_async_copy(in_ref, out_ref.at[me], local_sem).start()
            for k in range(1, N_DEV):
                copy_to(k).start()

        for k in range(1, N_DEV):
            px, py, pc = _peer(x, y, c, k)

            @when(px == 0)
            def _():
                pltpu.make_async_remote_copy(src_ref=in_ref, dst_ref=out_ref.at[4 * px + 2 * py + pc], send_sem=send_sems.at[k - 1],
                                             recv_sem=recv_sems.at[k - 1], device_id=(px, py, pc), device_id_type=MESH).wait_recv()

        @when(i_send)
        def _():
            for k in range(1, N_DEV):
                copy_to(k).wait_send()
            pltpu.make_async_copy(in_ref, out_ref.at[me], local_sem).wait()

    return pl.pallas_call(
        body, out_shape=SDS((n_blocks, *shard.shape), shard.dtype),
        in_specs=[pl.BlockSpec(memory_space=pl.ANY)], out_specs=pl.BlockSpec(memory_space=pl.ANY),
        scratch_shapes=[pltpu.SemaphoreType.DMA((N_DEV - 1,)), pltpu.SemaphoreType.DMA((N_DEV - 1,)), pltpu.SemaphoreType.DMA],
        name=name)(shard)


_HBM_SPEC = pl.BlockSpec(memory_space=pltpu.HBM)
_SEM_SPEC = pl.BlockSpec(memory_space=pltpu.SEMAPHORE)
_DATAFLOW = pltpu.SideEffectType.DATAFLOW_SIDE_EFFECTING


def _exchange_start(srcs, per_peer, name):
    n = len(srcs)
    lands = [jnp.zeros((N_DEV, *s.shape[-2:]), s.dtype) for s in srcs]

    def body(*refs):
        src_refs, land_refs = refs[:n], refs[n:2 * n]
        send_sems, recv_sems, token = refs[2 * n], refs[2 * n + 1], refs[-1]
        x, y, c = _my_place()
        me = 4 * x + 2 * y + c
        for a in range(n):
            for k in range(1, N_DEV):
                px, py, pc = _peer(x, y, c, k)
                mine = src_refs[a].at[4 * px + 2 * py + pc] if per_peer else src_refs[a]
                pltpu.make_async_remote_copy(src_ref=mine, dst_ref=land_refs[a].at[me], send_sem=send_sems.at[7 * a + k - 1],
                                             recv_sem=recv_sems.at[7 * a + k - 1], device_id=(px, py, pc),
                                             device_id_type=MESH).start()
        token[...] = jnp.zeros_like(token)

    n_sem = (N_DEV - 1) * n
    res = pl.pallas_call(
        body, name=name,
        out_shape=(pltpu.SemaphoreType.DMA((n_sem,)), pltpu.SemaphoreType.DMA((n_sem,)),
                   *[pltpu.HBM(a.shape, a.dtype) for a in srcs + lands], SDS((SUBLANES, LANES), F32)),
        in_specs=(_HBM_SPEC,) * (2 * n),
        out_specs=(_SEM_SPEC, _SEM_SPEC) + (_HBM_SPEC,) * (2 * n) + (pl.BlockSpec(memory_space=pltpu.VMEM),),
        input_output_aliases={i: 2 + i for i in range(2 * n)}, compiler_params=pltpu.CompilerParams(has_side_effects=_DATAFLOW),
    )(*[pltpu.with_memory_space_constraint(a, pltpu.HBM) for a in srcs + lands])
    return res[0], res[1], list(res[2:2 + n]), list(res[2 + n:2 + 2 * n]), res[-1]


def _exchange_wait(started, after, per_peer, name):
    send_sems, recv_sems, src_thrus, land_thrus, _ = started
    n = len(src_thrus)

    def body(*refs):
        src_refs, land_refs = refs[:n], refs[n:2 * n]
        send_sems, recv_sems = refs[2 * n], refs[2 * n + 1]
        x, y, c = _my_place()
        me = 4 * x + 2 * y + c
        for a in range(n):
            for k in range(1, N_DEV):
                px, py, pc = _peer(x, y, c, k)
                mine = src_refs[a].at[me] if per_peer else src_refs[a]
                copy = pltpu.make_async_remote_copy(src_ref=mine, dst_ref=land_refs[a].at[4 * px + 2 * py + pc],
                                                    send_sem=send_sems.at[7 * a + k - 1], recv_sem=recv_sems.at[7 * a + k - 1],
                                                    device_id=(px, py, pc), device_id_type=MESH)
                copy.wait_send()
                copy.wait_recv()

    res = pl.pallas_call(
        body, name=name, out_shape=tuple(pltpu.HBM(a.shape, a.dtype) for a in src_thrus + land_thrus),
        in_specs=(_HBM_SPEC,) * (2 * n) + (_SEM_SPEC, _SEM_SPEC, pl.BlockSpec(memory_space=pl.ANY)),
        out_specs=(_HBM_SPEC,) * (2 * n), input_output_aliases={i: i for i in range(2 * n)},
        compiler_params=pltpu.CompilerParams(has_side_effects=_DATAFLOW),
    )(*src_thrus, *land_thrus, send_sems, recv_sems, after)
    return list(res[n:])


def _row_tile(rows, cap):
    best = None
    for t in range(2 * SUBLANES, min(rows, cap) + 1, 2 * SUBLANES):
        if rows % t == 0:
            best = t
    return best or rows


def _sum_parts(parts, name, own=None):
    rows, cols = parts.shape[1:]

    def body(*refs):
        p_ref, g_out = refs[0], refs[-1]
        g = p_ref[0].astype(F32)
        for s in range(1, N_DEV):
            g = g + p_ref[s].astype(F32)
        if own is not None:
            g = g + refs[1][...].astype(F32)
        g_out[...] = g

    tr = _row_tile(rows, 384 * 1024 // cols)
    blk = pl.BlockSpec((tr, cols), lambda i: (i, 0))
    extra = [] if own is None else [own]
    return pl.pallas_call(
        body, grid=(rows // tr,),
        in_specs=[pl.BlockSpec((N_DEV, tr, cols), lambda i: (0, i, 0))] + [blk] * len(extra),
        out_specs=blk, out_shape=SDS((rows, cols), F32),
        compiler_params=_params(("parallel",)), name=name)(parts, *extra)


def _adamw(g, wgt, m, v, name):
    rows, cols = wgt.shape

    def body(g_ref, w_ref, m_ref, v_ref, d_out, m_out, v_out):
        g_ = g_ref[...]
        m_new = ADAM_B1 * m_ref[...] + (1.0 - ADAM_B1) * g_
        v_new = ADAM_B2 * v_ref[...] + (1.0 - ADAM_B2) * (g_ * g_)
        m_hat = m_new / (1.0 - ADAM_B1 ** ADAM_STEP)
        v_hat = v_new / (1.0 - ADAM_B2 ** ADAM_STEP)
        d_out[...] = -ADAM_LR * (m_hat / (jnp.sqrt(v_hat) + ADAM_EPS) + ADAM_WD * w_ref[...])
        m_out[...] = m_new
        v_out[...] = v_new

    tr = _row_tile(rows, 512 * 1024 // cols)
    blk = pl.BlockSpec((tr, cols), lambda i: (i, 0))
    return pl.pallas_call(
        body, grid=(rows // tr,), in_specs=[blk] * 4, out_specs=[blk] * 3, out_shape=[SDS((rows, cols), F32)] * 3,
        compiler_params=_params(("parallel",)), name=name)(g, wgt, m, v)


PACK_ALIGN = 2 * SUBLANES * LANES
PACK_ROWS = 512

ROW_SHARDED = ("s5_w_glu", "w_out", "ffn_w_down")
TRANSPOSED = ("w_in", "ffn_w_up", "w_branch_rwkv", "w_branch_s5")
SMALL_SHARDED = ("rwkv_w2", "rwkv_a2", "rwkv_g2", "ffn_conv_w")
SHARDED = ROW_SHARDED + TRANSPOSED + SMALL_SHARDED
F32_GATHER = ("ffn_conv_w",)
REPLICATED =("norm_mix_pre", "norm_mix_post", "norm_ffn_pre", "norm_ffn_post", "b_gate", "rwkv_shift_mu", "rwkv_w0",
              "rwkv_a0", "rwkv_k_k", "rwkv_k_a", "rwkv_r_k", "rwkv_lnx_w", "rwkv_lnx_b", "s5_a_re", "s5_a_im", "s5_b_re",
              "s5_b_im", "s5_c_re", "s5_c_im", "s5_d", "s5_log_step", "s5_b_glu", "ffn_conv_b")
WEIGHTS = ("norm_mix_pre", "norm_mix_post", "norm_ffn_pre", "norm_ffn_post", "w_in", "b_gate", "rwkv_shift_mu", "rwkv_w0",
           "rwkv_w2", "rwkv_a0", "rwkv_a2", "rwkv_g2", "rwkv_k_k", "rwkv_k_a", "rwkv_r_k", "rwkv_lnx_w", "rwkv_lnx_b",
           "s5_a_re", "s5_a_im", "s5_b_re", "s5_b_im", "s5_c_re", "s5_c_im", "s5_d", "s5_log_step", "s5_w_glu", "s5_b_glu",
           "w_branch_rwkv", "w_branch_s5", "w_out", "ffn_w_up", "ffn_conv_w", "ffn_conv_b", "ffn_w_down")


def _pad_flat(a):
    flat = a.reshape(-1)
    pad = (-flat.shape[0]) % PACK_ALIGN
    return jnp.pad(flat, (0, pad)) if pad else flat


def _pad_cols(a, mult):
    pad = (-a.shape[1]) % mult
    return jnp.pad(a, ((0, 0), (0, pad))) if pad else a


def _pack(pieces):
    flat = jnp.concatenate([_pad_flat(p) for p in pieces])
    pad = (-flat.shape[0]) % (PACK_ROWS * LANES)
    return (jnp.pad(flat, (0, pad)) if pad else flat).reshape(-1, LANES)


def _unpack(buf, shapes):
    lead = buf.shape[:-2]
    flat = buf.reshape(*lead, -1)
    out, off = [], 0
    for s in shapes:
        n = math.prod(s)
        out.append(flat[..., off:off + n].reshape(*lead, *s))
        off += n + (-n) % PACK_ALIGN
    return out


def _rows_to_full(stack):
    return stack.reshape(-1, stack.shape[2])


def _cols_to_full(stack):
    return stack.transpose(1, 0, 2).reshape(stack.shape[1], -1)


EARLY_GRADS = ("ffn_w_down", "ffn_w_up", "w_out")
REST_GRADS = ("s5_w_glu", "w_in", "w_branch_rwkv", "w_branch_s5")


def _local_step(x, tgt, wt, get_lowrank, get_late, send_early, send_mid, send_rest):
    wt = dict(wt)
    o512 = _block_ones(RW, HN)
    b_gate = wt["b_gate"]
    b_r, b_s = b_gate[:, :D_MODEL], b_gate[:, D_MODEL:]
    post_prm = [wt["rwkv_lnx_w"], wt["rwkv_lnx_b"], wt["rwkv_r_k"].reshape(1, RW)]
    mu = wt["rwkv_shift_mu"]

    a_re, a_im = wt["s5_a_re"].reshape(1, S5N), wt["s5_a_im"].reshape(1, S5N)
    ls = jnp.repeat(wt["s5_log_step"].reshape(S5G), S5P).reshape(1, S5N)
    b_re_t = wt["s5_b_re"].reshape(S5N, S5C).T
    b_im_t = wt["s5_b_im"].reshape(S5N, S5C).T
    c_re, c_im = wt["s5_c_re"].reshape(S5G, S5C, S5P), wt["s5_c_im"].reshape(S5G, S5C, S5P)
    abr, abi, bbr, bbi = _s5_prep_fwd(a_re, a_im, ls, b_re_t, b_im_t)
    bblk = jnp.concatenate([_s5_blockdiag_in(bbr), _s5_blockdiag_in(bbi)], axis=2)
    cblk = jnp.concatenate([_s5_blockdiag_out(c_re), -_s5_blockdiag_out(c_im)], axis=1)
    s5_d = wt["s5_d"]

    h1 = _rms_fwd(x, wt["norm_mix_pre"], "rms1_fwd")
    p_rw = _mm(h1, wt["w_in_head"][:N_RWKV], tb=True, name="mm_proj_rwkv")
    wt.update(get_lowrank(p_rw))
    zpad = lambda a, lo, n: jnp.pad(a.astype(F32), ((lo, n - lo - a.shape[0]), (0, 0)))
    w2p, a2p, g2p = zpad(wt["rwkv_w2"], 0, 256), zpad(wt["rwkv_a2"], 64, 256), zpad(wt["rwkv_g2"], 128, 256)
    pre_small = [wt["rwkv_w0"], wt["rwkv_a0"], wt["rwkv_k_k"], wt["rwkv_k_a"], w2p, a2p, g2p]
    r, decay, k2, v, aa, bb, g = _rwkv_pre_fwd(p_rw, mu, pre_small, o512)
    y, sst, sast, vst, s_fin = _wkv_fwd(r, decay, k2, v, aa, bb)
    wt.update(get_late(y))
    w_in = wt["w_in"]
    u = _mm(h1, w_in[N_RWKV:N_RWKV + S5W], tb=True, name="mm_proj_s5")
    gpre = _mm(h1, w_in[N_RWKV + S5W:], tb=True, name="mm_proj_gate")
    y_r = _rwkv_post_fwd(y, r, k2, v, g, post_prm, o512)
    o_r = _mm(y_r, wt["w_branch_rwkv"], tb=True, name="mm_branch_rwkv")
    ys, kre, kim = _s5_fwd(u, bblk, cblk, abr, abi, s5_d)
    t_glu = _mm(ys, wt["s5_w_glu"], name="mm_glu")
    out5 = _glu_fwd(ys, t_glu, wt["s5_b_glu"])
    o_s = _mm(out5, wt["w_branch_s5"], tb=True, name="mm_branch_s5")
    mi = _merge_fwd(gpre, o_r, o_s, b_r, b_s)
    mixed = _mm(mi, wt["w_out"], name="mm_out")
    x2, h2 = _mid_fwd(x, mixed, wt["norm_mix_post"], wt["norm_ffn_pre"])
    z = _mm(h2, wt["ffn_w_up"], tb=True, name="mm_up")
    act = _conv_fwd(z, wt["ffn_conv_w"], wt["ffn_conv_b"])
    f = _mm(act, wt["ffn_w_down"], name="mm_down")
    dx3, df, d_norm_ffn_post, loss_part = _final(x2, f, tgt, wt["norm_ffn_post"])

    gr = {"norm_ffn_post": d_norm_ffn_post}
    dact = _mm(df, wt["ffn_w_down"], tb=True, name="mm_down_dx")
    gr["ffn_w_down"] = _mm(act, df, ta=True, out_dtype=BF16, name="mm_down_dw")
    dzg, dzv, gr["ffn_conv_w"], gr["ffn_conv_b"] = _conv_bwd(z, dact, wt["ffn_conv_w"], wt["ffn_conv_b"])
    dz = jnp.concatenate([dzg, dzv], axis=1)
    dh2 = _mm(dz, wt["ffn_w_up"], name="mm_up_dx")
    gr["ffn_w_up"] = _mm(dz, h2, ta=True, out_dtype=BF16, name="mm_up_dw")
    dx2, dmixed, gr["norm_mix_post"], gr["norm_ffn_pre"] = _mid_bwd(x2, mixed, dh2, dx3, wt["norm_mix_post"], wt["norm_ffn_pre"])
    dmi = _mm(dmixed, wt["w_out"], tb=True, name="mm_out_dx")
    gr["w_out"] = _mm(mi, dmixed, ta=True, out_dtype=BF16, name="mm_out_dw")
    token = send_early({n: gr[n] for n in EARLY_GRADS})
    b_r = b_r + jnp.tile(token[0:1, :], (1, D_MODEL // LANES))
    dgp_r, dgp_s, do_r, do_s, db_r, db_s = _merge_bwd(gpre, o_r, o_s, dmi, b_r, b_s)
    gr["b_gate"] = jnp.concatenate([db_r, db_s], axis=1)
    dout5 = _mm(do_s, wt["w_branch_s5"], name="mm_branch_s5_dx")
    gr["w_branch_s5"] = _mm(do_s, out5, ta=True, out_dtype=BF16, name="mm_branch_s5_dw")
    dys_a, dt_glu, gr["s5_b_glu"] = _glu_bwd(ys, t_glu, dout5, wt["s5_b_glu"])
    dys_b = _mm(dt_glu, wt["s5_w_glu"], tb=True, name="mm_glu_dx")
    gr["s5_w_glu"] = _mm(ys, dt_glu, ta=True, out_dtype=BF16, name="mm_glu_dw")
    du, dbblk, dcblk, dabr, dabi, gr["s5_d"] = _s5_bwd(u, dys_a, dys_b, kre, kim, bblk, cblk, abr, abi, s5_d)
    gr["s5_c_re"] = _s5_blockdiag_out_t(dcblk[:, :S5_BN, :]).reshape(wt["s5_c_re"].shape)
    gr["s5_c_im"] = (-_s5_blockdiag_out_t(dcblk[:, S5_BN:, :])).reshape(wt["s5_c_im"].shape)
    dbbr, dbbi = _s5_blockdiag_in_t(dbblk[:, :, :S5_BN]), _s5_blockdiag_in_t(dbblk[:, :, S5_BN:])
    gsel = (lax.broadcasted_iota(jnp.int32, (S5N, LANES), 0) // S5P == lax.broadcasted_iota(jnp.int32, (S5N, LANES), 1)).astype(F32)
    d_are, d_aim, d_ls, d_bre_t, d_bim_t = _s5_prep_bwd(a_re, a_im, ls, b_re_t, b_im_t, (dabr, dabi, dbbr, dbbi), gsel)
    gr["s5_a_re"] = d_are.reshape(wt["s5_a_re"].shape)
    gr["s5_a_im"] = d_aim.reshape(wt["s5_a_im"].shape)
    gr["s5_log_step"] = d_ls[0:1, :S5G]
    gr["s5_b_re"] = d_bre_t.T.reshape(wt["s5_b_re"].shape)
    gr["s5_b_im"] = d_bim_t.T.reshape(wt["s5_b_im"].shape)
    dy_r = _mm(do_r, wt["w_branch_rwkv"], name="mm_branch_rwkv_dx")
    gr["w_branch_rwkv"] = _mm(do_r, y_r, ta=True, out_dtype=BF16, name="mm_branch_rwkv_dw")
    token = send_mid(gr)
    post_prm = [post_prm[0] + jnp.tile(token[0:1, :], (1, RW // LANES))] + post_prm[1:]
    dy, dr1, dk1, dv1, dg, gr["rwkv_lnx_w"], gr["rwkv_lnx_b"], d_rk = _rwkv_post_bwd(y, r, k2, v, g, dy_r, post_prm, o512)
    gr["rwkv_r_k"] = d_rk.reshape(wt["rwkv_r_k"].shape)
    dr2, ddecay, dk2, daa, dbb, dv2 = _wkv_bwd(r, decay, k2, aa, bb, dy, sst, sast, vst, s_fin)
    cots = [dr1, ddecay, dk1, dv1, daa, dbb, dg, dr2, dk2, dv2]
    dp_rw, gr["rwkv_shift_mu"], gr["rwkv_w0"], gr["rwkv_a0"], gr["rwkv_k_k"], gr["rwkv_k_a"], dw2p, da2p, dg2p = \
        _rwkv_pre_bwd(p_rw, cots, mu, pre_small, o512)
    gr["rwkv_w2"], gr["rwkv_a2"], gr["rwkv_g2"] = dw2p[:64], da2p[64:128], dg2p[128:]
    dproj = jnp.concatenate([dp_rw, du, dgp_r, dgp_s], axis=1)
    gr["w_in"] = _mm(dproj, h1, ta=True, out_dtype=BF16, name="mm_proj_dw")
    token = send_rest(gr)
    dh1 = _mm(dproj, w_in + token[0, 0].astype(w_in.dtype), name="mm_proj_dx")
    dx, gr["norm_mix_pre"] = _rms_bwd(x, wt["norm_mix_pre"], dh1, dx2, "rms1_bwd")
    return loss_part[0, 0], dx, gr


def kernel(x, norm_mix_pre, norm_mix_post, norm_ffn_pre, norm_ffn_post, w_in, b_gate, rwkv_shift_mu, rwkv_w0, rwkv_w2, rwkv_a0, rwkv_a2, rwkv_g2, rwkv_k_k, rwkv_k_a, rwkv_r_k, rwkv_lnx_w, rwkv_lnx_b, s5_a_re, s5_a_im, s5_b_re, s5_b_im, s5_c_re, s5_c_im, s5_d, s5_log_step, s5_w_glu, s5_b_glu, w_branch_rwkv, w_branch_s5, w_out, ffn_w_up, ffn_conv_w, ffn_conv_b, ffn_w_down, loss_target, m_norm_mix_pre, m_norm_mix_post, m_norm_ffn_pre, m_norm_ffn_post, m_w_in, m_b_gate, m_rwkv_shift_mu, m_rwkv_w0, m_rwkv_w2, m_rwkv_a0, m_rwkv_a2, m_rwkv_g2, m_rwkv_k_k, m_rwkv_k_a, m_rwkv_r_k, m_rwkv_lnx_w, m_rwkv_lnx_b, m_s5_a_re, m_s5_a_im, m_s5_b_re, m_s5_b_im, m_s5_c_re, m_s5_c_im, m_s5_d, m_s5_log_step, m_s5_w_glu, m_s5_b_glu, m_w_branch_rwkv, m_w_branch_s5, m_w_out, m_ffn_w_up, m_ffn_conv_w, m_ffn_conv_b, m_ffn_w_down, v_norm_mix_pre, v_norm_mix_post, v_norm_ffn_pre, v_norm_ffn_post, v_w_in, v_b_gate, v_rwkv_shift_mu, v_rwkv_w0, v_rwkv_w2, v_rwkv_a0, v_rwkv_a2, v_rwkv_g2, v_rwkv_k_k, v_rwkv_k_a, v_rwkv_r_k, v_rwkv_lnx_w, v_rwkv_lnx_b, v_s5_a_re, v_s5_a_im, v_s5_b_re, v_s5_b_im, v_s5_c_re, v_s5_c_im, v_s5_d, v_s5_log_step, v_s5_w_glu, v_s5_b_glu, v_w_branch_rwkv, v_w_branch_s5, v_w_out, v_ffn_w_up, v_ffn_conv_w, v_ffn_conv_b, v_ffn_w_down):
    args = dict(locals())
    wgt = {n: args[n] for n in WEIGHTS}
    mom = {n: args["m_" + n] for n in WEIGHTS}
    var = {n: args["v_" + n] for n in WEIGHTS}
    me = 4 * lax.axis_index("x") + 2 * lax.axis_index("y") + lax.axis_index("c")
    shard_shapes = {n: wgt[n].shape[1:] for n in SHARDED}
    sent_shapes = {n: (shard_shapes[n][::-1] if n in TRANSPOSED else shard_shapes[n]) for n in SHARDED}
    sent = lambda n, a: a.T if n in TRANSPOSED else a

    unshard = lambda n, blocks: _cols_to_full(blocks) if n in SMALL_SHARDED else _rows_to_full(blocks)

    lowrank = ("rwkv_w2", "rwkv_a2", "rwkv_g2")
    late = tuple(n for n in SHARDED if n not in lowrank and n not in F32_GATHER)
    w_in_mine = sent("w_in", wgt["w_in"][0].astype(BF16))
    head = _all_gather(w_in_mine, "all_gather_head", first_half_only=True)
    lowrank_payload = _pack([wgt[n][0].astype(BF16) for n in lowrank])
    lowrank_started = _exchange_start([lowrank_payload], False, "gather_lowrank_start")
    late_payloads = [w_in_mine if n == "w_in" else sent(n, wgt[n][0].astype(BF16)) for n in late]
    late_payloads.append(_pack([wgt[n][0] for n in F32_GATHER]))
    late_started = _exchange_start(late_payloads, False, "gather_late_start")
    wt = {n: wgt[n] for n in REPLICATED}
    wt["w_in_head"] = _rows_to_full(head)

    def get_lowrank(after):
        land, = _exchange_wait(lowrank_started, after, False, "gather_lowrank_wait")
        blocks = _unpack(lax.dynamic_update_index_in_dim(land, lowrank_payload, me, 0), [sent_shapes[n] for n in lowrank])
        return {n: _cols_to_full(b_) for n, b_ in zip(lowrank, blocks, strict=True)}

    def get_late(after):
        lands = _exchange_wait(late_started, after, False, "gather_late_wait")
        whole_ = [lax.dynamic_update_index_in_dim(land, mine, me, 0) for land, mine in zip(lands, late_payloads, strict=True)]
        out = {n: _rows_to_full(blocks) for n, blocks in zip(late, whole_[:-1], strict=True)}
        taps = _unpack(whole_[-1], [sent_shapes[n] for n in F32_GATHER])
        out.update({n: unshard(n, b_) for n, b_ in zip(F32_GATHER, taps, strict=True)})
        return out

    rest_names = REST_GRADS + lowrank
    whole = tuple(n for n in ("ffn_conv_w",) + REPLICATED if n != "norm_mix_pre")
    whole_mid = tuple(n for n in whole if not n.startswith("rwkv_"))
    whole_late = tuple(n for n in whole if n.startswith("rwkv_"))
    blocks_of = lambda g_: g_.reshape(N_DEV, -1, g_.shape[-1])
    as_sent = lambda n, g_: blocks_of(g_.T if n in lowrank else g_)
    started = {}

    def send_early(grads_):
        started["early"] = _exchange_start([as_sent(n, grads_[n]) for n in EARLY_GRADS], True, "exchange_early_start")
        return started["early"][4]

    def send_whole(key, names):
        def send(grads_):
            started[key + "_mine"] = _pack([grads_[n] for n in names])
            started[key] = _exchange_start([started[key + "_mine"]], False, "gather_" + key + "_start")
            return started[key][4]
        return send

    def send_rest(grads_):
        started["rest"] = _exchange_start([as_sent(n, grads_[n]) for n in rest_names], True, "exchange_rest_start")
        return started["rest"][4] + send_whole("late", whole_late)(grads_)

    loss_part, dx, gr = _local_step(x[0], loss_target[0], wt, get_lowrank, get_late, send_early, send_whole("mid", whole_mid),
                                    send_rest)

    grads, upd = {}, {}
    flat2d = lambda a: a.reshape(a.shape[-2:])

    def finish_large(key, names, after):
        lands = _exchange_wait(started[key], after, True, "exchange_" + key + "_wait")
        for n, land in zip(names, lands, strict=True):
            own = lax.dynamic_index_in_dim(as_sent(n, gr[n]), me, 0, keepdims=False)
            total = _sum_parts(land, "sum_" + n, own=own)
            grads[n] = (total.T if n in TRANSPOSED + lowrank else total).reshape(wgt[n].shape)
            trio = _adamw(flat2d(grads[n]), flat2d(wgt[n]), flat2d(mom[n]), flat2d(var[n]), "adamw_" + n)
            upd[n] = tuple(t.reshape(wgt[n].shape) for t in trio)
        return upd[names[-1]][0]

    got = {}

    def finish_whole(key, names, after):
        land, = _exchange_wait(started[key], after, False, "gather_" + key + "_wait")
        total = _sum_parts(lax.dynamic_update_index_in_dim(land, started[key + "_mine"], me, 0), "sum_" + key + "_grads")
        got.update(zip(names, _unpack(total, [gr[n].shape for n in names]), strict=True))
        return total

    done = finish_large("early", EARLY_GRADS, dx)
    done = finish_whole("mid", whole_mid, done)
    done = finish_large("rest", rest_names, done)
    done = finish_whole("late", whole_late, done)
    last, _ = lax.optimization_barrier((gr["norm_mix_pre"].reshape(SUBLANES, LANES), done))
    got["norm_mix_pre"] = _sum_parts(_all_gather(last, "gather_last_grad"), "sum_last_grad").reshape(gr["norm_mix_pre"].shape)
    packed = tuple(n for n in WEIGHTS if n not in EARLY_GRADS + rest_names)
    for n in packed:
        if n in SMALL_SHARDED:
            cols = shard_shapes[n][1]
            grads[n] = lax.dynamic_slice_in_dim(got[n], me * cols, cols, axis=1).reshape(wgt[n].shape)
        else:
            grads[n] = got[n].reshape(wgt[n].shape)

    pack_local = lambda src: _pack([src[n].reshape(-1) for n in packed])
    outs = _adamw(pack_local(grads), pack_local(wgt), pack_local(mom), pack_local(var), "adamw_small")
    shapes = [wgt[n].shape for n in packed]
    upd.update(zip(packed, zip(*[_unpack(o, shapes) for o in outs], strict=True), strict=True))
    loss = lax.psum(loss_part, AXES)
    return (loss, dx[None], *[grads[n] for n in WEIGHTS], *[upd[n][0] for n in WEIGHTS], *[upd[n][1] for n in WEIGHTS],
            *[upd[n][2] for n in WEIGHTS])
```

```python
import functools
import math

import jax
import jax.numpy as jnp
from jax import lax
from jax.experimental import pallas as pl
from jax.experimental.pallas import tpu as pltpu

F32 = jnp.float32
BF16 = jnp.bfloat16
SDS = jax.ShapeDtypeStruct
HI = lax.Precision.HIGHEST
MESH = pl.DeviceIdType.MESH
AXES = ("x", "y", "c")
N_DEV = 8

D_MODEL = 1024
RW = 512
HN = 64
N_RWKV = 1792
S5W = 512
S5G = 32
S5C = 16
S5P = 64
S5N = S5G * S5P
D_FF = 2816
NORM_EPS = 1e-6
LNX_EPS = 64e-5

ADAM_LR = 0.001
ADAM_B1 = 0.9
ADAM_B2 = 0.999
ADAM_EPS = 1e-08
ADAM_WD = 0.01
ADAM_STEP = 10

LANES = 128
SUBLANES = 8
VMEM_LIMIT = 56 * 1024 * 1024
WKV_CHUNK = 32


def _params(sem=("arbitrary",)):
    return pltpu.CompilerParams(dimension_semantics=sem, vmem_limit_bytes=VMEM_LIMIT)


def _pick(n, cap):
    best = None
    for t in range(LANES, min(n, cap) + 1, LANES):
        if n % t == 0:
            best = t
    return best or n


MM_VMEM_BUDGET = 40 * 1024 * 1024


def _mm_tiles(m, n, k, a_bytes, b_bytes, o_bytes):
    def divisors(d):
        return [t for t in range(LANES, d + 1, LANES) if d % t == 0] or [d]

    tn = _pick(n, 2304)
    best = None
    for tm in divisors(m):
        for tk in divisors(k):
            vmem = 2 * (tm * tk * a_bytes + tk * tn * b_bytes) + tm * tn * 4 + 2 * tm * tn * o_bytes
            if vmem > MM_VMEM_BUDGET:
                continue
            key = ((m // tm) * (n // tn) * (k // tk), -tm)
            if best is None or key < best[0]:
                best = (key, tm, tk)
    assert best is not None, (m, n, k)
    return best[1], tn, best[2]


def _mm(a, b, *, ta=False, tb=False, out_dtype=F32, name):
    m = a.shape[1] if ta else a.shape[0]
    k = a.shape[0] if ta else a.shape[1]
    n = b.shape[0] if tb else b.shape[1]
    assert (b.shape[1] if tb else b.shape[0]) == k
    tm, tn, tk = _mm_tiles(m, n, k, a.dtype.itemsize, b.dtype.itemsize, jnp.dtype(out_dtype).itemsize)
    nk = k // tk
    dims = (((0 if ta else 1,), (1 if tb else 0,)), ((), ()))

    def body(a_ref, b_ref, o_ref, acc_ref):
        kk = pl.program_id(2)

        @pl.when(kk == 0)
        def _():
            acc_ref[...] = jnp.zeros_like(acc_ref)

        acc_ref[...] += lax.dot_general(a_ref[...].astype(BF16), b_ref[...].astype(BF16), dims,
                                        preferred_element_type=F32)

        @pl.when(kk == nk - 1)
        def _():
            o_ref[...] = acc_ref[...].astype(o_ref.dtype)

    a_spec = pl.BlockSpec((tk, tm), lambda i, j, kk: (kk, i)) if ta else pl.BlockSpec((tm, tk), lambda i, j, kk: (i, kk))
    b_spec = pl.BlockSpec((tn, tk), lambda i, j, kk: (j, kk)) if tb else pl.BlockSpec((tk, tn), lambda i, j, kk: (kk, j))
    return pl.pallas_call(
        body, grid=(m // tm, n // tn, nk), in_specs=[a_spec, b_spec],
        out_specs=pl.BlockSpec((tm, tn), lambda i, j, kk: (i, j)),
        out_shape=SDS((m, n), out_dtype), scratch_shapes=[pltpu.VMEM((tm, tn), F32)],
        compiler_params=_params(("parallel", "parallel", "arbitrary")), name=name)(a, b)


def _rows(fn, rows, params, out_rows, out_accs, *, name, tl, reverse=False, scratch=()):
    first = rows[0][0] if isinstance(rows[0], tuple) else rows[0]
    length = first.shape[0]
    tl = min(tl, length)
    nt = length // tl
    rmap = (lambda i: nt - 1 - i) if reverse else (lambda i: i)
    specs, arrs = [], []
    for r in rows:
        arr, wdt, cb = r if isinstance(r, tuple) else (r, r.shape[1], 0)
        specs.append(pl.BlockSpec((tl, wdt), lambda i, cb=cb: (rmap(i), cb)))
        arrs.append(arr)
    for p in params:
        specs.append(pl.BlockSpec(p.shape, lambda i, nd=p.ndim: (0,) * nd))
        arrs.append(p)
    out_shape = [SDS((length, c), dt) for c, dt in out_rows] + [SDS(s, F32) for s in out_accs]
    out_specs = [pl.BlockSpec((tl, c), lambda i: (rmap(i), 0)) for c, _ in out_rows]
    out_specs += [pl.BlockSpec(s, lambda i, nd=len(s): (0,) * nd) for s in out_accs]
    nr, npar, nor, noa = len(rows), len(params), len(out_rows), len(out_accs)

    def body(*refs):
        rin, pin = refs[:nr], refs[nr:nr + npar]
        rout = refs[nr + npar:nr + npar + nor]
        aout = refs[nr + npar + nor:nr + npar + nor + noa]
        scr = refs[nr + npar + nor + noa:]
        step = pl.program_id(0)
        outs_r, outs_a = fn(step, [r[...] for r in rin], [p[...] for p in pin], scr)
        for ref, val in zip(rout, outs_r, strict=True):
            ref[...] = val.astype(ref.dtype)

        @pl.when(step == 0)
        def _():
            for ref in aout:
                ref[...] = jnp.zeros_like(ref)

        for ref, val in zip(aout, outs_a, strict=True):
            ref[...] += val.astype(F32)

    res = pl.pallas_call(body, grid=(nt,), in_specs=specs, out_specs=out_specs, out_shape=out_shape,
                         scratch_shapes=list(scratch), compiler_params=_params(), name=name)(*arrs)
    return list(res)


def _rms(x, g):
    return x * lax.rsqrt(jnp.mean(x * x, axis=-1, keepdims=True) + NORM_EPS) * g


def _sig(x):
    return 0.5 * (jnp.tanh(0.5 * x) + 1.0)


def _softplus(x):
    return jnp.maximum(x, 0.0) + jnp.log(1.0 + jnp.exp(-jnp.abs(x)))


def _gelu(x):
    return x * (0.5 * (1.0 + jnp.tanh(math.sqrt(2.0 / math.pi) * (x + 0.044715 * (x * x * x)))))


def _bdot(a, b):
    return jnp.dot(a.astype(BF16), b.astype(BF16), preferred_element_type=F32)


def _hdot(a, b):
    return jnp.dot(a, b, precision=HI, preferred_element_type=F32)


def _segsum_impl(x, ones):
    hi = x.astype(BF16)
    lo = (x - hi.astype(F32)).astype(BF16)
    ones = ones.astype(BF16)
    return jnp.dot(jnp.concatenate([hi, lo], axis=1), jnp.concatenate([ones, ones], axis=0), preferred_element_type=F32)


@jax.custom_vjp
def _segsum(x, ones):
    return _segsum_impl(x, ones)


_segsum.defvjp(lambda x, ones: (_segsum_impl(x, ones), ones),
               lambda ones, g: (_segsum_impl(g, ones), jnp.zeros_like(ones)))


def _block_ones(n, blk):
    i = lax.broadcasted_iota(jnp.int32, (n, n), 0) // blk
    j = lax.broadcasted_iota(jnp.int32, (n, n), 1) // blk
    return (i == j).astype(F32)


def _rms_fwd(x, g, name):
    return _rows(lambda s, r, p, _: ([_rms(r[0], p[0])], []), [x], [g], [(x.shape[1], BF16)], [], name=name, tl=512)[0]


def _rms_bwd(x, g, dh, dres, name):
    def fn(s, r, p, _):
        _, vjp = jax.vjp(_rms, r[0], p[0])
        dx, dg = vjp(r[1])
        return [dx + r[2]], [dg]
    return _rows(fn, [x, dh, dres], [g], [(x.shape[1], F32)], [g.shape], name=name, tl=256)


def _rwkv_pre_math(k_, lr, w0, a0, k_k, k_a, w2p, a2p, g2p, o512):
    pre_w = w0 + _bdot(jnp.tanh(lr), w2p)
    w = -_softplus(-pre_w) - 0.5
    decay = jnp.exp(-jnp.exp(w))
    a = _sig(a0 + _bdot(lr, a2p))
    g = _bdot(_sig(lr), g2p)
    kr = k_ * k_k
    kk = kr / jnp.maximum(jnp.sqrt(_segsum(kr * kr, o512)), 1e-12)
    k2 = k_ * (1.0 + (a - 1.0) * k_a)
    return decay, k2, -kk, kk * a, g


def _shift_down(p, prev_row):
    row = lax.broadcasted_iota(jnp.int32, p.shape, 0)
    return jnp.where(row == 0, jnp.broadcast_to(prev_row, p.shape), pltpu.roll(p, 1, 0))


def _shift_up(q, next_row):
    n = q.shape[0]
    row = lax.broadcasted_iota(jnp.int32, q.shape, 0)
    return jnp.where(row == n - 1, jnp.broadcast_to(next_row, q.shape), pltpu.roll(q, n - 1, 0))


def _rwkv_pre_fwd(p, mu, small, o512):
    def fn(step, r, prm, scr):
        car = scr[0]

        @pl.when(step == 0)
        def _():
            car[...] = jnp.zeros_like(car)

        x = r[0]
        prev = _shift_down(x, car[SUBLANES - 1:SUBLANES, :])
        car[...] = x[x.shape[0] - SUBLANES:, :]
        xs = x + (prev - x) * prm[0]
        decay, k2, aa, bb, g = _rwkv_pre_math(xs[:, RW:2 * RW], xs[:, 3 * RW:], *prm[1:])
        return [xs[:, :RW], decay, k2, xs[:, 2 * RW:3 * RW], aa, bb, g], []
    return _rows(fn, [p], [mu, *small, o512], [(RW, F32)] * 7, [], name="rwkv_pre_fwd", tl=256,
                 scratch=[pltpu.VMEM((SUBLANES, N_RWKV), F32)])


def _rwkv_pre_bwd(p, cots, mu, small, o512):
    length = p.shape[0]
    tl = min(256, length)
    nt = length // tl
    rows_per = tl // SUBLANES
    params = [mu, *small, o512]
    acc_shapes = [mu.shape] + [q.shape for q in small]
    nr, npar, nacc = 2 + len(cots), len(params), len(acc_shapes)

    def body(*refs):
        rin, pin = refs[:nr], refs[nr:nr + npar]
        dp_ref = refs[nr + npar]
        aout = refs[nr + npar + 1:nr + npar + 1 + nacc]
        car_q = refs[nr + npar + 1 + nacc]
        step = pl.program_id(0)

        @pl.when(step == 0)
        def _():
            car_q[...] = jnp.zeros_like(car_q)
            for ref in aout:
                ref[...] = jnp.zeros_like(ref)

        x = rin[0][...]
        prev_row = jnp.where(step == nt - 1, 0.0, rin[1][SUBLANES - 1:SUBLANES, :])
        dr, ddecay, dk2, dv, daa, dbb, dg, dr_b, dk2_b, dv_b = [r[...] for r in rin[2:]]
        dr, dk2, dv = dr + dr_b, dk2 + dk2_b, dv + dv_b
        prm = [q[...] for q in pin]
        mu_, o512_ = prm[0], prm[-1]
        prev = _shift_down(x, prev_row)
        xs = x + (prev - x) * mu_
        _, vjp = jax.vjp(lambda k_, lr, *w: _rwkv_pre_math(k_, lr, *w, o512_), xs[:, RW:2 * RW], xs[:, 3 * RW:], *prm[1:-1])
        dk_, dlr, *dsmall = vjp((ddecay, dk2, daa, dbb, dg))
        dxs = jnp.concatenate([dr, dk_, dv, dlr], axis=1)
        q = dxs * mu_
        dp_ref[...] = (dxs - q + _shift_up(q, car_q[0:1, :])).astype(dp_ref.dtype)
        car_q[...] = q[:SUBLANES, :]
        aout[0][...] += jnp.sum((prev - x) * dxs, axis=0, keepdims=True)
        for ref, val in zip(aout[1:], dsmall, strict=True):
            ref[...] += val

    rmap = lambda i: nt - 1 - i
    specs = [pl.BlockSpec((tl, N_RWKV), lambda i: (rmap(i), 0)),
             pl.BlockSpec((SUBLANES, N_RWKV), lambda i: (jnp.maximum(rmap(i) * rows_per - 1, 0), 0))]
    specs += [pl.BlockSpec((tl, RW), lambda i: (rmap(i), 0)) for _ in cots]
    specs += [pl.BlockSpec(q.shape, lambda i, nd=q.ndim: (0,) * nd) for q in params]
    out_shape = [SDS((length, N_RWKV), BF16)] + [SDS(sh, F32) for sh in acc_shapes]
    out_specs = [pl.BlockSpec((tl, N_RWKV), lambda i: (rmap(i), 0))]
    out_specs += [pl.BlockSpec(sh, lambda i, nd=len(sh): (0,) * nd) for sh in acc_shapes]
    res = pl.pallas_call(body, grid=(nt,), in_specs=specs, out_specs=out_specs, out_shape=out_shape,
                         scratch_shapes=[pltpu.VMEM((SUBLANES, N_RWKV), F32)],
                         compiler_params=_params(), name="rwkv_pre_bwd")(p, p, *cots, *params)
    return list(res)


def _rwkv_post_math(y, r, k2, v, g, lnx_w, lnx_b, r_k, o512):
    mean = _segsum(y, o512) * (1.0 / HN)
    yc = y - mean
    var = _segsum(yc * yc, o512) * (1.0 / HN)
    yn = yc * lax.rsqrt(var + LNX_EPS) * lnx_w + lnx_b
    bonus = _segsum(r * k2 * r_k, o512) * v
    return (yn + bonus) * g


def _rwkv_post_fwd(y, r, k2, v, g, prm, o512):
    return _rows(lambda s, rr, p, _: ([_rwkv_post_math(*rr, *p)], []), [y, r, k2, v, g], [*prm, o512],
                 [(RW, BF16)], [], name="rwkv_post_fwd", tl=256)[0]


def _rwkv_post_bwd(y, r, k2, v, g, dout, prm, o512):
    def fn(s, rr, p, _):
        o = p[-1]
        _, vjp = jax.vjp(lambda *a: _rwkv_post_math(*a, o), *rr[:5], *p[:-1])
        gr = vjp(rr[5])
        return list(gr[:5]), list(gr[5:])
    return _rows(fn, [y, r, k2, v, g, dout], [*prm, o512], [(RW, F32)] * 5, [q.shape for q in prm],
                 name="rwkv_post_bwd", tl=256)


def _glu_math(ys, t, b):
    return ys * _sig(t + b)


def _glu_fwd(ys, t, b):
    return _rows(lambda s, r, p, _: ([_glu_math(r[0], r[1], p[0])], []), [ys, t], [b], [(S5W, BF16)], [],
                 name="s5_glu_fwd", tl=512)[0]


def _glu_bwd(ys, t, dout, b):
    def fn(s, r, p, _):
        _, vjp = jax.vjp(_glu_math, r[0], r[1], p[0])
        dys, dt, db = vjp(r[2])
        return [dys, dt], [db]
    return _rows(fn, [ys, t, dout], [b], [(S5W, F32), (S5W, BF16)], [b.shape], name="s5_glu_bwd", tl=512)


def _merge_math(gp_r, gp_s, o_r, o_s, b_r, b_s):
    return _sig(gp_r + b_r) * o_r + _sig(gp_s + b_s) * o_s


def _merge_fwd(gpre, o_r, o_s, b_r, b_s):
    return _rows(lambda s, r, p, _: ([_merge_math(*r, *p)], []),
                 [(gpre, D_MODEL, 0), (gpre, D_MODEL, 1), o_r, o_s], [b_r, b_s], [(D_MODEL, BF16)], [],
                 name="merge_fwd", tl=256)[0]


def _merge_bwd(gpre, o_r, o_s, dmi, b_r, b_s):
    def fn(s, r, p, _):
        _, vjp = jax.vjp(_merge_math, *r[:4], *p)
        dgr, dgs, dor, dos, dbr, dbs = vjp(r[4])
        return [dgr, dgs, dor, dos], [dbr, dbs]
    return _rows(fn, [(gpre, D_MODEL, 0), (gpre, D_MODEL, 1), o_r, o_s, dmi], [b_r, b_s], [(D_MODEL, BF16)] * 4,
                 [b_r.shape, b_s.shape], name="merge_bwd", tl=256)


def _mid_fwd(x, mixed, g_post, g_pre):
    def fn(s, r, p, _):
        x2 = r[0] + _rms(r[1], p[0])
        return [x2, _rms(x2, p[1])], []
    return _rows(fn, [x, mixed], [g_post, g_pre], [(D_MODEL, F32), (D_MODEL, BF16)], [], name="mid_fwd", tl=256)


def _mid_bwd(x2, mixed, dh2, dx3, g_post, g_pre):
    def fn(s, r, p, _):
        _, vjp1 = jax.vjp(_rms, r[0], p[1])
        dx2, dg_pre = vjp1(r[2])
        dx2 = dx2 + r[3]
        _, vjp2 = jax.vjp(_rms, r[1], p[0])
        dmixed, dg_post = vjp2(dx2)
        return [dx2, dmixed], [dg_post, dg_pre]
    return _rows(fn, [x2, mixed, dh2, dx3], [g_post, g_pre], [(D_MODEL, F32), (D_MODEL, BF16)], [g_post.shape, g_pre.shape],
                 name="mid_bwd", tl=256)


def _final(x2, f, tgt, g_post):
    def fn(s, r, p, _):
        y, vjp = jax.vjp(_rms, r[1], p[0])
        diff = r[0] + y - r[2]
        dx3 = diff * (1.0 / D_MODEL)
        df, dg = vjp(dx3)
        part = 0.5 * jnp.sum(jnp.sum(diff * diff, axis=1, keepdims=True), axis=0, keepdims=True) * (1.0 / D_MODEL)
        return [dx3, df], [dg, jnp.broadcast_to(part, (1, LANES))]
    return _rows(fn, [x2, f, tgt], [g_post], [(D_MODEL, F32), (D_MODEL, BF16)], [g_post.shape, (1, LANES)], name="final", tl=256)


def _conv_taps(z, car):
    row = lax.broadcasted_iota(jnp.int32, z.shape, 0)
    z1 = jnp.where(row == 0, jnp.broadcast_to(car[7:8, :], z.shape), pltpu.roll(z, 1, 0))
    z2 = pltpu.roll(z, 2, 0)
    z2 = jnp.where(row == 0, jnp.broadcast_to(car[6:7, :], z.shape), z2)
    z2 = jnp.where(row == 1, jnp.broadcast_to(car[7:8, :], z.shape), z2)
    return z1, z2


def _conv(z, car, w, b):
    z1, z2 = _conv_taps(z, car)
    return b + w[0:1, :] * z2 + w[1:2, :] * z1 + w[2:3, :] * z, z1, z2


def _conv_fwd(z, conv_w, conv_b):
    length = z.shape[0]
    tl = min(256, length)
    nt = length // tl
    tc = _pick(D_FF, 1536)
    nb = D_FF // tc

    def body(zg_ref, zv_ref, wg_ref, wv_ref, bg_ref, bv_ref, o_ref, cg, cv):
        @pl.when(pl.program_id(1) == 0)
        def _():
            cg[...] = jnp.zeros_like(cg)
            cv[...] = jnp.zeros_like(cv)

        zg, zv = zg_ref[...], zv_ref[...]
        gate, _, _ = _conv(zg, cg[...], wg_ref[...], bg_ref[...])
        val, _, _ = _conv(zv, cv[...], wv_ref[...], bv_ref[...])
        cg[...] = zg[tl - SUBLANES:, :]
        cv[...] = zv[tl - SUBLANES:, :]
        o_ref[...] = (_gelu(gate) * val).astype(o_ref.dtype)

    zspec = lambda off: pl.BlockSpec((tl, tc), lambda j, i: (i, j + off))
    wspec = lambda off, r: pl.BlockSpec((r, tc), lambda j, i: (0, j + off))
    return pl.pallas_call(
        body, grid=(nb, nt),
        in_specs=[zspec(0), zspec(nb), wspec(0, 3), wspec(nb, 3), wspec(0, 1), wspec(nb, 1)],
        out_specs=pl.BlockSpec((tl, tc), lambda j, i: (i, j)), out_shape=SDS((length, D_FF), BF16),
        scratch_shapes=[pltpu.VMEM((SUBLANES, tc), F32)] * 2,
        compiler_params=_params(("arbitrary", "arbitrary")), name="conv_fwd")(z, z, conv_w, conv_w, conv_b, conv_b)


def _conv_bwd(z, dact, conv_w, conv_b):
    length = z.shape[0]
    tl = min(256, length)
    nt = length // tl
    tc = _pick(D_FF, 1536)
    nb = D_FF // tc
    rows_per = tl // SUBLANES

    def half_bwd(dzc, z, z1, z2, w, dcar):
        n = tl
        row = lax.broadcasted_iota(jnp.int32, dzc.shape, 0)
        u1 = jnp.where(row == n - 1, jnp.broadcast_to(dcar[0:1, :], dzc.shape), pltpu.roll(dzc, n - 1, 0))
        u2 = pltpu.roll(dzc, n - 2, 0)
        u2 = jnp.where(row == n - 2, jnp.broadcast_to(dcar[0:1, :], dzc.shape), u2)
        u2 = jnp.where(row == n - 1, jnp.broadcast_to(dcar[1:2, :], dzc.shape), u2)
        dz = w[2:3, :] * dzc + w[1:2, :] * u1 + w[0:1, :] * u2
        dw = jnp.concatenate([jnp.sum(dzc * z2, axis=0, keepdims=True), jnp.sum(dzc * z1, axis=0, keepdims=True),
                              jnp.sum(dzc * z, axis=0, keepdims=True)], axis=0)
        return dz, dw, jnp.sum(dzc, axis=0, keepdims=True)

    def body(zg_ref, zv_ref, pg_ref, pv_ref, da_ref, wg_ref, wv_ref, bg_ref, bv_ref,
             dzg_ref, dzv_ref, dwg_ref, dwv_ref, dbg_ref, dbv_ref, cg, cv):
        step = pl.program_id(1)

        @pl.when(step == 0)
        def _():
            cg[...] = jnp.zeros_like(cg)
            cv[...] = jnp.zeros_like(cv)
            for ref in (dwg_ref, dwv_ref, dbg_ref, dbv_ref):
                ref[...] = jnp.zeros_like(ref)

        is_first_tile = step == nt - 1
        zg, zv = zg_ref[...], zv_ref[...]
        pg = jnp.where(is_first_tile, 0.0, pg_ref[...])
        pv = jnp.where(is_first_tile, 0.0, pv_ref[...])
        wg, wv = wg_ref[...], wv_ref[...]
        gate, zg1, zg2 = _conv(zg, pg, wg, bg_ref[...])
        val, zv1, zv2 = _conv(zv, pv, wv, bv_ref[...])
        act_g, vjp = jax.vjp(_gelu, gate)
        da = da_ref[...]
        dgate = vjp(da * val)[0]
        dval = da * act_g
        dzg, dwg, dbg = half_bwd(dgate, zg, zg1, zg2, wg, cg[...])
        dzv, dwv, dbv = half_bwd(dval, zv, zv1, zv2, wv, cv[...])
        cg[...] = dgate[:SUBLANES, :]
        cv[...] = dval[:SUBLANES, :]
        dzg_ref[...] = dzg.astype(dzg_ref.dtype)
        dzv_ref[...] = dzv.astype(dzv_ref.dtype)
        dwg_ref[...] += dwg
        dwv_ref[...] += dwv
        dbg_ref[...] += dbg
        dbv_ref[...] += dbv

    rmap = lambda i: nt - 1 - i
    zspec = lambda off: pl.BlockSpec((tl, tc), lambda j, i: (rmap(i), j + off))
    pspec = lambda off: pl.BlockSpec((SUBLANES, tc), lambda j, i: (jnp.maximum(rmap(i) * rows_per - 1, 0), j + off))
    wspec = lambda off, r: pl.BlockSpec((r, tc), lambda j, i: (0, j + off))
    out_w = lambda r: pl.BlockSpec((r, tc), lambda j, i: (0, j))
    dzg, dzv, dwg, dwv, dbg, dbv = pl.pallas_call(
        body, grid=(nb, nt),
        in_specs=[zspec(0), zspec(nb), pspec(0), pspec(nb), pl.BlockSpec((tl, tc), lambda j, i: (rmap(i), j)),
                  wspec(0, 3), wspec(nb, 3), wspec(0, 1), wspec(nb, 1)],
        out_specs=[pl.BlockSpec((tl, tc), lambda j, i: (rmap(i), j))] * 2 + [out_w(3), out_w(3), out_w(1), out_w(1)],
        out_shape=[SDS((length, D_FF), BF16)] * 2 + [SDS((3, D_FF), F32)] * 2 + [SDS((1, D_FF), F32)] * 2,
        scratch_shapes=[pltpu.VMEM((SUBLANES, tc), F32)] * 2,
        compiler_params=_params(("arbitrary", "arbitrary")), name="conv_bwd")(z, z, z, z, dact, conv_w, conv_w, conv_b, conv_b)
    return dzg, dzv, jnp.concatenate([dwg, dwv], axis=1), jnp.concatenate([dbg, dbv], axis=1)


def _s5_prep_math(a_re, a_im, ls, b_re, b_im):
    dt = jnp.exp(ls)
    er = jnp.exp(a_re * dt)
    ph = a_im * dt
    abr, abi = er * jnp.cos(ph), er * jnp.sin(ph)
    den = a_re * a_re + a_im * a_im
    nr = abr - 1.0
    cr = (nr * a_re + abi * a_im) / den
    ci = (abi * a_re - nr * a_im) / den
    return abr, abi, cr * b_re - ci * b_im, cr * b_im + ci * b_re


def _s5_prep_fwd(a_re, a_im, ls, b_re, b_im):
    def body(ar, ai, l, br, bi, o1, o2, o3, o4):
        for ref, val in zip((o1, o2, o3, o4), _s5_prep_math(ar[...], ai[...], l[...], br[...], bi[...]), strict=True):
            ref[...] = val
    return pl.pallas_call(body, out_shape=[SDS((1, S5N), F32)] * 2 + [SDS((S5C, S5N), F32)] * 2,
                          name="s5_prep_fwd")(a_re, a_im, ls, b_re, b_im)


def _s5_prep_bwd(a_re, a_im, ls, b_re, b_im, cots, gsel):
    def body(ar, ai, l, br, bi, c1, c2, c3, c4, g_ref, o1, o2, o3, o4, o5):
        _, vjp = jax.vjp(_s5_prep_math, ar[...], ai[...], l[...], br[...], bi[...])
        d_ar, d_ai, d_ls, d_br, d_bi = vjp((c1[...], c2[...], c3[...], c4[...]))
        o1[...] = d_ar
        o2[...] = d_ai
        o3[...] = _hdot(jnp.broadcast_to(d_ls, (SUBLANES, S5N)), g_ref[...])
        o4[...] = d_br
        o5[...] = d_bi
    return pl.pallas_call(body, out_shape=[SDS((1, S5N), F32)] * 2 + [SDS((SUBLANES, LANES), F32)] + [SDS((S5C, S5N), F32)] * 2,
                          name="s5_prep_bwd")(a_re, a_im, ls, b_re, b_im, *cots, gsel)


def _cmul(ar, ai, br, bi):
    return ar * br - ai * bi, ar * bi + ai * br


def _s5_powers(abr, abi):
    shp = (SUBLANES, S5N)
    a1 = (jnp.broadcast_to(abr, shp), jnp.broadcast_to(abi, shp))
    a2 = _cmul(*a1, *a1)
    a4 = _cmul(*a2, *a2)
    row = lax.broadcasted_iota(jnp.int32, shp, 0)
    pr, pi = a1
    cur = a1
    for i in range(1, SUBLANES):
        cur = _cmul(*cur, *a1)
        pr = jnp.where(row == i, cur[0], pr)
        pi = jnp.where(row == i, cur[1], pi)
    return a1, a2, a4, (pr, pi)


def _s5_scan(sre_ref, sim_ref, car_re, car_im, abr, abi, n_rows, reverse):
    a1, a2, a4, (pr, pi) = _s5_powers(abr, abi)
    sgn = -1.0 if reverse else 1.0
    row = lax.broadcasted_iota(jnp.int32, (SUBLANES, S5N), 0)
    if reverse:
        qr, qi = pr, pi
        for i in range(SUBLANES):
            src = SUBLANES - 1 - i
            qr = jnp.where(row == i, jnp.broadcast_to(pr[src:src + 1, :], pr.shape), qr)
            qi = jnp.where(row == i, jnp.broadcast_to(pi[src:src + 1, :], pi.shape), qi)
        pr, pi = qr, qi
    nblk = n_rows // SUBLANES

    def blk(i, carry):
        cr, ci = carry
        b = (nblk - 1 - i) if reverse else i
        sl = pl.ds(pl.multiple_of(b * SUBLANES, SUBLANES), SUBLANES)
        xr, xi = sre_ref[sl, :], sim_ref[sl, :]
        for kk, (er, ei) in ((1, a1), (2, a2), (4, a4)):
            if reverse:
                sr = jnp.where(row < SUBLANES - kk, pltpu.roll(xr, SUBLANES - kk, 0), 0.0)
                si = jnp.where(row < SUBLANES - kk, pltpu.roll(xi, SUBLANES - kk, 0), 0.0)
            else:
                sr = jnp.where(row >= kk, pltpu.roll(xr, kk, 0), 0.0)
                si = jnp.where(row >= kk, pltpu.roll(xi, kk, 0), 0.0)
            dr, di = _cmul(er, sgn * ei, sr, si)
            xr, xi = xr + dr, xi + di
        dr, di = _cmul(pr, sgn * pi, cr, ci)
        xr, xi = xr + dr, xi + di
        sre_ref[sl, :] = xr
        sim_ref[sl, :] = xi
        edge = 0 if reverse else SUBLANES - 1
        return (jnp.broadcast_to(xr[edge:edge + 1, :], xr.shape), jnp.broadcast_to(xi[edge:edge + 1, :], xi.shape))

    cr, ci = lax.fori_loop(0, nblk, blk, (car_re[...], car_im[...]))
    car_re[...] = cr
    car_im[...] = ci


S5_BLK = 4
S5_BN = S5N // S5_BLK


def _split2(x):
    hi = x.astype(BF16)
    return hi, (x - hi.astype(F32)).astype(BF16)


def _dot3(a, b, ca, cb):
    ah, al = _split2(a)
    bh, bl = _split2(b)
    return lax.dot_general(jnp.concatenate([ah, ah, al], axis=ca), jnp.concatenate([bh, bl, bh], axis=cb),
                           (((ca,), (cb,)), ((), ())), preferred_element_type=F32)


def _dot1(a, b, ca, cb):
    return lax.dot_general(a.astype(BF16), b.astype(BF16), (((ca,), (cb,)), ((), ())), preferred_element_type=F32)


def _s5_project_in(u, bblk_ref, sre_ref, sim_ref):
    for j in range(S5_BLK):
        bu = _dot1(u[:, j * LANES:(j + 1) * LANES], bblk_ref[j], 1, 0)
        sre_ref[:, j * S5_BN:(j + 1) * S5_BN] = bu[:, :S5_BN]
        sim_ref[:, j * S5_BN:(j + 1) * S5_BN] = bu[:, S5_BN:]


def _s5_project_out(u, d, cblk_ref, sre_ref, sim_ref):
    ys = []
    for j in range(S5_BLK):
        sl = slice(j * S5_BN, (j + 1) * S5_BN)
        ys.append(_dot1(sre_ref[:, sl], cblk_ref[j, :S5_BN, :], 1, 0) + _dot1(sim_ref[:, sl], cblk_ref[j, S5_BN:, :], 1, 0))
    return jnp.concatenate(ys, axis=1) + d * u


def _s5_fwd(u, bblk, cblk, abr, abi, d):
    length = u.shape[0]
    tl = min(256, length)
    nt = length // tl

    def body(u_ref, b_ref, c_ref, ar_ref, ai_ref, d_ref, ys_ref, kr_ref, ki_ref, sre, sim, car_re, car_im):
        @pl.when(pl.program_id(0) == 0)
        def _():
            car_re[...] = jnp.zeros_like(car_re)
            car_im[...] = jnp.zeros_like(car_im)

        kr_ref[0] = car_re[...]
        ki_ref[0] = car_im[...]
        u_ = u_ref[...]
        _s5_project_in(u_, b_ref, sre, sim)
        _s5_scan(sre, sim, car_re, car_im, ar_ref[...], ai_ref[...], tl, False)
        ys_ref[...] = _gelu(_s5_project_out(u_, d_ref[...], c_ref, sre, sim))

    full = lambda a: pl.BlockSpec(a.shape, lambda i, nd=a.ndim: (0,) * nd)
    chk = pl.BlockSpec((1, SUBLANES, S5N), lambda i: (i, 0, 0))
    return pl.pallas_call(
        body, grid=(nt,), in_specs=[pl.BlockSpec((tl, S5W), lambda i: (i, 0)), full(bblk), full(cblk), full(abr), full(abi), full(d)],
        out_specs=[pl.BlockSpec((tl, S5W), lambda i: (i, 0)), chk, chk],
        out_shape=[SDS((length, S5W), F32), SDS((nt, SUBLANES, S5N), F32), SDS((nt, SUBLANES, S5N), F32)],
        scratch_shapes=[pltpu.VMEM((tl, S5N), F32)] * 2 + [pltpu.VMEM((SUBLANES, S5N), F32)] * 2,
        compiler_params=_params(), name="s5_fwd")(u, bblk, cblk, abr, abi, d)


def _s5_bwd(u, dys_a, dys_b, kre, kim, bblk, cblk, abr, abi, d):
    length = u.shape[0]
    tl = min(256, length)
    nt = length // tl

    def body(u_ref, da_ref, db_ref, kr_ref, ki_ref, b_ref, c_ref, ar_ref, ai_ref, d_ref,
             du_ref, dB_ref, dC_ref, dar_ref, dai_ref, dd_ref, sre, sim, gre, gim, car_re, car_im, dcar_re, dcar_im):
        @pl.when(pl.program_id(0) == 0)
        def _():
            dcar_re[...] = jnp.zeros_like(dcar_re)
            dcar_im[...] = jnp.zeros_like(dcar_im)
            for ref in (dB_ref, dC_ref, dar_ref, dai_ref, dd_ref):
                ref[...] = jnp.zeros_like(ref)

        u_ = u_ref[...]
        abr_, abi_, d_ = ar_ref[...], ai_ref[...], d_ref[...]
        car_re[...] = kr_ref[0]
        car_im[...] = ki_ref[0]
        _s5_project_in(u_, b_ref, sre, sim)
        _s5_scan(sre, sim, car_re, car_im, abr_, abi_, tl, False)
        y = _s5_project_out(u_, d_, c_ref, sre, sim)
        _, vjp = jax.vjp(_gelu, y)
        dy = vjp(da_ref[...] + db_ref[...])[0]
        dd_ref[...] += jnp.sum(dy * u_, axis=0, keepdims=True)
        for j in range(S5_BLK):
            sl = slice(j * S5_BN, (j + 1) * S5_BN)
            dyj = dy[:, j * LANES:(j + 1) * LANES]
            gre[:, sl] = _dot1(dyj, c_ref[j, :S5_BN, :], 1, 1)
            gim[:, sl] = _dot1(dyj, c_ref[j, S5_BN:, :], 1, 1)
            dC_ref[j, :S5_BN, :] += _dot1(sre[:, sl], dyj, 0, 0)
            dC_ref[j, S5_BN:, :] += _dot1(sim[:, sl], dyj, 0, 0)
        _s5_scan(gre, gim, dcar_re, dcar_im, abr_, abi_, tl, True)
        gr, gi = gre[...], gim[...]
        pr = _shift_down(sre[...], kr_ref[0, 0:1, :])
        pi = _shift_down(sim[...], ki_ref[0, 0:1, :])
        dar_ref[...] += jnp.sum(gr * pr + gi * pi, axis=0, keepdims=True)
        dai_ref[...] += jnp.sum(gi * pr - gr * pi, axis=0, keepdims=True)
        dus = []
        for j in range(S5_BLK):
            sl = slice(j * S5_BN, (j + 1) * S5_BN)
            uj = u_[:, j * LANES:(j + 1) * LANES]
            dus.append(_dot1(gre[:, sl], b_ref[j, :, :S5_BN], 1, 1) + _dot1(gim[:, sl], b_ref[j, :, S5_BN:], 1, 1))
            dB_ref[j, :, :S5_BN] += _dot1(uj, gre[:, sl], 0, 0)
            dB_ref[j, :, S5_BN:] += _dot1(uj, gim[:, sl], 0, 0)
        du_ref[...] = (jnp.concatenate(dus, axis=1) + d_ * dy).astype(du_ref.dtype)

    rmap = lambda i: nt - 1 - i
    full = lambda a: pl.BlockSpec(a.shape, lambda i, nd=a.ndim: (0,) * nd)
    row = pl.BlockSpec((tl, S5W), lambda i: (rmap(i), 0))
    chk = pl.BlockSpec((1, SUBLANES, S5N), lambda i: (rmap(i), 0, 0))
    return pl.pallas_call(
        body, grid=(nt,), in_specs=[row, row, row, chk, chk, full(bblk), full(cblk), full(abr), full(abi), full(d)],
        out_specs=[row, full(bblk), full(cblk), full(abr), full(abi), full(d)],
        out_shape=[SDS((length, S5W), BF16), SDS(bblk.shape, F32), SDS(cblk.shape, F32), SDS(abr.shape, F32),
                   SDS(abi.shape, F32), SDS(d.shape, F32)],
        scratch_shapes=[pltpu.VMEM((tl, S5N), F32)] * 4 + [pltpu.VMEM((SUBLANES, S5N), F32)] * 4,
        compiler_params=_params(), name="s5_bwd")(u, dys_a, dys_b, kre, kim, bblk, cblk, abr, abi, d)


def _s5_blockdiag_in(bb):
    t = bb.reshape(S5C, S5_BLK, S5_BN).transpose(1, 0, 2)
    t = jnp.tile(t, (1, SUBLANES, 1))
    mask = (lax.broadcasted_iota(jnp.int32, (LANES, S5_BN), 0) // S5C) == (lax.broadcasted_iota(jnp.int32, (LANES, S5_BN), 1) // S5P)
    return jnp.where(mask[None], t, 0.0)


def _s5_blockdiag_in_t(dblk):
    t = dblk.reshape(S5_BLK, SUBLANES, S5C, SUBLANES, S5P)
    t = jnp.diagonal(t, axis1=1, axis2=3)
    return t.transpose(1, 0, 3, 2).reshape(S5C, S5N)


def _s5_blockdiag_out(c):
    t = c.reshape(S5_BLK, SUBLANES, S5C, S5P).transpose(0, 3, 1, 2).reshape(S5_BLK, S5P, LANES)
    t = jnp.tile(t, (1, SUBLANES, 1))
    mask = (lax.broadcasted_iota(jnp.int32, (S5_BN, LANES), 0) // S5P) == (lax.broadcasted_iota(jnp.int32, (S5_BN, LANES), 1) // S5C)
    return jnp.where(mask[None], t, 0.0)


def _s5_blockdiag_out_t(dblk):
    t = dblk.reshape(S5_BLK, SUBLANES, S5P, SUBLANES, S5C)
    t = jnp.diagonal(t, axis1=1, axis2=3)
    return t.transpose(0, 3, 2, 1).reshape(S5G, S5C, S5P)


def _wkv_consts():
    lane = lax.broadcasted_iota(jnp.int32, (4 * HN, LANES), 1)
    row = lax.broadcasted_iota(jnp.int32, (4 * HN, LANES), 0)
    diag = ((lane % HN) == (row % HN)).astype(F32)
    ones = _block_ones(LANES, HN).astype(BF16)
    return lane % HN, diag, jnp.concatenate([ones, ones], axis=0)


def _wkv_split(p):
    hi = p.astype(BF16)
    lo = (p - hi.astype(F32)).astype(BF16)
    return jnp.concatenate([hi, lo], axis=1)


def _wkv_rowseg(p, ones):
    return jnp.dot(_wkv_split(p), ones, preferred_element_type=F32)


def _wkv_full(rows_ref, t):
    q = jnp.broadcast_to(rows_ref[t], (SUBLANES, RW))
    return jnp.concatenate([jnp.tile(q[:, hp * LANES:(hp + 1) * LANES], (SUBLANES, 1)) for hp in range(4)], axis=0)


def _wkv_colsum8(x):
    return jnp.concatenate([x[hp * HN:(hp + 1) * HN].reshape(SUBLANES, SUBLANES, LANES).sum(axis=0) for hp in range(4)], axis=1)


def _wkv_prebroadcast(src_refs, dst_refs, n):
    for s, dref in zip(src_refs, dst_refs, strict=True):
        dref[...] = s[...].reshape(n, 1, RW)


def _wkv_pair_sums(lhs_ref, out_ref):
    n, ns = out_ref.shape[0], out_ref.shape[1]
    ones_pair = _block_ones(2 * LANES, HN).astype(BF16)
    out_ref[...] = jnp.dot(lhs_ref[0:n].reshape(n * ns, 2 * LANES), ones_pair,
                           preferred_element_type=F32).reshape(n, ns, 2 * LANES)


def _wkv_transpose_out(acc, c, diag):
    lane = lax.broadcasted_iota(jnp.int32, (c, LANES), 1)
    outs = []
    for hp in range(4):
        zp = lax.dot_general(acc[hp * HN:(hp + 1) * HN], diag[:HN], (((0,), (0,)), ((), ())), precision=HI,
                             preferred_element_type=F32)
        outs.append(jnp.where(lane < HN, zp[:c], zp[HN:HN + c]))
    return jnp.concatenate(outs, axis=1)


def _wkv_fwd(r, w, k, v, a, b):
    length = r.shape[0]
    c = min(WKV_CHUNK, length)
    nc = length // c
    ns = 4 * HN

    def body(r_ref, w_ref, k_ref, v_ref, a_ref, b_ref, y_ref, sst_ref, sast_ref, vst_ref, fin_ref,
             s_ref, acc_ref, lhs_ref, ysum_ref, rb, wb, kb, vb, ab, bb, wab, beb, kab):
        @pl.when(pl.program_id(0) == 0)
        def _():
            s_ref[...] = jnp.zeros_like(s_ref)

        a_next = pltpu.roll(a_ref[...], c - 1, 0)
        o512 = _block_ones(RW, HN)
        _wkv_prebroadcast((r_ref, w_ref, k_ref, v_ref, a_ref, b_ref, w_ref[...] * a_next,
                           _segsum_impl(b_ref[...] * a_next, o512), _segsum_impl(k_ref[...] * a_next, o512)),
                          (rb, wb, kb, vb, ab, bb, wab, beb, kab), c)
        lane_t, diag, ones = _wkv_consts()

        def fill_v(q, carry):
            for half in range(2):
                lhs_ref[q, :, half * LANES:(half + 1) * LANES] = (_wkv_full(vb, 2 * q + half) * diag).astype(BF16)
            return carry

        lax.fori_loop(0, c // 2, fill_v, 0)
        _wkv_pair_sums(lhs_ref, vst_ref)

        def pair(p, carry):
            t0, t1 = 2 * p, 2 * p + 1
            s0 = s_ref[...]
            both = _wkv_rowseg(jnp.concatenate([s0 * _wkv_full(ab, t0), s0 * _wkv_full(wab, t0)], axis=0), ones)
            v_pair = vst_ref[p]
            sa0, v0 = both[:ns], v_pair[:, :LANES]
            s1 = s0 * _wkv_full(wb, t0) + sa0 * _wkv_full(bb, t0) + v0 * _wkv_full(kb, t0)
            sa1 = both[ns:] + sa0 * _wkv_full(beb, t0) + v0 * _wkv_full(kab, t0)
            s2 = s1 * _wkv_full(wb, t1) + sa1 * _wkv_full(bb, t1) + v_pair[:, LANES:] * _wkv_full(kb, t1)
            sst_ref[t0] = s0
            sast_ref[t0] = sa0
            sst_ref[t1] = s1
            sast_ref[t1] = sa1
            s_ref[...] = s2
            lhs_ref[p, :, 0:LANES] = (s1 * _wkv_full(rb, t0)).astype(BF16)
            lhs_ref[p, :, LANES:2 * LANES] = (s2 * _wkv_full(rb, t1)).astype(BF16)
            return carry

        lax.fori_loop(0, c // 2, pair, 0)
        _wkv_pair_sums(lhs_ref, ysum_ref)
        acc_ref[...] = jnp.zeros_like(acc_ref)

        def gather_y(p, carry):
            both = ysum_ref[p]
            acc = jnp.where(lane_t == 2 * p, both[:, :LANES], acc_ref[...])
            acc_ref[...] = jnp.where(lane_t == 2 * p + 1, both[:, LANES:], acc)
            return carry

        lax.fori_loop(0, c // 2, gather_y, 0)
        y_ref[...] = _wkv_transpose_out(acc_ref[...], c, diag)
        fin_ref[...] = s_ref[...]

    row = pl.BlockSpec((c, RW), lambda i: (i, 0))
    st = pl.BlockSpec((c, ns, LANES), lambda i: (i, 0, 0))
    state = pltpu.VMEM((ns, LANES), F32)
    st_pair = pl.BlockSpec((c // 2, ns, 2 * LANES), lambda i: (i, 0, 0))
    return pl.pallas_call(
        body, grid=(nc,), in_specs=[row] * 6,
        out_specs=[row, st, st, st_pair, pl.BlockSpec((ns, LANES), lambda i: (0, 0))],
        out_shape=[SDS((length, RW), F32)] + [SDS((length, ns, LANES), F32)] * 2 + [SDS((length // 2, ns, 2 * LANES), F32)]
        + [SDS((ns, LANES), F32)],
        scratch_shapes=[state] * 2 + [pltpu.VMEM((c, ns, 2 * LANES), BF16), pltpu.VMEM((c // 2, ns, 2 * LANES), F32)]
        + [pltpu.VMEM((c, 1, RW), F32)] * 9,
        compiler_params=_params(), name="wkv_fwd")(r, w, k, v, a, b)


def _wkv_bwd(r, w, k, a, b, dy, sst, sast, vst, fin):
    length = r.shape[0]
    c = min(WKV_CHUNK, length)
    nc = length // c
    ns = 4 * HN

    def body(r_ref, w_ref, k_ref, a_ref, b_ref, dy_ref, sst_ref, sast_ref, vst_ref, snext_ref, fin_ref,
             dr_ref, dw_ref, dk_ref, da_ref, db_ref, dv_ref,
             ds_ref, cur_ref, acc_ref, dyst, lhs_ref, dvsum_ref, rb, wb, kb, ab, bb, dyb, wbb, alb, rhb,
             p_r, p_w, p_k, p_a, p_b):
        @pl.when(pl.program_id(0) == 0)
        def _():
            ds_ref[...] = jnp.zeros_like(ds_ref)

        b_prev = pltpu.roll(b_ref[...], 1, 0)
        o512 = _block_ones(RW, HN)
        _wkv_prebroadcast((r_ref, w_ref, k_ref, a_ref, b_ref, dy_ref, w_ref[...] * b_prev,
                           _segsum_impl(a_ref[...] * b_prev, o512), _segsum_impl(r_ref[...] * b_ref[...], o512)),
                          (rb, wb, kb, ab, bb, dyb, wbb, alb, rhb), c)
        lane_t, diag, ones = _wkv_consts()

        def fill(q, carry):
            for half in range(2):
                lhs_ref[q, :, half * LANES:(half + 1) * LANES] = (_wkv_full(dyb, 2 * q + half) * diag).astype(BF16)
            return carry

        lax.fori_loop(0, c // 2, fill, 0)
        _wkv_pair_sums(lhs_ref, dyst)
        cur_ref[...] = jnp.where(pl.program_id(0) == 0, fin_ref[...], snext_ref[0])

        def pair(p, carry):
            ta = c - 1 - 2 * p
            tb = ta - 1
            s_a, s_b, s_c = cur_ref[...], sst_ref[ta], sst_ref[tb]
            dy_pair = dyst[c // 2 - 1 - p]
            dy_a, dy_b = dy_pair[:, LANES:], dy_pair[:, :LANES]
            ds_a = ds_ref[...] + dy_a * _wkv_full(rb, ta)
            both = _wkv_rowseg(jnp.concatenate([ds_a * _wkv_full(bb, ta), ds_a * _wkv_full(wbb, ta)], axis=0), ones)
            dsa_a = both[:ns]
            ds_b = ds_a * _wkv_full(wb, ta) + dsa_a * _wkv_full(ab, ta) + dy_b * _wkv_full(rb, tb)
            dsa_b = both[ns:] + dsa_a * _wkv_full(alb, ta) + dy_b * _wkv_full(rhb, tb)
            ds_ref[...] = ds_b * _wkv_full(wb, tb) + dsa_b * _wkv_full(ab, tb)
            cur_ref[...] = s_c
            lhs_ref[p, :, 0:LANES] = (ds_a * _wkv_full(kb, ta)).astype(BF16)
            lhs_ref[p, :, LANES:2 * LANES] = (ds_b * _wkv_full(kb, tb)).astype(BF16)
            v_pair = vst_ref[c // 2 - 1 - p]
            for t, s_t, s_prev, ds, dsa, dyy, vv in ((ta, s_a, s_b, ds_a, dsa_a, dy_a, v_pair[:, LANES:]),
                                                     (tb, s_b, s_c, ds_b, dsa_b, dy_b, v_pair[:, :LANES])):
                p_r[t] = _wkv_colsum8(s_t * dyy)
                p_k[t] = _wkv_colsum8(ds * vv)
                p_b[t] = _wkv_colsum8(ds * sast_ref[t])
                p_w[t] = _wkv_colsum8(ds * s_prev)
                p_a[t] = _wkv_colsum8(s_prev * dsa)
            return carry

        lax.fori_loop(0, c // 2, pair, 0)
        _wkv_pair_sums(lhs_ref, dvsum_ref)
        acc_ref[...] = jnp.zeros_like(acc_ref)

        def gather_dv(p, carry):
            both = dvsum_ref[p]
            acc = jnp.where(lane_t == c - 1 - 2 * p, both[:, :LANES], acc_ref[...])
            acc_ref[...] = jnp.where(lane_t == c - 2 - 2 * p, both[:, LANES:], acc)
            return carry

        lax.fori_loop(0, c // 2, gather_dv, 0)
        sel = (lax.broadcasted_iota(jnp.int32, (c, c * SUBLANES), 1) // SUBLANES
               == lax.broadcasted_iota(jnp.int32, (c, c * SUBLANES), 0)).astype(F32)
        for out_ref, part in ((dr_ref, p_r), (dw_ref, p_w), (dk_ref, p_k), (da_ref, p_a), (db_ref, p_b)):
            out_ref[...] = _dot3(sel, part[...].reshape(c * SUBLANES, RW), 1, 0)
        dv_ref[...] = _wkv_transpose_out(acc_ref[...], c, diag)

    rmap = lambda i: nc - 1 - i
    row = pl.BlockSpec((c, RW), lambda i: (rmap(i), 0))
    st = pl.BlockSpec((c, ns, LANES), lambda i: (rmap(i), 0, 0))
    nxt = pl.BlockSpec((1, ns, LANES), lambda i: (jnp.minimum((rmap(i) + 1) * c, length - 1), 0, 0))
    state = pltpu.VMEM((ns, LANES), F32)
    st_pair = pl.BlockSpec((c // 2, ns, 2 * LANES), lambda i: (rmap(i), 0, 0))
    return pl.pallas_call(
        body, grid=(nc,), in_specs=[row] * 6 + [st, st, st_pair, nxt, pl.BlockSpec((ns, LANES), lambda i: (0, 0))],
        out_specs=[row] * 6, out_shape=[SDS((length, RW), F32)] * 6,
        scratch_shapes=[state] * 3 + [pltpu.VMEM((c // 2, ns, 2 * LANES), F32), pltpu.VMEM((c, ns, 2 * LANES), BF16),
                                       pltpu.VMEM((c // 2, ns, 2 * LANES), F32)]
        + [pltpu.VMEM((c, 1, RW), F32)] * 9 + [pltpu.VMEM((c, SUBLANES, RW), F32)] * 5,
        compiler_params=_params(), name="wkv_bwd")(r, w, k, a, b, dy, sst, sast, vst, sst, fin)


def _my_place():
    return lax.axis_index("x"), lax.axis_index("y"), lax.axis_index("c")


def _peer(x, y, c, k):
    return (x ^ ((k >> 2) & 1), y ^ ((k >> 1) & 1), c ^ (k & 1))


def _all_gather(shard, name, first_half_only=False):
    n_blocks = N_DEV // 2 if first_half_only else N_DEV

    def body(in_ref, out_ref, send_sems, recv_sems, local_sem):
        x, y, c = _my_place()
        me = 4 * x + 2 * y + c
        i_send = (x == 0) if first_half_only else True

        def copy_to(k):
            return pltpu.make_async_remote_copy(src_ref=in_ref, dst_ref=out_ref.at[me], send_sem=send_sems.at[k - 1],
                                                recv_sem=recv_sems.at[k - 1], device_id=_peer(x, y, c, k), device_id_type=MESH)

        when = (lambda cond: pl.when(cond)) if first_half_only else (lambda cond: (lambda f: f()))

        @when(i_send)
        def _():
            pltpu.make_async_copy(in_ref, out_ref.at[me], local_sem).start()
            for k in range(1, N_DEV):
                copy_to(k).start()

        for k in range(1, N_DEV):
            px, py, pc = _peer(x, y, c, k)

            @when(px == 0)
            def _():
                pltpu.make_async_remote_copy(src_ref=in_ref, dst_ref=out_ref.at[4 * px + 2 * py + pc], send_sem=send_sems.at[k - 1],
                                             recv_sem=recv_sems.at[k - 1], device_id=(px, py, pc), device_id_type=MESH).wait_recv()

        @when(i_send)
        def _():
            for k in range(1, N_DEV):
                copy_to(k).wait_send()
            pltpu.make_async_copy(in_ref, out_ref.at[me], local_sem).wait()

    return pl.pallas_call(
        body, out_shape=SDS((n_blocks, *shard.shape), shard.dtype),
        in_specs=[pl.BlockSpec(memory_space=pl.ANY)], out_specs=pl.BlockSpec(memory_space=pl.ANY),
        scratch_shapes=[pltpu.SemaphoreType.DMA((N_DEV - 1,)), pltpu.SemaphoreType.DMA((N_DEV - 1,)), pltpu.SemaphoreType.DMA],
        name=name)(shard)


_HBM_SPEC = pl.BlockSpec(memory_space=pltpu.HBM)
_SEM_SPEC = pl.BlockSpec(memory_space=pltpu.SEMAPHORE)
_DATAFLOW = pltpu.SideEffectType.DATAFLOW_SIDE_EFFECTING


def _exchange_start(srcs, per_peer, name):
    n = len(srcs)
    lands = [jnp.zeros((N_DEV, *s.shape[-2:]), s.dtype) for s in srcs]

    def body(*refs):
        src_refs, land_refs = refs[:n], refs[n:2 * n]
        send_sems, recv_sems, token = refs[2 * n], refs[2 * n + 1], refs[-1]
        x, y, c = _my_place()
        me = 4 * x + 2 * y + c
        for a in range(n):
            for k in range(1, N_DEV):
                px, py, pc = _peer(x, y, c, k)
                mine = src_refs[a].at[4 * px + 2 * py + pc] if per_peer else src_refs[a]
                pltpu.make_async_remote_copy(src_ref=mine, dst_ref=land_refs[a].at[me], send_sem=send_sems.at[7 * a + k - 1],
                                             recv_sem=recv_sems.at[7 * a + k - 1], device_id=(px, py, pc),
                                             device_id_type=MESH).start()
        token[...] = jnp.zeros_like(token)

    n_sem = (N_DEV - 1) * n
    res = pl.pallas_call(
        body, name=name,
        out_shape=(pltpu.SemaphoreType.DMA((n_sem,)), pltpu.SemaphoreType.DMA((n_sem,)),
                   *[pltpu.HBM(a.shape, a.dtype) for a in srcs + lands], SDS((SUBLANES, LANES), F32)),
        in_specs=(_HBM_SPEC,) * (2 * n),
        out_specs=(_SEM_SPEC, _SEM_SPEC) + (_HBM_SPEC,) * (2 * n) + (pl.BlockSpec(memory_space=pltpu.VMEM),),
        input_output_aliases={i: 2 + i for i in range(2 * n)}, compiler_params=pltpu.CompilerParams(has_side_effects=_DATAFLOW),
    )(*[pltpu.with_memory_space_constraint(a, pltpu.HBM) for a in srcs + lands])
    return res[0], res[1], list(res[2:2 + n]), list(res[2 + n:2 + 2 * n]), res[-1]


def _exchange_wait(started, after, per_peer, name):
    send_sems, recv_sems, src_thrus, land_thrus, _ = started
    n = len(src_thrus)

    def body(*refs):
        src_refs, land_refs = refs[:n], refs[n:2 * n]
        send_sems, recv_sems = refs[2 * n], refs[2 * n + 1]
        x, y, c = _my_place()
        me = 4 * x + 2 * y + c
        for a in range(n):
            for k in range(1, N_DEV):
                px, py, pc = _peer(x, y, c, k)
                mine = src_refs[a].at[me] if per_peer else src_refs[a]
                copy = pltpu.make_async_remote_copy(src_ref=mine, dst_ref=land_refs[a].at[4 * px + 2 * py + pc],
                                                    send_sem=send_sems.at[7 * a + k - 1], recv_sem=recv_sems.at[7 * a + k - 1],
                                                    device_id=(px, py, pc), device_id_type=MESH)
                copy.wait_send()
                copy.wait_recv()

    res = pl.pallas_call(
        body, name=name, out_shape=tuple(pltpu.HBM(a.shape, a.dtype) for a in src_thrus + land_thrus),
        in_specs=(_HBM_SPEC,) * (2 * n) + (_SEM_SPEC, _SEM_SPEC, pl.BlockSpec(memory_space=pl.ANY)),
        out_specs=(_HBM_SPEC,) * (2 * n), input_output_aliases={i: i for i in range(2 * n)},
        compiler_params=pltpu.CompilerParams(has_side_effects=_DATAFLOW),
    )(*src_thrus, *land_thrus, send_sems, recv_sems, after)
    return list(res[n:])


def _row_tile(rows, cap):
    best = None
    for t in range(2 * SUBLANES, min(rows, cap) + 1, 2 * SUBLANES):
        if rows % t == 0:
            best = t
    return best or rows


def _sum_parts(parts, name, own=None):
    rows, cols = parts.shape[1:]

    def body(*refs):
        p_ref, g_out = refs[0], refs[-1]
        g = p_ref[0].astype(F32)
        for s in range(1, N_DEV):
            g = g + p_ref[s].astype(F32)
        if own is not None:
            g = g + refs[1][...].astype(F32)
        g_out[...] = g

    tr = _row_tile(rows, 384 * 1024 // cols)
    blk = pl.BlockSpec((tr, cols), lambda i: (i, 0))
    extra = [] if own is None else [own]
    return pl.pallas_call(
        body, grid=(rows // tr,),
        in_specs=[pl.BlockSpec((N_DEV, tr, cols), lambda i: (0, i, 0))] + [blk] * len(extra),
        out_specs=blk, out_shape=SDS((rows, cols), F32),
        compiler_params=_params(("parallel",)), name=name)(parts, *extra)


def _adamw(g, wgt, m, v, name):
    rows, cols = wgt.shape

    def body(g_ref, w_ref, m_ref, v_ref, d_out, m_out, v_out):
        g_ = g_ref[...]
        m_new = ADAM_B1 * m_ref[...] + (1.0 - ADAM_B1) * g_
        v_new = ADAM_B2 * v_ref[...] + (1.0 - ADAM_B2) * (g_ * g_)
        m_hat = m_new / (1.0 - ADAM_B1 ** ADAM_STEP)
        v_hat = v_new / (1.0 - ADAM_B2 ** ADAM_STEP)
        d_out[...] = -ADAM_LR * (m_hat / (jnp.sqrt(v_hat) + ADAM_EPS) + ADAM_WD * w_ref[...])
        m_out[...] = m_new
        v_out[...] = v_new

    tr = _row_tile(rows, 512 * 1024 // cols)
    blk = pl.BlockSpec((tr, cols), lambda i: (i, 0))
    return pl.pallas_call(
        body, grid=(rows // tr,), in_specs=[blk] * 4, out_specs=[blk] * 3, out_shape=[SDS((rows, cols), F32)] * 3,
        compiler_params=_params(("parallel",)), name=name)(g, wgt, m, v)


PACK_ALIGN = 2 * SUBLANES * LANES
PACK_ROWS = 512

ROW_SHARDED = ("s5_w_glu", "w_out", "ffn_w_down")
TRANSPOSED = ("w_in", "ffn_w_up", "w_branch_rwkv", "w_branch_s5")
SMALL_SHARDED = ("rwkv_w2", "rwkv_a2", "rwkv_g2", "ffn_conv_w")
SHARDED = ROW_SHARDED + TRANSPOSED + SMALL_SHARDED
F32_GATHER = ("ffn_conv_w",)
REPLICATED =("norm_mix_pre", "norm_mix_post", "norm_ffn_pre", "norm_ffn_post", "b_gate", "rwkv_shift_mu", "rwkv_w0",
              "rwkv_a0", "rwkv_k_k", "rwkv_k_a", "rwkv_r_k", "rwkv_lnx_w", "rwkv_lnx_b", "s5_a_re", "s5_a_im", "s5_b_re",
              "s5_b_im", "s5_c_re", "s5_c_im", "s5_d", "s5_log_step", "s5_b_glu", "ffn_conv_b")
WEIGHTS = ("norm_mix_pre", "norm_mix_post", "norm_ffn_pre", "norm_ffn_post", "w_in", "b_gate", "rwkv_shift_mu", "rwkv_w0",
           "rwkv_w2", "rwkv_a0", "rwkv_a2", "rwkv_g2", "rwkv_k_k", "rwkv_k_a", "rwkv_r_k", "rwkv_lnx_w", "rwkv_lnx_b",
           "s5_a_re", "s5_a_im", "s5_b_re", "s5_b_im", "s5_c_re", "s5_c_im", "s5_d", "s5_log_step", "s5_w_glu", "s5_b_glu",
           "w_branch_rwkv", "w_branch_s5", "w_out", "ffn_w_up", "ffn_conv_w", "ffn_conv_b", "ffn_w_down")


def _pad_flat(a):
    flat = a.reshape(-1)
    pad = (-flat.shape[0]) % PACK_ALIGN
    return jnp.pad(flat, (0, pad)) if pad else flat


def _pad_cols(a, mult):
    pad = (-a.shape[1]) % mult
    return jnp.pad(a, ((0, 0), (0, pad))) if pad else a


def _pack(pieces):
    flat = jnp.concatenate([_pad_flat(p) for p in pieces])
    pad = (-flat.shape[0]) % (PACK_ROWS * LANES)
    return (jnp.pad(flat, (0, pad)) if pad else flat).reshape(-1, LANES)


def _unpack(buf, shapes):
    lead = buf.shape[:-2]
    flat = buf.reshape(*lead, -1)
    out, off = [], 0
    for s in shapes:
        n = math.prod(s)
        out.append(flat[..., off:off + n].reshape(*lead, *s))
        off += n + (-n) % PACK_ALIGN
    return out


def _rows_to_full(stack):
    return stack.reshape(-1, stack.shape[2])


def _cols_to_full(stack):
    return stack.transpose(1, 0, 2).reshape(stack.shape[1], -1)


EARLY_GRADS = ("ffn_w_down", "ffn_w_up", "w_out")
REST_GRADS = ("s5_w_glu", "w_in", "w_branch_rwkv", "w_branch_s5")


def _local_step(x, tgt, wt, get_lowrank, get_late, send_early, send_mid, send_rest):
    wt = dict(wt)
    o512 = _block_ones(RW, HN)
    b_gate = wt["b_gate"]
    b_r, b_s = b_gate[:, :D_MODEL], b_gate[:, D_MODEL:]
    post_prm = [wt["rwkv_lnx_w"], wt["rwkv_lnx_b"], wt["rwkv_r_k"].reshape(1, RW)]
    mu = wt["rwkv_shift_mu"]

    a_re, a_im = wt["s5_a_re"].reshape(1, S5N), wt["s5_a_im"].reshape(1, S5N)
    ls = jnp.repeat(wt["s5_log_step"].reshape(S5G), S5P).reshape(1, S5N)
    b_re_t = wt["s5_b_re"].reshape(S5N, S5C).T
    b_im_t = wt["s5_b_im"].reshape(S5N, S5C).T
    c_re, c_im = wt["s5_c_re"].reshape(S5G, S5C, S5P), wt["s5_c_im"].reshape(S5G, S5C, S5P)
    abr, abi, bbr, bbi = _s5_prep_fwd(a_re, a_im, ls, b_re_t, b_im_t)
    bblk = jnp.concatenate([_s5_blockdiag_in(bbr), _s5_blockdiag_in(bbi)], axis=2)
    cblk = jnp.concatenate([_s5_blockdiag_out(c_re), -_s5_blockdiag_out(c_im)], axis=1)
    s5_d = wt["s5_d"]

    h1 = _rms_fwd(x, wt["norm_mix_pre"], "rms1_fwd")
    p_rw = _mm(h1, wt["w_in_head"][:N_RWKV], tb=True, name="mm_proj_rwkv")
    wt.update(get_lowrank(p_rw))
    zpad = lambda a, lo, n: jnp.pad(a.astype(F32), ((lo, n - lo - a.shape[0]), (0, 0)))
    w2p, a2p, g2p = zpad(wt["rwkv_w2"], 0, 256), zpad(wt["rwkv_a2"], 64, 256), zpad(wt["rwkv_g2"], 128, 256)
    pre_small = [wt["rwkv_w0"], wt["rwkv_a0"], wt["rwkv_k_k"], wt["rwkv_k_a"], w2p, a2p, g2p]
    r, decay, k2, v, aa, bb, g = _rwkv_pre_fwd(p_rw, mu, pre_small, o512)
    y, sst, sast, vst, s_fin = _wkv_fwd(r, decay, k2, v, aa, bb)
    wt.update(get_late(y))
    w_in = wt["w_in"]
    u = _mm(h1, w_in[N_RWKV:N_RWKV + S5W], tb=True, name="mm_proj_s5")
    gpre = _mm(h1, w_in[N_RWKV + S5W:], tb=True, name="mm_proj_gate")
    y_r = _rwkv_post_fwd(y, r, k2, v, g, post_prm, o512)
    o_r = _mm(y_r, wt["w_branch_rwkv"], tb=True, name="mm_branch_rwkv")
    ys, kre, kim = _s5_fwd(u, bblk, cblk, abr, abi, s5_d)
    t_glu = _mm(ys, wt["s5_w_glu"], name="mm_glu")
    out5 = _glu_fwd(ys, t_glu, wt["s5_b_glu"])
    o_s = _mm(out5, wt["w_branch_s5"], tb=True, name="mm_branch_s5")
    mi = _merge_fwd(gpre, o_r, o_s, b_r, b_s)
    mixed = _mm(mi, wt["w_out"], name="mm_out")
    x2, h2 = _mid_fwd(x, mixed, wt["norm_mix_post"], wt["norm_ffn_pre"])
    z = _mm(h2, wt["ffn_w_up"], tb=True, name="mm_up")
    act = _conv_fwd(z, wt["ffn_conv_w"], wt["ffn_conv_b"])
    f = _mm(act, wt["ffn_w_down"], name="mm_down")
    dx3, df, d_norm_ffn_post, loss_part = _final(x2, f, tgt, wt["norm_ffn_post"])

    gr = {"norm_ffn_post": d_norm_ffn_post}
    dact = _mm(df, wt["ffn_w_down"], tb=True, name="mm_down_dx")
    gr["ffn_w_down"] = _mm(act, df, ta=True, out_dtype=BF16, name="mm_down_dw")
    dzg, dzv, gr["ffn_conv_w"], gr["ffn_conv_b"] = _conv_bwd(z, dact, wt["ffn_conv_w"], wt["ffn_conv_b"])
    dz = jnp.concatenate([dzg, dzv], axis=1)
    dh2 = _mm(dz, wt["ffn_w_up"], name="mm_up_dx")
    gr["ffn_w_up"] = _mm(dz, h2, ta=True, out_dtype=BF16, name="mm_up_dw")
    dx2, dmixed, gr["norm_mix_post"], gr["norm_ffn_pre"] = _mid_bwd(x2, mixed, dh2, dx3, wt["norm_mix_post"], wt["norm_ffn_pre"])
    dmi = _mm(dmixed, wt["w_out"], tb=True, name="mm_out_dx")
    gr["w_out"] = _mm(mi, dmixed, ta=True, out_dtype=BF16, name="mm_out_dw")
    token = send_early({n: gr[n] for n in EARLY_GRADS})
    b_r = b_r + jnp.tile(token[0:1, :], (1, D_MODEL // LANES))
    dgp_r, dgp_s, do_r, do_s, db_r, db_s = _merge_bwd(gpre, o_r, o_s, dmi, b_r, b_s)
    gr["b_gate"] = jnp.concatenate([db_r, db_s], axis=1)
    dout5 = _mm(do_s, wt["w_branch_s5"], name="mm_branch_s5_dx")
    gr["w_branch_s5"] = _mm(do_s, out5, ta=True, out_dtype=BF16, name="mm_branch_s5_dw")
    dys_a, dt_glu, gr["s5_b_glu"] = _glu_bwd(ys, t_glu, dout5, wt["s5_b_glu"])
    dys_b = _mm(dt_glu, wt["s5_w_glu"], tb=True, name="mm_glu_dx")
    gr["s5_w_glu"] = _mm(ys, dt_glu, ta=True, out_dtype=BF16, name="mm_glu_dw")
    du, dbblk, dcblk, dabr, dabi, gr["s5_d"] = _s5_bwd(u, dys_a, dys_b, kre, kim, bblk, cblk, abr, abi, s5_d)
    gr["s5_c_re"] = _s5_blockdiag_out_t(dcblk[:, :S5_BN, :]).reshape(wt["s5_c_re"].shape)
    gr["s5_c_im"] = (-_s5_blockdiag_out_t(dcblk[:, S5_BN:, :])).reshape(wt["s5_c_im"].shape)
    dbbr, dbbi = _s5_blockdiag_in_t(dbblk[:, :, :S5_BN]), _s5_blockdiag_in_t(dbblk[:, :, S5_BN:])
    gsel = (lax.broadcasted_iota(jnp.int32, (S5N, LANES), 0) // S5P == lax.broadcasted_iota(jnp.int32, (S5N, LANES), 1)).astype(F32)
    d_are, d_aim, d_ls, d_bre_t, d_bim_t = _s5_prep_bwd(a_re, a_im, ls, b_re_t, b_im_t, (dabr, dabi, dbbr, dbbi), gsel)
    gr["s5_a_re"] = d_are.reshape(wt["s5_a_re"].shape)
    gr["s5_a_im"] = d_aim.reshape(wt["s5_a_im"].shape)
    gr["s5_log_step"] = d_ls[0:1, :S5G]
    gr["s5_b_re"] = d_bre_t.T.reshape(wt["s5_b_re"].shape)
    gr["s5_b_im"] = d_bim_t.T.reshape(wt["s5_b_im"].shape)
    dy_r = _mm(do_r, wt["w_branch_rwkv"], name="mm_branch_rwkv_dx")
    gr["w_branch_rwkv"] = _mm(do_r, y_r, ta=True, out_dtype=BF16, name="mm_branch_rwkv_dw")
    token = send_mid(gr)
    post_prm = [post_prm[0] + jnp.tile(token[0:1, :], (1, RW // LANES))] + post_prm[1:]
    dy, dr1, dk1, dv1, dg, gr["rwkv_lnx_w"], gr["rwkv_lnx_b"], d_rk = _rwkv_post_bwd(y, r, k2, v, g, dy_r, post_prm, o512)
    gr["rwkv_r_k"] = d_rk.reshape(wt["rwkv_r_k"].shape)
    dr2, ddecay, dk2, daa, dbb, dv2 = _wkv_bwd(r, decay, k2, aa, bb, dy, sst, sast, vst, s_fin)
    cots = [dr1, ddecay, dk1, dv1, daa, dbb, dg, dr2, dk2, dv2]
    dp_rw, gr["rwkv_shift_mu"], gr["rwkv_w0"], gr["rwkv_a0"], gr["rwkv_k_k"], gr["rwkv_k_a"], dw2p, da2p, dg2p = \
        _rwkv_pre_bwd(p_rw, cots, mu, pre_small, o512)
    gr["rwkv_w2"], gr["rwkv_a2"], gr["rwkv_g2"] = dw2p[:64], da2p[64:128], dg2p[128:]
    dproj = jnp.concatenate([dp_rw, du, dgp_r, dgp_s], axis=1)
    gr["w_in"] = _mm(dproj, h1, ta=True, out_dtype=BF16, name="mm_proj_dw")
    token = send_rest(gr)
    dh1 = _mm(dproj, w_in + token[0, 0].astype(w_in.dtype), name="mm_proj_dx")
    dx, gr["norm_mix_pre"] = _rms_bwd(x, wt["norm_mix_pre"], dh1, dx2, "rms1_bwd")
    return loss_part[0, 0], dx, gr


def kernel(x, norm_mix_pre, norm_mix_post, norm_ffn_pre, norm_ffn_post, w_in, b_gate, rwkv_shift_mu, rwkv_w0, rwkv_w2, rwkv_a0, rwkv_a2, rwkv_g2, rwkv_k_k, rwkv_k_a, rwkv_r_k, rwkv_lnx_w, rwkv_lnx_b, s5_a_re, s5_a_im, s5_b_re, s5_b_im, s5_c_re, s5_c_im, s5_d, s5_log_step, s5_w_glu, s5_b_glu, w_branch_rwkv, w_branch_s5, w_out, ffn_w_up, ffn_conv_w, ffn_conv_b, ffn_w_down, loss_target, m_norm_mix_pre, m_norm_mix_post, m_norm_ffn_pre, m_norm_ffn_post, m_w_in, m_b_gate, m_rwkv_shift_mu, m_rwkv_w0, m_rwkv_w2, m_rwkv_a0, m_rwkv_a2, m_rwkv_g2, m_rwkv_k_k, m_rwkv_k_a, m_rwkv_r_k, m_rwkv_lnx_w, m_rwkv_lnx_b, m_s5_a_re, m_s5_a_im, m_s5_b_re, m_s5_b_im, m_s5_c_re, m_s5_c_im, m_s5_d, m_s5_log_step, m_s5_w_glu, m_s5_b_glu, m_w_branch_rwkv, m_w_branch_s5, m_w_out, m_ffn_w_up, m_ffn_conv_w, m_ffn_conv_b, m_ffn_w_down, v_norm_mix_pre, v_norm_mix_post, v_norm_ffn_pre, v_norm_ffn_post, v_w_in, v_b_gate, v_rwkv_shift_mu, v_rwkv_w0, v_rwkv_w2, v_rwkv_a0, v_rwkv_a2, v_rwkv_g2, v_rwkv_k_k, v_rwkv_k_a, v_rwkv_r_k, v_rwkv_lnx_w, v_rwkv_lnx_b, v_s5_a_re, v_s5_a_im, v_s5_b_re, v_s5_b_im, v_s5_c_re, v_s5_c_im, v_s5_d, v_s5_log_step, v_s5_w_glu, v_s5_b_glu, v_w_branch_rwkv, v_w_branch_s5, v_w_out, v_ffn_w_up, v_ffn_conv_w, v_ffn_conv_b, v_ffn_w_down):
    args = dict(locals())
    wgt = {n: args[n] for n in WEIGHTS}
    mom = {n: args["m_" + n] for n in WEIGHTS}
    var = {n: args["v_" + n] for n in WEIGHTS}
    me = 4 * lax.axis_index("x") + 2 * lax.axis_index("y") + lax.axis_index("c")
    shard_shapes = {n: wgt[n].shape[1:] for n in SHARDED}
    sent_shapes = {n: (shard_shapes[n][::-1] if n in TRANSPOSED else shard_shapes[n]) for n in SHARDED}
    sent = lambda n, a: a.T if n in TRANSPOSED else a

    unshard = lambda n, blocks: _cols_to_full(blocks) if n in SMALL_SHARDED else _rows_to_full(blocks)

    lowrank = ("rwkv_w2", "rwkv_a2", "rwkv_g2")
    late = tuple(n for n in SHARDED if n not in lowrank and n not in F32_GATHER)
    w_in_mine = sent("w_in", wgt["w_in"][0].astype(BF16))
    head = _all_gather(w_in_mine, "all_gather_head", first_half_only=True)
    lowrank_payload, w_in_late, _ = lax.optimization_barrier((_pack([wgt[n][0].astype(BF16) for n in lowrank]), w_in_mine, head))
    lowrank_started = _exchange_start([lowrank_payload], False, "gather_lowrank_start")
    late_payloads = [w_in_late if n == "w_in" else sent(n, wgt[n][0].astype(BF16)) for n in late]
    late_payloads.append(_pack([wgt[n][0] for n in F32_GATHER]))
    late_started = _exchange_start(late_payloads, False, "gather_late_start")
    wt = {n: wgt[n] for n in REPLICATED}
    wt["norm_mix_pre"] = wgt["norm_mix_pre"] + jnp.tile((lowrank_started[4] + late_started[4])[0:1, :], (1, D_MODEL // LANES))
    wt["w_in_head"] = _rows_to_full(head)

    def get_lowrank(after):
        land, = _exchange_wait(lowrank_started, after, False, "gather_lowrank_wait")
        blocks = _unpack(lax.dynamic_update_index_in_dim(land, lowrank_payload, me, 0), [sent_shapes[n] for n in lowrank])
        return {n: _cols_to_full(b_) for n, b_ in zip(lowrank, blocks, strict=True)}

    def get_late(after):
        lands = _exchange_wait(late_started, after, False, "gather_late_wait")
        whole_ = [lax.dynamic_update_index_in_dim(land, mine, me, 0) for land, mine in zip(lands, late_payloads, strict=True)]
        out = {n: _rows_to_full(blocks) for n, blocks in zip(late, whole_[:-1], strict=True)}
        taps = _unpack(whole_[-1], [sent_shapes[n] for n in F32_GATHER])
        out.update({n: unshard(n, b_) for n, b_ in zip(F32_GATHER, taps, strict=True)})
        return out

    rest_names = REST_GRADS + lowrank
    whole = tuple(n for n in ("ffn_conv_w",) + REPLICATED if n != "norm_mix_pre")
    whole_mid = tuple(n for n in whole if not n.startswith("rwkv_"))
    whole_late = tuple(n for n in whole if n.startswith("rwkv_"))
    blocks_of = lambda g_: g_.reshape(N_DEV, -1, g_.shape[-1])
    as_sent = lambda n, g_: blocks_of(g_.T if n in lowrank else g_)
    started = {}

    def send_early(grads_):
        started["early"] = _exchange_start([as_sent(n, grads_[n]) for n in EARLY_GRADS], True, "exchange_early_start")
        return started["early"][4]

    def send_whole(key, names):
        def send(grads_):
            started[key + "_mine"] = _pack([grads_[n] for n in names])
            started[key] = _exchange_start([started[key + "_mine"]], False, "gather_" + key + "_start")
            return started[key][4]
        return send

    def send_rest(grads_):
        started["rest"] = _exchange_start([as_sent(n, grads_[n]) for n in rest_names], True, "exchange_rest_start")
        return started["rest"][4] + send_whole("late", whole_late)(grads_)

    loss_part, dx, gr = _local_step(x[0], loss_target[0], wt, get_lowrank, get_late, send_early, send_whole("mid", whole_mid),
                                    send_rest)

    grads, upd = {}, {}
    flat2d = lambda a: a.reshape(a.shape[-2:])

    def finish_large(key, names, after):
        lands = _exchange_wait(started[key], after, True, "exchange_" + key + "_wait")
        for n, land in zip(names, lands, strict=True):
            own = lax.dynamic_index_in_dim(as_sent(n, gr[n]), me, 0, keepdims=False)
            total = _sum_parts(land, "sum_" + n, own=own)
            grads[n] = (total.T if n in TRANSPOSED + lowrank else total).reshape(wgt[n].shape)
            trio = _adamw(flat2d(grads[n]), flat2d(wgt[n]), flat2d(mom[n]), flat2d(var[n]), "adamw_" + n)
            upd[n] = tuple(t.reshape(wgt[n].shape) for t in trio)
        return upd[names[-1]][0]

    got = {}

    def finish_whole(key, names, after):
        land, = _exchange_wait(started[key], after, False, "gather_" + key + "_wait")
        total = _sum_parts(lax.dynamic_update_index_in_dim(land, started[key + "_mine"], me, 0), "sum_" + key + "_grads")
        got.update(zip(names, _unpack(total, [gr[n].shape for n in names]), strict=True))
        return total

    done = finish_large("early", EARLY_GRADS, dx)
    done = finish_whole("mid", whole_mid, done)
    done = finish_large("rest", rest_names, done)
    done = finish_whole("late", whole_late, done)
    last, _ = lax.optimization_barrier((gr["norm_mix_pre"].reshape(SUBLANES, LANES), done))
    got["norm_mix_pre"] = _sum_parts(_all_gather(last, "gather_last_grad"), "sum_last_grad").reshape(gr["norm_mix_pre"].shape)
    packed = tuple(n for n in WEIGHTS if n not in EARLY_GRADS + rest_names)
    for n in packed:
        if n in SMALL_SHARDED:
            cols = shard_shapes[n][1]
            grads[n] = lax.dynamic_slice_in_dim(got[n], me * cols, cols, axis=1).reshape(wgt[n].shape)
        else:
            grads[n] = got[n].reshape(wgt[n].shape)

    pack_local = lambda src: _pack([src[n].reshape(-1) for n in packed])
    outs = _adamw(pack_local(grads), pack_local(wgt), pack_local(mom), pack_local(var), "adamw_small")
    shapes = [wgt[n].shape for n in packed]
    upd.update(zip(packed, zip(*[_unpack(o, shapes) for o in outs], strict=True), strict=True))
    loss = lax.psum(loss_part, AXES)
    return (loss, dx[None], *[grads[n] for n in WEIGHTS], *[upd[n][0] for n in WEIGHTS], *[upd[n][1] for n in WEIGHTS],
            *[upd[n][2] for n in WEIGHTS])
```

```python
import functools
import math

import jax
import jax.numpy as jnp
from jax import lax
from jax.experimental import pallas as pl
from jax.experimental.pallas import tpu as pltpu

F32 = jnp.float32
BF16 = jnp.bfloat16
SDS = jax.ShapeDtypeStruct
HI = lax.Precision.HIGHEST
MESH = pl.DeviceIdType.MESH
AXES = ("x", "y", "c")
N_DEV = 8

D_MODEL = 1024
RW = 512
HN = 64
N_RWKV = 1792
S5W = 512
S5G = 32
S5C = 16
S5P = 64
S5N = S5G * S5P
D_FF = 2816
NORM_EPS = 1e-6
LNX_EPS = 64e-5

ADAM_LR = 0.001
ADAM_B1 = 0.9
ADAM_B2 = 0.999
ADAM_EPS = 1e-08
ADAM_WD = 0.01
ADAM_STEP = 10

LANES = 128
SUBLANES = 8
VMEM_LIMIT = 56 * 1024 * 1024
WKV_CHUNK = 32


def _params(sem=("arbitrary",)):
    return pltpu.CompilerParams(dimension_semantics=sem, vmem_limit_bytes=VMEM_LIMIT)


def _pick(n, cap):
    best = None
    for t in range(LANES, min(n, cap) + 1, LANES):
        if n % t == 0:
            best = t
    return best or n


MM_VMEM_BUDGET = 40 * 1024 * 1024


def _mm_tiles(m, n, k, a_bytes, b_bytes, o_bytes):
    def divisors(d):
        return [t for t in range(LANES, d + 1, LANES) if d % t == 0] or [d]

    tn = _pick(n, 2304)
    best = None
    for tm in divisors(m):
        for tk in divisors(k):
            vmem = 2 * (tm * tk * a_bytes + tk * tn * b_bytes) + tm * tn * 4 + 2 * tm * tn * o_bytes
            if vmem > MM_VMEM_BUDGET:
                continue
            key = ((m // tm) * (n // tn) * (k // tk), -tm)
            if best is None or key < best[0]:
                best = (key, tm, tk)
    assert best is not None, (m, n, k)
    return best[1], tn, best[2]


def _mm(a, b, *, ta=False, tb=False, out_dtype=F32, name):
    m = a.shape[1] if ta else a.shape[0]
    k = a.shape[0] if ta else a.shape[1]
    n = b.shape[0] if tb else b.shape[1]
    assert (b.shape[1] if tb else b.shape[0]) == k
    tm, tn, tk = _mm_tiles(m, n, k, a.dtype.itemsize, b.dtype.itemsize, jnp.dtype(out_dtype).itemsize)
    nk = k // tk
    dims = (((0 if ta else 1,), (1 if tb else 0,)), ((), ()))

    def body(a_ref, b_ref, o_ref, acc_ref):
        kk = pl.program_id(2)

        @pl.when(kk == 0)
        def _():
            acc_ref[...] = jnp.zeros_like(acc_ref)

        acc_ref[...] += lax.dot_general(a_ref[...].astype(BF16), b_ref[...].astype(BF16), dims,
                                        preferred_element_type=F32)

        @pl.when(kk == nk - 1)
        def _():
            o_ref[...] = acc_ref[...].astype(o_ref.dtype)

    a_spec = pl.BlockSpec((tk, tm), lambda i, j, kk: (kk, i)) if ta else pl.BlockSpec((tm, tk), lambda i, j, kk: (i, kk))
    b_spec = pl.BlockSpec((tn, tk), lambda i, j, kk: (j, kk)) if tb else pl.BlockSpec((tk, tn), lambda i, j, kk: (kk, j))
    return pl.pallas_call(
        body, grid=(m // tm, n // tn, nk), in_specs=[a_spec, b_spec],
        out_specs=pl.BlockSpec((tm, tn), lambda i, j, kk: (i, j)),
        out_shape=SDS((m, n), out_dtype), scratch_shapes=[pltpu.VMEM((tm, tn), F32)],
        compiler_params=_params(("parallel", "parallel", "arbitrary")), name=name)(a, b)


def _rows(fn, rows, params, out_rows, out_accs, *, name, tl, reverse=False, scratch=()):
    first = rows[0][0] if isinstance(rows[0], tuple) else rows[0]
    length = first.shape[0]
    tl = min(tl, length)
    nt = length // tl
    rmap = (lambda i: nt - 1 - i) if reverse else (lambda i: i)
    specs, arrs = [], []
    for r in rows:
        arr, wdt, cb = r if isinstance(r, tuple) else (r, r.shape[1], 0)
        specs.append(pl.BlockSpec((tl, wdt), lambda i, cb=cb: (rmap(i), cb)))
        arrs.append(arr)
    for p in params:
        specs.append(pl.BlockSpec(p.shape, lambda i, nd=p.ndim: (0,) * nd))
        arrs.append(p)
    out_shape = [SDS((length, c), dt) for c, dt in out_rows] + [SDS(s, F32) for s in out_accs]
    out_specs = [pl.BlockSpec((tl, c), lambda i: (rmap(i), 0)) for c, _ in out_rows]
    out_specs += [pl.BlockSpec(s, lambda i, nd=len(s): (0,) * nd) for s in out_accs]
    nr, npar, nor, noa = len(rows), len(params), len(out_rows), len(out_accs)

    def body(*refs):
        rin, pin = refs[:nr], refs[nr:nr + npar]
        rout = refs[nr + npar:nr + npar + nor]
        aout = refs[nr + npar + nor:nr + npar + nor + noa]
        scr = refs[nr + npar + nor + noa:]
        step = pl.program_id(0)
        outs_r, outs_a = fn(step, [r[...] for r in rin], [p[...] for p in pin], scr)
        for ref, val in zip(rout, outs_r, strict=True):
            ref[...] = val.astype(ref.dtype)

        @pl.when(step == 0)
        def _():
            for ref in aout:
                ref[...] = jnp.zeros_like(ref)

        for ref, val in zip(aout, outs_a, strict=True):
            ref[...] += val.astype(F32)

    res = pl.pallas_call(body, grid=(nt,), in_specs=specs, out_specs=out_specs, out_shape=out_shape,
                         scratch_shapes=list(scratch), compiler_params=_params(), name=name)(*arrs)
    return list(res)


def _rms(x, g):
    return x * lax.rsqrt(jnp.mean(x * x, axis=-1, keepdims=True) + NORM_EPS) * g


def _sig(x):
    return 0.5 * (jnp.tanh(0.5 * x) + 1.0)


def _softplus(x):
    return jnp.maximum(x, 0.0) + jnp.log(1.0 + jnp.exp(-jnp.abs(x)))


def _gelu(x):
    return x * (0.5 * (1.0 + jnp.tanh(math.sqrt(2.0 / math.pi) * (x + 0.044715 * (x * x * x)))))


def _bdot(a, b):
    return jnp.dot(a.astype(BF16), b.astype(BF16), preferred_element_type=F32)


def _hdot(a, b):
    return jnp.dot(a, b, precision=HI, preferred_element_type=F32)


def _segsum_impl(x, ones):
    hi = x.astype(BF16)
    lo = (x - hi.astype(F32)).astype(BF16)
    ones = ones.astype(BF16)
    return jnp.dot(jnp.concatenate([hi, lo], axis=1), jnp.concatenate([ones, ones], axis=0), preferred_element_type=F32)


@jax.custom_vjp
def _segsum(x, ones):
    return _segsum_impl(x, ones)


_segsum.defvjp(lambda x, ones: (_segsum_impl(x, ones), ones),
               lambda ones, g: (_segsum_impl(g, ones), jnp.zeros_like(ones)))


def _block_ones(n, blk):
    i = lax.broadcasted_iota(jnp.int32, (n, n), 0) // blk
    j = lax.broadcasted_iota(jnp.int32, (n, n), 1) // blk
    return (i == j).astype(F32)


def _rms_fwd(x, g, name):
    return _rows(lambda s, r, p, _: ([_rms(r[0], p[0])], []), [x], [g], [(x.shape[1], BF16)], [], name=name, tl=512)[0]


def _rms_bwd(x, g, dh, dres, name):
    def fn(s, r, p, _):
        _, vjp = jax.vjp(_rms, r[0], p[0])
        dx, dg = vjp(r[1])
        return [dx + r[2]], [dg]
    return _rows(fn, [x, dh, dres], [g], [(x.shape[1], F32)], [g.shape], name=name, tl=256)


def _rwkv_pre_math(k_, lr, w0, a0, k_k, k_a, w2p, a2p, g2p, o512):
    pre_w = w0 + _bdot(jnp.tanh(lr), w2p)
    w = -_softplus(-pre_w) - 0.5
    decay = jnp.exp(-jnp.exp(w))
    a = _sig(a0 + _bdot(lr, a2p))
    g = _bdot(_sig(lr), g2p)
    kr = k_ * k_k
    kk = kr / jnp.maximum(jnp.sqrt(_segsum(kr * kr, o512)), 1e-12)
    k2 = k_ * (1.0 + (a - 1.0) * k_a)
    return decay, k2, -kk, kk * a, g


def _shift_down(p, prev_row):
    row = lax.broadcasted_iota(jnp.int32, p.shape, 0)
    return jnp.where(row == 0, jnp.broadcast_to(prev_row, p.shape), pltpu.roll(p, 1, 0))


def _shift_up(q, next_row):
    n = q.shape[0]
    row = lax.broadcasted_iota(jnp.int32, q.shape, 0)
    return jnp.where(row == n - 1, jnp.broadcast_to(next_row, q.shape), pltpu.roll(q, n - 1, 0))


def _rwkv_pre_fwd(p, mu, small, o512):
    def fn(step, r, prm, scr):
        car = scr[0]

        @pl.when(step == 0)
        def _():
            car[...] = jnp.zeros_like(car)

        x = r[0]
        prev = _shift_down(x, car[SUBLANES - 1:SUBLANES, :])
        car[...] = x[x.shape[0] - SUBLANES:, :]
        xs = x + (prev - x) * prm[0]
        decay, k2, aa, bb, g = _rwkv_pre_math(xs[:, RW:2 * RW], xs[:, 3 * RW:], *prm[1:])
        return [xs[:, :RW], decay, k2, xs[:, 2 * RW:3 * RW], aa, bb, g], []
    return _rows(fn, [p], [mu, *small, o512], [(RW, F32)] * 7, [], name="rwkv_pre_fwd", tl=256,
                 scratch=[pltpu.VMEM((SUBLANES, N_RWKV), F32)])


def _rwkv_pre_bwd(p, cots, mu, small, o512):
    length = p.shape[0]
    tl = min(256, length)
    nt = length // tl
    rows_per = tl // SUBLANES
    params = [mu, *small, o512]
    acc_shapes = [mu.shape] + [q.shape for q in small]
    nr, npar, nacc = 2 + len(cots), len(params), len(acc_shapes)

    def body(*refs):
        rin, pin = refs[:nr], refs[nr:nr + npar]
        dp_ref = refs[nr + npar]
        aout = refs[nr + npar + 1:nr + npar + 1 + nacc]
        car_q = refs[nr + npar + 1 + nacc]
        step = pl.program_id(0)

        @pl.when(step == 0)
        def _():
            car_q[...] = jnp.zeros_like(car_q)
            for ref in aout:
                ref[...] = jnp.zeros_like(ref)

        x = rin[0][...]
        prev_row = jnp.where(step == nt - 1, 0.0, rin[1][SUBLANES - 1:SUBLANES, :])
        dr, ddecay, dk2, dv, daa, dbb, dg, dr_b, dk2_b, dv_b = [r[...] for r in rin[2:]]
        dr, dk2, dv = dr + dr_b, dk2 + dk2_b, dv + dv_b
        prm = [q[...] for q in pin]
        mu_, o512_ = prm[0], prm[-1]
        prev = _shift_down(x, prev_row)
        xs = x + (prev - x) * mu_
        _, vjp = jax.vjp(lambda k_, lr, *w: _rwkv_pre_math(k_, lr, *w, o512_), xs[:, RW:2 * RW], xs[:, 3 * RW:], *prm[1:-1])
        dk_, dlr, *dsmall = vjp((ddecay, dk2, daa, dbb, dg))
        dxs = jnp.concatenate([dr, dk_, dv, dlr], axis=1)
        q = dxs * mu_
        dp_ref[...] = (dxs - q + _shift_up(q, car_q[0:1, :])).astype(dp_ref.dtype)
        car_q[...] = q[:SUBLANES, :]
        aout[0][...] += jnp.sum((prev - x) * dxs, axis=0, keepdims=True)
        for ref, val in zip(aout[1:], dsmall, strict=True):
            ref[...] += val

    rmap = lambda i: nt - 1 - i
    specs = [pl.BlockSpec((tl, N_RWKV), lambda i: (rmap(i), 0)),
             pl.BlockSpec((SUBLANES, N_RWKV), lambda i: (jnp.maximum(rmap(i) * rows_per - 1, 0), 0))]
    specs += [pl.BlockSpec((tl, RW), lambda i: (rmap(i), 0)) for _ in cots]
    specs += [pl.BlockSpec(q.shape, lambda i, nd=q.ndim: (0,) * nd) for q in params]
    out_shape = [SDS((length, N_RWKV), BF16)] + [SDS(sh, F32) for sh in acc_shapes]
    out_specs = [pl.BlockSpec((tl, N_RWKV), lambda i: (rmap(i), 0))]
    out_specs += [pl.BlockSpec(sh, lambda i, nd=len(sh): (0,) * nd) for sh in acc_shapes]
    res = pl.pallas_call(body, grid=(nt,), in_specs=specs, out_specs=out_specs, out_shape=out_shape,
                         scratch_shapes=[pltpu.VMEM((SUBLANES, N_RWKV), F32)],
                         compiler_params=_params(), name="rwkv_pre_bwd")(p, p, *cots, *params)
    return list(res)


def _rwkv_post_math(y, r, k2, v, g, lnx_w, lnx_b, r_k, o512):
    mean = _segsum(y, o512) * (1.0 / HN)
    yc = y - mean
    var = _segsum(yc * yc, o512) * (1.0 / HN)
    yn = yc * lax.rsqrt(var + LNX_EPS) * lnx_w + lnx_b
    bonus = _segsum(r * k2 * r_k, o512) * v
    return (yn + bonus) * g


def _rwkv_post_fwd(y, r, k2, v, g, prm, o512):
    return _rows(lambda s, rr, p, _: ([_rwkv_post_math(*rr, *p)], []), [y, r, k2, v, g], [*prm, o512],
                 [(RW, BF16)], [], name="rwkv_post_fwd", tl=256)[0]


def _rwkv_post_bwd(y, r, k2, v, g, dout, prm, o512):
    def fn(s, rr, p, _):
        o = p[-1]
        _, vjp = jax.vjp(lambda *a: _rwkv_post_math(*a, o), *rr[:5], *p[:-1])
        gr = vjp(rr[5])
        return list(gr[:5]), list(gr[5:])
    return _rows(fn, [y, r, k2, v, g, dout], [*prm, o512], [(RW, F32)] * 5, [q.shape for q in prm],
                 name="rwkv_post_bwd", tl=256)


def _glu_math(ys, t, b):
    return ys * _sig(t + b)


def _glu_fwd(ys, t, b):
    return _rows(lambda s, r, p, _: ([_glu_math(r[0], r[1], p[0])], []), [ys, t], [b], [(S5W, BF16)], [],
                 name="s5_glu_fwd", tl=512)[0]


def _glu_bwd(ys, t, dout, b):
    def fn(s, r, p, _):
        _, vjp = jax.vjp(_glu_math, r[0], r[1], p[0])
        dys, dt, db = vjp(r[2])
        return [dys, dt], [db]
    return _rows(fn, [ys, t, dout], [b], [(S5W, F32), (S5W, BF16)], [b.shape], name="s5_glu_bwd", tl=512)


def _merge_math(gp_r, gp_s, o_r, o_s, b_r, b_s):
    return _sig(gp_r + b_r) * o_r + _sig(gp_s + b_s) * o_s


def _merge_fwd(gpre, o_r, o_s, b_r, b_s):
    return _rows(lambda s, r, p, _: ([_merge_math(*r, *p)], []),
                 [(gpre, D_MODEL, 0), (gpre, D_MODEL, 1), o_r, o_s], [b_r, b_s], [(D_MODEL, BF16)], [],
                 name="merge_fwd", tl=256)[0]


def _merge_bwd(gpre, o_r, o_s, dmi, b_r, b_s):
    def fn(s, r, p, _):
        _, vjp = jax.vjp(_merge_math, *r[:4], *p)
        dgr, dgs, dor, dos, dbr, dbs = vjp(r[4])
        return [dgr, dgs, dor, dos], [dbr, dbs]
    return _rows(fn, [(gpre, D_MODEL, 0), (gpre, D_MODEL, 1), o_r, o_s, dmi], [b_r, b_s], [(D_MODEL, BF16)] * 4,
                 [b_r.shape, b_s.shape], name="merge_bwd", tl=256)


def _mid_fwd(x, mixed, g_post, g_pre):
    def fn(s, r, p, _):
        x2 = r[0] + _rms(r[1], p[0])
        return [x2, _rms(x2, p[1])], []
    return _rows(fn, [x, mixed], [g_post, g_pre], [(D_MODEL, F32), (D_MODEL, BF16)], [], name="mid_fwd", tl=256)


def _mid_bwd(x2, mixed, dh2, dx3, g_post, g_pre):
    def fn(s, r, p, _):
        _, vjp1 = jax.vjp(_rms, r[0], p[1])
        dx2, dg_pre = vjp1(r[2])
        dx2 = dx2 + r[3]
        _, vjp2 = jax.vjp(_rms, r[1], p[0])
        dmixed, dg_post = vjp2(dx2)
        return [dx2, dmixed], [dg_post, dg_pre]
    return _rows(fn, [x2, mixed, dh2, dx3], [g_post, g_pre], [(D_MODEL, F32), (D_MODEL, BF16)], [g_post.shape, g_pre.shape],
                 name="mid_bwd", tl=256)


def _final(x2, f, tgt, g_post):
    def fn(s, r, p, _):
        y, vjp = jax.vjp(_rms, r[1], p[0])
        diff = r[0] + y - r[2]
        dx3 = diff * (1.0 / D_MODEL)
        df, dg = vjp(dx3)
        part = 0.5 * jnp.sum(jnp.sum(diff * diff, axis=1, keepdims=True), axis=0, keepdims=True) * (1.0 / D_MODEL)
        return [dx3, df], [dg, jnp.broadcast_to(part, (1, LANES))]
    return _rows(fn, [x2, f, tgt], [g_post], [(D_MODEL, F32), (D_MODEL, BF16)], [g_post.shape, (1, LANES)], name="final", tl=256)


def _conv_taps(z, car):
    row = lax.broadcasted_iota(jnp.int32, z.shape, 0)
    z1 = jnp.where(row == 0, jnp.broadcast_to(car[7:8, :], z.shape), pltpu.roll(z, 1, 0))
    z2 = pltpu.roll(z, 2, 0)
    z2 = jnp.where(row == 0, jnp.broadcast_to(car[6:7, :], z.shape), z2)
    z2 = jnp.where(row == 1, jnp.broadcast_to(car[7:8, :], z.shape), z2)
    return z1, z2


def _conv(z, car, w, b):
    z1, z2 = _conv_taps(z, car)
    return b + w[0:1, :] * z2 + w[1:2, :] * z1 + w[2:3, :] * z, z1, z2


def _conv_fwd(z, conv_w, conv_b):
    length = z.shape[0]
    tl = min(256, length)
    nt = length // tl
    tc = _pick(D_FF, 1536)
    nb = D_FF // tc

    def body(zg_ref, zv_ref, wg_ref, wv_ref, bg_ref, bv_ref, o_ref, cg, cv):
        @pl.when(pl.program_id(1) == 0)
        def _():
            cg[...] = jnp.zeros_like(cg)
            cv[...] = jnp.zeros_like(cv)

        zg, zv = zg_ref[...], zv_ref[...]
        gate, _, _ = _conv(zg, cg[...], wg_ref[...], bg_ref[...])
        val, _, _ = _conv(zv, cv[...], wv_ref[...], bv_ref[...])
        cg[...] = zg[tl - SUBLANES:, :]
        cv[...] = zv[tl - SUBLANES:, :]
        o_ref[...] = (_gelu(gate) * val).astype(o_ref.dtype)

    zspec = lambda off: pl.BlockSpec((tl, tc), lambda j, i: (i, j + off))
    wspec = lambda off, r: pl.BlockSpec((r, tc), lambda j, i: (0, j + off))
    return pl.pallas_call(
        body, grid=(nb, nt),
        in_specs=[zspec(0), zspec(nb), wspec(0, 3), wspec(nb, 3), wspec(0, 1), wspec(nb, 1)],
        out_specs=pl.BlockSpec((tl, tc), lambda j, i: (i, j)), out_shape=SDS((length, D_FF), BF16),
        scratch_shapes=[pltpu.VMEM((SUBLANES, tc), F32)] * 2,
        compiler_params=_params(("arbitrary", "arbitrary")), name="conv_fwd")(z, z, conv_w, conv_w, conv_b, conv_b)


def _conv_bwd(z, dact, conv_w, conv_b):
    length = z.shape[0]
    tl = min(256, length)
    nt = length // tl
    tc = _pick(D_FF, 1536)
    nb = D_FF // tc
    rows_per = tl // SUBLANES

    def half_bwd(dzc, z, z1, z2, w, dcar):
        n = tl
        row = lax.broadcasted_iota(jnp.int32, dzc.shape, 0)
        u1 = jnp.where(row == n - 1, jnp.broadcast_to(dcar[0:1, :], dzc.shape), pltpu.roll(dzc, n - 1, 0))
        u2 = pltpu.roll(dzc, n - 2, 0)
        u2 = jnp.where(row == n - 2, jnp.broadcast_to(dcar[0:1, :], dzc.shape), u2)
        u2 = jnp.where(row == n - 1, jnp.broadcast_to(dcar[1:2, :], dzc.shape), u2)
        dz = w[2:3, :] * dzc + w[1:2, :] * u1 + w[0:1, :] * u2
        dw = jnp.concatenate([jnp.sum(dzc * z2, axis=0, keepdims=True), jnp.sum(dzc * z1, axis=0, keepdims=True),
                              jnp.sum(dzc * z, axis=0, keepdims=True)], axis=0)
        return dz, dw, jnp.sum(dzc, axis=0, keepdims=True)

    def body(zg_ref, zv_ref, pg_ref, pv_ref, da_ref, wg_ref, wv_ref, bg_ref, bv_ref,
             dzg_ref, dzv_ref, dwg_ref, dwv_ref, dbg_ref, dbv_ref, cg, cv):
        step = pl.program_id(1)

        @pl.when(step == 0)
        def _():
            cg[...] = jnp.zeros_like(cg)
            cv[...] = jnp.zeros_like(cv)
            for ref in (dwg_ref, dwv_ref, dbg_ref, dbv_ref):
                ref[...] = jnp.zeros_like(ref)

        is_first_tile = step == nt - 1
        zg, zv = zg_ref[...], zv_ref[...]
        pg = jnp.where(is_first_tile, 0.0, pg_ref[...])
        pv = jnp.where(is_first_tile, 0.0, pv_ref[...])
        wg, wv = wg_ref[...], wv_ref[...]
        gate, zg1, zg2 = _conv(zg, pg, wg, bg_ref[...])
        val, zv1, zv2 = _conv(zv, pv, wv, bv_ref[...])
        act_g, vjp = jax.vjp(_gelu, gate)
        da = da_ref[...]
        dgate = vjp(da * val)[0]
        dval = da * act_g
        dzg, dwg, dbg = half_bwd(dgate, zg, zg1, zg2, wg, cg[...])
        dzv, dwv, dbv = half_bwd(dval, zv, zv1, zv2, wv, cv[...])
        cg[...] = dgate[:SUBLANES, :]
        cv[...] = dval[:SUBLANES, :]
        dzg_ref[...] = dzg.astype(dzg_ref.dtype)
        dzv_ref[...] = dzv.astype(dzv_ref.dtype)
        dwg_ref[...] += dwg
        dwv_ref[...] += dwv
        dbg_ref[...] += dbg
        dbv_ref[...] += dbv

    rmap = lambda i: nt - 1 - i
    zspec = lambda off: pl.BlockSpec((tl, tc), lambda j, i: (rmap(i), j + off))
    pspec = lambda off: pl.BlockSpec((SUBLANES, tc), lambda j, i: (jnp.maximum(rmap(i) * rows_per - 1, 0), j + off))
    wspec = lambda off, r: pl.BlockSpec((r, tc), lambda j, i: (0, j + off))
    out_w = lambda r: pl.BlockSpec((r, tc), lambda j, i: (0, j))
    dzg, dzv, dwg, dwv, dbg, dbv = pl.pallas_call(
        body, grid=(nb, nt),
        in_specs=[zspec(0), zspec(nb), pspec(0), pspec(nb), pl.BlockSpec((tl, tc), lambda j, i: (rmap(i), j)),
                  wspec(0, 3), wspec(nb, 3), wspec(0, 1), wspec(nb, 1)],
        out_specs=[pl.BlockSpec((tl, tc), lambda j, i: (rmap(i), j))] * 2 + [out_w(3), out_w(3), out_w(1), out_w(1)],
        out_shape=[SDS((length, D_FF), BF16)] * 2 + [SDS((3, D_FF), F32)] * 2 + [SDS((1, D_FF), F32)] * 2,
        scratch_shapes=[pltpu.VMEM((SUBLANES, tc), F32)] * 2,
        compiler_params=_params(("arbitrary", "arbitrary")), name="conv_bwd")(z, z, z, z, dact, conv_w, conv_w, conv_b, conv_b)
    return dzg, dzv, jnp.concatenate([dwg, dwv], axis=1), jnp.concatenate([dbg, dbv], axis=1)


def _s5_prep_math(a_re, a_im, ls, b_re, b_im):
    dt = jnp.exp(ls)
    er = jnp.exp(a_re * dt)
    ph = a_im * dt
    abr, abi = er * jnp.cos(ph), er * jnp.sin(ph)
    den = a_re * a_re + a_im * a_im
    nr = abr - 1.0
    cr = (nr * a_re + abi * a_im) / den
    ci = (abi * a_re - nr * a_im) / den
    return abr, abi, cr * b_re - ci * b_im, cr * b_im + ci * b_re


def _s5_prep_fwd(a_re, a_im, ls, b_re, b_im):
    def body(ar, ai, l, br, bi, o1, o2, o3, o4):
        for ref, val in zip((o1, o2, o3, o4), _s5_prep_math(ar[...], ai[...], l[...], br[...], bi[...]), strict=True):
            ref[...] = val
    return pl.pallas_call(body, out_shape=[SDS((1, S5N), F32)] * 2 + [SDS((S5C, S5N), F32)] * 2,
                          name="s5_prep_fwd")(a_re, a_im, ls, b_re, b_im)


def _s5_prep_bwd(a_re, a_im, ls, b_re, b_im, cots, gsel):
    def body(ar, ai, l, br, bi, c1, c2, c3, c4, g_ref, o1, o2, o3, o4, o5):
        _, vjp = jax.vjp(_s5_prep_math, ar[...], ai[...], l[...], br[...], bi[...])
        d_ar, d_ai, d_ls, d_br, d_bi = vjp((c1[...], c2[...], c3[...], c4[...]))
        o1[...] = d_ar
        o2[...] = d_ai
        o3[...] = _hdot(jnp.broadcast_to(d_ls, (SUBLANES, S5N)), g_ref[...])
        o4[...] = d_br
        o5[...] = d_bi
    return pl.pallas_call(body, out_shape=[SDS((1, S5N), F32)] * 2 + [SDS((SUBLANES, LANES), F32)] + [SDS((S5C, S5N), F32)] * 2,
                          name="s5_prep_bwd")(a_re, a_im, ls, b_re, b_im, *cots, gsel)


def _cmul(ar, ai, br, bi):
    return ar * br - ai * bi, ar * bi + ai * br


def _s5_powers(abr, abi):
    shp = (SUBLANES, S5N)
    a1 = (jnp.broadcast_to(abr, shp), jnp.broadcast_to(abi, shp))
    a2 = _cmul(*a1, *a1)
    a4 = _cmul(*a2, *a2)
    row = lax.broadcasted_iota(jnp.int32, shp, 0)
    pr, pi = a1
    cur = a1
    for i in range(1, SUBLANES):
        cur = _cmul(*cur, *a1)
        pr = jnp.where(row == i, cur[0], pr)
        pi = jnp.where(row == i, cur[1], pi)
    return a1, a2, a4, (pr, pi)


def _s5_scan(sre_ref, sim_ref, car_re, car_im, abr, abi, n_rows, reverse):
    a1, a2, a4, (pr, pi) = _s5_powers(abr, abi)
    sgn = -1.0 if reverse else 1.0
    row = lax.broadcasted_iota(jnp.int32, (SUBLANES, S5N), 0)
    if reverse:
        qr, qi = pr, pi
        for i in range(SUBLANES):
            src = SUBLANES - 1 - i
            qr = jnp.where(row == i, jnp.broadcast_to(pr[src:src + 1, :], pr.shape), qr)
            qi = jnp.where(row == i, jnp.broadcast_to(pi[src:src + 1, :], pi.shape), qi)
        pr, pi = qr, qi
    nblk = n_rows // SUBLANES

    def blk(i, carry):
        cr, ci = carry
        b = (nblk - 1 - i) if reverse else i
        sl = pl.ds(pl.multiple_of(b * SUBLANES, SUBLANES), SUBLANES)
        xr, xi = sre_ref[sl, :], sim_ref[sl, :]
        for kk, (er, ei) in ((1, a1), (2, a2), (4, a4)):
            if reverse:
                sr = jnp.where(row < SUBLANES - kk, pltpu.roll(xr, SUBLANES - kk, 0), 0.0)
                si = jnp.where(row < SUBLANES - kk, pltpu.roll(xi, SUBLANES - kk, 0), 0.0)
            else:
                sr = jnp.where(row >= kk, pltpu.roll(xr, kk, 0), 0.0)
                si = jnp.where(row >= kk, pltpu.roll(xi, kk, 0), 0.0)
            dr, di = _cmul(er, sgn * ei, sr, si)
            xr, xi = xr + dr, xi + di
        dr, di = _cmul(pr, sgn * pi, cr, ci)
        xr, xi = xr + dr, xi + di
        sre_ref[sl, :] = xr
        sim_ref[sl, :] = xi
        edge = 0 if reverse else SUBLANES - 1
        return (jnp.broadcast_to(xr[edge:edge + 1, :], xr.shape), jnp.broadcast_to(xi[edge:edge + 1, :], xi.shape))

    cr, ci = lax.fori_loop(0, nblk, blk, (car_re[...], car_im[...]))
    car_re[...] = cr
    car_im[...] = ci


S5_BLK = 4
S5_BN = S5N // S5_BLK


def _split2(x):
    hi = x.astype(BF16)
    return hi, (x - hi.astype(F32)).astype(BF16)


def _dot3(a, b, ca, cb):
    ah, al = _split2(a)
    bh, bl = _split2(b)
    return lax.dot_general(jnp.concatenate([ah, ah, al], axis=ca), jnp.concatenate([bh, bl, bh], axis=cb),
                           (((ca,), (cb,)), ((), ())), preferred_element_type=F32)


def _dot1(a, b, ca, cb):
    return lax.dot_general(a.astype(BF16), b.astype(BF16), (((ca,), (cb,)), ((), ())), preferred_element_type=F32)


def _s5_project_in(u, bblk_ref, sre_ref, sim_ref):
    for j in range(S5_BLK):
        bu = _dot1(u[:, j * LANES:(j + 1) * LANES], bblk_ref[j], 1, 0)
        sre_ref[:, j * S5_BN:(j + 1) * S5_BN] = bu[:, :S5_BN]
        sim_ref[:, j * S5_BN:(j + 1) * S5_BN] = bu[:, S5_BN:]


def _s5_project_out(u, d, cblk_ref, sre_ref, sim_ref):
    ys = []
    for j in range(S5_BLK):
        sl = slice(j * S5_BN, (j + 1) * S5_BN)
        ys.append(_dot1(sre_ref[:, sl], cblk_ref[j, :S5_BN, :], 1, 0) + _dot1(sim_ref[:, sl], cblk_ref[j, S5_BN:, :], 1, 0))
    return jnp.concatenate(ys, axis=1) + d * u


def _s5_fwd(u, bblk, cblk, abr, abi, d):
    length = u.shape[0]
    tl = min(256, length)
    nt = length // tl

    def body(u_ref, b_ref, c_ref, ar_ref, ai_ref, d_ref, ys_ref, kr_ref, ki_ref, sre, sim, car_re, car_im):
        @pl.when(pl.program_id(0) == 0)
        def _():
            car_re[...] = jnp.zeros_like(car_re)
            car_im[...] = jnp.zeros_like(car_im)

        kr_ref[0] = car_re[...]
        ki_ref[0] = car_im[...]
        u_ = u_ref[...]
        _s5_project_in(u_, b_ref, sre, sim)
        _s5_scan(sre, sim, car_re, car_im, ar_ref[...], ai_ref[...], tl, False)
        ys_ref[...] = _gelu(_s5_project_out(u_, d_ref[...], c_ref, sre, sim))

    full = lambda a: pl.BlockSpec(a.shape, lambda i, nd=a.ndim: (0,) * nd)
    chk = pl.BlockSpec((1, SUBLANES, S5N), lambda i: (i, 0, 0))
    return pl.pallas_call(
        body, grid=(nt,), in_specs=[pl.BlockSpec((tl, S5W), lambda i: (i, 0)), full(bblk), full(cblk), full(abr), full(abi), full(d)],
        out_specs=[pl.BlockSpec((tl, S5W), lambda i: (i, 0)), chk, chk],
        out_shape=[SDS((length, S5W), F32), SDS((nt, SUBLANES, S5N), F32), SDS((nt, SUBLANES, S5N), F32)],
        scratch_shapes=[pltpu.VMEM((tl, S5N), F32)] * 2 + [pltpu.VMEM((SUBLANES, S5N), F32)] * 2,
        compiler_params=_params(), name="s5_fwd")(u, bblk, cblk, abr, abi, d)


def _s5_bwd(u, dys_a, dys_b, kre, kim, bblk, cblk, abr, abi, d):
    length = u.shape[0]
    tl = min(256, length)
    nt = length // tl

    def body(u_ref, da_ref, db_ref, kr_ref, ki_ref, b_ref, c_ref, ar_ref, ai_ref, d_ref,
             du_ref, dB_ref, dC_ref, dar_ref, dai_ref, dd_ref, sre, sim, gre, gim, car_re, car_im, dcar_re, dcar_im):
        @pl.when(pl.program_id(0) == 0)
        def _():
            dcar_re[...] = jnp.zeros_like(dcar_re)
            dcar_im[...] = jnp.zeros_like(dcar_im)
            for ref in (dB_ref, dC_ref, dar_ref, dai_ref, dd_ref):
                ref[...] = jnp.zeros_like(ref)

        u_ = u_ref[...]
        abr_, abi_, d_ = ar_ref[...], ai_ref[...], d_ref[...]
        car_re[...] = kr_ref[0]
        car_im[...] = ki_ref[0]
        _s5_project_in(u_, b_ref, sre, sim)
        _s5_scan(sre, sim, car_re, car_im, abr_, abi_, tl, False)
        y = _s5_project_out(u_, d_, c_ref, sre, sim)
        _, vjp = jax.vjp(_gelu, y)
        dy = vjp(da_ref[...] + db_ref[...])[0]
        dd_ref[...] += jnp.sum(dy * u_, axis=0, keepdims=True)
        for j in range(S5_BLK):
            sl = slice(j * S5_BN, (j + 1) * S5_BN)
            dyj = dy[:, j * LANES:(j + 1) * LANES]
            gre[:, sl] = _dot1(dyj, c_ref[j, :S5_BN, :], 1, 1)
            gim[:, sl] = _dot1(dyj, c_ref[j, S5_BN:, :], 1, 1)
            dC_ref[j, :S5_BN, :] += _dot1(sre[:, sl], dyj, 0, 0)
            dC_ref[j, S5_BN:, :] += _dot1(sim[:, sl], dyj, 0, 0)
        _s5_scan(gre, gim, dcar_re, dcar_im, abr_, abi_, tl, True)
        gr, gi = gre[...], gim[...]
        pr = _shift_down(sre[...], kr_ref[0, 0:1, :])
        pi = _shift_down(sim[...], ki_ref[0, 0:1, :])
        dar_ref[...] += jnp.sum(gr * pr + gi * pi, axis=0, keepdims=True)
        dai_ref[...] += jnp.sum(gi * pr - gr * pi, axis=0, keepdims=True)
        dus = []
        for j in range(S5_BLK):
            sl = slice(j * S5_BN, (j + 1) * S5_BN)
            uj = u_[:, j * LANES:(j + 1) * LANES]
            dus.append(_dot1(gre[:, sl], b_ref[j, :, :S5_BN], 1, 1) + _dot1(gim[:, sl], b_ref[j, :, S5_BN:], 1, 1))
            dB_ref[j, :, :S5_BN] += _dot1(uj, gre[:, sl], 0, 0)
            dB_ref[j, :, S5_BN:] += _dot1(uj, gim[:, sl], 0, 0)
        du_ref[...] = (jnp.concatenate(dus, axis=1) + d_ * dy).astype(du_ref.dtype)

    rmap = lambda i: nt - 1 - i
    full = lambda a: pl.BlockSpec(a.shape, lambda i, nd=a.ndim: (0,) * nd)
    row = pl.BlockSpec((tl, S5W), lambda i: (rmap(i), 0))
    chk = pl.BlockSpec((1, SUBLANES, S5N), lambda i: (rmap(i), 0, 0))
    return pl.pallas_call(
        body, grid=(nt,), in_specs=[row, row, row, chk, chk, full(bblk), full(cblk), full(abr), full(abi), full(d)],
        out_specs=[row, full(bblk), full(cblk), full(abr), full(abi), full(d)],
        out_shape=[SDS((length, S5W), BF16), SDS(bblk.shape, F32), SDS(cblk.shape, F32), SDS(abr.shape, F32),
                   SDS(abi.shape, F32), SDS(d.shape, F32)],
        scratch_shapes=[pltpu.VMEM((tl, S5N), F32)] * 4 + [pltpu.VMEM((SUBLANES, S5N), F32)] * 4,
        compiler_params=_params(), name="s5_bwd")(u, dys_a, dys_b, kre, kim, bblk, cblk, abr, abi, d)


def _s5_blockdiag_in(bb):
    t = bb.reshape(S5C, S5_BLK, S5_BN).transpose(1, 0, 2)
    t = jnp.tile(t, (1, SUBLANES, 1))
    mask = (lax.broadcasted_iota(jnp.int32, (LANES, S5_BN), 0) // S5C) == (lax.broadcasted_iota(jnp.int32, (LANES, S5_BN), 1) // S5P)
    return jnp.where(mask[None], t, 0.0)


def _s5_blockdiag_in_t(dblk):
    t = dblk.reshape(S5_BLK, SUBLANES, S5C, SUBLANES, S5P)
    t = jnp.diagonal(t, axis1=1, axis2=3)
    return t.transpose(1, 0, 3, 2).reshape(S5C, S5N)


def _s5_blockdiag_out(c):
    t = c.reshape(S5_BLK, SUBLANES, S5C, S5P).transpose(0, 3, 1, 2).reshape(S5_BLK, S5P, LANES)
    t = jnp.tile(t, (1, SUBLANES, 1))
    mask = (lax.broadcasted_iota(jnp.int32, (S5_BN, LANES), 0) // S5P) == (lax.broadcasted_iota(jnp.int32, (S5_BN, LANES), 1) // S5C)
    return jnp.where(mask[None], t, 0.0)


def _s5_blockdiag_out_t(dblk):
    t = dblk.reshape(S5_BLK, SUBLANES, S5P, SUBLANES, S5C)
    t = jnp.diagonal(t, axis1=1, axis2=3)
    return t.transpose(0, 3, 2, 1).reshape(S5G, S5C, S5P)


def _wkv_consts():
    lane = lax.broadcasted_iota(jnp.int32, (4 * HN, LANES), 1)
    row = lax.broadcasted_iota(jnp.int32, (4 * HN, LANES), 0)
    diag = ((lane % HN) == (row % HN)).astype(F32)
    ones = _block_ones(LANES, HN).astype(BF16)
    return lane % HN, diag, jnp.concatenate([ones, ones], axis=0)


def _wkv_split(p):
    hi = p.astype(BF16)
    lo = (p - hi.astype(F32)).astype(BF16)
    return jnp.concatenate([hi, lo], axis=1)


def _wkv_rowseg(p, ones):
    return jnp.dot(_wkv_split(p), ones, preferred_element_type=F32)


def _wkv_full(rows_ref, t):
    q = jnp.broadcast_to(rows_ref[t], (SUBLANES, RW))
    return jnp.concatenate([jnp.tile(q[:, hp * LANES:(hp + 1) * LANES], (SUBLANES, 1)) for hp in range(4)], axis=0)


def _wkv_colsum8(x):
    return jnp.concatenate([x[hp * HN:(hp + 1) * HN].reshape(SUBLANES, SUBLANES, LANES).sum(axis=0) for hp in range(4)], axis=1)


def _wkv_prebroadcast(src_refs, dst_refs, n):
    for s, dref in zip(src_refs, dst_refs, strict=True):
        dref[...] = s[...].reshape(n, 1, RW)


def _wkv_pair_sums(lhs_ref, out_ref):
    n, ns = out_ref.shape[0], out_ref.shape[1]
    ones_pair = _block_ones(2 * LANES, HN).astype(BF16)
    out_ref[...] = jnp.dot(lhs_ref[0:n].reshape(n * ns, 2 * LANES), ones_pair,
                           preferred_element_type=F32).reshape(n, ns, 2 * LANES)


def _wkv_transpose_out(acc, c, diag):
    lane = lax.broadcasted_iota(jnp.int32, (c, LANES), 1)
    pick = diag[:HN].astype(BF16)
    pick3 = jnp.concatenate([pick, pick, pick], axis=0)
    outs = []
    for hp in range(4):
        blk = acc[hp * HN:(hp + 1) * HN]
        hi = blk.astype(BF16)
        rest = blk - hi.astype(F32)
        mid = rest.astype(BF16)
        lo = (rest - mid.astype(F32)).astype(BF16)
        zp = lax.dot_general(jnp.concatenate([hi, mid, lo], axis=0), pick3, (((0,), (0,)), ((), ())), preferred_element_type=F32)
        outs.append(jnp.where(lane < HN, zp[:c], zp[HN:HN + c]))
    return jnp.concatenate(outs, axis=1)


def _wkv_fwd(r, w, k, v, a, b):
    length = r.shape[0]
    c = min(WKV_CHUNK, length)
    nc = length // c
    ns = 4 * HN

    def body(r_ref, w_ref, k_ref, v_ref, a_ref, b_ref, y_ref, sst_ref, sast_ref, vst_ref, fin_ref,
             s_ref, acc_ref, lhs_ref, ysum_ref, rb, wb, kb, vb, ab, bb, wab, beb, kab):
        @pl.when(pl.program_id(0) == 0)
        def _():
            s_ref[...] = jnp.zeros_like(s_ref)

        a_next = pltpu.roll(a_ref[...], c - 1, 0)
        o512 = _block_ones(RW, HN)
        _wkv_prebroadcast((r_ref, w_ref, k_ref, v_ref, a_ref, b_ref, w_ref[...] * a_next,
                           _segsum_impl(b_ref[...] * a_next, o512), _segsum_impl(k_ref[...] * a_next, o512)),
                          (rb, wb, kb, vb, ab, bb, wab, beb, kab), c)
        lane_t, diag, ones = _wkv_consts()

        def fill_v(q, carry):
            for half in range(2):
                lhs_ref[q, :, half * LANES:(half + 1) * LANES] = (_wkv_full(vb, 2 * q + half) * diag).astype(BF16)
            return carry

        lax.fori_loop(0, c // 2, fill_v, 0)
        _wkv_pair_sums(lhs_ref, vst_ref)

        def pair(p, carry):
            t0, t1 = 2 * p, 2 * p + 1
            s0 = s_ref[...]
            both = _wkv_rowseg(jnp.concatenate([s0 * _wkv_full(ab, t0), s0 * _wkv_full(wab, t0)], axis=0), ones)
            v_pair = vst_ref[p]
            sa0, v0 = both[:ns], v_pair[:, :LANES]
            s1 = s0 * _wkv_full(wb, t0) + sa0 * _wkv_full(bb, t0) + v0 * _wkv_full(kb, t0)
            sa1 = both[ns:] + sa0 * _wkv_full(beb, t0) + v0 * _wkv_full(kab, t0)
            s2 = s1 * _wkv_full(wb, t1) + sa1 * _wkv_full(bb, t1) + v_pair[:, LANES:] * _wkv_full(kb, t1)
            sst_ref[t0] = s0
            sast_ref[t0] = sa0
            sst_ref[t1] = s1
            sast_ref[t1] = sa1
            s_ref[...] = s2
            lhs_ref[p, :, 0:LANES] = (s1 * _wkv_full(rb, t0)).astype(BF16)
            lhs_ref[p, :, LANES:2 * LANES] = (s2 * _wkv_full(rb, t1)).astype(BF16)
            return carry

        lax.fori_loop(0, c // 2, pair, 0)
        _wkv_pair_sums(lhs_ref, ysum_ref)
        acc_ref[...] = jnp.zeros_like(acc_ref)

        def gather_y(p, carry):
            both = ysum_ref[p]
            acc = jnp.where(lane_t == 2 * p, both[:, :LANES], acc_ref[...])
            acc_ref[...] = jnp.where(lane_t == 2 * p + 1, both[:, LANES:], acc)
            return carry

        lax.fori_loop(0, c // 2, gather_y, 0)
        y_ref[...] = _wkv_transpose_out(acc_ref[...], c, diag)
        fin_ref[...] = s_ref[...]

    row = pl.BlockSpec((c, RW), lambda i: (i, 0))
    st = pl.BlockSpec((c, ns, LANES), lambda i: (i, 0, 0))
    state = pltpu.VMEM((ns, LANES), F32)
    st_pair = pl.BlockSpec((c // 2, ns, 2 * LANES), lambda i: (i, 0, 0))
    return pl.pallas_call(
        body, grid=(nc,), in_specs=[row] * 6,
        out_specs=[row, st, st, st_pair, pl.BlockSpec((ns, LANES), lambda i: (0, 0))],
        out_shape=[SDS((length, RW), F32)] + [SDS((length, ns, LANES), F32)] * 2 + [SDS((length // 2, ns, 2 * LANES), F32)]
        + [SDS((ns, LANES), F32)],
        scratch_shapes=[state] * 2 + [pltpu.VMEM((c, ns, 2 * LANES), BF16), pltpu.VMEM((c // 2, ns, 2 * LANES), F32)]
        + [pltpu.VMEM((c, 1, RW), F32)] * 9,
        compiler_params=_params(), name="wkv_fwd")(r, w, k, v, a, b)


def _wkv_bwd(r, w, k, a, b, dy, sst, sast, vst, fin):
    length = r.shape[0]
    c = min(WKV_CHUNK, length)
    nc = length // c
    ns = 4 * HN

    def body(r_ref, w_ref, k_ref, a_ref, b_ref, dy_ref, sst_ref, sast_ref, vst_ref, snext_ref, fin_ref,
             dr_ref, dw_ref, dk_ref, da_ref, db_ref, dv_ref,
             ds_ref, cur_ref, acc_ref, dyst, lhs_ref, dvsum_ref, rb, wb, kb, ab, bb, dyb, wbb, alb, rhb,
             p_r, p_w, p_k, p_a, p_b):
        @pl.when(pl.program_id(0) == 0)
        def _():
            ds_ref[...] = jnp.zeros_like(ds_ref)

        b_prev = pltpu.roll(b_ref[...], 1, 0)
        o512 = _block_ones(RW, HN)
        _wkv_prebroadcast((r_ref, w_ref, k_ref, a_ref, b_ref, dy_ref, w_ref[...] * b_prev,
                           _segsum_impl(a_ref[...] * b_prev, o512), _segsum_impl(r_ref[...] * b_ref[...], o512)),
                          (rb, wb, kb, ab, bb, dyb, wbb, alb, rhb), c)
        lane_t, diag, ones = _wkv_consts()

        def fill(q, carry):
            for half in range(2):
                lhs_ref[q, :, half * LANES:(half + 1) * LANES] = (_wkv_full(dyb, 2 * q + half) * diag).astype(BF16)
            return carry

        lax.fori_loop(0, c // 2, fill, 0)
        _wkv_pair_sums(lhs_ref, dyst)
        cur_ref[...] = jnp.where(pl.program_id(0) == 0, fin_ref[...], snext_ref[0])

        def pair(p, carry):
            ta = c - 1 - 2 * p
            tb = ta - 1
            s_a, s_b, s_c = cur_ref[...], sst_ref[ta], sst_ref[tb]
            dy_pair = dyst[c // 2 - 1 - p]
            dy_a, dy_b = dy_pair[:, LANES:], dy_pair[:, :LANES]
            ds_a = ds_ref[...] + dy_a * _wkv_full(rb, ta)
            both = _wkv_rowseg(jnp.concatenate([ds_a * _wkv_full(bb, ta), ds_a * _wkv_full(wbb, ta)], axis=0), ones)
            dsa_a = both[:ns]
            ds_b = ds_a * _wkv_full(wb, ta) + dsa_a * _wkv_full(ab, ta) + dy_b * _wkv_full(rb, tb)
            dsa_b = both[ns:] + dsa_a * _wkv_full(alb, ta) + dy_b * _wkv_full(rhb, tb)
            ds_ref[...] = ds_b * _wkv_full(wb, tb) + dsa_b * _wkv_full(ab, tb)
            cur_ref[...] = s_c
            lhs_ref[p, :, 0:LANES] = (ds_a * _wkv_full(kb, ta)).astype(BF16)
            lhs_ref[p, :, LANES:2 * LANES] = (ds_b * _wkv_full(kb, tb)).astype(BF16)
            v_pair = vst_ref[c // 2 - 1 - p]
            for t, s_t, s_prev, ds, dsa, dyy, vv in ((ta, s_a, s_b, ds_a, dsa_a, dy_a, v_pair[:, LANES:]),
                                                     (tb, s_b, s_c, ds_b, dsa_b, dy_b, v_pair[:, :LANES])):
                p_r[t] = _wkv_colsum8(s_t * dyy)
                p_k[t] = _wkv_colsum8(ds * vv)
                p_b[t] = _wkv_colsum8(ds * sast_ref[t])
                p_w[t] = _wkv_colsum8(ds * s_prev)
                p_a[t] = _wkv_colsum8(s_prev * dsa)
            return carry

        lax.fori_loop(0, c // 2, pair, 0)
        _wkv_pair_sums(lhs_ref, dvsum_ref)
        acc_ref[...] = jnp.zeros_like(acc_ref)

        def gather_dv(p, carry):
            both = dvsum_ref[p]
            acc = jnp.where(lane_t == c - 1 - 2 * p, both[:, :LANES], acc_ref[...])
            acc_ref[...] = jnp.where(lane_t == c - 2 - 2 * p, both[:, LANES:], acc)
            return carry

        lax.fori_loop(0, c // 2, gather_dv, 0)
        sel = (lax.broadcasted_iota(jnp.int32, (c, c * SUBLANES), 1) // SUBLANES
               == lax.broadcasted_iota(jnp.int32, (c, c * SUBLANES), 0)).astype(F32)
        for out_ref, part in ((dr_ref, p_r), (dw_ref, p_w), (dk_ref, p_k), (da_ref, p_a), (db_ref, p_b)):
            out_ref[...] = _dot3(sel, part[...].reshape(c * SUBLANES, RW), 1, 0)
        dv_ref[...] = _wkv_transpose_out(acc_ref[...], c, diag)

    rmap = lambda i: nc - 1 - i
    row = pl.BlockSpec((c, RW), lambda i: (rmap(i), 0))
    st = pl.BlockSpec((c, ns, LANES), lambda i: (rmap(i), 0, 0))
    nxt = pl.BlockSpec((1, ns, LANES), lambda i: (jnp.minimum((rmap(i) + 1) * c, length - 1), 0, 0))
    state = pltpu.VMEM((ns, LANES), F32)
    st_pair = pl.BlockSpec((c // 2, ns, 2 * LANES), lambda i: (rmap(i), 0, 0))
    return pl.pallas_call(
        body, grid=(nc,), in_specs=[row] * 6 + [st, st, st_pair, nxt, pl.BlockSpec((ns, LANES), lambda i: (0, 0))],
        out_specs=[row] * 6, out_shape=[SDS((length, RW), F32)] * 6,
        scratch_shapes=[state] * 3 + [pltpu.VMEM((c // 2, ns, 2 * LANES), F32), pltpu.VMEM((c, ns, 2 * LANES), BF16),
                                       pltpu.VMEM((c // 2, ns, 2 * LANES), F32)]
        + [pltpu.VMEM((c, 1, RW), F32)] * 9 + [pltpu.VMEM((c, SUBLANES, RW), F32)] * 5,
        compiler_params=_params(), name="wkv_bwd")(r, w, k, a, b, dy, sst, sast, vst, sst, fin)


def _my_place():
    return lax.axis_index("x"), lax.axis_index("y"), lax.axis_index("c")


def _peer(x, y, c, k):
    return (x ^ ((k >> 2) & 1), y ^ ((k >> 1) & 1), c ^ (k & 1))


def _all_gather(shard, name):
    def body(in_ref, out_ref, send_sems, recv_sems, local_sem):
        x, y, c = _my_place()
        me = 4 * x + 2 * y + c
        mine = pltpu.make_async_copy(in_ref, out_ref.at[me], local_sem)
        mine.start()
        sends = []
        for k in range(1, N_DEV):
            cp = pltpu.make_async_remote_copy(src_ref=in_ref, dst_ref=out_ref.at[me], send_sem=send_sems.at[k - 1],
                                              recv_sem=recv_sems.at[k - 1], device_id=_peer(x, y, c, k), device_id_type=MESH)
            cp.start()
            sends.append(cp)
        for k in range(1, N_DEV):
            px, py, pc = _peer(x, y, c, k)
            pltpu.make_async_remote_copy(src_ref=in_ref, dst_ref=out_ref.at[4 * px + 2 * py + pc], send_sem=send_sems.at[k - 1],
                                         recv_sem=recv_sems.at[k - 1], device_id=(px, py, pc), device_id_type=MESH).wait_recv()
        for cp in sends:
            cp.wait_send()
        mine.wait()

    return pl.pallas_call(
        body, out_shape=SDS((N_DEV, *shard.shape), shard.dtype),
        in_specs=[pl.BlockSpec(memory_space=pl.ANY)], out_specs=pl.BlockSpec(memory_space=pl.ANY),
        scratch_shapes=[pltpu.SemaphoreType.DMA((N_DEV - 1,)), pltpu.SemaphoreType.DMA((N_DEV - 1,)), pltpu.SemaphoreType.DMA],
        name=name)(shard)


_HBM_SPEC = pl.BlockSpec(memory_space=pltpu.HBM)
_SEM_SPEC = pl.BlockSpec(memory_space=pltpu.SEMAPHORE)
_DATAFLOW = pltpu.SideEffectType.DATAFLOW_SIDE_EFFECTING


def _exchange_start(srcs, per_peer, name):
    n = len(srcs)
    lands = [jnp.zeros((N_DEV, *s.shape[-2:]), s.dtype) for s in srcs]

    def body(*refs):
        src_refs, land_refs = refs[:n], refs[n:2 * n]
        send_sems, recv_sems, token = refs[2 * n], refs[2 * n + 1], refs[-1]
        x, y, c = _my_place()
        me = 4 * x + 2 * y + c
        for a in range(n):
            for k in range(1, N_DEV):
                px, py, pc = _peer(x, y, c, k)
                mine = src_refs[a].at[4 * px + 2 * py + pc] if per_peer else src_refs[a]
                pltpu.make_async_remote_copy(src_ref=mine, dst_ref=land_refs[a].at[me], send_sem=send_sems.at[7 * a + k - 1],
                                             recv_sem=recv_sems.at[7 * a + k - 1], device_id=(px, py, pc),
                                             device_id_type=MESH).start()
        token[...] = jnp.zeros_like(token)

    n_sem = (N_DEV - 1) * n
    res = pl.pallas_call(
        body, name=name,
        out_shape=(pltpu.SemaphoreType.DMA((n_sem,)), pltpu.SemaphoreType.DMA((n_sem,)),
                   *[pltpu.HBM(a.shape, a.dtype) for a in srcs + lands], SDS((SUBLANES, LANES), F32)),
        in_specs=(_HBM_SPEC,) * (2 * n),
        out_specs=(_SEM_SPEC, _SEM_SPEC) + (_HBM_SPEC,) * (2 * n) + (pl.BlockSpec(memory_space=pltpu.VMEM),),
        input_output_aliases={i: 2 + i for i in range(2 * n)}, compiler_params=pltpu.CompilerParams(has_side_effects=_DATAFLOW),
    )(*[pltpu.with_memory_space_constraint(a, pltpu.HBM) for a in srcs + lands])
    return res[0], res[1], list(res[2:2 + n]), list(res[2 + n:2 + 2 * n]), res[-1]


def _exchange_wait(started, after, per_peer, name):
    send_sems, recv_sems, src_thrus, land_thrus, _ = started
    n = len(src_thrus)

    def body(*refs):
        src_refs, land_refs = refs[:n], refs[n:2 * n]
        send_sems, recv_sems = refs[2 * n], refs[2 * n + 1]
        x, y, c = _my_place()
        me = 4 * x + 2 * y + c
        for a in range(n):
            for k in range(1, N_DEV):
                px, py, pc = _peer(x, y, c, k)
                mine = src_refs[a].at[me] if per_peer else src_refs[a]
                copy = pltpu.make_async_remote_copy(src_ref=mine, dst_ref=land_refs[a].at[4 * px + 2 * py + pc],
                                                    send_sem=send_sems.at[7 * a + k - 1], recv_sem=recv_sems.at[7 * a + k - 1],
                                                    device_id=(px, py, pc), device_id_type=MESH)
                copy.wait_send()
                copy.wait_recv()

    res = pl.pallas_call(
        body, name=name, out_shape=tuple(pltpu.HBM(a.shape, a.dtype) for a in src_thrus + land_thrus),
        in_specs=(_HBM_SPEC,) * (2 * n) + (_SEM_SPEC, _SEM_SPEC, pl.BlockSpec(memory_space=pl.ANY)),
        out_specs=(_HBM_SPEC,) * (2 * n), input_output_aliases={i: i for i in range(2 * n)},
        compiler_params=pltpu.CompilerParams(has_side_effects=_DATAFLOW),
    )(*src_thrus, *land_thrus, send_sems, recv_sems, after)
    return list(res[n:])


def _row_tile(rows, cap):
    best = None
    for t in range(2 * SUBLANES, min(rows, cap) + 1, 2 * SUBLANES):
        if rows % t == 0:
            best = t
    return best or rows


def _sum_parts(parts, name, own=None):
    rows, cols = parts.shape[1:]

    def body(*refs):
        p_ref, g_out = refs[0], refs[-1]
        g = p_ref[0].astype(F32)
        for s in range(1, N_DEV):
            g = g + p_ref[s].astype(F32)
        if own is not None:
            g = g + refs[1][...].astype(F32)
        g_out[...] = g

    tr = _row_tile(rows, 384 * 1024 // cols)
    blk = pl.BlockSpec((tr, cols), lambda i: (i, 0))
    extra = [] if own is None else [own]
    return pl.pallas_call(
        body, grid=(rows // tr,),
        in_specs=[pl.BlockSpec((N_DEV, tr, cols), lambda i: (0, i, 0))] + [blk] * len(extra),
        out_specs=blk, out_shape=SDS((rows, cols), F32),
        compiler_params=_params(("parallel",)), name=name)(parts, *extra)


def _adamw(g, wgt, m, v, name):
    rows, cols = wgt.shape

    def body(g_ref, w_ref, m_ref, v_ref, d_out, m_out, v_out):
        g_ = g_ref[...]
        m_new = ADAM_B1 * m_ref[...] + (1.0 - ADAM_B1) * g_
        v_new = ADAM_B2 * v_ref[...] + (1.0 - ADAM_B2) * (g_ * g_)
        m_hat = m_new / (1.0 - ADAM_B1 ** ADAM_STEP)
        v_hat = v_new / (1.0 - ADAM_B2 ** ADAM_STEP)
        d_out[...] = -ADAM_LR * (m_hat / (jnp.sqrt(v_hat) + ADAM_EPS) + ADAM_WD * w_ref[...])
        m_out[...] = m_new
        v_out[...] = v_new

    tr = _row_tile(rows, 512 * 1024 // cols)
    blk = pl.BlockSpec((tr, cols), lambda i: (i, 0))
    return pl.pallas_call(
        body, grid=(rows // tr,), in_specs=[blk] * 4, out_specs=[blk] * 3, out_shape=[SDS((rows, cols), F32)] * 3,
        compiler_params=_params(("parallel",)), name=name)(g, wgt, m, v)


PACK_ALIGN = 2 * SUBLANES * LANES
PACK_ROWS = 512

ROW_SHARDED = ("s5_w_glu", "w_out", "ffn_w_down")
TRANSPOSED = ("w_in", "ffn_w_up", "w_branch_rwkv", "w_branch_s5")
SMALL_SHARDED = ("rwkv_w2", "rwkv_a2", "rwkv_g2", "ffn_conv_w")
SHARDED = ROW_SHARDED + TRANSPOSED + SMALL_SHARDED
F32_GATHER = ("ffn_conv_w",)
REPLICATED =("norm_mix_pre", "norm_mix_post", "norm_ffn_pre", "norm_ffn_post", "b_gate", "rwkv_shift_mu", "rwkv_w0",
              "rwkv_a0", "rwkv_k_k", "rwkv_k_a", "rwkv_r_k", "rwkv_lnx_w", "rwkv_lnx_b", "s5_a_re", "s5_a_im", "s5_b_re",
              "s5_b_im", "s5_c_re", "s5_c_im", "s5_d", "s5_log_step", "s5_b_glu", "ffn_conv_b")
WEIGHTS = ("norm_mix_pre", "norm_mix_post", "norm_ffn_pre", "norm_ffn_post", "w_in", "b_gate", "rwkv_shift_mu", "rwkv_w0",
           "rwkv_w2", "rwkv_a0", "rwkv_a2", "rwkv_g2", "rwkv_k_k", "rwkv_k_a", "rwkv_r_k", "rwkv_lnx_w", "rwkv_lnx_b",
           "s5_a_re", "s5_a_im", "s5_b_re", "s5_b_im", "s5_c_re", "s5_c_im", "s5_d", "s5_log_step", "s5_w_glu", "s5_b_glu",
           "w_branch_rwkv", "w_branch_s5", "w_out", "ffn_w_up", "ffn_conv_w", "ffn_conv_b", "ffn_w_down")


def _pad_flat(a):
    flat = a.reshape(-1)
    pad = (-flat.shape[0]) % PACK_ALIGN
    return jnp.pad(flat, (0, pad)) if pad else flat


def _pad_cols(a, mult):
    pad = (-a.shape[1]) % mult
    return jnp.pad(a, ((0, 0), (0, pad))) if pad else a


def _pack(pieces):
    flat = jnp.concatenate([_pad_flat(p) for p in pieces])
    pad = (-flat.shape[0]) % (PACK_ROWS * LANES)
    return (jnp.pad(flat, (0, pad)) if pad else flat).reshape(-1, LANES)


def _unpack(buf, shapes):
    lead = buf.shape[:-2]
    flat = buf.reshape(*lead, -1)
    out, off = [], 0
    for s in shapes:
        n = math.prod(s)
        out.append(flat[..., off:off + n].reshape(*lead, *s))
        off += n + (-n) % PACK_ALIGN
    return out


def _rows_to_full(stack):
    return stack.reshape(-1, stack.shape[2])


def _cols_to_full(stack):
    return stack.transpose(1, 0, 2).reshape(stack.shape[1], -1)


EARLY_GRADS = ("ffn_w_down", "ffn_w_up", "w_out")
REST_GRADS = ("s5_w_glu", "w_in", "w_branch_rwkv", "w_branch_s5")


def _local_step(x, tgt, wt, get_late, send_early, send_mid, send_rest):
    wt = dict(wt)
    o512 = _block_ones(RW, HN)
    w_in = wt["w_in"]
    w_rw, w_u, w_g = w_in[:N_RWKV], w_in[N_RWKV:N_RWKV + S5W], w_in[N_RWKV + S5W:]
    b_gate = wt["b_gate"]
    b_r, b_s = b_gate[:, :D_MODEL], b_gate[:, D_MODEL:]
    zpad = lambda a, lo, n: jnp.pad(a.astype(F32), ((lo, n - lo - a.shape[0]), (0, 0)))
    w2p, a2p, g2p = zpad(wt["rwkv_w2"], 0, 256), zpad(wt["rwkv_a2"], 64, 256), zpad(wt["rwkv_g2"], 128, 256)
    pre_small = [wt["rwkv_w0"], wt["rwkv_a0"], wt["rwkv_k_k"], wt["rwkv_k_a"], w2p, a2p, g2p]
    post_prm = [wt["rwkv_lnx_w"], wt["rwkv_lnx_b"], wt["rwkv_r_k"].reshape(1, RW)]
    mu = wt["rwkv_shift_mu"]

    a_re, a_im = wt["s5_a_re"].reshape(1, S5N), wt["s5_a_im"].reshape(1, S5N)
    ls = jnp.repeat(wt["s5_log_step"].reshape(S5G), S5P).reshape(1, S5N)
    b_re_t = wt["s5_b_re"].reshape(S5N, S5C).T
    b_im_t = wt["s5_b_im"].reshape(S5N, S5C).T
    c_re, c_im = wt["s5_c_re"].reshape(S5G, S5C, S5P), wt["s5_c_im"].reshape(S5G, S5C, S5P)
    abr, abi, bbr, bbi = _s5_prep_fwd(a_re, a_im, ls, b_re_t, b_im_t)
    bblk = jnp.concatenate([_s5_blockdiag_in(bbr), _s5_blockdiag_in(bbi)], axis=2)
    cblk = jnp.concatenate([_s5_blockdiag_out(c_re), -_s5_blockdiag_out(c_im)], axis=1)
    s5_d = wt["s5_d"]

    h1 = _rms_fwd(x, wt["norm_mix_pre"], "rms1_fwd")
    p_rw = _mm(h1, w_rw, tb=True, name="mm_proj_rwkv")
    u = _mm(h1, w_u, tb=True, name="mm_proj_s5")
    gpre = _mm(h1, w_g, tb=True, name="mm_proj_gate")
    r, decay, k2, v, aa, bb, g = _rwkv_pre_fwd(p_rw, mu, pre_small, o512)
    y, sst, sast, vst, s_fin = _wkv_fwd(r, decay, k2, v, aa, bb)
    wt.update(get_late(y))
    y_r = _rwkv_post_fwd(y, r, k2, v, g, post_prm, o512)
    o_r = _mm(y_r, wt["w_branch_rwkv"], tb=True, name="mm_branch_rwkv")
    ys, kre, kim = _s5_fwd(u, bblk, cblk, abr, abi, s5_d)
    t_glu = _mm(ys, wt["s5_w_glu"], name="mm_glu")
    out5 = _glu_fwd(ys, t_glu, wt["s5_b_glu"])
    o_s = _mm(out5, wt["w_branch_s5"], tb=True, name="mm_branch_s5")
    mi = _merge_fwd(gpre, o_r, o_s, b_r, b_s)
    mixed = _mm(mi, wt["w_out"], name="mm_out")
    x2, h2 = _mid_fwd(x, mixed, wt["norm_mix_post"], wt["norm_ffn_pre"])
    z = _mm(h2, wt["ffn_w_up"], tb=True, name="mm_up")
    act = _conv_fwd(z, wt["ffn_conv_w"], wt["ffn_conv_b"])
    f = _mm(act, wt["ffn_w_down"], name="mm_down")
    dx3, df, d_norm_ffn_post, loss_part = _final(x2, f, tgt, wt["norm_ffn_post"])

    gr = {"norm_ffn_post": d_norm_ffn_post}
    dact = _mm(df, wt["ffn_w_down"], tb=True, name="mm_down_dx")
    gr["ffn_w_down"] = _mm(act, df, ta=True, out_dtype=BF16, name="mm_down_dw")
    dzg, dzv, gr["ffn_conv_w"], gr["ffn_conv_b"] = _conv_bwd(z, dact, wt["ffn_conv_w"], wt["ffn_conv_b"])
    dz = jnp.concatenate([dzg, dzv], axis=1)
    dh2 = _mm(dz, wt["ffn_w_up"], name="mm_up_dx")
    gr["ffn_w_up"] = _mm(dz, h2, ta=True, out_dtype=BF16, name="mm_up_dw")
    dx2, dmixed, gr["norm_mix_post"], gr["norm_ffn_pre"] = _mid_bwd(x2, mixed, dh2, dx3, wt["norm_mix_post"], wt["norm_ffn_pre"])
    dmi = _mm(dmixed, wt["w_out"], tb=True, name="mm_out_dx")
    gr["w_out"] = _mm(mi, dmixed, ta=True, out_dtype=BF16, name="mm_out_dw")
    token = send_early({n: gr[n] for n in EARLY_GRADS})
    b_r = b_r + jnp.tile(token[0:1, :], (1, D_MODEL // LANES))
    dgp_r, dgp_s, do_r, do_s, db_r, db_s = _merge_bwd(gpre, o_r, o_s, dmi, b_r, b_s)
    gr["b_gate"] = jnp.concatenate([db_r, db_s], axis=1)
    dout5 = _mm(do_s, wt["w_branch_s5"], name="mm_branch_s5_dx")
    gr["w_branch_s5"] = _mm(do_s, out5, ta=True, out_dtype=BF16, name="mm_branch_s5_dw")
    dys_a, dt_glu, gr["s5_b_glu"] = _glu_bwd(ys, t_glu, dout5, wt["s5_b_glu"])
    dys_b = _mm(dt_glu, wt["s5_w_glu"], tb=True, name="mm_glu_dx")
    gr["s5_w_glu"] = _mm(ys, dt_glu, ta=True, out_dtype=BF16, name="mm_glu_dw")
    du, dbblk, dcblk, dabr, dabi, gr["s5_d"] = _s5_bwd(u, dys_a, dys_b, kre, kim, bblk, cblk, abr, abi, s5_d)
    gr["s5_c_re"] = _s5_blockdiag_out_t(dcblk[:, :S5_BN, :]).reshape(wt["s5_c_re"].shape)
    gr["s5_c_im"] = (-_s5_blockdiag_out_t(dcblk[:, S5_BN:, :])).reshape(wt["s5_c_im"].shape)
    dbbr, dbbi = _s5_blockdiag_in_t(dbblk[:, :, :S5_BN]), _s5_blockdiag_in_t(dbblk[:, :, S5_BN:])
    gsel = (lax.broadcasted_iota(jnp.int32, (S5N, LANES), 0) // S5P == lax.broadcasted_iota(jnp.int32, (S5N, LANES), 1)).astype(F32)
    d_are, d_aim, d_ls, d_bre_t, d_bim_t = _s5_prep_bwd(a_re, a_im, ls, b_re_t, b_im_t, (dabr, dabi, dbbr, dbbi), gsel)
    gr["s5_a_re"] = d_are.reshape(wt["s5_a_re"].shape)
    gr["s5_a_im"] = d_aim.reshape(wt["s5_a_im"].shape)
    gr["s5_log_step"] = d_ls[0:1, :S5G]
    gr["s5_b_re"] = d_bre_t.T.reshape(wt["s5_b_re"].shape)
    gr["s5_b_im"] = d_bim_t.T.reshape(wt["s5_b_im"].shape)
    dy_r = _mm(do_r, wt["w_branch_rwkv"], name="mm_branch_rwkv_dx")
    gr["w_branch_rwkv"] = _mm(do_r, y_r, ta=True, out_dtype=BF16, name="mm_branch_rwkv_dw")
    token = send_mid(gr)
    post_prm = [post_prm[0] + jnp.tile(token[0:1, :], (1, RW // LANES))] + post_prm[1:]
    dy, dr1, dk1, dv1, dg, gr["rwkv_lnx_w"], gr["rwkv_lnx_b"], d_rk = _rwkv_post_bwd(y, r, k2, v, g, dy_r, post_prm, o512)
    gr["rwkv_r_k"] = d_rk.reshape(wt["rwkv_r_k"].shape)
    dr2, ddecay, dk2, daa, dbb, dv2 = _wkv_bwd(r, decay, k2, aa, bb, dy, sst, sast, vst, s_fin)
    cots = [dr1, ddecay, dk1, dv1, daa, dbb, dg, dr2, dk2, dv2]
    dp_rw, gr["rwkv_shift_mu"], gr["rwkv_w0"], gr["rwkv_a0"], gr["rwkv_k_k"], gr["rwkv_k_a"], dw2p, da2p, dg2p = \
        _rwkv_pre_bwd(p_rw, cots, mu, pre_small, o512)
    gr["rwkv_w2"], gr["rwkv_a2"], gr["rwkv_g2"] = dw2p[:64], da2p[64:128], dg2p[128:]
    dproj = jnp.concatenate([dp_rw, du, dgp_r, dgp_s], axis=1)
    gr["w_in"] = _mm(dproj, h1, ta=True, out_dtype=BF16, name="mm_proj_dw")
    token = send_rest(gr)
    dh1 = _mm(dproj, w_in + token[0, 0].astype(w_in.dtype), name="mm_proj_dx")
    dx, gr["norm_mix_pre"] = _rms_bwd(x, wt["norm_mix_pre"], dh1, dx2, "rms1_bwd")
    return loss_part[0, 0], dx, gr


def kernel(x, norm_mix_pre, norm_mix_post, norm_ffn_pre, norm_ffn_post, w_in, b_gate, rwkv_shift_mu, rwkv_w0, rwkv_w2, rwkv_a0, rwkv_a2, rwkv_g2, rwkv_k_k, rwkv_k_a, rwkv_r_k, rwkv_lnx_w, rwkv_lnx_b, s5_a_re, s5_a_im, s5_b_re, s5_b_im, s5_c_re, s5_c_im, s5_d, s5_log_step, s5_w_glu, s5_b_glu, w_branch_rwkv, w_branch_s5, w_out, ffn_w_up, ffn_conv_w, ffn_conv_b, ffn_w_down, loss_target, m_norm_mix_pre, m_norm_mix_post, m_norm_ffn_pre, m_norm_ffn_post, m_w_in, m_b_gate, m_rwkv_shift_mu, m_rwkv_w0, m_rwkv_w2, m_rwkv_a0, m_rwkv_a2, m_rwkv_g2, m_rwkv_k_k, m_rwkv_k_a, m_rwkv_r_k, m_rwkv_lnx_w, m_rwkv_lnx_b, m_s5_a_re, m_s5_a_im, m_s5_b_re, m_s5_b_im, m_s5_c_re, m_s5_c_im, m_s5_d, m_s5_log_step, m_s5_w_glu, m_s5_b_glu, m_w_branch_rwkv, m_w_branch_s5, m_w_out, m_ffn_w_up, m_ffn_conv_w, m_ffn_conv_b, m_ffn_w_down, v_norm_mix_pre, v_norm_mix_post, v_norm_ffn_pre, v_norm_ffn_post, v_w_in, v_b_gate, v_rwkv_shift_mu, v_rwkv_w0, v_rwkv_w2, v_rwkv_a0, v_rwkv_a2, v_rwkv_g2, v_rwkv_k_k, v_rwkv_k_a, v_rwkv_r_k, v_rwkv_lnx_w, v_rwkv_lnx_b, v_s5_a_re, v_s5_a_im, v_s5_b_re, v_s5_b_im, v_s5_c_re, v_s5_c_im, v_s5_d, v_s5_log_step, v_s5_w_glu, v_s5_b_glu, v_w_branch_rwkv, v_w_branch_s5, v_w_out, v_ffn_w_up, v_ffn_conv_w, v_ffn_conv_b, v_ffn_w_down):
    args = dict(locals())
    wgt = {n: args[n] for n in WEIGHTS}
    mom = {n: args["m_" + n] for n in WEIGHTS}
    var = {n: args["v_" + n] for n in WEIGHTS}
    me = 4 * lax.axis_index("x") + 2 * lax.axis_index("y") + lax.axis_index("c")
    shard_shapes = {n: wgt[n].shape[1:] for n in SHARDED}
    sent_shapes = {n: (shard_shapes[n][::-1] if n in TRANSPOSED else shard_shapes[n]) for n in SHARDED}
    sent = lambda n, a: a.T if n in TRANSPOSED else a

    unshard = lambda n, blocks: _cols_to_full(blocks) if n in SMALL_SHARDED else _rows_to_full(blocks)

    first = ("w_in", "rwkv_w2", "rwkv_a2", "rwkv_g2")
    late = tuple(n for n in SHARDED if n not in first and n not in F32_GATHER)
    late_payloads = [sent(n, wgt[n][0].astype(BF16)) for n in late] + [_pack([wgt[n][0] for n in F32_GATHER])]
    late_started = _exchange_start(late_payloads, False, "gather_late_start")
    got_first = _unpack(_all_gather(_pack([sent(n, wgt[n][0].astype(BF16)) for n in first]), "all_gather_weights"),
                        [sent_shapes[n] for n in first])
    wt = {n: wgt[n] for n in REPLICATED}
    for n, blocks in zip(first, got_first, strict=True):
        wt[n] = unshard(n, blocks)

    def get_late(after):
        lands = _exchange_wait(late_started, after, False, "gather_late_wait")
        whole_ = [lax.dynamic_update_index_in_dim(land, mine, me, 0) for land, mine in zip(lands, late_payloads, strict=True)]
        out = {n: _rows_to_full(blocks) for n, blocks in zip(late, whole_[:-1], strict=True)}
        taps = _unpack(whole_[-1], [sent_shapes[n] for n in F32_GATHER])
        out.update({n: unshard(n, b_) for n, b_ in zip(F32_GATHER, taps, strict=True)})
        return out

    lowrank = ("rwkv_w2", "rwkv_a2", "rwkv_g2")
    rest_names = REST_GRADS + lowrank
    whole = tuple(n for n in ("ffn_conv_w",) + REPLICATED if n != "norm_mix_pre")
    whole_mid = tuple(n for n in whole if not n.startswith("rwkv_"))
    whole_late = tuple(n for n in whole if n.startswith("rwkv_"))
    blocks_of = lambda g_: g_.reshape(N_DEV, -1, g_.shape[-1])
    as_sent = lambda n, g_: blocks_of(g_.T if n in lowrank else g_)
    started = {}

    def send_early(grads_):
        started["early"] = _exchange_start([as_sent(n, grads_[n]) for n in EARLY_GRADS], True, "exchange_early_start")
        return started["early"][4]

    def send_whole(key, names):
        def send(grads_):
            started[key + "_mine"] = _pack([grads_[n] for n in names])
            started[key] = _exchange_start([started[key + "_mine"]], False, "gather_" + key + "_start")
            return started[key][4]
        return send

    def send_rest(grads_):
        started["rest"] = _exchange_start([as_sent(n, grads_[n]) for n in rest_names], True, "exchange_rest_start")
        return started["rest"][4] + send_whole("late", whole_late)(grads_)

    loss_part, dx, gr = _local_step(x[0], loss_target[0], wt, get_late, send_early, send_whole("mid", whole_mid), send_rest)

    grads, upd = {}, {}
    flat2d = lambda a: a.reshape(a.shape[-2:])

    def finish_large(key, names, after):
        lands = _exchange_wait(started[key], after, True, "exchange_" + key + "_wait")
        for n, land in zip(names, lands, strict=True):
            own = lax.dynamic_index_in_dim(as_sent(n, gr[n]), me, 0, keepdims=False)
            total = _sum_parts(land, "sum_" + n, own=own)
            grads[n] = (total.T if n in TRANSPOSED + lowrank else total).reshape(wgt[n].shape)
            trio = _adamw(flat2d(grads[n]), flat2d(wgt[n]), flat2d(mom[n]), flat2d(var[n]), "adamw_" + n)
            upd[n] = tuple(t.reshape(wgt[n].shape) for t in trio)
        return upd[names[-1]][0]

    got = {}

    def finish_whole(key, names, after):
        land, = _exchange_wait(started[key], after, False, "gather_" + key + "_wait")
        total = _sum_parts(lax.dynamic_update_index_in_dim(land, started[key + "_mine"], me, 0), "sum_" + key + "_grads")
        got.update(zip(names, _unpack(total, [gr[n].shape for n in names]), strict=True))
        return total

    done = finish_large("early", EARLY_GRADS, dx)
    done = finish_whole("mid", whole_mid, done)
    done = finish_large("rest", rest_names, done)
    done = finish_whole("late", whole_late, done)
    last, _ = lax.optimization_barrier((gr["norm_mix_pre"].reshape(SUBLANES, LANES), done))
    got["norm_mix_pre"] = _sum_parts(_all_gather(last, "gather_last_grad"), "sum_last_grad").reshape(gr["norm_mix_pre"].shape)
    packed = tuple(n for n in WEIGHTS if n not in EARLY_GRADS + rest_names)
    for n in packed:
        if n in SMALL_SHARDED:
            cols = shard_shapes[n][1]
            grads[n] = lax.dynamic_slice_in_dim(got[n], me * cols, cols, axis=1).reshape(wgt[n].shape)
        else:
            grads[n] = got[n].reshape(wgt[n].shape)

    pack_local = lambda src: _pack([src[n].reshape(-1) for n in packed])
    outs = _adamw(pack_local(grads), pack_local(wgt), pack_local(mom), pack_local(var), "adamw_small")
    shapes = [wgt[n].shape for n in packed]
    upd.update(zip(packed, zip(*[_unpack(o, shapes) for o in outs], strict=True), strict=True))
    loss = lax.psum(loss_part, AXES)
    return (loss, dx[None], *[grads[n] for n in WEIGHTS], *[upd[n][0] for n in WEIGHTS], *[upd[n][1] for n in WEIGHTS],
            *[upd[n][2] for n in WEIGHTS])
```

```python
import functools
import math

import jax
import jax.numpy as jnp
from jax import lax
from jax.experimental import pallas as pl
from jax.experimental.pallas import tpu as pltpu

F32 = jnp.float32
BF16 = jnp.bfloat16
SDS = jax.ShapeDtypeStruct
HI = lax.Precision.HIGHEST
MESH = pl.DeviceIdType.MESH
AXES = ("x", "y", "c")
N_DEV = 8

D_MODEL = 1024
RW = 512
HN = 64
N_RWKV = 1792
S5W = 512
S5G = 32
S5C = 16
S5P = 64
S5N = S5G * S5P
D_FF = 2816
NORM_EPS = 1e-6
LNX_EPS = 64e-5

ADAM_LR = 0.001
ADAM_B1 = 0.9
ADAM_B2 = 0.999
ADAM_EPS = 1e-08
ADAM_WD = 0.01
ADAM_STEP = 10

LANES = 128
SUBLANES = 8
VMEM_LIMIT = 56 * 1024 * 1024
WKV_CHUNK = 32


def _params(sem=("arbitrary",)):
    return pltpu.CompilerParams(dimension_semantics=sem, vmem_limit_bytes=VMEM_LIMIT)


def _pick(n, cap):
    best = None
    for t in range(LANES, min(n, cap) + 1, LANES):
        if n % t == 0:
            best = t
    return best or n


MM_VMEM_BUDGET = 40 * 1024 * 1024


def _mm_tiles(m, n, k, a_bytes, b_bytes, o_bytes):
    def divisors(d):
        return [t for t in range(LANES, d + 1, LANES) if d % t == 0] or [d]

    tn = _pick(n, 2304)
    best = None
    for tm in divisors(m):
        for tk in divisors(k):
            vmem = 2 * (tm * tk * a_bytes + tk * tn * b_bytes) + tm * tn * 4 + 2 * tm * tn * o_bytes
            if vmem > MM_VMEM_BUDGET:
                continue
            key = ((m // tm) * (n // tn) * (k // tk), -tm)
            if best is None or key < best[0]:
                best = (key, tm, tk)
    assert best is not None, (m, n, k)
    return best[1], tn, best[2]


def _mm(a, b, *, ta=False, tb=False, out_dtype=F32, name):
    m = a.shape[1] if ta else a.shape[0]
    k = a.shape[0] if ta else a.shape[1]
    n = b.shape[0] if tb else b.shape[1]
    assert (b.shape[1] if tb else b.shape[0]) == k
    tm, tn, tk = _mm_tiles(m, n, k, a.dtype.itemsize, b.dtype.itemsize, jnp.dtype(out_dtype).itemsize)
    nk = k // tk
    dims = (((0 if ta else 1,), (1 if tb else 0,)), ((), ()))

    def body(a_ref, b_ref, o_ref, acc_ref):
        kk = pl.program_id(2)

        @pl.when(kk == 0)
        def _():
            acc_ref[...] = jnp.zeros_like(acc_ref)

        acc_ref[...] += lax.dot_general(a_ref[...].astype(BF16), b_ref[...].astype(BF16), dims,
                                        preferred_element_type=F32)

        @pl.when(kk == nk - 1)
        def _():
            o_ref[...] = acc_ref[...].astype(o_ref.dtype)

    a_spec = pl.BlockSpec((tk, tm), lambda i, j, kk: (kk, i)) if ta else pl.BlockSpec((tm, tk), lambda i, j, kk: (i, kk))
    b_spec = pl.BlockSpec((tn, tk), lambda i, j, kk: (j, kk)) if tb else pl.BlockSpec((tk, tn), lambda i, j, kk: (kk, j))
    return pl.pallas_call(
        body, grid=(m // tm, n // tn, nk), in_specs=[a_spec, b_spec],
        out_specs=pl.BlockSpec((tm, tn), lambda i, j, kk: (i, j)),
        out_shape=SDS((m, n), out_dtype), scratch_shapes=[pltpu.VMEM((tm, tn), F32)],
        compiler_params=_params(("parallel", "parallel", "arbitrary")), name=name)(a, b)


def _rows(fn, rows, params, out_rows, out_accs, *, name, tl, reverse=False, scratch=()):
    first = rows[0][0] if isinstance(rows[0], tuple) else rows[0]
    length = first.shape[0]
    tl = min(tl, length)
    nt = length // tl
    rmap = (lambda i: nt - 1 - i) if reverse else (lambda i: i)
    specs, arrs = [], []
    for r in rows:
        arr, wdt, cb = r if isinstance(r, tuple) else (r, r.shape[1], 0)
        specs.append(pl.BlockSpec((tl, wdt), lambda i, cb=cb: (rmap(i), cb)))
        arrs.append(arr)
    for p in params:
        specs.append(pl.BlockSpec(p.shape, lambda i, nd=p.ndim: (0,) * nd))
        arrs.append(p)
    out_shape = [SDS((length, c), dt) for c, dt in out_rows] + [SDS(s, F32) for s in out_accs]
    out_specs = [pl.BlockSpec((tl, c), lambda i: (rmap(i), 0)) for c, _ in out_rows]
    out_specs += [pl.BlockSpec(s, lambda i, nd=len(s): (0,) * nd) for s in out_accs]
    nr, npar, nor, noa = len(rows), len(params), len(out_rows), len(out_accs)

    def body(*refs):
        rin, pin = refs[:nr], refs[nr:nr + npar]
        rout = refs[nr + npar:nr + npar + nor]
        aout = refs[nr + npar + nor:nr + npar + nor + noa]
        scr = refs[nr + npar + nor + noa:]
        step = pl.program_id(0)
        outs_r, outs_a = fn(step, [r[...] for r in rin], [p[...] for p in pin], scr)
        for ref, val in zip(rout, outs_r, strict=True):
            ref[...] = val.astype(ref.dtype)

        @pl.when(step == 0)
        def _():
            for ref in aout:
                ref[...] = jnp.zeros_like(ref)

        for ref, val in zip(aout, outs_a, strict=True):
            ref[...] += val.astype(F32)

    res = pl.pallas_call(body, grid=(nt,), in_specs=specs, out_specs=out_specs, out_shape=out_shape,
                         scratch_shapes=list(scratch), compiler_params=_params(), name=name)(*arrs)
    return list(res)


def _rms(x, g):
    return x * lax.rsqrt(jnp.mean(x * x, axis=-1, keepdims=True) + NORM_EPS) * g


def _sig(x):
    return 0.5 * (jnp.tanh(0.5 * x) + 1.0)


def _softplus(x):
    return jnp.maximum(x, 0.0) + jnp.log(1.0 + jnp.exp(-jnp.abs(x)))


def _gelu(x):
    return x * (0.5 * (1.0 + jnp.tanh(math.sqrt(2.0 / math.pi) * (x + 0.044715 * (x * x * x)))))


def _bdot(a, b):
    return jnp.dot(a.astype(BF16), b.astype(BF16), preferred_element_type=F32)


def _hdot(a, b):
    return jnp.dot(a, b, precision=HI, preferred_element_type=F32)


def _segsum_impl(x, ones):
    hi = x.astype(BF16)
    lo = (x - hi.astype(F32)).astype(BF16)
    ones = ones.astype(BF16)
    return jnp.dot(jnp.concatenate([hi, lo], axis=1), jnp.concatenate([ones, ones], axis=0), preferred_element_type=F32)


@jax.custom_vjp
def _segsum(x, ones):
    return _segsum_impl(x, ones)


_segsum.defvjp(lambda x, ones: (_segsum_impl(x, ones), ones),
               lambda ones, g: (_segsum_impl(g, ones), jnp.zeros_like(ones)))


def _block_ones(n, blk):
    i = lax.broadcasted_iota(jnp.int32, (n, n), 0) // blk
    j = lax.broadcasted_iota(jnp.int32, (n, n), 1) // blk
    return (i == j).astype(F32)


def _rms_fwd(x, g, name):
    return _rows(lambda s, r, p, _: ([_rms(r[0], p[0])], []), [x], [g], [(x.shape[1], BF16)], [], name=name, tl=512)[0]


def _rms_bwd(x, g, dh, dres, name):
    def fn(s, r, p, _):
        _, vjp = jax.vjp(_rms, r[0], p[0])
        dx, dg = vjp(r[1])
        return [dx + r[2]], [dg]
    return _rows(fn, [x, dh, dres], [g], [(x.shape[1], F32)], [g.shape], name=name, tl=256)


def _rwkv_pre_math(k_, lr, w0, a0, k_k, k_a, w2p, a2p, g2p, o512):
    pre_w = w0 + _bdot(jnp.tanh(lr), w2p)
    w = -_softplus(-pre_w) - 0.5
    decay = jnp.exp(-jnp.exp(w))
    a = _sig(a0 + _bdot(lr, a2p))
    g = _bdot(_sig(lr), g2p)
    kr = k_ * k_k
    kk = kr / jnp.maximum(jnp.sqrt(_segsum(kr * kr, o512)), 1e-12)
    k2 = k_ * (1.0 + (a - 1.0) * k_a)
    return decay, k2, -kk, kk * a, g


def _shift_down(p, prev_row):
    row = lax.broadcasted_iota(jnp.int32, p.shape, 0)
    return jnp.where(row == 0, jnp.broadcast_to(prev_row, p.shape), pltpu.roll(p, 1, 0))


def _shift_up(q, next_row):
    n = q.shape[0]
    row = lax.broadcasted_iota(jnp.int32, q.shape, 0)
    return jnp.where(row == n - 1, jnp.broadcast_to(next_row, q.shape), pltpu.roll(q, n - 1, 0))


def _rwkv_pre_fwd(p, mu, small, o512):
    def fn(step, r, prm, scr):
        car = scr[0]

        @pl.when(step == 0)
        def _():
            car[...] = jnp.zeros_like(car)

        x = r[0]
        prev = _shift_down(x, car[SUBLANES - 1:SUBLANES, :])
        car[...] = x[x.shape[0] - SUBLANES:, :]
        xs = x + (prev - x) * prm[0]
        decay, k2, aa, bb, g = _rwkv_pre_math(xs[:, RW:2 * RW], xs[:, 3 * RW:], *prm[1:])
        return [xs[:, :RW], decay, k2, xs[:, 2 * RW:3 * RW], aa, bb, g], []
    return _rows(fn, [p], [mu, *small, o512], [(RW, F32)] * 7, [], name="rwkv_pre_fwd", tl=256,
                 scratch=[pltpu.VMEM((SUBLANES, N_RWKV), F32)])


def _rwkv_pre_bwd(p, cots, mu, small, o512):
    length = p.shape[0]
    tl = min(256, length)
    nt = length // tl
    rows_per = tl // SUBLANES
    params = [mu, *small, o512]
    acc_shapes = [mu.shape] + [q.shape for q in small]
    nr, npar, nacc = 2 + len(cots), len(params), len(acc_shapes)

    def body(*refs):
        rin, pin = refs[:nr], refs[nr:nr + npar]
        dp_ref = refs[nr + npar]
        aout = refs[nr + npar + 1:nr + npar + 1 + nacc]
        car_q = refs[nr + npar + 1 + nacc]
        step = pl.program_id(0)

        @pl.when(step == 0)
        def _():
            car_q[...] = jnp.zeros_like(car_q)
            for ref in aout:
                ref[...] = jnp.zeros_like(ref)

        x = rin[0][...]
        prev_row = jnp.where(step == nt - 1, 0.0, rin[1][SUBLANES - 1:SUBLANES, :])
        dr, ddecay, dk2, dv, daa, dbb, dg, dr_b, dk2_b, dv_b = [r[...] for r in rin[2:]]
        dr, dk2, dv = dr + dr_b, dk2 + dk2_b, dv + dv_b
        prm = [q[...] for q in pin]
        mu_, o512_ = prm[0], prm[-1]
        prev = _shift_down(x, prev_row)
        xs = x + (prev - x) * mu_
        _, vjp = jax.vjp(lambda k_, lr, *w: _rwkv_pre_math(k_, lr, *w, o512_), xs[:, RW:2 * RW], xs[:, 3 * RW:], *prm[1:-1])
        dk_, dlr, *dsmall = vjp((ddecay, dk2, daa, dbb, dg))
        dxs = jnp.concatenate([dr, dk_, dv, dlr], axis=1)
        q = dxs * mu_
        dp_ref[...] = (dxs - q + _shift_up(q, car_q[0:1, :])).astype(dp_ref.dtype)
        car_q[...] = q[:SUBLANES, :]
        aout[0][...] += jnp.sum((prev - x) * dxs, axis=0, keepdims=True)
        for ref, val in zip(aout[1:], dsmall, strict=True):
            ref[...] += val

    rmap = lambda i: nt - 1 - i
    specs = [pl.BlockSpec((tl, N_RWKV), lambda i: (rmap(i), 0)),
             pl.BlockSpec((SUBLANES, N_RWKV), lambda i: (jnp.maximum(rmap(i) * rows_per - 1, 0), 0))]
    specs += [pl.BlockSpec((tl, RW), lambda i: (rmap(i), 0)) for _ in cots]
    specs += [pl.BlockSpec(q.shape, lambda i, nd=q.ndim: (0,) * nd) for q in params]
    out_shape = [SDS((length, N_RWKV), BF16)] + [SDS(sh, F32) for sh in acc_shapes]
    out_specs = [pl.BlockSpec((tl, N_RWKV), lambda i: (rmap(i), 0))]
    out_specs += [pl.BlockSpec(sh, lambda i, nd=len(sh): (0,) * nd) for sh in acc_shapes]
    res = pl.pallas_call(body, grid=(nt,), in_specs=specs, out_specs=out_specs, out_shape=out_shape,
                         scratch_shapes=[pltpu.VMEM((SUBLANES, N_RWKV), F32)],
                         compiler_params=_params(), name="rwkv_pre_bwd")(p, p, *cots, *params)
    return list(res)


def _rwkv_post_math(y, r, k2, v, g, lnx_w, lnx_b, r_k, o512):
    mean = _segsum(y, o512) * (1.0 / HN)
    yc = y - mean
    var = _segsum(yc * yc, o512) * (1.0 / HN)
    yn = yc * lax.rsqrt(var + LNX_EPS) * lnx_w + lnx_b
    bonus = _segsum(r * k2 * r_k, o512) * v
    return (yn + bonus) * g


def _rwkv_post_fwd(y, r, k2, v, g, prm, o512):
    return _rows(lambda s, rr, p, _: ([_rwkv_post_math(*rr, *p)], []), [y, r, k2, v, g], [*prm, o512],
                 [(RW, BF16)], [], name="rwkv_post_fwd", tl=256)[0]


def _rwkv_post_bwd(y, r, k2, v, g, dout, prm, o512):
    def fn(s, rr, p, _):
        o = p[-1]
        _, vjp = jax.vjp(lambda *a: _rwkv_post_math(*a, o), *rr[:5], *p[:-1])
        gr = vjp(rr[5])
        return list(gr[:5]), list(gr[5:])
    return _rows(fn, [y, r, k2, v, g, dout], [*prm, o512], [(RW, F32)] * 5, [q.shape for q in prm],
                 name="rwkv_post_bwd", tl=256)


def _glu_math(ys, t, b):
    return ys * _sig(t + b)


def _glu_fwd(ys, t, b):
    return _rows(lambda s, r, p, _: ([_glu_math(r[0], r[1], p[0])], []), [ys, t], [b], [(S5W, BF16)], [],
                 name="s5_glu_fwd", tl=512)[0]


def _glu_bwd(ys, t, dout, b):
    def fn(s, r, p, _):
        _, vjp = jax.vjp(_glu_math, r[0], r[1], p[0])
        dys, dt, db = vjp(r[2])
        return [dys, dt], [db]
    return _rows(fn, [ys, t, dout], [b], [(S5W, F32), (S5W, BF16)], [b.shape], name="s5_glu_bwd", tl=512)


def _merge_math(gp_r, gp_s, o_r, o_s, b_r, b_s):
    return _sig(gp_r + b_r) * o_r + _sig(gp_s + b_s) * o_s


def _merge_fwd(gpre, o_r, o_s, b_r, b_s):
    return _rows(lambda s, r, p, _: ([_merge_math(*r, *p)], []),
                 [(gpre, D_MODEL, 0), (gpre, D_MODEL, 1), o_r, o_s], [b_r, b_s], [(D_MODEL, BF16)], [],
                 name="merge_fwd", tl=256)[0]


def _merge_bwd(gpre, o_r, o_s, dmi, b_r, b_s):
    def fn(s, r, p, _):
        _, vjp = jax.vjp(_merge_math, *r[:4], *p)
        dgr, dgs, dor, dos, dbr, dbs = vjp(r[4])
        return [dgr, dgs, dor, dos], [dbr, dbs]
    return _rows(fn, [(gpre, D_MODEL, 0), (gpre, D_MODEL, 1), o_r, o_s, dmi], [b_r, b_s], [(D_MODEL, BF16)] * 4,
                 [b_r.shape, b_s.shape], name="merge_bwd", tl=256)


def _mid_fwd(x, mixed, g_post, g_pre):
    def fn(s, r, p, _):
        x2 = r[0] + _rms(r[1], p[0])
        return [x2, _rms(x2, p[1])], []
    return _rows(fn, [x, mixed], [g_post, g_pre], [(D_MODEL, F32), (D_MODEL, BF16)], [], name="mid_fwd", tl=256)


def _mid_bwd(x2, mixed, dh2, dx3, g_post, g_pre):
    def fn(s, r, p, _):
        _, vjp1 = jax.vjp(_rms, r[0], p[1])
        dx2, dg_pre = vjp1(r[2])
        dx2 = dx2 + r[3]
        _, vjp2 = jax.vjp(_rms, r[1], p[0])
        dmixed, dg_post = vjp2(dx2)
        return [dx2, dmixed], [dg_post, dg_pre]
    return _rows(fn, [x2, mixed, dh2, dx3], [g_post, g_pre], [(D_MODEL, F32), (D_MODEL, BF16)], [g_post.shape, g_pre.shape],
                 name="mid_bwd", tl=256)


def _final(x2, f, tgt, g_post):
    def fn(s, r, p, _):
        y, vjp = jax.vjp(_rms, r[1], p[0])
        diff = r[0] + y - r[2]
        dx3 = diff * (1.0 / D_MODEL)
        df, dg = vjp(dx3)
        part = 0.5 * jnp.sum(jnp.sum(diff * diff, axis=1, keepdims=True), axis=0, keepdims=True) * (1.0 / D_MODEL)
        return [dx3, df], [dg, jnp.broadcast_to(part, (1, LANES))]
    return _rows(fn, [x2, f, tgt], [g_post], [(D_MODEL, F32), (D_MODEL, BF16)], [g_post.shape, (1, LANES)], name="final", tl=256)


def _conv_taps(z, car):
    row = lax.broadcasted_iota(jnp.int32, z.shape, 0)
    z1 = jnp.where(row == 0, jnp.broadcast_to(car[7:8, :], z.shape), pltpu.roll(z, 1, 0))
    z2 = pltpu.roll(z, 2, 0)
    z2 = jnp.where(row == 0, jnp.broadcast_to(car[6:7, :], z.shape), z2)
    z2 = jnp.where(row == 1, jnp.broadcast_to(car[7:8, :], z.shape), z2)
    return z1, z2


def _conv(z, car, w, b):
    z1, z2 = _conv_taps(z, car)
    return b + w[0:1, :] * z2 + w[1:2, :] * z1 + w[2:3, :] * z, z1, z2


def _conv_fwd(z, conv_w, conv_b):
    length = z.shape[0]
    tl = min(256, length)
    nt = length // tl
    tc = _pick(D_FF, 1536)
    nb = D_FF // tc

    def body(zg_ref, zv_ref, wg_ref, wv_ref, bg_ref, bv_ref, o_ref, cg, cv):
        @pl.when(pl.program_id(1) == 0)
        def _():
            cg[...] = jnp.zeros_like(cg)
            cv[...] = jnp.zeros_like(cv)

        zg, zv = zg_ref[...], zv_ref[...]
        gate, _, _ = _conv(zg, cg[...], wg_ref[...], bg_ref[...])
        val, _, _ = _conv(zv, cv[...], wv_ref[...], bv_ref[...])
        cg[...] = zg[tl - SUBLANES:, :]
        cv[...] = zv[tl - SUBLANES:, :]
        o_ref[...] = (_gelu(gate) * val).astype(o_ref.dtype)

    zspec = lambda off: pl.BlockSpec((tl, tc), lambda j, i: (i, j + off))
    wspec = lambda off, r: pl.BlockSpec((r, tc), lambda j, i: (0, j + off))
    return pl.pallas_call(
        body, grid=(nb, nt),
        in_specs=[zspec(0), zspec(nb), wspec(0, 3), wspec(nb, 3), wspec(0, 1), wspec(nb, 1)],
        out_specs=pl.BlockSpec((tl, tc), lambda j, i: (i, j)), out_shape=SDS((length, D_FF), BF16),
        scratch_shapes=[pltpu.VMEM((SUBLANES, tc), F32)] * 2,
        compiler_params=_params(("arbitrary", "arbitrary")), name="conv_fwd")(z, z, conv_w, conv_w, conv_b, conv_b)


def _conv_bwd(z, dact, conv_w, conv_b):
    length = z.shape[0]
    tl = min(256, length)
    nt = length // tl
    tc = _pick(D_FF, 1536)
    nb = D_FF // tc
    rows_per = tl // SUBLANES

    def half_bwd(dzc, z, z1, z2, w, dcar):
        n = tl
        row = lax.broadcasted_iota(jnp.int32, dzc.shape, 0)
        u1 = jnp.where(row == n - 1, jnp.broadcast_to(dcar[0:1, :], dzc.shape), pltpu.roll(dzc, n - 1, 0))
        u2 = pltpu.roll(dzc, n - 2, 0)
        u2 = jnp.where(row == n - 2, jnp.broadcast_to(dcar[0:1, :], dzc.shape), u2)
        u2 = jnp.where(row == n - 1, jnp.broadcast_to(dcar[1:2, :], dzc.shape), u2)
        dz = w[2:3, :] * dzc + w[1:2, :] * u1 + w[0:1, :] * u2
        dw = jnp.concatenate([jnp.sum(dzc * z2, axis=0, keepdims=True), jnp.sum(dzc * z1, axis=0, keepdims=True),
                              jnp.sum(dzc * z, axis=0, keepdims=True)], axis=0)
        return dz, dw, jnp.sum(dzc, axis=0, keepdims=True)

    def body(zg_ref, zv_ref, pg_ref, pv_ref, da_ref, wg_ref, wv_ref, bg_ref, bv_ref,
             dzg_ref, dzv_ref, dwg_ref, dwv_ref, dbg_ref, dbv_ref, cg, cv):
        step = pl.program_id(1)

        @pl.when(step == 0)
        def _():
            cg[...] = jnp.zeros_like(cg)
            cv[...] = jnp.zeros_like(cv)
            for ref in (dwg_ref, dwv_ref, dbg_ref, dbv_ref):
                ref[...] = jnp.zeros_like(ref)

        is_first_tile = step == nt - 1
        zg, zv = zg_ref[...], zv_ref[...]
        pg = jnp.where(is_first_tile, 0.0, pg_ref[...])
        pv = jnp.where(is_first_tile, 0.0, pv_ref[...])
        wg, wv = wg_ref[...], wv_ref[...]
        gate, zg1, zg2 = _conv(zg, pg, wg, bg_ref[...])
        val, zv1, zv2 = _conv(zv, pv, wv, bv_ref[...])
        act_g, vjp = jax.vjp(_gelu, gate)
        da = da_ref[...]
        dgate = vjp(da * val)[0]
        dval = da * act_g
        dzg, dwg, dbg = half_bwd(dgate, zg, zg1, zg2, wg, cg[...])
        dzv, dwv, dbv = half_bwd(dval, zv, zv1, zv2, wv, cv[...])
        cg[...] = dgate[:SUBLANES, :]
        cv[...] = dval[:SUBLANES, :]
        dzg_ref[...] = dzg.astype(dzg_ref.dtype)
        dzv_ref[...] = dzv.astype(dzv_ref.dtype)
        dwg_ref[...] += dwg
        dwv_ref[...] += dwv
        dbg_ref[...] += dbg
        dbv_ref[...] += dbv

    rmap = lambda i: nt - 1 - i
    zspec = lambda off: pl.BlockSpec((tl, tc), lambda j, i: (rmap(i), j + off))
    pspec = lambda off: pl.BlockSpec((SUBLANES, tc), lambda j, i: (jnp.maximum(rmap(i) * rows_per - 1, 0), j + off))
    wspec = lambda off, r: pl.BlockSpec((r, tc), lambda j, i: (0, j + off))
    out_w = lambda r: pl.BlockSpec((r, tc), lambda j, i: (0, j))
    dzg, dzv, dwg, dwv, dbg, dbv = pl.pallas_call(
        body, grid=(nb, nt),
        in_specs=[zspec(0), zspec(nb), pspec(0), pspec(nb), pl.BlockSpec((tl, tc), lambda j, i: (rmap(i), j)),
                  wspec(0, 3), wspec(nb, 3), wspec(0, 1), wspec(nb, 1)],
        out_specs=[pl.BlockSpec((tl, tc), lambda j, i: (rmap(i), j))] * 2 + [out_w(3), out_w(3), out_w(1), out_w(1)],
        out_shape=[SDS((length, D_FF), BF16)] * 2 + [SDS((3, D_FF), F32)] * 2 + [SDS((1, D_FF), F32)] * 2,
        scratch_shapes=[pltpu.VMEM((SUBLANES, tc), F32)] * 2,
        compiler_params=_params(("arbitrary", "arbitrary")), name="conv_bwd")(z, z, z, z, dact, conv_w, conv_w, conv_b, conv_b)
    return dzg, dzv, jnp.concatenate([dwg, dwv], axis=1), jnp.concatenate([dbg, dbv], axis=1)


def _s5_prep_math(a_re, a_im, ls, b_re, b_im):
    dt = jnp.exp(ls)
    er = jnp.exp(a_re * dt)
    ph = a_im * dt
    abr, abi = er * jnp.cos(ph), er * jnp.sin(ph)
    den = a_re * a_re + a_im * a_im
    nr = abr - 1.0
    cr = (nr * a_re + abi * a_im) / den
    ci = (abi * a_re - nr * a_im) / den
    return abr, abi, cr * b_re - ci * b_im, cr * b_im + ci * b_re


def _s5_prep_fwd(a_re, a_im, ls, b_re, b_im):
    def body(ar, ai, l, br, bi, o1, o2, o3, o4):
        for ref, val in zip((o1, o2, o3, o4), _s5_prep_math(ar[...], ai[...], l[...], br[...], bi[...]), strict=True):
            ref[...] = val
    return pl.pallas_call(body, out_shape=[SDS((1, S5N), F32)] * 2 + [SDS((S5C, S5N), F32)] * 2,
                          name="s5_prep_fwd")(a_re, a_im, ls, b_re, b_im)


def _s5_prep_bwd(a_re, a_im, ls, b_re, b_im, cots, gsel):
    def body(ar, ai, l, br, bi, c1, c2, c3, c4, g_ref, o1, o2, o3, o4, o5):
        _, vjp = jax.vjp(_s5_prep_math, ar[...], ai[...], l[...], br[...], bi[...])
        d_ar, d_ai, d_ls, d_br, d_bi = vjp((c1[...], c2[...], c3[...], c4[...]))
        o1[...] = d_ar
        o2[...] = d_ai
        o3[...] = _hdot(jnp.broadcast_to(d_ls, (SUBLANES, S5N)), g_ref[...])
        o4[...] = d_br
        o5[...] = d_bi
    return pl.pallas_call(body, out_shape=[SDS((1, S5N), F32)] * 2 + [SDS((SUBLANES, LANES), F32)] + [SDS((S5C, S5N), F32)] * 2,
                          name="s5_prep_bwd")(a_re, a_im, ls, b_re, b_im, *cots, gsel)


def _cmul(ar, ai, br, bi):
    return ar * br - ai * bi, ar * bi + ai * br


def _s5_powers(abr, abi):
    shp = (SUBLANES, S5N)
    a1 = (jnp.broadcast_to(abr, shp), jnp.broadcast_to(abi, shp))
    a2 = _cmul(*a1, *a1)
    a4 = _cmul(*a2, *a2)
    row = lax.broadcasted_iota(jnp.int32, shp, 0)
    pr, pi = a1
    cur = a1
    for i in range(1, SUBLANES):
        cur = _cmul(*cur, *a1)
        pr = jnp.where(row == i, cur[0], pr)
        pi = jnp.where(row == i, cur[1], pi)
    return a1, a2, a4, (pr, pi)


def _s5_scan(sre_ref, sim_ref, car_re, car_im, abr, abi, n_rows, reverse):
    a1, a2, a4, (pr, pi) = _s5_powers(abr, abi)
    sgn = -1.0 if reverse else 1.0
    row = lax.broadcasted_iota(jnp.int32, (SUBLANES, S5N), 0)
    if reverse:
        qr, qi = pr, pi
        for i in range(SUBLANES):
            src = SUBLANES - 1 - i
            qr = jnp.where(row == i, jnp.broadcast_to(pr[src:src + 1, :], pr.shape), qr)
            qi = jnp.where(row == i, jnp.broadcast_to(pi[src:src + 1, :], pi.shape), qi)
        pr, pi = qr, qi
    nblk = n_rows // SUBLANES

    def blk(i, carry):
        cr, ci = carry
        b = (nblk - 1 - i) if reverse else i
        sl = pl.ds(pl.multiple_of(b * SUBLANES, SUBLANES), SUBLANES)
        xr, xi = sre_ref[sl, :], sim_ref[sl, :]
        for kk, (er, ei) in ((1, a1), (2, a2), (4, a4)):
            if reverse:
                sr = jnp.where(row < SUBLANES - kk, pltpu.roll(xr, SUBLANES - kk, 0), 0.0)
                si = jnp.where(row < SUBLANES - kk, pltpu.roll(xi, SUBLANES - kk, 0), 0.0)
            else:
                sr = jnp.where(row >= kk, pltpu.roll(xr, kk, 0), 0.0)
                si = jnp.where(row >= kk, pltpu.roll(xi, kk, 0), 0.0)
            dr, di = _cmul(er, sgn * ei, sr, si)
            xr, xi = xr + dr, xi + di
        dr, di = _cmul(pr, sgn * pi, cr, ci)
        xr, xi = xr + dr, xi + di
        sre_ref[sl, :] = xr
        sim_ref[sl, :] = xi
        edge = 0 if reverse else SUBLANES - 1
        return (jnp.broadcast_to(xr[edge:edge + 1, :], xr.shape), jnp.broadcast_to(xi[edge:edge + 1, :], xi.shape))

    cr, ci = lax.fori_loop(0, nblk, blk, (car_re[...], car_im[...]))
    car_re[...] = cr
    car_im[...] = ci


S5_BLK = 4
S5_BN = S5N // S5_BLK


def _split2(x):
    hi = x.astype(BF16)
    return hi, (x - hi.astype(F32)).astype(BF16)


def _dot3(a, b, ca, cb):
    ah, al = _split2(a)
    bh, bl = _split2(b)
    return lax.dot_general(jnp.concatenate([ah, ah, al], axis=ca), jnp.concatenate([bh, bl, bh], axis=cb),
                           (((ca,), (cb,)), ((), ())), preferred_element_type=F32)


def _dot1(a, b, ca, cb):
    return lax.dot_general(a.astype(BF16), b.astype(BF16), (((ca,), (cb,)), ((), ())), preferred_element_type=F32)


def _s5_project_in(u, bblk_ref, sre_ref, sim_ref):
    for j in range(S5_BLK):
        bu = _dot1(u[:, j * LANES:(j + 1) * LANES], bblk_ref[j], 1, 0)
        sre_ref[:, j * S5_BN:(j + 1) * S5_BN] = bu[:, :S5_BN]
        sim_ref[:, j * S5_BN:(j + 1) * S5_BN] = bu[:, S5_BN:]


def _s5_project_out(u, d, cblk_ref, sre_ref, sim_ref):
    ys = []
    for j in range(S5_BLK):
        sl = slice(j * S5_BN, (j + 1) * S5_BN)
        ys.append(_dot1(sre_ref[:, sl], cblk_ref[j, :S5_BN, :], 1, 0) + _dot1(sim_ref[:, sl], cblk_ref[j, S5_BN:, :], 1, 0))
    return jnp.concatenate(ys, axis=1) + d * u


def _s5_fwd(u, bblk, cblk, abr, abi, d):
    length = u.shape[0]
    tl = min(256, length)
    nt = length // tl

    def body(u_ref, b_ref, c_ref, ar_ref, ai_ref, d_ref, ys_ref, kr_ref, ki_ref, sre, sim, car_re, car_im):
        @pl.when(pl.program_id(0) == 0)
        def _():
            car_re[...] = jnp.zeros_like(car_re)
            car_im[...] = jnp.zeros_like(car_im)

        kr_ref[0] = car_re[...]
        ki_ref[0] = car_im[...]
        u_ = u_ref[...]
        _s5_project_in(u_, b_ref, sre, sim)
        _s5_scan(sre, sim, car_re, car_im, ar_ref[...], ai_ref[...], tl, False)
        ys_ref[...] = _gelu(_s5_project_out(u_, d_ref[...], c_ref, sre, sim))

    full = lambda a: pl.BlockSpec(a.shape, lambda i, nd=a.ndim: (0,) * nd)
    chk = pl.BlockSpec((1, SUBLANES, S5N), lambda i: (i, 0, 0))
    return pl.pallas_call(
        body, grid=(nt,), in_specs=[pl.BlockSpec((tl, S5W), lambda i: (i, 0)), full(bblk), full(cblk), full(abr), full(abi), full(d)],
        out_specs=[pl.BlockSpec((tl, S5W), lambda i: (i, 0)), chk, chk],
        out_shape=[SDS((length, S5W), F32), SDS((nt, SUBLANES, S5N), F32), SDS((nt, SUBLANES, S5N), F32)],
        scratch_shapes=[pltpu.VMEM((tl, S5N), F32)] * 2 + [pltpu.VMEM((SUBLANES, S5N), F32)] * 2,
        compiler_params=_params(), name="s5_fwd")(u, bblk, cblk, abr, abi, d)


def _s5_bwd(u, dys_a, dys_b, kre, kim, bblk, cblk, abr, abi, d):
    length = u.shape[0]
    tl = min(256, length)
    nt = length // tl

    def body(u_ref, da_ref, db_ref, kr_ref, ki_ref, b_ref, c_ref, ar_ref, ai_ref, d_ref,
             du_ref, dB_ref, dC_ref, dar_ref, dai_ref, dd_ref, sre, sim, gre, gim, car_re, car_im, dcar_re, dcar_im):
        @pl.when(pl.program_id(0) == 0)
        def _():
            dcar_re[...] = jnp.zeros_like(dcar_re)
            dcar_im[...] = jnp.zeros_like(dcar_im)
            for ref in (dB_ref, dC_ref, dar_ref, dai_ref, dd_ref):
                ref[...] = jnp.zeros_like(ref)

        u_ = u_ref[...]
        abr_, abi_, d_ = ar_ref[...], ai_ref[...], d_ref[...]
        car_re[...] = kr_ref[0]
        car_im[...] = ki_ref[0]
        _s5_project_in(u_, b_ref, sre, sim)
        _s5_scan(sre, sim, car_re, car_im, abr_, abi_, tl, False)
        y = _s5_project_out(u_, d_, c_ref, sre, sim)
        _, vjp = jax.vjp(_gelu, y)
        dy = vjp(da_ref[...] + db_ref[...])[0]
        dd_ref[...] += jnp.sum(dy * u_, axis=0, keepdims=True)
        for j in range(S5_BLK):
            sl = slice(j * S5_BN, (j + 1) * S5_BN)
            dyj = dy[:, j * LANES:(j + 1) * LANES]
            gre[:, sl] = _dot1(dyj, c_ref[j, :S5_BN, :], 1, 1)
            gim[:, sl] = _dot1(dyj, c_ref[j, S5_BN:, :], 1, 1)
            dC_ref[j, :S5_BN, :] += _dot1(sre[:, sl], dyj, 0, 0)
            dC_ref[j, S5_BN:, :] += _dot1(sim[:, sl], dyj, 0, 0)
        _s5_scan(gre, gim, dcar_re, dcar_im, abr_, abi_, tl, True)
        gr, gi = gre[...], gim[...]
        pr = _shift_down(sre[...], kr_ref[0, 0:1, :])
        pi = _shift_down(sim[...], ki_ref[0, 0:1, :])
        dar_ref[...] += jnp.sum(gr * pr + gi * pi, axis=0, keepdims=True)
        dai_ref[...] += jnp.sum(gi * pr - gr * pi, axis=0, keepdims=True)
        dus = []
        for j in range(S5_BLK):
            sl = slice(j * S5_BN, (j + 1) * S5_BN)
            uj = u_[:, j * LANES:(j + 1) * LANES]
            dus.append(_dot1(gre[:, sl], b_ref[j, :, :S5_BN], 1, 1) + _dot1(gim[:, sl], b_ref[j, :, S5_BN:], 1, 1))
            dB_ref[j, :, :S5_BN] += _dot1(uj, gre[:, sl], 0, 0)
            dB_ref[j, :, S5_BN:] += _dot1(uj, gim[:, sl], 0, 0)
        du_ref[...] = (jnp.concatenate(dus, axis=1) + d_ * dy).astype(du_ref.dtype)

    rmap = lambda i: nt - 1 - i
    full = lambda a: pl.BlockSpec(a.shape, lambda i, nd=a.ndim: (0,) * nd)
    row = pl.BlockSpec((tl, S5W), lambda i: (rmap(i), 0))
    chk = pl.BlockSpec((1, SUBLANES, S5N), lambda i: (rmap(i), 0, 0))
    return pl.pallas_call(
        body, grid=(nt,), in_specs=[row, row, row, chk, chk, full(bblk), full(cblk), full(abr), full(abi), full(d)],
        out_specs=[row, full(bblk), full(cblk), full(abr), full(abi), full(d)],
        out_shape=[SDS((length, S5W), BF16), SDS(bblk.shape, F32), SDS(cblk.shape, F32), SDS(abr.shape, F32),
                   SDS(abi.shape, F32), SDS(d.shape, F32)],
        scratch_shapes=[pltpu.VMEM((tl, S5N), F32)] * 4 + [pltpu.VMEM((SUBLANES, S5N), F32)] * 4,
        compiler_params=_params(), name="s5_bwd")(u, dys_a, dys_b, kre, kim, bblk, cblk, abr, abi, d)


def _s5_blockdiag_in(bb):
    t = bb.reshape(S5C, S5_BLK, S5_BN).transpose(1, 0, 2)
    t = jnp.tile(t, (1, SUBLANES, 1))
    mask = (lax.broadcasted_iota(jnp.int32, (LANES, S5_BN), 0) // S5C) == (lax.broadcasted_iota(jnp.int32, (LANES, S5_BN), 1) // S5P)
    return jnp.where(mask[None], t, 0.0)


def _s5_blockdiag_in_t(dblk):
    t = dblk.reshape(S5_BLK, SUBLANES, S5C, SUBLANES, S5P)
    t = jnp.diagonal(t, axis1=1, axis2=3)
    return t.transpose(1, 0, 3, 2).reshape(S5C, S5N)


def _s5_blockdiag_out(c):
    t = c.reshape(S5_BLK, SUBLANES, S5C, S5P).transpose(0, 3, 1, 2).reshape(S5_BLK, S5P, LANES)
    t = jnp.tile(t, (1, SUBLANES, 1))
    mask = (lax.broadcasted_iota(jnp.int32, (S5_BN, LANES), 0) // S5P) == (lax.broadcasted_iota(jnp.int32, (S5_BN, LANES), 1) // S5C)
    return jnp.where(mask[None], t, 0.0)


def _s5_blockdiag_out_t(dblk):
    t = dblk.reshape(S5_BLK, SUBLANES, S5P, SUBLANES, S5C)
    t = jnp.diagonal(t, axis1=1, axis2=3)
    return t.transpose(0, 3, 2, 1).reshape(S5G, S5C, S5P)


def _wkv_consts():
    lane = lax.broadcasted_iota(jnp.int32, (4 * HN, LANES), 1)
    row = lax.broadcasted_iota(jnp.int32, (4 * HN, LANES), 0)
    diag = ((lane % HN) == (row % HN)).astype(F32)
    ones = _block_ones(LANES, HN).astype(BF16)
    return lane % HN, diag, jnp.concatenate([ones, ones], axis=0)


def _wkv_split(p):
    hi = p.astype(BF16)
    lo = (p - hi.astype(F32)).astype(BF16)
    return jnp.concatenate([hi, lo], axis=1)


def _wkv_rowseg(p, ones):
    return jnp.dot(_wkv_split(p), ones, preferred_element_type=F32)


def _wkv_full(rows_ref, t):
    q = jnp.broadcast_to(rows_ref[t], (SUBLANES, RW))
    return jnp.concatenate([jnp.tile(q[:, hp * LANES:(hp + 1) * LANES], (SUBLANES, 1)) for hp in range(4)], axis=0)


def _wkv_colsum8(x):
    return jnp.concatenate([x[hp * HN:(hp + 1) * HN].reshape(SUBLANES, SUBLANES, LANES).sum(axis=0) for hp in range(4)], axis=1)


def _wkv_prebroadcast(src_refs, dst_refs, n):
    for s, dref in zip(src_refs, dst_refs, strict=True):
        dref[...] = s[...].reshape(n, 1, RW)


def _wkv_pair_sums(lhs_ref, out_ref):
    n, ns = out_ref.shape[0], out_ref.shape[1]
    ones_pair = _block_ones(2 * LANES, HN).astype(BF16)
    out_ref[...] = jnp.dot(lhs_ref[0:n].reshape(n * ns, 2 * LANES), ones_pair,
                           preferred_element_type=F32).reshape(n, ns, 2 * LANES)


def _wkv_transpose_out(acc, c, diag):
    lane = lax.broadcasted_iota(jnp.int32, (c, LANES), 1)
    pick = diag[:HN].astype(BF16)
    pick3 = jnp.concatenate([pick, pick, pick], axis=0)
    outs = []
    for hp in range(4):
        blk = acc[hp * HN:(hp + 1) * HN]
        hi = blk.astype(BF16)
        rest = blk - hi.astype(F32)
        mid = rest.astype(BF16)
        lo = (rest - mid.astype(F32)).astype(BF16)
        zp = lax.dot_general(jnp.concatenate([hi, mid, lo], axis=0), pick3, (((0,), (0,)), ((), ())), preferred_element_type=F32)
        outs.append(jnp.where(lane < HN, zp[:c], zp[HN:HN + c]))
    return jnp.concatenate(outs, axis=1)


def _wkv_fwd(r, w, k, v, a, b):
    length = r.shape[0]
    c = min(WKV_CHUNK, length)
    nc = length // c
    ns = 4 * HN

    def body(r_ref, w_ref, k_ref, v_ref, a_ref, b_ref, y_ref, sst_ref, sast_ref, vst_ref, fin_ref,
             s_ref, acc_ref, lhs_ref, ysum_ref, rb, wb, kb, vb, ab, bb, wab, beb, kab):
        @pl.when(pl.program_id(0) == 0)
        def _():
            s_ref[...] = jnp.zeros_like(s_ref)

        a_next = pltpu.roll(a_ref[...], c - 1, 0)
        o512 = _block_ones(RW, HN)
        _wkv_prebroadcast((r_ref, w_ref, k_ref, v_ref, a_ref, b_ref, w_ref[...] * a_next,
                           _segsum_impl(b_ref[...] * a_next, o512), _segsum_impl(k_ref[...] * a_next, o512)),
                          (rb, wb, kb, vb, ab, bb, wab, beb, kab), c)
        lane_t, diag, ones = _wkv_consts()

        def fill_v(q, carry):
            for half in range(2):
                lhs_ref[q, :, half * LANES:(half + 1) * LANES] = (_wkv_full(vb, 2 * q + half) * diag).astype(BF16)
            return carry

        lax.fori_loop(0, c // 2, fill_v, 0)
        _wkv_pair_sums(lhs_ref, vst_ref)

        def pair(p, carry):
            t0, t1 = 2 * p, 2 * p + 1
            s0 = s_ref[...]
            both = _wkv_rowseg(jnp.concatenate([s0 * _wkv_full(ab, t0), s0 * _wkv_full(wab, t0)], axis=0), ones)
            v_pair = vst_ref[p]
            sa0, v0 = both[:ns], v_pair[:, :LANES]
            s1 = s0 * _wkv_full(wb, t0) + sa0 * _wkv_full(bb, t0) + v0 * _wkv_full(kb, t0)
            sa1 = both[ns:] + sa0 * _wkv_full(beb, t0) + v0 * _wkv_full(kab, t0)
            s2 = s1 * _wkv_full(wb, t1) + sa1 * _wkv_full(bb, t1) + v_pair[:, LANES:] * _wkv_full(kb, t1)
            sst_ref[t0] = s0
            sast_ref[t0] = sa0
            sst_ref[t1] = s1
            sast_ref[t1] = sa1
            s_ref[...] = s2
            lhs_ref[p, :, 0:LANES] = (s1 * _wkv_full(rb, t0)).astype(BF16)
            lhs_ref[p, :, LANES:2 * LANES] = (s2 * _wkv_full(rb, t1)).astype(BF16)
            return carry

        lax.fori_loop(0, c // 2, pair, 0)
        _wkv_pair_sums(lhs_ref, ysum_ref)
        acc_ref[...] = jnp.zeros_like(acc_ref)

        def gather_y(p, carry):
            both = ysum_ref[p]
            acc = jnp.where(lane_t == 2 * p, both[:, :LANES], acc_ref[...])
            acc_ref[...] = jnp.where(lane_t == 2 * p + 1, both[:, LANES:], acc)
            return carry

        lax.fori_loop(0, c // 2, gather_y, 0)
        y_ref[...] = _wkv_transpose_out(acc_ref[...], c, diag)
        fin_ref[...] = s_ref[...]

    row = pl.BlockSpec((c, RW), lambda i: (i, 0))
    st = pl.BlockSpec((c, ns, LANES), lambda i: (i, 0, 0))
    state = pltpu.VMEM((ns, LANES), F32)
    st_pair = pl.BlockSpec((c // 2, ns, 2 * LANES), lambda i: (i, 0, 0))
    return pl.pallas_call(
        body, grid=(nc,), in_specs=[row] * 6,
        out_specs=[row, st, st, st_pair, pl.BlockSpec((ns, LANES), lambda i: (0, 0))],
        out_shape=[SDS((length, RW), F32)] + [SDS((length, ns, LANES), F32)] * 2 + [SDS((length // 2, ns, 2 * LANES), F32)]
        + [SDS((ns, LANES), F32)],
        scratch_shapes=[state] * 2 + [pltpu.VMEM((c, ns, 2 * LANES), BF16), pltpu.VMEM((c // 2, ns, 2 * LANES), F32)]
        + [pltpu.VMEM((c, 1, RW), F32)] * 9,
        compiler_params=_params(), name="wkv_fwd")(r, w, k, v, a, b)


def _wkv_bwd(r, w, k, a, b, dy, sst, sast, vst, fin):
    length = r.shape[0]
    c = min(WKV_CHUNK, length)
    nc = length // c
    ns = 4 * HN

    def body(r_ref, w_ref, k_ref, a_ref, b_ref, dy_ref, sst_ref, sast_ref, vst_ref, snext_ref, fin_ref,
             dr_ref, dw_ref, dk_ref, da_ref, db_ref, dv_ref,
             ds_ref, cur_ref, acc_ref, dyst, lhs_ref, dvsum_ref, rb, wb, kb, ab, bb, dyb, wbb, alb, rhb,
             p_r, p_w, p_k, p_a, p_b):
        @pl.when(pl.program_id(0) == 0)
        def _():
            ds_ref[...] = jnp.zeros_like(ds_ref)

        b_prev = pltpu.roll(b_ref[...], 1, 0)
        o512 = _block_ones(RW, HN)
        _wkv_prebroadcast((r_ref, w_ref, k_ref, a_ref, b_ref, dy_ref, w_ref[...] * b_prev,
                           _segsum_impl(a_ref[...] * b_prev, o512), _segsum_impl(r_ref[...] * b_ref[...], o512)),
                          (rb, wb, kb, ab, bb, dyb, wbb, alb, rhb), c)
        lane_t, diag, ones = _wkv_consts()

        def fill(q, carry):
            for half in range(2):
                lhs_ref[q, :, half * LANES:(half + 1) * LANES] = (_wkv_full(dyb, 2 * q + half) * diag).astype(BF16)
            return carry

        lax.fori_loop(0, c // 2, fill, 0)
        _wkv_pair_sums(lhs_ref, dyst)
        cur_ref[...] = jnp.where(pl.program_id(0) == 0, fin_ref[...], snext_ref[0])

        def pair(p, carry):
            ta = c - 1 - 2 * p
            tb = ta - 1
            s_a, s_b, s_c = cur_ref[...], sst_ref[ta], sst_ref[tb]
            dy_pair = dyst[c // 2 - 1 - p]
            dy_a, dy_b = dy_pair[:, LANES:], dy_pair[:, :LANES]
            ds_a = ds_ref[...] + dy_a * _wkv_full(rb, ta)
            both = _wkv_rowseg(jnp.concatenate([ds_a * _wkv_full(bb, ta), ds_a * _wkv_full(wbb, ta)], axis=0), ones)
            dsa_a = both[:ns]
            ds_b = ds_a * _wkv_full(wb, ta) + dsa_a * _wkv_full(ab, ta) + dy_b * _wkv_full(rb, tb)
            dsa_b = both[ns:] + dsa_a * _wkv_full(alb, ta) + dy_b * _wkv_full(rhb, tb)
            ds_ref[...] = ds_b * _wkv_full(wb, tb) + dsa_b * _wkv_full(ab, tb)
            cur_ref[...] = s_c
            lhs_ref[p, :, 0:LANES] = (ds_a * _wkv_full(kb, ta)).astype(BF16)
            lhs_ref[p, :, LANES:2 * LANES] = (ds_b * _wkv_full(kb, tb)).astype(BF16)
            v_pair = vst_ref[c // 2 - 1 - p]
            for t, s_t, s_prev, ds, dsa, dyy, vv in ((ta, s_a, s_b, ds_a, dsa_a, dy_a, v_pair[:, LANES:]),
                                                     (tb, s_b, s_c, ds_b, dsa_b, dy_b, v_pair[:, :LANES])):
                p_r[t] = _wkv_colsum8(s_t * dyy)
                p_k[t] = _wkv_colsum8(ds * vv)
                p_b[t] = _wkv_colsum8(ds * sast_ref[t])
                p_w[t] = _wkv_colsum8(ds * s_prev)
                p_a[t] = _wkv_colsum8(s_prev * dsa)
            return carry

        lax.fori_loop(0, c // 2, pair, 0)
        _wkv_pair_sums(lhs_ref, dvsum_ref)
        acc_ref[...] = jnp.zeros_like(acc_ref)

        def gather_dv(p, carry):
            both = dvsum_ref[p]
            acc = jnp.where(lane_t == c - 1 - 2 * p, both[:, :LANES], acc_ref[...])
            acc_ref[...] = jnp.where(lane_t == c - 2 - 2 * p, both[:, LANES:], acc)
            return carry

        lax.fori_loop(0, c // 2, gather_dv, 0)
        sel = (lax.broadcasted_iota(jnp.int32, (c, c * SUBLANES), 1) // SUBLANES
               == lax.broadcasted_iota(jnp.int32, (c, c * SUBLANES), 0)).astype(F32)
        for out_ref, part in ((dr_ref, p_r), (dw_ref, p_w), (dk_ref, p_k), (da_ref, p_a), (db_ref, p_b)):
            out_ref[...] = _dot3(sel, part[...].reshape(c * SUBLANES, RW), 1, 0)
        dv_ref[...] = _wkv_transpose_out(acc_ref[...], c, diag)

    rmap = lambda i: nc - 1 - i
    row = pl.BlockSpec((c, RW), lambda i: (rmap(i), 0))
    st = pl.BlockSpec((c, ns, LANES), lambda i: (rmap(i), 0, 0))
    nxt = pl.BlockSpec((1, ns, LANES), lambda i: (jnp.minimum((rmap(i) + 1) * c, length - 1), 0, 0))
    state = pltpu.VMEM((ns, LANES), F32)
    st_pair = pl.BlockSpec((c // 2, ns, 2 * LANES), lambda i: (rmap(i), 0, 0))
    return pl.pallas_call(
        body, grid=(nc,), in_specs=[row] * 6 + [st, st, st_pair, nxt, pl.BlockSpec((ns, LANES), lambda i: (0, 0))],
        out_specs=[row] * 6, out_shape=[SDS((length, RW), F32)] * 6,
        scratch_shapes=[state] * 3 + [pltpu.VMEM((c // 2, ns, 2 * LANES), F32), pltpu.VMEM((c, ns, 2 * LANES), BF16),
                                       pltpu.VMEM((c // 2, ns, 2 * LANES), F32)]
        + [pltpu.VMEM((c, 1, RW), F32)] * 9 + [pltpu.VMEM((c, SUBLANES, RW), F32)] * 5,
        compiler_params=_params(), name="wkv_bwd")(r, w, k, a, b, dy, sst, sast, vst, sst, fin)


def _my_place():
    return lax.axis_index("x"), lax.axis_index("y"), lax.axis_index("c")


def _peer(x, y, c, k):
    return (x ^ ((k >> 2) & 1), y ^ ((k >> 1) & 1), c ^ (k & 1))


def _all_gather(shard, name):
    def body(in_ref, out_ref, send_sems, recv_sems, local_sem):
        x, y, c = _my_place()
        me = 4 * x + 2 * y + c
        mine = pltpu.make_async_copy(in_ref, out_ref.at[me], local_sem)
        mine.start()
        sends = []
        for k in range(1, N_DEV):
            cp = pltpu.make_async_remote_copy(src_ref=in_ref, dst_ref=out_ref.at[me], send_sem=send_sems.at[k - 1],
                                              recv_sem=recv_sems.at[k - 1], device_id=_peer(x, y, c, k), device_id_type=MESH)
            cp.start()
            sends.append(cp)
        for k in range(1, N_DEV):
            px, py, pc = _peer(x, y, c, k)
            pltpu.make_async_remote_copy(src_ref=in_ref, dst_ref=out_ref.at[4 * px + 2 * py + pc], send_sem=send_sems.at[k - 1],
                                         recv_sem=recv_sems.at[k - 1], device_id=(px, py, pc), device_id_type=MESH).wait_recv()
        for cp in sends:
            cp.wait_send()
        mine.wait()

    return pl.pallas_call(
        body, out_shape=SDS((N_DEV, *shard.shape), shard.dtype),
        in_specs=[pl.BlockSpec(memory_space=pl.ANY)], out_specs=pl.BlockSpec(memory_space=pl.ANY),
        scratch_shapes=[pltpu.SemaphoreType.DMA((N_DEV - 1,)), pltpu.SemaphoreType.DMA((N_DEV - 1,)), pltpu.SemaphoreType.DMA],
        name=name)(shard)


def _all_gather_two_level(shard, name):
    def body(in_ref, out_ref, send_sems, recv_sems, local_sem):
        x, y, c = _my_place()
        me, sibling = (x, y, c), (x, y, 1 - c)
        chips = [(1 - x, y), (x, 1 - y), (1 - x, 1 - y)]

        def block(px, py, pc):
            return out_ref.at[4 * px + 2 * py + pc]

        def copy(k, blk, to, src=None):
            return pltpu.make_async_remote_copy(src_ref=block(*blk) if src is None else src, dst_ref=block(*blk),
                                                send_sem=send_sems.at[k], recv_sem=recv_sems.at[k], device_id=to,
                                                device_id_type=MESH)

        mine = pltpu.make_async_copy(in_ref, block(*me), local_sem)
        mine.start()
        first = [copy(0, me, sibling, src=in_ref)] + [copy(1 + j, me, (*chip, c), src=in_ref) for j, chip in enumerate(chips)]
        for cp in first:
            cp.start()
        passed = [copy(4 + j, (*chip, c), sibling) for j, chip in enumerate(chips)]
        for j, chip in enumerate(chips):
            copy(1 + j, (*chip, c), me).wait_recv()
            passed[j].start()
        copy(0, sibling, me).wait_recv()
        for j, chip in enumerate(chips):
            copy(4 + j, (*chip, 1 - c), me).wait_recv()
        for cp in first + passed:
            cp.wait_send()
        mine.wait()

    return pl.pallas_call(
        body, out_shape=SDS((N_DEV, *shard.shape), shard.dtype),
        in_specs=[pl.BlockSpec(memory_space=pl.ANY)], out_specs=pl.BlockSpec(memory_space=pl.ANY),
        scratch_shapes=[pltpu.SemaphoreType.DMA((N_DEV - 1,)), pltpu.SemaphoreType.DMA((N_DEV - 1,)), pltpu.SemaphoreType.DMA],
        name=name)(shard)


_HBM_SPEC = pl.BlockSpec(memory_space=pltpu.HBM)
_SEM_SPEC = pl.BlockSpec(memory_space=pltpu.SEMAPHORE)
_DATAFLOW = pltpu.SideEffectType.DATAFLOW_SIDE_EFFECTING


def _exchange_start(srcs, per_peer, name):
    n = len(srcs)
    lands = [jnp.zeros((N_DEV, *s.shape[-2:]), s.dtype) for s in srcs]

    def body(*refs):
        src_refs, land_refs = refs[:n], refs[n:2 * n]
        send_sems, recv_sems, token = refs[2 * n], refs[2 * n + 1], refs[-1]
        x, y, c = _my_place()
        me = 4 * x + 2 * y + c
        for a in range(n):
            for k in range(1, N_DEV):
                px, py, pc = _peer(x, y, c, k)
                mine = src_refs[a].at[4 * px + 2 * py + pc] if per_peer else src_refs[a]
                pltpu.make_async_remote_copy(src_ref=mine, dst_ref=land_refs[a].at[me], send_sem=send_sems.at[7 * a + k - 1],
                                             recv_sem=recv_sems.at[7 * a + k - 1], device_id=(px, py, pc),
                                             device_id_type=MESH).start()
        token[...] = jnp.zeros_like(token)

    n_sem = (N_DEV - 1) * n
    res = pl.pallas_call(
        body, name=name,
        out_shape=(pltpu.SemaphoreType.DMA((n_sem,)), pltpu.SemaphoreType.DMA((n_sem,)),
                   *[pltpu.HBM(a.shape, a.dtype) for a in srcs + lands], SDS((SUBLANES, LANES), F32)),
        in_specs=(_HBM_SPEC,) * (2 * n),
        out_specs=(_SEM_SPEC, _SEM_SPEC) + (_HBM_SPEC,) * (2 * n) + (pl.BlockSpec(memory_space=pltpu.VMEM),),
        input_output_aliases={i: 2 + i for i in range(2 * n)}, compiler_params=pltpu.CompilerParams(has_side_effects=_DATAFLOW),
    )(*[pltpu.with_memory_space_constraint(a, pltpu.HBM) for a in srcs + lands])
    return res[0], res[1], list(res[2:2 + n]), list(res[2 + n:2 + 2 * n]), res[-1]


def _exchange_wait(started, after, per_peer, name):
    send_sems, recv_sems, src_thrus, land_thrus, _ = started
    n = len(src_thrus)

    def body(*refs):
        src_refs, land_refs = refs[:n], refs[n:2 * n]
        send_sems, recv_sems = refs[2 * n], refs[2 * n + 1]
        x, y, c = _my_place()
        me = 4 * x + 2 * y + c
        for a in range(n):
            for k in range(1, N_DEV):
                px, py, pc = _peer(x, y, c, k)
                mine = src_refs[a].at[me] if per_peer else src_refs[a]
                copy = pltpu.make_async_remote_copy(src_ref=mine, dst_ref=land_refs[a].at[4 * px + 2 * py + pc],
                                                    send_sem=send_sems.at[7 * a + k - 1], recv_sem=recv_sems.at[7 * a + k - 1],
                                                    device_id=(px, py, pc), device_id_type=MESH)
                copy.wait_send()
                copy.wait_recv()

    res = pl.pallas_call(
        body, name=name, out_shape=tuple(pltpu.HBM(a.shape, a.dtype) for a in src_thrus + land_thrus),
        in_specs=(_HBM_SPEC,) * (2 * n) + (_SEM_SPEC, _SEM_SPEC, pl.BlockSpec(memory_space=pl.ANY)),
        out_specs=(_HBM_SPEC,) * (2 * n), input_output_aliases={i: i for i in range(2 * n)},
        compiler_params=pltpu.CompilerParams(has_side_effects=_DATAFLOW),
    )(*src_thrus, *land_thrus, send_sems, recv_sems, after)
    return list(res[n:])


def _row_tile(rows, cap):
    best = None
    for t in range(2 * SUBLANES, min(rows, cap) + 1, 2 * SUBLANES):
        if rows % t == 0:
            best = t
    return best or rows


def _sum_parts(parts, name, own=None):
    rows, cols = parts.shape[1:]

    def body(*refs):
        p_ref, g_out = refs[0], refs[-1]
        g = p_ref[0].astype(F32)
        for s in range(1, N_DEV):
            g = g + p_ref[s].astype(F32)
        if own is not None:
            g = g + refs[1][...].astype(F32)
        g_out[...] = g

    tr = _row_tile(rows, 384 * 1024 // cols)
    blk = pl.BlockSpec((tr, cols), lambda i: (i, 0))
    extra = [] if own is None else [own]
    return pl.pallas_call(
        body, grid=(rows // tr,),
        in_specs=[pl.BlockSpec((N_DEV, tr, cols), lambda i: (0, i, 0))] + [blk] * len(extra),
        out_specs=blk, out_shape=SDS((rows, cols), F32),
        compiler_params=_params(("parallel",)), name=name)(parts, *extra)


def _adamw(g, wgt, m, v, name):
    rows, cols = wgt.shape

    def body(g_ref, w_ref, m_ref, v_ref, d_out, m_out, v_out):
        g_ = g_ref[...]
        m_new = ADAM_B1 * m_ref[...] + (1.0 - ADAM_B1) * g_
        v_new = ADAM_B2 * v_ref[...] + (1.0 - ADAM_B2) * (g_ * g_)
        m_hat = m_new / (1.0 - ADAM_B1 ** ADAM_STEP)
        v_hat = v_new / (1.0 - ADAM_B2 ** ADAM_STEP)
        d_out[...] = -ADAM_LR * (m_hat / (jnp.sqrt(v_hat) + ADAM_EPS) + ADAM_WD * w_ref[...])
        m_out[...] = m_new
        v_out[...] = v_new

    tr = _row_tile(rows, 512 * 1024 // cols)
    blk = pl.BlockSpec((tr, cols), lambda i: (i, 0))
    return pl.pallas_call(
        body, grid=(rows // tr,), in_specs=[blk] * 4, out_specs=[blk] * 3, out_shape=[SDS((rows, cols), F32)] * 3,
        compiler_params=_params(("parallel",)), name=name)(g, wgt, m, v)


PACK_ALIGN = 2 * SUBLANES * LANES
PACK_ROWS = 512

ROW_SHARDED = ("s5_w_glu", "w_out", "ffn_w_down")
TRANSPOSED = ("w_in", "ffn_w_up", "w_branch_rwkv", "w_branch_s5")
SMALL_SHARDED = ("rwkv_w2", "rwkv_a2", "rwkv_g2", "ffn_conv_w")
SHARDED = ROW_SHARDED + TRANSPOSED + SMALL_SHARDED
F32_GATHER = ("ffn_conv_w",)
REPLICATED =("norm_mix_pre", "norm_mix_post", "norm_ffn_pre", "norm_ffn_post", "b_gate", "rwkv_shift_mu", "rwkv_w0",
              "rwkv_a0", "rwkv_k_k", "rwkv_k_a", "rwkv_r_k", "rwkv_lnx_w", "rwkv_lnx_b", "s5_a_re", "s5_a_im", "s5_b_re",
              "s5_b_im", "s5_c_re", "s5_c_im", "s5_d", "s5_log_step", "s5_b_glu", "ffn_conv_b")
WEIGHTS = ("norm_mix_pre", "norm_mix_post", "norm_ffn_pre", "norm_ffn_post", "w_in", "b_gate", "rwkv_shift_mu", "rwkv_w0",
           "rwkv_w2", "rwkv_a0", "rwkv_a2", "rwkv_g2", "rwkv_k_k", "rwkv_k_a", "rwkv_r_k", "rwkv_lnx_w", "rwkv_lnx_b",
           "s5_a_re", "s5_a_im", "s5_b_re", "s5_b_im", "s5_c_re", "s5_c_im", "s5_d", "s5_log_step", "s5_w_glu", "s5_b_glu",
           "w_branch_rwkv", "w_branch_s5", "w_out", "ffn_w_up", "ffn_conv_w", "ffn_conv_b", "ffn_w_down")


def _pad_flat(a):
    flat = a.reshape(-1)
    pad = (-flat.shape[0]) % PACK_ALIGN
    return jnp.pad(flat, (0, pad)) if pad else flat


def _pad_cols(a, mult):
    pad = (-a.shape[1]) % mult
    return jnp.pad(a, ((0, 0), (0, pad))) if pad else a


def _pack(pieces):
    flat = jnp.concatenate([_pad_flat(p) for p in pieces])
    pad = (-flat.shape[0]) % (PACK_ROWS * LANES)
    return (jnp.pad(flat, (0, pad)) if pad else flat).reshape(-1, LANES)


def _unpack(buf, shapes):
    lead = buf.shape[:-2]
    flat = buf.reshape(*lead, -1)
    out, off = [], 0
    for s in shapes:
        n = math.prod(s)
        out.append(flat[..., off:off + n].reshape(*lead, *s))
        off += n + (-n) % PACK_ALIGN
    return out


def _rows_to_full(stack):
    return stack.reshape(-1, stack.shape[2])


def _cols_to_full(stack):
    return stack.transpose(1, 0, 2).reshape(stack.shape[1], -1)


EARLY_GRADS = ("ffn_w_down", "ffn_w_up", "w_out")
REST_GRADS = ("s5_w_glu", "w_in", "w_branch_rwkv", "w_branch_s5")


def _local_step(x, tgt, wt, get_late, send_early, send_mid, send_rest):
    wt = dict(wt)
    o512 = _block_ones(RW, HN)
    w_in = wt["w_in"]
    w_rw, w_u, w_g = w_in[:N_RWKV], w_in[N_RWKV:N_RWKV + S5W], w_in[N_RWKV + S5W:]
    b_gate = wt["b_gate"]
    b_r, b_s = b_gate[:, :D_MODEL], b_gate[:, D_MODEL:]
    zpad = lambda a, lo, n: jnp.pad(a.astype(F32), ((lo, n - lo - a.shape[0]), (0, 0)))
    w2p, a2p, g2p = zpad(wt["rwkv_w2"], 0, 256), zpad(wt["rwkv_a2"], 64, 256), zpad(wt["rwkv_g2"], 128, 256)
    pre_small = [wt["rwkv_w0"], wt["rwkv_a0"], wt["rwkv_k_k"], wt["rwkv_k_a"], w2p, a2p, g2p]
    post_prm = [wt["rwkv_lnx_w"], wt["rwkv_lnx_b"], wt["rwkv_r_k"].reshape(1, RW)]
    mu = wt["rwkv_shift_mu"]

    a_re, a_im = wt["s5_a_re"].reshape(1, S5N), wt["s5_a_im"].reshape(1, S5N)
    ls = jnp.repeat(wt["s5_log_step"].reshape(S5G), S5P).reshape(1, S5N)
    b_re_t = wt["s5_b_re"].reshape(S5N, S5C).T
    b_im_t = wt["s5_b_im"].reshape(S5N, S5C).T
    c_re, c_im = wt["s5_c_re"].reshape(S5G, S5C, S5P), wt["s5_c_im"].reshape(S5G, S5C, S5P)
    abr, abi, bbr, bbi = _s5_prep_fwd(a_re, a_im, ls, b_re_t, b_im_t)
    bblk = jnp.concatenate([_s5_blockdiag_in(bbr), _s5_blockdiag_in(bbi)], axis=2)
    cblk = jnp.concatenate([_s5_blockdiag_out(c_re), -_s5_blockdiag_out(c_im)], axis=1)
    s5_d = wt["s5_d"]

    h1 = _rms_fwd(x, wt["norm_mix_pre"], "rms1_fwd")
    p_rw = _mm(h1, w_rw, tb=True, name="mm_proj_rwkv")
    u = _mm(h1, w_u, tb=True, name="mm_proj_s5")
    gpre = _mm(h1, w_g, tb=True, name="mm_proj_gate")
    r, decay, k2, v, aa, bb, g = _rwkv_pre_fwd(p_rw, mu, pre_small, o512)
    y, sst, sast, vst, s_fin = _wkv_fwd(r, decay, k2, v, aa, bb)
    wt.update(get_late(y))
    y_r = _rwkv_post_fwd(y, r, k2, v, g, post_prm, o512)
    o_r = _mm(y_r, wt["w_branch_rwkv"], tb=True, name="mm_branch_rwkv")
    ys, kre, kim = _s5_fwd(u, bblk, cblk, abr, abi, s5_d)
    t_glu = _mm(ys, wt["s5_w_glu"], name="mm_glu")
    out5 = _glu_fwd(ys, t_glu, wt["s5_b_glu"])
    o_s = _mm(out5, wt["w_branch_s5"], tb=True, name="mm_branch_s5")
    mi = _merge_fwd(gpre, o_r, o_s, b_r, b_s)
    mixed = _mm(mi, wt["w_out"], name="mm_out")
    x2, h2 = _mid_fwd(x, mixed, wt["norm_mix_post"], wt["norm_ffn_pre"])
    z = _mm(h2, wt["ffn_w_up"], tb=True, name="mm_up")
    act = _conv_fwd(z, wt["ffn_conv_w"], wt["ffn_conv_b"])
    f = _mm(act, wt["ffn_w_down"], name="mm_down")
    dx3, df, d_norm_ffn_post, loss_part = _final(x2, f, tgt, wt["norm_ffn_post"])

    gr = {"norm_ffn_post": d_norm_ffn_post}
    dact = _mm(df, wt["ffn_w_down"], tb=True, name="mm_down_dx")
    gr["ffn_w_down"] = _mm(act, df, ta=True, out_dtype=BF16, name="mm_down_dw")
    dzg, dzv, gr["ffn_conv_w"], gr["ffn_conv_b"] = _conv_bwd(z, dact, wt["ffn_conv_w"], wt["ffn_conv_b"])
    dz = jnp.concatenate([dzg, dzv], axis=1)
    dh2 = _mm(dz, wt["ffn_w_up"], name="mm_up_dx")
    gr["ffn_w_up"] = _mm(dz, h2, ta=True, out_dtype=BF16, name="mm_up_dw")
    dx2, dmixed, gr["norm_mix_post"], gr["norm_ffn_pre"] = _mid_bwd(x2, mixed, dh2, dx3, wt["norm_mix_post"], wt["norm_ffn_pre"])
    dmi = _mm(dmixed, wt["w_out"], tb=True, name="mm_out_dx")
    gr["w_out"] = _mm(mi, dmixed, ta=True, out_dtype=BF16, name="mm_out_dw")
    token = send_early({n: gr[n] for n in EARLY_GRADS})
    b_r = b_r + jnp.tile(token[0:1, :], (1, D_MODEL // LANES))
    dgp_r, dgp_s, do_r, do_s, db_r, db_s = _merge_bwd(gpre, o_r, o_s, dmi, b_r, b_s)
    gr["b_gate"] = jnp.concatenate([db_r, db_s], axis=1)
    dout5 = _mm(do_s, wt["w_branch_s5"], name="mm_branch_s5_dx")
    gr["w_branch_s5"] = _mm(do_s, out5, ta=True, out_dtype=BF16, name="mm_branch_s5_dw")
    dys_a, dt_glu, gr["s5_b_glu"] = _glu_bwd(ys, t_glu, dout5, wt["s5_b_glu"])
    dys_b = _mm(dt_glu, wt["s5_w_glu"], tb=True, name="mm_glu_dx")
    gr["s5_w_glu"] = _mm(ys, dt_glu, ta=True, out_dtype=BF16, name="mm_glu_dw")
    du, dbblk, dcblk, dabr, dabi, gr["s5_d"] = _s5_bwd(u, dys_a, dys_b, kre, kim, bblk, cblk, abr, abi, s5_d)
    gr["s5_c_re"] = _s5_blockdiag_out_t(dcblk[:, :S5_BN, :]).reshape(wt["s5_c_re"].shape)
    gr["s5_c_im"] = (-_s5_blockdiag_out_t(dcblk[:, S5_BN:, :])).reshape(wt["s5_c_im"].shape)
    dbbr, dbbi = _s5_blockdiag_in_t(dbblk[:, :, :S5_BN]), _s5_blockdiag_in_t(dbblk[:, :, S5_BN:])
    gsel = (lax.broadcasted_iota(jnp.int32, (S5N, LANES), 0) // S5P == lax.broadcasted_iota(jnp.int32, (S5N, LANES), 1)).astype(F32)
    d_are, d_aim, d_ls, d_bre_t, d_bim_t = _s5_prep_bwd(a_re, a_im, ls, b_re_t, b_im_t, (dabr, dabi, dbbr, dbbi), gsel)
    gr["s5_a_re"] = d_are.reshape(wt["s5_a_re"].shape)
    gr["s5_a_im"] = d_aim.reshape(wt["s5_a_im"].shape)
    gr["s5_log_step"] = d_ls[0:1, :S5G]
    gr["s5_b_re"] = d_bre_t.T.reshape(wt["s5_b_re"].shape)
    gr["s5_b_im"] = d_bim_t.T.reshape(wt["s5_b_im"].shape)
    dy_r = _mm(do_r, wt["w_branch_rwkv"], name="mm_branch_rwkv_dx")
    gr["w_branch_rwkv"] = _mm(do_r, y_r, ta=True, out_dtype=BF16, name="mm_branch_rwkv_dw")
    token = send_mid(gr)
    post_prm = [post_prm[0] + jnp.tile(token[0:1, :], (1, RW // LANES))] + post_prm[1:]
    dy, dr1, dk1, dv1, dg, gr["rwkv_lnx_w"], gr["rwkv_lnx_b"], d_rk = _rwkv_post_bwd(y, r, k2, v, g, dy_r, post_prm, o512)
    gr["rwkv_r_k"] = d_rk.reshape(wt["rwkv_r_k"].shape)
    dr2, ddecay, dk2, daa, dbb, dv2 = _wkv_bwd(r, decay, k2, aa, bb, dy, sst, sast, vst, s_fin)
    cots = [dr1, ddecay, dk1, dv1, daa, dbb, dg, dr2, dk2, dv2]
    dp_rw, gr["rwkv_shift_mu"], gr["rwkv_w0"], gr["rwkv_a0"], gr["rwkv_k_k"], gr["rwkv_k_a"], dw2p, da2p, dg2p = \
        _rwkv_pre_bwd(p_rw, cots, mu, pre_small, o512)
    gr["rwkv_w2"], gr["rwkv_a2"], gr["rwkv_g2"] = dw2p[:64], da2p[64:128], dg2p[128:]
    dproj = jnp.concatenate([dp_rw, du, dgp_r, dgp_s], axis=1)
    gr["w_in"] = _mm(dproj, h1, ta=True, out_dtype=BF16, name="mm_proj_dw")
    token = send_rest(gr)
    dh1 = _mm(dproj, w_in + token[0, 0].astype(w_in.dtype), name="mm_proj_dx")
    dx, gr["norm_mix_pre"] = _rms_bwd(x, wt["norm_mix_pre"], dh1, dx2, "rms1_bwd")
    return loss_part[0, 0], dx, gr


def kernel(x, norm_mix_pre, norm_mix_post, norm_ffn_pre, norm_ffn_post, w_in, b_gate, rwkv_shift_mu, rwkv_w0, rwkv_w2, rwkv_a0, rwkv_a2, rwkv_g2, rwkv_k_k, rwkv_k_a, rwkv_r_k, rwkv_lnx_w, rwkv_lnx_b, s5_a_re, s5_a_im, s5_b_re, s5_b_im, s5_c_re, s5_c_im, s5_d, s5_log_step, s5_w_glu, s5_b_glu, w_branch_rwkv, w_branch_s5, w_out, ffn_w_up, ffn_conv_w, ffn_conv_b, ffn_w_down, loss_target, m_norm_mix_pre, m_norm_mix_post, m_norm_ffn_pre, m_norm_ffn_post, m_w_in, m_b_gate, m_rwkv_shift_mu, m_rwkv_w0, m_rwkv_w2, m_rwkv_a0, m_rwkv_a2, m_rwkv_g2, m_rwkv_k_k, m_rwkv_k_a, m_rwkv_r_k, m_rwkv_lnx_w, m_rwkv_lnx_b, m_s5_a_re, m_s5_a_im, m_s5_b_re, m_s5_b_im, m_s5_c_re, m_s5_c_im, m_s5_d, m_s5_log_step, m_s5_w_glu, m_s5_b_glu, m_w_branch_rwkv, m_w_branch_s5, m_w_out, m_ffn_w_up, m_ffn_conv_w, m_ffn_conv_b, m_ffn_w_down, v_norm_mix_pre, v_norm_mix_post, v_norm_ffn_pre, v_norm_ffn_post, v_w_in, v_b_gate, v_rwkv_shift_mu, v_rwkv_w0, v_rwkv_w2, v_rwkv_a0, v_rwkv_a2, v_rwkv_g2, v_rwkv_k_k, v_rwkv_k_a, v_rwkv_r_k, v_rwkv_lnx_w, v_rwkv_lnx_b, v_s5_a_re, v_s5_a_im, v_s5_b_re, v_s5_b_im, v_s5_c_re, v_s5_c_im, v_s5_d, v_s5_log_step, v_s5_w_glu, v_s5_b_glu, v_w_branch_rwkv, v_w_branch_s5, v_w_out, v_ffn_w_up, v_ffn_conv_w, v_ffn_conv_b, v_ffn_w_down):
    args = dict(locals())
    wgt = {n: args[n] for n in WEIGHTS}
    mom = {n: args["m_" + n] for n in WEIGHTS}
    var = {n: args["v_" + n] for n in WEIGHTS}
    me = 4 * lax.axis_index("x") + 2 * lax.axis_index("y") + lax.axis_index("c")
    shard_shapes = {n: wgt[n].shape[1:] for n in SHARDED}
    sent_shapes = {n: (shard_shapes[n][::-1] if n in TRANSPOSED else shard_shapes[n]) for n in SHARDED}
    sent = lambda n, a: a.T if n in TRANSPOSED else a

    unshard = lambda n, blocks: _cols_to_full(blocks) if n in SMALL_SHARDED else _rows_to_full(blocks)

    first = ("w_in", "rwkv_w2", "rwkv_a2", "rwkv_g2")
    late = tuple(n for n in SHARDED if n not in first and n not in F32_GATHER)
    late_payloads = [sent(n, wgt[n][0].astype(BF16)) for n in late] + [_pack([wgt[n][0] for n in F32_GATHER])]
    late_started = _exchange_start(late_payloads, False, "gather_late_start")
    got_first = _unpack(_all_gather_two_level(_pack([sent(n, wgt[n][0].astype(BF16)) for n in first]), "all_gather_weights"),
                        [sent_shapes[n] for n in first])
    wt = {n: wgt[n] for n in REPLICATED}
    for n, blocks in zip(first, got_first, strict=True):
        wt[n] = unshard(n, blocks)

    def get_late(after):
        lands = _exchange_wait(late_started, after, False, "gather_late_wait")
        whole_ = [lax.dynamic_update_index_in_dim(land, mine, me, 0) for land, mine in zip(lands, late_payloads, strict=True)]
        out = {n: _rows_to_full(blocks) for n, blocks in zip(late, whole_[:-1], strict=True)}
        taps = _unpack(whole_[-1], [sent_shapes[n] for n in F32_GATHER])
        out.update({n: unshard(n, b_) for n, b_ in zip(F32_GATHER, taps, strict=True)})
        return out

    lowrank = ("rwkv_w2", "rwkv_a2", "rwkv_g2")
    rest_names = REST_GRADS + lowrank
    whole = tuple(n for n in ("ffn_conv_w",) + REPLICATED if n != "norm_mix_pre")
    whole_mid = tuple(n for n in whole if not n.startswith("rwkv_"))
    whole_late = tuple(n for n in whole if n.startswith("rwkv_"))
    blocks_of = lambda g_: g_.reshape(N_DEV, -1, g_.shape[-1])
    as_sent = lambda n, g_: blocks_of(g_.T if n in lowrank else g_)
    started = {}

    def send_early(grads_):
        started["early"] = _exchange_start([as_sent(n, grads_[n]) for n in EARLY_GRADS], True, "exchange_early_start")
        return started["early"][4]

    def send_whole(key, names):
        def send(grads_):
            started[key + "_mine"] = _pack([grads_[n] for n in names])
            started[key] = _exchange_start([started[key + "_mine"]], False, "gather_" + key + "_start")
            return started[key][4]
        return send

    def send_rest(grads_):
        started["rest"] = _exchange_start([as_sent(n, grads_[n]) for n in rest_names], True, "exchange_rest_start")
        return started["rest"][4] + send_whole("late", whole_late)(grads_)

    loss_part, dx, gr = _local_step(x[0], loss_target[0], wt, get_late, send_early, send_whole("mid", whole_mid), send_rest)

    grads, upd = {}, {}
    flat2d = lambda a: a.reshape(a.shape[-2:])

    def finish_large(key, names, after):
        lands = _exchange_wait(started[key], after, True, "exchange_" + key + "_wait")
        for n, land in zip(names, lands, strict=True):
            own = lax.dynamic_index_in_dim(as_sent(n, gr[n]), me, 0, keepdims=False)
            total = _sum_parts(land, "sum_" + n, own=own)
            grads[n] = (total.T if n in TRANSPOSED + lowrank else total).reshape(wgt[n].shape)
            trio = _adamw(flat2d(grads[n]), flat2d(wgt[n]), flat2d(mom[n]), flat2d(var[n]), "adamw_" + n)
            upd[n] = tuple(t.reshape(wgt[n].shape) for t in trio)
        return upd[names[-1]][0]

    got = {}

    def finish_whole(key, names, after):
        land, = _exchange_wait(started[key], after, False, "gather_" + key + "_wait")
        total = _sum_parts(lax.dynamic_update_index_in_dim(land, started[key + "_mine"], me, 0), "sum_" + key + "_grads")
        got.update(zip(names, _unpack(total, [gr[n].shape for n in names]), strict=True))
        return total

    done = finish_large("early", EARLY_GRADS, dx)
    done = finish_whole("mid", whole_mid, done)
    done = finish_large("rest", rest_names, done)
    done = finish_whole("late", whole_late, done)
    last, _ = lax.optimization_barrier((gr["norm_mix_pre"].reshape(SUBLANES, LANES), done))
    got["norm_mix_pre"] = _sum_parts(_all_gather(last, "gather_last_grad"), "sum_last_grad").reshape(gr["norm_mix_pre"].shape)
    packed = tuple(n for n in WEIGHTS if n not in EARLY_GRADS + rest_names)
    for n in packed:
        if n in SMALL_SHARDED:
            cols = shard_shapes[n][1]
            grads[n] = lax.dynamic_slice_in_dim(got[n], me * cols, cols, axis=1).reshape(wgt[n].shape)
        else:
            grads[n] = got[n].reshape(wgt[n].shape)

    pack_local = lambda src: _pack([src[n].reshape(-1) for n in packed])
    outs = _adamw(pack_local(grads), pack_local(wgt), pack_local(mom), pack_local(var), "adamw_small")
    shapes = [wgt[n].shape for n in packed]
    upd.update(zip(packed, zip(*[_unpack(o, shapes) for o in outs], strict=True), strict=True))
    loss = lax.psum(loss_part, AXES)
    return (loss, dx[None], *[grads[n] for n in WEIGHTS], *[upd[n][0] for n in WEIGHTS], *[upd[n][1] for n in WEIGHTS],
            *[upd[n][2] for n in WEIGHTS])
```

```python
import math

import jax
import jax.numpy as jnp
from jax import lax
from jax.experimental import pallas as pl
from jax.experimental.pallas import tpu as pltpu

F32 = jnp.float32
BF16 = jnp.bfloat16
SDS = jax.ShapeDtypeStruct
HI = lax.Precision.HIGHEST
MESH = pl.DeviceIdType.MESH
AXES = ("x", "y", "c")
N_DEV = 8

D_MODEL = 1024
RW = 512
HN = 64
N_RWKV = 1792
S5W = 512
S5G = 32
S5C = 16
S5P = 64
S5N = S5G * S5P
D_FF = 2816
NORM_EPS = 1e-6
LNX_EPS = 64e-5

ADAM_LR = 0.001
ADAM_B1 = 0.9
ADAM_B2 = 0.999
ADAM_EPS = 1e-08
ADAM_WD = 0.01
ADAM_STEP = 10

LANES = 128
SUBLANES = 8
VMEM_LIMIT = 56 * 1024 * 1024
WKV_CHUNK = 32


def _params(sem=("arbitrary",)):
    return pltpu.CompilerParams(dimension_semantics=sem, vmem_limit_bytes=VMEM_LIMIT)


def _pick(n, cap):
    best = None
    for t in range(LANES, min(n, cap) + 1, LANES):
        if n % t == 0:
            best = t
    return best or n


MM_VMEM_BUDGET = 40 * 1024 * 1024


def _mm_tiles(m, n, k, a_bytes, b_bytes, o_bytes):
    def divisors(d):
        return [t for t in range(LANES, d + 1, LANES) if d % t == 0] or [d]

    tn = _pick(n, 2304)
    best = None
    for tm in divisors(m):
        for tk in divisors(k):
            vmem = 2 * (tm * tk * a_bytes + tk * tn * b_bytes) + tm * tn * 4 + 2 * tm * tn * o_bytes
            if vmem > MM_VMEM_BUDGET:
                continue
            key = ((m // tm) * (n // tn) * (k // tk), -tm)
            if best is None or key < best[0]:
                best = (key, tm, tk)
    assert best is not None, (m, n, k)
    return best[1], tn, best[2]


def _mm(a, b, *, ta=False, tb=False, out_dtype=F32, name):
    m = a.shape[1] if ta else a.shape[0]
    k = a.shape[0] if ta else a.shape[1]
    n = b.shape[0] if tb else b.shape[1]
    assert (b.shape[1] if tb else b.shape[0]) == k
    tm, tn, tk = _mm_tiles(m, n, k, a.dtype.itemsize, b.dtype.itemsize, jnp.dtype(out_dtype).itemsize)
    nk = k // tk
    dims = (((0 if ta else 1,), (1 if tb else 0,)), ((), ()))

    def body(a_ref, b_ref, o_ref, acc_ref):
        kk = pl.program_id(2)

        @pl.when(kk == 0)
        def _():
            acc_ref[...] = jnp.zeros_like(acc_ref)

        acc_ref[...] += lax.dot_general(a_ref[...].astype(BF16), b_ref[...].astype(BF16), dims,
                                        preferred_element_type=F32)

        @pl.when(kk == nk - 1)
        def _():
            o_ref[...] = acc_ref[...].astype(o_ref.dtype)

    a_spec = pl.BlockSpec((tk, tm), lambda i, j, kk: (kk, i)) if ta else pl.BlockSpec((tm, tk), lambda i, j, kk: (i, kk))
    b_spec = pl.BlockSpec((tn, tk), lambda i, j, kk: (j, kk)) if tb else pl.BlockSpec((tk, tn), lambda i, j, kk: (kk, j))
    return pl.pallas_call(
        body, grid=(m // tm, n // tn, nk), in_specs=[a_spec, b_spec],
        out_specs=pl.BlockSpec((tm, tn), lambda i, j, kk: (i, j)),
        out_shape=SDS((m, n), out_dtype), scratch_shapes=[pltpu.VMEM((tm, tn), F32)],
        compiler_params=_params(("parallel", "parallel", "arbitrary")), name=name)(a, b)


def _rows(fn, rows, params, out_rows, out_accs, *, name, tl, reverse=False, scratch=()):
    first = rows[0][0] if isinstance(rows[0], tuple) else rows[0]
    length = first.shape[0]
    tl = min(tl, length)
    nt = length // tl
    rmap = (lambda i: nt - 1 - i) if reverse else (lambda i: i)
    specs, arrs = [], []
    for r in rows:
        arr, wdt, cb = r if isinstance(r, tuple) else (r, r.shape[1], 0)
        specs.append(pl.BlockSpec((tl, wdt), lambda i, cb=cb: (rmap(i), cb)))
        arrs.append(arr)
    for p in params:
        specs.append(pl.BlockSpec(p.shape, lambda i, nd=p.ndim: (0,) * nd))
        arrs.append(p)
    out_shape = [SDS((length, c), dt) for c, dt in out_rows] + [SDS(s, F32) for s in out_accs]
    out_specs = [pl.BlockSpec((tl, c), lambda i: (rmap(i), 0)) for c, _ in out_rows]
    out_specs += [pl.BlockSpec(s, lambda i, nd=len(s): (0,) * nd) for s in out_accs]
    nr, npar, nor, noa = len(rows), len(params), len(out_rows), len(out_accs)

    def body(*refs):
        rin, pin = refs[:nr], refs[nr:nr + npar]
        rout = refs[nr + npar:nr + npar + nor]
        aout = refs[nr + npar + nor:nr + npar + nor + noa]
        scr = refs[nr + npar + nor + noa:]
        step = pl.program_id(0)
        outs_r, outs_a = fn(step, [r[...] for r in rin], [p[...] for p in pin], scr)
        for ref, val in zip(rout, outs_r, strict=True):
            ref[...] = val.astype(ref.dtype)

        @pl.when(step == 0)
        def _():
            for ref in aout:
                ref[...] = jnp.zeros_like(ref)

        for ref, val in zip(aout, outs_a, strict=True):
            ref[...] += val.astype(F32)

    res = pl.pallas_call(body, grid=(nt,), in_specs=specs, out_specs=out_specs, out_shape=out_shape,
                         scratch_shapes=list(scratch), compiler_params=_params(), name=name)(*arrs)
    return list(res)


def _rms(x, g):
    return x * lax.rsqrt(jnp.mean(x * x, axis=-1, keepdims=True) + NORM_EPS) * g


def _sig(x):
    return 0.5 * (jnp.tanh(0.5 * x) + 1.0)


def _softplus(x):
    return jnp.maximum(x, 0.0) + jnp.log(1.0 + jnp.exp(-jnp.abs(x)))


def _gelu(x):
    return x * (0.5 * (1.0 + jnp.tanh(math.sqrt(2.0 / math.pi) * (x + 0.044715 * (x * x * x)))))


def _bdot(a, b):
    return jnp.dot(a.astype(BF16), b.astype(BF16), preferred_element_type=F32)


def _hdot(a, b):
    return jnp.dot(a, b, precision=HI, preferred_element_type=F32)


def _segsum_impl(x, ones):
    hi = x.astype(BF16)
    lo = (x - hi.astype(F32)).astype(BF16)
    ones = ones.astype(BF16)
    return jnp.dot(jnp.concatenate([hi, lo], axis=1), jnp.concatenate([ones, ones], axis=0), preferred_element_type=F32)


@jax.custom_vjp
def _segsum(x, ones):
    return _segsum_impl(x, ones)


_segsum.defvjp(lambda x, ones: (_segsum_impl(x, ones), ones),
               lambda ones, g: (_segsum_impl(g, ones), jnp.zeros_like(ones)))


def _block_ones(n, blk):
    i = lax.broadcasted_iota(jnp.int32, (n, n), 0) // blk
    j = lax.broadcasted_iota(jnp.int32, (n, n), 1) // blk
    return (i == j).astype(F32)


def _rms_fwd(x, g, name):
    return _rows(lambda s, r, p, _: ([_rms(r[0], p[0])], []), [x], [g], [(x.shape[1], BF16)], [], name=name, tl=512)[0]


def _rms_bwd(x, g, dh, dres, name):
    def fn(s, r, p, _):
        _, vjp = jax.vjp(_rms, r[0], p[0])
        dx, dg = vjp(r[1])
        return [dx + r[2]], [dg]
    return _rows(fn, [x, dh, dres], [g], [(x.shape[1], F32)], [g.shape], name=name, tl=256)


def _rwkv_pre_math(k_, lr, w0, a0, k_k, k_a, w2p, a2p, g2p, o512):
    pre_w = w0 + _bdot(jnp.tanh(lr), w2p)
    w = -_softplus(-pre_w) - 0.5
    decay = jnp.exp(-jnp.exp(w))
    a = _sig(a0 + _bdot(lr, a2p))
    g = _bdot(_sig(lr), g2p)
    kr = k_ * k_k
    kk = kr / jnp.maximum(jnp.sqrt(_segsum(kr * kr, o512)), 1e-12)
    k2 = k_ * (1.0 + (a - 1.0) * k_a)
    return decay, k2, -kk, kk * a, g


def _shift_down(p, prev_row):
    row = lax.broadcasted_iota(jnp.int32, p.shape, 0)
    return jnp.where(row == 0, jnp.broadcast_to(prev_row, p.shape), pltpu.roll(p, 1, 0))


def _shift_up(q, next_row):
    n = q.shape[0]
    row = lax.broadcasted_iota(jnp.int32, q.shape, 0)
    return jnp.where(row == n - 1, jnp.broadcast_to(next_row, q.shape), pltpu.roll(q, n - 1, 0))


def _rwkv_pre_fwd(p, mu, small, o512):
    def fn(step, r, prm, scr):
        car = scr[0]

        @pl.when(step == 0)
        def _():
            car[...] = jnp.zeros_like(car)

        x = r[0]
        prev = _shift_down(x, car[SUBLANES - 1:SUBLANES, :])
        car[...] = x[x.shape[0] - SUBLANES:, :]
        xs = x + (prev - x) * prm[0]
        decay, k2, aa, bb, g = _rwkv_pre_math(xs[:, RW:2 * RW], xs[:, 3 * RW:], *prm[1:])
        return [xs[:, :RW], decay, k2, xs[:, 2 * RW:3 * RW], aa, bb, g], []
    return _rows(fn, [p], [mu, *small, o512], [(RW, F32)] * 7, [], name="rwkv_pre_fwd", tl=256,
                 scratch=[pltpu.VMEM((SUBLANES, N_RWKV), F32)])


def _rwkv_pre_bwd(p, cots, mu, small, o512):
    length = p.shape[0]
    tl = min(256, length)
    nt = length // tl
    rows_per = tl // SUBLANES
    params = [mu, *small, o512]
    acc_shapes = [mu.shape] + [q.shape for q in small]
    nr, npar, nacc = 2 + len(cots), len(params), len(acc_shapes)

    def body(*refs):
        rin, pin = refs[:nr], refs[nr:nr + npar]
        dp_ref = refs[nr + npar]
        aout = refs[nr + npar + 1:nr + npar + 1 + nacc]
        car_q = refs[nr + npar + 1 + nacc]
        step = pl.program_id(0)

        @pl.when(step == 0)
        def _():
            car_q[...] = jnp.zeros_like(car_q)
            for ref in aout:
                ref[...] = jnp.zeros_like(ref)

        x = rin[0][...]
        prev_row = jnp.where(step == nt - 1, 0.0, rin[1][SUBLANES - 1:SUBLANES, :])
        dr, ddecay, dk2, dv, daa, dbb, dg, dr_b, dk2_b, dv_b = [r[...] for r in rin[2:]]
        dr, dk2, dv = dr + dr_b, dk2 + dk2_b, dv + dv_b
        prm = [q[...] for q in pin]
        mu_, o512_ = prm[0], prm[-1]
        prev = _shift_down(x, prev_row)
        xs = x + (prev - x) * mu_
        _, vjp = jax.vjp(lambda k_, lr, *w: _rwkv_pre_math(k_, lr, *w, o512_), xs[:, RW:2 * RW], xs[:, 3 * RW:], *prm[1:-1])
        dk_, dlr, *dsmall = vjp((ddecay, dk2, daa, dbb, dg))
        dxs = jnp.concatenate([dr, dk_, dv, dlr], axis=1)
        q = dxs * mu_
        dp_ref[...] = (dxs - q + _shift_up(q, car_q[0:1, :])).astype(dp_ref.dtype)
        car_q[...] = q[:SUBLANES, :]
        aout[0][...] += jnp.sum((prev - x) * dxs, axis=0, keepdims=True)
        for ref, val in zip(aout[1:], dsmall, strict=True):
            ref[...] += val

    rmap = lambda i: nt - 1 - i
    specs = [pl.BlockSpec((tl, N_RWKV), lambda i: (rmap(i), 0)),
             pl.BlockSpec((SUBLANES, N_RWKV), lambda i: (jnp.maximum(rmap(i) * rows_per - 1, 0), 0))]
    specs += [pl.BlockSpec((tl, RW), lambda i: (rmap(i), 0)) for _ in cots]
    specs += [pl.BlockSpec(q.shape, lambda i, nd=q.ndim: (0,) * nd) for q in params]
    out_shape = [SDS((length, N_RWKV), BF16)] + [SDS(sh, F32) for sh in acc_shapes]
    out_specs = [pl.BlockSpec((tl, N_RWKV), lambda i: (rmap(i), 0))]
    out_specs += [pl.BlockSpec(sh, lambda i, nd=len(sh): (0,) * nd) for sh in acc_shapes]
    res = pl.pallas_call(body, grid=(nt,), in_specs=specs, out_specs=out_specs, out_shape=out_shape,
                         scratch_shapes=[pltpu.VMEM((SUBLANES, N_RWKV), F32)],
                         compiler_params=_params(), name="rwkv_pre_bwd")(p, p, *cots, *params)
    return list(res)


def _rwkv_post_math(y, r, k2, v, g, lnx_w, lnx_b, r_k, o512):
    mean = _segsum(y, o512) * (1.0 / HN)
    yc = y - mean
    var = _segsum(yc * yc, o512) * (1.0 / HN)
    yn = yc * lax.rsqrt(var + LNX_EPS) * lnx_w + lnx_b
    bonus = _segsum(r * k2 * r_k, o512) * v
    return (yn + bonus) * g


def _rwkv_post_fwd(y, r, k2, v, g, prm, o512):
    return _rows(lambda s, rr, p, _: ([_rwkv_post_math(*rr, *p)], []), [y, r, k2, v, g], [*prm, o512],
                 [(RW, BF16)], [], name="rwkv_post_fwd", tl=256)[0]


def _rwkv_post_bwd(y, r, k2, v, g, dout, prm, o512):
    def fn(s, rr, p, _):
        o = p[-1]
        _, vjp = jax.vjp(lambda *a: _rwkv_post_math(*a, o), *rr[:5], *p[:-1])
        gr = vjp(rr[5])
        return list(gr[:5]), list(gr[5:])
    return _rows(fn, [y, r, k2, v, g, dout], [*prm, o512], [(RW, F32)] * 5, [q.shape for q in prm],
                 name="rwkv_post_bwd", tl=256)


def _glu_math(ys, t, b):
    return ys * _sig(t + b)


def _glu_fwd(ys, t, b):
    return _rows(lambda s, r, p, _: ([_glu_math(r[0], r[1], p[0])], []), [ys, t], [b], [(S5W, BF16)], [],
                 name="s5_glu_fwd", tl=512)[0]


def _glu_bwd(ys, t, dout, b):
    def fn(s, r, p, _):
        _, vjp = jax.vjp(_glu_math, r[0], r[1], p[0])
        dys, dt, db = vjp(r[2])
        return [dys, dt], [db]
    return _rows(fn, [ys, t, dout], [b], [(S5W, F32), (S5W, BF16)], [b.shape], name="s5_glu_bwd", tl=512)


def _merge_math(gp_r, gp_s, o_r, o_s, b_r, b_s):
    return _sig(gp_r + b_r) * o_r + _sig(gp_s + b_s) * o_s


def _merge_fwd(gpre, o_r, o_s, b_r, b_s):
    return _rows(lambda s, r, p, _: ([_merge_math(*r, *p)], []),
                 [(gpre, D_MODEL, 0), (gpre, D_MODEL, 1), o_r, o_s], [b_r, b_s], [(D_MODEL, BF16)], [],
                 name="merge_fwd", tl=256)[0]


def _merge_bwd(gpre, o_r, o_s, dmi, b_r, b_s):
    def fn(s, r, p, _):
        _, vjp = jax.vjp(_merge_math, *r[:4], *p)
        dgr, dgs, dor, dos, dbr, dbs = vjp(r[4])
        return [dgr, dgs, dor, dos], [dbr, dbs]
    return _rows(fn, [(gpre, D_MODEL, 0), (gpre, D_MODEL, 1), o_r, o_s, dmi], [b_r, b_s], [(D_MODEL, BF16)] * 4,
                 [b_r.shape, b_s.shape], name="merge_bwd", tl=256)


def _mid_fwd(x, mixed, g_post, g_pre):
    def fn(s, r, p, _):
        x2 = r[0] + _rms(r[1], p[0])
        return [x2, _rms(x2, p[1])], []
    return _rows(fn, [x, mixed], [g_post, g_pre], [(D_MODEL, F32), (D_MODEL, BF16)], [], name="mid_fwd", tl=256)


def _mid_bwd(x2, mixed, dh2, dx3, g_post, g_pre):
    def fn(s, r, p, _):
        _, vjp1 = jax.vjp(_rms, r[0], p[1])
        dx2, dg_pre = vjp1(r[2])
        dx2 = dx2 + r[3]
        _, vjp2 = jax.vjp(_rms, r[1], p[0])
        dmixed, dg_post = vjp2(dx2)
        return [dx2, dmixed], [dg_post, dg_pre]
    return _rows(fn, [x2, mixed, dh2, dx3], [g_post, g_pre], [(D_MODEL, F32), (D_MODEL, BF16)], [g_post.shape, g_pre.shape],
                 name="mid_bwd", tl=256)


def _final(x2, f, tgt, g_post):
    def fn(s, r, p, _):
        y, vjp = jax.vjp(_rms, r[1], p[0])
        diff = r[0] + y - r[2]
        dx3 = diff * (1.0 / D_MODEL)
        df, dg = vjp(dx3)
        part = 0.5 * jnp.sum(jnp.sum(diff * diff, axis=1, keepdims=True), axis=0, keepdims=True) * (1.0 / D_MODEL)
        return [dx3, df], [dg, jnp.broadcast_to(part, (1, LANES))]
    return _rows(fn, [x2, f, tgt], [g_post], [(D_MODEL, F32), (D_MODEL, BF16)], [g_post.shape, (1, LANES)], name="final", tl=256)


def _conv_taps(z, car):
    row = lax.broadcasted_iota(jnp.int32, z.shape, 0)
    z1 = jnp.where(row == 0, jnp.broadcast_to(car[7:8, :], z.shape), pltpu.roll(z, 1, 0))
    z2 = pltpu.roll(z, 2, 0)
    z2 = jnp.where(row == 0, jnp.broadcast_to(car[6:7, :], z.shape), z2)
    z2 = jnp.where(row == 1, jnp.broadcast_to(car[7:8, :], z.shape), z2)
    return z1, z2


def _conv(z, car, w, b):
    z1, z2 = _conv_taps(z, car)
    return b + w[0:1, :] * z2 + w[1:2, :] * z1 + w[2:3, :] * z, z1, z2


def _conv_fwd(z, conv_w, conv_b):
    length = z.shape[0]
    tl = min(256, length)
    nt = length // tl
    tc = _pick(D_FF, 1536)
    nb = D_FF // tc

    def body(zg_ref, zv_ref, wg_ref, wv_ref, bg_ref, bv_ref, o_ref, cg, cv):
        @pl.when(pl.program_id(1) == 0)
        def _():
            cg[...] = jnp.zeros_like(cg)
            cv[...] = jnp.zeros_like(cv)

        zg, zv = zg_ref[...], zv_ref[...]
        gate, _, _ = _conv(zg, cg[...], wg_ref[...], bg_ref[...])
        val, _, _ = _conv(zv, cv[...], wv_ref[...], bv_ref[...])
        cg[...] = zg[tl - SUBLANES:, :]
        cv[...] = zv[tl - SUBLANES:, :]
        o_ref[...] = (_gelu(gate) * val).astype(o_ref.dtype)

    zspec = lambda off: pl.BlockSpec((tl, tc), lambda j, i: (i, j + off))
    wspec = lambda off, r: pl.BlockSpec((r, tc), lambda j, i: (0, j + off))
    return pl.pallas_call(
        body, grid=(nb, nt),
        in_specs=[zspec(0), zspec(nb), wspec(0, 3), wspec(nb, 3), wspec(0, 1), wspec(nb, 1)],
        out_specs=pl.BlockSpec((tl, tc), lambda j, i: (i, j)), out_shape=SDS((length, D_FF), BF16),
        scratch_shapes=[pltpu.VMEM((SUBLANES, tc), F32)] * 2,
        compiler_params=_params(("arbitrary", "arbitrary")), name="conv_fwd")(z, z, conv_w, conv_w, conv_b, conv_b)


def _conv_bwd(z, dact, conv_w, conv_b):
    length = z.shape[0]
    tl = min(256, length)
    nt = length // tl
    tc = _pick(D_FF, 1536)
    nb = D_FF // tc
    rows_per = tl // SUBLANES

    def half_bwd(dzc, z, z1, z2, w, dcar):
        n = tl
        row = lax.broadcasted_iota(jnp.int32, dzc.shape, 0)
        u1 = jnp.where(row == n - 1, jnp.broadcast_to(dcar[0:1, :], dzc.shape), pltpu.roll(dzc, n - 1, 0))
        u2 = pltpu.roll(dzc, n - 2, 0)
        u2 = jnp.where(row == n - 2, jnp.broadcast_to(dcar[0:1, :], dzc.shape), u2)
        u2 = jnp.where(row == n - 1, jnp.broadcast_to(dcar[1:2, :], dzc.shape), u2)
        dz = w[2:3, :] * dzc + w[1:2, :] * u1 + w[0:1, :] * u2
        dw = jnp.concatenate([jnp.sum(dzc * z2, axis=0, keepdims=True), jnp.sum(dzc * z1, axis=0, keepdims=True),
                              jnp.sum(dzc * z, axis=0, keepdims=True)], axis=0)
        return dz, dw, jnp.sum(dzc, axis=0, keepdims=True)

    def body(zg_ref, zv_ref, pg_ref, pv_ref, da_ref, wg_ref, wv_ref, bg_ref, bv_ref,
             dzg_ref, dzv_ref, dwg_ref, dwv_ref, dbg_ref, dbv_ref, cg, cv):
        step = pl.program_id(1)

        @pl.when(step == 0)
        def _():
            cg[...] = jnp.zeros_like(cg)
            cv[...] = jnp.zeros_like(cv)
            for ref in (dwg_ref, dwv_ref, dbg_ref, dbv_ref):
                ref[...] = jnp.zeros_like(ref)

        is_first_tile = step == nt - 1
        zg, zv = zg_ref[...], zv_ref[...]
        pg = jnp.where(is_first_tile, 0.0, pg_ref[...])
        pv = jnp.where(is_first_tile, 0.0, pv_ref[...])
        wg, wv = wg_ref[...], wv_ref[...]
        gate, zg1, zg2 = _conv(zg, pg, wg, bg_ref[...])
        val, zv1, zv2 = _conv(zv, pv, wv, bv_ref[...])
        act_g, vjp = jax.vjp(_gelu, gate)
        da = da_ref[...]
        dgate = vjp(da * val)[0]
        dval = da * act_g
        dzg, dwg, dbg = half_bwd(dgate, zg, zg1, zg2, wg, cg[...])
        dzv, dwv, dbv = half_bwd(dval, zv, zv1, zv2, wv, cv[...])
        cg[...] = dgate[:SUBLANES, :]
        cv[...] = dval[:SUBLANES, :]
        dzg_ref[...] = dzg.astype(dzg_ref.dtype)
        dzv_ref[...] = dzv.astype(dzv_ref.dtype)
        dwg_ref[...] += dwg
        dwv_ref[...] += dwv
        dbg_ref[...] += dbg
        dbv_ref[...] += dbv

    rmap = lambda i: nt - 1 - i
    zspec = lambda off: pl.BlockSpec((tl, tc), lambda j, i: (rmap(i), j + off))
    pspec = lambda off: pl.BlockSpec((SUBLANES, tc), lambda j, i: (jnp.maximum(rmap(i) * rows_per - 1, 0), j + off))
    wspec = lambda off, r: pl.BlockSpec((r, tc), lambda j, i: (0, j + off))
    out_w = lambda r: pl.BlockSpec((r, tc), lambda j, i: (0, j))
    dzg, dzv, dwg, dwv, dbg, dbv = pl.pallas_call(
        body, grid=(nb, nt),
        in_specs=[zspec(0), zspec(nb), pspec(0), pspec(nb), pl.BlockSpec((tl, tc), lambda j, i: (rmap(i), j)),
                  wspec(0, 3), wspec(nb, 3), wspec(0, 1), wspec(nb, 1)],
        out_specs=[pl.BlockSpec((tl, tc), lambda j, i: (rmap(i), j))] * 2 + [out_w(3), out_w(3), out_w(1), out_w(1)],
        out_shape=[SDS((length, D_FF), BF16)] * 2 + [SDS((3, D_FF), F32)] * 2 + [SDS((1, D_FF), F32)] * 2,
        scratch_shapes=[pltpu.VMEM((SUBLANES, tc), F32)] * 2,
        compiler_params=_params(("arbitrary", "arbitrary")), name="conv_bwd")(z, z, z, z, dact, conv_w, conv_w, conv_b, conv_b)
    return dzg, dzv, jnp.concatenate([dwg, dwv], axis=1), jnp.concatenate([dbg, dbv], axis=1)


def _s5_prep_math(a_re, a_im, ls, b_re, b_im):
    dt = jnp.exp(ls)
    er = jnp.exp(a_re * dt)
    ph = a_im * dt
    abr, abi = er * jnp.cos(ph), er * jnp.sin(ph)
    den = a_re * a_re + a_im * a_im
    nr = abr - 1.0
    cr = (nr * a_re + abi * a_im) / den
    ci = (abi * a_re - nr * a_im) / den
    return abr, abi, cr * b_re - ci * b_im, cr * b_im + ci * b_re


def _s5_prep_fwd(a_re, a_im, ls, b_re, b_im):
    def body(ar, ai, l, br, bi, o1, o2, o3, o4):
        for ref, val in zip((o1, o2, o3, o4), _s5_prep_math(ar[...], ai[...], l[...], br[...], bi[...]), strict=True):
            ref[...] = val
    return pl.pallas_call(body, out_shape=[SDS((1, S5N), F32)] * 2 + [SDS((S5C, S5N), F32)] * 2,
                          name="s5_prep_fwd")(a_re, a_im, ls, b_re, b_im)


def _s5_prep_bwd(a_re, a_im, ls, b_re, b_im, cots, gsel):
    def body(ar, ai, l, br, bi, c1, c2, c3, c4, g_ref, o1, o2, o3, o4, o5):
        _, vjp = jax.vjp(_s5_prep_math, ar[...], ai[...], l[...], br[...], bi[...])
        d_ar, d_ai, d_ls, d_br, d_bi = vjp((c1[...], c2[...], c3[...], c4[...]))
        o1[...] = d_ar
        o2[...] = d_ai
        o3[...] = _hdot(jnp.broadcast_to(d_ls, (SUBLANES, S5N)), g_ref[...])
        o4[...] = d_br
        o5[...] = d_bi
    return pl.pallas_call(body, out_shape=[SDS((1, S5N), F32)] * 2 + [SDS((SUBLANES, LANES), F32)] + [SDS((S5C, S5N), F32)] * 2,
                          name="s5_prep_bwd")(a_re, a_im, ls, b_re, b_im, *cots, gsel)


def _cmul(ar, ai, br, bi):
    return ar * br - ai * bi, ar * bi + ai * br


def _s5_powers(abr, abi):
    shp = (SUBLANES, S5N)
    a1 = (jnp.broadcast_to(abr, shp), jnp.broadcast_to(abi, shp))
    a2 = _cmul(*a1, *a1)
    a4 = _cmul(*a2, *a2)
    row = lax.broadcasted_iota(jnp.int32, shp, 0)
    pr, pi = a1
    cur = a1
    for i in range(1, SUBLANES):
        cur = _cmul(*cur, *a1)
        pr = jnp.where(row == i, cur[0], pr)
        pi = jnp.where(row == i, cur[1], pi)
    return a1, a2, a4, (pr, pi)


def _s5_scan(sre_ref, sim_ref, car_re, car_im, abr, abi, n_rows, reverse):
    a1, a2, a4, (pr, pi) = _s5_powers(abr, abi)
    sgn = -1.0 if reverse else 1.0
    row = lax.broadcasted_iota(jnp.int32, (SUBLANES, S5N), 0)
    if reverse:
        qr, qi = pr, pi
        for i in range(SUBLANES):
            src = SUBLANES - 1 - i
            qr = jnp.where(row == i, jnp.broadcast_to(pr[src:src + 1, :], pr.shape), qr)
            qi = jnp.where(row == i, jnp.broadcast_to(pi[src:src + 1, :], pi.shape), qi)
        pr, pi = qr, qi
    nblk = n_rows // SUBLANES

    def blk(i, carry):
        cr, ci = carry
        b = (nblk - 1 - i) if reverse else i
        sl = pl.ds(pl.multiple_of(b * SUBLANES, SUBLANES), SUBLANES)
        xr, xi = sre_ref[sl, :], sim_ref[sl, :]
        for kk, (er, ei) in ((1, a1), (2, a2), (4, a4)):
            if reverse:
                sr = jnp.where(row < SUBLANES - kk, pltpu.roll(xr, SUBLANES - kk, 0), 0.0)
                si = jnp.where(row < SUBLANES - kk, pltpu.roll(xi, SUBLANES - kk, 0), 0.0)
            else:
                sr = jnp.where(row >= kk, pltpu.roll(xr, kk, 0), 0.0)
                si = jnp.where(row >= kk, pltpu.roll(xi, kk, 0), 0.0)
            dr, di = _cmul(er, sgn * ei, sr, si)
            xr, xi = xr + dr, xi + di
        dr, di = _cmul(pr, sgn * pi, cr, ci)
        xr, xi = xr + dr, xi + di
        sre_ref[sl, :] = xr
        sim_ref[sl, :] = xi
        edge = 0 if reverse else SUBLANES - 1
        return (jnp.broadcast_to(xr[edge:edge + 1, :], xr.shape), jnp.broadcast_to(xi[edge:edge + 1, :], xi.shape))

    cr, ci = lax.fori_loop(0, nblk, blk, (car_re[...], car_im[...]))
    car_re[...] = cr
    car_im[...] = ci


S5_BLK = 4
S5_BN = S5N // S5_BLK


def _split2(x):
    hi = x.astype(BF16)
    return hi, (x - hi.astype(F32)).astype(BF16)


def _dot3(a, b, ca, cb):
    ah, al = _split2(a)
    bh, bl = _split2(b)
    return lax.dot_general(jnp.concatenate([ah, ah, al], axis=ca), jnp.concatenate([bh, bl, bh], axis=cb),
                           (((ca,), (cb,)), ((), ())), preferred_element_type=F32)


def _dot1(a, b, ca, cb):
    return lax.dot_general(a.astype(BF16), b.astype(BF16), (((ca,), (cb,)), ((), ())), preferred_element_type=F32)


def _s5_project_in(u, bblk_ref, sre_ref, sim_ref):
    for j in range(S5_BLK):
        bu = _dot1(u[:, j * LANES:(j + 1) * LANES], bblk_ref[j], 1, 0)
        sre_ref[:, j * S5_BN:(j + 1) * S5_BN] = bu[:, :S5_BN]
        sim_ref[:, j * S5_BN:(j + 1) * S5_BN] = bu[:, S5_BN:]


def _s5_project_out(u, d, cblk_ref, sre_ref, sim_ref):
    ys = []
    for j in range(S5_BLK):
        sl = slice(j * S5_BN, (j + 1) * S5_BN)
        ys.append(_dot1(sre_ref[:, sl], cblk_ref[j, :S5_BN, :], 1, 0) + _dot1(sim_ref[:, sl], cblk_ref[j, S5_BN:, :], 1, 0))
    return jnp.concatenate(ys, axis=1) + d * u


def _s5_fwd(u, bblk, cblk, abr, abi, d):
    length = u.shape[0]
    tl = min(256, length)
    nt = length // tl

    def body(u_ref, b_ref, c_ref, ar_ref, ai_ref, d_ref, ys_ref, kr_ref, ki_ref, sre, sim, car_re, car_im):
        @pl.when(pl.program_id(0) == 0)
        def _():
            car_re[...] = jnp.zeros_like(car_re)
            car_im[...] = jnp.zeros_like(car_im)

        kr_ref[0] = car_re[...]
        ki_ref[0] = car_im[...]
        u_ = u_ref[...]
        _s5_project_in(u_, b_ref, sre, sim)
        _s5_scan(sre, sim, car_re, car_im, ar_ref[...], ai_ref[...], tl, False)
        ys_ref[...] = _gelu(_s5_project_out(u_, d_ref[...], c_ref, sre, sim))

    full = lambda a: pl.BlockSpec(a.shape, lambda i, nd=a.ndim: (0,) * nd)
    chk = pl.BlockSpec((1, SUBLANES, S5N), lambda i: (i, 0, 0))
    return pl.pallas_call(
        body, grid=(nt,), in_specs=[pl.BlockSpec((tl, S5W), lambda i: (i, 0)), full(bblk), full(cblk), full(abr), full(abi), full(d)],
        out_specs=[pl.BlockSpec((tl, S5W), lambda i: (i, 0)), chk, chk],
        out_shape=[SDS((length, S5W), F32), SDS((nt, SUBLANES, S5N), F32), SDS((nt, SUBLANES, S5N), F32)],
        scratch_shapes=[pltpu.VMEM((tl, S5N), F32)] * 2 + [pltpu.VMEM((SUBLANES, S5N), F32)] * 2,
        compiler_params=_params(), name="s5_fwd")(u, bblk, cblk, abr, abi, d)


def _s5_bwd(u, dys_a, dys_b, kre, kim, bblk, cblk, abr, abi, d):
    length = u.shape[0]
    tl = min(256, length)
    nt = length // tl

    def body(u_ref, da_ref, db_ref, kr_ref, ki_ref, b_ref, c_ref, ar_ref, ai_ref, d_ref,
             du_ref, dB_ref, dC_ref, dar_ref, dai_ref, dd_ref, sre, sim, gre, gim, car_re, car_im, dcar_re, dcar_im):
        @pl.when(pl.program_id(0) == 0)
        def _():
            dcar_re[...] = jnp.zeros_like(dcar_re)
            dcar_im[...] = jnp.zeros_like(dcar_im)
            for ref in (dB_ref, dC_ref, dar_ref, dai_ref, dd_ref):
                ref[...] = jnp.zeros_like(ref)

        u_ = u_ref[...]
        abr_, abi_, d_ = ar_ref[...], ai_ref[...], d_ref[...]
        car_re[...] = kr_ref[0]
        car_im[...] = ki_ref[0]
        _s5_project_in(u_, b_ref, sre, sim)
        _s5_scan(sre, sim, car_re, car_im, abr_, abi_, tl, False)
        y = _s5_project_out(u_, d_, c_ref, sre, sim)
        _, vjp = jax.vjp(_gelu, y)
        dy = vjp(da_ref[...] + db_ref[...])[0]
        dd_ref[...] += jnp.sum(dy * u_, axis=0, keepdims=True)
        for j in range(S5_BLK):
            sl = slice(j * S5_BN, (j + 1) * S5_BN)
            dyj = dy[:, j * LANES:(j + 1) * LANES]
            gre[:, sl] = _dot1(dyj, c_ref[j, :S5_BN, :], 1, 1)
            gim[:, sl] = _dot1(dyj, c_ref[j, S5_BN:, :], 1, 1)
            dC_ref[j, :S5_BN, :] += _dot1(sre[:, sl], dyj, 0, 0)
            dC_ref[j, S5_BN:, :] += _dot1(sim[:, sl], dyj, 0, 0)
        _s5_scan(gre, gim, dcar_re, dcar_im, abr_, abi_, tl, True)
        gr, gi = gre[...], gim[...]
        pr = _shift_down(sre[...], kr_ref[0, 0:1, :])
        pi = _shift_down(sim[...], ki_ref[0, 0:1, :])
        dar_ref[...] += jnp.sum(gr * pr + gi * pi, axis=0, keepdims=True)
        dai_ref[...] += jnp.sum(gi * pr - gr * pi, axis=0, keepdims=True)
        dus = []
        for j in range(S5_BLK):
            sl = slice(j * S5_BN, (j + 1) * S5_BN)
            uj = u_[:, j * LANES:(j + 1) * LANES]
            dus.append(_dot1(gre[:, sl], b_ref[j, :, :S5_BN], 1, 1) + _dot1(gim[:, sl], b_ref[j, :, S5_BN:], 1, 1))
            dB_ref[j, :, :S5_BN] += _dot1(uj, gre[:, sl], 0, 0)
            dB_ref[j, :, S5_BN:] += _dot1(uj, gim[:, sl], 0, 0)
        du_ref[...] = (jnp.concatenate(dus, axis=1) + d_ * dy).astype(du_ref.dtype)

    rmap = lambda i: nt - 1 - i
    full = lambda a: pl.BlockSpec(a.shape, lambda i, nd=a.ndim: (0,) * nd)
    row = pl.BlockSpec((tl, S5W), lambda i: (rmap(i), 0))
    chk = pl.BlockSpec((1, SUBLANES, S5N), lambda i: (rmap(i), 0, 0))
    return pl.pallas_call(
        body, grid=(nt,), in_specs=[row, row, row, chk, chk, full(bblk), full(cblk), full(abr), full(abi), full(d)],
        out_specs=[row, full(bblk), full(cblk), full(abr), full(abi), full(d)],
        out_shape=[SDS((length, S5W), BF16), SDS(bblk.shape, F32), SDS(cblk.shape, F32), SDS(abr.shape, F32),
                   SDS(abi.shape, F32), SDS(d.shape, F32)],
        scratch_shapes=[pltpu.VMEM((tl, S5N), F32)] * 4 + [pltpu.VMEM((SUBLANES, S5N), F32)] * 4,
        compiler_params=_params(), name="s5_bwd")(u, dys_a, dys_b, kre, kim, bblk, cblk, abr, abi, d)


def _s5_blockdiag_in(bb):
    t = bb.reshape(S5C, S5_BLK, S5_BN).transpose(1, 0, 2)
    t = jnp.tile(t, (1, SUBLANES, 1))
    mask = (lax.broadcasted_iota(jnp.int32, (LANES, S5_BN), 0) // S5C) == (lax.broadcasted_iota(jnp.int32, (LANES, S5_BN), 1) // S5P)
    return jnp.where(mask[None], t, 0.0)


def _s5_blockdiag_in_t(dblk):
    t = dblk.reshape(S5_BLK, SUBLANES, S5C, SUBLANES, S5P)
    t = jnp.diagonal(t, axis1=1, axis2=3)
    return t.transpose(1, 0, 3, 2).reshape(S5C, S5N)


def _s5_blockdiag_out(c):
    t = c.reshape(S5_BLK, SUBLANES, S5C, S5P).transpose(0, 3, 1, 2).reshape(S5_BLK, S5P, LANES)
    t = jnp.tile(t, (1, SUBLANES, 1))
    mask = (lax.broadcasted_iota(jnp.int32, (S5_BN, LANES), 0) // S5P) == (lax.broadcasted_iota(jnp.int32, (S5_BN, LANES), 1) // S5C)
    return jnp.where(mask[None], t, 0.0)


def _s5_blockdiag_out_t(dblk):
    t = dblk.reshape(S5_BLK, SUBLANES, S5P, SUBLANES, S5C)
    t = jnp.diagonal(t, axis1=1, axis2=3)
    return t.transpose(0, 3, 2, 1).reshape(S5G, S5C, S5P)


def _wkv_consts():
    lane = lax.broadcasted_iota(jnp.int32, (4 * HN, LANES), 1)
    row = lax.broadcasted_iota(jnp.int32, (4 * HN, LANES), 0)
    diag = ((lane % HN) == (row % HN)).astype(F32)
    ones = _block_ones(LANES, HN).astype(BF16)
    return lane % HN, diag, jnp.concatenate([ones, ones], axis=0)


def _wkv_split(p):
    hi = p.astype(BF16)
    lo = (p - hi.astype(F32)).astype(BF16)
    return jnp.concatenate([hi, lo], axis=1)


def _wkv_rowseg(p, ones):
    return jnp.dot(_wkv_split(p), ones, preferred_element_type=F32)


def _wkv_full(rows_ref, t):
    q = jnp.broadcast_to(rows_ref[t], (SUBLANES, RW))
    return jnp.concatenate([jnp.tile(q[:, hp * LANES:(hp + 1) * LANES], (SUBLANES, 1)) for hp in range(4)], axis=0)


def _wkv_colsum8(x):
    return jnp.concatenate([x[hp * HN:(hp + 1) * HN].reshape(SUBLANES, SUBLANES, LANES).sum(axis=0) for hp in range(4)], axis=1)


def _wkv_prebroadcast(src_refs, dst_refs, n):
    for s, dref in zip(src_refs, dst_refs, strict=True):
        dref[...] = s[...].reshape(n, 1, RW)


def _wkv_pair_sums(lhs_ref, out_ref):
    n, ns = out_ref.shape[0], out_ref.shape[1]
    ones_pair = _block_ones(2 * LANES, HN).astype(BF16)
    out_ref[...] = jnp.dot(lhs_ref[0:n].reshape(n * ns, 2 * LANES), ones_pair,
                           preferred_element_type=F32).reshape(n, ns, 2 * LANES)


def _wkv_transpose_out(acc, c, diag):
    lane = lax.broadcasted_iota(jnp.int32, (c, LANES), 1)
    pick = diag[:HN].astype(BF16)
    pick3 = jnp.concatenate([pick, pick, pick], axis=0)
    outs = []
    for hp in range(4):
        blk = acc[hp * HN:(hp + 1) * HN]
        hi = blk.astype(BF16)
        rest = blk - hi.astype(F32)
        mid = rest.astype(BF16)
        lo = (rest - mid.astype(F32)).astype(BF16)
        zp = lax.dot_general(jnp.concatenate([hi, mid, lo], axis=0), pick3, (((0,), (0,)), ((), ())), preferred_element_type=F32)
        outs.append(jnp.where(lane < HN, zp[:c], zp[HN:HN + c]))
    return jnp.concatenate(outs, axis=1)


def _wkv_fwd(r, w, k, v, a, b):
    length = r.shape[0]
    c = min(WKV_CHUNK, length)
    nc = length // c
    ns = 4 * HN

    def body(r_ref, w_ref, k_ref, v_ref, a_ref, b_ref, y_ref, sst_ref, sast_ref, vst_ref, fin_ref,
             s_ref, acc_ref, lhs_ref, ysum_ref, rb, wb, kb, vb, ab, bb, wab, beb, kab):
        @pl.when(pl.program_id(0) == 0)
        def _():
            s_ref[...] = jnp.zeros_like(s_ref)

        a_next = pltpu.roll(a_ref[...], c - 1, 0)
        o512 = _block_ones(RW, HN)
        _wkv_prebroadcast((r_ref, w_ref, k_ref, v_ref, a_ref, b_ref, w_ref[...] * a_next,
                           _segsum_impl(b_ref[...] * a_next, o512), _segsum_impl(k_ref[...] * a_next, o512)),
                          (rb, wb, kb, vb, ab, bb, wab, beb, kab), c)
        lane_t, diag, ones = _wkv_consts()

        def fill_v(q, carry):
            for half in range(2):
                lhs_ref[q, :, half * LANES:(half + 1) * LANES] = (_wkv_full(vb, 2 * q + half) * diag).astype(BF16)
            return carry

        lax.fori_loop(0, c // 2, fill_v, 0)
        _wkv_pair_sums(lhs_ref, vst_ref)

        def pair(p, carry):
            t0, t1 = 2 * p, 2 * p + 1
            s0 = s_ref[...]
            both = _wkv_rowseg(jnp.concatenate([s0 * _wkv_full(ab, t0), s0 * _wkv_full(wab, t0)], axis=0), ones)
            v_pair = vst_ref[p]
            sa0, v0 = both[:ns], v_pair[:, :LANES]
            s1 = s0 * _wkv_full(wb, t0) + sa0 * _wkv_full(bb, t0) + v0 * _wkv_full(kb, t0)
            sa1 = both[ns:] + sa0 * _wkv_full(beb, t0) + v0 * _wkv_full(kab, t0)
            s2 = s1 * _wkv_full(wb, t1) + sa1 * _wkv_full(bb, t1) + v_pair[:, LANES:] * _wkv_full(kb, t1)
            sst_ref[t0] = s0
            sast_ref[t0] = sa0
            sst_ref[t1] = s1
            sast_ref[t1] = sa1
            s_ref[...] = s2
            lhs_ref[p, :, 0:LANES] = (s1 * _wkv_full(rb, t0)).astype(BF16)
            lhs_ref[p, :, LANES:2 * LANES] = (s2 * _wkv_full(rb, t1)).astype(BF16)
            return carry

        lax.fori_loop(0, c // 2, pair, 0)
        _wkv_pair_sums(lhs_ref, ysum_ref)
        acc_ref[...] = jnp.zeros_like(acc_ref)

        def gather_y(p, carry):
            both = ysum_ref[p]
            acc = jnp.where(lane_t == 2 * p, both[:, :LANES], acc_ref[...])
            acc_ref[...] = jnp.where(lane_t == 2 * p + 1, both[:, LANES:], acc)
            return carry

        lax.fori_loop(0, c // 2, gather_y, 0)
        y_ref[...] = _wkv_transpose_out(acc_ref[...], c, diag)
        fin_ref[...] = s_ref[...]

    row = pl.BlockSpec((c, RW), lambda i: (i, 0))
    st = pl.BlockSpec((c, ns, LANES), lambda i: (i, 0, 0))
    state = pltpu.VMEM((ns, LANES), F32)
    st_pair = pl.BlockSpec((c // 2, ns, 2 * LANES), lambda i: (i, 0, 0))
    return pl.pallas_call(
        body, grid=(nc,), in_specs=[row] * 6,
        out_specs=[row, st, st, st_pair, pl.BlockSpec((ns, LANES), lambda i: (0, 0))],
        out_shape=[SDS((length, RW), F32)] + [SDS((length, ns, LANES), F32)] * 2 + [SDS((length // 2, ns, 2 * LANES), F32)]
        + [SDS((ns, LANES), F32)],
        scratch_shapes=[state] * 2 + [pltpu.VMEM((c, ns, 2 * LANES), BF16), pltpu.VMEM((c // 2, ns, 2 * LANES), F32)]
        + [pltpu.VMEM((c, 1, RW), F32)] * 9,
        compiler_params=_params(), name="wkv_fwd")(r, w, k, v, a, b)


def _wkv_bwd(r, w, k, a, b, dy, sst, sast, vst, fin):
    length = r.shape[0]
    c = min(WKV_CHUNK, length)
    nc = length // c
    ns = 4 * HN

    def body(r_ref, w_ref, k_ref, a_ref, b_ref, dy_ref, sst_ref, sast_ref, vst_ref, snext_ref, fin_ref,
             dr_ref, dw_ref, dk_ref, da_ref, db_ref, dv_ref,
             ds_ref, cur_ref, acc_ref, dyst, lhs_ref, dvsum_ref, rb, wb, kb, ab, bb, dyb, wbb, alb, rhb,
             p_r, p_w, p_k, p_a, p_b):
        @pl.when(pl.program_id(0) == 0)
        def _():
            ds_ref[...] = jnp.zeros_like(ds_ref)

        b_prev = pltpu.roll(b_ref[...], 1, 0)
        o512 = _block_ones(RW, HN)
        _wkv_prebroadcast((r_ref, w_ref, k_ref, a_ref, b_ref, dy_ref, w_ref[...] * b_prev,
                           _segsum_impl(a_ref[...] * b_prev, o512), _segsum_impl(r_ref[...] * b_ref[...], o512)),
                          (rb, wb, kb, ab, bb, dyb, wbb, alb, rhb), c)
        lane_t, diag, ones = _wkv_consts()

        def fill(q, carry):
            for half in range(2):
                lhs_ref[q, :, half * LANES:(half + 1) * LANES] = (_wkv_full(dyb, 2 * q + half) * diag).astype(BF16)
            return carry

        lax.fori_loop(0, c // 2, fill, 0)
        _wkv_pair_sums(lhs_ref, dyst)
        cur_ref[...] = jnp.where(pl.program_id(0) == 0, fin_ref[...], snext_ref[0])

        def pair(p, carry):
            ta = c - 1 - 2 * p
            tb = ta - 1
            s_a, s_b, s_c = cur_ref[...], sst_ref[ta], sst_ref[tb]
            dy_pair = dyst[c // 2 - 1 - p]
            dy_a, dy_b = dy_pair[:, LANES:], dy_pair[:, :LANES]
            ds_a = ds_ref[...] + dy_a * _wkv_full(rb, ta)
            both = _wkv_rowseg(jnp.concatenate([ds_a * _wkv_full(bb, ta), ds_a * _wkv_full(wbb, ta)], axis=0), ones)
            dsa_a = both[:ns]
            ds_b = ds_a * _wkv_full(wb, ta) + dsa_a * _wkv_full(ab, ta) + dy_b * _wkv_full(rb, tb)
            dsa_b = both[ns:] + dsa_a * _wkv_full(alb, ta) + dy_b * _wkv_full(rhb, tb)
            ds_ref[...] = ds_b * _wkv_full(wb, tb) + dsa_b * _wkv_full(ab, tb)
            cur_ref[...] = s_c
            lhs_ref[p, :, 0:LANES] = (ds_a * _wkv_full(kb, ta)).astype(BF16)
            lhs_ref[p, :, LANES:2 * LANES] = (ds_b * _wkv_full(kb, tb)).astype(BF16)
            v_pair = vst_ref[c // 2 - 1 - p]
            for t, s_t, s_prev, ds, dsa, dyy, vv in ((ta, s_a, s_b, ds_a, dsa_a, dy_a, v_pair[:, LANES:]),
                                                     (tb, s_b, s_c, ds_b, dsa_b, dy_b, v_pair[:, :LANES])):
                p_r[t] = _wkv_colsum8(s_t * dyy)
                p_k[t] = _wkv_colsum8(ds * vv)
                p_b[t] = _wkv_colsum8(ds * sast_ref[t])
                p_w[t] = _wkv_colsum8(ds * s_prev)
                p_a[t] = _wkv_colsum8(s_prev * dsa)
            return carry

        lax.fori_loop(0, c // 2, pair, 0)
        _wkv_pair_sums(lhs_ref, dvsum_ref)
        acc_ref[...] = jnp.zeros_like(acc_ref)

        def gather_dv(p, carry):
            both = dvsum_ref[p]
            acc = jnp.where(lane_t == c - 1 - 2 * p, both[:, :LANES], acc_ref[...])
            acc_ref[...] = jnp.where(lane_t == c - 2 - 2 * p, both[:, LANES:], acc)
            return carry

        lax.fori_loop(0, c // 2, gather_dv, 0)
        sel = (lax.broadcasted_iota(jnp.int32, (c, c * SUBLANES), 1) // SUBLANES
               == lax.broadcasted_iota(jnp.int32, (c, c * SUBLANES), 0)).astype(F32)
        for out_ref, part in ((dr_ref, p_r), (dw_ref, p_w), (dk_ref, p_k), (da_ref, p_a), (db_ref, p_b)):
            out_ref[...] = _dot3(sel, part[...].reshape(c * SUBLANES, RW), 1, 0)
        dv_ref[...] = _wkv_transpose_out(acc_ref[...], c, diag)

    rmap = lambda i: nc - 1 - i
    row = pl.BlockSpec((c, RW), lambda i: (rmap(i), 0))
    st = pl.BlockSpec((c, ns, LANES), lambda i: (rmap(i), 0, 0))
    nxt = pl.BlockSpec((1, ns, LANES), lambda i: (jnp.minimum((rmap(i) + 1) * c, length - 1), 0, 0))
    state = pltpu.VMEM((ns, LANES), F32)
    st_pair = pl.BlockSpec((c // 2, ns, 2 * LANES), lambda i: (rmap(i), 0, 0))
    return pl.pallas_call(
        body, grid=(nc,), in_specs=[row] * 6 + [st, st, st_pair, nxt, pl.BlockSpec((ns, LANES), lambda i: (0, 0))],
        out_specs=[row] * 6, out_shape=[SDS((length, RW), F32)] * 6,
        scratch_shapes=[state] * 3 + [pltpu.VMEM((c // 2, ns, 2 * LANES), F32), pltpu.VMEM((c, ns, 2 * LANES), BF16),
                                       pltpu.VMEM((c // 2, ns, 2 * LANES), F32)]
        + [pltpu.VMEM((c, 1, RW), F32)] * 9 + [pltpu.VMEM((c, SUBLANES, RW), F32)] * 5,
        compiler_params=_params(), name="wkv_bwd")(r, w, k, a, b, dy, sst, sast, vst, sst, fin)


def _my_place():
    return lax.axis_index("x"), lax.axis_index("y"), lax.axis_index("c")


def _peer(x, y, c, k):
    return (x ^ ((k >> 2) & 1), y ^ ((k >> 1) & 1), c ^ (k & 1))


def _all_gather(shard, name):
    def body(in_ref, out_ref, send_sems, recv_sems, local_sem):
        x, y, c = _my_place()
        me = 4 * x + 2 * y + c
        mine = pltpu.make_async_copy(in_ref, out_ref.at[me], local_sem)
        mine.start()
        sends = []
        for k in range(1, N_DEV):
            cp = pltpu.make_async_remote_copy(src_ref=in_ref, dst_ref=out_ref.at[me], send_sem=send_sems.at[k - 1],
                                              recv_sem=recv_sems.at[k - 1], device_id=_peer(x, y, c, k), device_id_type=MESH)
            cp.start()
            sends.append(cp)
        for k in range(1, N_DEV):
            px, py, pc = _peer(x, y, c, k)
            pltpu.make_async_remote_copy(src_ref=in_ref, dst_ref=out_ref.at[4 * px + 2 * py + pc], send_sem=send_sems.at[k - 1],
                                         recv_sem=recv_sems.at[k - 1], device_id=(px, py, pc), device_id_type=MESH).wait_recv()
        for cp in sends:
            cp.wait_send()
        mine.wait()

    return pl.pallas_call(
        body, out_shape=SDS((N_DEV, *shard.shape), shard.dtype),
        in_specs=[pl.BlockSpec(memory_space=pl.ANY)], out_specs=pl.BlockSpec(memory_space=pl.ANY),
        scratch_shapes=[pltpu.SemaphoreType.DMA((N_DEV - 1,)), pltpu.SemaphoreType.DMA((N_DEV - 1,)), pltpu.SemaphoreType.DMA],
        name=name)(shard)


def _all_gather_two_level(shard, name):
    def body(in_ref, out_ref, send_sems, recv_sems, local_sem):
        x, y, c = _my_place()
        me, sibling = (x, y, c), (x, y, 1 - c)
        chips = [(1 - x, y), (x, 1 - y), (1 - x, 1 - y)]

        def block(px, py, pc):
            return out_ref.at[4 * px + 2 * py + pc]

        def copy(k, blk, to, src=None):
            return pltpu.make_async_remote_copy(src_ref=block(*blk) if src is None else src, dst_ref=block(*blk),
                                                send_sem=send_sems.at[k], recv_sem=recv_sems.at[k], device_id=to,
                                                device_id_type=MESH)

        mine = pltpu.make_async_copy(in_ref, block(*me), local_sem)
        mine.start()
        first = [copy(0, me, sibling, src=in_ref)] + [copy(1 + j, me, (*chip, c), src=in_ref) for j, chip in enumerate(chips)]
        for cp in first:
            cp.start()
        passed = [copy(4 + j, (*chip, c), sibling) for j, chip in enumerate(chips)]
        for j, chip in enumerate(chips):
            copy(1 + j, (*chip, c), me).wait_recv()
            passed[j].start()
        copy(0, sibling, me).wait_recv()
        for j, chip in enumerate(chips):
            copy(4 + j, (*chip, 1 - c), me).wait_recv()
        for cp in first + passed:
            cp.wait_send()
        mine.wait()

    return pl.pallas_call(
        body, out_shape=SDS((N_DEV, *shard.shape), shard.dtype),
        in_specs=[pl.BlockSpec(memory_space=pl.ANY)], out_specs=pl.BlockSpec(memory_space=pl.ANY),
        scratch_shapes=[pltpu.SemaphoreType.DMA((N_DEV - 1,)), pltpu.SemaphoreType.DMA((N_DEV - 1,)), pltpu.SemaphoreType.DMA],
        name=name)(shard)


_HBM_SPEC = pl.BlockSpec(memory_space=pltpu.HBM)
_SEM_SPEC = pl.BlockSpec(memory_space=pltpu.SEMAPHORE)
_DATAFLOW = pltpu.SideEffectType.DATAFLOW_SIDE_EFFECTING


def _exchange_start(srcs, per_peer, name):
    n = len(srcs)
    lands = [jnp.zeros((N_DEV, *s.shape[-2:]), s.dtype) for s in srcs]

    def body(*refs):
        src_refs, land_refs = refs[:n], refs[n:2 * n]
        send_sems, recv_sems, token = refs[2 * n], refs[2 * n + 1], refs[-1]
        x, y, c = _my_place()
        me = 4 * x + 2 * y + c
        for a in range(n):
            for k in range(1, N_DEV):
                px, py, pc = _peer(x, y, c, k)
                mine = src_refs[a].at[4 * px + 2 * py + pc] if per_peer else src_refs[a]
                pltpu.make_async_remote_copy(src_ref=mine, dst_ref=land_refs[a].at[me], send_sem=send_sems.at[7 * a + k - 1],
                                             recv_sem=recv_sems.at[7 * a + k - 1], device_id=(px, py, pc),
                                             device_id_type=MESH).start()
        token[...] = jnp.zeros_like(token)

    n_sem = (N_DEV - 1) * n
    res = pl.pallas_call(
        body, name=name,
        out_shape=(pltpu.SemaphoreType.DMA((n_sem,)), pltpu.SemaphoreType.DMA((n_sem,)),
                   *[pltpu.HBM(a.shape, a.dtype) for a in srcs + lands], SDS((SUBLANES, LANES), F32)),
        in_specs=(_HBM_SPEC,) * (2 * n),
        out_specs=(_SEM_SPEC, _SEM_SPEC) + (_HBM_SPEC,) * (2 * n) + (pl.BlockSpec(memory_space=pltpu.VMEM),),
        input_output_aliases={i: 2 + i for i in range(2 * n)}, compiler_params=pltpu.CompilerParams(has_side_effects=_DATAFLOW),
    )(*[pltpu.with_memory_space_constraint(a, pltpu.HBM) for a in srcs + lands])
    return res[0], res[1], list(res[2:2 + n]), list(res[2 + n:2 + 2 * n]), res[-1]


def _exchange_wait(started, after, per_peer, name):
    send_sems, recv_sems, src_thrus, land_thrus, _ = started
    n = len(src_thrus)

    def body(*refs):
        src_refs, land_refs = refs[:n], refs[n:2 * n]
        send_sems, recv_sems = refs[2 * n], refs[2 * n + 1]
        x, y, c = _my_place()
        me = 4 * x + 2 * y + c
        for a in range(n):
            for k in range(1, N_DEV):
                px, py, pc = _peer(x, y, c, k)
                mine = src_refs[a].at[me] if per_peer else src_refs[a]
                copy = pltpu.make_async_remote_copy(src_ref=mine, dst_ref=land_refs[a].at[4 * px + 2 * py + pc],
                                                    send_sem=send_sems.at[7 * a + k - 1], recv_sem=recv_sems.at[7 * a + k - 1],
                                                    device_id=(px, py, pc), device_id_type=MESH)
                copy.wait_send()
                copy.wait_recv()

    res = pl.pallas_call(
        body, name=name, out_shape=tuple(pltpu.HBM(a.shape, a.dtype) for a in src_thrus + land_thrus),
        in_specs=(_HBM_SPEC,) * (2 * n) + (_SEM_SPEC, _SEM_SPEC, pl.BlockSpec(memory_space=pl.ANY)),
        out_specs=(_HBM_SPEC,) * (2 * n), input_output_aliases={i: i for i in range(2 * n)},
        compiler_params=pltpu.CompilerParams(has_side_effects=_DATAFLOW),
    )(*src_thrus, *land_thrus, send_sems, recv_sems, after)
    return list(res[n:])


def _row_tile(rows, cap):
    best = None
    for t in range(2 * SUBLANES, min(rows, cap) + 1, 2 * SUBLANES):
        if rows % t == 0:
            best = t
    return best or rows


def _sum_parts(parts, name, own=None):
    rows, cols = parts.shape[1:]

    def body(*refs):
        p_ref, g_out = refs[0], refs[-1]
        g = p_ref[0].astype(F32)
        for s in range(1, N_DEV):
            g = g + p_ref[s].astype(F32)
        if own is not None:
            g = g + refs[1][...].astype(F32)
        g_out[...] = g

    tr = _row_tile(rows, 384 * 1024 // cols)
    blk = pl.BlockSpec((tr, cols), lambda i: (i, 0))
    extra = [] if own is None else [own]
    return pl.pallas_call(
        body, grid=(rows // tr,),
        in_specs=[pl.BlockSpec((N_DEV, tr, cols), lambda i: (0, i, 0))] + [blk] * len(extra),
        out_specs=blk, out_shape=SDS((rows, cols), F32),
        compiler_params=_params(("parallel",)), name=name)(parts, *extra)


def _sum_adamw(parts, own, wgt, m, v, name):
    rows, cols = wgt.shape

    def body(p_ref, o_ref, w_ref, m_ref, v_ref, g_out, d_out, m_out, v_out):
        g_ = p_ref[0].astype(F32)
        for s in range(1, N_DEV):
            g_ = g_ + p_ref[s].astype(F32)
        g_ = g_ + o_ref[...].astype(F32)
        m_new = ADAM_B1 * m_ref[...] + (1.0 - ADAM_B1) * g_
        v_new = ADAM_B2 * v_ref[...] + (1.0 - ADAM_B2) * (g_ * g_)
        m_hat = m_new / (1.0 - ADAM_B1 ** ADAM_STEP)
        v_hat = v_new / (1.0 - ADAM_B2 ** ADAM_STEP)
        g_out[...] = g_
        d_out[...] = -ADAM_LR * (m_hat / (jnp.sqrt(v_hat) + ADAM_EPS) + ADAM_WD * w_ref[...])
        m_out[...] = m_new
        v_out[...] = v_new

    tr = _row_tile(rows, 384 * 1024 // cols)
    blk = pl.BlockSpec((tr, cols), lambda i: (i, 0))
    return pl.pallas_call(
        body, grid=(rows // tr,), in_specs=[pl.BlockSpec((N_DEV, tr, cols), lambda i: (0, i, 0))] + [blk] * 4,
        out_specs=[blk] * 4, out_shape=[SDS((rows, cols), F32)] * 4,
        compiler_params=_params(("parallel",)), name=name)(parts, own, wgt, m, v)


def _adamw(g, wgt, m, v, name):
    rows, cols = wgt.shape

    def body(g_ref, w_ref, m_ref, v_ref, d_out, m_out, v_out):
        g_ = g_ref[...]
        m_new = ADAM_B1 * m_ref[...] + (1.0 - ADAM_B1) * g_
        v_new = ADAM_B2 * v_ref[...] + (1.0 - ADAM_B2) * (g_ * g_)
        m_hat = m_new / (1.0 - ADAM_B1 ** ADAM_STEP)
        v_hat = v_new / (1.0 - ADAM_B2 ** ADAM_STEP)
        d_out[...] = -ADAM_LR * (m_hat / (jnp.sqrt(v_hat) + ADAM_EPS) + ADAM_WD * w_ref[...])
        m_out[...] = m_new
        v_out[...] = v_new

    tr = _row_tile(rows, 512 * 1024 // cols)
    blk = pl.BlockSpec((tr, cols), lambda i: (i, 0))
    return pl.pallas_call(
        body, grid=(rows // tr,), in_specs=[blk] * 4, out_specs=[blk] * 3, out_shape=[SDS((rows, cols), F32)] * 3,
        compiler_params=_params(("parallel",)), name=name)(g, wgt, m, v)


PACK_ALIGN = 2 * SUBLANES * LANES
PACK_ROWS = 512

ROW_SHARDED = ("s5_w_glu", "w_out", "ffn_w_down")
TRANSPOSED = ("w_in", "ffn_w_up", "w_branch_rwkv", "w_branch_s5")
SMALL_SHARDED = ("rwkv_w2", "rwkv_a2", "rwkv_g2", "ffn_conv_w")
SHARDED = ROW_SHARDED + TRANSPOSED + SMALL_SHARDED
F32_GATHER = ("ffn_conv_w",)
REPLICATED =("norm_mix_pre", "norm_mix_post", "norm_ffn_pre", "norm_ffn_post", "b_gate", "rwkv_shift_mu", "rwkv_w0",
              "rwkv_a0", "rwkv_k_k", "rwkv_k_a", "rwkv_r_k", "rwkv_lnx_w", "rwkv_lnx_b", "s5_a_re", "s5_a_im", "s5_b_re",
              "s5_b_im", "s5_c_re", "s5_c_im", "s5_d", "s5_log_step", "s5_b_glu", "ffn_conv_b")
WEIGHTS = ("norm_mix_pre", "norm_mix_post", "norm_ffn_pre", "norm_ffn_post", "w_in", "b_gate", "rwkv_shift_mu", "rwkv_w0",
           "rwkv_w2", "rwkv_a0", "rwkv_a2", "rwkv_g2", "rwkv_k_k", "rwkv_k_a", "rwkv_r_k", "rwkv_lnx_w", "rwkv_lnx_b",
           "s5_a_re", "s5_a_im", "s5_b_re", "s5_b_im", "s5_c_re", "s5_c_im", "s5_d", "s5_log_step", "s5_w_glu", "s5_b_glu",
           "w_branch_rwkv", "w_branch_s5", "w_out", "ffn_w_up", "ffn_conv_w", "ffn_conv_b", "ffn_w_down")


def _pad_flat(a):
    flat = a.reshape(-1)
    pad = (-flat.shape[0]) % PACK_ALIGN
    return jnp.pad(flat, (0, pad)) if pad else flat


def _pack(pieces):
    flat = jnp.concatenate([_pad_flat(p) for p in pieces])
    pad = (-flat.shape[0]) % (PACK_ROWS * LANES)
    return (jnp.pad(flat, (0, pad)) if pad else flat).reshape(-1, LANES)


def _unpack(buf, shapes):
    lead = buf.shape[:-2]
    flat = buf.reshape(*lead, -1)
    out, off = [], 0
    for s in shapes:
        n = math.prod(s)
        out.append(flat[..., off:off + n].reshape(*lead, *s))
        off += n + (-n) % PACK_ALIGN
    return out


def _rows_to_full(stack):
    return stack.reshape(-1, stack.shape[2])


def _cols_to_full(stack):
    return stack.transpose(1, 0, 2).reshape(stack.shape[1], -1)


EARLY_GRADS = ("ffn_w_down", "ffn_w_up", "w_out")
REST_GRADS = ("s5_w_glu", "w_in", "w_branch_rwkv", "w_branch_s5")


def _local_step(x, tgt, wt, get_late, send_early, send_mid, send_rest):
    wt = dict(wt)
    o512 = _block_ones(RW, HN)
    w_in = wt["w_in"]
    w_rw, w_u, w_g = w_in[:N_RWKV], w_in[N_RWKV:N_RWKV + S5W], w_in[N_RWKV + S5W:]
    b_gate = wt["b_gate"]
    b_r, b_s = b_gate[:, :D_MODEL], b_gate[:, D_MODEL:]
    zpad = lambda a, lo, n: jnp.pad(a.astype(F32), ((lo, n - lo - a.shape[0]), (0, 0)))
    w2p, a2p, g2p = zpad(wt["rwkv_w2"], 0, 256), zpad(wt["rwkv_a2"], 64, 256), zpad(wt["rwkv_g2"], 128, 256)
    pre_small = [wt["rwkv_w0"], wt["rwkv_a0"], wt["rwkv_k_k"], wt["rwkv_k_a"], w2p, a2p, g2p]
    post_prm = [wt["rwkv_lnx_w"], wt["rwkv_lnx_b"], wt["rwkv_r_k"].reshape(1, RW)]
    mu = wt["rwkv_shift_mu"]

    a_re, a_im = wt["s5_a_re"].reshape(1, S5N), wt["s5_a_im"].reshape(1, S5N)
    ls = jnp.repeat(wt["s5_log_step"].reshape(S5G), S5P).reshape(1, S5N)
    b_re_t = wt["s5_b_re"].reshape(S5N, S5C).T
    b_im_t = wt["s5_b_im"].reshape(S5N, S5C).T
    c_re, c_im = wt["s5_c_re"].reshape(S5G, S5C, S5P), wt["s5_c_im"].reshape(S5G, S5C, S5P)
    abr, abi, bbr, bbi = _s5_prep_fwd(a_re, a_im, ls, b_re_t, b_im_t)
    bblk = jnp.concatenate([_s5_blockdiag_in(bbr), _s5_blockdiag_in(bbi)], axis=2)
    cblk = jnp.concatenate([_s5_blockdiag_out(c_re), -_s5_blockdiag_out(c_im)], axis=1)
    s5_d = wt["s5_d"]

    h1 = _rms_fwd(x, wt["norm_mix_pre"], "rms1_fwd")
    p_rw = _mm(h1, w_rw, tb=True, name="mm_proj_rwkv")
    u = _mm(h1, w_u, tb=True, name="mm_proj_s5")
    gpre = _mm(h1, w_g, tb=True, name="mm_proj_gate")
    r, decay, k2, v, aa, bb, g = _rwkv_pre_fwd(p_rw, mu, pre_small, o512)
    y, sst, sast, vst, s_fin = _wkv_fwd(r, decay, k2, v, aa, bb)
    wt.update(get_late(y))
    y_r = _rwkv_post_fwd(y, r, k2, v, g, post_prm, o512)
    o_r = _mm(y_r, wt["w_branch_rwkv"], tb=True, name="mm_branch_rwkv")
    ys, kre, kim = _s5_fwd(u, bblk, cblk, abr, abi, s5_d)
    t_glu = _mm(ys, wt["s5_w_glu"], name="mm_glu")
    out5 = _glu_fwd(ys, t_glu, wt["s5_b_glu"])
    o_s = _mm(out5, wt["w_branch_s5"], tb=True, name="mm_branch_s5")
    mi = _merge_fwd(gpre, o_r, o_s, b_r, b_s)
    mixed = _mm(mi, wt["w_out"], name="mm_out")
    x2, h2 = _mid_fwd(x, mixed, wt["norm_mix_post"], wt["norm_ffn_pre"])
    z = _mm(h2, wt["ffn_w_up"], tb=True, name="mm_up")
    act = _conv_fwd(z, wt["ffn_conv_w"], wt["ffn_conv_b"])
    f = _mm(act, wt["ffn_w_down"], name="mm_down")
    dx3, df, d_norm_ffn_post, loss_part = _final(x2, f, tgt, wt["norm_ffn_post"])

    gr = {"norm_ffn_post": d_norm_ffn_post}
    dact = _mm(df, wt["ffn_w_down"], tb=True, name="mm_down_dx")
    gr["ffn_w_down"] = _mm(act, df, ta=True, out_dtype=BF16, name="mm_down_dw")
    dzg, dzv, gr["ffn_conv_w"], gr["ffn_conv_b"] = _conv_bwd(z, dact, wt["ffn_conv_w"], wt["ffn_conv_b"])
    dz = jnp.concatenate([dzg, dzv], axis=1)
    dh2 = _mm(dz, wt["ffn_w_up"], name="mm_up_dx")
    gr["ffn_w_up"] = _mm(dz, h2, ta=True, out_dtype=BF16, name="mm_up_dw")
    dx2, dmixed, gr["norm_mix_post"], gr["norm_ffn_pre"] = _mid_bwd(x2, mixed, dh2, dx3, wt["norm_mix_post"], wt["norm_ffn_pre"])
    dmi = _mm(dmixed, wt["w_out"], tb=True, name="mm_out_dx")
    gr["w_out"] = _mm(mi, dmixed, ta=True, out_dtype=BF16, name="mm_out_dw")
    token = send_early({n: gr[n] for n in EARLY_GRADS})
    b_r = b_r + jnp.tile(token[0:1, :], (1, D_MODEL // LANES))
    dgp_r, dgp_s, do_r, do_s, db_r, db_s = _merge_bwd(gpre, o_r, o_s, dmi, b_r, b_s)
    gr["b_gate"] = jnp.concatenate([db_r, db_s], axis=1)
    dout5 = _mm(do_s, wt["w_branch_s5"], name="mm_branch_s5_dx")
    gr["w_branch_s5"] = _mm(do_s, out5, ta=True, out_dtype=BF16, name="mm_branch_s5_dw")
    dys_a, dt_glu, gr["s5_b_glu"] = _glu_bwd(ys, t_glu, dout5, wt["s5_b_glu"])
    dys_b = _mm(dt_glu, wt["s5_w_glu"], tb=True, name="mm_glu_dx")
    gr["s5_w_glu"] = _mm(ys, dt_glu, ta=True, out_dtype=BF16, name="mm_glu_dw")
    du, dbblk, dcblk, dabr, dabi, gr["s5_d"] = _s5_bwd(u, dys_a, dys_b, kre, kim, bblk, cblk, abr, abi, s5_d)
    gr["s5_c_re"] = _s5_blockdiag_out_t(dcblk[:, :S5_BN, :]).reshape(wt["s5_c_re"].shape)
    gr["s5_c_im"] = (-_s5_blockdiag_out_t(dcblk[:, S5_BN:, :])).reshape(wt["s5_c_im"].shape)
    dbbr, dbbi = _s5_blockdiag_in_t(dbblk[:, :, :S5_BN]), _s5_blockdiag_in_t(dbblk[:, :, S5_BN:])
    gsel = (lax.broadcasted_iota(jnp.int32, (S5N, LANES), 0) // S5P == lax.broadcasted_iota(jnp.int32, (S5N, LANES), 1)).astype(F32)
    d_are, d_aim, d_ls, d_bre_t, d_bim_t = _s5_prep_bwd(a_re, a_im, ls, b_re_t, b_im_t, (dabr, dabi, dbbr, dbbi), gsel)
    gr["s5_a_re"] = d_are.reshape(wt["s5_a_re"].shape)
    gr["s5_a_im"] = d_aim.reshape(wt["s5_a_im"].shape)
    gr["s5_log_step"] = d_ls[0:1, :S5G]
    gr["s5_b_re"] = d_bre_t.T.reshape(wt["s5_b_re"].shape)
    gr["s5_b_im"] = d_bim_t.T.reshape(wt["s5_b_im"].shape)
    dy_r = _mm(do_r, wt["w_branch_rwkv"], name="mm_branch_rwkv_dx")
    gr["w_branch_rwkv"] = _mm(do_r, y_r, ta=True, out_dtype=BF16, name="mm_branch_rwkv_dw")
    token = send_mid(gr)
    post_prm = [post_prm[0] + jnp.tile(token[0:1, :], (1, RW // LANES))] + post_prm[1:]
    dy, dr1, dk1, dv1, dg, gr["rwkv_lnx_w"], gr["rwkv_lnx_b"], d_rk = _rwkv_post_bwd(y, r, k2, v, g, dy_r, post_prm, o512)
    gr["rwkv_r_k"] = d_rk.reshape(wt["rwkv_r_k"].shape)
    dr2, ddecay, dk2, daa, dbb, dv2 = _wkv_bwd(r, decay, k2, aa, bb, dy, sst, sast, vst, s_fin)
    cots = [dr1, ddecay, dk1, dv1, daa, dbb, dg, dr2, dk2, dv2]
    dp_rw, gr["rwkv_shift_mu"], gr["rwkv_w0"], gr["rwkv_a0"], gr["rwkv_k_k"], gr["rwkv_k_a"], dw2p, da2p, dg2p = \
        _rwkv_pre_bwd(p_rw, cots, mu, pre_small, o512)
    gr["rwkv_w2"], gr["rwkv_a2"], gr["rwkv_g2"] = dw2p[:64], da2p[64:128], dg2p[128:]
    dproj = jnp.concatenate([dp_rw, du, dgp_r, dgp_s], axis=1)
    gr["w_in"] = _mm(dproj, h1, ta=True, out_dtype=BF16, name="mm_proj_dw")
    token = send_rest(gr)
    dh1 = _mm(dproj, w_in + token[0, 0].astype(w_in.dtype), name="mm_proj_dx")
    dx, gr["norm_mix_pre"] = _rms_bwd(x, wt["norm_mix_pre"], dh1, dx2, "rms1_bwd")
    return loss_part[0, 0], dx, gr


def kernel(x, norm_mix_pre, norm_mix_post, norm_ffn_pre, norm_ffn_post, w_in, b_gate, rwkv_shift_mu, rwkv_w0, rwkv_w2, rwkv_a0, rwkv_a2, rwkv_g2, rwkv_k_k, rwkv_k_a, rwkv_r_k, rwkv_lnx_w, rwkv_lnx_b, s5_a_re, s5_a_im, s5_b_re, s5_b_im, s5_c_re, s5_c_im, s5_d, s5_log_step, s5_w_glu, s5_b_glu, w_branch_rwkv, w_branch_s5, w_out, ffn_w_up, ffn_conv_w, ffn_conv_b, ffn_w_down, loss_target, m_norm_mix_pre, m_norm_mix_post, m_norm_ffn_pre, m_norm_ffn_post, m_w_in, m_b_gate, m_rwkv_shift_mu, m_rwkv_w0, m_rwkv_w2, m_rwkv_a0, m_rwkv_a2, m_rwkv_g2, m_rwkv_k_k, m_rwkv_k_a, m_rwkv_r_k, m_rwkv_lnx_w, m_rwkv_lnx_b, m_s5_a_re, m_s5_a_im, m_s5_b_re, m_s5_b_im, m_s5_c_re, m_s5_c_im, m_s5_d, m_s5_log_step, m_s5_w_glu, m_s5_b_glu, m_w_branch_rwkv, m_w_branch_s5, m_w_out, m_ffn_w_up, m_ffn_conv_w, m_ffn_conv_b, m_ffn_w_down, v_norm_mix_pre, v_norm_mix_post, v_norm_ffn_pre, v_norm_ffn_post, v_w_in, v_b_gate, v_rwkv_shift_mu, v_rwkv_w0, v_rwkv_w2, v_rwkv_a0, v_rwkv_a2, v_rwkv_g2, v_rwkv_k_k, v_rwkv_k_a, v_rwkv_r_k, v_rwkv_lnx_w, v_rwkv_lnx_b, v_s5_a_re, v_s5_a_im, v_s5_b_re, v_s5_b_im, v_s5_c_re, v_s5_c_im, v_s5_d, v_s5_log_step, v_s5_w_glu, v_s5_b_glu, v_w_branch_rwkv, v_w_branch_s5, v_w_out, v_ffn_w_up, v_ffn_conv_w, v_ffn_conv_b, v_ffn_w_down):
    args = dict(locals())
    wgt = {n: args[n] for n in WEIGHTS}
    mom = {n: args["m_" + n] for n in WEIGHTS}
    var = {n: args["v_" + n] for n in WEIGHTS}
    me = 4 * lax.axis_index("x") + 2 * lax.axis_index("y") + lax.axis_index("c")
    shard_shapes = {n: wgt[n].shape[1:] for n in SHARDED}
    sent_shapes = {n: (shard_shapes[n][::-1] if n in TRANSPOSED else shard_shapes[n]) for n in SHARDED}
    sent = lambda n, a: a.T if n in TRANSPOSED else a

    unshard = lambda n, blocks: _cols_to_full(blocks) if n in SMALL_SHARDED else _rows_to_full(blocks)

    first = ("w_in", "rwkv_w2", "rwkv_a2", "rwkv_g2")
    late = tuple(n for n in SHARDED if n not in first and n not in F32_GATHER)
    late_payloads = [sent(n, wgt[n][0].astype(BF16)) for n in late] + [_pack([wgt[n][0] for n in F32_GATHER])]
    late_started = _exchange_start(late_payloads, False, "gather_late_start")
    got_first = _unpack(_all_gather_two_level(_pack([sent(n, wgt[n][0].astype(BF16)) for n in first]), "all_gather_weights"),
                        [sent_shapes[n] for n in first])
    wt = {n: wgt[n] for n in REPLICATED}
    for n, blocks in zip(first, got_first, strict=True):
        wt[n] = unshard(n, blocks)

    def get_late(after):
        lands = _exchange_wait(late_started, after, False, "gather_late_wait")
        whole_ = [lax.dynamic_update_index_in_dim(land, mine, me, 0) for land, mine in zip(lands, late_payloads, strict=True)]
        out = {n: _rows_to_full(blocks) for n, blocks in zip(late, whole_[:-1], strict=True)}
        taps = _unpack(whole_[-1], [sent_shapes[n] for n in F32_GATHER])
        out.update({n: unshard(n, b_) for n, b_ in zip(F32_GATHER, taps, strict=True)})
        return out

    lowrank = ("rwkv_w2", "rwkv_a2", "rwkv_g2")
    rest_names = REST_GRADS + lowrank
    whole = tuple(n for n in ("ffn_conv_w",) + REPLICATED if n != "norm_mix_pre")
    whole_mid = tuple(n for n in whole if not n.startswith("rwkv_"))
    whole_late = tuple(n for n in whole if n.startswith("rwkv_"))
    blocks_of = lambda g_: g_.reshape(N_DEV, -1, g_.shape[-1])
    as_sent = lambda n, g_: blocks_of(g_.T if n in lowrank else g_)
    started = {}

    def send_early(grads_):
        started["early"] = _exchange_start([as_sent(n, grads_[n]) for n in EARLY_GRADS], True, "exchange_early_start")
        return started["early"][4]

    def send_whole(key, names):
        def send(grads_):
            started[key + "_mine"] = _pack([grads_[n] for n in names])
            started[key] = _exchange_start([started[key + "_mine"]], False, "gather_" + key + "_start")
            return started[key][4]
        return send

    def send_rest(grads_):
        started["rest"] = _exchange_start([as_sent(n, grads_[n]) for n in rest_names], True, "exchange_rest_start")
        return started["rest"][4] + send_whole("late", whole_late)(grads_)

    loss_part, dx, gr = _local_step(x[0], loss_target[0], wt, get_late, send_early, send_whole("mid", whole_mid), send_rest)

    grads, upd = {}, {}
    flat2d = lambda a: a.reshape(a.shape[-2:])

    def finish_large(key, names, after):
        lands = _exchange_wait(started[key], after, True, "exchange_" + key + "_wait")
        for n, land in zip(names, lands, strict=True):
            own = lax.dynamic_index_in_dim(as_sent(n, gr[n]), me, 0, keepdims=False)
            if n in ROW_SHARDED:
                total, *trio = _sum_adamw(land, own, flat2d(wgt[n]), flat2d(mom[n]), flat2d(var[n]), "sum_adamw_" + n)
                grads[n] = total.reshape(wgt[n].shape)
            else:
                total = _sum_parts(land, "sum_" + n, own=own)
                grads[n] = total.T.reshape(wgt[n].shape)
                trio = _adamw(flat2d(grads[n]), flat2d(wgt[n]), flat2d(mom[n]), flat2d(var[n]), "adamw_" + n)
            upd[n] = tuple(t.reshape(wgt[n].shape) for t in trio)
        return upd[names[-1]][0]

    got = {}

    def finish_whole(key, names, after):
        land, = _exchange_wait(started[key], after, False, "gather_" + key + "_wait")
        total = _sum_parts(lax.dynamic_update_index_in_dim(land, started[key + "_mine"], me, 0), "sum_" + key + "_grads")
        got.update(zip(names, _unpack(total, [gr[n].shape for n in names]), strict=True))
        return total

    done = finish_large("early", EARLY_GRADS, dx)
    done = finish_whole("mid", whole_mid, done)
    done = finish_large("rest", rest_names, done)
    done = finish_whole("late", whole_late, done)
    last, _ = lax.optimization_barrier((gr["norm_mix_pre"].reshape(SUBLANES, LANES), done))
    got["norm_mix_pre"] = _sum_parts(_all_gather(last, "gather_last_grad"), "sum_last_grad").reshape(gr["norm_mix_pre"].shape)
    packed = tuple(n for n in WEIGHTS if n not in EARLY_GRADS + rest_names)
    for n in packed:
        if n in SMALL_SHARDED:
            cols = shard_shapes[n][1]
            grads[n] = lax.dynamic_slice_in_dim(got[n], me * cols, cols, axis=1).reshape(wgt[n].shape)
        else:
            grads[n] = got[n].reshape(wgt[n].shape)

    pack_local = lambda src: _pack([src[n].reshape(-1) for n in packed])
    outs = _adamw(pack_local(grads), pack_local(wgt), pack_local(mom), pack_local(var), "adamw_small")
    shapes = [wgt[n].shape for n in packed]
    upd.update(zip(packed, zip(*[_unpack(o, shapes) for o in outs], strict=True), strict=True))
    loss = lax.psum(loss_part, AXES)
    return (loss, dx[None], *[grads[n] for n in WEIGHTS], *[upd[n][0] for n in WEIGHTS], *[upd[n][1] for n in WEIGHTS],
            *[upd[n][2] for n in WEIGHTS])
```
